```python
import jax, jax.numpy as jnp
from jax import lax
import numpy as np

D_MODEL = 1024
BATCH = 32
SEQ = 256
DEPTH = 1
DEC_BATCH = 4
DEC_SEQ = 4096
PAST_LEN = 512

GRID_W = 64
EPS = 1e-6
N_HEADS = 16
N_KV_HEADS = 4
HEAD_DIM = 64
ATTN_W = N_HEADS * HEAD_DIM
KV_W = N_KV_HEADS * HEAD_DIM
ROPE_THETA = 10000.0
Q_BLOCK = 128
SSD_EXPAND = 2
D_INNER = SSD_EXPAND * D_MODEL
SSD_HEAD_DIM = 64
SSD_HEADS = D_INNER // SSD_HEAD_DIM
SSD_GROUPS = 4
D_STATE = 128
D_CONV = 4
CHUNK = 128
CONV_CH = D_INNER + 2 * SSD_GROUPS * D_STATE
N_EXPERTS = 64
TOP_K = 8
N_EXPERT_GROUPS = 8
TOPK_GROUPS = 4
D_EXPERT = 256
D_SHARED = 256
ROUTED_SCALE = 2.5
SPLIT_SIZES = (ATTN_W, KV_W, KV_W, D_INNER, CONV_CH, 2 * SSD_HEADS, 2 * D_MODEL)
IN_COLS = sum(SPLIT_SIZES)

kernel_name = 'hybrid_ssd_gqa_moe_diffusion_step'


def rmsnorm(x, g):
    xf = x.astype(jnp.float32)
    y = xf * lax.rsqrt(jnp.mean(xf * xf, axis=-1, keepdims=True) + EPS)
    return (y * g.astype(jnp.float32)).astype(x.dtype)


def axial_rope(n_tokens):
    n_rows = n_tokens // GRID_W
    rows = jnp.repeat(jnp.arange(n_rows), GRID_W).astype(jnp.float32)
    cols = jnp.tile(jnp.arange(GRID_W), n_rows).astype(jnp.float32)
    n_freq = HEAD_DIM // 4
    freqs = ROPE_THETA ** (-jnp.arange(n_freq, dtype=jnp.float32) / n_freq)
    ang = jnp.concatenate([rows[:, None] * freqs, cols[:, None] * freqs], axis=-1)
    return jnp.cos(ang), jnp.sin(ang)


def apply_rope(x, cos, sin):
    b, t, h, d = x.shape
    xf = x.astype(jnp.float32).reshape(b, t, h, d // 2, 2)
    x1, x2 = xf[..., 0], xf[..., 1]
    cs, sn = cos[None, :, None, :], sin[None, :, None, :]
    out = jnp.stack([x1 * cs - x2 * sn, x1 * sn + x2 * cs], axis=-1)
    return out.reshape(b, t, h, d).astype(x.dtype)


def block_attention(q, k, v):
    b, tq = q.shape[:2]
    nb = tq // Q_BLOCK
    rep = N_HEADS // N_KV_HEADS
    qb = q.reshape(b, nb, Q_BLOCK, N_KV_HEADS, rep, HEAD_DIM).transpose(1, 0, 2, 3, 4, 5)
    scale = HEAD_DIM ** -0.5

    def one_block(qblk):
        s = jnp.einsum('bqgrd,bkgd->bgrqk', qblk, k, preferred_element_type=jnp.float32) * scale
        p = jax.nn.softmax(s, axis=-1).astype(v.dtype)
        return jnp.einsum('bgrqk,bkgd->bqgrd', p, v)

    o = lax.map(one_block, qb)
    return o.transpose(1, 0, 2, 3, 4, 5).reshape(b, tq, ATTN_W)


def centred_dwconv(u, w, bias):
    t = u.shape[1]
    left = D_CONV // 2
    right = D_CONV - 1 - left
    up = jnp.pad(u, ((0, 0), (left, right), (0, 0)))
    y = sum(up[:, i:i + t] * w[i] for i in range(D_CONV)) + bias
    return jax.nn.silu(y)


def ssd_chunked(x, dt, a_neg, bm, cm, h0):
    f32 = jnp.float32
    b, t = x.shape[:2]
    nc = t // CHUNK
    r = SSD_HEADS // SSD_GROUPS
    xc = (x.astype(f32) * dt[..., None]).reshape(b, nc, CHUNK, SSD_GROUPS, r, SSD_HEAD_DIM)
    la = (dt * a_neg.astype(f32)).reshape(b, nc, CHUNK, SSD_GROUPS, r)
    acum = jnp.cumsum(la, axis=2)
    bc = bm.astype(f32).reshape(b, nc, CHUNK, SSD_GROUPS, D_STATE)
    cc = cm.astype(f32).reshape(b, nc, CHUNK, SSD_GROUPS, D_STATE)
    lower = jnp.tril(jnp.ones((CHUNK, CHUNK), dtype=bool))
    diff = acum[:, :, :, None] - acum[:, :, None, :]
    decay = jnp.where(lower[None, None, :, :, None, None], jnp.exp(jnp.minimum(diff, 0.0)), 0.0)
    cb = jnp.einsum('bclgn,bcsgn->bclsg', cc, bc)
    y_diag = jnp.einsum('bclsg,bclsgr,bcsgrp->bclgrp', cb, decay, xc)
    decay_end = jnp.exp(acum[:, :, -1:] - acum)
    states = jnp.einsum('bclgn,bclgr,bclgrp->bcgrpn', bc, decay_end, xc)
    chunk_decay = jnp.exp(acum[:, :, -1])

    def step(h, inp):
        st, dcy = inp
        return h * dcy[..., None, None] + st, h

    h0r = h0.astype(f32).reshape(b, SSD_GROUPS, r, SSD_HEAD_DIM, D_STATE)
    h_last, h_starts = lax.scan(step, h0r, (states.transpose(1, 0, 2, 3, 4, 5),
                                            chunk_decay.transpose(1, 0, 2, 3)))
    h_starts = h_starts.transpose(1, 0, 2, 3, 4, 5)
    y_off = jnp.einsum('bclgn,bcgrpn,bclgr->bclgrp', cc, h_starts, jnp.exp(acum))
    y = (y_diag + y_off).reshape(b, t, SSD_HEADS, SSD_HEAD_DIM)
    return y, h_last.reshape(b, SSD_HEADS, SSD_HEAD_DIM, D_STATE)


def ssd_bidirectional(xbc, dt_raw, a_log, dt_bias, d_skip, h0_f, h0_b):
    b, t = xbc.shape[:2]
    gn = SSD_GROUPS * D_STATE
    xs, bs, cs = jnp.split(xbc, [D_INNER, D_INNER + gn], axis=-1)
    x = xs.reshape(b, t, SSD_HEADS, SSD_HEAD_DIM)
    bm = bs.reshape(b, t, SSD_GROUPS, D_STATE)
    cm = cs.reshape(b, t, SSD_GROUPS, D_STATE)
    dt = jax.nn.softplus(dt_raw.astype(jnp.float32).reshape(b, t, 2, SSD_HEADS)
                         + dt_bias.astype(jnp.float32))
    a_neg = -jnp.exp(a_log.astype(jnp.float32))
    y_f, h_f = ssd_chunked(x, dt[:, :, 0], a_neg[0], bm, cm, h0_f)
    y_b, h_b = ssd_chunked(x[:, ::-1], dt[:, ::-1, 1], a_neg[1], bm[:, ::-1], cm[:, ::-1], h0_b)
    y = y_f + y_b[:, ::-1] + d_skip.astype(jnp.float32)[:, None] * x.astype(jnp.float32)
    return y.reshape(b, t, D_INNER).astype(xbc.dtype), h_f, h_b


def token_mixer(h, lw, rope, ctx_k, ctx_v, h0_f, h0_b):
    b, t, _ = h.shape
    proj = h @ lw['w_in']
    idx = np.cumsum(SPLIT_SIZES)[:-1].tolist()
    q, k, v, z, xbc, dt_raw, gates = jnp.split(proj, idx, axis=-1)
    q = rmsnorm(q.reshape(b, t, N_HEADS, HEAD_DIM), lw['q_norm_g'])
    k = rmsnorm(k.reshape(b, t, N_KV_HEADS, HEAD_DIM), lw['k_norm_g'])
    v = v.reshape(b, t, N_KV_HEADS, HEAD_DIM)
    if rope is None:
        k_all, v_all = k, v
        h0_f = jnp.zeros((b, SSD_HEADS, SSD_HEAD_DIM, D_STATE), jnp.float32)
        h0_b = jnp.zeros((b, SSD_HEADS, SSD_HEAD_DIM, D_STATE), jnp.float32)
    else:
        cos, sin = rope
        q = apply_rope(q, cos, sin)
        k_lat = apply_rope(k, cos, sin)
        k_all = jnp.concatenate([k_lat, ctx_k.astype(k.dtype)], axis=1)
        v_all = jnp.concatenate([v, ctx_v.astype(v.dtype)], axis=1)
    attn_o = block_attention(q, k_all, v_all) @ lw['w_attn_proj']
    xbc = centred_dwconv(xbc, lw['conv_w'], lw['conv_b'])
    y, h_f, h_b = ssd_bidirectional(xbc, dt_raw, lw['a_log'], lw['dt_bias'], lw['d_skip'], h0_f, h0_b)
    ssd_o = rmsnorm(y * jax.nn.silu(z), lw['ssd_norm_g']) @ lw['w_ssd_proj']
    g_attn, g_ssd = jnp.split(gates, 2, axis=-1)
    merged = jax.nn.sigmoid(g_attn) * attn_o + jax.nn.sigmoid(g_ssd) * ssd_o
    return merged @ lw['w_out'], k, v, h_f, h_b


def moe(h, lw):
    b, t, d = h.shape
    f32 = jnp.float32
    tok = h.reshape(b * t, d)
    scores = jax.nn.sigmoid((tok @ lw['w_router']).astype(f32))
    sel = scores + lw['router_bias'].astype(f32)
    per_group = N_EXPERTS // N_EXPERT_GROUPS
    grp_score = lax.top_k(sel.reshape(-1, N_EXPERT_GROUPS, per_group), 2)[0].sum(-1)
    _, gidx = lax.top_k(grp_score, TOPK_GROUPS)
    gmask = jnp.sum(jax.nn.one_hot(gidx, N_EXPERT_GROUPS, dtype=f32), axis=-2) > 0
    sel = jnp.where(jnp.repeat(gmask, per_group, axis=-1), sel, -jnp.inf)
    _, eidx = lax.top_k(sel, TOP_K)
    w = jnp.take_along_axis(scores, eidx, axis=-1)
    w = w / jnp.sum(w, axis=-1, keepdims=True) * ROUTED_SCALE
    combine = jnp.sum(jax.nn.one_hot(eidx, N_EXPERTS, dtype=f32) * w[..., None], axis=-2)

    def expert_step(acc, ew):
        wg, wu, wd, cw = ew
        hid = jax.nn.silu(tok @ wg) * (tok @ wu)
        return acc + (hid @ wd) * cw[:, None].astype(tok.dtype), None

    routed, _ = lax.scan(expert_step, jnp.zeros_like(tok),
                         (lw['w_exp_gate'], lw['w_exp_up'], lw['w_exp_down'], combine.T))
    shared = (jax.nn.silu(tok @ lw['w_sh_gate']) * (tok @ lw['w_sh_up'])) @ lw['w_sh_down']
    return (routed + shared).reshape(b, t, d)


def trunk_layer(x, cvec, lw, rope, ctx_k, ctx_v, h0_f, h0_b):
    m = jax.nn.silu(cvec) @ lw['w_mod'] + lw['b_mod']
    sh1, sc1, g1, sh2, sc2, g2 = jnp.split(m, 6, axis=-1)
    h = rmsnorm(x, lw['norm1_g']) * (1 + sc1) + sh1
    mix, k, v, h_f, h_b = token_mixer(h, lw, rope, ctx_k, ctx_v, h0_f, h0_b)
    x = x + g1 * mix
    h = rmsnorm(x, lw['norm2_g']) * (1 + sc2) + sh2
    x = x + g2 * moe(h, lw)
    return x, k, v, h_f, h_b


def setup_inputs(seed: int = 0) -> dict:
    key = jax.random.key(seed)
    ks = jax.random.split(key, 32)
    nrm = jax.random.normal
    f = jnp.float32
    dt0 = jnp.exp(jax.random.uniform(ks[17], (DEPTH, 2, SSD_HEADS), minval=np.log(1e-3), maxval=np.log(1e-1)))
    return {
        'x_prompt': nrm(ks[0], (BATCH, SEQ, D_MODEL), f),
        'x_sample': nrm(ks[1], (DEC_BATCH, DEC_SEQ, D_MODEL), f),
        'cache_k': nrm(ks[2], (DEC_BATCH, DEPTH, PAST_LEN, N_KV_HEADS, HEAD_DIM), f),
        'cache_v': nrm(ks[3], (DEC_BATCH, DEPTH, PAST_LEN, N_KV_HEADS, HEAD_DIM), f),
        'state_ssm': 0.3 * nrm(ks[4], (DEC_BATCH, DEPTH, 2, SSD_HEADS, SSD_HEAD_DIM, D_STATE), f),
        'c': nrm(ks[5], (DEC_BATCH, D_MODEL), f),
        'c_ctx': nrm(ks[6], (D_MODEL,), f),
        'w_mod': 0.5 * D_MODEL ** -0.5 * nrm(ks[7], (DEPTH, D_MODEL, 6 * D_MODEL), f),
        'b_mod': 0.01 * nrm(ks[8], (DEPTH, 6 * D_MODEL), f),
        'norm1_g': 1.0 + 0.01 * nrm(ks[9], (DEPTH, D_MODEL), f),
        'norm2_g': 1.0 + 0.01 * nrm(ks[10], (DEPTH, D_MODEL), f),
        'w_in': D_MODEL ** -0.5 * nrm(ks[11], (DEPTH, D_MODEL, IN_COLS), f),
        'q_norm_g': 1.0 + 0.01 * nrm(ks[12], (DEPTH, HEAD_DIM), f),
        'k_norm_g': 1.0 + 0.01 * nrm(ks[13], (DEPTH, HEAD_DIM), f),
        'conv_w': D_CONV ** -0.5 * nrm(ks[14], (DEPTH, D_CONV, CONV_CH), f),
        'conv_b': 0.01 * nrm(ks[15], (DEPTH, CONV_CH), f),
        'a_log': jnp.log(jax.random.uniform(ks[16], (DEPTH, 2, SSD_HEADS), minval=1.0, maxval=16.0)),
        'dt_bias': dt0 + jnp.log(-jnp.expm1(-dt0)),
        'd_skip': 1.0 + 0.1 * nrm(ks[18], (DEPTH, SSD_HEADS), f),
        'ssd_norm_g': 1.0 + 0.01 * nrm(ks[19], (DEPTH, D_INNER), f),
        'w_attn_proj': ATTN_W ** -0.5 * nrm(ks[20], (DEPTH, ATTN_W, D_MODEL), f),
        'w_ssd_proj': D_INNER ** -0.5 * nrm(ks[21], (DEPTH, D_INNER, D_MODEL), f),
        'w_out': D_MODEL ** -0.5 * nrm(ks[22], (DEPTH, D_MODEL, D_MODEL), f),
        'w_router': D_MODEL ** -0.5 * nrm(ks[23], (DEPTH, D_MODEL, N_EXPERTS), f),
        'router_bias': 0.01 * nrm(ks[24], (DEPTH, N_EXPERTS), f),
        'w_exp_gate': D_MODEL ** -0.5 * nrm(ks[25], (DEPTH, N_EXPERTS, D_MODEL, D_EXPERT), f),
        'w_exp_up': D_MODEL ** -0.5 * nrm(ks[26], (DEPTH, N_EXPERTS, D_MODEL, D_EXPERT), f),
        'w_exp_down': D_EXPERT ** -0.5 * nrm(ks[27], (DEPTH, N_EXPERTS, D_EXPERT, D_MODEL), f),
        'w_sh_gate': D_MODEL ** -0.5 * nrm(ks[28], (DEPTH, D_MODEL, D_SHARED), f),
        'w_sh_up': D_MODEL ** -0.5 * nrm(ks[29], (DEPTH, D_MODEL, D_SHARED), f),
        'w_sh_down': D_SHARED ** -0.5 * nrm(ks[30], (DEPTH, D_SHARED, D_MODEL), f),
    }


def reference(x_prompt, x_sample, cache_k, cache_v, state_ssm, c, c_ctx, w_mod, b_mod,
              norm1_g, norm2_g, w_in, q_norm_g, k_norm_g, conv_w, conv_b, a_log, dt_bias,
              d_skip, ssd_norm_g, w_attn_proj, w_ssd_proj, w_out, w_router, router_bias,
              w_exp_gate, w_exp_up, w_exp_down, w_sh_gate, w_sh_up, w_sh_down):
    rope = axial_rope(x_sample.shape[1])
    c_prompt = c_ctx[None, None, :]
    c_sample = c[:, None, :]
    y_prompt, y_sample = x_prompt, x_sample
    ks_new, vs_new, ss_new = [], [], []
    for l in range(DEPTH):
        lw = {
            'w_mod': w_mod[l], 'b_mod': b_mod[l], 'norm1_g': norm1_g[l], 'norm2_g': norm2_g[l],
            'w_in': w_in[l], 'q_norm_g': q_norm_g[l], 'k_norm_g': k_norm_g[l],
            'conv_w': conv_w[l], 'conv_b': conv_b[l], 'a_log': a_log[l], 'dt_bias': dt_bias[l],
            'd_skip': d_skip[l], 'ssd_norm_g': ssd_norm_g[l], 'w_attn_proj': w_attn_proj[l],
            'w_ssd_proj': w_ssd_proj[l], 'w_out': w_out[l], 'w_router': w_router[l],
            'router_bias': router_bias[l], 'w_exp_gate': w_exp_gate[l], 'w_exp_up': w_exp_up[l],
            'w_exp_down': w_exp_down[l], 'w_sh_gate': w_sh_gate[l], 'w_sh_up': w_sh_up[l],
            'w_sh_down': w_sh_down[l],
        }
        y_prompt, k_ctx, v_ctx, hf_ctx, hb_ctx = trunk_layer(y_prompt, c_prompt, lw, None,
                                                             None, None, None, None)
        ks_new.append(k_ctx)
        vs_new.append(v_ctx)
        ss_new.append(jnp.stack([hf_ctx, hb_ctx], axis=1))
        y_sample, _, _, _, _ = trunk_layer(y_sample, c_sample, lw, rope, cache_k[:, l], cache_v[:, l],
                                           state_ssm[:, l, 0], state_ssm[:, l, 1])
    new_cache_k = jnp.stack(ks_new, axis=1)
    new_cache_v = jnp.stack(vs_new, axis=1)
    new_state_ssm = jnp.stack(ss_new, axis=1)
    return (y_prompt, y_sample, new_cache_k, new_cache_v, new_state_ssm)
```

```python
import functools

import numpy as np
import jax
import jax.numpy as jnp
from jax import lax
from jax.experimental import pallas as pl
from jax.experimental.pallas import tpu as pltpu

F32 = jnp.float32
BF16 = jnp.bfloat16

D_MODEL = 1024
GRID_W = 64
EPS = 1e-6
N_HEADS = 16
N_KV_HEADS = 4
HEAD_DIM = 64
ATTN_W = N_HEADS * HEAD_DIM
KV_W = N_KV_HEADS * HEAD_DIM
ROPE_THETA = 10000.0
D_INNER = 2048
SSD_HEAD_DIM = 64
SSD_HEADS = 32
SSD_GROUPS = 4
D_STATE = 128
D_CONV = 4
CHUNK = 128
CONV_CH = D_INNER + 2 * SSD_GROUPS * D_STATE
N_EXPERTS = 64
TOP_K = 8
N_EXPERT_GROUPS = 8
TOPK_GROUPS = 4
D_EXPERT = 256
D_SHARED = 256
ROUTED_SCALE = 2.5
SPLIT_SIZES = (ATTN_W, KV_W, KV_W, D_INNER, CONV_CH, 2 * SSD_HEADS, 2 * D_MODEL)

LANES = 128
KVD_W = N_KV_HEADS * LANES
C_Q, C_K, C_V, C_G, C_Z, C_X, C_DT, C_END = 0, 1024, 1536, 2048, 4096, 6144, 9216, 9472
VMEM_LIMIT = 56 * 1024 * 1024


def _cparams(sem):
    return pltpu.CompilerParams(dimension_semantics=sem, vmem_limit_bytes=VMEM_LIMIT)


def _silu(x):
    return x * jax.nn.sigmoid(x)


def _bdot(a, b):
    return jnp.dot(a.astype(BF16), b.astype(BF16), preferred_element_type=F32)


def _bdot_nt(a, b):
    return lax.dot_general(a.astype(BF16), b.astype(BF16), (((1,), (1,)), ((), ())),
                           preferred_element_type=F32)


def _mod_kernel(c_ref, w_ref, b_ref, o_ref):
    o_ref[...] = _bdot(_silu(c_ref[...]), w_ref[...]) + b_ref[...]


def _mod_call(cvec, w_mod, b_mod):
    n = w_mod.shape[1]
    bn = 1024
    return pl.pallas_call(
        _mod_kernel,
        out_shape=jax.ShapeDtypeStruct((8, n), F32),
        grid=(n // bn,),
        in_specs=[pl.BlockSpec((8, D_MODEL), lambda j: (0, 0)),
                  pl.BlockSpec((D_MODEL, bn), lambda j: (0, j)),
                  pl.BlockSpec((1, bn), lambda j: (0, j))],
        out_specs=pl.BlockSpec((8, bn), lambda j: (0, j)),
        compiler_params=_cparams(("arbitrary",)),
        name="mod",
    )(cvec, w_mod, b_mod)


def _inproj_kernel(x_ref, mod_ref, g1_ref, w_ref, qg_ref, kg_ref, cos_ref, sin_ref,
                   q_ref, k_ref, v_ref, gates_ref, z_ref, xbc_ref, dt_ref, *, rope):
    tm = x_ref.shape[0]
    x = x_ref[...]
    inv = lax.rsqrt(jnp.mean(x * x, axis=-1, keepdims=True) + EPS)
    h = (x * inv) * g1_ref[...]
    h = h * (1.0 + mod_ref[:, 1024:2048]) + mod_ref[:, 0:1024]
    hb = h.astype(BF16)

    lane = lax.broadcasted_iota(jnp.int32, (tm, LANES), 1)
    lo = lane < HEAD_DIM
    even = (lane & 1) == 0
    if rope:
        cos = cos_ref[...]
        sin = sin_ref[...]

    def rope_fn(blk):
        nxt = pltpu.roll(blk, LANES - 1, 1)
        prv = pltpu.roll(blk, 1, 1)
        return blk * cos + jnp.where(even, nxt, prv) * sin

    qg = qg_ref[...]
    kg = kg_ref[...]
    q = jnp.dot(hb, w_ref[:, C_Q:C_K], preferred_element_type=F32)
    for j in range(ATTN_W // LANES):
        blk = q[:, j * LANES:(j + 1) * LANES]
        sq = blk * blk
        s_all = jnp.sum(sq, axis=-1, keepdims=True)
        s_lo = jnp.sum(jnp.where(lo, sq, 0.0), axis=-1, keepdims=True)
        ms = jnp.where(lo, s_lo, s_all - s_lo) * (1.0 / HEAD_DIM)
        blk = blk * lax.rsqrt(ms + EPS) * qg
        if rope:
            blk = rope_fn(blk)
        q_ref[:, j * LANES:(j + 1) * LANES] = (blk * (HEAD_DIM ** -0.5)).astype(q_ref.dtype)

    k = jnp.dot(hb, w_ref[:, C_K:C_V], preferred_element_type=F32)
    for j in range(N_KV_HEADS):
        blk = k[:, j * LANES:(j + 1) * LANES]
        ms = jnp.mean(blk * blk, axis=-1, keepdims=True)
        blk = blk * lax.rsqrt(ms + EPS) * kg
        if rope:
            blk = rope_fn(blk)
        k_ref[:, j * LANES:(j + 1) * LANES] = blk.astype(k_ref.dtype)

    v_ref[...] = jnp.dot(hb, w_ref[:, C_V:C_G], preferred_element_type=F32).astype(v_ref.dtype)
    gates_ref[...] = jnp.dot(hb, w_ref[:, C_G:C_Z], preferred_element_type=F32).astype(gates_ref.dtype)
    z_ref[...] = jnp.dot(hb, w_ref[:, C_Z:C_X], preferred_element_type=F32).astype(z_ref.dtype)
    xbc_ref[...] = jnp.dot(hb, w_ref[:, C_X:C_DT], preferred_element_type=F32).astype(xbc_ref.dtype)
    dt_ref[...] = jnp.dot(hb, w_ref[:, C_DT:C_END], preferred_element_type=F32)


def _inproj_call(x, mod_rows, g1, w, qg, kg, cos, sin, *, rope, kv_dtype, tm):
    b, t, _ = x.shape
    nt = t // tm
    tok = lambda width: pl.BlockSpec((None, tm, width), lambda bi, i: (bi, i, 0))
    const2 = lambda shape: pl.BlockSpec(shape, lambda bi, i: (0, 0))
    out_shape = (
        jax.ShapeDtypeStruct((b, t, ATTN_W), BF16),
        jax.ShapeDtypeStruct((b, t, KVD_W), kv_dtype),
        jax.ShapeDtypeStruct((b, t, KVD_W), kv_dtype),
        jax.ShapeDtypeStruct((b, t, 2 * D_MODEL), BF16),
        jax.ShapeDtypeStruct((b, t, D_INNER), BF16),
        jax.ShapeDtypeStruct((b, t, CONV_CH), BF16),
        jax.ShapeDtypeStruct((b, t, 2 * LANES), F32),
    )
    return pl.pallas_call(
        functools.partial(_inproj_kernel, rope=rope),
        out_shape=out_shape,
        grid=(b, nt),
        in_specs=[tok(D_MODEL),
                  pl.BlockSpec((None, 1, 6 * D_MODEL), lambda bi, i: (bi, 0, 0)),
                  const2((1, D_MODEL)),
                  pl.BlockSpec((D_MODEL, C_END), lambda bi, i: (0, 0), pipeline_mode=pl.Buffered(1)),
                  const2((1, LANES)), const2((1, LANES)),
                  pl.BlockSpec((tm, LANES), lambda bi, i: (i, 0)),
                  pl.BlockSpec((tm, LANES), lambda bi, i: (i, 0))],
        out_specs=(tok(ATTN_W), tok(KVD_W), tok(KVD_W), tok(2 * D_MODEL), tok(D_INNER),
                   tok(CONV_CH), tok(2 * LANES)),
        compiler_params=_cparams(("arbitrary", "arbitrary")),
        name="inproj",
    )(x, mod_rows, g1, w, qg, kg, cos, sin)


def _attn_kernel(*refs, has_ctx):
    if has_ctx:
        q_ref, k_ref, v_ref, kc_ref, vc_ref, o_ref = refs
    else:
        q_ref, k_ref, v_ref, o_ref = refs
    tq = q_ref.shape[0]
    k = k_ref[...].astype(BF16)
    v = v_ref[...].astype(BF16)
    if has_ctx:
        kc = kc_ref[...].astype(BF16)
        vc = vc_ref[...].astype(BF16)
    lane = lax.broadcasted_iota(jnp.int32, (tq, LANES), 1)
    lo = lane < HEAD_DIM
    for j in range(2):
        q2 = q_ref[:, j * LANES:(j + 1) * LANES]
        halves = []
        for half in range(2):
            qa = jnp.where(lo if half == 0 else jnp.logical_not(lo), q2, jnp.zeros_like(q2))
            s = _bdot_nt(qa, k)
            m = jnp.max(s, axis=-1, keepdims=True)
            if has_ctx:
                sc = _bdot_nt(qa, kc)
                m = jnp.maximum(m, jnp.max(sc, axis=-1, keepdims=True))
            p = jnp.exp(s - m)
            l = jnp.sum(p, axis=-1, keepdims=True)
            o = jnp.dot(p.astype(BF16), v, preferred_element_type=F32)
            if has_ctx:
                pc = jnp.exp(sc - m)
                l = l + jnp.sum(pc, axis=-1, keepdims=True)
                o = o + jnp.dot(pc.astype(BF16), vc, preferred_element_type=F32)
            halves.append(o * (1.0 / l))
        o_ref[:, j * LANES:(j + 1) * LANES] = jnp.where(lo, halves[0], halves[1]).astype(o_ref.dtype)


def _attn_call(q, k, v, kc, vc, *, tq):
    b, t, _ = q.shape
    has_ctx = kc is not None
    nq = t // tq
    in_specs = [pl.BlockSpec((None, tq, 2 * LANES), lambda bi, g, i: (bi, i, g)),
                pl.BlockSpec((None, t, LANES), lambda bi, g, i: (bi, 0, g)),
                pl.BlockSpec((None, t, LANES), lambda bi, g, i: (bi, 0, g))]
    args = [q, k, v]
    if has_ctx:
        tc = kc.shape[1]
        in_specs += [pl.BlockSpec((None, tc, LANES), lambda bi, g, i: (bi, 0, g)),
                     pl.BlockSpec((None, tc, LANES), lambda bi, g, i: (bi, 0, g))]
        args += [kc, vc]
    return pl.pallas_call(
        functools.partial(_attn_kernel, has_ctx=has_ctx),
        out_shape=jax.ShapeDtypeStruct((b, t, ATTN_W), BF16),
        grid=(b, N_KV_HEADS, nq),
        in_specs=in_specs,
        out_specs=pl.BlockSpec((None, tq, 2 * LANES), lambda bi, g, i: (bi, i, g)),
        compiler_params=_cparams(("arbitrary", "arbitrary", "arbitrary")),
        name="attn",
    )(*args)


def _softplus(x):
    return jnp.maximum(x, 0.0) + jnp.log(1.0 + jnp.exp(-jnp.abs(x)))


def _ssd_kernel(*refs, nc, has_h0, want_hfin):
    refs = list(refs)
    xbc_ref, prev_ref, next_ref, dt_ref, cw_ref, cb_ref, an_ref, dtb_ref, dsk_ref, tri_ref = refs[:10]
    refs = refs[10:]
    h0_ref = refs.pop(0) if has_h0 else None
    y_ref = refs.pop(0)
    hfin_ref = refs.pop(0) if want_hfin else None
    h_scr = refs.pop(0)

    L = CHUNK
    d = pl.program_id(1)
    c = pl.program_id(2)
    cidx = c + d * (nc - 1 - 2 * c)

    @pl.when(c == 0)
    def _():
        if has_h0:
            h_scr[...] = h0_ref[...]
        else:
            h_scr[...] = jnp.zeros_like(h_scr)

    row = lax.broadcasted_iota(jnp.int32, (L, LANES), 0)
    lane = lax.broadcasted_iota(jnp.int32, (L, LANES), 1)
    lo = lane < SSD_HEAD_DIM
    top = row < SSD_HEAD_DIM
    first = cidx == 0
    last = cidx == nc - 1

    def conv_cols(a, w):
        xm = xbc_ref[:, a:a + w].astype(F32)
        rw = lax.broadcasted_iota(jnp.int32, (L, w), 0)
        p6 = jnp.where(first, 0.0, prev_ref[6:7, a:a + w].astype(F32))
        p7 = jnp.where(first, 0.0, prev_ref[7:8, a:a + w].astype(F32))
        n0 = jnp.where(last, 0.0, next_ref[0:1, a:a + w].astype(F32))
        r1 = jnp.where(rw == 0, p7, pltpu.roll(xm, 1, 0))
        r2 = jnp.where(rw == 0, p6, jnp.where(rw == 1, p7, pltpu.roll(xm, 2, 0)))
        rn = jnp.where(rw == L - 1, n0, pltpu.roll(xm, L - 1, 0))
        y = (r2 * cw_ref[0:1, a:a + w] + r1 * cw_ref[1:2, a:a + w] + xm * cw_ref[2:3, a:a + w]
             + rn * cw_ref[3:4, a:a + w] + cb_ref[:, a:a + w])
        return _silu(y)

    tri = tri_ref[...]
    dt = _softplus(dt_ref[...] + dtb_ref[...])
    la = dt * an_ref[...]
    acum = jnp.dot(tri, la, preferred_element_type=F32, precision=lax.Precision.HIGHEST)
    dt_t = dt.T
    la_t = la.T
    acum_t = acum.T
    tot_t = jnp.sum(la_t, axis=1, keepdims=True)
    w_t = dt_t * jnp.exp(tot_t - acum_t)
    e_acum = jnp.exp(acum)
    e_tot_t = jnp.exp(tot_t)
    dsk_on = jnp.where(d == 0, 1.0, 0.0)

    for g in range(SSD_GROUPS):
        bg = conv_cols(D_INNER + g * D_STATE, D_STATE)
        cg = conv_cols(D_INNER + SSD_GROUPS * D_STATE + g * D_STATE, D_STATE)
        cbm = _bdot_nt(cg, bg)
        bgb = bg.astype(BF16)
        cgb = cg.astype(BF16)
        for pr in range(4):
            hp = g * 4 + pr
            ha, hb = 2 * hp, 2 * hp + 1
            xp = conv_cols(hp * LANES, LANES)
            xpb = xp.astype(BF16)
            ys = []
            for hh in (ha, hb):
                diff = acum[:, hh:hh + 1] - acum_t[hh:hh + 1, :]
                m = cbm * (jnp.exp(jnp.minimum(diff, 0.0)) * tri) * dt_t[hh:hh + 1, :]
                ys.append(jnp.dot(m.astype(BF16), xpb, preferred_element_type=F32))
            y = jnp.where(lo, ys[0], ys[1])
            h_old = h_scr[hp]
            yo = lax.dot_general(cgb, h_old.astype(BF16), (((1,), (1,)), ((), ())),
                                 preferred_element_type=F32)
            y = y + yo * jnp.where(lo, e_acum[:, ha:ha + 1], e_acum[:, hb:hb + 1])
            y = y + (dsk_on * dsk_ref[:, hp * LANES:(hp + 1) * LANES]) * xp
            y_ref[:, hp * LANES:(hp + 1) * LANES] = y.astype(y_ref.dtype)
            wsel = jnp.where(top, w_t[ha:ha + 1, :], w_t[hb:hb + 1, :])
            st = jnp.dot((xp.T * wsel).astype(BF16), bgb, preferred_element_type=F32)
            cd = jnp.where(top, e_tot_t[ha:ha + 1, :], e_tot_t[hb:hb + 1, :])
            h_scr[hp] = h_old * cd + st

    if want_hfin:
        @pl.when(c == nc - 1)
        def _():
            hfin_ref[...] = h_scr[...]


def _ssd_call(xbc, dt, conv_w, conv_b, a_neg, dt_bias, dskip, tri, h0, *, want_hfin, y_dtype):
    b, t, _ = xbc.shape
    nc = t // CHUNK
    has_h0 = h0 is not None
    rb = CHUNK // 8
    nrb = t // 8

    def cmap(d, c):
        return c + d * (nc - 1 - 2 * c)

    in_specs = [
        pl.BlockSpec((None, CHUNK, CONV_CH), lambda bi, d, c: (bi, cmap(d, c), 0)),
        pl.BlockSpec((None, 8, CONV_CH), lambda bi, d, c: (bi, jnp.maximum(cmap(d, c) * rb - 1, 0), 0)),
        pl.BlockSpec((None, 8, CONV_CH), lambda bi, d, c: (bi, jnp.minimum((cmap(d, c) + 1) * rb, nrb - 1), 0)),
        pl.BlockSpec((None, CHUNK, LANES), lambda bi, d, c: (bi, cmap(d, c), d)),
        pl.BlockSpec((D_CONV, CONV_CH), lambda bi, d, c: (0, 0)),
        pl.BlockSpec((1, CONV_CH), lambda bi, d, c: (0, 0)),
        pl.BlockSpec((None, 1, LANES), lambda bi, d, c: (d, 0, 0)),
        pl.BlockSpec((None, 1, LANES), lambda bi, d, c: (d, 0, 0)),
        pl.BlockSpec((1, D_INNER), lambda bi, d, c: (0, 0)),
        pl.BlockSpec((None, CHUNK, CHUNK), lambda bi, d, c: (d, 0, 0)),
    ]
    args = [xbc, xbc, xbc, dt, conv_w, conv_b, a_neg, dt_bias, dskip, tri]
    hshape = (SSD_HEADS // 2, 2 * SSD_HEAD_DIM, D_STATE)
    if has_h0:
        in_specs.append(pl.BlockSpec((None, None) + hshape, lambda bi, d, c: (bi, d, 0, 0, 0)))
        args.append(h0)
    out_shape = [jax.ShapeDtypeStruct((2, b, t, D_INNER), y_dtype)]
    out_specs = [pl.BlockSpec((None, None, CHUNK, D_INNER), lambda bi, d, c: (d, bi, cmap(d, c), 0))]
    if want_hfin:
        out_shape.append(jax.ShapeDtypeStruct((b, 2) + hshape, F32))
        out_specs.append(pl.BlockSpec((None, None) + hshape, lambda bi, d, c: (bi, d, 0, 0, 0)))
    res = pl.pallas_call(
        functools.partial(_ssd_kernel, nc=nc, has_h0=has_h0, want_hfin=want_hfin),
        out_shape=tuple(out_shape),
        grid=(b, 2, nc),
        in_specs=in_specs,
        out_specs=tuple(out_specs),
        scratch_shapes=[pltpu.VMEM(hshape, F32)],
        compiler_params=_cparams(("arbitrary", "arbitrary", "arbitrary")),
        name="ssd",
    )(*args)
    return res if want_hfin else (res[0], None)


def _route(logits_t, bias_col):
    e, n = logits_t.shape
    per = e // N_EXPERT_GROUPS
    scores = jax.nn.sigmoid(logits_t)
    sel = scores + bias_col
    neg = jnp.float32(-jnp.inf)
    gs = []
    for g in range(N_EXPERT_GROUPS):
        blk = sel[g * per:(g + 1) * per, :]
        m1 = jnp.max(blk, axis=0, keepdims=True)
        is_m1 = blk == m1
        cnt = jnp.sum(jnp.where(is_m1, 1.0, 0.0), axis=0, keepdims=True)
        m2 = jnp.max(jnp.where(is_m1, neg, blk), axis=0, keepdims=True)
        gs.append(m1 + jnp.where(cnt >= 2.0, m1, m2))
    keep = []
    for g in range(N_EXPERT_GROUPS):
        rank = jnp.zeros_like(gs[g])
        for j in range(N_EXPERT_GROUPS):
            if j == g:
                continue
            beats = (gs[j] > gs[g]) if j > g else (gs[j] >= gs[g])
            rank = rank + jnp.where(beats, 1.0, 0.0)
        keep.append(rank < float(TOPK_GROUPS))
    selm = jnp.concatenate(
        [jnp.where(keep[g], sel[g * per:(g + 1) * per, :], neg) for g in range(N_EXPERT_GROUPS)], axis=0)
    eidx = lax.broadcasted_iota(jnp.int32, (e, n), 0)
    rank = jnp.zeros((e, n), F32)
    for j in range(e):
        rj = selm[j:j + 1, :]
        tie = jnp.where(eidx > j, 1.0, 0.0)
        rank = rank + jnp.where(rj > selm, 1.0, jnp.where(rj == selm, tie, 0.0))
    w = jnp.where(rank < float(TOP_K), scores, 0.0)
    wsum = jnp.sum(w, axis=0, keepdims=True)
    return w / wsum * ROUTED_SCALE


def _merge_kernel(x_ref, attn_ref, yf_ref, yb_ref, z_ref, gates_ref, mod_ref, wa_ref, ws_ref, wo_ref,
                  sg_ref, n2_ref, wr_ref, rb_ref, wsg_ref, wsu_ref, wsd_ref,
                  xb_ref, h2_ref, comb_ref):
    x = x_ref[...]
    yy = yf_ref[...].astype(F32) + yb_ref[...].astype(F32)
    u = yy * _silu(z_ref[...].astype(F32))
    un = u * lax.rsqrt(jnp.mean(u * u, axis=-1, keepdims=True) + EPS) * sg_ref[...]
    ssd_o = _bdot(un, ws_ref[...])
    attn_o = jnp.dot(attn_ref[...], wa_ref[...], preferred_element_type=F32)
    ga = jax.nn.sigmoid(gates_ref[:, 0:D_MODEL].astype(F32))
    gs = jax.nn.sigmoid(gates_ref[:, D_MODEL:2 * D_MODEL].astype(F32))
    mix = _bdot(ga * attn_o + gs * ssd_o, wo_ref[...])
    x1 = x + mod_ref[:, 2048:3072] * mix
    h2 = x1 * lax.rsqrt(jnp.mean(x1 * x1, axis=-1, keepdims=True) + EPS) * n2_ref[...]
    h2 = h2 * (1.0 + mod_ref[:, 4096:5120]) + mod_ref[:, 3072:4096]
    h2b = h2.astype(BF16)
    h2_ref[...] = h2b
    logits_t = _bdot_nt(wr_ref[...], h2b)
    comb_t = _route(logits_t, rb_ref[...])
    comb_t = jnp.concatenate([comb_t, jnp.zeros_like(comb_t)], axis=0)
    comb_ref[...] = comb_t.T
    hid = _silu(jnp.dot(h2b, wsg_ref[...], preferred_element_type=F32)) * \
        jnp.dot(h2b, wsu_ref[...], preferred_element_type=F32)
    xb_ref[...] = x1 + mod_ref[:, 5120:6144] * _bdot(hid, wsd_ref[...])


def _merge_call(x, attn, y2, z, gates, mod_rows, wa, ws, wo, sg, n2, wr_t, rb, wsg, wsu, wsd, *, tm):
    b, t, _ = x.shape
    nt = t // tm
    tok = lambda width: pl.BlockSpec((None, tm, width), lambda bi, i: (bi, i, 0))
    const2 = lambda shape: pl.BlockSpec(shape, lambda bi, i: (0, 0))
    return pl.pallas_call(
        _merge_kernel,
        out_shape=(jax.ShapeDtypeStruct((b, t, D_MODEL), F32),
                   jax.ShapeDtypeStruct((b, t, D_MODEL), BF16),
                   jax.ShapeDtypeStruct((b, t, LANES), F32)),
        grid=(b, nt),
        in_specs=[tok(D_MODEL), tok(ATTN_W),
                  pl.BlockSpec((None, None, tm, D_INNER), lambda bi, i: (0, bi, i, 0)),
                  pl.BlockSpec((None, None, tm, D_INNER), lambda bi, i: (1, bi, i, 0)),
                  tok(D_INNER), tok(2 * D_MODEL),
                  pl.BlockSpec((None, 1, 6 * D_MODEL), lambda bi, i: (bi, 0, 0)),
                  const2((ATTN_W, D_MODEL)), const2((D_INNER, D_MODEL)), const2((D_MODEL, D_MODEL)),
                  const2((1, D_INNER)), const2((1, D_MODEL)),
                  const2((N_EXPERTS, D_MODEL)), const2((N_EXPERTS, 1)),
                  const2((D_MODEL, D_SHARED)), const2((D_MODEL, D_SHARED)), const2((D_SHARED, D_MODEL))],
        out_specs=(tok(D_MODEL), tok(D_MODEL), tok(LANES)),
        compiler_params=_cparams(("arbitrary", "arbitrary")),
        name="merge",
    )(x, attn, y2, y2, z, gates, mod_rows, wa, ws, wo, sg, n2, wr_t, rb, wsg, wsu, wsd)


def _moe_kernel(h2_ref, comb_ref, mod_ref, wg_ref, wu_ref, wd_ref, acc_ref, o_ref, *, eb):
    j = pl.program_id(0)
    h2b = h2_ref[...]
    comb = comb_ref[...]
    lane = lax.broadcasted_iota(jnp.int32, comb.shape, 1)
    tot = jnp.zeros(o_ref.shape, F32)
    for e in range(eb):
        hid = _silu(jnp.dot(h2b, wg_ref[e], preferred_element_type=F32)) * \
            jnp.dot(h2b, wu_ref[e], preferred_element_type=F32)
        cw = jnp.sum(jnp.where(lane == j * eb + e, comb, 0.0), axis=-1, keepdims=True)
        tot = tot + _bdot(hid, wd_ref[e]) * cw
    o_ref[...] = acc_ref[...] + mod_ref[:, 5120:6144] * tot


def _moe_call(h2, comb, mod_rows, wg, wu, wd, acc, *, tm, eb):
    b, t, _ = h2.shape
    nt = t // tm
    tok = lambda width: pl.BlockSpec((None, tm, width), lambda j, bi, i: (bi, i, 0))
    return pl.pallas_call(
        functools.partial(_moe_kernel, eb=eb),
        out_shape=jax.ShapeDtypeStruct((b, t, D_MODEL), F32),
        grid=(N_EXPERTS // eb, b, nt),
        in_specs=[tok(D_MODEL), tok(LANES),
                  pl.BlockSpec((None, 1, 6 * D_MODEL), lambda j, bi, i: (bi, 0, 0)),
                  pl.BlockSpec((eb, D_MODEL, D_EXPERT), lambda j, bi, i: (j, 0, 0)),
                  pl.BlockSpec((eb, D_MODEL, D_EXPERT), lambda j, bi, i: (j, 0, 0)),
                  pl.BlockSpec((eb, D_EXPERT, D_MODEL), lambda j, bi, i: (j, 0, 0)),
                  tok(D_MODEL)],
        out_specs=tok(D_MODEL),
        input_output_aliases={6: 0},
        compiler_params=_cparams(("arbitrary", "arbitrary", "arbitrary")),
        name="moe",
    )(h2, comb, mod_rows, wg, wu, wd, acc)


def _rope_tables(t):
    n_rows = t // GRID_W
    rows = jnp.repeat(jnp.arange(n_rows), GRID_W).astype(F32)
    cols = jnp.tile(jnp.arange(GRID_W), n_rows).astype(F32)
    n_freq = HEAD_DIM // 4
    freqs = ROPE_THETA ** (-jnp.arange(n_freq, dtype=F32) / n_freq)
    ang = jnp.concatenate([rows[:, None] * freqs, cols[:, None] * freqs], axis=-1)
    ang = jnp.repeat(ang, 2, axis=-1)
    ang = jnp.concatenate([ang, ang], axis=-1)
    sign = jnp.where(jnp.arange(LANES) % 2 == 0, -1.0, 1.0).astype(F32)
    return jnp.cos(ang), jnp.sin(ang) * sign


def _dup_heads(a):
    s = a.shape[:-1]
    a4 = a.reshape(s + (N_KV_HEADS, HEAD_DIM))
    return jnp.concatenate([a4, a4], axis=-1).reshape(s + (KVD_W,))


def _prep_w_in(w_in):
    idx = np.cumsum(SPLIT_SIZES)[:-1].tolist()
    q, k, v, z, xbc, dt, gates = jnp.split(w_in, idx, axis=-1)
    pad = jnp.zeros((D_MODEL, LANES - SSD_HEADS), w_in.dtype)
    cols = [q, _dup_heads(k), _dup_heads(v), gates, z, xbc,
            dt[:, :SSD_HEADS], pad, dt[:, SSD_HEADS:], pad]
    return jnp.concatenate(cols, axis=-1).astype(BF16)


def _pad_heads(a):
    return jnp.pad(a.astype(F32), ((0, 0), (0, LANES - SSD_HEADS)))[:, None, :]


def _trunk(x, mod_rows, wts, rope_tabs, ctx_k, ctx_v, h0, *, tm, tq, want_state):
    b, t, _ = x.shape
    rope = rope_tabs is not None
    if rope:
        cos, sin = rope_tabs
    else:
        cos = sin = jnp.zeros((t, LANES), F32)
    kv_dtype = BF16 if rope else F32
    q, k, v, gates, z, xbc, dt = _inproj_call(x, mod_rows, wts["g1"], wts["w_in"], wts["qg"], wts["kg"],
                                              cos, sin, rope=rope, kv_dtype=kv_dtype, tm=tm)
    attn = _attn_call(q, k, v, ctx_k, ctx_v, tq=tq)
    y2, hfin = _ssd_call(xbc, dt, wts["conv_w"], wts["conv_b"], wts["a_neg"], wts["dt_bias"], wts["dskip"],
                         wts["tri"], h0, want_hfin=want_state, y_dtype=F32)
    xb, h2, comb = _merge_call(x, attn, y2, z, gates, mod_rows, wts["wa"], wts["ws"], wts["wo"], wts["sg"],
                               wts["n2"], wts["wr_t"], wts["rb"], wts["wsg"], wts["wsu"], wts["wsd"], tm=tm)
    out = _moe_call(h2, comb, mod_rows, wts["weg"], wts["weu"], wts["wed"], xb, tm=tm, eb=8)
    return out, k, v, hfin


def kernel(x_prompt, x_sample, cache_k, cache_v, state_ssm, c, c_ctx, w_mod, b_mod, norm1_g, norm2_g, w_in,
           q_norm_g, k_norm_g, conv_w, conv_b, a_log, dt_bias, d_skip, ssd_norm_g, w_attn_proj, w_ssd_proj,
           w_out, w_router, router_bias, w_exp_gate, w_exp_up, w_exp_down, w_sh_gate, w_sh_up, w_sh_down):
    depth = w_mod.shape[0]
    assert depth == 1, "single trunk layer"
    bp, tp, _ = x_prompt.shape
    bs, ts, _ = x_sample.shape
    l = 0
    cvec = jnp.concatenate([c_ctx[None, :], c, jnp.zeros((8 - 1 - bs, D_MODEL), F32)], axis=0)
    mod = _mod_call(cvec, w_mod[l], b_mod[l][None, :])
    mod_prompt = jnp.broadcast_to(mod[0:1][:, None, :], (bp, 1, 6 * D_MODEL))
    mod_sample = mod[1:1 + bs][:, None, :]

    lower = np.tril(np.ones((CHUNK, CHUNK), np.float32))
    wts = dict(
        g1=norm1_g[l][None, :], n2=norm2_g[l][None, :],
        w_in=_prep_w_in(w_in[l]),
        qg=jnp.tile(q_norm_g[l], 2)[None, :], kg=jnp.tile(k_norm_g[l], 2)[None, :],
        conv_w=conv_w[l], conv_b=conv_b[l][None, :],
        a_neg=_pad_heads(-jnp.exp(a_log[l].astype(F32))), dt_bias=_pad_heads(dt_bias[l]),
        dskip=jnp.repeat(d_skip[l].astype(F32), SSD_HEAD_DIM)[None, :],
        tri=jnp.asarray(np.stack([lower, lower.T])),
        sg=ssd_norm_g[l][None, :],
        wa=w_attn_proj[l].astype(BF16), ws=w_ssd_proj[l].astype(BF16), wo=w_out[l].astype(BF16),
        wr_t=w_router[l].T.astype(BF16), rb=router_bias[l].astype(F32)[:, None],
        wsg=w_sh_gate[l].astype(BF16), wsu=w_sh_up[l].astype(BF16), wsd=w_sh_down[l].astype(BF16),
        weg=w_exp_gate[l].astype(BF16), weu=w_exp_up[l].astype(BF16), wed=w_exp_down[l].astype(BF16),
    )

    y_prompt, k_p, v_p, hfin = _trunk(x_prompt, mod_prompt, wts, None, None, None, None,
                                      tm=256, tq=256, want_state=True)
    new_k = k_p.reshape(bp, tp, N_KV_HEADS, LANES)[..., :HEAD_DIM][:, None]
    new_v = v_p.reshape(bp, tp, N_KV_HEADS, LANES)[..., :HEAD_DIM][:, None]
    new_state = hfin.reshape(bp, 1, 2, SSD_HEADS, SSD_HEAD_DIM, D_STATE)

    past = cache_k.shape[2]
    ctx_k = _dup_heads(cache_k[:, l].reshape(bs, past, KV_W)).astype(BF16)
    ctx_v = _dup_heads(cache_v[:, l].reshape(bs, past, KV_W)).astype(BF16)
    h0 = state_ssm[:, l].reshape(bs, 2, SSD_HEADS // 2, 2 * SSD_HEAD_DIM, D_STATE)
    y_sample, _, _, _ = _trunk(x_sample, mod_sample, wts, _rope_tables(ts), ctx_k, ctx_v, h0,
                               tm=256, tq=128, want_state=False)
    return (y_prompt, y_sample, new_k, new_v, new_state)
```

```python
import functools

import numpy as np
import jax
import jax.numpy as jnp
from jax import lax
from jax.experimental import pallas as pl
from jax.experimental.pallas import tpu as pltpu

F32 = jnp.float32
BF16 = jnp.bfloat16

D_MODEL = 1024
GRID_W = 64
EPS = 1e-6
N_HEADS = 16
N_KV_HEADS = 4
HEAD_DIM = 64
ATTN_W = N_HEADS * HEAD_DIM
KV_W = N_KV_HEADS * HEAD_DIM
ROPE_THETA = 10000.0
D_INNER = 2048
SSD_HEAD_DIM = 64
SSD_HEADS = 32
SSD_GROUPS = 4
D_STATE = 128
D_CONV = 4
CHUNK = 128
CONV_CH = D_INNER + 2 * SSD_GROUPS * D_STATE
N_EXPERTS = 64
TOP_K = 8
N_EXPERT_GROUPS = 8
TOPK_GROUPS = 4
D_EXPERT = 256
D_SHARED = 256
ROUTED_SCALE = 2.5
SPLIT_SIZES = (ATTN_W, KV_W, KV_W, D_INNER, CONV_CH, 2 * SSD_HEADS, 2 * D_MODEL)

LANES = 128
KVD_W = N_KV_HEADS * LANES
C_Q, C_K, C_V, C_G, C_Z, C_X, C_DT, C_END = 0, 1024, 1536, 2048, 4096, 6144, 9216, 9472
VMEM_LIMIT = 56 * 1024 * 1024
Q_SCALE = HEAD_DIM ** -0.5 * 1.4426950408889634


def _cparams(sem):
    return pltpu.CompilerParams(dimension_semantics=sem, vmem_limit_bytes=VMEM_LIMIT)


def _silu(x):
    return x * jax.nn.sigmoid(x)


def _bdot(a, b):
    return jnp.dot(a.astype(BF16), b.astype(BF16), preferred_element_type=F32)


def _bdot_nt(a, b):
    return lax.dot_general(a.astype(BF16), b.astype(BF16), (((1,), (1,)), ((), ())),
                           preferred_element_type=F32)


def _mod_kernel(c_ref, w_ref, b_ref, o_ref):
    o_ref[...] = _bdot(_silu(c_ref[...]), w_ref[...]) + b_ref[...]


def _mod_call(cvec, w_mod, b_mod):
    n = w_mod.shape[1]
    bn = 1024
    return pl.pallas_call(
        _mod_kernel,
        out_shape=jax.ShapeDtypeStruct((8, n), F32),
        grid=(n // bn,),
        in_specs=[pl.BlockSpec((8, D_MODEL), lambda j: (0, 0)),
                  pl.BlockSpec((D_MODEL, bn), lambda j: (0, j)),
                  pl.BlockSpec((1, bn), lambda j: (0, j))],
        out_specs=pl.BlockSpec((8, bn), lambda j: (0, j)),
        compiler_params=_cparams(("arbitrary",)),
        name="mod",
    )(cvec, w_mod, b_mod)


def _inproj_kernel(x_ref, mod_ref, g1_ref, w_ref, qg_ref, kg_ref, cos_ref, sin_ref,
                   q_ref, k_ref, v_ref, gates_ref, z_ref, xbc_ref, dt_ref, *, rope):
    tm = x_ref.shape[0]
    x = x_ref[...]
    inv = lax.rsqrt(jnp.mean(x * x, axis=-1, keepdims=True) + EPS)
    h = (x * inv) * g1_ref[...]
    h = h * (1.0 + mod_ref[:, 1024:2048]) + mod_ref[:, 0:1024]
    hb = h.astype(BF16)

    lane = lax.broadcasted_iota(jnp.int32, (tm, LANES), 1)
    lo = lane < HEAD_DIM
    even = (lane & 1) == 0
    if rope:
        cos = cos_ref[...]
        sin = sin_ref[...]

    def rope_fn(blk):
        nxt = pltpu.roll(blk, LANES - 1, 1)
        prv = pltpu.roll(blk, 1, 1)
        return blk * cos + jnp.where(even, nxt, prv) * sin

    qg = qg_ref[...]
    kg = kg_ref[...]
    q = jnp.dot(hb, w_ref[:, C_Q:C_K], preferred_element_type=F32)
    for j in range(ATTN_W // LANES):
        blk = q[:, j * LANES:(j + 1) * LANES]
        sq = blk * blk
        s_all = jnp.sum(sq, axis=-1, keepdims=True)
        s_lo = jnp.sum(jnp.where(lo, sq, 0.0), axis=-1, keepdims=True)
        ms = jnp.where(lo, s_lo, s_all - s_lo) * (1.0 / HEAD_DIM)
        blk = blk * lax.rsqrt(ms + EPS) * qg
        if rope:
            blk = rope_fn(blk)
        q_ref[:, j * LANES:(j + 1) * LANES] = (blk * Q_SCALE).astype(q_ref.dtype)

    k = jnp.dot(hb, w_ref[:, C_K:C_V], preferred_element_type=F32)
    for j in range(N_KV_HEADS):
        blk = k[:, j * LANES:(j + 1) * LANES]
        ms = jnp.mean(blk * blk, axis=-1, keepdims=True)
        blk = blk * lax.rsqrt(ms + EPS) * kg
        if rope:
            blk = rope_fn(blk)
        k_ref[:, j * LANES:(j + 1) * LANES] = blk.astype(k_ref.dtype)

    v_ref[...] = jnp.dot(hb, w_ref[:, C_V:C_G], preferred_element_type=F32).astype(v_ref.dtype)
    gates_ref[...] = jnp.dot(hb, w_ref[:, C_G:C_Z], preferred_element_type=F32).astype(gates_ref.dtype)
    z_ref[...] = jnp.dot(hb, w_ref[:, C_Z:C_X], preferred_element_type=F32).astype(z_ref.dtype)
    xbc_ref[...] = jnp.dot(hb, w_ref[:, C_X:C_DT], preferred_element_type=F32).astype(xbc_ref.dtype)
    dt_ref[...] = jnp.dot(hb, w_ref[:, C_DT:C_END], preferred_element_type=F32)


def _inproj_call(x, mod_rows, g1, w, qg, kg, cos, sin, *, rope, kv_dtype, tm):
    b, t, _ = x.shape
    nt = t // tm
    tok = lambda width: pl.BlockSpec((None, tm, width), lambda bi, i: (bi, i, 0))
    const2 = lambda shape: pl.BlockSpec(shape, lambda bi, i: (0, 0))
    out_shape = (
        jax.ShapeDtypeStruct((b, t, ATTN_W), BF16),
        jax.ShapeDtypeStruct((b, t, KVD_W), kv_dtype),
        jax.ShapeDtypeStruct((b, t, KVD_W), kv_dtype),
        jax.ShapeDtypeStruct((b, t, 2 * D_MODEL), BF16),
        jax.ShapeDtypeStruct((b, t, D_INNER), BF16),
        jax.ShapeDtypeStruct((b, t, CONV_CH), BF16),
        jax.ShapeDtypeStruct((b, t, 2 * LANES), F32),
    )
    return pl.pallas_call(
        functools.partial(_inproj_kernel, rope=rope),
        out_shape=out_shape,
        grid=(b, nt),
        in_specs=[tok(D_MODEL),
                  pl.BlockSpec((None, 1, 6 * D_MODEL), lambda bi, i: (bi, 0, 0)),
                  const2((1, D_MODEL)),
                  pl.BlockSpec((D_MODEL, C_END), lambda bi, i: (0, 0), pipeline_mode=pl.Buffered(1)),
                  const2((1, LANES)), const2((1, LANES)),
                  pl.BlockSpec((tm, LANES), lambda bi, i: (i, 0)),
                  pl.BlockSpec((tm, LANES), lambda bi, i: (i, 0))],
        out_specs=(tok(ATTN_W), tok(KVD_W), tok(KVD_W), tok(2 * D_MODEL), tok(D_INNER),
                   tok(CONV_CH), tok(2 * LANES)),
        compiler_params=_cparams(("arbitrary", "arbitrary")),
        name="inproj",
    )(x, mod_rows, g1, w, qg, kg, cos, sin)


def _attn_kernel(q_ref, k_ref, v_ref, o_ref, *, kc):
    tq = q_ref.shape[0]
    tk = k_ref.shape[0]
    lane = lax.broadcasted_iota(jnp.int32, (tq, LANES), 1)
    lo = lane < HEAD_DIM
    qs = []
    for j in range(2):
        q2 = q_ref[:, j * LANES:(j + 1) * LANES]
        zero = jnp.zeros_like(q2)
        qs += [jnp.where(lo, q2, zero), jnp.where(lo, zero, q2)]
    q4 = jnp.concatenate(qs, axis=0)
    rows = 4 * tq
    m = jnp.full((rows, 1), -jnp.inf, F32)
    l = jnp.zeros((rows, 1), F32)
    acc = jnp.zeros((rows, LANES), F32)
    for c in range(tk // kc):
        kch = k_ref[c * kc:(c + 1) * kc, :].astype(BF16)
        vch = v_ref[c * kc:(c + 1) * kc, :].astype(BF16)
        s = _bdot_nt(q4, kch)
        m_new = jnp.maximum(m, jnp.max(s, axis=-1, keepdims=True))
        alpha = jnp.exp2(m - m_new)
        p = jnp.exp2(s - m_new)
        l = l * alpha + jnp.sum(p, axis=-1, keepdims=True)
        acc = acc * alpha + jnp.dot(p.astype(BF16), vch, preferred_element_type=F32)
        m = m_new
    o = acc * (1.0 / l)
    for j in range(2):
        oa = o[(2 * j) * tq:(2 * j + 1) * tq]
        ob = o[(2 * j + 1) * tq:(2 * j + 2) * tq]
        o_ref[:, j * LANES:(j + 1) * LANES] = jnp.where(lo, oa, ob).astype(o_ref.dtype)


def _attn_call(q, k, v, *, tq):
    b, t, _ = q.shape
    tk = k.shape[1]
    nq = t // tq
    kc = 512 if tk % 512 == 0 else tk
    return pl.pallas_call(
        functools.partial(_attn_kernel, kc=kc),
        out_shape=jax.ShapeDtypeStruct((b, t, ATTN_W), BF16),
        grid=(b, N_KV_HEADS, nq),
        in_specs=[pl.BlockSpec((None, tq, 2 * LANES), lambda bi, g, i: (bi, i, g)),
                  pl.BlockSpec((None, tk, LANES), lambda bi, g, i: (bi, 0, g)),
                  pl.BlockSpec((None, tk, LANES), lambda bi, g, i: (bi, 0, g))],
        out_specs=pl.BlockSpec((None, tq, 2 * LANES), lambda bi, g, i: (bi, i, g)),
        compiler_params=_cparams(("arbitrary", "arbitrary", "arbitrary")),
        name="attn",
    )(q, k, v)


def _softplus(x):
    return jnp.maximum(x, 0.0) + jnp.log(1.0 + jnp.exp(-jnp.abs(x)))


def _ssd_kernel(*refs, nc, has_h0, want_hfin):
    refs = list(refs)
    xbc_ref, prev_ref, next_ref, dt_ref, cw_ref, cb_ref, an_ref, dtb_ref, dsk_ref, tri_ref = refs[:10]
    refs = refs[10:]
    h0_ref = refs.pop(0) if has_h0 else None
    y_ref = refs.pop(0)
    hfin_ref = refs.pop(0) if want_hfin else None
    h_scr = refs.pop(0)

    L = CHUNK
    d = pl.program_id(1)
    c = pl.program_id(2)
    cidx = c + d * (nc - 1 - 2 * c)

    @pl.when(c == 0)
    def _():
        if has_h0:
            h_scr[...] = h0_ref[...]
        else:
            h_scr[...] = jnp.zeros_like(h_scr)

    row = lax.broadcasted_iota(jnp.int32, (L, LANES), 0)
    lane = lax.broadcasted_iota(jnp.int32, (L, LANES), 1)
    lo = lane < SSD_HEAD_DIM
    top = row < SSD_HEAD_DIM
    first = cidx == 0
    last = cidx == nc - 1

    def conv_cols(a, w):
        xm = xbc_ref[:, a:a + w].astype(F32)
        rw = lax.broadcasted_iota(jnp.int32, (L, w), 0)
        p6 = jnp.where(first, 0.0, prev_ref[6:7, a:a + w].astype(F32))
        p7 = jnp.where(first, 0.0, prev_ref[7:8, a:a + w].astype(F32))
        n0 = jnp.where(last, 0.0, next_ref[0:1, a:a + w].astype(F32))
        r1 = jnp.where(rw == 0, p7, pltpu.roll(xm, 1, 0))
        r2 = jnp.where(rw == 0, p6, jnp.where(rw == 1, p7, pltpu.roll(xm, 2, 0)))
        rn = jnp.where(rw == L - 1, n0, pltpu.roll(xm, L - 1, 0))
        y = (r2 * cw_ref[0:1, a:a + w] + r1 * cw_ref[1:2, a:a + w] + xm * cw_ref[2:3, a:a + w]
             + rn * cw_ref[3:4, a:a + w] + cb_ref[:, a:a + w])
        return _silu(y)

    tri = tri_ref[...]
    dt = _softplus(dt_ref[...] + dtb_ref[...])
    la = dt * an_ref[...]
    acum = jnp.dot(tri, la, preferred_element_type=F32, precision=lax.Precision.HIGHEST)
    dt_t = dt.T
    la_t = la.T
    acum_t = acum.T
    tot_t = jnp.sum(la_t, axis=1, keepdims=True)
    w_t = dt_t * jnp.exp(tot_t - acum_t)
    e_acum = jnp.exp(acum)
    e_tot_t = jnp.exp(tot_t)
    dsk_on = jnp.where(d == 0, 1.0, 0.0)

    for g in range(SSD_GROUPS):
        bg = conv_cols(D_INNER + g * D_STATE, D_STATE)
        cg = conv_cols(D_INNER + SSD_GROUPS * D_STATE + g * D_STATE, D_STATE)
        cbm = _bdot_nt(cg, bg)
        bgb = bg.astype(BF16)
        cgb = cg.astype(BF16)
        for pr in range(4):
            hp = g * 4 + pr
            ha, hb = 2 * hp, 2 * hp + 1
            xp = conv_cols(hp * LANES, LANES)
            xpb = xp.astype(BF16)
            ys = []
            for hh in (ha, hb):
                diff = acum[:, hh:hh + 1] - acum_t[hh:hh + 1, :]
                m = cbm * (jnp.exp(jnp.minimum(diff, 0.0)) * tri) * dt_t[hh:hh + 1, :]
                ys.append(jnp.dot(m.astype(BF16), xpb, preferred_element_type=F32))
            y = jnp.where(lo, ys[0], ys[1])
            h_old = h_scr[hp]
            yo = lax.dot_general(cgb, h_old.astype(BF16), (((1,), (1,)), ((), ())),
                                 preferred_element_type=F32)
            y = y + yo * jnp.where(lo, e_acum[:, ha:ha + 1], e_acum[:, hb:hb + 1])
            y = y + (dsk_on * dsk_ref[:, hp * LANES:(hp + 1) * LANES]) * xp
            y_ref[:, hp * LANES:(hp + 1) * LANES] = y.astype(y_ref.dtype)
            wsel = jnp.where(top, w_t[ha:ha + 1, :], w_t[hb:hb + 1, :])
            st = jnp.dot((xp.T * wsel).astype(BF16), bgb, preferred_element_type=F32)
            cd = jnp.where(top, e_tot_t[ha:ha + 1, :], e_tot_t[hb:hb + 1, :])
            h_scr[hp] = h_old * cd + st

    if want_hfin:
        @pl.when(c == nc - 1)
        def _():
            hfin_ref[...] = h_scr[...]


def _ssd_call(xbc, dt, conv_w, conv_b, a_neg, dt_bias, dskip, tri, h0, *, want_hfin, y_dtype):
    b, t, _ = xbc.shape
    nc = t // CHUNK
    has_h0 = h0 is not None
    rb = CHUNK // 8
    nrb = t // 8

    def cmap(d, c):
        return c + d * (nc - 1 - 2 * c)

    in_specs = [
        pl.BlockSpec((None, CHUNK, CONV_CH), lambda bi, d, c: (bi, cmap(d, c), 0)),
        pl.BlockSpec((None, 8, CONV_CH), lambda bi, d, c: (bi, jnp.maximum(cmap(d, c) * rb - 1, 0), 0)),
        pl.BlockSpec((None, 8, CONV_CH), lambda bi, d, c: (bi, jnp.minimum((cmap(d, c) + 1) * rb, nrb - 1), 0)),
        pl.BlockSpec((None, CHUNK, LANES), lambda bi, d, c: (bi, cmap(d, c), d)),
        pl.BlockSpec((D_CONV, CONV_CH), lambda bi, d, c: (0, 0)),
        pl.BlockSpec((1, CONV_CH), lambda bi, d, c: (0, 0)),
        pl.BlockSpec((None, 1, LANES), lambda bi, d, c: (d, 0, 0)),
        pl.BlockSpec((None, 1, LANES), lambda bi, d, c: (d, 0, 0)),
        pl.BlockSpec((1, D_INNER), lambda bi, d, c: (0, 0)),
        pl.BlockSpec((None, CHUNK, CHUNK), lambda bi, d, c: (d, 0, 0)),
    ]
    args = [xbc, xbc, xbc, dt, conv_w, conv_b, a_neg, dt_bias, dskip, tri]
    hshape = (SSD_HEADS // 2, 2 * SSD_HEAD_DIM, D_STATE)
    if has_h0:
        in_specs.append(pl.BlockSpec((None, None) + hshape, lambda bi, d, c: (bi, d, 0, 0, 0)))
        args.append(h0)
    out_shape = [jax.ShapeDtypeStruct((2, b, t, D_INNER), y_dtype)]
    out_specs = [pl.BlockSpec((None, None, CHUNK, D_INNER), lambda bi, d, c: (d, bi, cmap(d, c), 0))]
    if want_hfin:
        out_shape.append(jax.ShapeDtypeStruct((b, 2) + hshape, F32))
        out_specs.append(pl.BlockSpec((None, None) + hshape, lambda bi, d, c: (bi, d, 0, 0, 0)))
    res = pl.pallas_call(
        functools.partial(_ssd_kernel, nc=nc, has_h0=has_h0, want_hfin=want_hfin),
        out_shape=tuple(out_shape),
        grid=(b, 2, nc),
        in_specs=in_specs,
        out_specs=tuple(out_specs),
        scratch_shapes=[pltpu.VMEM(hshape, F32)],
        compiler_params=_cparams(("arbitrary", "arbitrary", "arbitrary")),
        name="ssd",
    )(*args)
    return res if want_hfin else (res[0], None)


def _route(logits_t, bias_col):
    e, n = logits_t.shape
    per = e // N_EXPERT_GROUPS
    scores = jax.nn.sigmoid(logits_t)
    sel = scores + bias_col
    neg = jnp.float32(-jnp.inf)
    gs = []
    for g in range(N_EXPERT_GROUPS):
        blk = sel[g * per:(g + 1) * per, :]
        m1 = jnp.max(blk, axis=0, keepdims=True)
        is_m1 = blk == m1
        cnt = jnp.sum(jnp.where(is_m1, 1.0, 0.0), axis=0, keepdims=True)
        m2 = jnp.max(jnp.where(is_m1, neg, blk), axis=0, keepdims=True)
        gs.append(m1 + jnp.where(cnt >= 2.0, m1, m2))
    keep = []
    for g in range(N_EXPERT_GROUPS):
        rank = jnp.zeros_like(gs[g])
        for j in range(N_EXPERT_GROUPS):
            if j == g:
                continue
            beats = (gs[j] > gs[g]) if j > g else (gs[j] >= gs[g])
            rank = rank + jnp.where(beats, 1.0, 0.0)
        keep.append(rank < float(TOPK_GROUPS))
    selm = jnp.concatenate(
        [jnp.where(keep[g], sel[g * per:(g + 1) * per, :], neg) for g in range(N_EXPERT_GROUPS)], axis=0)
    eidx = lax.broadcasted_iota(jnp.int32, (e, n), 0)
    rank = jnp.zeros((e, n), F32)
    for j in range(e):
        rj = selm[j:j + 1, :]
        tie = jnp.where(eidx > j, 1.0, 0.0)
        rank = rank + jnp.where(rj > selm, 1.0, jnp.where(rj == selm, tie, 0.0))
    w = jnp.where(rank < float(TOP_K), scores, 0.0)
    wsum = jnp.sum(w, axis=0, keepdims=True)
    return w / wsum * ROUTED_SCALE


def _merge_kernel(x_ref, attn_ref, yf_ref, yb_ref, z_ref, gates_ref, mod_ref, wa_ref, ws_ref, wo_ref,
                  sg_ref, n2_ref, wr_ref, rb_ref, wsg_ref, wsu_ref, wsd_ref,
                  xb_ref, h2_ref, comb_ref):
    x = x_ref[...]
    yy = yf_ref[...].astype(F32) + yb_ref[...].astype(F32)
    u = yy * _silu(z_ref[...].astype(F32))
    un = u * lax.rsqrt(jnp.mean(u * u, axis=-1, keepdims=True) + EPS) * sg_ref[...]
    ssd_o = _bdot(un, ws_ref[...])
    attn_o = jnp.dot(attn_ref[...], wa_ref[...], preferred_element_type=F32)
    ga = jax.nn.sigmoid(gates_ref[:, 0:D_MODEL].astype(F32))
    gs = jax.nn.sigmoid(gates_ref[:, D_MODEL:2 * D_MODEL].astype(F32))
    mix = _bdot(ga * attn_o + gs * ssd_o, wo_ref[...])
    x1 = x + mod_ref[:, 2048:3072] * mix
    h2 = x1 * lax.rsqrt(jnp.mean(x1 * x1, axis=-1, keepdims=True) + EPS) * n2_ref[...]
    h2 = h2 * (1.0 + mod_ref[:, 4096:5120]) + mod_ref[:, 3072:4096]
    h2b = h2.astype(BF16)
    h2_ref[...] = h2b
    logits_t = _bdot_nt(wr_ref[...], h2b)
    comb_t = _route(logits_t, rb_ref[...])
    comb_t = jnp.concatenate([comb_t, jnp.zeros_like(comb_t)], axis=0)
    comb_ref[...] = comb_t.T
    hid = _silu(jnp.dot(h2b, wsg_ref[...], preferred_element_type=F32)) * \
        jnp.dot(h2b, wsu_ref[...], preferred_element_type=F32)
    xb_ref[...] = x1 + mod_ref[:, 5120:6144] * _bdot(hid, wsd_ref[...])


def _merge_call(x, attn, y2, z, gates, mod_rows, wa, ws, wo, sg, n2, wr_t, rb, wsg, wsu, wsd, *, tm):
    b, t, _ = x.shape
    nt = t // tm
    tok = lambda width: pl.BlockSpec((None, tm, width), lambda bi, i: (bi, i, 0))
    const2 = lambda shape: pl.BlockSpec(shape, lambda bi, i: (0, 0))
    return pl.pallas_call(
        _merge_kernel,
        out_shape=(jax.ShapeDtypeStruct((b, t, D_MODEL), F32),
                   jax.ShapeDtypeStruct((b, t, D_MODEL), BF16),
                   jax.ShapeDtypeStruct((b, t, LANES), F32)),
        grid=(b, nt),
        in_specs=[tok(D_MODEL), tok(ATTN_W),
                  pl.BlockSpec((None, None, tm, D_INNER), lambda bi, i: (0, bi, i, 0)),
                  pl.BlockSpec((None, None, tm, D_INNER), lambda bi, i: (1, bi, i, 0)),
                  tok(D_INNER), tok(2 * D_MODEL),
                  pl.BlockSpec((None, 1, 6 * D_MODEL), lambda bi, i: (bi, 0, 0)),
                  const2((ATTN_W, D_MODEL)), const2((D_INNER, D_MODEL)), const2((D_MODEL, D_MODEL)),
                  const2((1, D_INNER)), const2((1, D_MODEL)),
                  const2((N_EXPERTS, D_MODEL)), const2((N_EXPERTS, 1)),
                  const2((D_MODEL, D_SHARED)), const2((D_MODEL, D_SHARED)), const2((D_SHARED, D_MODEL))],
        out_specs=(tok(D_MODEL), tok(D_MODEL), tok(LANES)),
        compiler_params=_cparams(("arbitrary", "arbitrary")),
        name="merge",
    )(x, attn, y2, y2, z, gates, mod_rows, wa, ws, wo, sg, n2, wr_t, rb, wsg, wsu, wsd)


def _moe_kernel(h2_ref, comb_ref, mod_ref, wg_ref, wu_ref, wd_ref, xb_ref, o_ref, acc_ref, *, eb, neb):
    j = pl.program_id(1)

    @pl.when(j == 0)
    def _():
        acc_ref[...] = jnp.zeros_like(acc_ref)

    h2b = h2_ref[...]
    comb = comb_ref[...]
    lane = lax.broadcasted_iota(jnp.int32, comb.shape, 1)
    tot = jnp.zeros(acc_ref.shape, F32)
    for e in range(eb):
        hid = _silu(jnp.dot(h2b, wg_ref[e], preferred_element_type=F32)) * \
            jnp.dot(h2b, wu_ref[e], preferred_element_type=F32)
        cw = jnp.sum(jnp.where(lane == j * eb + e, comb, 0.0), axis=-1, keepdims=True)
        tot = tot + _bdot(hid, wd_ref[e]) * cw
    acc_ref[...] += tot

    @pl.when(j == neb - 1)
    def _():
        o_ref[...] = xb_ref[...] + mod_ref[:, 5120:6144] * acc_ref[...]


def _moe_call(h2, comb, mod_rows, wg, wu, wd, xb, *, tm, eb):
    b, t, _ = h2.shape
    n = b * t
    assert n % tm == 0 and (tm % t == 0 or t % tm == 0)
    neb = N_EXPERTS // eb
    tok = lambda width: pl.BlockSpec((tm, width), lambda i, j: (i, 0))
    wspec = lambda s: pl.BlockSpec((eb,) + s, lambda i, j: (j, 0, 0))
    out = pl.pallas_call(
        functools.partial(_moe_kernel, eb=eb, neb=neb),
        out_shape=jax.ShapeDtypeStruct((n, D_MODEL), F32),
        grid=(n // tm, neb),
        in_specs=[tok(D_MODEL), tok(LANES),
                  pl.BlockSpec((None, 1, 6 * D_MODEL), lambda i, j: ((i * tm) // t, 0, 0)),
                  wspec((D_MODEL, D_EXPERT)), wspec((D_MODEL, D_EXPERT)), wspec((D_EXPERT, D_MODEL)),
                  tok(D_MODEL)],
        out_specs=tok(D_MODEL),
        scratch_shapes=[pltpu.VMEM((tm, D_MODEL), F32)],
        compiler_params=_cparams(("arbitrary", "arbitrary")),
        name="moe",
    )(h2.reshape(n, D_MODEL), comb.reshape(n, LANES), mod_rows, wg, wu, wd, xb.reshape(n, D_MODEL))
    return out.reshape(b, t, D_MODEL)


def _rope_tables(t):
    n_rows = t // GRID_W
    rows = jnp.repeat(jnp.arange(n_rows), GRID_W).astype(F32)
    cols = jnp.tile(jnp.arange(GRID_W), n_rows).astype(F32)
    n_freq = HEAD_DIM // 4
    freqs = ROPE_THETA ** (-jnp.arange(n_freq, dtype=F32) / n_freq)
    ang = jnp.concatenate([rows[:, None] * freqs, cols[:, None] * freqs], axis=-1)
    ang = jnp.repeat(ang, 2, axis=-1)
    ang = jnp.concatenate([ang, ang], axis=-1)
    sign = jnp.where(jnp.arange(LANES) % 2 == 0, -1.0, 1.0).astype(F32)
    return jnp.cos(ang), jnp.sin(ang) * sign


def _dup_heads(a):
    s = a.shape[:-1]
    a4 = a.reshape(s + (N_KV_HEADS, HEAD_DIM))
    return jnp.concatenate([a4, a4], axis=-1).reshape(s + (KVD_W,))


def _prep_w_in(w_in):
    idx = np.cumsum(SPLIT_SIZES)[:-1].tolist()
    q, k, v, z, xbc, dt, gates = jnp.split(w_in, idx, axis=-1)
    pad = jnp.zeros((D_MODEL, LANES - SSD_HEADS), w_in.dtype)
    cols = [q, _dup_heads(k), _dup_heads(v), gates, z, xbc,
            dt[:, :SSD_HEADS], pad, dt[:, SSD_HEADS:], pad]
    return jnp.concatenate(cols, axis=-1).astype(BF16)


def _pad_heads(a):
    return jnp.pad(a.astype(F32), ((0, 0), (0, LANES - SSD_HEADS)))[:, None, :]


def _trunk(x, mod_rows, wts, rope_tabs, ctx_k, ctx_v, h0, *, tm, tq, want_state):
    b, t, _ = x.shape
    rope = rope_tabs is not None
    if rope:
        cos, sin = rope_tabs
    else:
        cos = sin = jnp.zeros((t, LANES), F32)
    kv_dtype = BF16 if rope else F32
    q, k, v, gates, z, xbc, dt = _inproj_call(x, mod_rows, wts["g1"], wts["w_in"], wts["qg"], wts["kg"],
                                              cos, sin, rope=rope, kv_dtype=kv_dtype, tm=tm)
    if ctx_k is not None:
        k_all = jnp.concatenate([k, ctx_k], axis=1)
        v_all = jnp.concatenate([v, ctx_v], axis=1)
    else:
        k_all, v_all = k, v
    attn = _attn_call(q, k_all, v_all, tq=tq)
    y2, hfin = _ssd_call(xbc, dt, wts["conv_w"], wts["conv_b"], wts["a_neg"], wts["dt_bias"], wts["dskip"],
                         wts["tri"], h0, want_hfin=want_state, y_dtype=F32)
    xb, h2, comb = _merge_call(x, attn, y2, z, gates, mod_rows, wts["wa"], wts["ws"], wts["wo"], wts["sg"],
                               wts["n2"], wts["wr_t"], wts["rb"], wts["wsg"], wts["wsu"], wts["wsd"], tm=tm)
    out = _moe_call(h2, comb, mod_rows, wts["weg"], wts["weu"], wts["wed"], xb, tm=1024, eb=4)
    return out, k, v, hfin


def kernel(x_prompt, x_sample, cache_k, cache_v, state_ssm, c, c_ctx, w_mod, b_mod, norm1_g, norm2_g, w_in,
           q_norm_g, k_norm_g, conv_w, conv_b, a_log, dt_bias, d_skip, ssd_norm_g, w_attn_proj, w_ssd_proj,
           w_out, w_router, router_bias, w_exp_gate, w_exp_up, w_exp_down, w_sh_gate, w_sh_up, w_sh_down):
    depth = w_mod.shape[0]
    assert depth == 1, "single trunk layer"
    bp, tp, _ = x_prompt.shape
    bs, ts, _ = x_sample.shape
    l = 0
    cvec = jnp.concatenate([c_ctx[None, :], c, jnp.zeros((8 - 1 - bs, D_MODEL), F32)], axis=0)
    mod = _mod_call(cvec, w_mod[l], b_mod[l][None, :])
    mod_prompt = jnp.broadcast_to(mod[0:1][:, None, :], (bp, 1, 6 * D_MODEL))
    mod_sample = mod[1:1 + bs][:, None, :]

    lower = np.tril(np.ones((CHUNK, CHUNK), np.float32))
    wts = dict(
        g1=norm1_g[l][None, :], n2=norm2_g[l][None, :],
        w_in=_prep_w_in(w_in[l]),
        qg=jnp.tile(q_norm_g[l], 2)[None, :], kg=jnp.tile(k_norm_g[l], 2)[None, :],
        conv_w=conv_w[l], conv_b=conv_b[l][None, :],
        a_neg=_pad_heads(-jnp.exp(a_log[l].astype(F32))), dt_bias=_pad_heads(dt_bias[l]),
        dskip=jnp.repeat(d_skip[l].astype(F32), SSD_HEAD_DIM)[None, :],
        tri=jnp.asarray(np.stack([lower, lower.T])),
        sg=ssd_norm_g[l][None, :],
        wa=w_attn_proj[l].astype(BF16), ws=w_ssd_proj[l].astype(BF16), wo=w_out[l].astype(BF16),
        wr_t=w_router[l].T.astype(BF16), rb=router_bias[l].astype(F32)[:, None],
        wsg=w_sh_gate[l].astype(BF16), wsu=w_sh_up[l].astype(BF16), wsd=w_sh_down[l].astype(BF16),
        weg=w_exp_gate[l].astype(BF16), weu=w_exp_up[l].astype(BF16), wed=w_exp_down[l].astype(BF16),
    )

    y_prompt, k_p, v_p, hfin = _trunk(x_prompt, mod_prompt, wts, None, None, None, None,
                                      tm=256, tq=256, want_state=True)
    new_k = k_p.reshape(bp, tp, N_KV_HEADS, LANES)[..., :HEAD_DIM][:, None]
    new_v = v_p.reshape(bp, tp, N_KV_HEADS, LANES)[..., :HEAD_DIM][:, None]
    new_state = hfin.reshape(bp, 1, 2, SSD_HEADS, SSD_HEAD_DIM, D_STATE)

    past = cache_k.shape[2]
    ctx_k = _dup_heads(cache_k[:, l].reshape(bs, past, KV_W)).astype(BF16)
    ctx_v = _dup_heads(cache_v[:, l].reshape(bs, past, KV_W)).astype(BF16)
    h0 = state_ssm[:, l].reshape(bs, 2, SSD_HEADS // 2, 2 * SSD_HEAD_DIM, D_STATE)
    y_sample, _, _, _ = _trunk(x_sample, mod_sample, wts, _rope_tables(ts), ctx_k, ctx_v, h0,
                               tm=256, tq=128, want_state=False)
    return (y_prompt, y_sample, new_k, new_v, new_state)
```

```python
import functools

import numpy as np
import jax
import jax.numpy as jnp
from jax import lax
from jax.experimental import pallas as pl
from jax.experimental.pallas import tpu as pltpu

F32 = jnp.float32
BF16 = jnp.bfloat16

D_MODEL = 1024
GRID_W = 64
EPS = 1e-6
N_HEADS = 16
N_KV_HEADS = 4
HEAD_DIM = 64
ATTN_W = N_HEADS * HEAD_DIM
KV_W = N_KV_HEADS * HEAD_DIM
ROPE_THETA = 10000.0
D_INNER = 2048
SSD_HEAD_DIM = 64
SSD_HEADS = 32
SSD_GROUPS = 4
D_STATE = 128
D_CONV = 4
CHUNK = 128
CONV_CH = D_INNER + 2 * SSD_GROUPS * D_STATE
N_EXPERTS = 64
TOP_K = 8
N_EXPERT_GROUPS = 8
TOPK_GROUPS = 4
D_EXPERT = 256
D_SHARED = 256
ROUTED_SCALE = 2.5
SPLIT_SIZES = (ATTN_W, KV_W, KV_W, D_INNER, CONV_CH, 2 * SSD_HEADS, 2 * D_MODEL)

LANES = 128
KVD_W = N_KV_HEADS * LANES
C_Q, C_K, C_V, C_G, C_Z, C_X, C_DT, C_END = 0, 1024, 1536, 2048, 4096, 6144, 9216, 9472
VMEM_LIMIT = 56 * 1024 * 1024
Q_SCALE = HEAD_DIM ** -0.5 * 1.4426950408889634


def _cparams(sem):
    return pltpu.CompilerParams(dimension_semantics=sem, vmem_limit_bytes=VMEM_LIMIT)


def _silu(x):
    return x * jax.nn.sigmoid(x)


def _bdot(a, b):
    return jnp.dot(a.astype(BF16), b.astype(BF16), preferred_element_type=F32)


def _bdot_nt(a, b):
    return lax.dot_general(a.astype(BF16), b.astype(BF16), (((1,), (1,)), ((), ())),
                           preferred_element_type=F32)


def _mod_kernel(c_ref, w_ref, b_ref, o_ref):
    o_ref[...] = _bdot(_silu(c_ref[...]), w_ref[...]) + b_ref[...]


def _mod_call(cvec, w_mod, b_mod):
    n = w_mod.shape[1]
    bn = 1024
    return pl.pallas_call(
        _mod_kernel,
        out_shape=jax.ShapeDtypeStruct((8, n), F32),
        grid=(n // bn,),
        in_specs=[pl.BlockSpec((8, D_MODEL), lambda j: (0, 0)),
                  pl.BlockSpec((D_MODEL, bn), lambda j: (0, j)),
                  pl.BlockSpec((1, bn), lambda j: (0, j))],
        out_specs=pl.BlockSpec((8, bn), lambda j: (0, j)),
        compiler_params=_cparams(("arbitrary",)),
        name="mod",
    )(cvec, w_mod, b_mod)


def _inproj_kernel(x_ref, mod_ref, g1_ref, w_ref, qg_ref, kg_ref, cos_ref, sin_ref,
                   q_ref, k_ref, v_ref, gates_ref, z_ref, xbc_ref, dt_ref, *, rope):
    tm = x_ref.shape[0]
    x = x_ref[...]
    inv = lax.rsqrt(jnp.mean(x * x, axis=-1, keepdims=True) + EPS)
    h = (x * inv) * g1_ref[...]
    h = h * (1.0 + mod_ref[:, 1024:2048]) + mod_ref[:, 0:1024]
    hb = h.astype(BF16)

    lane = lax.broadcasted_iota(jnp.int32, (tm, LANES), 1)
    lo = lane < HEAD_DIM
    even = (lane & 1) == 0
    if rope:
        cos = cos_ref[...]
        sin = sin_ref[...]

    def rope_fn(blk):
        nxt = pltpu.roll(blk, LANES - 1, 1)
        prv = pltpu.roll(blk, 1, 1)
        return blk * cos + jnp.where(even, nxt, prv) * sin

    qg = qg_ref[...]
    kg = kg_ref[...]
    q = jnp.dot(hb, w_ref[:, C_Q:C_K], preferred_element_type=F32)
    for j in range(ATTN_W // LANES):
        blk = q[:, j * LANES:(j + 1) * LANES]
        sq = blk * blk
        s_all = jnp.sum(sq, axis=-1, keepdims=True)
        s_lo = jnp.sum(jnp.where(lo, sq, 0.0), axis=-1, keepdims=True)
        ms = jnp.where(lo, s_lo, s_all - s_lo) * (1.0 / HEAD_DIM)
        blk = blk * lax.rsqrt(ms + EPS) * qg
        if rope:
            blk = rope_fn(blk)
        q_ref[:, j * LANES:(j + 1) * LANES] = (blk * Q_SCALE).astype(q_ref.dtype)

    k = jnp.dot(hb, w_ref[:, C_K:C_V], preferred_element_type=F32)
    for j in range(N_KV_HEADS):
        blk = k[:, j * LANES:(j + 1) * LANES]
        ms = jnp.mean(blk * blk, axis=-1, keepdims=True)
        blk = blk * lax.rsqrt(ms + EPS) * kg
        if rope:
            blk = rope_fn(blk)
        k_ref[:, j * LANES:(j + 1) * LANES] = blk.astype(k_ref.dtype)

    v_ref[...] = jnp.dot(hb, w_ref[:, C_V:C_G], preferred_element_type=F32).astype(v_ref.dtype)
    gates_ref[...] = jnp.dot(hb, w_ref[:, C_G:C_Z], preferred_element_type=F32).astype(gates_ref.dtype)
    z_ref[...] = jnp.dot(hb, w_ref[:, C_Z:C_X], preferred_element_type=F32).astype(z_ref.dtype)
    xbc_ref[...] = jnp.dot(hb, w_ref[:, C_X:C_DT], preferred_element_type=F32).astype(xbc_ref.dtype)
    dt_ref[...] = jnp.dot(hb, w_ref[:, C_DT:C_END], preferred_element_type=F32)


def _inproj_call(x, mod_rows, g1, w, qg, kg, cos, sin, *, rope, kv_dtype, tm):
    b, t, _ = x.shape
    nt = t // tm
    tok = lambda width: pl.BlockSpec((None, tm, width), lambda bi, i: (bi, i, 0))
    const2 = lambda shape: pl.BlockSpec(shape, lambda bi, i: (0, 0))
    out_shape = (
        jax.ShapeDtypeStruct((b, t, ATTN_W), BF16),
        jax.ShapeDtypeStruct((b, t, KVD_W), kv_dtype),
        jax.ShapeDtypeStruct((b, t, KVD_W), kv_dtype),
        jax.ShapeDtypeStruct((b, t, 2 * D_MODEL), BF16),
        jax.ShapeDtypeStruct((b, t, D_INNER), BF16),
        jax.ShapeDtypeStruct((b, t, CONV_CH), BF16),
        jax.ShapeDtypeStruct((b, t, 2 * LANES), F32),
    )
    return pl.pallas_call(
        functools.partial(_inproj_kernel, rope=rope),
        out_shape=out_shape,
        grid=(b, nt),
        in_specs=[tok(D_MODEL),
                  pl.BlockSpec((None, 1, 6 * D_MODEL), lambda bi, i: (bi, 0, 0)),
                  const2((1, D_MODEL)),
                  pl.BlockSpec((D_MODEL, C_END), lambda bi, i: (0, 0), pipeline_mode=pl.Buffered(1)),
                  const2((1, LANES)), const2((1, LANES)),
                  pl.BlockSpec((tm, LANES), lambda bi, i: (i, 0)),
                  pl.BlockSpec((tm, LANES), lambda bi, i: (i, 0))],
        out_specs=(tok(ATTN_W), tok(KVD_W), tok(KVD_W), tok(2 * D_MODEL), tok(D_INNER),
                   tok(CONV_CH), tok(2 * LANES)),
        compiler_params=_cparams(("arbitrary", "arbitrary")),
        name="inproj",
    )(x, mod_rows, g1, w, qg, kg, cos, sin)


def _attn_kernel(q_ref, k_ref, v_ref, o_ref, *, kc):
    tq = q_ref.shape[0]
    tk = k_ref.shape[0]
    lane = lax.broadcasted_iota(jnp.int32, (tq, LANES), 1)
    lo = lane < HEAD_DIM
    qs = []
    for j in range(2):
        q2 = q_ref[:, j * LANES:(j + 1) * LANES]
        zero = jnp.zeros_like(q2)
        qs += [jnp.where(lo, q2, zero), jnp.where(lo, zero, q2)]
    q4 = jnp.concatenate(qs, axis=0)
    rows = 4 * tq
    lane_k = lax.broadcasted_iota(jnp.int32, (kc, LANES), 1)
    m = jnp.full((rows, 1), -jnp.inf, F32)
    acc = jnp.zeros((rows, LANES), F32)
    for c in range(tk // kc):
        kch = k_ref[c * kc:(c + 1) * kc, :].astype(BF16)
        vch = v_ref[c * kc:(c + 1) * kc, :].astype(BF16)
        vch = jnp.where(lane_k < HEAD_DIM, vch, jnp.ones_like(vch))
        s = _bdot_nt(q4, kch)
        m_new = jnp.maximum(m, jnp.max(s, axis=-1, keepdims=True))
        alpha = jnp.exp2(m - m_new)
        p = jnp.exp2((s - m_new).astype(BF16))
        acc = acc * alpha + jnp.dot(p, vch, preferred_element_type=F32)
        m = m_new
    o = acc * (1.0 / pltpu.roll(acc, HEAD_DIM, 1))
    for j in range(2):
        oa = o[(2 * j) * tq:(2 * j + 1) * tq]
        ob = pltpu.roll(o[(2 * j + 1) * tq:(2 * j + 2) * tq], HEAD_DIM, 1)
        o_ref[:, j * LANES:(j + 1) * LANES] = jnp.where(lo, oa, ob).astype(o_ref.dtype)


def _attn_call(q, k, v, *, tq):
    b, t, _ = q.shape
    tk = k.shape[1]
    nq = t // tq
    kc = 512 if tk % 512 == 0 else tk
    return pl.pallas_call(
        functools.partial(_attn_kernel, kc=kc),
        out_shape=jax.ShapeDtypeStruct((b, t, ATTN_W), BF16),
        grid=(b, N_KV_HEADS, nq),
        in_specs=[pl.BlockSpec((None, tq, 2 * LANES), lambda bi, g, i: (bi, i, g)),
                  pl.BlockSpec((None, tk, LANES), lambda bi, g, i: (bi, 0, g)),
                  pl.BlockSpec((None, tk, LANES), lambda bi, g, i: (bi, 0, g))],
        out_specs=pl.BlockSpec((None, tq, 2 * LANES), lambda bi, g, i: (bi, i, g)),
        compiler_params=_cparams(("arbitrary", "arbitrary", "arbitrary")),
        name="attn",
    )(q, k, v)


LOG2E = 1.4426950408889634


def _softplus(x):
    return jnp.maximum(x, 0.0) + jnp.log(1.0 + jnp.exp(-jnp.abs(x)))


def _ssd_kernel(*refs, nc, reverse, has_h0, want_hfin):
    refs = list(refs)
    conv = not reverse
    if conv:
        xbc_ref, prev_ref, next_ref, cw_ref, cb_ref, dsk_ref = refs[:6]
        refs = refs[6:]
    else:
        xc_ref = refs.pop(0)
    dt_ref, an_ref, dtb_ref, tri_ref = refs[:4]
    refs = refs[4:]
    h0_ref = refs.pop(0) if has_h0 else None
    hprev_ref = refs.pop(0) if (want_hfin and reverse) else None
    y_ref = refs.pop(0)
    xco_ref = refs.pop(0) if conv else None
    hfin_ref = refs.pop(0) if want_hfin else None
    h_scr = refs.pop(0)
    del hprev_ref

    L = CHUNK
    c = pl.program_id(1)
    cidx = (nc - 1 - c) if reverse else c

    @pl.when(c == 0)
    def _():
        if has_h0:
            h_scr[...] = h0_ref[...]
        else:
            h_scr[...] = jnp.zeros_like(h_scr)

    row = lax.broadcasted_iota(jnp.int32, (L, LANES), 0)
    lane = lax.broadcasted_iota(jnp.int32, (L, LANES), 1)
    lo = lane < SSD_HEAD_DIM
    top = row < SSD_HEAD_DIM

    if conv:
        first = cidx == 0
        last = cidx == nc - 1

        def cols(a, w):
            xm = xbc_ref[:, a:a + w].astype(F32)
            rw = lax.broadcasted_iota(jnp.int32, (L, w), 0)
            p6 = jnp.where(first, 0.0, prev_ref[6:7, a:a + w].astype(F32))
            p7 = jnp.where(first, 0.0, prev_ref[7:8, a:a + w].astype(F32))
            n0 = jnp.where(last, 0.0, next_ref[0:1, a:a + w].astype(F32))
            r1 = jnp.where(rw == 0, p7, pltpu.roll(xm, 1, 0))
            r2 = jnp.where(rw == 0, p6, jnp.where(rw == 1, p7, pltpu.roll(xm, 2, 0)))
            rn = jnp.where(rw == L - 1, n0, pltpu.roll(xm, L - 1, 0))
            y = (r2 * cw_ref[0:1, a:a + w] + r1 * cw_ref[1:2, a:a + w] + xm * cw_ref[2:3, a:a + w]
                 + rn * cw_ref[3:4, a:a + w] + cb_ref[:, a:a + w])
            y = _silu(y).astype(BF16)
            xco_ref[:, a:a + w] = y
            return y
    else:
        def cols(a, w):
            return xc_ref[:, a:a + w]

    causal = tri_ref[...] > 0.0
    dt = _softplus(dt_ref[...] + dtb_ref[...])
    la2 = dt * (an_ref[...] * LOG2E)
    acum2 = jnp.dot(tri_ref[...], la2, preferred_element_type=F32, precision=lax.Precision.HIGHEST)
    dt_t = dt.T
    acum2_t = acum2.T
    tot2_t = jnp.sum(la2.T, axis=1, keepdims=True)
    lg_dt_t = jnp.log2(dt_t)
    r_t = lg_dt_t - acum2_t
    w_t = jnp.exp2(lg_dt_t + tot2_t - acum2_t)
    e_acum = jnp.exp2(acum2)
    e_tot_t = jnp.exp2(tot2_t)

    for g in range(SSD_GROUPS):
        bgb = cols(D_INNER + g * D_STATE, D_STATE)
        cgb = cols(D_INNER + SSD_GROUPS * D_STATE + g * D_STATE, D_STATE)
        cbm = _bdot_nt(cgb, bgb)
        for pr in range(4):
            hp = g * 4 + pr
            ha, hb = 2 * hp, 2 * hp + 1
            xpb = cols(hp * LANES, LANES)
            ys = []
            for hh in (ha, hb):
                e = jnp.exp2(acum2[:, hh:hh + 1] + r_t[hh:hh + 1, :])
                m = cbm * jnp.where(causal, e, 0.0)
                ys.append(jnp.dot(m.astype(BF16), xpb, preferred_element_type=F32))
            y = jnp.where(lo, ys[0], ys[1])
            h_old = h_scr[hp]
            yo = lax.dot_general(cgb, h_old.astype(BF16), (((1,), (1,)), ((), ())),
                                 preferred_element_type=F32)
            y = y + yo * jnp.where(lo, e_acum[:, ha:ha + 1], e_acum[:, hb:hb + 1])
            if conv:
                y = y + dsk_ref[:, hp * LANES:(hp + 1) * LANES] * xpb.astype(F32)
            y_ref[:, hp * LANES:(hp + 1) * LANES] = y.astype(y_ref.dtype)
            wsel = jnp.where(top, w_t[ha:ha + 1, :], w_t[hb:hb + 1, :])
            st = jnp.dot((xpb.astype(F32).T * wsel).astype(BF16), bgb, preferred_element_type=F32)
            cd = jnp.where(top, e_tot_t[ha:ha + 1, :], e_tot_t[hb:hb + 1, :])
            h_scr[hp] = h_old * cd + st

    if want_hfin:
        @pl.when(c == nc - 1)
        def _():
            hfin_ref[...] = h_scr[...]


def _ssd_sweep(xin, dt, wts, h0, hprev, *, reverse, want_hfin):
    b, t, _ = xin.shape
    nc = t // CHUNK
    has_h0 = h0 is not None
    rb = CHUNK // 8
    nrb = t // 8
    d = 1 if reverse else 0
    cmap = (lambda c: nc - 1 - c) if reverse else (lambda c: c)
    hshape = (SSD_HEADS // 2, 2 * SSD_HEAD_DIM, D_STATE)

    chunk_spec = pl.BlockSpec((None, CHUNK, CONV_CH), lambda bi, c: (bi, cmap(c), 0))
    if reverse:
        in_specs = [chunk_spec]
        args = [xin]
    else:
        in_specs = [
            chunk_spec,
            pl.BlockSpec((None, 8, CONV_CH), lambda bi, c: (bi, jnp.maximum(c * rb - 1, 0), 0)),
            pl.BlockSpec((None, 8, CONV_CH), lambda bi, c: (bi, jnp.minimum((c + 1) * rb, nrb - 1), 0)),
            pl.BlockSpec((D_CONV, CONV_CH), lambda bi, c: (0, 0)),
            pl.BlockSpec((1, CONV_CH), lambda bi, c: (0, 0)),
            pl.BlockSpec((1, D_INNER), lambda bi, c: (0, 0)),
        ]
        args = [xin, xin, xin, wts["conv_w"], wts["conv_b"], wts["dskip"]]
    in_specs += [
        pl.BlockSpec((None, CHUNK, LANES), lambda bi, c: (bi, cmap(c), d)),
        pl.BlockSpec((None, 1, LANES), lambda bi, c: (d, 0, 0)),
        pl.BlockSpec((None, 1, LANES), lambda bi, c: (d, 0, 0)),
        pl.BlockSpec((None, CHUNK, CHUNK), lambda bi, c: (d, 0, 0)),
    ]
    args += [dt, wts["a_neg"], wts["dt_bias"], wts["tri"]]
    if has_h0:
        in_specs.append(pl.BlockSpec((None, None) + hshape, lambda bi, c: (bi, d, 0, 0, 0)))
        args.append(h0)
    aliases = {}
    if want_hfin and reverse:
        aliases = {len(args): 1}
        in_specs.append(pl.BlockSpec(memory_space=pl.ANY))
        args.append(hprev)
    out_shape = [jax.ShapeDtypeStruct((b, t, D_INNER), BF16)]
    out_specs = [pl.BlockSpec((None, CHUNK, D_INNER), lambda bi, c: (bi, cmap(c), 0))]
    if not reverse:
        out_shape.append(jax.ShapeDtypeStruct((b, t, CONV_CH), BF16))
        out_specs.append(pl.BlockSpec((None, CHUNK, CONV_CH), lambda bi, c: (bi, c, 0)))
    if want_hfin:
        out_shape.append(jax.ShapeDtypeStruct((b, 2) + hshape, F32))
        out_specs.append(pl.BlockSpec((None, None) + hshape, lambda bi, c: (bi, d, 0, 0, 0)))
    return pl.pallas_call(
        functools.partial(_ssd_kernel, nc=nc, reverse=reverse, has_h0=has_h0, want_hfin=want_hfin),
        out_shape=tuple(out_shape),
        grid=(b, nc),
        in_specs=in_specs,
        out_specs=tuple(out_specs),
        scratch_shapes=[pltpu.VMEM(hshape, F32)],
        input_output_aliases=aliases,
        compiler_params=_cparams(("arbitrary", "arbitrary")),
        name="ssd_bwd" if reverse else "ssd_fwd",
    )(*args)


def _ssd_call(xbc, dt, wts, h0, *, want_hfin):
    res = _ssd_sweep(xbc, dt, wts, h0, None, reverse=False, want_hfin=want_hfin)
    y_f, xc = res[0], res[1]
    hf = res[2] if want_hfin else None
    res = _ssd_sweep(xc, dt, wts, h0, hf, reverse=True, want_hfin=want_hfin)
    return y_f, res[0], (res[1] if want_hfin else None)


def _route(logits_t, bias_col):
    e, n = logits_t.shape
    per = e // N_EXPERT_GROUPS
    scores = jax.nn.sigmoid(logits_t)
    sel = scores + bias_col
    neg = jnp.float32(-jnp.inf)
    gs = []
    for g in range(N_EXPERT_GROUPS):
        blk = sel[g * per:(g + 1) * per, :]
        m1 = jnp.max(blk, axis=0, keepdims=True)
        is_m1 = blk == m1
        cnt = jnp.sum(jnp.where(is_m1, 1.0, 0.0), axis=0, keepdims=True)
        m2 = jnp.max(jnp.where(is_m1, neg, blk), axis=0, keepdims=True)
        gs.append(m1 + jnp.where(cnt >= 2.0, m1, m2))
    keep = []
    for g in range(N_EXPERT_GROUPS):
        rank = jnp.zeros_like(gs[g])
        for j in range(N_EXPERT_GROUPS):
            if j == g:
                continue
            beats = (gs[j] > gs[g]) if j > g else (gs[j] >= gs[g])
            rank = rank + jnp.where(beats, 1.0, 0.0)
        keep.append(rank < float(TOPK_GROUPS))
    selm = jnp.concatenate(
        [jnp.where(keep[g], sel[g * per:(g + 1) * per, :], neg) for g in range(N_EXPERT_GROUPS)], axis=0)
    eidx = lax.broadcasted_iota(jnp.int32, (e, n), 0)
    rank = jnp.zeros((e, n), F32)
    for j in range(e):
        rj = selm[j:j + 1, :]
        tie = jnp.where(eidx > j, 1.0, 0.0)
        rank = rank + jnp.where(rj > selm, 1.0, jnp.where(rj == selm, tie, 0.0))
    w = jnp.where(rank < float(TOP_K), scores, 0.0)
    wsum = jnp.sum(w, axis=0, keepdims=True)
    return w / wsum * ROUTED_SCALE


def _merge_kernel(x_ref, attn_ref, yf_ref, yb_ref, z_ref, gates_ref, mod_ref, wa_ref, ws_ref, wo_ref,
                  sg_ref, n2_ref, wr_ref, rb_ref, wsg_ref, wsu_ref, wsd_ref,
                  xb_ref, h2_ref, comb_ref):
    x = x_ref[...]
    yy = yf_ref[...].astype(F32) + yb_ref[...].astype(F32)
    u = yy * _silu(z_ref[...].astype(F32))
    un = u * lax.rsqrt(jnp.mean(u * u, axis=-1, keepdims=True) + EPS) * sg_ref[...]
    ssd_o = _bdot(un, ws_ref[...])
    attn_o = jnp.dot(attn_ref[...], wa_ref[...], preferred_element_type=F32)
    ga = jax.nn.sigmoid(gates_ref[:, 0:D_MODEL].astype(F32))
    gs = jax.nn.sigmoid(gates_ref[:, D_MODEL:2 * D_MODEL].astype(F32))
    mix = _bdot(ga * attn_o + gs * ssd_o, wo_ref[...])
    x1 = x + mod_ref[:, 2048:3072] * mix
    h2 = x1 * lax.rsqrt(jnp.mean(x1 * x1, axis=-1, keepdims=True) + EPS) * n2_ref[...]
    h2 = h2 * (1.0 + mod_ref[:, 4096:5120]) + mod_ref[:, 3072:4096]
    h2b = h2.astype(BF16)
    h2_ref[...] = h2b
    logits_t = _bdot_nt(wr_ref[...], h2b)
    comb_t = _route(logits_t, rb_ref[...])
    comb_t = jnp.concatenate([comb_t, jnp.zeros_like(comb_t)], axis=0)
    comb_ref[...] = comb_t.T
    hid = _silu(jnp.dot(h2b, wsg_ref[...], preferred_element_type=F32)) * \
        jnp.dot(h2b, wsu_ref[...], preferred_element_type=F32)
    xb_ref[...] = x1 + mod_ref[:, 5120:6144] * _bdot(hid, wsd_ref[...])


def _merge_call(x, attn, y_f, y_b, z, gates, mod_rows, wa, ws, wo, sg, n2, wr_t, rb, wsg, wsu, wsd, *, tm):
    b, t, _ = x.shape
    nt = t // tm
    tok = lambda width: pl.BlockSpec((None, tm, width), lambda bi, i: (bi, i, 0))
    const2 = lambda shape: pl.BlockSpec(shape, lambda bi, i: (0, 0))
    return pl.pallas_call(
        _merge_kernel,
        out_shape=(jax.ShapeDtypeStruct((b, t, D_MODEL), F32),
                   jax.ShapeDtypeStruct((b, t, D_MODEL), BF16),
                   jax.ShapeDtypeStruct((b, t, LANES), F32)),
        grid=(b, nt),
        in_specs=[tok(D_MODEL), tok(ATTN_W),
                  tok(D_INNER), tok(D_INNER),
                  tok(D_INNER), tok(2 * D_MODEL),
                  pl.BlockSpec((None, 1, 6 * D_MODEL), lambda bi, i: (bi, 0, 0)),
                  const2((ATTN_W, D_MODEL)), const2((D_INNER, D_MODEL)), const2((D_MODEL, D_MODEL)),
                  const2((1, D_INNER)), const2((1, D_MODEL)),
                  const2((N_EXPERTS, D_MODEL)), const2((N_EXPERTS, 1)),
                  const2((D_MODEL, D_SHARED)), const2((D_MODEL, D_SHARED)), const2((D_SHARED, D_MODEL))],
        out_specs=(tok(D_MODEL), tok(D_MODEL), tok(LANES)),
        compiler_params=_cparams(("arbitrary", "arbitrary")),
        name="merge",
    )(x, attn, y_f, y_b, z, gates, mod_rows, wa, ws, wo, sg, n2, wr_t, rb, wsg, wsu, wsd)


def _moe_kernel(h2_ref, comb_ref, mod_ref, wg_ref, wu_ref, wd_ref, xb_ref, o_ref, acc_ref, *, eb, neb):
    j = pl.program_id(1)

    @pl.when(j == 0)
    def _():
        acc_ref[...] = jnp.zeros_like(acc_ref)

    h2b = h2_ref[...]
    comb = comb_ref[...]
    lane = lax.broadcasted_iota(jnp.int32, comb.shape, 1)
    tot = jnp.zeros(acc_ref.shape, F32)
    for e in range(eb):
        hid = _silu(jnp.dot(h2b, wg_ref[e], preferred_element_type=F32)) * \
            jnp.dot(h2b, wu_ref[e], preferred_element_type=F32)
        cw = jnp.sum(jnp.where(lane == j * eb + e, comb, 0.0), axis=-1, keepdims=True)
        tot = tot + _bdot(hid, wd_ref[e]) * cw
    acc_ref[...] += tot

    @pl.when(j == neb - 1)
    def _():
        o_ref[...] = xb_ref[...] + mod_ref[:, 5120:6144] * acc_ref[...]


def _moe_call(h2, comb, mod_rows, wg, wu, wd, xb, *, tm, eb):
    b, t, _ = h2.shape
    n = b * t
    assert n % tm == 0 and (tm % t == 0 or t % tm == 0)
    neb = N_EXPERTS // eb
    tok = lambda width: pl.BlockSpec((tm, width), lambda i, j: (i, 0))
    wspec = lambda s: pl.BlockSpec((eb,) + s, lambda i, j: (j, 0, 0))
    out = pl.pallas_call(
        functools.partial(_moe_kernel, eb=eb, neb=neb),
        out_shape=jax.ShapeDtypeStruct((n, D_MODEL), F32),
        grid=(n // tm, neb),
        in_specs=[tok(D_MODEL), tok(LANES),
                  pl.BlockSpec((None, 1, 6 * D_MODEL), lambda i, j: ((i * tm) // t, 0, 0)),
                  wspec((D_MODEL, D_EXPERT)), wspec((D_MODEL, D_EXPERT)), wspec((D_EXPERT, D_MODEL)),
                  tok(D_MODEL)],
        out_specs=tok(D_MODEL),
        scratch_shapes=[pltpu.VMEM((tm, D_MODEL), F32)],
        compiler_params=_cparams(("arbitrary", "arbitrary")),
        name="moe",
    )(h2.reshape(n, D_MODEL), comb.reshape(n, LANES), mod_rows, wg, wu, wd, xb.reshape(n, D_MODEL))
    return out.reshape(b, t, D_MODEL)


def _rope_tables(t):
    n_rows = t // GRID_W
    rows = jnp.repeat(jnp.arange(n_rows), GRID_W).astype(F32)
    cols = jnp.tile(jnp.arange(GRID_W), n_rows).astype(F32)
    n_freq = HEAD_DIM // 4
    freqs = ROPE_THETA ** (-jnp.arange(n_freq, dtype=F32) / n_freq)
    ang = jnp.concatenate([rows[:, None] * freqs, cols[:, None] * freqs], axis=-1)
    ang = jnp.repeat(ang, 2, axis=-1)
    ang = jnp.concatenate([ang, ang], axis=-1)
    sign = jnp.where(jnp.arange(LANES) % 2 == 0, -1.0, 1.0).astype(F32)
    return jnp.cos(ang), jnp.sin(ang) * sign


def _dup_heads(a):
    s = a.shape[:-1]
    a4 = a.reshape(s + (N_KV_HEADS, HEAD_DIM))
    return jnp.concatenate([a4, a4], axis=-1).reshape(s + (KVD_W,))


def _prep_w_in(w_in):
    idx = np.cumsum(SPLIT_SIZES)[:-1].tolist()
    q, k, v, z, xbc, dt, gates = jnp.split(w_in, idx, axis=-1)
    pad = jnp.zeros((D_MODEL, LANES - SSD_HEADS), w_in.dtype)
    cols = [q, _dup_heads(k), _dup_heads(v), gates, z, xbc,
            dt[:, :SSD_HEADS], pad, dt[:, SSD_HEADS:], pad]
    return jnp.concatenate(cols, axis=-1).astype(BF16)


def _pad_heads(a):
    return jnp.pad(a.astype(F32), ((0, 0), (0, LANES - SSD_HEADS)))[:, None, :]


def _trunk(x, mod_rows, wts, rope_tabs, ctx_k, ctx_v, h0, *, tm, tq, want_state):
    b, t, _ = x.shape
    rope = rope_tabs is not None
    if rope:
        cos, sin = rope_tabs
    else:
        cos = sin = jnp.zeros((t, LANES), F32)
    kv_dtype = BF16 if rope else F32
    q, k, v, gates, z, xbc, dt = _inproj_call(x, mod_rows, wts["g1"], wts["w_in"], wts["qg"], wts["kg"],
                                              cos, sin, rope=rope, kv_dtype=kv_dtype, tm=tm)
    if ctx_k is not None:
        k_all = jnp.concatenate([k, ctx_k], axis=1)
        v_all = jnp.concatenate([v, ctx_v], axis=1)
    else:
        k_all, v_all = k, v
    attn = _attn_call(q, k_all, v_all, tq=tq)
    y_f, y_b, hfin = _ssd_call(xbc, dt, wts, h0, want_hfin=want_state)
    xb, h2, comb = _merge_call(x, attn, y_f, y_b, z, gates, mod_rows, wts["wa"], wts["ws"], wts["wo"], wts["sg"],
                               wts["n2"], wts["wr_t"], wts["rb"], wts["wsg"], wts["wsu"], wts["wsd"], tm=tm)
    out = _moe_call(h2, comb, mod_rows, wts["weg"], wts["weu"], wts["wed"], xb, tm=1024, eb=4)
    return out, k, v, hfin


def kernel(x_prompt, x_sample, cache_k, cache_v, state_ssm, c, c_ctx, w_mod, b_mod, norm1_g, norm2_g, w_in,
           q_norm_g, k_norm_g, conv_w, conv_b, a_log, dt_bias, d_skip, ssd_norm_g, w_attn_proj, w_ssd_proj,
           w_out, w_router, router_bias, w_exp_gate, w_exp_up, w_exp_down, w_sh_gate, w_sh_up, w_sh_down):
    depth = w_mod.shape[0]
    assert depth == 1, "single trunk layer"
    bp, tp, _ = x_prompt.shape
    bs, ts, _ = x_sample.shape
    l = 0
    cvec = jnp.concatenate([c_ctx[None, :], c, jnp.zeros((8 - 1 - bs, D_MODEL), F32)], axis=0)
    mod = _mod_call(cvec, w_mod[l], b_mod[l][None, :])
    mod_prompt = jnp.broadcast_to(mod[0:1][:, None, :], (bp, 1, 6 * D_MODEL))
    mod_sample = mod[1:1 + bs][:, None, :]

    lower = np.tril(np.ones((CHUNK, CHUNK), np.float32))
    wts = dict(
        g1=norm1_g[l][None, :], n2=norm2_g[l][None, :],
        w_in=_prep_w_in(w_in[l]),
        qg=jnp.tile(q_norm_g[l], 2)[None, :], kg=jnp.tile(k_norm_g[l], 2)[None, :],
        conv_w=conv_w[l], conv_b=conv_b[l][None, :],
        a_neg=_pad_heads(-jnp.exp(a_log[l].astype(F32))), dt_bias=_pad_heads(dt_bias[l]),
        dskip=jnp.repeat(d_skip[l].astype(F32), SSD_HEAD_DIM)[None, :],
        tri=jnp.asarray(np.stack([lower, lower.T])),
        sg=ssd_norm_g[l][None, :],
        wa=w_attn_proj[l].astype(BF16), ws=w_ssd_proj[l].astype(BF16), wo=w_out[l].astype(BF16),
        wr_t=w_router[l].T.astype(BF16), rb=router_bias[l].astype(F32)[:, None],
        wsg=w_sh_gate[l].astype(BF16), wsu=w_sh_up[l].astype(BF16), wsd=w_sh_down[l].astype(BF16),
        weg=w_exp_gate[l].astype(BF16), weu=w_exp_up[l].astype(BF16), wed=w_exp_down[l].astype(BF16),
    )

    y_prompt, k_p, v_p, hfin = _trunk(x_prompt, mod_prompt, wts, None, None, None, None,
                                      tm=256, tq=256, want_state=True)
    new_k = k_p.reshape(bp, tp, N_KV_HEADS, LANES)[..., :HEAD_DIM][:, None]
    new_v = v_p.reshape(bp, tp, N_KV_HEADS, LANES)[..., :HEAD_DIM][:, None]
    new_state = hfin.reshape(bp, 1, 2, SSD_HEADS, SSD_HEAD_DIM, D_STATE)

    past = cache_k.shape[2]
    ctx_k = _dup_heads(cache_k[:, l].reshape(bs, past, KV_W)).astype(BF16)
    ctx_v = _dup_heads(cache_v[:, l].reshape(bs, past, KV_W)).astype(BF16)
    h0 = state_ssm[:, l].reshape(bs, 2, SSD_HEADS // 2, 2 * SSD_HEAD_DIM, D_STATE)
    y_sample, _, _, _ = _trunk(x_sample, mod_sample, wts, _rope_tables(ts), ctx_k, ctx_v, h0,
                               tm=256, tq=256, want_state=False)
    return (y_prompt, y_sample, new_k, new_v, new_state)
```

```python
import functools

import numpy as np
import jax
import jax.numpy as jnp
from jax import lax
from jax.experimental import pallas as pl
from jax.experimental.pallas import tpu as pltpu
from jax.experimental.pallas import tpu_sc as plsc

F32 = jnp.float32
BF16 = jnp.bfloat16

D_MODEL = 1024
GRID_W = 64
EPS = 1e-6
N_HEADS = 16
N_KV_HEADS = 4
HEAD_DIM = 64
ATTN_W = N_HEADS * HEAD_DIM
KV_W = N_KV_HEADS * HEAD_DIM
ROPE_THETA = 10000.0
D_INNER = 2048
SSD_HEAD_DIM = 64
SSD_HEADS = 32
SSD_GROUPS = 4
D_STATE = 128
D_CONV = 4
CHUNK = 128
CONV_CH = D_INNER + 2 * SSD_GROUPS * D_STATE
N_EXPERTS = 64
TOP_K = 8
N_EXPERT_GROUPS = 8
TOPK_GROUPS = 4
D_EXPERT = 256
D_SHARED = 256
ROUTED_SCALE = 2.5
SPLIT_SIZES = (ATTN_W, KV_W, KV_W, D_INNER, CONV_CH, 2 * SSD_HEADS, 2 * D_MODEL)

LANES = 128
KVD_W = N_KV_HEADS * LANES
C_Q, C_K, C_V, C_G, C_Z, C_X, C_DT, C_END = 0, 1024, 1536, 2048, 4096, 6144, 9216, 9472
VMEM_LIMIT = 56 * 1024 * 1024
Q_SCALE = HEAD_DIM ** -0.5 * 1.4426950408889634


def _cparams(sem):
    return pltpu.CompilerParams(dimension_semantics=sem, vmem_limit_bytes=VMEM_LIMIT)


def _silu(x):
    return x * jax.nn.sigmoid(x)


def _bdot(a, b):
    return jnp.dot(a.astype(BF16), b.astype(BF16), preferred_element_type=F32)


def _bdot_nt(a, b):
    return lax.dot_general(a.astype(BF16), b.astype(BF16), (((1,), (1,)), ((), ())),
                           preferred_element_type=F32)


def _mod_kernel(c_ref, w_ref, b_ref, o_ref):
    o_ref[...] = _bdot(_silu(c_ref[...]), w_ref[...]) + b_ref[...]


def _mod_call(cvec, w_mod, b_mod):
    n = w_mod.shape[1]
    bn = 1024
    return pl.pallas_call(
        _mod_kernel,
        out_shape=jax.ShapeDtypeStruct((8, n), F32),
        grid=(n // bn,),
        in_specs=[pl.BlockSpec((8, D_MODEL), lambda j: (0, 0)),
                  pl.BlockSpec((D_MODEL, bn), lambda j: (0, j)),
                  pl.BlockSpec((1, bn), lambda j: (0, j))],
        out_specs=pl.BlockSpec((8, bn), lambda j: (0, j)),
        compiler_params=_cparams(("arbitrary",)),
        name="mod",
    )(cvec, w_mod, b_mod)


def _inproj_kernel(x_ref, mod_ref, g1_ref, w_ref, qg_ref, kg_ref, cos_ref, sin_ref,
                   q_ref, k_ref, v_ref, gates_ref, z_ref, xbc_ref, dt_ref, *, rope):
    tm = x_ref.shape[0]
    x = x_ref[...]
    inv = lax.rsqrt(jnp.mean(x * x, axis=-1, keepdims=True) + EPS)
    h = (x * inv) * g1_ref[...]
    h = h * (1.0 + mod_ref[:, 1024:2048]) + mod_ref[:, 0:1024]
    hb = h.astype(BF16)

    lane = lax.broadcasted_iota(jnp.int32, (tm, LANES), 1)
    lo = lane < HEAD_DIM
    even = (lane & 1) == 0
    if rope:
        cos = cos_ref[...]
        sin = sin_ref[...]

    def rope_fn(blk):
        nxt = pltpu.roll(blk, LANES - 1, 1)
        prv = pltpu.roll(blk, 1, 1)
        return blk * cos + jnp.where(even, nxt, prv) * sin

    qg = qg_ref[...]
    kg = kg_ref[...]
    q = jnp.dot(hb, w_ref[:, C_Q:C_K], preferred_element_type=F32)
    for j in range(ATTN_W // LANES):
        blk = q[:, j * LANES:(j + 1) * LANES]
        sq = blk * blk
        s_all = jnp.sum(sq, axis=-1, keepdims=True)
        s_lo = jnp.sum(jnp.where(lo, sq, 0.0), axis=-1, keepdims=True)
        ms = jnp.where(lo, s_lo, s_all - s_lo) * (1.0 / HEAD_DIM)
        blk = blk * lax.rsqrt(ms + EPS) * qg
        if rope:
            blk = rope_fn(blk)
        q_ref[:, j * LANES:(j + 1) * LANES] = (blk * Q_SCALE).astype(q_ref.dtype)

    k = jnp.dot(hb, w_ref[:, C_K:C_V], preferred_element_type=F32)
    for j in range(N_KV_HEADS):
        blk = k[:, j * LANES:(j + 1) * LANES]
        ms = jnp.mean(blk * blk, axis=-1, keepdims=True)
        blk = blk * lax.rsqrt(ms + EPS) * kg
        if rope:
            blk = rope_fn(blk)
        k_ref[:, j * LANES:(j + 1) * LANES] = blk.astype(k_ref.dtype)

    v_ref[...] = jnp.dot(hb, w_ref[:, C_V:C_G], preferred_element_type=F32).astype(v_ref.dtype)
    gates_ref[...] = jnp.dot(hb, w_ref[:, C_G:C_Z], preferred_element_type=F32).astype(gates_ref.dtype)
    z_ref[...] = jnp.dot(hb, w_ref[:, C_Z:C_X], preferred_element_type=F32).astype(z_ref.dtype)
    xbc_ref[...] = jnp.dot(hb, w_ref[:, C_X:C_DT], preferred_element_type=F32).astype(xbc_ref.dtype)
    dt_ref[...] = jnp.dot(hb, w_ref[:, C_DT:C_END], preferred_element_type=F32)


def _inproj_call(x, mod_rows, g1, w, qg, kg, cos, sin, *, rope, kv_dtype, tm):
    b, t, _ = x.shape
    nt = t // tm
    tok = lambda width: pl.BlockSpec((None, tm, width), lambda bi, i: (bi, i, 0))
    const2 = lambda shape: pl.BlockSpec(shape, lambda bi, i: (0, 0))
    out_shape = (
        jax.ShapeDtypeStruct((b, t, ATTN_W), BF16),
        jax.ShapeDtypeStruct((b, t, KVD_W), kv_dtype),
        jax.ShapeDtypeStruct((b, t, KVD_W), kv_dtype),
        jax.ShapeDtypeStruct((b, t, 2 * D_MODEL), BF16),
        jax.ShapeDtypeStruct((b, t, D_INNER), BF16),
        jax.ShapeDtypeStruct((b, t, CONV_CH), BF16),
        jax.ShapeDtypeStruct((b, t, 2 * LANES), F32),
    )
    return pl.pallas_call(
        functools.partial(_inproj_kernel, rope=rope),
        out_shape=out_shape,
        grid=(b, nt),
        in_specs=[tok(D_MODEL),
                  pl.BlockSpec((None, 1, 6 * D_MODEL), lambda bi, i: (bi, 0, 0)),
                  const2((1, D_MODEL)),
                  pl.BlockSpec((D_MODEL, C_END), lambda bi, i: (0, 0), pipeline_mode=pl.Buffered(1)),
                  const2((1, LANES)), const2((1, LANES)),
                  pl.BlockSpec((tm, LANES), lambda bi, i: (i, 0)),
                  pl.BlockSpec((tm, LANES), lambda bi, i: (i, 0))],
        out_specs=(tok(ATTN_W), tok(KVD_W), tok(KVD_W), tok(2 * D_MODEL), tok(D_INNER),
                   tok(CONV_CH), tok(2 * LANES)),
        compiler_params=_cparams(("arbitrary", "arbitrary")),
        name="inproj",
    )(x, mod_rows, g1, w, qg, kg, cos, sin)


def _attn_kernel(q_ref, k_ref, v_ref, o_ref, *, kc):
    tq = q_ref.shape[0]
    tk = k_ref.shape[0]
    lane = lax.broadcasted_iota(jnp.int32, (tq, LANES), 1)
    lo = lane < HEAD_DIM
    qs = []
    for j in range(2):
        q2 = q_ref[:, j * LANES:(j + 1) * LANES]
        zero = jnp.zeros_like(q2)
        qs += [jnp.where(lo, q2, zero), jnp.where(lo, zero, q2)]
    q4 = jnp.concatenate(qs, axis=0)
    rows = 4 * tq
    lane_k = lax.broadcasted_iota(jnp.int32, (kc, LANES), 1)
    m = jnp.full((rows, 1), -jnp.inf, F32)
    acc = jnp.zeros((rows, LANES), F32)
    for c in range(tk // kc):
        kch = k_ref[c * kc:(c + 1) * kc, :].astype(BF16)
        vch = v_ref[c * kc:(c + 1) * kc, :].astype(BF16)
        vch = jnp.where(lane_k < HEAD_DIM, vch, jnp.ones_like(vch))
        s = _bdot_nt(q4, kch)
        m_new = jnp.maximum(m, jnp.max(s, axis=-1, keepdims=True))
        alpha = jnp.exp2(m - m_new)
        p = jnp.exp2((s - m_new).astype(BF16))
        acc = acc * alpha + jnp.dot(p, vch, preferred_element_type=F32)
        m = m_new
    o = acc * (1.0 / pltpu.roll(acc, HEAD_DIM, 1))
    for j in range(2):
        oa = o[(2 * j) * tq:(2 * j + 1) * tq]
        ob = pltpu.roll(o[(2 * j + 1) * tq:(2 * j + 2) * tq], HEAD_DIM, 1)
        o_ref[:, j * LANES:(j + 1) * LANES] = jnp.where(lo, oa, ob).astype(o_ref.dtype)


def _attn_call(q, k, v, *, tq):
    b, t, _ = q.shape
    tk = k.shape[1]
    nq = t // tq
    kc = 512 if tk % 512 == 0 else tk
    return pl.pallas_call(
        functools.partial(_attn_kernel, kc=kc),
        out_shape=jax.ShapeDtypeStruct((b, t, ATTN_W), BF16),
        grid=(b, N_KV_HEADS, nq),
        in_specs=[pl.BlockSpec((None, tq, 2 * LANES), lambda bi, g, i: (bi, i, g)),
                  pl.BlockSpec((None, tk, LANES), lambda bi, g, i: (bi, 0, g)),
                  pl.BlockSpec((None, tk, LANES), lambda bi, g, i: (bi, 0, g))],
        out_specs=pl.BlockSpec((None, tq, 2 * LANES), lambda bi, g, i: (bi, i, g)),
        compiler_params=_cparams(("arbitrary", "arbitrary", "arbitrary")),
        name="attn",
    )(q, k, v)


LOG2E = 1.4426950408889634


def _softplus(x):
    return jnp.maximum(x, 0.0) + jnp.log(1.0 + jnp.exp(-jnp.abs(x)))


def _ssd_kernel(*refs, nc, reverse, has_h0, want_hfin):
    refs = list(refs)
    conv = not reverse
    if conv:
        xbc_ref, prev_ref, next_ref, cw_ref, cb_ref, dsk_ref = refs[:6]
        refs = refs[6:]
    else:
        xc_ref = refs.pop(0)
    dt_ref, an_ref, dtb_ref, tri_ref = refs[:4]
    refs = refs[4:]
    h0_ref = refs.pop(0) if has_h0 else None
    hprev_ref = refs.pop(0) if (want_hfin and reverse) else None
    y_ref = refs.pop(0)
    xco_ref = refs.pop(0) if conv else None
    hfin_ref = refs.pop(0) if want_hfin else None
    h_scr = refs.pop(0)

    L = CHUNK
    c = pl.program_id(1)
    cidx = (nc - 1 - c) if reverse else c

    @pl.when(c == 0)
    def _():
        if has_h0:
            h_scr[...] = h0_ref[...]
        else:
            h_scr[...] = jnp.zeros_like(h_scr)

    row = lax.broadcasted_iota(jnp.int32, (L, LANES), 0)
    lane = lax.broadcasted_iota(jnp.int32, (L, LANES), 1)
    lo = lane < SSD_HEAD_DIM
    top = row < SSD_HEAD_DIM

    if conv:
        first = cidx == 0
        last = cidx == nc - 1

        def cols(a, w):
            xm = xbc_ref[:, a:a + w].astype(F32)
            rw = lax.broadcasted_iota(jnp.int32, (L, w), 0)
            p6 = jnp.where(first, 0.0, prev_ref[6:7, a:a + w].astype(F32))
            p7 = jnp.where(first, 0.0, prev_ref[7:8, a:a + w].astype(F32))
            n0 = jnp.where(last, 0.0, next_ref[0:1, a:a + w].astype(F32))
            r1 = jnp.where(rw == 0, p7, pltpu.roll(xm, 1, 0))
            r2 = jnp.where(rw == 0, p6, jnp.where(rw == 1, p7, pltpu.roll(xm, 2, 0)))
            rn = jnp.where(rw == L - 1, n0, pltpu.roll(xm, L - 1, 0))
            y = (r2 * cw_ref[0:1, a:a + w] + r1 * cw_ref[1:2, a:a + w] + xm * cw_ref[2:3, a:a + w]
                 + rn * cw_ref[3:4, a:a + w] + cb_ref[:, a:a + w])
            y = _silu(y).astype(BF16)
            xco_ref[:, a:a + w] = y
            return y
    else:
        def cols(a, w):
            return xc_ref[:, a:a + w]

    causal = tri_ref[...] > 0.0
    dt = _softplus(dt_ref[...] + dtb_ref[...])
    la2 = dt * (an_ref[...] * LOG2E)
    acum2 = jnp.dot(tri_ref[...], la2, preferred_element_type=F32, precision=lax.Precision.HIGHEST)
    dt_t = dt.T
    acum2_t = acum2.T
    tot2_t = jnp.sum(la2.T, axis=1, keepdims=True)
    lg_dt_t = jnp.log2(dt_t)
    r_t = lg_dt_t - acum2_t
    w_t = jnp.exp2(lg_dt_t + tot2_t - acum2_t)
    e_acum = jnp.exp2(acum2)
    e_tot_t = jnp.exp2(tot2_t)

    for g in range(SSD_GROUPS):
        bgb = cols(D_INNER + g * D_STATE, D_STATE)
        cgb = cols(D_INNER + SSD_GROUPS * D_STATE + g * D_STATE, D_STATE)
        cbm = _bdot_nt(cgb, bgb)
        h_grp = h_scr[4 * g:4 * g + 4]
        yo_grp = _bdot_nt(cgb, h_grp.reshape(4 * LANES, D_STATE))
        for pr in range(4):
            hp = g * 4 + pr
            ha, hb = 2 * hp, 2 * hp + 1
            xpb = cols(hp * LANES, LANES)
            zero = jnp.zeros_like(xpb)
            xs = jnp.concatenate([jnp.where(lo, xpb, zero), jnp.where(lo, zero, xpb)], axis=0)
            ms = []
            for hh in (ha, hb):
                e = jnp.exp2(acum2[:, hh:hh + 1] + r_t[hh:hh + 1, :])
                ms.append((cbm * jnp.where(causal, e, 0.0)).astype(BF16))
            y = jnp.dot(jnp.concatenate(ms, axis=1), xs, preferred_element_type=F32)
            y = y + yo_grp[:, pr * LANES:(pr + 1) * LANES] * \
                jnp.where(lo, e_acum[:, ha:ha + 1], e_acum[:, hb:hb + 1])
            if conv:
                y = y + dsk_ref[:, hp * LANES:(hp + 1) * LANES] * xpb.astype(F32)
            y_ref[:, hp * LANES:(hp + 1) * LANES] = y.astype(y_ref.dtype)
            wsel = jnp.where(top, w_t[ha:ha + 1, :], w_t[hb:hb + 1, :])
            st = jnp.dot((xpb.astype(F32).T * wsel).astype(BF16), bgb, preferred_element_type=F32)
            cd = jnp.where(top, e_tot_t[ha:ha + 1, :], e_tot_t[hb:hb + 1, :])
            h_scr[hp] = h_grp[pr] * cd + st

    if want_hfin:
        @pl.when(c == nc - 1)
        def _():
            if reverse:
                hfin_ref[0] = hprev_ref[...]
                hfin_ref[1] = h_scr[...]
            else:
                hfin_ref[...] = h_scr[...]


def _ssd_sweep(xin, dt, wts, h0, hprev, *, reverse, want_hfin):
    b, t, _ = xin.shape
    nc = t // CHUNK
    has_h0 = h0 is not None
    rb = CHUNK // 8
    nrb = t // 8
    d = 1 if reverse else 0
    cmap = (lambda c: nc - 1 - c) if reverse else (lambda c: c)
    hshape = (SSD_HEADS // 2, 2 * SSD_HEAD_DIM, D_STATE)

    chunk_spec = pl.BlockSpec((None, CHUNK, CONV_CH), lambda bi, c: (bi, cmap(c), 0))
    if reverse:
        in_specs = [chunk_spec]
        args = [xin]
    else:
        in_specs = [
            chunk_spec,
            pl.BlockSpec((None, 8, CONV_CH), lambda bi, c: (bi, jnp.maximum(c * rb - 1, 0), 0)),
            pl.BlockSpec((None, 8, CONV_CH), lambda bi, c: (bi, jnp.minimum((c + 1) * rb, nrb - 1), 0)),
            pl.BlockSpec((D_CONV, CONV_CH), lambda bi, c: (0, 0)),
            pl.BlockSpec((1, CONV_CH), lambda bi, c: (0, 0)),
            pl.BlockSpec((1, D_INNER), lambda bi, c: (0, 0)),
        ]
        args = [xin, xin, xin, wts["conv_w"], wts["conv_b"], wts["dskip"]]
    in_specs += [
        pl.BlockSpec((None, CHUNK, LANES), lambda bi, c: (bi, cmap(c), d)),
        pl.BlockSpec((None, 1, LANES), lambda bi, c: (d, 0, 0)),
        pl.BlockSpec((None, 1, LANES), lambda bi, c: (d, 0, 0)),
        pl.BlockSpec((None, CHUNK, CHUNK), lambda bi, c: (d, 0, 0)),
    ]
    args += [dt, wts["a_neg"], wts["dt_bias"], wts["tri"]]
    if has_h0:
        in_specs.append(pl.BlockSpec((None, None) + hshape, lambda bi, c: (bi, d, 0, 0, 0)))
        args.append(h0)
    if want_hfin and reverse:
        in_specs.append(pl.BlockSpec((None,) + hshape, lambda bi, c: (bi, 0, 0, 0)))
        args.append(hprev)
    out_shape = [jax.ShapeDtypeStruct((b, t, D_INNER), BF16)]
    out_specs = [pl.BlockSpec((None, CHUNK, D_INNER), lambda bi, c: (bi, cmap(c), 0))]
    if not reverse:
        out_shape.append(jax.ShapeDtypeStruct((b, t, CONV_CH), BF16))
        out_specs.append(pl.BlockSpec((None, CHUNK, CONV_CH), lambda bi, c: (bi, c, 0)))
    if want_hfin and reverse:
        out_shape.append(jax.ShapeDtypeStruct((b, 2) + hshape, F32))
        out_specs.append(pl.BlockSpec((None, 2) + hshape, lambda bi, c: (bi, 0, 0, 0, 0)))
    elif want_hfin:
        out_shape.append(jax.ShapeDtypeStruct((b,) + hshape, F32))
        out_specs.append(pl.BlockSpec((None,) + hshape, lambda bi, c: (bi, 0, 0, 0)))
    return pl.pallas_call(
        functools.partial(_ssd_kernel, nc=nc, reverse=reverse, has_h0=has_h0, want_hfin=want_hfin),
        out_shape=tuple(out_shape),
        grid=(b, nc),
        in_specs=in_specs,
        out_specs=tuple(out_specs),
        scratch_shapes=[pltpu.VMEM(hshape, F32)],
        compiler_params=_cparams(("arbitrary", "arbitrary")),
        name="ssd_bwd" if reverse else "ssd_fwd",
    )(*args)


def _ssd_call(xbc, dt, wts, h0, *, want_hfin):
    res = _ssd_sweep(xbc, dt, wts, h0, None, reverse=False, want_hfin=want_hfin)
    y_f, xc = res[0], res[1]
    hf = res[2] if want_hfin else None
    res = _ssd_sweep(xc, dt, wts, h0, hf, reverse=True, want_hfin=want_hfin)
    return y_f, res[0], (res[1] if want_hfin else None)


def _route(logits_t, bias_col):
    e, n = logits_t.shape
    per = e // N_EXPERT_GROUPS
    scores = jax.nn.sigmoid(logits_t)
    sel = scores + bias_col
    neg = jnp.float32(-jnp.inf)
    gs = []
    for g in range(N_EXPERT_GROUPS):
        blk = sel[g * per:(g + 1) * per, :]
        m1 = jnp.max(blk, axis=0, keepdims=True)
        is_m1 = blk == m1
        cnt = jnp.sum(jnp.where(is_m1, 1.0, 0.0), axis=0, keepdims=True)
        m2 = jnp.max(jnp.where(is_m1, neg, blk), axis=0, keepdims=True)
        gs.append(m1 + jnp.where(cnt >= 2.0, m1, m2))
    keep = []
    for g in range(N_EXPERT_GROUPS):
        rank = jnp.zeros_like(gs[g])
        for j in range(N_EXPERT_GROUPS):
            if j == g:
                continue
            beats = (gs[j] > gs[g]) if j > g else (gs[j] >= gs[g])
            rank = rank + jnp.where(beats, 1.0, 0.0)
        keep.append(rank < float(TOPK_GROUPS))
    selm = jnp.concatenate(
        [jnp.where(keep[g], sel[g * per:(g + 1) * per, :], neg) for g in range(N_EXPERT_GROUPS)], axis=0)
    eidx = lax.broadcasted_iota(jnp.int32, (e, n), 0)
    rank = jnp.zeros((e, n), F32)
    for j in range(e):
        rj = selm[j:j + 1, :]
        tie = jnp.where(eidx > j, 1.0, 0.0)
        rank = rank + jnp.where(rj > selm, 1.0, jnp.where(rj == selm, tie, 0.0))
    w = jnp.where(rank < float(TOP_K), scores, 0.0)
    wsum = jnp.sum(w, axis=0, keepdims=True)
    return w / wsum * ROUTED_SCALE, rank


def _pack_bf16_pairs(h):
    c = h.shape[1] // 2
    lo = pltpu.bitcast(h[:, :c].astype(BF16).astype(F32), jnp.uint32)
    hi = pltpu.bitcast(h[:, c:].astype(BF16).astype(F32), jnp.uint32)
    return (lo >> 16) | (hi & jnp.uint32(0xFFFF0000))


def _unpack_bf16_pairs(w):
    lo = pltpu.bitcast(w << 16, F32)
    hi = pltpu.bitcast(w & jnp.uint32(0xFFFF0000), F32)
    return lo, hi


def _rows8(rows):
    n = rows[0].shape[1]
    ridx = lax.broadcasted_iota(jnp.int32, (TOP_K, n), 0)
    out = jnp.zeros((TOP_K, n), rows[0].dtype)
    for k, r in enumerate(rows):
        out = jnp.where(ridx == k, r, out)
    return out


def _merge_kernel(x_ref, attn_ref, yf_ref, yb_ref, z_ref, gates_ref, mod_ref, wa_ref, ws_ref, wo_ref,
                  sg_ref, n2_ref, wr_ref, rb_ref, wsg_ref, wsu_ref, wsd_ref,
                  xb_ref, ha_ref, hb_ref, eid_ref, pos_ref, wk_ref, cnt_ref):
    tm = x_ref.shape[0]

    @pl.when(pl.program_id(0) == 0)
    def _():
        cnt_ref[...] = jnp.zeros_like(cnt_ref)

    x = x_ref[...]
    yy = yf_ref[...].astype(F32) + yb_ref[...].astype(F32)
    u = yy * _silu(z_ref[...].astype(F32))
    un = u * lax.rsqrt(jnp.mean(u * u, axis=-1, keepdims=True) + EPS) * sg_ref[...]
    ssd_o = _bdot(un, ws_ref[...])
    attn_o = jnp.dot(attn_ref[...], wa_ref[...], preferred_element_type=F32)
    ga = jax.nn.sigmoid(gates_ref[:, 0:D_MODEL].astype(F32))
    gs = jax.nn.sigmoid(gates_ref[:, D_MODEL:2 * D_MODEL].astype(F32))
    mix = _bdot(ga * attn_o + gs * ssd_o, wo_ref[...])
    x1 = x + mod_ref[:, 2048:3072] * mix
    h2 = x1 * lax.rsqrt(jnp.mean(x1 * x1, axis=-1, keepdims=True) + EPS) * n2_ref[...]
    h2 = h2 * (1.0 + mod_ref[:, 4096:5120]) + mod_ref[:, 3072:4096]
    h2b = h2.astype(BF16)
    ha_ref[...] = _pack_bf16_pairs(h2[:, :D_MODEL // 2])
    hb_ref[...] = _pack_bf16_pairs(h2[:, D_MODEL // 2:])

    logits_t = _bdot_nt(wr_ref[...], h2b)
    comb_t, rank = _route(logits_t, rb_ref[...])
    chosen = jnp.where(rank < float(TOP_K), 1.0, 0.0)
    r_i = lax.broadcasted_iota(jnp.int32, (tm, tm), 0)
    c_i = lax.broadcasted_iota(jnp.int32, (tm, tm), 1)
    before = jnp.where(r_i < c_i, 1.0, 0.0).astype(BF16)
    pos = cnt_ref[:, 0:1] + jnp.dot(chosen.astype(BF16), before, preferred_element_type=F32)
    cnt_ref[...] += jnp.sum(chosen, axis=1, keepdims=True)
    eidx = lax.broadcasted_iota(jnp.int32, rank.shape, 0).astype(F32)
    eids, poss, wks = [], [], []
    for k in range(TOP_K):
        selk = rank == float(k)
        eids.append(jnp.sum(jnp.where(selk, eidx, 0.0), axis=0, keepdims=True))
        poss.append(jnp.sum(jnp.where(selk, pos, 0.0), axis=0, keepdims=True))
        wks.append(jnp.sum(jnp.where(selk, comb_t, 0.0), axis=0, keepdims=True))
    eid_ref[...] = _rows8(eids).astype(jnp.int32)
    pos_ref[...] = _rows8(poss).astype(jnp.int32)
    wk8 = _rows8(wks)
    wk_ref[...] = jnp.concatenate([wk8, jnp.zeros((LANES - TOP_K, tm), F32)], axis=0).T

    hid = _silu(jnp.dot(h2b, wsg_ref[...], preferred_element_type=F32)) * \
        jnp.dot(h2b, wsu_ref[...], preferred_element_type=F32)
    xb_ref[...] = x1 + mod_ref[:, 5120:6144] * _bdot(hid, wsd_ref[...])


def _merge_call(x, attn, y_f, y_b, z, gates, mod_rows, wa, ws, wo, sg, n2, wr_t, rb, wsg, wsu, wsd, *, tm):
    b, t, _ = x.shape
    n = b * t
    flat = lambda a: a.reshape(n, a.shape[-1])
    tok = lambda width: pl.BlockSpec((tm, width), lambda i: (i, 0))
    const2 = lambda shape: pl.BlockSpec(shape, lambda i: (0, 0))
    k8 = pl.BlockSpec((TOP_K, tm), lambda i: (0, i))
    half = D_MODEL // 4
    return pl.pallas_call(
        _merge_kernel,
        out_shape=(jax.ShapeDtypeStruct((n, D_MODEL), F32),
                   jax.ShapeDtypeStruct((n, half), jnp.uint32),
                   jax.ShapeDtypeStruct((n, half), jnp.uint32),
                   jax.ShapeDtypeStruct((TOP_K, n), jnp.int32),
                   jax.ShapeDtypeStruct((TOP_K, n), jnp.int32),
                   jax.ShapeDtypeStruct((n, LANES), F32),
                   jax.ShapeDtypeStruct((N_EXPERTS, LANES), F32)),
        grid=(n // tm,),
        in_specs=[tok(D_MODEL), tok(ATTN_W), tok(D_INNER), tok(D_INNER), tok(D_INNER), tok(2 * D_MODEL),
                  pl.BlockSpec((None, 1, 6 * D_MODEL), lambda i: ((i * tm) // t, 0, 0)),
                  const2((ATTN_W, D_MODEL)), const2((D_INNER, D_MODEL)), const2((D_MODEL, D_MODEL)),
                  const2((1, D_INNER)), const2((1, D_MODEL)),
                  const2((N_EXPERTS, D_MODEL)), const2((N_EXPERTS, 1)),
                  const2((D_MODEL, D_SHARED)), const2((D_MODEL, D_SHARED)), const2((D_SHARED, D_MODEL))],
        out_specs=(tok(D_MODEL), tok(half), tok(half), k8, k8, tok(LANES), const2((N_EXPERTS, LANES))),
        compiler_params=_cparams(("arbitrary",)),
        name="merge",
    )(flat(x), flat(attn), flat(y_f), flat(y_b), flat(z), flat(gates), mod_rows, wa, ws, wo, sg, n2, wr_t, rb,
      wsg, wsu, wsd)


ROW_TILE = 512
SC_WINDOW = 128


def _slots_kernel(start_ref, eid_ref, pos_ref, slot_ref):
    eid = eid_ref[...]
    slot = pos_ref[...]
    for e in range(N_EXPERTS):
        slot = slot + jnp.where(eid == e, start_ref[e], 0)
    slot_ref[...] = slot


def _slots_call(start, eid, pos):
    n = eid.shape[1]
    bn = 2048 if n % 2048 == 0 else n
    spec = pl.BlockSpec((TOP_K, bn), lambda i, s: (0, i))
    return pl.pallas_call(
        _slots_kernel,
        out_shape=jax.ShapeDtypeStruct((TOP_K, n), jnp.int32),
        grid_spec=pltpu.PrefetchScalarGridSpec(num_scalar_prefetch=1, grid=(n // bn,),
                                               in_specs=[spec, spec], out_specs=spec),
        compiler_params=_cparams(("arbitrary",)),
        name="slots",
    )(start, eid, pos)


def _sc_dispatch(x, slots, p):
    n, d = x.shape
    mesh = plsc.VectorSubcoreMesh(core_axis_name="core", subcore_axis_name="subcore")

    @functools.partial(pl.kernel, out_type=jax.ShapeDtypeStruct((p, d), x.dtype), mesh=mesh)
    def k(x_hbm, s_hbm, o_hbm):
        def body(x_vmem, s_vmem):
            for kk in range(TOP_K):
                pltpu.sync_copy(x_vmem, o_hbm.at[s_vmem.at[kk]])

        pltpu.emit_pipeline(
            body,
            grid=(n // SC_WINDOW,),
            in_specs=[pl.BlockSpec((SC_WINDOW, d), index_map=lambda i: (i, 0)),
                      pl.BlockSpec((TOP_K, SC_WINDOW), index_map=lambda i: (0, i))],
            out_specs=[],
            core_axis_name=("core", "subcore"),
            dimension_semantics=(pltpu.PARALLEL,),
        )(x_hbm, s_hbm)

    return k(x, slots)


def _sc_combine(y, slots):
    kk, n = slots.shape
    d = y.shape[1]
    mesh = plsc.VectorSubcoreMesh(core_axis_name="core", subcore_axis_name="subcore")

    @functools.partial(pl.kernel, out_type=jax.ShapeDtypeStruct((kk * n, d), y.dtype), mesh=mesh)
    def k(y_hbm, s_hbm, o_hbm):
        def body(s_vmem, o_vmem):
            pltpu.sync_copy(y_hbm.at[s_vmem.at[0]], o_vmem)

        pltpu.emit_pipeline(
            body,
            grid=(kk * n // SC_WINDOW,),
            in_specs=[pl.BlockSpec((1, SC_WINDOW), index_map=lambda i: (0, i))],
            out_specs=[pl.BlockSpec((SC_WINDOW, d), index_map=lambda i: (i, 0))],
            core_axis_name=("core", "subcore"),
            dimension_semantics=(pltpu.PARALLEL,),
        )(s_hbm, o_hbm)

    return k(y, slots.reshape(1, kk * n)).reshape(kk, n, d)


def _ffn_kernel(te_ref, nt_ref, xa_ref, xb_ref, wg_ref, wu_ref, wd_ref, ya_ref, yb_ref):
    del te_ref

    @pl.when(pl.program_id(0) < nt_ref[0])
    def _():
        parts = _unpack_bf16_pairs(xa_ref[...]) + _unpack_bf16_pairs(xb_ref[...])
        x = jnp.concatenate(parts, axis=1).astype(BF16)
        hid = _silu(jnp.dot(x, wg_ref[...], preferred_element_type=F32)) * \
            jnp.dot(x, wu_ref[...], preferred_element_type=F32)
        y = _bdot(hid, wd_ref[...])
        ya_ref[...] = _pack_bf16_pairs(y[:, :D_MODEL // 2])
        yb_ref[...] = _pack_bf16_pairs(y[:, D_MODEL // 2:])


def _ffn_call(tile_expert, n_tiles, xa, xb, wg, wu, wd):
    p, half = xa.shape
    rows = pl.BlockSpec((ROW_TILE, half), lambda j, te, nt: (j, 0))
    wspec = lambda s: pl.BlockSpec((None,) + s, lambda j, te, nt: (te[j], 0, 0))
    return pl.pallas_call(
        _ffn_kernel,
        out_shape=(jax.ShapeDtypeStruct((p, half), jnp.uint32), jax.ShapeDtypeStruct((p, half), jnp.uint32)),
        grid_spec=pltpu.PrefetchScalarGridSpec(
            num_scalar_prefetch=2, grid=(p // ROW_TILE,),
            in_specs=[rows, rows, wspec((D_MODEL, D_EXPERT)), wspec((D_MODEL, D_EXPERT)),
                      wspec((D_EXPERT, D_MODEL))],
            out_specs=(rows, rows)),
        compiler_params=_cparams(("arbitrary",)),
        name="ffn",
    )(tile_expert, n_tiles, xa, xb, wg, wu, wd)


def _final_kernel(xb_ref, ya_ref, yb_ref, wk_ref, mod_ref, o_ref):
    q = D_MODEL // 4
    accs = [jnp.zeros((xb_ref.shape[0], q), F32) for _ in range(4)]
    for k in range(TOP_K):
        w = wk_ref[:, k:k + 1]
        parts = _unpack_bf16_pairs(ya_ref[k]) + _unpack_bf16_pairs(yb_ref[k])
        accs = [a + w * p for a, p in zip(accs, parts)]
    for i, a in enumerate(accs):
        o_ref[:, i * q:(i + 1) * q] = xb_ref[:, i * q:(i + 1) * q] + mod_ref[:, 5120 + i * q:5120 + (i + 1) * q] * a


def _final_call(xb, ya, yb, wk, mod_rows, t, *, tm):
    n = xb.shape[0]
    half = ya.shape[2]
    tok = lambda width: pl.BlockSpec((tm, width), lambda i: (i, 0))
    yspec = pl.BlockSpec((TOP_K, tm, half), lambda i: (0, i, 0))
    return pl.pallas_call(
        _final_kernel,
        out_shape=jax.ShapeDtypeStruct((n, D_MODEL), F32),
        grid=(n // tm,),
        in_specs=[tok(D_MODEL), yspec, yspec, tok(LANES),
                  pl.BlockSpec((None, 1, 6 * D_MODEL), lambda i: ((i * tm) // t, 0, 0))],
        out_specs=tok(D_MODEL),
        compiler_params=_cparams(("arbitrary",)),
        name="final",
    )(xb, ya, yb, wk, mod_rows)


def _moe_call(ha, hb, eid, pos, wk, counts, xb, mod_rows, wg, wu, wd, t):
    n = xb.shape[0]
    max_tiles = n * TOP_K // ROW_TILE + N_EXPERTS
    p = max_tiles * ROW_TILE
    cnt = counts[:, 0].astype(jnp.int32)
    tiles_e = (cnt + ROW_TILE - 1) // ROW_TILE
    ends = jnp.cumsum(tiles_e)
    start = (ends - tiles_e) * ROW_TILE
    tile_expert = jnp.minimum(jnp.searchsorted(ends, jnp.arange(max_tiles, dtype=jnp.int32), side="right"),
                              N_EXPERTS - 1).astype(jnp.int32)
    n_tiles = ends[-1:].astype(jnp.int32)
    slots = _slots_call(start.astype(jnp.int32), eid, pos)
    xa = _sc_dispatch(ha, slots, p)
    xbb = _sc_dispatch(hb, slots, p)
    ya, yb = _ffn_call(tile_expert, n_tiles, xa, xbb, wg, wu, wd)
    ga = _sc_combine(ya, slots)
    gb = _sc_combine(yb, slots)
    return _final_call(xb, ga, gb, wk, mod_rows, t, tm=256)


def _rope_tables(t):
    n_rows = t // GRID_W
    rows = jnp.repeat(jnp.arange(n_rows), GRID_W).astype(F32)
    cols = jnp.tile(jnp.arange(GRID_W), n_rows).astype(F32)
    n_freq = HEAD_DIM // 4
    freqs = ROPE_THETA ** (-jnp.arange(n_freq, dtype=F32) / n_freq)
    ang = jnp.concatenate([rows[:, None] * freqs, cols[:, None] * freqs], axis=-1)
    ang = jnp.repeat(ang, 2, axis=-1)
    ang = jnp.concatenate([ang, ang], axis=-1)
    sign = jnp.where(jnp.arange(LANES) % 2 == 0, -1.0, 1.0).astype(F32)
    return jnp.cos(ang), jnp.sin(ang) * sign


def _dup_heads(a):
    s = a.shape[:-1]
    a4 = a.reshape(s + (N_KV_HEADS, HEAD_DIM))
    return jnp.concatenate([a4, a4], axis=-1).reshape(s + (KVD_W,))


def _prep_w_in(w_in):
    idx = np.cumsum(SPLIT_SIZES)[:-1].tolist()
    q, k, v, z, xbc, dt, gates = jnp.split(w_in, idx, axis=-1)
    pad = jnp.zeros((D_MODEL, LANES - SSD_HEADS), w_in.dtype)
    cols = [q, _dup_heads(k), _dup_heads(v), gates, z, xbc,
            dt[:, :SSD_HEADS], pad, dt[:, SSD_HEADS:], pad]
    return jnp.concatenate(cols, axis=-1).astype(BF16)


def _pad_heads(a):
    return jnp.pad(a.astype(F32), ((0, 0), (0, LANES - SSD_HEADS)))[:, None, :]


def _trunk(x, mod_rows, wts, rope_tabs, ctx_k, ctx_v, h0, *, tm, tq, want_state):
    b, t, _ = x.shape
    rope = rope_tabs is not None
    if rope:
        cos, sin = rope_tabs
    else:
        cos = sin = jnp.zeros((t, LANES), F32)
    kv_dtype = BF16 if rope else F32
    q, k, v, gates, z, xbc, dt = _inproj_call(x, mod_rows, wts["g1"], wts["w_in"], wts["qg"], wts["kg"],
                                              cos, sin, rope=rope, kv_dtype=kv_dtype, tm=tm)
    if ctx_k is not None:
        k_all = jnp.concatenate([k, ctx_k], axis=1)
        v_all = jnp.concatenate([v, ctx_v], axis=1)
    else:
        k_all, v_all = k, v
    attn = _attn_call(q, k_all, v_all, tq=tq)
    y_f, y_b, hfin = _ssd_call(xbc, dt, wts, h0, want_hfin=want_state)
    xb, ha, hb, eid, pos, wk, counts = _merge_call(
        x, attn, y_f, y_b, z, gates, mod_rows, wts["wa"], wts["ws"], wts["wo"], wts["sg"],
        wts["n2"], wts["wr_t"], wts["rb"], wts["wsg"], wts["wsu"], wts["wsd"], tm=tm)
    out = _moe_call(ha, hb, eid, pos, wk, counts, xb, mod_rows, wts["weg"], wts["weu"], wts["wed"], t)
    return out.reshape(b, t, D_MODEL), k, v, hfin


def kernel(x_prompt, x_sample, cache_k, cache_v, state_ssm, c, c_ctx, w_mod, b_mod, norm1_g, norm2_g, w_in,
           q_norm_g, k_norm_g, conv_w, conv_b, a_log, dt_bias, d_skip, ssd_norm_g, w_attn_proj, w_ssd_proj,
           w_out, w_router, router_bias, w_exp_gate, w_exp_up, w_exp_down, w_sh_gate, w_sh_up, w_sh_down):
    depth = w_mod.shape[0]
    assert depth == 1, "single trunk layer"
    bp, tp, _ = x_prompt.shape
    bs, ts, _ = x_sample.shape
    l = 0
    cvec = jnp.concatenate([c_ctx[None, :], c, jnp.zeros((8 - 1 - bs, D_MODEL), F32)], axis=0)
    mod = _mod_call(cvec, w_mod[l], b_mod[l][None, :])
    mod_prompt = jnp.broadcast_to(mod[0:1][:, None, :], (bp, 1, 6 * D_MODEL))
    mod_sample = mod[1:1 + bs][:, None, :]

    lower = np.tril(np.ones((CHUNK, CHUNK), np.float32))
    wts = dict(
        g1=norm1_g[l][None, :], n2=norm2_g[l][None, :],
        w_in=_prep_w_in(w_in[l]),
        qg=jnp.tile(q_norm_g[l], 2)[None, :], kg=jnp.tile(k_norm_g[l], 2)[None, :],
        conv_w=conv_w[l], conv_b=conv_b[l][None, :],
        a_neg=_pad_heads(-jnp.exp(a_log[l].astype(F32))), dt_bias=_pad_heads(dt_bias[l]),
        dskip=jnp.repeat(d_skip[l].astype(F32), SSD_HEAD_DIM)[None, :],
        tri=jnp.asarray(np.stack([lower, lower.T])),
        sg=ssd_norm_g[l][None, :],
        wa=w_attn_proj[l].astype(BF16), ws=w_ssd_proj[l].astype(BF16), wo=w_out[l].astype(BF16),
        wr_t=w_router[l].T.astype(BF16), rb=router_bias[l].astype(F32)[:, None],
        wsg=w_sh_gate[l].astype(BF16), wsu=w_sh_up[l].astype(BF16), wsd=w_sh_down[l].astype(BF16),
        weg=w_exp_gate[l].astype(BF16), weu=w_exp_up[l].astype(BF16), wed=w_exp_down[l].astype(BF16),
    )

    y_prompt, k_p, v_p, hfin = _trunk(x_prompt, mod_prompt, wts, None, None, None, None,
                                      tm=256, tq=256, want_state=True)
    new_k = k_p.reshape(bp, tp, N_KV_HEADS, LANES)[..., :HEAD_DIM][:, None]
    new_v = v_p.reshape(bp, tp, N_KV_HEADS, LANES)[..., :HEAD_DIM][:, None]
    new_state = hfin.reshape(bp, 1, 2, SSD_HEADS, SSD_HEAD_DIM, D_STATE)

    past = cache_k.shape[2]
    ctx_k = _dup_heads(cache_k[:, l].reshape(bs, past, KV_W)).astype(BF16)
    ctx_v = _dup_heads(cache_v[:, l].reshape(bs, past, KV_W)).astype(BF16)
    h0 = state_ssm[:, l].reshape(bs, 2, SSD_HEADS // 2, 2 * SSD_HEAD_DIM, D_STATE)
    y_sample, _, _, _ = _trunk(x_sample, mod_sample, wts, _rope_tables(ts), ctx_k, ctx_v, h0,
                               tm=256, tq=256, want_state=False)
    return (y_prompt, y_sample, new_k, new_v, new_state)
```

```python
import functools

import numpy as np
import jax
import jax.numpy as jnp
from jax import lax
from jax.experimental import pallas as pl
from jax.experimental.pallas import tpu as pltpu
from jax.experimental.pallas import tpu_sc as plsc

F32 = jnp.float32
BF16 = jnp.bfloat16

D_MODEL = 1024
GRID_W = 64
EPS = 1e-6
N_HEADS = 16
N_KV_HEADS = 4
HEAD_DIM = 64
ATTN_W = N_HEADS * HEAD_DIM
KV_W = N_KV_HEADS * HEAD_DIM
ROPE_THETA = 10000.0
D_INNER = 2048
SSD_HEAD_DIM = 64
SSD_HEADS = 32
SSD_GROUPS = 4
D_STATE = 128
D_CONV = 4
CHUNK = 128
CONV_CH = D_INNER + 2 * SSD_GROUPS * D_STATE
N_EXPERTS = 64
TOP_K = 8
N_EXPERT_GROUPS = 8
TOPK_GROUPS = 4
D_EXPERT = 256
D_SHARED = 256
ROUTED_SCALE = 2.5
SPLIT_SIZES = (ATTN_W, KV_W, KV_W, D_INNER, CONV_CH, 2 * SSD_HEADS, 2 * D_MODEL)

LANES = 128
KVD_W = N_KV_HEADS * LANES
C_Q, C_K, C_V, C_G, C_Z, C_X, C_DT, C_END = 0, 1024, 1536, 2048, 4096, 6144, 9216, 9472
VMEM_LIMIT = 56 * 1024 * 1024
Q_SCALE = HEAD_DIM ** -0.5 * 1.4426950408889634


def _cparams(sem):
    return pltpu.CompilerParams(dimension_semantics=sem, vmem_limit_bytes=VMEM_LIMIT)


def _silu(x):
    return x * jax.nn.sigmoid(x)


def _bdot(a, b):
    return jnp.dot(a.astype(BF16), b.astype(BF16), preferred_element_type=F32)


def _bdot_nt(a, b):
    return lax.dot_general(a.astype(BF16), b.astype(BF16), (((1,), (1,)), ((), ())),
                           preferred_element_type=F32)


def _mod_kernel(c_ref, w_ref, b_ref, o_ref):
    o_ref[...] = _bdot(_silu(c_ref[...]), w_ref[...]) + b_ref[...]


def _mod_call(cvec, w_mod, b_mod):
    n = w_mod.shape[1]
    bn = 1024
    return pl.pallas_call(
        _mod_kernel,
        out_shape=jax.ShapeDtypeStruct((8, n), F32),
        grid=(n // bn,),
        in_specs=[pl.BlockSpec((8, D_MODEL), lambda j: (0, 0)),
                  pl.BlockSpec((D_MODEL, bn), lambda j: (0, j)),
                  pl.BlockSpec((1, bn), lambda j: (0, j))],
        out_specs=pl.BlockSpec((8, bn), lambda j: (0, j)),
        compiler_params=_cparams(("arbitrary",)),
        name="mod",
    )(cvec, w_mod, b_mod)


def _inproj_kernel(x_ref, mod_ref, g1_ref, w_ref, qg_ref, kg_ref, cos_ref, sin_ref,
                   q_ref, k_ref, v_ref, gates_ref, z_ref, xbc_ref, dt_ref, *, rope):
    tm = x_ref.shape[0]
    x = x_ref[...]
    inv = lax.rsqrt(jnp.mean(x * x, axis=-1, keepdims=True) + EPS)
    h = (x * inv) * g1_ref[...]
    h = h * (1.0 + mod_ref[:, 1024:2048]) + mod_ref[:, 0:1024]
    hb = h.astype(BF16)

    lane = lax.broadcasted_iota(jnp.int32, (tm, LANES), 1)
    lo = lane < HEAD_DIM
    even = (lane & 1) == 0
    if rope:
        cos = cos_ref[...]
        sin = sin_ref[...]

    def rope_fn(blk):
        nxt = pltpu.roll(blk, LANES - 1, 1)
        prv = pltpu.roll(blk, 1, 1)
        return blk * cos + jnp.where(even, nxt, prv) * sin

    qg = qg_ref[...]
    kg = kg_ref[...]
    q = jnp.dot(hb, w_ref[:, C_Q:C_K], preferred_element_type=F32)
    for j in range(ATTN_W // LANES):
        blk = q[:, j * LANES:(j + 1) * LANES]
        sq = blk * blk
        s_all = jnp.sum(sq, axis=-1, keepdims=True)
        s_lo = jnp.sum(jnp.where(lo, sq, 0.0), axis=-1, keepdims=True)
        ms = jnp.where(lo, s_lo, s_all - s_lo) * (1.0 / HEAD_DIM)
        blk = blk * lax.rsqrt(ms + EPS) * qg
        if rope:
            blk = rope_fn(blk)
        q_ref[:, j * LANES:(j + 1) * LANES] = (blk * Q_SCALE).astype(q_ref.dtype)

    k = jnp.dot(hb, w_ref[:, C_K:C_V], preferred_element_type=F32)
    for j in range(N_KV_HEADS):
        blk = k[:, j * LANES:(j + 1) * LANES]
        ms = jnp.mean(blk * blk, axis=-1, keepdims=True)
        blk = blk * lax.rsqrt(ms + EPS) * kg
        if rope:
            blk = rope_fn(blk)
        k_ref[:, j * LANES:(j + 1) * LANES] = blk.astype(k_ref.dtype)

    v_ref[...] = jnp.dot(hb, w_ref[:, C_V:C_G], preferred_element_type=F32).astype(v_ref.dtype)
    gates_ref[...] = jnp.dot(hb, w_ref[:, C_G:C_Z], preferred_element_type=F32).astype(gates_ref.dtype)
    z_ref[...] = jnp.dot(hb, w_ref[:, C_Z:C_X], preferred_element_type=F32).astype(z_ref.dtype)
    xbc_ref[...] = jnp.dot(hb, w_ref[:, C_X:C_DT], preferred_element_type=F32).astype(xbc_ref.dtype)
    dt_ref[...] = jnp.dot(hb, w_ref[:, C_DT:C_END], preferred_element_type=F32)


def _inproj_call(x, mod_rows, g1, w, qg, kg, cos, sin, *, rope, kv_dtype, tm):
    b, t, _ = x.shape
    nt = t // tm
    tok = lambda width: pl.BlockSpec((None, tm, width), lambda bi, i: (bi, i, 0))
    const2 = lambda shape: pl.BlockSpec(shape, lambda bi, i: (0, 0))
    out_shape = (
        jax.ShapeDtypeStruct((b, t, ATTN_W), BF16),
        jax.ShapeDtypeStruct((b, t, KVD_W), kv_dtype),
        jax.ShapeDtypeStruct((b, t, KVD_W), kv_dtype),
        jax.ShapeDtypeStruct((b, t, 2 * D_MODEL), BF16),
        jax.ShapeDtypeStruct((b, t, D_INNER), BF16),
        jax.ShapeDtypeStruct((b, t, CONV_CH), BF16),
        jax.ShapeDtypeStruct((b, t, 2 * LANES), F32),
    )
    return pl.pallas_call(
        functools.partial(_inproj_kernel, rope=rope),
        out_shape=out_shape,
        grid=(b, nt),
        in_specs=[tok(D_MODEL),
                  pl.BlockSpec((None, 1, 6 * D_MODEL), lambda bi, i: (bi, 0, 0)),
                  const2((1, D_MODEL)),
                  pl.BlockSpec((D_MODEL, C_END), lambda bi, i: (0, 0), pipeline_mode=pl.Buffered(1)),
                  const2((1, LANES)), const2((1, LANES)),
                  pl.BlockSpec((tm, LANES), lambda bi, i: (i, 0)),
                  pl.BlockSpec((tm, LANES), lambda bi, i: (i, 0))],
        out_specs=(tok(ATTN_W), tok(KVD_W), tok(KVD_W), tok(2 * D_MODEL), tok(D_INNER),
                   tok(CONV_CH), tok(2 * LANES)),
        compiler_params=_cparams(("arbitrary", "arbitrary")),
        name="inproj",
    )(x, mod_rows, g1, w, qg, kg, cos, sin)


def _attn_kernel(q_ref, k_ref, v_ref, o_ref, *, kc):
    tq = q_ref.shape[0]
    tk = k_ref.shape[0]
    lane = lax.broadcasted_iota(jnp.int32, (tq, LANES), 1)
    lo = lane < HEAD_DIM
    qs = []
    for j in range(2):
        q2 = q_ref[:, j * LANES:(j + 1) * LANES]
        zero = jnp.zeros_like(q2)
        qs += [jnp.where(lo, q2, zero), jnp.where(lo, zero, q2)]
    q4 = jnp.concatenate(qs, axis=0)
    rows = 4 * tq
    lane_k = lax.broadcasted_iota(jnp.int32, (kc, LANES), 1)
    m = jnp.full((rows, 1), -jnp.inf, F32)
    acc = jnp.zeros((rows, LANES), F32)
    for c in range(tk // kc):
        kch = k_ref[c * kc:(c + 1) * kc, :].astype(BF16)
        vch = v_ref[c * kc:(c + 1) * kc, :].astype(BF16)
        vch = jnp.where(lane_k < HEAD_DIM, vch, jnp.ones_like(vch))
        s = _bdot_nt(q4, kch)
        m_new = jnp.maximum(m, jnp.max(s, axis=-1, keepdims=True))
        alpha = jnp.exp2(m - m_new)
        p = jnp.exp2((s - m_new).astype(BF16))
        acc = acc * alpha + jnp.dot(p, vch, preferred_element_type=F32)
        m = m_new
    o = acc * (1.0 / pltpu.roll(acc, HEAD_DIM, 1))
    for j in range(2):
        oa = o[(2 * j) * tq:(2 * j + 1) * tq]
        ob = pltpu.roll(o[(2 * j + 1) * tq:(2 * j + 2) * tq], HEAD_DIM, 1)
        o_ref[:, j * LANES:(j + 1) * LANES] = jnp.where(lo, oa, ob).astype(o_ref.dtype)


def _attn_call(q, k, v, *, tq):
    b, t, _ = q.shape
    tk = k.shape[1]
    nq = t // tq
    kc = 512 if tk % 512 == 0 else tk
    return pl.pallas_call(
        functools.partial(_attn_kernel, kc=kc),
        out_shape=jax.ShapeDtypeStruct((b, t, ATTN_W), BF16),
        grid=(b, N_KV_HEADS, nq),
        in_specs=[pl.BlockSpec((None, tq, 2 * LANES), lambda bi, g, i: (bi, i, g)),
                  pl.BlockSpec((None, tk, LANES), lambda bi, g, i: (bi, 0, g)),
                  pl.BlockSpec((None, tk, LANES), lambda bi, g, i: (bi, 0, g))],
        out_specs=pl.BlockSpec((None, tq, 2 * LANES), lambda bi, g, i: (bi, i, g)),
        compiler_params=_cparams(("arbitrary", "arbitrary", "arbitrary")),
        name="attn",
    )(q, k, v)


LOG2E = 1.4426950408889634


def _softplus(x):
    return jnp.maximum(x, 0.0) + jnp.log(1.0 + jnp.exp(-jnp.abs(x)))


def _ssd_kernel(*refs, nc, reverse, has_h0, want_hfin):
    refs = list(refs)
    conv = not reverse
    if conv:
        xbc_ref, prev_ref, next_ref, cw_ref, cb_ref, dsk_ref = refs[:6]
        refs = refs[6:]
    else:
        xc_ref = refs.pop(0)
    dt_ref, an_ref, dtb_ref, tri_ref = refs[:4]
    refs = refs[4:]
    h0_ref = refs.pop(0) if has_h0 else None
    hprev_ref = refs.pop(0) if (want_hfin and reverse) else None
    y_ref = refs.pop(0)
    xco_ref = refs.pop(0) if conv else None
    hfin_ref = refs.pop(0) if want_hfin else None
    h_scr = refs.pop(0)

    L = CHUNK
    c = pl.program_id(1)
    cidx = (nc - 1 - c) if reverse else c

    @pl.when(c == 0)
    def _():
        if has_h0:
            h_scr[...] = h0_ref[...]
        else:
            h_scr[...] = jnp.zeros_like(h_scr)

    row = lax.broadcasted_iota(jnp.int32, (L, LANES), 0)
    lane = lax.broadcasted_iota(jnp.int32, (L, LANES), 1)
    lo = lane < SSD_HEAD_DIM
    top = row < SSD_HEAD_DIM

    if conv:
        first = cidx == 0
        last = cidx == nc - 1

        def cols(a, w):
            xm = xbc_ref[:, a:a + w].astype(F32)
            rw = lax.broadcasted_iota(jnp.int32, (L, w), 0)
            p6 = jnp.where(first, 0.0, prev_ref[6:7, a:a + w].astype(F32))
            p7 = jnp.where(first, 0.0, prev_ref[7:8, a:a + w].astype(F32))
            n0 = jnp.where(last, 0.0, next_ref[0:1, a:a + w].astype(F32))
            r1 = jnp.where(rw == 0, p7, pltpu.roll(xm, 1, 0))
            r2 = jnp.where(rw == 0, p6, jnp.where(rw == 1, p7, pltpu.roll(xm, 2, 0)))
            rn = jnp.where(rw == L - 1, n0, pltpu.roll(xm, L - 1, 0))
            y = (r2 * cw_ref[0:1, a:a + w] + r1 * cw_ref[1:2, a:a + w] + xm * cw_ref[2:3, a:a + w]
                 + rn * cw_ref[3:4, a:a + w] + cb_ref[:, a:a + w])
            y = _silu(y).astype(BF16)
            xco_ref[:, a:a + w] = y
            return y
    else:
        def cols(a, w):
            return xc_ref[:, a:a + w]

    causal = tri_ref[...] > 0.0
    dt = _softplus(dt_ref[...] + dtb_ref[...])
    la2 = dt * (an_ref[...] * LOG2E)
    acum2 = jnp.dot(tri_ref[...], la2, preferred_element_type=F32, precision=lax.Precision.HIGHEST)
    dt_t = dt.T
    acum2_t = acum2.T
    tot2_t = jnp.sum(la2.T, axis=1, keepdims=True)
    lg_dt_t = jnp.log2(dt_t)
    r_t = lg_dt_t - acum2_t
    w_t = jnp.exp2(lg_dt_t + tot2_t - acum2_t)
    e_acum = jnp.exp2(acum2)
    e_tot_t = jnp.exp2(tot2_t)

    for g in range(SSD_GROUPS):
        bgb = cols(D_INNER + g * D_STATE, D_STATE)
        cgb = cols(D_INNER + SSD_GROUPS * D_STATE + g * D_STATE, D_STATE)
        cbm = _bdot_nt(cgb, bgb)
        h_grp = h_scr[4 * g:4 * g + 4]
        yo_grp = _bdot_nt(cgb, h_grp.reshape(4 * LANES, D_STATE))
        for pr in range(4):
            hp = g * 4 + pr
            ha, hb = 2 * hp, 2 * hp + 1
            xpb = cols(hp * LANES, LANES)
            zero = jnp.zeros_like(xpb)
            xs = jnp.concatenate([jnp.where(lo, xpb, zero), jnp.where(lo, zero, xpb)], axis=0)
            ms = []
            for hh in (ha, hb):
                e = jnp.exp2(acum2[:, hh:hh + 1] + r_t[hh:hh + 1, :])
                ms.append((cbm * jnp.where(causal, e, 0.0)).astype(BF16))
            y = jnp.dot(jnp.concatenate(ms, axis=1), xs, preferred_element_type=F32)
            y = y + yo_grp[:, pr * LANES:(pr + 1) * LANES] * \
                jnp.where(lo, e_acum[:, ha:ha + 1], e_acum[:, hb:hb + 1])
            if conv:
                y = y + dsk_ref[:, hp * LANES:(hp + 1) * LANES] * xpb.astype(F32)
            y_ref[:, hp * LANES:(hp + 1) * LANES] = y.astype(y_ref.dtype)
            wsel = jnp.where(top, w_t[ha:ha + 1, :], w_t[hb:hb + 1, :])
            st = jnp.dot((xpb.astype(F32).T * wsel).astype(BF16), bgb, preferred_element_type=F32)
            cd = jnp.where(top, e_tot_t[ha:ha + 1, :], e_tot_t[hb:hb + 1, :])
            h_scr[hp] = h_grp[pr] * cd + st

    if want_hfin:
        @pl.when(c == nc - 1)
        def _():
            if reverse:
                hfin_ref[0] = hprev_ref[...]
                hfin_ref[1] = h_scr[...]
            else:
                hfin_ref[...] = h_scr[...]


def _ssd_sweep(xin, dt, wts, h0, hprev, *, reverse, want_hfin):
    b, t, _ = xin.shape
    nc = t // CHUNK
    has_h0 = h0 is not None
    rb = CHUNK // 8
    nrb = t // 8
    d = 1 if reverse else 0
    cmap = (lambda c: nc - 1 - c) if reverse else (lambda c: c)
    hshape = (SSD_HEADS // 2, 2 * SSD_HEAD_DIM, D_STATE)

    chunk_spec = pl.BlockSpec((None, CHUNK, CONV_CH), lambda bi, c: (bi, cmap(c), 0))
    if reverse:
        in_specs = [chunk_spec]
        args = [xin]
    else:
        in_specs = [
            chunk_spec,
            pl.BlockSpec((None, 8, CONV_CH), lambda bi, c: (bi, jnp.maximum(c * rb - 1, 0), 0)),
            pl.BlockSpec((None, 8, CONV_CH), lambda bi, c: (bi, jnp.minimum((c + 1) * rb, nrb - 1), 0)),
            pl.BlockSpec((D_CONV, CONV_CH), lambda bi, c: (0, 0)),
            pl.BlockSpec((1, CONV_CH), lambda bi, c: (0, 0)),
            pl.BlockSpec((1, D_INNER), lambda bi, c: (0, 0)),
        ]
        args = [xin, xin, xin, wts["conv_w"], wts["conv_b"], wts["dskip"]]
    in_specs += [
        pl.BlockSpec((None, CHUNK, LANES), lambda bi, c: (bi, cmap(c), d)),
        pl.BlockSpec((None, 1, LANES), lambda bi, c: (d, 0, 0)),
        pl.BlockSpec((None, 1, LANES), lambda bi, c: (d, 0, 0)),
        pl.BlockSpec((None, CHUNK, CHUNK), lambda bi, c: (d, 0, 0)),
    ]
    args += [dt, wts["a_neg"], wts["dt_bias"], wts["tri"]]
    if has_h0:
        in_specs.append(pl.BlockSpec((None, None) + hshape, lambda bi, c: (bi, d, 0, 0, 0)))
        args.append(h0)
    if want_hfin and reverse:
        in_specs.append(pl.BlockSpec((None,) + hshape, lambda bi, c: (bi, 0, 0, 0)))
        args.append(hprev)
    out_shape = [jax.ShapeDtypeStruct((b, t, D_INNER), BF16)]
    out_specs = [pl.BlockSpec((None, CHUNK, D_INNER), lambda bi, c: (bi, cmap(c), 0))]
    if not reverse:
        out_shape.append(jax.ShapeDtypeStruct((b, t, CONV_CH), BF16))
        out_specs.append(pl.BlockSpec((None, CHUNK, CONV_CH), lambda bi, c: (bi, c, 0)))
    if want_hfin and reverse:
        out_shape.append(jax.ShapeDtypeStruct((b, 2) + hshape, F32))
        out_specs.append(pl.BlockSpec((None, 2) + hshape, lambda bi, c: (bi, 0, 0, 0, 0)))
    elif want_hfin:
        out_shape.append(jax.ShapeDtypeStruct((b,) + hshape, F32))
        out_specs.append(pl.BlockSpec((None,) + hshape, lambda bi, c: (bi, 0, 0, 0)))
    return pl.pallas_call(
        functools.partial(_ssd_kernel, nc=nc, reverse=reverse, has_h0=has_h0, want_hfin=want_hfin),
        out_shape=tuple(out_shape),
        grid=(b, nc),
        in_specs=in_specs,
        out_specs=tuple(out_specs),
        scratch_shapes=[pltpu.VMEM(hshape, F32)],
        compiler_params=_cparams(("arbitrary", "arbitrary")),
        name="ssd_bwd" if reverse else "ssd_fwd",
    )(*args)


def _ssd_call(xbc, dt, wts, h0, *, want_hfin):
    res = _ssd_sweep(xbc, dt, wts, h0, None, reverse=False, want_hfin=want_hfin)
    y_f, xc = res[0], res[1]
    hf = res[2] if want_hfin else None
    res = _ssd_sweep(xc, dt, wts, h0, hf, reverse=True, want_hfin=want_hfin)
    return y_f, res[0], (res[1] if want_hfin else None)


def _route(logits_t, bias_col):
    e, n = logits_t.shape
    per = e // N_EXPERT_GROUPS
    scores = jax.nn.sigmoid(logits_t)
    sel = scores + bias_col
    neg = jnp.float32(-jnp.inf)
    gs = []
    for g in range(N_EXPERT_GROUPS):
        blk = sel[g * per:(g + 1) * per, :]
        m1 = jnp.max(blk, axis=0, keepdims=True)
        is_m1 = blk == m1
        cnt = jnp.sum(jnp.where(is_m1, 1.0, 0.0), axis=0, keepdims=True)
        m2 = jnp.max(jnp.where(is_m1, neg, blk), axis=0, keepdims=True)
        gs.append(m1 + jnp.where(cnt >= 2.0, m1, m2))
    keep = []
    for g in range(N_EXPERT_GROUPS):
        rank = jnp.zeros_like(gs[g])
        for j in range(N_EXPERT_GROUPS):
            if j == g:
                continue
            beats = (gs[j] > gs[g]) if j > g else (gs[j] >= gs[g])
            rank = rank + jnp.where(beats, 1.0, 0.0)
        keep.append(rank < float(TOPK_GROUPS))
    selm = jnp.concatenate(
        [jnp.where(keep[g], sel[g * per:(g + 1) * per, :], neg) for g in range(N_EXPERT_GROUPS)], axis=0)
    eidx = lax.broadcasted_iota(jnp.int32, (e, n), 0)
    rank = jnp.zeros((e, n), F32)
    for j in range(e):
        rj = selm[j:j + 1, :]
        tie = jnp.where(eidx > j, 1.0, 0.0)
        rank = rank + jnp.where(rj > selm, 1.0, jnp.where(rj == selm, tie, 0.0))
    w = jnp.where(rank < float(TOP_K), scores, 0.0)
    wsum = jnp.sum(w, axis=0, keepdims=True)
    return w / wsum * ROUTED_SCALE, rank


def _pack_bf16_pairs(h):
    c = h.shape[1] // 2
    lo = pltpu.bitcast(h[:, :c].astype(BF16).astype(F32), jnp.uint32)
    hi = pltpu.bitcast(h[:, c:].astype(BF16).astype(F32), jnp.uint32)
    return (lo >> 16) | (hi & jnp.uint32(0xFFFF0000))


def _unpack_bf16_pairs(w):
    lo = pltpu.bitcast(w << 16, F32)
    hi = pltpu.bitcast(w & jnp.uint32(0xFFFF0000), F32)
    return lo, hi


def _rows8(rows):
    n = rows[0].shape[1]
    ridx = lax.broadcasted_iota(jnp.int32, (TOP_K, n), 0)
    out = jnp.zeros((TOP_K, n), rows[0].dtype)
    for k, r in enumerate(rows):
        out = jnp.where(ridx == k, r, out)
    return out


def _merge_kernel(x_ref, attn_ref, yf_ref, yb_ref, z_ref, gates_ref, mod_ref, wa_ref, ws_ref, wo_ref,
                  sg_ref, n2_ref, wr_ref, rb_ref, wsg_ref, wsu_ref, wsd_ref,
                  xb_ref, ha_ref, hb_ref, eid_ref, pos_ref, wk_ref, cnt_ref):
    tm = x_ref.shape[0]

    @pl.when(pl.program_id(0) == 0)
    def _():
        cnt_ref[...] = jnp.zeros_like(cnt_ref)

    x = x_ref[...]
    yy = yf_ref[...].astype(F32) + yb_ref[...].astype(F32)
    u = yy * _silu(z_ref[...].astype(F32))
    un = u * lax.rsqrt(jnp.mean(u * u, axis=-1, keepdims=True) + EPS) * sg_ref[...]
    ssd_o = _bdot(un, ws_ref[...])
    attn_o = jnp.dot(attn_ref[...], wa_ref[...], preferred_element_type=F32)
    ga = jax.nn.sigmoid(gates_ref[:, 0:D_MODEL].astype(F32))
    gs = jax.nn.sigmoid(gates_ref[:, D_MODEL:2 * D_MODEL].astype(F32))
    mix = _bdot(ga * attn_o + gs * ssd_o, wo_ref[...])
    x1 = x + mod_ref[:, 2048:3072] * mix
    h2 = x1 * lax.rsqrt(jnp.mean(x1 * x1, axis=-1, keepdims=True) + EPS) * n2_ref[...]
    h2 = h2 * (1.0 + mod_ref[:, 4096:5120]) + mod_ref[:, 3072:4096]
    h2b = h2.astype(BF16)
    ha_ref[...] = _pack_bf16_pairs(h2[:, :D_MODEL // 2])
    hb_ref[...] = _pack_bf16_pairs(h2[:, D_MODEL // 2:])

    logits_t = _bdot_nt(wr_ref[...], h2b)
    comb_t, rank = _route(logits_t, rb_ref[...])
    chosen = jnp.where(rank < float(TOP_K), 1.0, 0.0)
    r_i = lax.broadcasted_iota(jnp.int32, (tm, tm), 0)
    c_i = lax.broadcasted_iota(jnp.int32, (tm, tm), 1)
    before = jnp.where(r_i < c_i, 1.0, 0.0).astype(BF16)
    pos = cnt_ref[:, 0:1] + jnp.dot(chosen.astype(BF16), before, preferred_element_type=F32)
    cnt_ref[...] += jnp.sum(chosen, axis=1, keepdims=True)
    eidx = lax.broadcasted_iota(jnp.int32, rank.shape, 0).astype(F32)
    eids, poss, wks = [], [], []
    for k in range(TOP_K):
        selk = rank == float(k)
        eids.append(jnp.sum(jnp.where(selk, eidx, 0.0), axis=0, keepdims=True))
        poss.append(jnp.sum(jnp.where(selk, pos, 0.0), axis=0, keepdims=True))
        wks.append(jnp.sum(jnp.where(selk, comb_t, 0.0), axis=0, keepdims=True))
    eid_ref[...] = _rows8(eids).astype(jnp.int32)
    pos_ref[...] = _rows8(poss).astype(jnp.int32)
    wk8 = _rows8(wks)
    wk_ref[...] = jnp.concatenate([wk8, jnp.zeros((LANES - TOP_K, tm), F32)], axis=0).T

    hid = _silu(jnp.dot(h2b, wsg_ref[...], preferred_element_type=F32)) * \
        jnp.dot(h2b, wsu_ref[...], preferred_element_type=F32)
    xb_ref[...] = x1 + mod_ref[:, 5120:6144] * _bdot(hid, wsd_ref[...])


def _merge_call(x, attn, y_f, y_b, z, gates, mod_rows, wa, ws, wo, sg, n2, wr_t, rb, wsg, wsu, wsd, *, tm):
    b, t, _ = x.shape
    n = b * t
    flat = lambda a: a.reshape(n, a.shape[-1])
    tok = lambda width: pl.BlockSpec((tm, width), lambda i: (i, 0))
    const2 = lambda shape: pl.BlockSpec(shape, lambda i: (0, 0))
    k8 = pl.BlockSpec((TOP_K, tm), lambda i: (0, i))
    half = D_MODEL // 4
    return pl.pallas_call(
        _merge_kernel,
        out_shape=(jax.ShapeDtypeStruct((n, D_MODEL), F32),
                   jax.ShapeDtypeStruct((n, half), jnp.uint32),
                   jax.ShapeDtypeStruct((n, half), jnp.uint32),
                   jax.ShapeDtypeStruct((TOP_K, n), jnp.int32),
                   jax.ShapeDtypeStruct((TOP_K, n), jnp.int32),
                   jax.ShapeDtypeStruct((n, LANES), F32),
                   jax.ShapeDtypeStruct((N_EXPERTS, LANES), F32)),
        grid=(n // tm,),
        in_specs=[tok(D_MODEL), tok(ATTN_W), tok(D_INNER), tok(D_INNER), tok(D_INNER), tok(2 * D_MODEL),
                  pl.BlockSpec((None, 1, 6 * D_MODEL), lambda i: ((i * tm) // t, 0, 0)),
                  const2((ATTN_W, D_MODEL)), const2((D_INNER, D_MODEL)), const2((D_MODEL, D_MODEL)),
                  const2((1, D_INNER)), const2((1, D_MODEL)),
                  const2((N_EXPERTS, D_MODEL)), const2((N_EXPERTS, 1)),
                  const2((D_MODEL, D_SHARED)), const2((D_MODEL, D_SHARED)), const2((D_SHARED, D_MODEL))],
        out_specs=(tok(D_MODEL), tok(half), tok(half), k8, k8, tok(LANES), const2((N_EXPERTS, LANES))),
        compiler_params=_cparams(("arbitrary",)),
        name="merge",
    )(flat(x), flat(attn), flat(y_f), flat(y_b), flat(z), flat(gates), mod_rows, wa, ws, wo, sg, n2, wr_t, rb,
      wsg, wsu, wsd)


ROW_TILE = 512
SC_WINDOW = 128


def _slots_kernel(start_ref, eid_ref, pos_ref, slot_ref):
    eid = eid_ref[...]
    slot = pos_ref[...]
    for e in range(N_EXPERTS):
        slot = slot + jnp.where(eid == e, start_ref[e], 0)
    slot_ref[...] = slot


def _slots_call(start, eid, pos):
    n = eid.shape[1]
    bn = 2048 if n % 2048 == 0 else n
    spec = pl.BlockSpec((TOP_K, bn), lambda i, s: (0, i))
    return pl.pallas_call(
        _slots_kernel,
        out_shape=jax.ShapeDtypeStruct((TOP_K, n), jnp.int32),
        grid_spec=pltpu.PrefetchScalarGridSpec(num_scalar_prefetch=1, grid=(n // bn,),
                                               in_specs=[spec, spec], out_specs=spec),
        compiler_params=_cparams(("arbitrary",)),
        name="slots",
    )(start, eid, pos)


def _sc_dispatch(x, slots, p):
    n, d = x.shape
    mesh = plsc.VectorSubcoreMesh(core_axis_name="core", subcore_axis_name="subcore")

    @functools.partial(pl.kernel, out_type=jax.ShapeDtypeStruct((p, d), x.dtype), mesh=mesh)
    def k(x_hbm, s_hbm, o_hbm):
        def body(x_vmem, s_vmem):
            for kk in range(TOP_K):
                pltpu.sync_copy(x_vmem, o_hbm.at[s_vmem.at[kk]])

        pltpu.emit_pipeline(
            body,
            grid=(n // SC_WINDOW,),
            in_specs=[pl.BlockSpec((SC_WINDOW, d), index_map=lambda i: (i, 0)),
                      pl.BlockSpec((TOP_K, SC_WINDOW), index_map=lambda i: (0, i))],
            out_specs=[],
            core_axis_name=("core", "subcore"),
            dimension_semantics=(pltpu.PARALLEL,),
        )(x_hbm, s_hbm)

    return k(x, slots)


def _sc_combine(y, slots):
    kk, n = slots.shape
    d = y.shape[1]
    mesh = plsc.VectorSubcoreMesh(core_axis_name="core", subcore_axis_name="subcore")

    @functools.partial(pl.kernel, out_type=jax.ShapeDtypeStruct((kk * n, d), y.dtype), mesh=mesh)
    def k(y_hbm, s_hbm, o_hbm):
        def body(s_vmem, o_vmem):
            pltpu.sync_copy(y_hbm.at[s_vmem.at[0]], o_vmem)

        pltpu.emit_pipeline(
            body,
            grid=(kk * n // SC_WINDOW,),
            in_specs=[pl.BlockSpec((1, SC_WINDOW), index_map=lambda i: (0, i))],
            out_specs=[pl.BlockSpec((SC_WINDOW, d), index_map=lambda i: (i, 0))],
            core_axis_name=("core", "subcore"),
            dimension_semantics=(pltpu.PARALLEL,),
        )(s_hbm, o_hbm)

    return k(y, slots.reshape(1, kk * n)).reshape(kk, n, d)


def _ffn_kernel(te_ref, nt_ref, xa_ref, xb_ref, wg_ref, wu_ref, wd_ref, ya_ref, yb_ref):
    del te_ref

    @pl.when(pl.program_id(0) < nt_ref[0])
    def _():
        parts = _unpack_bf16_pairs(xa_ref[...]) + _unpack_bf16_pairs(xb_ref[...])
        x = jnp.concatenate(parts, axis=1).astype(BF16)
        hid = _silu(jnp.dot(x, wg_ref[...], preferred_element_type=F32)) * \
            jnp.dot(x, wu_ref[...], preferred_element_type=F32)
        y = _bdot(hid, wd_ref[...])
        ya_ref[...] = _pack_bf16_pairs(y[:, :D_MODEL // 2])
        yb_ref[...] = _pack_bf16_pairs(y[:, D_MODEL // 2:])


def _ffn_call(tile_expert, n_tiles, xa, xb, wg, wu, wd):
    p, half = xa.shape
    rows = pl.BlockSpec((ROW_TILE, half), lambda j, te, nt: (j, 0))
    wspec = lambda s: pl.BlockSpec((None,) + s, lambda j, te, nt: (te[j], 0, 0))
    return pl.pallas_call(
        _ffn_kernel,
        out_shape=(jax.ShapeDtypeStruct((p, half), jnp.uint32), jax.ShapeDtypeStruct((p, half), jnp.uint32)),
        grid_spec=pltpu.PrefetchScalarGridSpec(
            num_scalar_prefetch=2, grid=(p // ROW_TILE,),
            in_specs=[rows, rows, wspec((D_MODEL, D_EXPERT)), wspec((D_MODEL, D_EXPERT)),
                      wspec((D_EXPERT, D_MODEL))],
            out_specs=(rows, rows)),
        compiler_params=_cparams(("arbitrary",)),
        name="ffn",
    )(tile_expert, n_tiles, xa, xb, wg, wu, wd)


def _final_kernel(xb_ref, ya_ref, yb_ref, wk_ref, mod_ref, o_ref):
    q = D_MODEL // 4
    accs = [jnp.zeros((xb_ref.shape[0], q), F32) for _ in range(4)]
    for k in range(TOP_K):
        w = wk_ref[:, k:k + 1]
        parts = _unpack_bf16_pairs(ya_ref[k]) + _unpack_bf16_pairs(yb_ref[k])
        accs = [a + w * p for a, p in zip(accs, parts)]
    for i, a in enumerate(accs):
        o_ref[:, i * q:(i + 1) * q] = xb_ref[:, i * q:(i + 1) * q] + mod_ref[:, 5120 + i * q:5120 + (i + 1) * q] * a


def _final_call(xb, ya, yb, wk, mod_rows, t, *, tm):
    n = xb.shape[0]
    half = ya.shape[2]
    tok = lambda width: pl.BlockSpec((tm, width), lambda i: (i, 0))
    yspec = pl.BlockSpec((TOP_K, tm, half), lambda i: (0, i, 0))
    return pl.pallas_call(
        _final_kernel,
        out_shape=jax.ShapeDtypeStruct((n, D_MODEL), F32),
        grid=(n // tm,),
        in_specs=[tok(D_MODEL), yspec, yspec, tok(LANES),
                  pl.BlockSpec((None, 1, 6 * D_MODEL), lambda i: ((i * tm) // t, 0, 0))],
        out_specs=tok(D_MODEL),
        compiler_params=_cparams(("arbitrary",)),
        name="final",
    )(xb, ya, yb, wk, mod_rows)


def _moe_call(ha, hb, eid, pos, wk, counts, xb, mod_rows, wg, wu, wd, t):
    n = xb.shape[0]
    max_tiles = n * TOP_K // ROW_TILE + N_EXPERTS
    p = max_tiles * ROW_TILE
    cnt = counts[:, 0].astype(jnp.int32)
    tiles_e = (cnt + ROW_TILE - 1) // ROW_TILE
    ends = jnp.cumsum(tiles_e)
    start = (ends - tiles_e) * ROW_TILE
    tile_ids = jnp.arange(max_tiles, dtype=jnp.int32)
    tile_expert = jnp.minimum(jnp.sum((ends[None, :] <= tile_ids[:, None]).astype(jnp.int32), axis=1),
                              N_EXPERTS - 1)
    n_tiles = ends[-1:].astype(jnp.int32)
    slots = _slots_call(start.astype(jnp.int32), eid, pos)
    xa = _sc_dispatch(ha, slots, p)
    xbb = _sc_dispatch(hb, slots, p)
    ya, yb = _ffn_call(tile_expert, n_tiles, xa, xbb, wg, wu, wd)
    ga = _sc_combine(ya, slots)
    gb = _sc_combine(yb, slots)
    return _final_call(xb, ga, gb, wk, mod_rows, t, tm=256)


def _rope_tables(t):
    n_rows = t // GRID_W
    rows = jnp.repeat(jnp.arange(n_rows), GRID_W).astype(F32)
    cols = jnp.tile(jnp.arange(GRID_W), n_rows).astype(F32)
    n_freq = HEAD_DIM // 4
    freqs = ROPE_THETA ** (-jnp.arange(n_freq, dtype=F32) / n_freq)
    ang = jnp.concatenate([rows[:, None] * freqs, cols[:, None] * freqs], axis=-1)
    ang = jnp.repeat(ang, 2, axis=-1)
    ang = jnp.concatenate([ang, ang], axis=-1)
    sign = jnp.where(jnp.arange(LANES) % 2 == 0, -1.0, 1.0).astype(F32)
    return jnp.cos(ang), jnp.sin(ang) * sign


def _dup_heads(a):
    s = a.shape[:-1]
    a4 = a.reshape(s + (N_KV_HEADS, HEAD_DIM))
    return jnp.concatenate([a4, a4], axis=-1).reshape(s + (KVD_W,))


def _prep_w_in(w_in):
    idx = np.cumsum(SPLIT_SIZES)[:-1].tolist()
    q, k, v, z, xbc, dt, gates = jnp.split(w_in, idx, axis=-1)
    pad = jnp.zeros((D_MODEL, LANES - SSD_HEADS), w_in.dtype)
    cols = [q, _dup_heads(k), _dup_heads(v), gates, z, xbc,
            dt[:, :SSD_HEADS], pad, dt[:, SSD_HEADS:], pad]
    return jnp.concatenate(cols, axis=-1).astype(BF16)


def _pad_heads(a):
    return jnp.pad(a.astype(F32), ((0, 0), (0, LANES - SSD_HEADS)))[:, None, :]


def _trunk(x, mod_rows, wts, rope_tabs, ctx_k, ctx_v, h0, *, tm, tq, want_state):
    b, t, _ = x.shape
    rope = rope_tabs is not None
    if rope:
        cos, sin = rope_tabs
    else:
        cos = sin = jnp.zeros((t, LANES), F32)
    kv_dtype = BF16 if rope else F32
    q, k, v, gates, z, xbc, dt = _inproj_call(x, mod_rows, wts["g1"], wts["w_in"], wts["qg"], wts["kg"],
                                              cos, sin, rope=rope, kv_dtype=kv_dtype, tm=tm)
    if ctx_k is not None:
        k_all = jnp.concatenate([k, ctx_k], axis=1)
        v_all = jnp.concatenate([v, ctx_v], axis=1)
    else:
        k_all, v_all = k, v
    attn = _attn_call(q, k_all, v_all, tq=tq)
    y_f, y_b, hfin = _ssd_call(xbc, dt, wts, h0, want_hfin=want_state)
    xb, ha, hb, eid, pos, wk, counts = _merge_call(
        x, attn, y_f, y_b, z, gates, mod_rows, wts["wa"], wts["ws"], wts["wo"], wts["sg"],
        wts["n2"], wts["wr_t"], wts["rb"], wts["wsg"], wts["wsu"], wts["wsd"], tm=tm)
    out = _moe_call(ha, hb, eid, pos, wk, counts, xb, mod_rows, wts["weg"], wts["weu"], wts["wed"], t)
    return out.reshape(b, t, D_MODEL), k, v, hfin


def kernel(x_prompt, x_sample, cache_k, cache_v, state_ssm, c, c_ctx, w_mod, b_mod, norm1_g, norm2_g, w_in,
           q_norm_g, k_norm_g, conv_w, conv_b, a_log, dt_bias, d_skip, ssd_norm_g, w_attn_proj, w_ssd_proj,
           w_out, w_router, router_bias, w_exp_gate, w_exp_up, w_exp_down, w_sh_gate, w_sh_up, w_sh_down):
    depth = w_mod.shape[0]
    assert depth == 1, "single trunk layer"
    bp, tp, _ = x_prompt.shape
    bs, ts, _ = x_sample.shape
    l = 0
    cvec = jnp.concatenate([c_ctx[None, :], c, jnp.zeros((8 - 1 - bs, D_MODEL), F32)], axis=0)
    mod = _mod_call(cvec, w_mod[l], b_mod[l][None, :])
    mod_prompt = jnp.broadcast_to(mod[0:1][:, None, :], (bp, 1, 6 * D_MODEL))
    mod_sample = mod[1:1 + bs][:, None, :]

    lower = np.tril(np.ones((CHUNK, CHUNK), np.float32))
    wts = dict(
        g1=norm1_g[l][None, :], n2=norm2_g[l][None, :],
        w_in=_prep_w_in(w_in[l]),
        qg=jnp.tile(q_norm_g[l], 2)[None, :], kg=jnp.tile(k_norm_g[l], 2)[None, :],
        conv_w=conv_w[l], conv_b=conv_b[l][None, :],
        a_neg=_pad_heads(-jnp.exp(a_log[l].astype(F32))), dt_bias=_pad_heads(dt_bias[l]),
        dskip=jnp.repeat(d_skip[l].astype(F32), SSD_HEAD_DIM)[None, :],
        tri=jnp.asarray(np.stack([lower, lower.T])),
        sg=ssd_norm_g[l][None, :],
        wa=w_attn_proj[l].astype(BF16), ws=w_ssd_proj[l].astype(BF16), wo=w_out[l].astype(BF16),
        wr_t=w_router[l].T.astype(BF16), rb=router_bias[l].astype(F32)[:, None],
        wsg=w_sh_gate[l].astype(BF16), wsu=w_sh_up[l].astype(BF16), wsd=w_sh_down[l].astype(BF16),
        weg=w_exp_gate[l].astype(BF16), weu=w_exp_up[l].astype(BF16), wed=w_exp_down[l].astype(BF16),
    )

    y_prompt, k_p, v_p, hfin = _trunk(x_prompt, mod_prompt, wts, None, None, None, None,
                                      tm=256, tq=256, want_state=True)
    new_k = k_p.reshape(bp, tp, N_KV_HEADS, LANES)[..., :HEAD_DIM][:, None]
    new_v = v_p.reshape(bp, tp, N_KV_HEADS, LANES)[..., :HEAD_DIM][:, None]
    new_state = hfin.reshape(bp, 1, 2, SSD_HEADS, SSD_HEAD_DIM, D_STATE)

    past = cache_k.shape[2]
    ctx_k = _dup_heads(cache_k[:, l].reshape(bs, past, KV_W)).astype(BF16)
    ctx_v = _dup_heads(cache_v[:, l].reshape(bs, past, KV_W)).astype(BF16)
    h0 = state_ssm[:, l].reshape(bs, 2, SSD_HEADS // 2, 2 * SSD_HEAD_DIM, D_STATE)
    y_sample, _, _, _ = _trunk(x_sample, mod_sample, wts, _rope_tables(ts), ctx_k, ctx_v, h0,
                               tm=256, tq=256, want_state=False)
    return (y_prompt, y_sample, new_k, new_v, new_state)
```

```python
import functools

import numpy as np
import jax
import jax.numpy as jnp
from jax import lax
from jax.experimental import pallas as pl
from jax.experimental.pallas import tpu as pltpu
from jax.experimental.pallas import tpu_sc as plsc

F32 = jnp.float32
BF16 = jnp.bfloat16

D_MODEL = 1024
GRID_W = 64
EPS = 1e-6
N_HEADS = 16
N_KV_HEADS = 4
HEAD_DIM = 64
ATTN_W = N_HEADS * HEAD_DIM
KV_W = N_KV_HEADS * HEAD_DIM
ROPE_THETA = 10000.0
D_INNER = 2048
SSD_HEAD_DIM = 64
SSD_HEADS = 32
SSD_GROUPS = 4
D_STATE = 128
D_CONV = 4
CHUNK = 128
CONV_CH = D_INNER + 2 * SSD_GROUPS * D_STATE
N_EXPERTS = 64
TOP_K = 8
N_EXPERT_GROUPS = 8
TOPK_GROUPS = 4
D_EXPERT = 256
D_SHARED = 256
ROUTED_SCALE = 2.5
SPLIT_SIZES = (ATTN_W, KV_W, KV_W, D_INNER, CONV_CH, 2 * SSD_HEADS, 2 * D_MODEL)

LANES = 128
KVD_W = N_KV_HEADS * LANES
C_Q, C_K, C_V, C_G, C_Z, C_X, C_DT, C_END = 0, 1024, 1536, 2048, 4096, 6144, 9216, 9472
VMEM_LIMIT = 56 * 1024 * 1024
Q_SCALE = HEAD_DIM ** -0.5 * 1.4426950408889634


def _cparams(sem):
    return pltpu.CompilerParams(dimension_semantics=sem, vmem_limit_bytes=VMEM_LIMIT)


def _silu(x):
    return x * jax.nn.sigmoid(x)


def _bdot(a, b):
    return jnp.dot(a.astype(BF16), b.astype(BF16), preferred_element_type=F32)


def _bdot_nt(a, b):
    return lax.dot_general(a.astype(BF16), b.astype(BF16), (((1,), (1,)), ((), ())),
                           preferred_element_type=F32)


def _mod_kernel(c_ref, w_ref, b_ref, o_ref):
    o_ref[...] = _bdot(_silu(c_ref[...]), w_ref[...]) + b_ref[...]


def _mod_call(cvec, w_mod, b_mod):
    n = w_mod.shape[1]
    bn = 1024
    return pl.pallas_call(
        _mod_kernel,
        out_shape=jax.ShapeDtypeStruct((8, n), F32),
        grid=(n // bn,),
        in_specs=[pl.BlockSpec((8, D_MODEL), lambda j: (0, 0)),
                  pl.BlockSpec((D_MODEL, bn), lambda j: (0, j)),
                  pl.BlockSpec((1, bn), lambda j: (0, j))],
        out_specs=pl.BlockSpec((8, bn), lambda j: (0, j)),
        compiler_params=_cparams(("arbitrary",)),
        name="mod",
    )(cvec, w_mod, b_mod)


def _inproj_kernel(x_ref, mod_ref, g1_ref, w_ref, qg_ref, kg_ref, cos_ref, sin_ref,
                   q_ref, k_ref, v_ref, gates_ref, z_ref, xbc_ref, dt_ref, *, rope):
    tm = x_ref.shape[0]
    x = x_ref[...]
    inv = lax.rsqrt(jnp.mean(x * x, axis=-1, keepdims=True) + EPS)
    h = (x * inv) * g1_ref[...]
    h = h * (1.0 + mod_ref[:, 1024:2048]) + mod_ref[:, 0:1024]
    hb = h.astype(BF16)

    lane = lax.broadcasted_iota(jnp.int32, (tm, LANES), 1)
    lo = lane < HEAD_DIM
    even = (lane & 1) == 0
    if rope:
        cos = cos_ref[...]
        sin = sin_ref[...]

    def rope_fn(blk):
        nxt = pltpu.roll(blk, LANES - 1, 1)
        prv = pltpu.roll(blk, 1, 1)
        return blk * cos + jnp.where(even, nxt, prv) * sin

    qg = qg_ref[...]
    kg = kg_ref[...]
    q = jnp.dot(hb, w_ref[:, C_Q:C_K], preferred_element_type=F32)
    for j in range(ATTN_W // LANES):
        blk = q[:, j * LANES:(j + 1) * LANES]
        sq = blk * blk
        s_all = jnp.sum(sq, axis=-1, keepdims=True)
        s_lo = jnp.sum(jnp.where(lo, sq, 0.0), axis=-1, keepdims=True)
        ms = jnp.where(lo, s_lo, s_all - s_lo) * (1.0 / HEAD_DIM)
        blk = blk * lax.rsqrt(ms + EPS) * qg
        if rope:
            blk = rope_fn(blk)
        q_ref[:, j * LANES:(j + 1) * LANES] = (blk * Q_SCALE).astype(q_ref.dtype)

    k = jnp.dot(hb, w_ref[:, C_K:C_V], preferred_element_type=F32)
    for j in range(N_KV_HEADS):
        blk = k[:, j * LANES:(j + 1) * LANES]
        ms = jnp.mean(blk * blk, axis=-1, keepdims=True)
        blk = blk * lax.rsqrt(ms + EPS) * kg
        if rope:
            blk = rope_fn(blk)
        k_ref[:, j * LANES:(j + 1) * LANES] = blk.astype(k_ref.dtype)

    v_ref[...] = jnp.dot(hb, w_ref[:, C_V:C_G], preferred_element_type=F32).astype(v_ref.dtype)
    gates_ref[...] = jnp.dot(hb, w_ref[:, C_G:C_Z], preferred_element_type=F32).astype(gates_ref.dtype)
    z_ref[...] = jnp.dot(hb, w_ref[:, C_Z:C_X], preferred_element_type=F32).astype(z_ref.dtype)
    xbc_ref[...] = jnp.dot(hb, w_ref[:, C_X:C_DT], preferred_element_type=F32).astype(xbc_ref.dtype)
    dt_ref[...] = jnp.dot(hb, w_ref[:, C_DT:C_END], preferred_element_type=F32)


def _inproj_call(x, mod_rows, g1, w, qg, kg, cos, sin, *, rope, kv_dtype, tm):
    b, t, _ = x.shape
    nt = t // tm
    tok = lambda width: pl.BlockSpec((None, tm, width), lambda bi, i: (bi, i, 0))
    const2 = lambda shape: pl.BlockSpec(shape, lambda bi, i: (0, 0))
    out_shape = (
        jax.ShapeDtypeStruct((b, t, ATTN_W), BF16),
        jax.ShapeDtypeStruct((b, t, KVD_W), kv_dtype),
        jax.ShapeDtypeStruct((b, t, KVD_W), kv_dtype),
        jax.ShapeDtypeStruct((b, t, 2 * D_MODEL), BF16),
        jax.ShapeDtypeStruct((b, t, D_INNER), BF16),
        jax.ShapeDtypeStruct((b, t, CONV_CH), BF16),
        jax.ShapeDtypeStruct((b, t, 2 * LANES), F32),
    )
    return pl.pallas_call(
        functools.partial(_inproj_kernel, rope=rope),
        out_shape=out_shape,
        grid=(b, nt),
        in_specs=[tok(D_MODEL),
                  pl.BlockSpec((None, 1, 6 * D_MODEL), lambda bi, i: (bi, 0, 0)),
                  const2((1, D_MODEL)),
                  pl.BlockSpec((D_MODEL, C_END), lambda bi, i: (0, 0), pipeline_mode=pl.Buffered(1)),
                  const2((1, LANES)), const2((1, LANES)),
                  pl.BlockSpec((tm, LANES), lambda bi, i: (i, 0)),
                  pl.BlockSpec((tm, LANES), lambda bi, i: (i, 0))],
        out_specs=(tok(ATTN_W), tok(KVD_W), tok(KVD_W), tok(2 * D_MODEL), tok(D_INNER),
                   tok(CONV_CH), tok(2 * LANES)),
        compiler_params=_cparams(("arbitrary", "arbitrary")),
        name="inproj",
    )(x, mod_rows, g1, w, qg, kg, cos, sin)


def _attn_kernel(q_ref, k_ref, v_ref, o_ref, *, kc):
    tq = q_ref.shape[0]
    tk = k_ref.shape[0]
    lane = lax.broadcasted_iota(jnp.int32, (tq, LANES), 1)
    lo = lane < HEAD_DIM
    qs = []
    for j in range(2):
        q2 = q_ref[:, j * LANES:(j + 1) * LANES]
        zero = jnp.zeros_like(q2)
        qs += [jnp.where(lo, q2, zero), jnp.where(lo, zero, q2)]
    q4 = jnp.concatenate(qs, axis=0)
    rows = 4 * tq
    lane_k = lax.broadcasted_iota(jnp.int32, (kc, LANES), 1)
    m = jnp.full((rows, 1), -jnp.inf, F32)
    acc = jnp.zeros((rows, LANES), F32)
    for c in range(tk // kc):
        kch = k_ref[c * kc:(c + 1) * kc, :].astype(BF16)
        vch = v_ref[c * kc:(c + 1) * kc, :].astype(BF16)
        vch = jnp.where(lane_k < HEAD_DIM, vch, jnp.ones_like(vch))
        s = _bdot_nt(q4, kch)
        m_new = jnp.maximum(m, jnp.max(s, axis=-1, keepdims=True))
        alpha = jnp.exp2(m - m_new)
        p = jnp.exp2((s - m_new).astype(BF16))
        acc = acc * alpha + jnp.dot(p, vch, preferred_element_type=F32)
        m = m_new
    o = acc * (1.0 / pltpu.roll(acc, HEAD_DIM, 1))
    for j in range(2):
        oa = o[(2 * j) * tq:(2 * j + 1) * tq]
        ob = pltpu.roll(o[(2 * j + 1) * tq:(2 * j + 2) * tq], HEAD_DIM, 1)
        o_ref[:, j * LANES:(j + 1) * LANES] = jnp.where(lo, oa, ob).astype(o_ref.dtype)


def _attn_call(q, k, v, *, tq):
    b, t, _ = q.shape
    tk = k.shape[1]
    nq = t // tq
    kc = 512 if tk % 512 == 0 else tk
    return pl.pallas_call(
        functools.partial(_attn_kernel, kc=kc),
        out_shape=jax.ShapeDtypeStruct((b, t, ATTN_W), BF16),
        grid=(b, N_KV_HEADS, nq),
        in_specs=[pl.BlockSpec((None, tq, 2 * LANES), lambda bi, g, i: (bi, i, g)),
                  pl.BlockSpec((None, tk, LANES), lambda bi, g, i: (bi, 0, g)),
                  pl.BlockSpec((None, tk, LANES), lambda bi, g, i: (bi, 0, g))],
        out_specs=pl.BlockSpec((None, tq, 2 * LANES), lambda bi, g, i: (bi, i, g)),
        compiler_params=_cparams(("arbitrary", "arbitrary", "arbitrary")),
        name="attn",
    )(q, k, v)


LOG2E = 1.4426950408889634


def _softplus(x):
    return jnp.maximum(x, 0.0) + jnp.log(1.0 + jnp.exp(-jnp.abs(x)))


def _ssd_kernel(*refs, nc, reverse, has_h0, want_hfin):
    refs = list(refs)
    conv = not reverse
    if conv:
        xbc_ref, prev_ref, next_ref, cw_ref, cb_ref, dsk_ref = refs[:6]
        refs = refs[6:]
    else:
        xc_ref = refs.pop(0)
    dt_ref, an_ref, dtb_ref, tri_ref = refs[:4]
    refs = refs[4:]
    h0_ref = refs.pop(0) if has_h0 else None
    hprev_ref = refs.pop(0) if (want_hfin and reverse) else None
    y_ref = refs.pop(0)
    xco_ref = refs.pop(0) if conv else None
    hfin_ref = refs.pop(0) if want_hfin else None
    h_scr = refs.pop(0)

    L = CHUNK
    c = pl.program_id(1)
    cidx = (nc - 1 - c) if reverse else c

    @pl.when(c == 0)
    def _():
        if has_h0:
            h_scr[...] = h0_ref[...]
        else:
            h_scr[...] = jnp.zeros_like(h_scr)

    row = lax.broadcasted_iota(jnp.int32, (L, LANES), 0)
    lane = lax.broadcasted_iota(jnp.int32, (L, LANES), 1)
    lo = lane < SSD_HEAD_DIM
    top = row < SSD_HEAD_DIM

    if conv:
        first = cidx == 0
        last = cidx == nc - 1

        def cols(a, w):
            xm = xbc_ref[:, a:a + w].astype(F32)
            rw = lax.broadcasted_iota(jnp.int32, (L, w), 0)
            p6 = jnp.where(first, 0.0, prev_ref[6:7, a:a + w].astype(F32))
            p7 = jnp.where(first, 0.0, prev_ref[7:8, a:a + w].astype(F32))
            n0 = jnp.where(last, 0.0, next_ref[0:1, a:a + w].astype(F32))
            r1 = jnp.where(rw == 0, p7, pltpu.roll(xm, 1, 0))
            r2 = jnp.where(rw == 0, p6, jnp.where(rw == 1, p7, pltpu.roll(xm, 2, 0)))
            rn = jnp.where(rw == L - 1, n0, pltpu.roll(xm, L - 1, 0))
            y = (r2 * cw_ref[0:1, a:a + w] + r1 * cw_ref[1:2, a:a + w] + xm * cw_ref[2:3, a:a + w]
                 + rn * cw_ref[3:4, a:a + w] + cb_ref[:, a:a + w])
            y = _silu(y).astype(BF16)
            xco_ref[:, a:a + w] = y
            return y
    else:
        def cols(a, w):
            return xc_ref[:, a:a + w]

    causal = tri_ref[...] > 0.0
    dt = _softplus(dt_ref[...] + dtb_ref[...])
    la2 = dt * (an_ref[...] * LOG2E)
    acum2 = jnp.dot(tri_ref[...], la2, preferred_element_type=F32, precision=lax.Precision.HIGHEST)
    dt_t = dt.T
    acum2_t = acum2.T
    tot2_t = jnp.sum(la2.T, axis=1, keepdims=True)
    lg_dt_t = jnp.log2(dt_t)
    r_t = lg_dt_t - acum2_t
    w_t = jnp.exp2(lg_dt_t + tot2_t - acum2_t)
    e_acum = jnp.exp2(acum2)
    e_tot_t = jnp.exp2(tot2_t)

    for g in range(SSD_GROUPS):
        bgb = cols(D_INNER + g * D_STATE, D_STATE)
        cgb = cols(D_INNER + SSD_GROUPS * D_STATE + g * D_STATE, D_STATE)
        cbm = _bdot_nt(cgb, bgb)
        h_grp = h_scr[4 * g:4 * g + 4]
        yo_grp = _bdot_nt(cgb, h_grp.reshape(4 * LANES, D_STATE))
        for pr in range(4):
            hp = g * 4 + pr
            ha, hb = 2 * hp, 2 * hp + 1
            xpb = cols(hp * LANES, LANES)
            zero = jnp.zeros_like(xpb)
            xs = jnp.concatenate([jnp.where(lo, xpb, zero), jnp.where(lo, zero, xpb)], axis=0)
            ms = []
            for hh in (ha, hb):
                e = jnp.exp2(acum2[:, hh:hh + 1] + r_t[hh:hh + 1, :])
                ms.append((cbm * jnp.where(causal, e, 0.0)).astype(BF16))
            y = jnp.dot(jnp.concatenate(ms, axis=1), xs, preferred_element_type=F32)
            y = y + yo_grp[:, pr * LANES:(pr + 1) * LANES] * \
                jnp.where(lo, e_acum[:, ha:ha + 1], e_acum[:, hb:hb + 1])
            if conv:
                y = y + dsk_ref[:, hp * LANES:(hp + 1) * LANES] * xpb.astype(F32)
            y_ref[:, hp * LANES:(hp + 1) * LANES] = y.astype(y_ref.dtype)
            wsel = jnp.where(top, w_t[ha:ha + 1, :], w_t[hb:hb + 1, :])
            st = jnp.dot((xpb.astype(F32).T * wsel).astype(BF16), bgb, preferred_element_type=F32)
            cd = jnp.where(top, e_tot_t[ha:ha + 1, :], e_tot_t[hb:hb + 1, :])
            h_scr[hp] = h_grp[pr] * cd + st

    if want_hfin:
        @pl.when(c == nc - 1)
        def _():
            if reverse:
                hfin_ref[0] = hprev_ref[...]
                hfin_ref[1] = h_scr[...]
            else:
                hfin_ref[...] = h_scr[...]


def _ssd_sweep(xin, dt, wts, h0, hprev, *, reverse, want_hfin):
    b, t, _ = xin.shape
    nc = t // CHUNK
    has_h0 = h0 is not None
    rb = CHUNK // 8
    nrb = t // 8
    d = 1 if reverse else 0
    cmap = (lambda c: nc - 1 - c) if reverse else (lambda c: c)
    hshape = (SSD_HEADS // 2, 2 * SSD_HEAD_DIM, D_STATE)

    chunk_spec = pl.BlockSpec((None, CHUNK, CONV_CH), lambda bi, c: (bi, cmap(c), 0))
    if reverse:
        in_specs = [chunk_spec]
        args = [xin]
    else:
        in_specs = [
            chunk_spec,
            pl.BlockSpec((None, 8, CONV_CH), lambda bi, c: (bi, jnp.maximum(c * rb - 1, 0), 0)),
            pl.BlockSpec((None, 8, CONV_CH), lambda bi, c: (bi, jnp.minimum((c + 1) * rb, nrb - 1), 0)),
            pl.BlockSpec((D_CONV, CONV_CH), lambda bi, c: (0, 0)),
            pl.BlockSpec((1, CONV_CH), lambda bi, c: (0, 0)),
            pl.BlockSpec((1, D_INNER), lambda bi, c: (0, 0)),
        ]
        args = [xin, xin, xin, wts["conv_w"], wts["conv_b"], wts["dskip"]]
    in_specs += [
        pl.BlockSpec((None, CHUNK, LANES), lambda bi, c: (bi, cmap(c), d)),
        pl.BlockSpec((None, 1, LANES), lambda bi, c: (d, 0, 0)),
        pl.BlockSpec((None, 1, LANES), lambda bi, c: (d, 0, 0)),
        pl.BlockSpec((None, CHUNK, CHUNK), lambda bi, c: (d, 0, 0)),
    ]
    args += [dt, wts["a_neg"], wts["dt_bias"], wts["tri"]]
    if has_h0:
        in_specs.append(pl.BlockSpec((None, None) + hshape, lambda bi, c: (bi, d, 0, 0, 0)))
        args.append(h0)
    if want_hfin and reverse:
        in_specs.append(pl.BlockSpec((None,) + hshape, lambda bi, c: (bi, 0, 0, 0)))
        args.append(hprev)
    out_shape = [jax.ShapeDtypeStruct((b, t, D_INNER), BF16)]
    out_specs = [pl.BlockSpec((None, CHUNK, D_INNER), lambda bi, c: (bi, cmap(c), 0))]
    if not reverse:
        out_shape.append(jax.ShapeDtypeStruct((b, t, CONV_CH), BF16))
        out_specs.append(pl.BlockSpec((None, CHUNK, CONV_CH), lambda bi, c: (bi, c, 0)))
    if want_hfin and reverse:
        out_shape.append(jax.ShapeDtypeStruct((b, 2) + hshape, F32))
        out_specs.append(pl.BlockSpec((None, 2) + hshape, lambda bi, c: (bi, 0, 0, 0, 0)))
    elif want_hfin:
        out_shape.append(jax.ShapeDtypeStruct((b,) + hshape, F32))
        out_specs.append(pl.BlockSpec((None,) + hshape, lambda bi, c: (bi, 0, 0, 0)))
    return pl.pallas_call(
        functools.partial(_ssd_kernel, nc=nc, reverse=reverse, has_h0=has_h0, want_hfin=want_hfin),
        out_shape=tuple(out_shape),
        grid=(b, nc),
        in_specs=in_specs,
        out_specs=tuple(out_specs),
        scratch_shapes=[pltpu.VMEM(hshape, F32)],
        compiler_params=_cparams(("arbitrary", "arbitrary")),
        name="ssd_bwd" if reverse else "ssd_fwd",
    )(*args)


def _ssd_call(xbc, dt, wts, h0, *, want_hfin):
    res = _ssd_sweep(xbc, dt, wts, h0, None, reverse=False, want_hfin=want_hfin)
    y_f, xc = res[0], res[1]
    hf = res[2] if want_hfin else None
    res = _ssd_sweep(xc, dt, wts, h0, hf, reverse=True, want_hfin=want_hfin)
    return y_f, res[0], (res[1] if want_hfin else None)


def _route(logits_t, bias_col):
    e, n = logits_t.shape
    per = e // N_EXPERT_GROUPS
    scores = jax.nn.sigmoid(logits_t)
    sel = scores + bias_col
    neg = jnp.float32(-jnp.inf)
    gs = []
    for g in range(N_EXPERT_GROUPS):
        blk = sel[g * per:(g + 1) * per, :]
        m1 = jnp.max(blk, axis=0, keepdims=True)
        is_m1 = blk == m1
        cnt = jnp.sum(jnp.where(is_m1, 1.0, 0.0), axis=0, keepdims=True)
        m2 = jnp.max(jnp.where(is_m1, neg, blk), axis=0, keepdims=True)
        gs.append(m1 + jnp.where(cnt >= 2.0, m1, m2))
    keep = []
    for g in range(N_EXPERT_GROUPS):
        rank = jnp.zeros_like(gs[g])
        for j in range(N_EXPERT_GROUPS):
            if j == g:
                continue
            beats = (gs[j] > gs[g]) if j > g else (gs[j] >= gs[g])
            rank = rank + jnp.where(beats, 1.0, 0.0)
        keep.append(rank < float(TOPK_GROUPS))
    selm = jnp.concatenate(
        [jnp.where(keep[g], sel[g * per:(g + 1) * per, :], neg) for g in range(N_EXPERT_GROUPS)], axis=0)
    eidx = lax.broadcasted_iota(jnp.int32, (e, n), 0).astype(F32)
    cur = selm
    picks = []
    for _ in range(TOP_K):
        m = jnp.max(cur, axis=0, keepdims=True)
        idx = jnp.min(jnp.where(cur == m, eidx, float(e)), axis=0, keepdims=True)
        hit = eidx == idx
        picks.append((idx, hit))
        cur = jnp.where(hit, neg, cur)
    return scores, picks


def _pack_bf16_pairs(h):
    c = h.shape[1] // 2
    lo = pltpu.bitcast(h[:, :c].astype(BF16).astype(F32), jnp.uint32)
    hi = pltpu.bitcast(h[:, c:].astype(BF16).astype(F32), jnp.uint32)
    return (lo >> 16) | (hi & jnp.uint32(0xFFFF0000))


def _unpack_bf16_pairs(w):
    lo = pltpu.bitcast(w << 16, F32)
    hi = pltpu.bitcast(w & jnp.uint32(0xFFFF0000), F32)
    return lo, hi


def _rows8(rows):
    n = rows[0].shape[1]
    ridx = lax.broadcasted_iota(jnp.int32, (TOP_K, n), 0)
    out = jnp.zeros((TOP_K, n), rows[0].dtype)
    for k, r in enumerate(rows):
        out = jnp.where(ridx == k, r, out)
    return out


MERGE_SUB = 256


def _merge_kernel(x_ref, attn_ref, yf_ref, yb_ref, z_ref, gates_ref, mod_ref, wa_ref, ws_ref, wo_ref,
                  sg_ref, n2_ref, wr_ref, rb_ref, wsg_ref, wsu_ref, wsd_ref,
                  xb_ref, ha_ref, hb_ref, eid_ref, pos_ref, wk_ref, cnt_ref):
    tm = x_ref.shape[0]
    sub = MERGE_SUB

    @pl.when(pl.program_id(0) == 0)
    def _():
        cnt_ref[...] = jnp.zeros_like(cnt_ref)

    r_i = lax.broadcasted_iota(jnp.int32, (sub, sub), 0)
    c_i = lax.broadcasted_iota(jnp.int32, (sub, sub), 1)
    before = jnp.where(r_i < c_i, 1.0, 0.0).astype(BF16)
    cnt = cnt_ref[:, 0:1]

    for r0 in range(0, tm, sub):
        rs = slice(r0, r0 + sub)
        x = x_ref[rs, :]
        yy = yf_ref[rs, :].astype(F32) + yb_ref[rs, :].astype(F32)
        u = yy * _silu(z_ref[rs, :]).astype(F32)
        un = u * lax.rsqrt(jnp.mean(u * u, axis=-1, keepdims=True) + EPS) * sg_ref[...]
        ssd_o = _bdot(un, ws_ref[...])
        attn_o = jnp.dot(attn_ref[rs, :], wa_ref[...], preferred_element_type=F32)
        ga = jax.nn.sigmoid(gates_ref[rs, 0:D_MODEL]).astype(F32)
        gs = jax.nn.sigmoid(gates_ref[rs, D_MODEL:2 * D_MODEL]).astype(F32)
        mix = _bdot(ga * attn_o + gs * ssd_o, wo_ref[...])
        x1 = x + mod_ref[:, 2048:3072] * mix
        h2 = x1 * lax.rsqrt(jnp.mean(x1 * x1, axis=-1, keepdims=True) + EPS) * n2_ref[...]
        h2 = h2 * (1.0 + mod_ref[:, 4096:5120]) + mod_ref[:, 3072:4096]
        h2b = h2.astype(BF16)
        ha_ref[rs, :] = _pack_bf16_pairs(h2[:, :D_MODEL // 2])
        hb_ref[rs, :] = _pack_bf16_pairs(h2[:, D_MODEL // 2:])

        logits_t = _bdot_nt(wr_ref[...], h2b)
        scores, picks = _route(logits_t, rb_ref[...])
        chosen = jnp.zeros_like(scores)
        for _, hit in picks:
            chosen = chosen + jnp.where(hit, 1.0, 0.0)
        pos = cnt + jnp.dot(chosen.astype(BF16), before, preferred_element_type=F32)
        cnt = cnt + jnp.sum(chosen, axis=1, keepdims=True)
        poss = [jnp.sum(jnp.where(hit, pos, 0.0), axis=0, keepdims=True) for _, hit in picks]
        wks = [jnp.sum(jnp.where(hit, scores, 0.0), axis=0, keepdims=True) for _, hit in picks]
        wsum = wks[0]
        for w in wks[1:]:
            wsum = wsum + w
        eid_ref[:, rs] = _rows8([idx for idx, _ in picks]).astype(jnp.int32)
        pos_ref[:, rs] = _rows8(poss).astype(jnp.int32)
        wk8 = _rows8(wks) / wsum * ROUTED_SCALE
        wk_ref[rs, :] = jnp.concatenate([wk8, jnp.zeros((LANES - TOP_K, sub), F32)], axis=0).T

        hid = _silu(jnp.dot(h2b, wsg_ref[...], preferred_element_type=F32)) * \
            jnp.dot(h2b, wsu_ref[...], preferred_element_type=F32)
        xb_ref[rs, :] = x1 + mod_ref[:, 5120:6144] * _bdot(hid, wsd_ref[...])

    cnt_ref[...] = jnp.broadcast_to(cnt, cnt_ref.shape)


def _merge_call(x, attn, y_f, y_b, z, gates, mod_rows, wa, ws, wo, sg, n2, wr_t, rb, wsg, wsu, wsd, *, tm):
    b, t, _ = x.shape
    n = b * t
    flat = lambda a: a.reshape(n, a.shape[-1])
    tok = lambda width: pl.BlockSpec((tm, width), lambda i: (i, 0))
    const2 = lambda shape: pl.BlockSpec(shape, lambda i: (0, 0), pipeline_mode=pl.Buffered(1))
    k8 = pl.BlockSpec((TOP_K, tm), lambda i: (0, i))
    half = D_MODEL // 4
    return pl.pallas_call(
        _merge_kernel,
        out_shape=(jax.ShapeDtypeStruct((n, D_MODEL), F32),
                   jax.ShapeDtypeStruct((n, half), jnp.uint32),
                   jax.ShapeDtypeStruct((n, half), jnp.uint32),
                   jax.ShapeDtypeStruct((TOP_K, n), jnp.int32),
                   jax.ShapeDtypeStruct((TOP_K, n), jnp.int32),
                   jax.ShapeDtypeStruct((n, LANES), F32),
                   jax.ShapeDtypeStruct((N_EXPERTS, LANES), F32)),
        grid=(n // tm,),
        in_specs=[tok(D_MODEL), tok(ATTN_W), tok(D_INNER), tok(D_INNER), tok(D_INNER), tok(2 * D_MODEL),
                  pl.BlockSpec((None, 1, 6 * D_MODEL), lambda i: ((i * tm) // t, 0, 0)),
                  const2((ATTN_W, D_MODEL)), const2((D_INNER, D_MODEL)), const2((D_MODEL, D_MODEL)),
                  const2((1, D_INNER)), const2((1, D_MODEL)),
                  const2((N_EXPERTS, D_MODEL)), const2((N_EXPERTS, 1)),
                  const2((D_MODEL, D_SHARED)), const2((D_MODEL, D_SHARED)), const2((D_SHARED, D_MODEL))],
        out_specs=(tok(D_MODEL), tok(half), tok(half), k8, k8, tok(LANES),
                   pl.BlockSpec((N_EXPERTS, LANES), lambda i: (0, 0))),
        compiler_params=_cparams(("arbitrary",)),
        name="merge",
    )(flat(x), flat(attn), flat(y_f), flat(y_b), flat(z), flat(gates), mod_rows, wa, ws, wo, sg, n2, wr_t, rb,
      wsg, wsu, wsd)


ROW_TILE = 512
SC_WINDOW = 128


def _slots_kernel(start_ref, eid_ref, pos_ref, slot_ref):
    eid = eid_ref[...]
    slot = pos_ref[...]
    for e in range(N_EXPERTS):
        slot = slot + jnp.where(eid == e, start_ref[e], 0)
    slot_ref[...] = slot


def _slots_call(start, eid, pos):
    n = eid.shape[1]
    bn = 2048 if n % 2048 == 0 else n
    spec = pl.BlockSpec((TOP_K, bn), lambda i, s: (0, i))
    return pl.pallas_call(
        _slots_kernel,
        out_shape=jax.ShapeDtypeStruct((TOP_K, n), jnp.int32),
        grid_spec=pltpu.PrefetchScalarGridSpec(num_scalar_prefetch=1, grid=(n // bn,),
                                               in_specs=[spec, spec], out_specs=spec),
        compiler_params=_cparams(("arbitrary",)),
        name="slots",
    )(start, eid, pos)


def _sc_dispatch(x, slots, p):
    n, d = x.shape
    mesh = plsc.VectorSubcoreMesh(core_axis_name="core", subcore_axis_name="subcore")

    @functools.partial(pl.kernel, out_type=jax.ShapeDtypeStruct((p, d), x.dtype), mesh=mesh)
    def k(x_hbm, s_hbm, o_hbm):
        def body(x_vmem, s_vmem):
            for kk in range(TOP_K):
                pltpu.sync_copy(x_vmem, o_hbm.at[s_vmem.at[kk]])

        pltpu.emit_pipeline(
            body,
            grid=(n // SC_WINDOW,),
            in_specs=[pl.BlockSpec((SC_WINDOW, d), index_map=lambda i: (i, 0)),
                      pl.BlockSpec((TOP_K, SC_WINDOW), index_map=lambda i: (0, i))],
            out_specs=[],
            core_axis_name=("core", "subcore"),
            dimension_semantics=(pltpu.PARALLEL,),
        )(x_hbm, s_hbm)

    return k(x, slots)


def _sc_combine(y, slots):
    kk, n = slots.shape
    d = y.shape[1]
    mesh = plsc.VectorSubcoreMesh(core_axis_name="core", subcore_axis_name="subcore")

    @functools.partial(pl.kernel, out_type=jax.ShapeDtypeStruct((kk * n, d), y.dtype), mesh=mesh)
    def k(y_hbm, s_hbm, o_hbm):
        def body(s_vmem, o_vmem):
            pltpu.sync_copy(y_hbm.at[s_vmem.at[0]], o_vmem)

        pltpu.emit_pipeline(
            body,
            grid=(kk * n // SC_WINDOW,),
            in_specs=[pl.BlockSpec((1, SC_WINDOW), index_map=lambda i: (0, i))],
            out_specs=[pl.BlockSpec((SC_WINDOW, d), index_map=lambda i: (i, 0))],
            core_axis_name=("core", "subcore"),
            dimension_semantics=(pltpu.PARALLEL,),
        )(s_hbm, o_hbm)

    return k(y, slots.reshape(1, kk * n)).reshape(kk, n, d)


def _ffn_kernel(te_ref, nt_ref, xa_ref, xb_ref, wg_ref, wu_ref, wd_ref, ya_ref, yb_ref, wg_s, wu_s, wd_s):
    j = pl.program_id(0)

    @pl.when(j < nt_ref[0])
    def _():
        @pl.when(jnp.logical_or(j == 0, te_ref[j] != te_ref[jnp.maximum(j - 1, 0)]))
        def _():
            wg_s[...] = wg_ref[...].astype(BF16)
            wu_s[...] = wu_ref[...].astype(BF16)
            wd_s[...] = wd_ref[...].astype(BF16)

        parts = _unpack_bf16_pairs(xa_ref[...]) + _unpack_bf16_pairs(xb_ref[...])
        x = jnp.concatenate(parts, axis=1).astype(BF16)
        hid = _silu(jnp.dot(x, wg_s[...], preferred_element_type=F32)) * \
            jnp.dot(x, wu_s[...], preferred_element_type=F32)
        y = _bdot(hid, wd_s[...])
        ya_ref[...] = _pack_bf16_pairs(y[:, :D_MODEL // 2])
        yb_ref[...] = _pack_bf16_pairs(y[:, D_MODEL // 2:])


def _ffn_call(tile_expert, n_tiles, xa, xb, wg, wu, wd):
    p, half = xa.shape
    max_tiles = p // ROW_TILE
    rows_in = pl.BlockSpec((ROW_TILE, half), lambda j, te, nt: (jnp.maximum(jnp.minimum(j, nt[0] - 1), 0), 0))
    rows_out = pl.BlockSpec((ROW_TILE, half), lambda j, te, nt: (jnp.where(j < nt[0], j, max_tiles - 1), 0))
    wspec = lambda s: pl.BlockSpec((None,) + s, lambda j, te, nt: (te[j], 0, 0))
    return pl.pallas_call(
        _ffn_kernel,
        out_shape=(jax.ShapeDtypeStruct((p, half), jnp.uint32), jax.ShapeDtypeStruct((p, half), jnp.uint32)),
        grid_spec=pltpu.PrefetchScalarGridSpec(
            num_scalar_prefetch=2, grid=(max_tiles,),
            in_specs=[rows_in, rows_in, wspec((D_MODEL, D_EXPERT)), wspec((D_MODEL, D_EXPERT)),
                      wspec((D_EXPERT, D_MODEL))],
            out_specs=(rows_out, rows_out),
            scratch_shapes=[pltpu.VMEM((D_MODEL, D_EXPERT), BF16), pltpu.VMEM((D_MODEL, D_EXPERT), BF16),
                            pltpu.VMEM((D_EXPERT, D_MODEL), BF16)]),
        compiler_params=_cparams(("arbitrary",)),
        name="ffn",
    )(tile_expert, n_tiles, xa, xb, wg, wu, wd)


def _final_kernel(xb_ref, ya_ref, yb_ref, wk_ref, mod_ref, o_ref):
    q = D_MODEL // 4
    accs = [jnp.zeros((xb_ref.shape[0], q), F32) for _ in range(4)]
    for k in range(TOP_K):
        w = wk_ref[:, k:k + 1]
        parts = _unpack_bf16_pairs(ya_ref[k]) + _unpack_bf16_pairs(yb_ref[k])
        accs = [a + w * p for a, p in zip(accs, parts)]
    for i, a in enumerate(accs):
        o_ref[:, i * q:(i + 1) * q] = xb_ref[:, i * q:(i + 1) * q] + mod_ref[:, 5120 + i * q:5120 + (i + 1) * q] * a


def _final_call(xb, ya, yb, wk, mod_rows, t, *, tm):
    n = xb.shape[0]
    half = ya.shape[2]
    tok = lambda width: pl.BlockSpec((tm, width), lambda i: (i, 0))
    yspec = pl.BlockSpec((TOP_K, tm, half), lambda i: (0, i, 0))
    return pl.pallas_call(
        _final_kernel,
        out_shape=jax.ShapeDtypeStruct((n, D_MODEL), F32),
        grid=(n // tm,),
        in_specs=[tok(D_MODEL), yspec, yspec, tok(LANES),
                  pl.BlockSpec((None, 1, 6 * D_MODEL), lambda i: ((i * tm) // t, 0, 0))],
        out_specs=tok(D_MODEL),
        compiler_params=_cparams(("arbitrary",)),
        name="final",
    )(xb, ya, yb, wk, mod_rows)


def _moe_call(ha, hb, eid, pos, wk, counts, xb, mod_rows, wg, wu, wd, t):
    n = xb.shape[0]
    max_tiles = n * TOP_K // ROW_TILE + N_EXPERTS
    p = max_tiles * ROW_TILE
    cnt = counts[:, 0].astype(jnp.int32)
    tiles_e = (cnt + ROW_TILE - 1) // ROW_TILE
    ends = jnp.cumsum(tiles_e)
    start = (ends - tiles_e) * ROW_TILE
    tile_ids = jnp.arange(max_tiles, dtype=jnp.int32)
    tile_expert = jnp.minimum(jnp.sum((ends[None, :] <= tile_ids[:, None]).astype(jnp.int32), axis=1),
                              N_EXPERTS - 1)
    n_tiles = ends[-1:].astype(jnp.int32)
    slots = _slots_call(start.astype(jnp.int32), eid, pos)
    xa = _sc_dispatch(ha, slots, p)
    xbb = _sc_dispatch(hb, slots, p)
    ya, yb = _ffn_call(tile_expert, n_tiles, xa, xbb, wg, wu, wd)
    ga = _sc_combine(ya, slots)
    gb = _sc_combine(yb, slots)
    return _final_call(xb, ga, gb, wk, mod_rows, t, tm=256)


def _rope_tables(t):
    n_rows = t // GRID_W
    rows = jnp.repeat(jnp.arange(n_rows), GRID_W).astype(F32)
    cols = jnp.tile(jnp.arange(GRID_W), n_rows).astype(F32)
    n_freq = HEAD_DIM // 4
    freqs = ROPE_THETA ** (-jnp.arange(n_freq, dtype=F32) / n_freq)
    ang = jnp.concatenate([rows[:, None] * freqs, cols[:, None] * freqs], axis=-1)
    ang = jnp.repeat(ang, 2, axis=-1)
    ang = jnp.concatenate([ang, ang], axis=-1)
    sign = jnp.where(jnp.arange(LANES) % 2 == 0, -1.0, 1.0).astype(F32)
    return jnp.cos(ang), jnp.sin(ang) * sign


def _dup_heads(a):
    s = a.shape[:-1]
    a4 = a.reshape(s + (N_KV_HEADS, HEAD_DIM))
    return jnp.concatenate([a4, a4], axis=-1).reshape(s + (KVD_W,))


def _prep_w_in(w_in):
    idx = np.cumsum(SPLIT_SIZES)[:-1].tolist()
    q, k, v, z, xbc, dt, gates = jnp.split(w_in, idx, axis=-1)
    pad = jnp.zeros((D_MODEL, LANES - SSD_HEADS), w_in.dtype)
    cols = [q, _dup_heads(k), _dup_heads(v), gates, z, xbc,
            dt[:, :SSD_HEADS], pad, dt[:, SSD_HEADS:], pad]
    return jnp.concatenate(cols, axis=-1).astype(BF16)


def _pad_heads(a):
    return jnp.pad(a.astype(F32), ((0, 0), (0, LANES - SSD_HEADS)))[:, None, :]


def _trunk(x, mod_rows, wts, rope_tabs, ctx_k, ctx_v, h0, *, tm, tq, want_state):
    b, t, _ = x.shape
    rope = rope_tabs is not None
    if rope:
        cos, sin = rope_tabs
    else:
        cos = sin = jnp.zeros((t, LANES), F32)
    kv_dtype = BF16 if rope else F32
    q, k, v, gates, z, xbc, dt = _inproj_call(x, mod_rows, wts["g1"], wts["w_in"], wts["qg"], wts["kg"],
                                              cos, sin, rope=rope, kv_dtype=kv_dtype, tm=tm)
    if ctx_k is not None:
        k_all = jnp.concatenate([k, ctx_k], axis=1)
        v_all = jnp.concatenate([v, ctx_v], axis=1)
    else:
        k_all, v_all = k, v
    attn = _attn_call(q, k_all, v_all, tq=tq)
    y_f, y_b, hfin = _ssd_call(xbc, dt, wts, h0, want_hfin=want_state)
    xb, ha, hb, eid, pos, wk, counts = _merge_call(
        x, attn, y_f, y_b, z, gates, mod_rows, wts["wa"], wts["ws"], wts["wo"], wts["sg"],
        wts["n2"], wts["wr_t"], wts["rb"], wts["wsg"], wts["wsu"], wts["wsd"], tm=MERGE_SUB)
    out = _moe_call(ha, hb, eid, pos, wk, counts, xb, mod_rows, wts["weg"], wts["weu"], wts["wed"], t)
    return out.reshape(b, t, D_MODEL), k, v, hfin


def kernel(x_prompt, x_sample, cache_k, cache_v, state_ssm, c, c_ctx, w_mod, b_mod, norm1_g, norm2_g, w_in,
           q_norm_g, k_norm_g, conv_w, conv_b, a_log, dt_bias, d_skip, ssd_norm_g, w_attn_proj, w_ssd_proj,
           w_out, w_router, router_bias, w_exp_gate, w_exp_up, w_exp_down, w_sh_gate, w_sh_up, w_sh_down):
    depth = w_mod.shape[0]
    assert depth == 1, "single trunk layer"
    bp, tp, _ = x_prompt.shape
    bs, ts, _ = x_sample.shape
    l = 0
    cvec = jnp.concatenate([c_ctx[None, :], c, jnp.zeros((8 - 1 - bs, D_MODEL), F32)], axis=0)
    mod = _mod_call(cvec, w_mod[l], b_mod[l][None, :])
    mod_prompt = jnp.broadcast_to(mod[0:1][:, None, :], (bp, 1, 6 * D_MODEL))
    mod_sample = mod[1:1 + bs][:, None, :]

    lower = np.tril(np.ones((CHUNK, CHUNK), np.float32))
    wts = dict(
        g1=norm1_g[l][None, :], n2=norm2_g[l][None, :],
        w_in=_prep_w_in(w_in[l]),
        qg=jnp.tile(q_norm_g[l], 2)[None, :], kg=jnp.tile(k_norm_g[l], 2)[None, :],
        conv_w=conv_w[l], conv_b=conv_b[l][None, :],
        a_neg=_pad_heads(-jnp.exp(a_log[l].astype(F32))), dt_bias=_pad_heads(dt_bias[l]),
        dskip=jnp.repeat(d_skip[l].astype(F32), SSD_HEAD_DIM)[None, :],
        tri=jnp.asarray(np.stack([lower, lower.T])),
        sg=ssd_norm_g[l][None, :],
        wa=w_attn_proj[l].astype(BF16), ws=w_ssd_proj[l].astype(BF16), wo=w_out[l].astype(BF16),
        wr_t=w_router[l].T.astype(BF16), rb=router_bias[l].astype(F32)[:, None],
        wsg=w_sh_gate[l].astype(BF16), wsu=w_sh_up[l].astype(BF16), wsd=w_sh_down[l].astype(BF16),
        weg=w_exp_gate[l], weu=w_exp_up[l], wed=w_exp_down[l],
    )

    y_prompt, k_p, v_p, hfin = _trunk(x_prompt, mod_prompt, wts, None, None, None, None,
                                      tm=256, tq=256, want_state=True)
    new_k = k_p.reshape(bp, tp, N_KV_HEADS, LANES)[..., :HEAD_DIM][:, None]
    new_v = v_p.reshape(bp, tp, N_KV_HEADS, LANES)[..., :HEAD_DIM][:, None]
    new_state = hfin.reshape(bp, 1, 2, SSD_HEADS, SSD_HEAD_DIM, D_STATE)

    past = cache_k.shape[2]
    ctx_k = _dup_heads(cache_k[:, l].reshape(bs, past, KV_W)).astype(BF16)
    ctx_v = _dup_heads(cache_v[:, l].reshape(bs, past, KV_W)).astype(BF16)
    h0 = state_ssm[:, l].reshape(bs, 2, SSD_HEADS // 2, 2 * SSD_HEAD_DIM, D_STATE)
    y_sample, _, _, _ = _trunk(x_sample, mod_sample, wts, _rope_tables(ts), ctx_k, ctx_v, h0,
                               tm=256, tq=256, want_state=False)
    return (y_prompt, y_sample, new_k, new_v, new_state)
```

```python
import functools

import numpy as np
import jax
import jax.numpy as jnp
from jax import lax
from jax.experimental import pallas as pl
from jax.experimental.pallas import tpu as pltpu
from jax.experimental.pallas import tpu_sc as plsc

F32 = jnp.float32
BF16 = jnp.bfloat16

D_MODEL = 1024
GRID_W = 64
EPS = 1e-6
N_HEADS = 16
N_KV_HEADS = 4
HEAD_DIM = 64
ATTN_W = N_HEADS * HEAD_DIM
KV_W = N_KV_HEADS * HEAD_DIM
ROPE_THETA = 10000.0
D_INNER = 2048
SSD_HEAD_DIM = 64
SSD_HEADS = 32
SSD_GROUPS = 4
D_STATE = 128
D_CONV = 4
CHUNK = 128
CONV_CH = D_INNER + 2 * SSD_GROUPS * D_STATE
N_EXPERTS = 64
TOP_K = 8
N_EXPERT_GROUPS = 8
TOPK_GROUPS = 4
D_EXPERT = 256
D_SHARED = 256
ROUTED_SCALE = 2.5
SPLIT_SIZES = (ATTN_W, KV_W, KV_W, D_INNER, CONV_CH, 2 * SSD_HEADS, 2 * D_MODEL)

LANES = 128
KVD_W = N_KV_HEADS * LANES
C_Q, C_K, C_V, C_G, C_Z, C_X, C_DT, C_END = 0, 1024, 1536, 2048, 4096, 6144, 9216, 9472
VMEM_LIMIT = 56 * 1024 * 1024
Q_SCALE = HEAD_DIM ** -0.5 * 1.4426950408889634


def _cparams(sem):
    return pltpu.CompilerParams(dimension_semantics=sem, vmem_limit_bytes=VMEM_LIMIT)


def _silu(x):
    return x * jax.nn.sigmoid(x)


def _bdot(a, b):
    return jnp.dot(a.astype(BF16), b.astype(BF16), preferred_element_type=F32)


def _bdot_nt(a, b):
    return lax.dot_general(a.astype(BF16), b.astype(BF16), (((1,), (1,)), ((), ())),
                           preferred_element_type=F32)


def _mod_kernel(c_ref, w_ref, b_ref, o_ref):
    o_ref[...] = _bdot(_silu(c_ref[...]), w_ref[...]) + b_ref[...]


def _mod_call(cvec, w_mod, b_mod):
    n = w_mod.shape[1]
    bn = 1024
    return pl.pallas_call(
        _mod_kernel,
        out_shape=jax.ShapeDtypeStruct((8, n), F32),
        grid=(n // bn,),
        in_specs=[pl.BlockSpec((8, D_MODEL), lambda j: (0, 0)),
                  pl.BlockSpec((D_MODEL, bn), lambda j: (0, j)),
                  pl.BlockSpec((1, bn), lambda j: (0, j))],
        out_specs=pl.BlockSpec((8, bn), lambda j: (0, j)),
        compiler_params=_cparams(("arbitrary",)),
        name="mod",
    )(cvec, w_mod, b_mod)


def _inproj_kernel(x_ref, mod_ref, g1_ref, w_ref, qg_ref, kg_ref, cos_ref, sin_ref,
                   q_ref, k_ref, v_ref, gates_ref, z_ref, xbc_ref, dt_ref, *, rope):
    tm = x_ref.shape[0]
    x = x_ref[...]
    inv = lax.rsqrt(jnp.mean(x * x, axis=-1, keepdims=True) + EPS)
    h = (x * inv) * g1_ref[...]
    h = h * (1.0 + mod_ref[:, 1024:2048]) + mod_ref[:, 0:1024]
    hb = h.astype(BF16)

    lane = lax.broadcasted_iota(jnp.int32, (tm, LANES), 1)
    lo = lane < HEAD_DIM
    even = (lane & 1) == 0
    if rope:
        cos = cos_ref[...]
        sin = sin_ref[...]

    def rope_fn(blk):
        nxt = pltpu.roll(blk, LANES - 1, 1)
        prv = pltpu.roll(blk, 1, 1)
        return blk * cos + jnp.where(even, nxt, prv) * sin

    qg = qg_ref[...]
    kg = kg_ref[...]
    q = jnp.dot(hb, w_ref[:, C_Q:C_K], preferred_element_type=F32)
    for j in range(ATTN_W // LANES):
        blk = q[:, j * LANES:(j + 1) * LANES]
        sq = blk * blk
        s_all = jnp.sum(sq, axis=-1, keepdims=True)
        s_lo = jnp.sum(jnp.where(lo, sq, 0.0), axis=-1, keepdims=True)
        ms = jnp.where(lo, s_lo, s_all - s_lo) * (1.0 / HEAD_DIM)
        blk = blk * lax.rsqrt(ms + EPS) * qg
        if rope:
            blk = rope_fn(blk)
        q_ref[:, j * LANES:(j + 1) * LANES] = (blk * Q_SCALE).astype(q_ref.dtype)

    k = jnp.dot(hb, w_ref[:, C_K:C_V], preferred_element_type=F32)
    for j in range(N_KV_HEADS):
        blk = k[:, j * LANES:(j + 1) * LANES]
        ms = jnp.mean(blk * blk, axis=-1, keepdims=True)
        blk = blk * lax.rsqrt(ms + EPS) * kg
        if rope:
            blk = rope_fn(blk)
        k_ref[:, j * LANES:(j + 1) * LANES] = blk.astype(k_ref.dtype)

    v_ref[...] = jnp.dot(hb, w_ref[:, C_V:C_G], preferred_element_type=F32).astype(v_ref.dtype)
    gates_ref[...] = jnp.dot(hb, w_ref[:, C_G:C_Z], preferred_element_type=F32).astype(gates_ref.dtype)
    z_ref[...] = jnp.dot(hb, w_ref[:, C_Z:C_X], preferred_element_type=F32).astype(z_ref.dtype)
    xbc_ref[...] = jnp.dot(hb, w_ref[:, C_X:C_DT], preferred_element_type=F32).astype(xbc_ref.dtype)
    dt_ref[...] = jnp.dot(hb, w_ref[:, C_DT:C_END], preferred_element_type=F32)


def _inproj_call(x, mod_rows, g1, w, qg, kg, cos, sin, *, rope, kv_dtype, tm):
    b, t, _ = x.shape
    nt = t // tm
    tok = lambda width: pl.BlockSpec((None, tm, width), lambda bi, i: (bi, i, 0))
    const2 = lambda shape: pl.BlockSpec(shape, lambda bi, i: (0, 0))
    out_shape = (
        jax.ShapeDtypeStruct((b, t, ATTN_W), BF16),
        jax.ShapeDtypeStruct((b, t, KVD_W), kv_dtype),
        jax.ShapeDtypeStruct((b, t, KVD_W), kv_dtype),
        jax.ShapeDtypeStruct((b, t, 2 * D_MODEL), BF16),
        jax.ShapeDtypeStruct((b, t, D_INNER), BF16),
        jax.ShapeDtypeStruct((b, t, CONV_CH), BF16),
        jax.ShapeDtypeStruct((b, t, 2 * LANES), F32),
    )
    return pl.pallas_call(
        functools.partial(_inproj_kernel, rope=rope),
        out_shape=out_shape,
        grid=(b, nt),
        in_specs=[tok(D_MODEL),
                  pl.BlockSpec((None, 1, 6 * D_MODEL), lambda bi, i: (bi, 0, 0)),
                  const2((1, D_MODEL)),
                  pl.BlockSpec((D_MODEL, C_END), lambda bi, i: (0, 0), pipeline_mode=pl.Buffered(1)),
                  const2((1, LANES)), const2((1, LANES)),
                  pl.BlockSpec((tm, LANES), lambda bi, i: (i, 0)),
                  pl.BlockSpec((tm, LANES), lambda bi, i: (i, 0))],
        out_specs=(tok(ATTN_W), tok(KVD_W), tok(KVD_W), tok(2 * D_MODEL), tok(D_INNER),
                   tok(CONV_CH), tok(2 * LANES)),
        compiler_params=_cparams(("arbitrary", "arbitrary")),
        name="inproj",
    )(x, mod_rows, g1, w, qg, kg, cos, sin)


KEY_CHUNK = 512


def _key_chunk(n):
    return KEY_CHUNK if n % KEY_CHUNK == 0 else n


def _attn_kernel(*refs, has_ctx):
    if has_ctx:
        q_ref, k_ref, v_ref, kctx_ref, vctx_ref, o_ref = refs
        sources = ((k_ref, v_ref), (kctx_ref, vctx_ref))
    else:
        q_ref, k_ref, v_ref, o_ref = refs
        sources = ((k_ref, v_ref),)
    tq = q_ref.shape[0]
    lane = lax.broadcasted_iota(jnp.int32, (tq, LANES), 1)
    lo = lane < HEAD_DIM
    qs = []
    for j in range(2):
        q2 = q_ref[:, j * LANES:(j + 1) * LANES]
        zero = jnp.zeros_like(q2)
        qs += [jnp.where(lo, q2, zero), jnp.where(lo, zero, q2)]
    q4 = jnp.concatenate(qs, axis=0)
    rows = 4 * tq
    m = jnp.full((rows, 1), -jnp.inf, F32)
    acc = jnp.zeros((rows, LANES), F32)
    chunks = [(kr, vr, c, _key_chunk(kr.shape[0])) for kr, vr in sources
              for c in range(kr.shape[0] // _key_chunk(kr.shape[0]))]
    for kr, vr, c, kc in chunks:
        kch = kr[c * kc:(c + 1) * kc, :].astype(BF16)
        vch = vr[c * kc:(c + 1) * kc, :].astype(BF16)
        lane_k = lax.broadcasted_iota(jnp.int32, (kc, LANES), 1)
        vch = jnp.where(lane_k < HEAD_DIM, vch, jnp.ones_like(vch))
        s = _bdot_nt(q4, kch)
        m_new = jnp.maximum(m, jnp.max(s, axis=-1, keepdims=True))
        alpha = jnp.exp2(m - m_new)
        p = jnp.exp2((s - m_new).astype(BF16))
        acc = acc * alpha + jnp.dot(p, vch, preferred_element_type=F32)
        m = m_new
    o = acc * (1.0 / pltpu.roll(acc, HEAD_DIM, 1))
    for j in range(2):
        oa = o[(2 * j) * tq:(2 * j + 1) * tq]
        ob = pltpu.roll(o[(2 * j + 1) * tq:(2 * j + 2) * tq], HEAD_DIM, 1)
        o_ref[:, j * LANES:(j + 1) * LANES] = jnp.where(lo, oa, ob).astype(o_ref.dtype)


def _attn_call(q, k, v, kctx, vctx, *, tq):
    b, t, _ = q.shape
    tk = k.shape[1]
    nq = t // tq
    has_ctx = kctx is not None
    kv_spec = lambda n: pl.BlockSpec((None, n, LANES), lambda bi, g, i: (bi, 0, g))
    in_specs = [pl.BlockSpec((None, tq, 2 * LANES), lambda bi, g, i: (bi, i, g)), kv_spec(tk), kv_spec(tk)]
    args = [q, k, v]
    if has_ctx:
        in_specs += [kv_spec(kctx.shape[1]), kv_spec(kctx.shape[1])]
        args += [kctx, vctx]
    return pl.pallas_call(
        functools.partial(_attn_kernel, has_ctx=has_ctx),
        out_shape=jax.ShapeDtypeStruct((b, t, ATTN_W), BF16),
        grid=(b, N_KV_HEADS, nq),
        in_specs=in_specs,
        out_specs=pl.BlockSpec((None, tq, 2 * LANES), lambda bi, g, i: (bi, i, g)),
        compiler_params=_cparams(("arbitrary", "arbitrary", "arbitrary")),
        name="attn",
    )(*args)


LOG2E = 1.4426950408889634


def _softplus(x):
    return jnp.maximum(x, 0.0) + jnp.log(1.0 + jnp.exp(-jnp.abs(x)))


def _ssd_kernel(*refs, nc, reverse, has_h0, want_hfin):
    refs = list(refs)
    conv = not reverse
    if conv:
        xbc_ref, prev_ref, next_ref, cw_ref, cb_ref, dsk_ref = refs[:6]
        refs = refs[6:]
    else:
        xc_ref = refs.pop(0)
    dt_ref, an_ref, dtb_ref, tri_ref = refs[:4]
    refs = refs[4:]
    h0_ref = refs.pop(0) if has_h0 else None
    hprev_ref = refs.pop(0) if (want_hfin and reverse) else None
    y_ref = refs.pop(0)
    xco_ref = refs.pop(0) if conv else None
    hfin_ref = refs.pop(0) if want_hfin else None
    h_scr = refs.pop(0)

    L = CHUNK
    c = pl.program_id(1)
    cidx = (nc - 1 - c) if reverse else c

    @pl.when(c == 0)
    def _():
        if has_h0:
            h_scr[...] = h0_ref[...]
        else:
            h_scr[...] = jnp.zeros_like(h_scr)

    row = lax.broadcasted_iota(jnp.int32, (L, LANES), 0)
    lane = lax.broadcasted_iota(jnp.int32, (L, LANES), 1)
    lo = lane < SSD_HEAD_DIM
    top = row < SSD_HEAD_DIM

    if conv:
        first = cidx == 0
        last = cidx == nc - 1

        def cols(a, w):
            xm = xbc_ref[:, a:a + w].astype(F32)
            rw = lax.broadcasted_iota(jnp.int32, (L, w), 0)
            p6 = jnp.where(first, 0.0, prev_ref[6:7, a:a + w].astype(F32))
            p7 = jnp.where(first, 0.0, prev_ref[7:8, a:a + w].astype(F32))
            n0 = jnp.where(last, 0.0, next_ref[0:1, a:a + w].astype(F32))
            r1 = jnp.where(rw == 0, p7, pltpu.roll(xm, 1, 0))
            r2 = jnp.where(rw == 0, p6, jnp.where(rw == 1, p7, pltpu.roll(xm, 2, 0)))
            rn = jnp.where(rw == L - 1, n0, pltpu.roll(xm, L - 1, 0))
            y = (r2 * cw_ref[0:1, a:a + w] + r1 * cw_ref[1:2, a:a + w] + xm * cw_ref[2:3, a:a + w]
                 + rn * cw_ref[3:4, a:a + w] + cb_ref[:, a:a + w])
            y = _silu(y).astype(BF16)
            xco_ref[:, a:a + w] = y
            return y
    else:
        def cols(a, w):
            return xc_ref[:, a:a + w]

    causal = tri_ref[...] > 0.0
    dt = _softplus(dt_ref[...] + dtb_ref[...])
    la2 = dt * (an_ref[...] * LOG2E)
    acum2 = jnp.dot(tri_ref[...], la2, preferred_element_type=F32, precision=lax.Precision.HIGHEST)
    dt_t = dt.T
    acum2_t = acum2.T
    tot2_t = jnp.sum(la2.T, axis=1, keepdims=True)
    lg_dt_t = jnp.log2(dt_t)
    r_t = lg_dt_t - acum2_t
    w_t = jnp.exp2(lg_dt_t + tot2_t - acum2_t)
    e_acum = jnp.exp2(acum2)
    e_tot_t = jnp.exp2(tot2_t)

    for g in range(SSD_GROUPS):
        bgb = cols(D_INNER + g * D_STATE, D_STATE)
        cgb = cols(D_INNER + SSD_GROUPS * D_STATE + g * D_STATE, D_STATE)
        cbm = _bdot_nt(cgb, bgb)
        h_grp = h_scr[4 * g:4 * g + 4]
        yo_grp = _bdot_nt(cgb, h_grp.reshape(4 * LANES, D_STATE))
        for pr in range(4):
            hp = g * 4 + pr
            ha, hb = 2 * hp, 2 * hp + 1
            xpb = cols(hp * LANES, LANES)
            zero = jnp.zeros_like(xpb)
            xs = jnp.concatenate([jnp.where(lo, xpb, zero), jnp.where(lo, zero, xpb)], axis=0)
            ms = []
            for hh in (ha, hb):
                e = jnp.exp2(acum2[:, hh:hh + 1] + r_t[hh:hh + 1, :])
                ms.append((cbm * jnp.where(causal, e, 0.0)).astype(BF16))
            y = jnp.dot(jnp.concatenate(ms, axis=1), xs, preferred_element_type=F32)
            y = y + yo_grp[:, pr * LANES:(pr + 1) * LANES] * \
                jnp.where(lo, e_acum[:, ha:ha + 1], e_acum[:, hb:hb + 1])
            if conv:
                y = y + dsk_ref[:, hp * LANES:(hp + 1) * LANES] * xpb.astype(F32)
            y_ref[:, hp * LANES:(hp + 1) * LANES] = y.astype(y_ref.dtype)
            wsel = jnp.where(top, w_t[ha:ha + 1, :], w_t[hb:hb + 1, :])
            st = jnp.dot((xpb.astype(F32).T * wsel).astype(BF16), bgb, preferred_element_type=F32)
            cd = jnp.where(top, e_tot_t[ha:ha + 1, :], e_tot_t[hb:hb + 1, :])
            h_scr[hp] = h_grp[pr] * cd + st

    if want_hfin:
        @pl.when(c == nc - 1)
        def _():
            if reverse:
                hfin_ref[0] = hprev_ref[...]
                hfin_ref[1] = h_scr[...]
            else:
                hfin_ref[...] = h_scr[...]


def _ssd_sweep(xin, dt, wts, h0, hprev, *, reverse, want_hfin):
    b, t, _ = xin.shape
    nc = t // CHUNK
    has_h0 = h0 is not None
    rb = CHUNK // 8
    nrb = t // 8
    d = 1 if reverse else 0
    cmap = (lambda c: nc - 1 - c) if reverse else (lambda c: c)
    hshape = (SSD_HEADS // 2, 2 * SSD_HEAD_DIM, D_STATE)

    chunk_spec = pl.BlockSpec((None, CHUNK, CONV_CH), lambda bi, c: (bi, cmap(c), 0))
    if reverse:
        in_specs = [chunk_spec]
        args = [xin]
    else:
        in_specs = [
            chunk_spec,
            pl.BlockSpec((None, 8, CONV_CH), lambda bi, c: (bi, jnp.maximum(c * rb - 1, 0), 0)),
            pl.BlockSpec((None, 8, CONV_CH), lambda bi, c: (bi, jnp.minimum((c + 1) * rb, nrb - 1), 0)),
            pl.BlockSpec((D_CONV, CONV_CH), lambda bi, c: (0, 0)),
            pl.BlockSpec((1, CONV_CH), lambda bi, c: (0, 0)),
            pl.BlockSpec((1, D_INNER), lambda bi, c: (0, 0)),
        ]
        args = [xin, xin, xin, wts["conv_w"], wts["conv_b"], wts["dskip"]]
    in_specs += [
        pl.BlockSpec((None, CHUNK, LANES), lambda bi, c: (bi, cmap(c), d)),
        pl.BlockSpec((None, 1, LANES), lambda bi, c: (d, 0, 0)),
        pl.BlockSpec((None, 1, LANES), lambda bi, c: (d, 0, 0)),
        pl.BlockSpec((None, CHUNK, CHUNK), lambda bi, c: (d, 0, 0)),
    ]
    args += [dt, wts["a_neg"], wts["dt_bias"], wts["tri"]]
    if has_h0:
        in_specs.append(pl.BlockSpec((None, None) + hshape, lambda bi, c: (bi, d, 0, 0, 0)))
        args.append(h0)
    if want_hfin and reverse:
        in_specs.append(pl.BlockSpec((None,) + hshape, lambda bi, c: (bi, 0, 0, 0)))
        args.append(hprev)
    out_shape = [jax.ShapeDtypeStruct((b, t, D_INNER), BF16)]
    out_specs = [pl.BlockSpec((None, CHUNK, D_INNER), lambda bi, c: (bi, cmap(c), 0))]
    if not reverse:
        out_shape.append(jax.ShapeDtypeStruct((b, t, CONV_CH), BF16))
        out_specs.append(pl.BlockSpec((None, CHUNK, CONV_CH), lambda bi, c: (bi, c, 0)))
    if want_hfin and reverse:
        out_shape.append(jax.ShapeDtypeStruct((b, 2) + hshape, F32))
        out_specs.append(pl.BlockSpec((None, 2) + hshape, lambda bi, c: (bi, 0, 0, 0, 0)))
    elif want_hfin:
        out_shape.append(jax.ShapeDtypeStruct((b,) + hshape, F32))
        out_specs.append(pl.BlockSpec((None,) + hshape, lambda bi, c: (bi, 0, 0, 0)))
    return pl.pallas_call(
        functools.partial(_ssd_kernel, nc=nc, reverse=reverse, has_h0=has_h0, want_hfin=want_hfin),
        out_shape=tuple(out_shape),
        grid=(b, nc),
        in_specs=in_specs,
        out_specs=tuple(out_specs),
        scratch_shapes=[pltpu.VMEM(hshape, F32)],
        compiler_params=_cparams(("arbitrary", "arbitrary")),
        name="ssd_bwd" if reverse else "ssd_fwd",
    )(*args)


def _ssd_call(xbc, dt, wts, h0, *, want_hfin):
    res = _ssd_sweep(xbc, dt, wts, h0, None, reverse=False, want_hfin=want_hfin)
    y_f, xc = res[0], res[1]
    hf = res[2] if want_hfin else None
    res = _ssd_sweep(xc, dt, wts, h0, hf, reverse=True, want_hfin=want_hfin)
    return y_f, res[0], (res[1] if want_hfin else None)


def _route(logits_t, bias_col):
    e, n = logits_t.shape
    per = e // N_EXPERT_GROUPS
    scores = jax.nn.sigmoid(logits_t)
    sel = scores + bias_col
    neg = jnp.float32(-jnp.inf)
    gs = []
    for g in range(N_EXPERT_GROUPS):
        blk = sel[g * per:(g + 1) * per, :]
        m1 = jnp.max(blk, axis=0, keepdims=True)
        is_m1 = blk == m1
        cnt = jnp.sum(jnp.where(is_m1, 1.0, 0.0), axis=0, keepdims=True)
        m2 = jnp.max(jnp.where(is_m1, neg, blk), axis=0, keepdims=True)
        gs.append(m1 + jnp.where(cnt >= 2.0, m1, m2))
    keep = []
    for g in range(N_EXPERT_GROUPS):
        rank = jnp.zeros_like(gs[g])
        for j in range(N_EXPERT_GROUPS):
            if j == g:
                continue
            beats = (gs[j] > gs[g]) if j > g else (gs[j] >= gs[g])
            rank = rank + jnp.where(beats, 1.0, 0.0)
        keep.append(rank < float(TOPK_GROUPS))
    selm = jnp.concatenate(
        [jnp.where(keep[g], sel[g * per:(g + 1) * per, :], neg) for g in range(N_EXPERT_GROUPS)], axis=0)
    eidx = lax.broadcasted_iota(jnp.int32, (e, n), 0).astype(F32)
    cur = selm
    picks = []
    for _ in range(TOP_K):
        m = jnp.max(cur, axis=0, keepdims=True)
        idx = jnp.min(jnp.where(cur == m, eidx, float(e)), axis=0, keepdims=True)
        hit = eidx == idx
        picks.append((idx, hit))
        cur = jnp.where(hit, neg, cur)
    return scores, picks


def _pack_bf16_pairs(h):
    c = h.shape[1] // 2
    lo = pltpu.bitcast(h[:, :c].astype(BF16).astype(F32), jnp.uint32)
    hi = pltpu.bitcast(h[:, c:].astype(BF16).astype(F32), jnp.uint32)
    return (lo >> 16) | (hi & jnp.uint32(0xFFFF0000))


def _unpack_bf16_pairs(w):
    lo = pltpu.bitcast(w << 16, F32)
    hi = pltpu.bitcast(w & jnp.uint32(0xFFFF0000), F32)
    return lo, hi


def _rows8(rows):
    n = rows[0].shape[1]
    ridx = lax.broadcasted_iota(jnp.int32, (TOP_K, n), 0)
    out = jnp.zeros((TOP_K, n), rows[0].dtype)
    for k, r in enumerate(rows):
        out = jnp.where(ridx == k, r, out)
    return out


MERGE_SUB = 256


def _merge_kernel(x_ref, attn_ref, yf_ref, yb_ref, z_ref, gates_ref, mod_ref, wa_ref, ws_ref, wo_ref,
                  sg_ref, n2_ref, wr_ref, rb_ref, wsg_ref, wsu_ref, wsd_ref,
                  xb_ref, ha_ref, hb_ref, eid_ref, pos_ref, wk_ref, cnt_ref):
    tm = x_ref.shape[0]
    sub = MERGE_SUB

    @pl.when(pl.program_id(0) == 0)
    def _():
        cnt_ref[...] = jnp.zeros_like(cnt_ref)

    r_i = lax.broadcasted_iota(jnp.int32, (sub, sub), 0)
    c_i = lax.broadcasted_iota(jnp.int32, (sub, sub), 1)
    before = jnp.where(r_i < c_i, 1.0, 0.0).astype(BF16)
    cnt = cnt_ref[:, 0:1]

    for r0 in range(0, tm, sub):
        rs = slice(r0, r0 + sub)
        x = x_ref[rs, :]
        yy = yf_ref[rs, :].astype(F32) + yb_ref[rs, :].astype(F32)
        u = yy * _silu(z_ref[rs, :]).astype(F32)
        un = u * lax.rsqrt(jnp.mean(u * u, axis=-1, keepdims=True) + EPS) * sg_ref[...]
        ssd_o = _bdot(un, ws_ref[...])
        attn_o = jnp.dot(attn_ref[rs, :], wa_ref[...], preferred_element_type=F32)
        ga = jax.nn.sigmoid(gates_ref[rs, 0:D_MODEL]).astype(F32)
        gs = jax.nn.sigmoid(gates_ref[rs, D_MODEL:2 * D_MODEL]).astype(F32)
        mix = _bdot(ga * attn_o + gs * ssd_o, wo_ref[...])
        x1 = x + mod_ref[:, 2048:3072] * mix
        h2 = x1 * lax.rsqrt(jnp.mean(x1 * x1, axis=-1, keepdims=True) + EPS) * n2_ref[...]
        h2 = h2 * (1.0 + mod_ref[:, 4096:5120]) + mod_ref[:, 3072:4096]
        h2b = h2.astype(BF16)
        ha_ref[rs, :] = _pack_bf16_pairs(h2[:, :D_MODEL // 2])
        hb_ref[rs, :] = _pack_bf16_pairs(h2[:, D_MODEL // 2:])

        logits_t = _bdot_nt(wr_ref[...], h2b)
        scores, picks = _route(logits_t, rb_ref[...])
        chosen = jnp.zeros_like(scores)
        for _, hit in picks:
            chosen = chosen + jnp.where(hit, 1.0, 0.0)
        pos = cnt + jnp.dot(chosen.astype(BF16), before, preferred_element_type=F32)
        cnt = cnt + jnp.sum(chosen, axis=1, keepdims=True)
        poss = [jnp.sum(jnp.where(hit, pos, 0.0), axis=0, keepdims=True) for _, hit in picks]
        wks = [jnp.sum(jnp.where(hit, scores, 0.0), axis=0, keepdims=True) for _, hit in picks]
        wsum = wks[0]
        for w in wks[1:]:
            wsum = wsum + w
        eid_ref[:, rs] = _rows8([idx for idx, _ in picks]).astype(jnp.int32)
        pos_ref[:, rs] = _rows8(poss).astype(jnp.int32)
        wk8 = _rows8(wks) / wsum * ROUTED_SCALE
        wk_ref[rs, :] = jnp.concatenate([wk8, jnp.zeros((LANES - TOP_K, sub), F32)], axis=0).T

        hid = _silu(jnp.dot(h2b, wsg_ref[...], preferred_element_type=F32)) * \
            jnp.dot(h2b, wsu_ref[...], preferred_element_type=F32)
        xb_ref[rs, :] = x1 + mod_ref[:, 5120:6144] * _bdot(hid, wsd_ref[...])

    cnt_ref[...] = jnp.broadcast_to(cnt, cnt_ref.shape)


def _merge_call(x, attn, y_f, y_b, z, gates, mod_rows, wa, ws, wo, sg, n2, wr_t, rb, wsg, wsu, wsd, *, tm):
    b, t, _ = x.shape
    n = b * t
    flat = lambda a: a.reshape(n, a.shape[-1])
    tok = lambda width: pl.BlockSpec((tm, width), lambda i: (i, 0))
    const2 = lambda shape: pl.BlockSpec(shape, lambda i: (0, 0), pipeline_mode=pl.Buffered(1))
    k8 = pl.BlockSpec((TOP_K, tm), lambda i: (0, i))
    half = D_MODEL // 4
    return pl.pallas_call(
        _merge_kernel,
        out_shape=(jax.ShapeDtypeStruct((n, D_MODEL), F32),
                   jax.ShapeDtypeStruct((n, half), jnp.uint32),
                   jax.ShapeDtypeStruct((n, half), jnp.uint32),
                   jax.ShapeDtypeStruct((TOP_K, n), jnp.int32),
                   jax.ShapeDtypeStruct((TOP_K, n), jnp.int32),
                   jax.ShapeDtypeStruct((n, LANES), F32),
                   jax.ShapeDtypeStruct((N_EXPERTS, LANES), F32)),
        grid=(n // tm,),
        in_specs=[tok(D_MODEL), tok(ATTN_W), tok(D_INNER), tok(D_INNER), tok(D_INNER), tok(2 * D_MODEL),
                  pl.BlockSpec((None, 1, 6 * D_MODEL), lambda i: ((i * tm) // t, 0, 0)),
                  const2((ATTN_W, D_MODEL)), const2((D_INNER, D_MODEL)), const2((D_MODEL, D_MODEL)),
                  const2((1, D_INNER)), const2((1, D_MODEL)),
                  const2((N_EXPERTS, D_MODEL)), const2((N_EXPERTS, 1)),
                  const2((D_MODEL, D_SHARED)), const2((D_MODEL, D_SHARED)), const2((D_SHARED, D_MODEL))],
        out_specs=(tok(D_MODEL), tok(half), tok(half), k8, k8, tok(LANES),
                   pl.BlockSpec((N_EXPERTS, LANES), lambda i: (0, 0))),
        compiler_params=_cparams(("arbitrary",)),
        name="merge",
    )(flat(x), flat(attn), flat(y_f), flat(y_b), flat(z), flat(gates), mod_rows, wa, ws, wo, sg, n2, wr_t, rb,
      wsg, wsu, wsd)


ROW_TILE = 512
SC_WINDOW = 128


def _slots_kernel(start_ref, eid_ref, pos_ref, slot_ref):
    eid = eid_ref[...]
    slot = pos_ref[...]
    for e in range(N_EXPERTS):
        slot = slot + jnp.where(eid == e, start_ref[e], 0)
    slot_ref[...] = slot


def _slots_call(start, eid, pos):
    n = eid.shape[1]
    bn = 2048 if n % 2048 == 0 else n
    spec = pl.BlockSpec((TOP_K, bn), lambda i, s: (0, i))
    return pl.pallas_call(
        _slots_kernel,
        out_shape=jax.ShapeDtypeStruct((TOP_K, n), jnp.int32),
        grid_spec=pltpu.PrefetchScalarGridSpec(num_scalar_prefetch=1, grid=(n // bn,),
                                               in_specs=[spec, spec], out_specs=spec),
        compiler_params=_cparams(("arbitrary",)),
        name="slots",
    )(start, eid, pos)


def _sc_dispatch(x, slots, p):
    n, d = x.shape
    mesh = plsc.VectorSubcoreMesh(core_axis_name="core", subcore_axis_name="subcore")

    @functools.partial(pl.kernel, out_type=jax.ShapeDtypeStruct((p, d), x.dtype), mesh=mesh)
    def k(x_hbm, s_hbm, o_hbm):
        def body(x_vmem, s_vmem):
            for kk in range(TOP_K):
                pltpu.sync_copy(x_vmem, o_hbm.at[s_vmem.at[kk]])

        pltpu.emit_pipeline(
            body,
            grid=(n // SC_WINDOW,),
            in_specs=[pl.BlockSpec((SC_WINDOW, d), index_map=lambda i: (i, 0)),
                      pl.BlockSpec((TOP_K, SC_WINDOW), index_map=lambda i: (0, i))],
            out_specs=[],
            core_axis_name=("core", "subcore"),
            dimension_semantics=(pltpu.PARALLEL,),
        )(x_hbm, s_hbm)

    return k(x, slots)


def _sc_combine(y, slots):
    kk, n = slots.shape
    d = y.shape[1]
    mesh = plsc.VectorSubcoreMesh(core_axis_name="core", subcore_axis_name="subcore")

    @functools.partial(pl.kernel, out_type=jax.ShapeDtypeStruct((kk * n, d), y.dtype), mesh=mesh)
    def k(y_hbm, s_hbm, o_hbm):
        def body(s_vmem, o_vmem):
            pltpu.sync_copy(y_hbm.at[s_vmem.at[0]], o_vmem)

        pltpu.emit_pipeline(
            body,
            grid=(kk * n // SC_WINDOW,),
            in_specs=[pl.BlockSpec((1, SC_WINDOW), index_map=lambda i: (0, i))],
            out_specs=[pl.BlockSpec((SC_WINDOW, d), index_map=lambda i: (i, 0))],
            core_axis_name=("core", "subcore"),
            dimension_semantics=(pltpu.PARALLEL,),
        )(s_hbm, o_hbm)

    return k(y, slots.reshape(1, kk * n)).reshape(kk, n, d)


def _ffn_kernel(st_ref, nt_ref, wg_ref, wu_ref, wd_ref, xa_hbm, xb_hbm, ya_hbm, yb_hbm,
                wg_s, wu_s, wd_s, xa_buf, xb_buf, ya_buf, yb_buf, in_sem, out_sem):
    e = pl.program_id(0)
    ne = pl.num_programs(0)
    n = nt_ref[e]
    t = ROW_TILE

    def fetch(row, slot):
        r = pl.multiple_of(row, t)
        return (pltpu.make_async_copy(xa_hbm.at[pl.ds(r, t)], xa_buf.at[slot], in_sem.at[0, slot]),
                pltpu.make_async_copy(xb_hbm.at[pl.ds(r, t)], xb_buf.at[slot], in_sem.at[1, slot]))

    def put(row, slot):
        r = pl.multiple_of(row, t)
        return (pltpu.make_async_copy(ya_buf.at[slot], ya_hbm.at[pl.ds(r, t)], out_sem.at[0, slot]),
                pltpu.make_async_copy(yb_buf.at[slot], yb_hbm.at[pl.ds(r, t)], out_sem.at[1, slot]))

    @pl.when(jnp.logical_and(e == 0, n > 0))
    def _():
        for c in fetch(st_ref[0], 0):
            c.start()

    @pl.when(n > 0)
    def _():
        wg_s[...] = wg_ref[...].astype(BF16)
        wu_s[...] = wu_ref[...].astype(BF16)
        wd_s[...] = wd_ref[...].astype(BF16)
        base = st_ref[e]

        def body(i, carry):
            slot = lax.rem(i, 2)
            row = base + i * t
            for c in fetch(row, slot):
                c.wait()

            @pl.when(i + 1 < n)
            def _():
                for c in fetch(row + t, 1 - slot):
                    c.start()

            @pl.when(i >= 2)
            def _():
                for c in put(row - 2 * t, slot):
                    c.wait()

            parts = _unpack_bf16_pairs(xa_buf[slot]) + _unpack_bf16_pairs(xb_buf[slot])
            x = jnp.concatenate(parts, axis=1).astype(BF16)
            hid = _silu(jnp.dot(x, wg_s[...], preferred_element_type=F32)) * \
                jnp.dot(x, wu_s[...], preferred_element_type=F32)
            y = _bdot(hid, wd_s[...])
            ya_buf[slot] = _pack_bf16_pairs(y[:, :D_MODEL // 2])
            yb_buf[slot] = _pack_bf16_pairs(y[:, D_MODEL // 2:])
            for c in put(row, slot):
                c.start()
            return carry

        lax.fori_loop(0, n, body, 0)

        @pl.when(n >= 2)
        def _():
            for c in put(base + (n - 2) * t, lax.rem(n - 2, 2)):
                c.wait()

        for c in put(base + (n - 1) * t, lax.rem(n - 1, 2)):
            c.wait()

    nxt = jnp.minimum(e + 1, ne - 1)

    @pl.when(jnp.logical_and(e + 1 < ne, nt_ref[nxt] > 0))
    def _():
        for c in fetch(st_ref[nxt], 0):
            c.start()


def _ffn_call(start, tiles_e, xa, xb, wg, wu, wd):
    p, half = xa.shape
    wspec = lambda s: pl.BlockSpec((None,) + s, lambda e, st, nt: (e, 0, 0))
    hbm = pl.BlockSpec(memory_space=pl.ANY)
    buf = pltpu.VMEM((2, ROW_TILE, half), jnp.uint32)
    return pl.pallas_call(
        _ffn_kernel,
        out_shape=(jax.ShapeDtypeStruct((p, half), jnp.uint32), jax.ShapeDtypeStruct((p, half), jnp.uint32)),
        grid_spec=pltpu.PrefetchScalarGridSpec(
            num_scalar_prefetch=2, grid=(N_EXPERTS,),
            in_specs=[wspec((D_MODEL, D_EXPERT)), wspec((D_MODEL, D_EXPERT)), wspec((D_EXPERT, D_MODEL)),
                      hbm, hbm],
            out_specs=(hbm, hbm),
            scratch_shapes=[pltpu.VMEM((D_MODEL, D_EXPERT), BF16), pltpu.VMEM((D_MODEL, D_EXPERT), BF16),
                            pltpu.VMEM((D_EXPERT, D_MODEL), BF16), buf, buf, buf, buf,
                            pltpu.SemaphoreType.DMA((2, 2)), pltpu.SemaphoreType.DMA((2, 2))]),
        compiler_params=_cparams(("arbitrary",)),
        name="ffn",
    )(start, tiles_e, wg, wu, wd, xa, xb)


def _final_kernel(xb_ref, ya_ref, yb_ref, wk_ref, mod_ref, o_ref):
    q = D_MODEL // 4
    accs = [jnp.zeros((xb_ref.shape[0], q), F32) for _ in range(4)]
    for k in range(TOP_K):
        w = wk_ref[:, k:k + 1]
        parts = _unpack_bf16_pairs(ya_ref[k]) + _unpack_bf16_pairs(yb_ref[k])
        accs = [a + w * p for a, p in zip(accs, parts)]
    for i, a in enumerate(accs):
        o_ref[:, i * q:(i + 1) * q] = xb_ref[:, i * q:(i + 1) * q] + mod_ref[:, 5120 + i * q:5120 + (i + 1) * q] * a


def _final_call(xb, ya, yb, wk, mod_rows, t, *, tm):
    n = xb.shape[0]
    half = ya.shape[2]
    tok = lambda width: pl.BlockSpec((tm, width), lambda i: (i, 0))
    yspec = pl.BlockSpec((TOP_K, tm, half), lambda i: (0, i, 0))
    return pl.pallas_call(
        _final_kernel,
        out_shape=jax.ShapeDtypeStruct((n, D_MODEL), F32),
        grid=(n // tm,),
        in_specs=[tok(D_MODEL), yspec, yspec, tok(LANES),
                  pl.BlockSpec((None, 1, 6 * D_MODEL), lambda i: ((i * tm) // t, 0, 0))],
        out_specs=tok(D_MODEL),
        compiler_params=_cparams(("arbitrary",)),
        name="final",
    )(xb, ya, yb, wk, mod_rows)


def _moe_call(ha, hb, eid, pos, wk, counts, xb, mod_rows, wg, wu, wd, t):
    n = xb.shape[0]
    max_tiles = n * TOP_K // ROW_TILE + N_EXPERTS
    p = max_tiles * ROW_TILE
    cnt = counts[:, 0].astype(jnp.int32)
    tiles_e = (cnt + ROW_TILE - 1) // ROW_TILE
    ends = jnp.cumsum(tiles_e)
    start = (ends - tiles_e) * ROW_TILE
    slots = _slots_call(start.astype(jnp.int32), eid, pos)
    xa = _sc_dispatch(ha, slots, p)
    xbb = _sc_dispatch(hb, slots, p)
    ya, yb = _ffn_call(start.astype(jnp.int32), tiles_e.astype(jnp.int32), xa, xbb, wg, wu, wd)
    ga = _sc_combine(ya, slots)
    gb = _sc_combine(yb, slots)
    return _final_call(xb, ga, gb, wk, mod_rows, t, tm=256)


def _rope_tables(t):
    n_rows = t // GRID_W
    rows = jnp.repeat(jnp.arange(n_rows), GRID_W).astype(F32)
    cols = jnp.tile(jnp.arange(GRID_W), n_rows).astype(F32)
    n_freq = HEAD_DIM // 4
    freqs = ROPE_THETA ** (-jnp.arange(n_freq, dtype=F32) / n_freq)
    ang = jnp.concatenate([rows[:, None] * freqs, cols[:, None] * freqs], axis=-1)
    ang = jnp.repeat(ang, 2, axis=-1)
    ang = jnp.concatenate([ang, ang], axis=-1)
    sign = jnp.where(jnp.arange(LANES) % 2 == 0, -1.0, 1.0).astype(F32)
    return jnp.cos(ang), jnp.sin(ang) * sign


def _dup_heads(a):
    s = a.shape[:-1]
    a4 = a.reshape(s + (N_KV_HEADS, HEAD_DIM))
    return jnp.concatenate([a4, a4], axis=-1).reshape(s + (KVD_W,))


def _prep_w_in(w_in):
    idx = np.cumsum(SPLIT_SIZES)[:-1].tolist()
    q, k, v, z, xbc, dt, gates = jnp.split(w_in, idx, axis=-1)
    pad = jnp.zeros((D_MODEL, LANES - SSD_HEADS), w_in.dtype)
    cols = [q, _dup_heads(k), _dup_heads(v), gates, z, xbc,
            dt[:, :SSD_HEADS], pad, dt[:, SSD_HEADS:], pad]
    return jnp.concatenate(cols, axis=-1).astype(BF16)


def _pad_heads(a):
    return jnp.pad(a.astype(F32), ((0, 0), (0, LANES - SSD_HEADS)))[:, None, :]


def _trunk(x, mod_rows, wts, rope_tabs, ctx_k, ctx_v, h0, *, tm, tq, want_state):
    b, t, _ = x.shape
    rope = rope_tabs is not None
    if rope:
        cos, sin = rope_tabs
    else:
        cos = sin = jnp.zeros((t, LANES), F32)
    kv_dtype = BF16 if rope else F32
    q, k, v, gates, z, xbc, dt = _inproj_call(x, mod_rows, wts["g1"], wts["w_in"], wts["qg"], wts["kg"],
                                              cos, sin, rope=rope, kv_dtype=kv_dtype, tm=tm)
    attn = _attn_call(q, k, v, ctx_k, ctx_v, tq=tq)
    y_f, y_b, hfin = _ssd_call(xbc, dt, wts, h0, want_hfin=want_state)
    xb, ha, hb, eid, pos, wk, counts = _merge_call(
        x, attn, y_f, y_b, z, gates, mod_rows, wts["wa"], wts["ws"], wts["wo"], wts["sg"],
        wts["n2"], wts["wr_t"], wts["rb"], wts["wsg"], wts["wsu"], wts["wsd"], tm=MERGE_SUB)
    out = _moe_call(ha, hb, eid, pos, wk, counts, xb, mod_rows, wts["weg"], wts["weu"], wts["wed"], t)
    return out.reshape(b, t, D_MODEL), k, v, hfin


def kernel(x_prompt, x_sample, cache_k, cache_v, state_ssm, c, c_ctx, w_mod, b_mod, norm1_g, norm2_g, w_in,
           q_norm_g, k_norm_g, conv_w, conv_b, a_log, dt_bias, d_skip, ssd_norm_g, w_attn_proj, w_ssd_proj,
           w_out, w_router, router_bias, w_exp_gate, w_exp_up, w_exp_down, w_sh_gate, w_sh_up, w_sh_down):
    depth = w_mod.shape[0]
    assert depth == 1, "single trunk layer"
    bp, tp, _ = x_prompt.shape
    bs, ts, _ = x_sample.shape
    l = 0
    cvec = jnp.concatenate([c_ctx[None, :], c, jnp.zeros((8 - 1 - bs, D_MODEL), F32)], axis=0)
    mod = _mod_call(cvec, w_mod.reshape(D_MODEL, 6 * D_MODEL), b_mod.reshape(1, 6 * D_MODEL))
    mod_prompt = jnp.broadcast_to(mod[0:1][:, None, :], (bp, 1, 6 * D_MODEL))
    mod_sample = mod[1:1 + bs][:, None, :]

    lower = np.tril(np.ones((CHUNK, CHUNK), np.float32))
    wts = dict(
        g1=norm1_g[l][None, :], n2=norm2_g[l][None, :],
        w_in=_prep_w_in(w_in.reshape(D_MODEL, w_in.shape[-1])),
        qg=jnp.tile(q_norm_g[l], 2)[None, :], kg=jnp.tile(k_norm_g[l], 2)[None, :],
        conv_w=conv_w[l], conv_b=conv_b[l][None, :],
        a_neg=_pad_heads(-jnp.exp(a_log[l].astype(F32))), dt_bias=_pad_heads(dt_bias[l]),
        dskip=jnp.repeat(d_skip[l].astype(F32), SSD_HEAD_DIM)[None, :],
        tri=jnp.asarray(np.stack([lower, lower.T])),
        sg=ssd_norm_g[l][None, :],
        wa=w_attn_proj[l].astype(BF16), ws=w_ssd_proj[l].astype(BF16), wo=w_out[l].astype(BF16),
        wr_t=w_router[l].T.astype(BF16), rb=router_bias[l].astype(F32)[:, None],
        wsg=w_sh_gate[l].astype(BF16), wsu=w_sh_up[l].astype(BF16), wsd=w_sh_down[l].astype(BF16),
        weg=w_exp_gate.reshape(w_exp_gate.shape[1:]), weu=w_exp_up.reshape(w_exp_up.shape[1:]),
        wed=w_exp_down.reshape(w_exp_down.shape[1:]),
    )

    y_prompt, k_p, v_p, hfin = _trunk(x_prompt, mod_prompt, wts, None, None, None, None,
                                      tm=256, tq=256, want_state=True)
    new_k = k_p.reshape(bp, tp, N_KV_HEADS, LANES)[..., :HEAD_DIM][:, None]
    new_v = v_p.reshape(bp, tp, N_KV_HEADS, LANES)[..., :HEAD_DIM][:, None]
    new_state = hfin.reshape(bp, 1, 2, SSD_HEADS, SSD_HEAD_DIM, D_STATE)

    past = cache_k.shape[2]
    ctx_k = _dup_heads(cache_k[:, l].reshape(bs, past, KV_W)).astype(BF16)
    ctx_v = _dup_heads(cache_v[:, l].reshape(bs, past, KV_W)).astype(BF16)
    h0 = state_ssm[:, l].reshape(bs, 2, SSD_HEADS // 2, 2 * SSD_HEAD_DIM, D_STATE)
    y_sample, _, _, _ = _trunk(x_sample, mod_sample, wts, _rope_tables(ts), ctx_k, ctx_v, h0,
                               tm=256, tq=256, want_state=False)
    return (y_prompt, y_sample, new_k, new_v, new_state)
```

```python
import functools

import numpy as np
import jax
import jax.numpy as jnp
from jax import lax
from jax.experimental import pallas as pl
from jax.experimental.pallas import tpu as pltpu
from jax.experimental.pallas import tpu_sc as plsc

F32 = jnp.float32
BF16 = jnp.bfloat16

D_MODEL = 1024
GRID_W = 64
EPS = 1e-6
N_HEADS = 16
N_KV_HEADS = 4
HEAD_DIM = 64
ATTN_W = N_HEADS * HEAD_DIM
KV_W = N_KV_HEADS * HEAD_DIM
ROPE_THETA = 10000.0
D_INNER = 2048
SSD_HEAD_DIM = 64
SSD_HEADS = 32
SSD_GROUPS = 4
D_STATE = 128
D_CONV = 4
CHUNK = 128
CONV_CH = D_INNER + 2 * SSD_GROUPS * D_STATE
N_EXPERTS = 64
TOP_K = 8
N_EXPERT_GROUPS = 8
TOPK_GROUPS = 4
D_EXPERT = 256
D_SHARED = 256
ROUTED_SCALE = 2.5
SPLIT_SIZES = (ATTN_W, KV_W, KV_W, D_INNER, CONV_CH, 2 * SSD_HEADS, 2 * D_MODEL)

LANES = 128
KVD_W = N_KV_HEADS * LANES
C_Q, C_K, C_V, C_G, C_Z, C_X, C_DT, C_END = 0, 1024, 1536, 2048, 4096, 6144, 9216, 9472
VMEM_LIMIT = 56 * 1024 * 1024
Q_SCALE = HEAD_DIM ** -0.5 * 1.4426950408889634


def _cparams(sem):
    return pltpu.CompilerParams(dimension_semantics=sem, vmem_limit_bytes=VMEM_LIMIT)


def _silu(x):
    return x * jax.nn.sigmoid(x)


def _bdot(a, b):
    return jnp.dot(a.astype(BF16), b.astype(BF16), preferred_element_type=F32)


def _bdot_nt(a, b):
    return lax.dot_general(a.astype(BF16), b.astype(BF16), (((1,), (1,)), ((), ())),
                           preferred_element_type=F32)


def _mod_kernel(c_ref, w_ref, b_ref, o_ref):
    o_ref[...] = _bdot(_silu(c_ref[...]), w_ref[...]) + b_ref[...]


def _mod_call(cvec, w_mod, b_mod):
    n = w_mod.shape[1]
    bn = 1024
    return pl.pallas_call(
        _mod_kernel,
        out_shape=jax.ShapeDtypeStruct((8, n), F32),
        grid=(n // bn,),
        in_specs=[pl.BlockSpec((8, D_MODEL), lambda j: (0, 0)),
                  pl.BlockSpec((D_MODEL, bn), lambda j: (0, j)),
                  pl.BlockSpec((1, bn), lambda j: (0, j))],
        out_specs=pl.BlockSpec((8, bn), lambda j: (0, j)),
        compiler_params=_cparams(("arbitrary",)),
        name="mod",
    )(cvec, w_mod, b_mod)


def _inproj_kernel(x_ref, mod_ref, g1_ref, w_ref, qg_ref, kg_ref, cos_ref, sin_ref,
                   q_ref, k_ref, v_ref, gates_ref, z_ref, xbc_ref, dt_ref, *, rope):
    tm = x_ref.shape[0]
    x = x_ref[...]
    inv = lax.rsqrt(jnp.mean(x * x, axis=-1, keepdims=True) + EPS)
    h = (x * inv) * g1_ref[...]
    h = h * (1.0 + mod_ref[:, 1024:2048]) + mod_ref[:, 0:1024]
    hb = h.astype(BF16)

    lane = lax.broadcasted_iota(jnp.int32, (tm, LANES), 1)
    lo = lane < HEAD_DIM
    even = (lane & 1) == 0
    if rope:
        cos = cos_ref[...]
        sin = sin_ref[...]

    def rope_fn(blk):
        nxt = pltpu.roll(blk, LANES - 1, 1)
        prv = pltpu.roll(blk, 1, 1)
        return blk * cos + jnp.where(even, nxt, prv) * sin

    qg = qg_ref[...]
    kg = kg_ref[...]
    q = jnp.dot(hb, w_ref[:, C_Q:C_K], preferred_element_type=F32)
    for j in range(ATTN_W // LANES):
        blk = q[:, j * LANES:(j + 1) * LANES]
        sq = blk * blk
        s_all = jnp.sum(sq, axis=-1, keepdims=True)
        s_lo = jnp.sum(jnp.where(lo, sq, 0.0), axis=-1, keepdims=True)
        ms = jnp.where(lo, s_lo, s_all - s_lo) * (1.0 / HEAD_DIM)
        blk = blk * lax.rsqrt(ms + EPS) * qg
        if rope:
            blk = rope_fn(blk)
        q_ref[:, j * LANES:(j + 1) * LANES] = (blk * Q_SCALE).astype(q_ref.dtype)

    k = jnp.dot(hb, w_ref[:, C_K:C_V], preferred_element_type=F32)
    for j in range(N_KV_HEADS):
        blk = k[:, j * LANES:(j + 1) * LANES]
        ms = jnp.mean(blk * blk, axis=-1, keepdims=True)
        blk = blk * lax.rsqrt(ms + EPS) * kg
        if rope:
            blk = rope_fn(blk)
        k_ref[:, j * LANES:(j + 1) * LANES] = blk.astype(k_ref.dtype)

    v_ref[...] = jnp.dot(hb, w_ref[:, C_V:C_G], preferred_element_type=F32).astype(v_ref.dtype)
    gates_ref[...] = jnp.dot(hb, w_ref[:, C_G:C_Z], preferred_element_type=F32).astype(gates_ref.dtype)
    z_ref[...] = jnp.dot(hb, w_ref[:, C_Z:C_X], preferred_element_type=F32).astype(z_ref.dtype)
    xbc_ref[...] = jnp.dot(hb, w_ref[:, C_X:C_DT], preferred_element_type=F32).astype(xbc_ref.dtype)
    dt_ref[...] = jnp.dot(hb, w_ref[:, C_DT:C_END], preferred_element_type=F32)


def _inproj_call(x, mod_rows, g1, w, qg, kg, cos, sin, *, rope, kv_dtype, tm):
    b, t, _ = x.shape
    nt = t // tm
    tok = lambda width: pl.BlockSpec((None, tm, width), lambda bi, i: (bi, i, 0))
    const2 = lambda shape: pl.BlockSpec(shape, lambda bi, i: (0, 0))
    out_shape = (
        jax.ShapeDtypeStruct((b, t, ATTN_W), BF16),
        jax.ShapeDtypeStruct((b, t, KVD_W), kv_dtype),
        jax.ShapeDtypeStruct((b, t, KVD_W), kv_dtype),
        jax.ShapeDtypeStruct((b, t, 2 * D_MODEL), BF16),
        jax.ShapeDtypeStruct((b, t, D_INNER), BF16),
        jax.ShapeDtypeStruct((b, t, CONV_CH), BF16),
        jax.ShapeDtypeStruct((b, t, 2 * LANES), F32),
    )
    return pl.pallas_call(
        functools.partial(_inproj_kernel, rope=rope),
        out_shape=out_shape,
        grid=(b, nt),
        in_specs=[tok(D_MODEL),
                  pl.BlockSpec((None, 1, 6 * D_MODEL), lambda bi, i: (bi, 0, 0)),
                  const2((1, D_MODEL)),
                  pl.BlockSpec((D_MODEL, C_END), lambda bi, i: (0, 0), pipeline_mode=pl.Buffered(1)),
                  const2((1, LANES)), const2((1, LANES)),
                  pl.BlockSpec((tm, LANES), lambda bi, i: (i, 0)),
                  pl.BlockSpec((tm, LANES), lambda bi, i: (i, 0))],
        out_specs=(tok(ATTN_W), tok(KVD_W), tok(KVD_W), tok(2 * D_MODEL), tok(D_INNER),
                   tok(CONV_CH), tok(2 * LANES)),
        compiler_params=_cparams(("arbitrary", "arbitrary")),
        name="inproj",
    )(x, mod_rows, g1, w, qg, kg, cos, sin)


KEY_CHUNK = 512


def _key_chunk(n):
    return KEY_CHUNK if n % KEY_CHUNK == 0 else n


def _attn_kernel(*refs, has_ctx):
    if has_ctx:
        q_ref, k_ref, v_ref, kctx_ref, vctx_ref, o_ref = refs
        sources = ((k_ref, v_ref), (kctx_ref, vctx_ref))
    else:
        q_ref, k_ref, v_ref, o_ref = refs
        sources = ((k_ref, v_ref),)
    tq = q_ref.shape[0]
    lane = lax.broadcasted_iota(jnp.int32, (tq, LANES), 1)
    lo = lane < HEAD_DIM
    qs = []
    for j in range(2):
        q2 = q_ref[:, j * LANES:(j + 1) * LANES]
        zero = jnp.zeros_like(q2)
        qs += [jnp.where(lo, q2, zero), jnp.where(lo, zero, q2)]
    q4 = jnp.concatenate(qs, axis=0)
    rows = 4 * tq
    m = jnp.full((rows, 1), -jnp.inf, F32)
    acc = jnp.zeros((rows, LANES), F32)
    chunks = [(kr, vr, c, _key_chunk(kr.shape[0])) for kr, vr in sources
              for c in range(kr.shape[0] // _key_chunk(kr.shape[0]))]
    for kr, vr, c, kc in chunks:
        kch = kr[c * kc:(c + 1) * kc, :].astype(BF16)
        vch = vr[c * kc:(c + 1) * kc, :].astype(BF16)
        lane_k = lax.broadcasted_iota(jnp.int32, (kc, LANES), 1)
        vch = jnp.where(lane_k < HEAD_DIM, vch, jnp.ones_like(vch))
        s = _bdot_nt(q4, kch)
        m_new = jnp.maximum(m, jnp.max(s, axis=-1, keepdims=True))
        alpha = jnp.exp2(m - m_new)
        p = jnp.exp2((s - m_new).astype(BF16))
        acc = acc * alpha + jnp.dot(p, vch, preferred_element_type=F32)
        m = m_new
    o = acc * (1.0 / pltpu.roll(acc, HEAD_DIM, 1))
    for j in range(2):
        oa = o[(2 * j) * tq:(2 * j + 1) * tq]
        ob = pltpu.roll(o[(2 * j + 1) * tq:(2 * j + 2) * tq], HEAD_DIM, 1)
        o_ref[:, j * LANES:(j + 1) * LANES] = jnp.where(lo, oa, ob).astype(o_ref.dtype)


def _attn_call(q, k, v, kctx, vctx, *, tq):
    b, t, _ = q.shape
    tk = k.shape[1]
    nq = t // tq
    has_ctx = kctx is not None
    kv_spec = lambda n: pl.BlockSpec((None, n, LANES), lambda bi, g, i: (bi, 0, g))
    in_specs = [pl.BlockSpec((None, tq, 2 * LANES), lambda bi, g, i: (bi, i, g)), kv_spec(tk), kv_spec(tk)]
    args = [q, k, v]
    if has_ctx:
        in_specs += [kv_spec(kctx.shape[1]), kv_spec(kctx.shape[1])]
        args += [kctx, vctx]
    return pl.pallas_call(
        functools.partial(_attn_kernel, has_ctx=has_ctx),
        out_shape=jax.ShapeDtypeStruct((b, t, ATTN_W), BF16),
        grid=(b, N_KV_HEADS, nq),
        in_specs=in_specs,
        out_specs=pl.BlockSpec((None, tq, 2 * LANES), lambda bi, g, i: (bi, i, g)),
        compiler_params=_cparams(("arbitrary", "arbitrary", "arbitrary")),
        name="attn",
    )(*args)


LOG2E = 1.4426950408889634


def _softplus(x):
    return jnp.maximum(x, 0.0) + jnp.log(1.0 + jnp.exp(-jnp.abs(x)))


def _ssd_kernel(*refs, nc, reverse, has_h0, want_hfin):
    refs = list(refs)
    conv = not reverse
    if conv:
        xbc_ref, prev_ref, next_ref, cw_ref, cb_ref, dsk_ref = refs[:6]
        refs = refs[6:]
    else:
        xc_ref = refs.pop(0)
    dt_ref, an_ref, dtb_ref, tri_ref = refs[:4]
    refs = refs[4:]
    h0_ref = refs.pop(0) if has_h0 else None
    hprev_ref = refs.pop(0) if (want_hfin and reverse) else None
    y_ref = refs.pop(0)
    xco_ref = refs.pop(0) if conv else None
    hfin_ref = refs.pop(0) if want_hfin else None
    h_scr = refs.pop(0)

    L = CHUNK
    c = pl.program_id(1)
    cidx = (nc - 1 - c) if reverse else c

    @pl.when(c == 0)
    def _():
        if has_h0:
            h_scr[...] = h0_ref[...]
        else:
            h_scr[...] = jnp.zeros_like(h_scr)

    row = lax.broadcasted_iota(jnp.int32, (L, LANES), 0)
    lane = lax.broadcasted_iota(jnp.int32, (L, LANES), 1)
    lo = lane < SSD_HEAD_DIM
    top = row < SSD_HEAD_DIM

    if conv:
        first = cidx == 0
        last = cidx == nc - 1

        def cols(a, w):
            xm = xbc_ref[:, a:a + w].astype(F32)
            rw = lax.broadcasted_iota(jnp.int32, (L, w), 0)
            p6 = jnp.where(first, 0.0, prev_ref[6:7, a:a + w].astype(F32))
            p7 = jnp.where(first, 0.0, prev_ref[7:8, a:a + w].astype(F32))
            n0 = jnp.where(last, 0.0, next_ref[0:1, a:a + w].astype(F32))
            r1 = jnp.where(rw == 0, p7, pltpu.roll(xm, 1, 0))
            r2 = jnp.where(rw == 0, p6, jnp.where(rw == 1, p7, pltpu.roll(xm, 2, 0)))
            rn = jnp.where(rw == L - 1, n0, pltpu.roll(xm, L - 1, 0))
            y = (r2 * cw_ref[0:1, a:a + w] + r1 * cw_ref[1:2, a:a + w] + xm * cw_ref[2:3, a:a + w]
                 + rn * cw_ref[3:4, a:a + w] + cb_ref[:, a:a + w])
            y = _silu(y).astype(BF16)
            xco_ref[:, a:a + w] = y
            return y
    else:
        def cols(a, w):
            return xc_ref[:, a:a + w]

    causal = tri_ref[...] > 0.0
    dt = _softplus(dt_ref[...] + dtb_ref[...])
    la2 = dt * (an_ref[...] * LOG2E)
    acum2 = jnp.dot(tri_ref[...], la2, preferred_element_type=F32, precision=lax.Precision.HIGHEST)
    dt_t = dt.T
    acum2_t = acum2.T
    tot2_t = jnp.sum(la2.T, axis=1, keepdims=True)
    lg_dt_t = jnp.log2(dt_t)
    r_t = lg_dt_t - acum2_t
    w_t = jnp.exp2(lg_dt_t + tot2_t - acum2_t)
    e_acum = jnp.exp2(acum2)
    e_tot_t = jnp.exp2(tot2_t)

    for g in range(SSD_GROUPS):
        bgb = cols(D_INNER + g * D_STATE, D_STATE)
        cgb = cols(D_INNER + SSD_GROUPS * D_STATE + g * D_STATE, D_STATE)
        cbm = _bdot_nt(cgb, bgb)
        h_grp = h_scr[4 * g:4 * g + 4]
        yo_grp = _bdot_nt(cgb, h_grp.reshape(4 * LANES, D_STATE))
        for pr in range(4):
            hp = g * 4 + pr
            ha, hb = 2 * hp, 2 * hp + 1
            xpb = cols(hp * LANES, LANES)
            zero = jnp.zeros_like(xpb)
            xs = jnp.concatenate([jnp.where(lo, xpb, zero), jnp.where(lo, zero, xpb)], axis=0)
            ms = []
            for hh in (ha, hb):
                e = jnp.exp2(acum2[:, hh:hh + 1] + r_t[hh:hh + 1, :])
                ms.append((cbm * jnp.where(causal, e, 0.0)).astype(BF16))
            y = jnp.dot(jnp.concatenate(ms, axis=1), xs, preferred_element_type=F32)
            y = y + yo_grp[:, pr * LANES:(pr + 1) * LANES] * \
                jnp.where(lo, e_acum[:, ha:ha + 1], e_acum[:, hb:hb + 1])
            if conv:
                y = y + dsk_ref[:, hp * LANES:(hp + 1) * LANES] * xpb.astype(F32)
            y_ref[:, hp * LANES:(hp + 1) * LANES] = y.astype(y_ref.dtype)
            wsel = jnp.where(top, w_t[ha:ha + 1, :], w_t[hb:hb + 1, :])
            st = jnp.dot((xpb.astype(F32).T * wsel).astype(BF16), bgb, preferred_element_type=F32)
            cd = jnp.where(top, e_tot_t[ha:ha + 1, :], e_tot_t[hb:hb + 1, :])
            h_scr[hp] = h_grp[pr] * cd + st

    if want_hfin:
        @pl.when(c == nc - 1)
        def _():
            if reverse:
                hfin_ref[0] = hprev_ref[...]
                hfin_ref[1] = h_scr[...]
            else:
                hfin_ref[...] = h_scr[...]


def _ssd_sweep(xin, dt, wts, h0, hprev, *, reverse, want_hfin):
    b, t, _ = xin.shape
    nc = t // CHUNK
    has_h0 = h0 is not None
    rb = CHUNK // 8
    nrb = t // 8
    d = 1 if reverse else 0
    cmap = (lambda c: nc - 1 - c) if reverse else (lambda c: c)
    hshape = (SSD_HEADS // 2, 2 * SSD_HEAD_DIM, D_STATE)

    chunk_spec = pl.BlockSpec((None, CHUNK, CONV_CH), lambda bi, c: (bi, cmap(c), 0))
    if reverse:
        in_specs = [chunk_spec]
        args = [xin]
    else:
        in_specs = [
            chunk_spec,
            pl.BlockSpec((None, 8, CONV_CH), lambda bi, c: (bi, jnp.maximum(c * rb - 1, 0), 0)),
            pl.BlockSpec((None, 8, CONV_CH), lambda bi, c: (bi, jnp.minimum((c + 1) * rb, nrb - 1), 0)),
            pl.BlockSpec((D_CONV, CONV_CH), lambda bi, c: (0, 0)),
            pl.BlockSpec((1, CONV_CH), lambda bi, c: (0, 0)),
            pl.BlockSpec((1, D_INNER), lambda bi, c: (0, 0)),
        ]
        args = [xin, xin, xin, wts["conv_w"], wts["conv_b"], wts["dskip"]]
    in_specs += [
        pl.BlockSpec((None, CHUNK, LANES), lambda bi, c: (bi, cmap(c), d)),
        pl.BlockSpec((None, 1, LANES), lambda bi, c: (d, 0, 0)),
        pl.BlockSpec((None, 1, LANES), lambda bi, c: (d, 0, 0)),
        pl.BlockSpec((None, CHUNK, CHUNK), lambda bi, c: (d, 0, 0)),
    ]
    args += [dt, wts["a_neg"], wts["dt_bias"], wts["tri"]]
    if has_h0:
        in_specs.append(pl.BlockSpec((None, None) + hshape, lambda bi, c: (bi, d, 0, 0, 0)))
        args.append(h0)
    if want_hfin and reverse:
        in_specs.append(pl.BlockSpec((None,) + hshape, lambda bi, c: (bi, 0, 0, 0)))
        args.append(hprev)
    out_shape = [jax.ShapeDtypeStruct((b, t, D_INNER), BF16)]
    out_specs = [pl.BlockSpec((None, CHUNK, D_INNER), lambda bi, c: (bi, cmap(c), 0))]
    if not reverse:
        out_shape.append(jax.ShapeDtypeStruct((b, t, CONV_CH), BF16))
        out_specs.append(pl.BlockSpec((None, CHUNK, CONV_CH), lambda bi, c: (bi, c, 0)))
    if want_hfin and reverse:
        out_shape.append(jax.ShapeDtypeStruct((b, 2) + hshape, F32))
        out_specs.append(pl.BlockSpec((None, 2) + hshape, lambda bi, c: (bi, 0, 0, 0, 0)))
    elif want_hfin:
        out_shape.append(jax.ShapeDtypeStruct((b,) + hshape, F32))
        out_specs.append(pl.BlockSpec((None,) + hshape, lambda bi, c: (bi, 0, 0, 0)))
    return pl.pallas_call(
        functools.partial(_ssd_kernel, nc=nc, reverse=reverse, has_h0=has_h0, want_hfin=want_hfin),
        out_shape=tuple(out_shape),
        grid=(b, nc),
        in_specs=in_specs,
        out_specs=tuple(out_specs),
        scratch_shapes=[pltpu.VMEM(hshape, F32)],
        compiler_params=_cparams(("arbitrary", "arbitrary")),
        name="ssd_bwd" if reverse else "ssd_fwd",
    )(*args)


def _ssd_call(xbc, dt, wts, h0, *, want_hfin):
    res = _ssd_sweep(xbc, dt, wts, h0, None, reverse=False, want_hfin=want_hfin)
    y_f, xc = res[0], res[1]
    hf = res[2] if want_hfin else None
    res = _ssd_sweep(xc, dt, wts, h0, hf, reverse=True, want_hfin=want_hfin)
    return y_f, res[0], (res[1] if want_hfin else None)


def _route(logits_t, bias_col):
    e, n = logits_t.shape
    per = e // N_EXPERT_GROUPS
    scores = jax.nn.sigmoid(logits_t)
    sel = scores + bias_col
    neg = jnp.float32(-jnp.inf)
    gs = []
    for g in range(N_EXPERT_GROUPS):
        blk = sel[g * per:(g + 1) * per, :]
        m1 = jnp.max(blk, axis=0, keepdims=True)
        is_m1 = blk == m1
        cnt = jnp.sum(jnp.where(is_m1, 1.0, 0.0), axis=0, keepdims=True)
        m2 = jnp.max(jnp.where(is_m1, neg, blk), axis=0, keepdims=True)
        gs.append(m1 + jnp.where(cnt >= 2.0, m1, m2))
    keep = []
    for g in range(N_EXPERT_GROUPS):
        rank = jnp.zeros_like(gs[g])
        for j in range(N_EXPERT_GROUPS):
            if j == g:
                continue
            beats = (gs[j] > gs[g]) if j > g else (gs[j] >= gs[g])
            rank = rank + jnp.where(beats, 1.0, 0.0)
        keep.append(rank < float(TOPK_GROUPS))
    selm = jnp.concatenate(
        [jnp.where(keep[g], sel[g * per:(g + 1) * per, :], neg) for g in range(N_EXPERT_GROUPS)], axis=0)
    eidx = lax.broadcasted_iota(jnp.int32, (e, n), 0).astype(F32)
    cur = selm
    picks = []
    for _ in range(TOP_K):
        m = jnp.max(cur, axis=0, keepdims=True)
        idx = jnp.min(jnp.where(cur == m, eidx, float(e)), axis=0, keepdims=True)
        hit = eidx == idx
        picks.append((idx, hit))
        cur = jnp.where(hit, neg, cur)
    return scores, picks


def _pack_bf16_pairs(h):
    c = h.shape[1] // 2
    lo = pltpu.bitcast(h[:, :c].astype(BF16).astype(F32), jnp.uint32)
    hi = pltpu.bitcast(h[:, c:].astype(BF16).astype(F32), jnp.uint32)
    return (lo >> 16) | (hi & jnp.uint32(0xFFFF0000))


def _unpack_bf16_pairs(w):
    lo = pltpu.bitcast(w << 16, F32)
    hi = pltpu.bitcast(w & jnp.uint32(0xFFFF0000), F32)
    return lo, hi


def _rows8(rows):
    n = rows[0].shape[1]
    ridx = lax.broadcasted_iota(jnp.int32, (TOP_K, n), 0)
    out = jnp.zeros((TOP_K, n), rows[0].dtype)
    for k, r in enumerate(rows):
        out = jnp.where(ridx == k, r, out)
    return out


MERGE_SUB = 256


def _merge_kernel(x_ref, attn_ref, yf_ref, yb_ref, z_ref, gates_ref, mod_ref, wa_ref, ws_ref, wo_ref,
                  sg_ref, n2_ref, wr_ref, rb_ref, wsg_ref, wsu_ref, wsd_ref,
                  xb_ref, ha_ref, hb_ref, eid_ref, pos_ref, wk_ref, cnt_ref):
    tm = x_ref.shape[0]
    sub = MERGE_SUB

    @pl.when(pl.program_id(0) == 0)
    def _():
        cnt_ref[...] = jnp.zeros_like(cnt_ref)

    r_i = lax.broadcasted_iota(jnp.int32, (sub, sub), 0)
    c_i = lax.broadcasted_iota(jnp.int32, (sub, sub), 1)
    before = jnp.where(r_i < c_i, 1.0, 0.0).astype(BF16)
    cnt = cnt_ref[:, 0:1]

    for r0 in range(0, tm, sub):
        rs = slice(r0, r0 + sub)
        x = x_ref[rs, :]
        yy = yf_ref[rs, :].astype(F32) + yb_ref[rs, :].astype(F32)
        u = yy * _silu(z_ref[rs, :]).astype(F32)
        un = u * lax.rsqrt(jnp.mean(u * u, axis=-1, keepdims=True) + EPS) * sg_ref[...]
        ssd_o = _bdot(un, ws_ref[...])
        attn_o = jnp.dot(attn_ref[rs, :], wa_ref[...], preferred_element_type=F32)
        ga = jax.nn.sigmoid(gates_ref[rs, 0:D_MODEL]).astype(F32)
        gs = jax.nn.sigmoid(gates_ref[rs, D_MODEL:2 * D_MODEL]).astype(F32)
        mix = _bdot(ga * attn_o + gs * ssd_o, wo_ref[...])
        x1 = x + mod_ref[:, 2048:3072] * mix
        h2 = x1 * lax.rsqrt(jnp.mean(x1 * x1, axis=-1, keepdims=True) + EPS) * n2_ref[...]
        h2 = h2 * (1.0 + mod_ref[:, 4096:5120]) + mod_ref[:, 3072:4096]
        h2b = h2.astype(BF16)
        ha_ref[rs, :] = _pack_bf16_pairs(h2[:, :D_MODEL // 2])
        hb_ref[rs, :] = _pack_bf16_pairs(h2[:, D_MODEL // 2:])

        logits_t = _bdot_nt(wr_ref[...], h2b)
        scores, picks = _route(logits_t, rb_ref[...])
        chosen = jnp.zeros_like(scores)
        for _, hit in picks:
            chosen = chosen + jnp.where(hit, 1.0, 0.0)
        pos = cnt + jnp.dot(chosen.astype(BF16), before, preferred_element_type=F32)
        cnt = cnt + jnp.sum(chosen, axis=1, keepdims=True)
        poss = [jnp.sum(jnp.where(hit, pos, 0.0), axis=0, keepdims=True) for _, hit in picks]
        wks = [jnp.sum(jnp.where(hit, scores, 0.0), axis=0, keepdims=True) for _, hit in picks]
        wsum = wks[0]
        for w in wks[1:]:
            wsum = wsum + w
        eid_ref[:, rs] = _rows8([idx for idx, _ in picks]).astype(jnp.int32)
        pos_ref[:, rs] = _rows8(poss).astype(jnp.int32)
        wk8 = _rows8(wks) / wsum * ROUTED_SCALE
        wk_ref[rs, :] = jnp.concatenate([wk8, jnp.zeros((LANES - TOP_K, sub), F32)], axis=0).T

        hid = _silu(jnp.dot(h2b, wsg_ref[...], preferred_element_type=F32)) * \
            jnp.dot(h2b, wsu_ref[...], preferred_element_type=F32)
        xb_ref[rs, :] = x1 + mod_ref[:, 5120:6144] * _bdot(hid, wsd_ref[...])

    cnt_ref[...] = jnp.broadcast_to(cnt, cnt_ref.shape)


def _merge_call(x, attn, y_f, y_b, z, gates, mod_rows, wa, ws, wo, sg, n2, wr_t, rb, wsg, wsu, wsd, *, tm):
    b, t, _ = x.shape
    n = b * t
    flat = lambda a: a.reshape(n, a.shape[-1])
    tok = lambda width: pl.BlockSpec((tm, width), lambda i: (i, 0))
    const2 = lambda shape: pl.BlockSpec(shape, lambda i: (0, 0), pipeline_mode=pl.Buffered(1))
    k8 = pl.BlockSpec((TOP_K, tm), lambda i: (0, i))
    half = D_MODEL // 4
    return pl.pallas_call(
        _merge_kernel,
        out_shape=(jax.ShapeDtypeStruct((n, D_MODEL), F32),
                   jax.ShapeDtypeStruct((n, half), jnp.uint32),
                   jax.ShapeDtypeStruct((n, half), jnp.uint32),
                   jax.ShapeDtypeStruct((TOP_K, n), jnp.int32),
                   jax.ShapeDtypeStruct((TOP_K, n), jnp.int32),
                   jax.ShapeDtypeStruct((n, LANES), F32),
                   jax.ShapeDtypeStruct((N_EXPERTS, LANES), F32)),
        grid=(n // tm,),
        in_specs=[tok(D_MODEL), tok(ATTN_W), tok(D_INNER), tok(D_INNER), tok(D_INNER), tok(2 * D_MODEL),
                  pl.BlockSpec((None, 1, 6 * D_MODEL), lambda i: ((i * tm) // t, 0, 0)),
                  const2((ATTN_W, D_MODEL)), const2((D_INNER, D_MODEL)), const2((D_MODEL, D_MODEL)),
                  const2((1, D_INNER)), const2((1, D_MODEL)),
                  const2((N_EXPERTS, D_MODEL)), const2((N_EXPERTS, 1)),
                  const2((D_MODEL, D_SHARED)), const2((D_MODEL, D_SHARED)), const2((D_SHARED, D_MODEL))],
        out_specs=(tok(D_MODEL), tok(half), tok(half), k8, k8, tok(LANES),
                   pl.BlockSpec((N_EXPERTS, LANES), lambda i: (0, 0))),
        compiler_params=_cparams(("arbitrary",)),
        name="merge",
    )(flat(x), flat(attn), flat(y_f), flat(y_b), flat(z), flat(gates), mod_rows, wa, ws, wo, sg, n2, wr_t, rb,
      wsg, wsu, wsd)


ROW_TILE = 512
SC_WINDOW = 128


def _slots_kernel(start_ref, eid_ref, pos_ref, slot_ref):
    eid = eid_ref[...]
    slot = pos_ref[...]
    for e in range(N_EXPERTS):
        slot = slot + jnp.where(eid == e, start_ref[e], 0)
    slot_ref[...] = slot


def _slots_call(start, eid, pos):
    n = eid.shape[1]
    bn = 2048 if n % 2048 == 0 else n
    spec = pl.BlockSpec((TOP_K, bn), lambda i, s: (0, i))
    return pl.pallas_call(
        _slots_kernel,
        out_shape=jax.ShapeDtypeStruct((TOP_K, n), jnp.int32),
        grid_spec=pltpu.PrefetchScalarGridSpec(num_scalar_prefetch=1, grid=(n // bn,),
                                               in_specs=[spec, spec], out_specs=spec),
        compiler_params=_cparams(("arbitrary",)),
        name="slots",
    )(start, eid, pos)


def _sc_dispatch(x, slots, p):
    n, d = x.shape
    mesh = plsc.VectorSubcoreMesh(core_axis_name="core", subcore_axis_name="subcore")

    @functools.partial(pl.kernel, out_type=jax.ShapeDtypeStruct((p, d), x.dtype), mesh=mesh)
    def k(x_hbm, s_hbm, o_hbm):
        def body(x_vmem, s_vmem):
            for kk in range(TOP_K):
                pltpu.sync_copy(x_vmem, o_hbm.at[s_vmem.at[kk]])

        pltpu.emit_pipeline(
            body,
            grid=(n // SC_WINDOW,),
            in_specs=[pl.BlockSpec((SC_WINDOW, d), index_map=lambda i: (i, 0)),
                      pl.BlockSpec((TOP_K, SC_WINDOW), index_map=lambda i: (0, i))],
            out_specs=[],
            core_axis_name=("core", "subcore"),
            dimension_semantics=(pltpu.PARALLEL,),
        )(x_hbm, s_hbm)

    return k(x, slots)


def _sc_combine(y, slots):
    kk, n = slots.shape
    d = y.shape[1]
    mesh = plsc.VectorSubcoreMesh(core_axis_name="core", subcore_axis_name="subcore")

    @functools.partial(pl.kernel, out_type=jax.ShapeDtypeStruct((kk * n, d), y.dtype), mesh=mesh)
    def k(y_hbm, s_hbm, o_hbm):
        def body(s_vmem, o_vmem):
            pltpu.sync_copy(y_hbm.at[s_vmem.at[0]], o_vmem)

        pltpu.emit_pipeline(
            body,
            grid=(kk * n // SC_WINDOW,),
            in_specs=[pl.BlockSpec((1, SC_WINDOW), index_map=lambda i: (0, i))],
            out_specs=[pl.BlockSpec((SC_WINDOW, d), index_map=lambda i: (i, 0))],
            core_axis_name=("core", "subcore"),
            dimension_semantics=(pltpu.PARALLEL,),
        )(s_hbm, o_hbm)

    return k(y, slots.reshape(1, kk * n)).reshape(kk, n, d)


FFN_IN_BUFS = 4
FFN_OUT_BUFS = 3


def _ffn_kernel(st_ref, nt_ref, wg_ref, wu_ref, wd_ref, xa_hbm, xb_hbm, ya_hbm, yb_hbm,
                wg_s, wu_s, wd_s, xa_buf, xb_buf, ya_buf, yb_buf, in_sem, out_sem):
    e = pl.program_id(0)
    ne = pl.num_programs(0)
    n = nt_ref[e]
    t = ROW_TILE
    ahead = FFN_IN_BUFS - 1

    def fetch(row, slot):
        r = pl.multiple_of(row, t)
        return (pltpu.make_async_copy(xa_hbm.at[pl.ds(r, t)], xa_buf.at[slot], in_sem.at[0, slot]),
                pltpu.make_async_copy(xb_hbm.at[pl.ds(r, t)], xb_buf.at[slot], in_sem.at[1, slot]))

    def put(row, slot):
        r = pl.multiple_of(row, t)
        return (pltpu.make_async_copy(ya_buf.at[slot], ya_hbm.at[pl.ds(r, t)], out_sem.at[0, slot]),
                pltpu.make_async_copy(yb_buf.at[slot], yb_hbm.at[pl.ds(r, t)], out_sem.at[1, slot]))

    def start_head(expert):
        for k in range(ahead):
            @pl.when(k < nt_ref[expert])
            def _():
                for c in fetch(st_ref[expert] + k * t, k):
                    c.start()

    @pl.when(e == 0)
    def _():
        start_head(0)

    @pl.when(n > 0)
    def _():
        wg_s[...] = wg_ref[...].astype(BF16)
        wu_s[...] = wu_ref[...].astype(BF16)
        wd_s[...] = wd_ref[...].astype(BF16)
        base = st_ref[e]

        def body(i, carry):
            si = lax.rem(i, FFN_IN_BUFS)
            so = lax.rem(i, FFN_OUT_BUFS)
            row = base + i * t
            for c in fetch(row, si):
                c.wait()

            @pl.when(i + ahead < n)
            def _():
                for c in fetch(row + ahead * t, lax.rem(i + ahead, FFN_IN_BUFS)):
                    c.start()

            @pl.when(i >= FFN_OUT_BUFS)
            def _():
                for c in put(row - FFN_OUT_BUFS * t, so):
                    c.wait()

            parts = _unpack_bf16_pairs(xa_buf[si]) + _unpack_bf16_pairs(xb_buf[si])
            x = jnp.concatenate(parts, axis=1).astype(BF16)
            hid = _silu(jnp.dot(x, wg_s[...], preferred_element_type=F32)) * \
                jnp.dot(x, wu_s[...], preferred_element_type=F32)
            y = _bdot(hid, wd_s[...])
            ya_buf[so] = _pack_bf16_pairs(y[:, :D_MODEL // 2])
            yb_buf[so] = _pack_bf16_pairs(y[:, D_MODEL // 2:])
            for c in put(row, so):
                c.start()
            return carry

        lax.fori_loop(0, n, body, 0)

        for k in range(FFN_OUT_BUFS):
            @pl.when(n - 1 - k >= 0)
            def _():
                j = n - 1 - k
                for c in put(base + j * t, lax.rem(j, FFN_OUT_BUFS)):
                    c.wait()

    @pl.when(e + 1 < ne)
    def _():
        start_head(jnp.minimum(e + 1, ne - 1))


def _ffn_call(start, tiles_e, xa, xb, wg, wu, wd):
    p, half = xa.shape
    wspec = lambda s: pl.BlockSpec((None,) + s, lambda e, st, nt: (e, 0, 0))
    hbm = pl.BlockSpec(memory_space=pl.ANY)
    ibuf = pltpu.VMEM((FFN_IN_BUFS, ROW_TILE, half), jnp.uint32)
    obuf = pltpu.VMEM((FFN_OUT_BUFS, ROW_TILE, half), jnp.uint32)
    return pl.pallas_call(
        _ffn_kernel,
        out_shape=(jax.ShapeDtypeStruct((p, half), jnp.uint32), jax.ShapeDtypeStruct((p, half), jnp.uint32)),
        grid_spec=pltpu.PrefetchScalarGridSpec(
            num_scalar_prefetch=2, grid=(N_EXPERTS,),
            in_specs=[wspec((D_MODEL, D_EXPERT)), wspec((D_MODEL, D_EXPERT)), wspec((D_EXPERT, D_MODEL)),
                      hbm, hbm],
            out_specs=(hbm, hbm),
            scratch_shapes=[pltpu.VMEM((D_MODEL, D_EXPERT), BF16), pltpu.VMEM((D_MODEL, D_EXPERT), BF16),
                            pltpu.VMEM((D_EXPERT, D_MODEL), BF16), ibuf, ibuf, obuf, obuf,
                            pltpu.SemaphoreType.DMA((2, FFN_IN_BUFS)),
                            pltpu.SemaphoreType.DMA((2, FFN_OUT_BUFS))]),
        compiler_params=_cparams(("arbitrary",)),
        name="ffn",
    )(start, tiles_e, wg, wu, wd, xa, xb)


def _final_kernel(xb_ref, ya_ref, yb_ref, wk_ref, mod_ref, o_ref):
    q = D_MODEL // 4
    accs = [jnp.zeros((xb_ref.shape[0], q), F32) for _ in range(4)]
    for k in range(TOP_K):
        w = wk_ref[:, k:k + 1]
        parts = _unpack_bf16_pairs(ya_ref[k]) + _unpack_bf16_pairs(yb_ref[k])
        accs = [a + w * p for a, p in zip(accs, parts)]
    for i, a in enumerate(accs):
        o_ref[:, i * q:(i + 1) * q] = xb_ref[:, i * q:(i + 1) * q] + mod_ref[:, 5120 + i * q:5120 + (i + 1) * q] * a


def _final_call(xb, ya, yb, wk, mod_rows, t, *, tm):
    n = xb.shape[0]
    half = ya.shape[2]
    tok = lambda width: pl.BlockSpec((tm, width), lambda i: (i, 0))
    yspec = pl.BlockSpec((TOP_K, tm, half), lambda i: (0, i, 0))
    return pl.pallas_call(
        _final_kernel,
        out_shape=jax.ShapeDtypeStruct((n, D_MODEL), F32),
        grid=(n // tm,),
        in_specs=[tok(D_MODEL), yspec, yspec, tok(LANES),
                  pl.BlockSpec((None, 1, 6 * D_MODEL), lambda i: ((i * tm) // t, 0, 0))],
        out_specs=tok(D_MODEL),
        compiler_params=_cparams(("arbitrary",)),
        name="final",
    )(xb, ya, yb, wk, mod_rows)


def _moe_call(ha, hb, eid, pos, wk, counts, xb, mod_rows, wg, wu, wd, t):
    n = xb.shape[0]
    max_tiles = n * TOP_K // ROW_TILE + N_EXPERTS
    p = max_tiles * ROW_TILE
    cnt = counts[:, 0].astype(jnp.int32)
    tiles_e = (cnt + ROW_TILE - 1) // ROW_TILE
    ends = jnp.cumsum(tiles_e)
    start = (ends - tiles_e) * ROW_TILE
    slots = _slots_call(start.astype(jnp.int32), eid, pos)
    xa = _sc_dispatch(ha, slots, p)
    xbb = _sc_dispatch(hb, slots, p)
    ya, yb = _ffn_call(start.astype(jnp.int32), tiles_e.astype(jnp.int32), xa, xbb, wg, wu, wd)
    ga = _sc_combine(ya, slots)
    gb = _sc_combine(yb, slots)
    return _final_call(xb, ga, gb, wk, mod_rows, t, tm=256)


def _rope_tables(t):
    n_rows = t // GRID_W
    rows = jnp.repeat(jnp.arange(n_rows), GRID_W).astype(F32)
    cols = jnp.tile(jnp.arange(GRID_W), n_rows).astype(F32)
    n_freq = HEAD_DIM // 4
    freqs = ROPE_THETA ** (-jnp.arange(n_freq, dtype=F32) / n_freq)
    ang = jnp.concatenate([rows[:, None] * freqs, cols[:, None] * freqs], axis=-1)
    ang = jnp.repeat(ang, 2, axis=-1)
    ang = jnp.concatenate([ang, ang], axis=-1)
    sign = jnp.where(jnp.arange(LANES) % 2 == 0, -1.0, 1.0).astype(F32)
    return jnp.cos(ang), jnp.sin(ang) * sign


def _dup_heads(a):
    s = a.shape[:-1]
    a4 = a.reshape(s + (N_KV_HEADS, HEAD_DIM))
    return jnp.concatenate([a4, a4], axis=-1).reshape(s + (KVD_W,))


def _prep_w_in(w_in):
    idx = np.cumsum(SPLIT_SIZES)[:-1].tolist()
    q, k, v, z, xbc, dt, gates = jnp.split(w_in, idx, axis=-1)
    pad = jnp.zeros((D_MODEL, LANES - SSD_HEADS), w_in.dtype)
    cols = [q, _dup_heads(k), _dup_heads(v), gates, z, xbc,
            dt[:, :SSD_HEADS], pad, dt[:, SSD_HEADS:], pad]
    return jnp.concatenate(cols, axis=-1).astype(BF16)


def _pad_heads(a):
    return jnp.pad(a.astype(F32), ((0, 0), (0, LANES - SSD_HEADS)))[:, None, :]


def _trunk(x, mod_rows, wts, rope_tabs, ctx_k, ctx_v, h0, *, tm, tq, want_state):
    b, t, _ = x.shape
    rope = rope_tabs is not None
    if rope:
        cos, sin = rope_tabs
    else:
        cos = sin = jnp.zeros((t, LANES), F32)
    kv_dtype = BF16 if rope else F32
    q, k, v, gates, z, xbc, dt = _inproj_call(x, mod_rows, wts["g1"], wts["w_in"], wts["qg"], wts["kg"],
                                              cos, sin, rope=rope, kv_dtype=kv_dtype, tm=tm)
    attn = _attn_call(q, k, v, ctx_k, ctx_v, tq=tq)
    y_f, y_b, hfin = _ssd_call(xbc, dt, wts, h0, want_hfin=want_state)
    xb, ha, hb, eid, pos, wk, counts = _merge_call(
        x, attn, y_f, y_b, z, gates, mod_rows, wts["wa"], wts["ws"], wts["wo"], wts["sg"],
        wts["n2"], wts["wr_t"], wts["rb"], wts["wsg"], wts["wsu"], wts["wsd"], tm=MERGE_SUB)
    out = _moe_call(ha, hb, eid, pos, wk, counts, xb, mod_rows, wts["weg"], wts["weu"], wts["wed"], t)
    return out.reshape(b, t, D_MODEL), k, v, hfin


def kernel(x_prompt, x_sample, cache_k, cache_v, state_ssm, c, c_ctx, w_mod, b_mod, norm1_g, norm2_g, w_in,
           q_norm_g, k_norm_g, conv_w, conv_b, a_log, dt_bias, d_skip, ssd_norm_g, w_attn_proj, w_ssd_proj,
           w_out, w_router, router_bias, w_exp_gate, w_exp_up, w_exp_down, w_sh_gate, w_sh_up, w_sh_down):
    depth = w_mod.shape[0]
    assert depth == 1, "single trunk layer"
    bp, tp, _ = x_prompt.shape
    bs, ts, _ = x_sample.shape
    l = 0
    cvec = jnp.concatenate([c_ctx[None, :], c, jnp.zeros((8 - 1 - bs, D_MODEL), F32)], axis=0)
    mod = _mod_call(cvec, w_mod.reshape(D_MODEL, 6 * D_MODEL), b_mod.reshape(1, 6 * D_MODEL))
    mod_prompt = jnp.broadcast_to(mod[0:1][:, None, :], (bp, 1, 6 * D_MODEL))
    mod_sample = mod[1:1 + bs][:, None, :]

    lower = np.tril(np.ones((CHUNK, CHUNK), np.float32))
    wts = dict(
        g1=norm1_g[l][None, :], n2=norm2_g[l][None, :],
        w_in=_prep_w_in(w_in.reshape(D_MODEL, w_in.shape[-1])),
        qg=jnp.tile(q_norm_g[l], 2)[None, :], kg=jnp.tile(k_norm_g[l], 2)[None, :],
        conv_w=conv_w[l], conv_b=conv_b[l][None, :],
        a_neg=_pad_heads(-jnp.exp(a_log[l].astype(F32))), dt_bias=_pad_heads(dt_bias[l]),
        dskip=jnp.repeat(d_skip[l].astype(F32), SSD_HEAD_DIM)[None, :],
        tri=jnp.asarray(np.stack([lower, lower.T])),
        sg=ssd_norm_g[l][None, :],
        wa=w_attn_proj[l].astype(BF16), ws=w_ssd_proj[l].astype(BF16), wo=w_out[l].astype(BF16),
        wr_t=w_router[l].T.astype(BF16), rb=router_bias[l].astype(F32)[:, None],
        wsg=w_sh_gate[l].astype(BF16), wsu=w_sh_up[l].astype(BF16), wsd=w_sh_down[l].astype(BF16),
        weg=w_exp_gate.reshape(w_exp_gate.shape[1:]), weu=w_exp_up.reshape(w_exp_up.shape[1:]),
        wed=w_exp_down.reshape(w_exp_down.shape[1:]),
    )

    y_prompt, k_p, v_p, hfin = _trunk(x_prompt, mod_prompt, wts, None, None, None, None,
                                      tm=256, tq=256, want_state=True)
    new_k = k_p.reshape(bp, tp, N_KV_HEADS, LANES)[..., :HEAD_DIM][:, None]
    new_v = v_p.reshape(bp, tp, N_KV_HEADS, LANES)[..., :HEAD_DIM][:, None]
    new_state = hfin.reshape(bp, 1, 2, SSD_HEADS, SSD_HEAD_DIM, D_STATE)

    past = cache_k.shape[2]
    ctx_k = _dup_heads(cache_k[:, l].reshape(bs, past, KV_W)).astype(BF16)
    ctx_v = _dup_heads(cache_v[:, l].reshape(bs, past, KV_W)).astype(BF16)
    h0 = state_ssm[:, l].reshape(bs, 2, SSD_HEADS // 2, 2 * SSD_HEAD_DIM, D_STATE)
    y_sample, _, _, _ = _trunk(x_sample, mod_sample, wts, _rope_tables(ts), ctx_k, ctx_v, h0,
                               tm=256, tq=256, want_state=False)
    return (y_prompt, y_sample, new_k, new_v, new_state)
```

```python
import functools

import numpy as np
import jax
import jax.numpy as jnp
from jax import lax
from jax.experimental import pallas as pl
from jax.experimental.pallas import tpu as pltpu
from jax.experimental.pallas import tpu_sc as plsc

F32 = jnp.float32
BF16 = jnp.bfloat16

D_MODEL = 1024
GRID_W = 64
EPS = 1e-6
N_HEADS = 16
N_KV_HEADS = 4
HEAD_DIM = 64
ATTN_W = N_HEADS * HEAD_DIM
KV_W = N_KV_HEADS * HEAD_DIM
ROPE_THETA = 10000.0
D_INNER = 2048
SSD_HEAD_DIM = 64
SSD_HEADS = 32
SSD_GROUPS = 4
D_STATE = 128
D_CONV = 4
CHUNK = 128
CONV_CH = D_INNER + 2 * SSD_GROUPS * D_STATE
N_EXPERTS = 64
TOP_K = 8
N_EXPERT_GROUPS = 8
TOPK_GROUPS = 4
D_EXPERT = 256
D_SHARED = 256
ROUTED_SCALE = 2.5
SPLIT_SIZES = (ATTN_W, KV_W, KV_W, D_INNER, CONV_CH, 2 * SSD_HEADS, 2 * D_MODEL)

LANES = 128
KVD_W = N_KV_HEADS * LANES
C_Q, C_K, C_V, C_G, C_Z, C_X, C_DT, C_END = 0, 1024, 1280, 1536, 3584, 5632, 8704, 8960
VMEM_LIMIT = 56 * 1024 * 1024
Q_SCALE = HEAD_DIM ** -0.5 * 1.4426950408889634


def _cparams(sem):
    return pltpu.CompilerParams(dimension_semantics=sem, vmem_limit_bytes=VMEM_LIMIT)


def _silu(x):
    return x * jax.nn.sigmoid(x)


def _bdot(a, b):
    return jnp.dot(a.astype(BF16), b.astype(BF16), preferred_element_type=F32)


def _bdot_nt(a, b):
    return lax.dot_general(a.astype(BF16), b.astype(BF16), (((1,), (1,)), ((), ())),
                           preferred_element_type=F32)


def _mod_kernel(c_ref, w_ref, b_ref, o_ref):
    o_ref[...] = _bdot(_silu(c_ref[...]), w_ref[...]) + b_ref[...]


def _mod_call(cvec, w_mod, b_mod):
    n = w_mod.shape[1]
    bn = 1024
    return pl.pallas_call(
        _mod_kernel,
        out_shape=jax.ShapeDtypeStruct((8, n), F32),
        grid=(n // bn,),
        in_specs=[pl.BlockSpec((8, D_MODEL), lambda j: (0, 0)),
                  pl.BlockSpec((D_MODEL, bn), lambda j: (0, j)),
                  pl.BlockSpec((1, bn), lambda j: (0, j))],
        out_specs=pl.BlockSpec((8, bn), lambda j: (0, j)),
        compiler_params=_cparams(("arbitrary",)),
        name="mod",
    )(cvec, w_mod, b_mod)


def _inproj_kernel(*refs, rope, emit_kv):
    if emit_kv:
        (x_ref, mod_ref, g1_ref, w_ref, qg_ref, kg_ref, cos_ref, sin_ref,
         q_ref, k_ref, v_ref, gates_ref, z_ref, xbc_ref, dt_ref, kraw_ref, vraw_ref) = refs
    else:
        (x_ref, mod_ref, g1_ref, w_ref, qg_ref, kg_ref, cos_ref, sin_ref,
         q_ref, k_ref, v_ref, gates_ref, z_ref, xbc_ref, dt_ref) = refs
    tm = x_ref.shape[0]
    x = x_ref[...]
    inv = lax.rsqrt(jnp.mean(x * x, axis=-1, keepdims=True) + EPS)
    h = (x * inv) * g1_ref[...]
    h = h * (1.0 + mod_ref[:, 1024:2048]) + mod_ref[:, 0:1024]
    hb = h.astype(BF16)

    lane = lax.broadcasted_iota(jnp.int32, (tm, LANES), 1)
    lo = lane < HEAD_DIM
    even = (lane & 1) == 0
    if rope:
        cos = cos_ref[...]
        sin = sin_ref[...]

    def rope_fn(blk):
        nxt = pltpu.roll(blk, LANES - 1, 1)
        prv = pltpu.roll(blk, 1, 1)
        return blk * cos + jnp.where(even, nxt, prv) * sin

    def head_norm(blk, g):
        sq = blk * blk
        s_all = jnp.sum(sq, axis=-1, keepdims=True)
        s_lo = jnp.sum(jnp.where(lo, sq, 0.0), axis=-1, keepdims=True)
        ms = jnp.where(lo, s_lo, s_all - s_lo) * (1.0 / HEAD_DIM)
        return blk * lax.rsqrt(ms + EPS) * g

    def dup_heads(blk):
        sw = pltpu.roll(blk, HEAD_DIM, 1)
        return jnp.where(lo, blk, sw), jnp.where(lo, sw, blk)

    qg = qg_ref[...]
    kg = kg_ref[...]
    q = jnp.dot(hb, w_ref[:, C_Q:C_K], preferred_element_type=F32)
    for j in range(ATTN_W // LANES):
        blk = head_norm(q[:, j * LANES:(j + 1) * LANES], qg)
        if rope:
            blk = rope_fn(blk)
        q_ref[:, j * LANES:(j + 1) * LANES] = (blk * Q_SCALE).astype(q_ref.dtype)

    k = jnp.dot(hb, w_ref[:, C_K:C_V], preferred_element_type=F32)
    v = jnp.dot(hb, w_ref[:, C_V:C_G], preferred_element_type=F32)
    for j in range(KV_W // LANES):
        kb = head_norm(k[:, j * LANES:(j + 1) * LANES], kg)
        vb = v[:, j * LANES:(j + 1) * LANES]
        if emit_kv:
            kraw_ref[:, j * LANES:(j + 1) * LANES] = kb
            vraw_ref[:, j * LANES:(j + 1) * LANES] = vb
        if rope:
            kb = rope_fn(kb)
        for i, (kd, vd) in enumerate(zip(dup_heads(kb), dup_heads(vb))):
            c0 = (2 * j + i) * LANES
            k_ref[:, c0:c0 + LANES] = kd.astype(k_ref.dtype)
            v_ref[:, c0:c0 + LANES] = vd.astype(v_ref.dtype)

    gates_ref[...] = jnp.dot(hb, w_ref[:, C_G:C_Z], preferred_element_type=F32).astype(gates_ref.dtype)
    z_ref[...] = jnp.dot(hb, w_ref[:, C_Z:C_X], preferred_element_type=F32).astype(z_ref.dtype)
    xbc_ref[...] = jnp.dot(hb, w_ref[:, C_X:C_DT], preferred_element_type=F32).astype(xbc_ref.dtype)
    dt_ref[...] = jnp.dot(hb, w_ref[:, C_DT:C_END], preferred_element_type=F32)


def _inproj_call(x, mod_rows, g1, w, qg, kg, cos, sin, *, rope, emit_kv, tm):
    b, t, _ = x.shape
    nt = t // tm
    tok = lambda width: pl.BlockSpec((None, tm, width), lambda bi, i: (bi, i, 0))
    const2 = lambda shape: pl.BlockSpec(shape, lambda bi, i: (0, 0))
    out_shape = [
        jax.ShapeDtypeStruct((b, t, ATTN_W), BF16),
        jax.ShapeDtypeStruct((b, t, KVD_W), BF16),
        jax.ShapeDtypeStruct((b, t, KVD_W), BF16),
        jax.ShapeDtypeStruct((b, t, 2 * D_MODEL), BF16),
        jax.ShapeDtypeStruct((b, t, D_INNER), BF16),
        jax.ShapeDtypeStruct((b, t, CONV_CH), BF16),
        jax.ShapeDtypeStruct((b, t, 2 * LANES), F32),
    ]
    out_specs = [tok(ATTN_W), tok(KVD_W), tok(KVD_W), tok(2 * D_MODEL), tok(D_INNER), tok(CONV_CH),
                 tok(2 * LANES)]
    if emit_kv:
        out_shape += [jax.ShapeDtypeStruct((b, t, KV_W), F32)] * 2
        out_specs += [tok(KV_W), tok(KV_W)]
    return pl.pallas_call(
        functools.partial(_inproj_kernel, rope=rope, emit_kv=emit_kv),
        out_shape=tuple(out_shape),
        grid=(b, nt),
        in_specs=[tok(D_MODEL),
                  pl.BlockSpec((None, 1, 6 * D_MODEL), lambda bi, i: (bi, 0, 0)),
                  const2((1, D_MODEL)),
                  pl.BlockSpec((D_MODEL, C_END), lambda bi, i: (0, 0), pipeline_mode=pl.Buffered(1)),
                  const2((1, LANES)), const2((1, LANES)),
                  pl.BlockSpec((tm, LANES), lambda bi, i: (i, 0)),
                  pl.BlockSpec((tm, LANES), lambda bi, i: (i, 0))],
        out_specs=tuple(out_specs),
        compiler_params=_cparams(("arbitrary", "arbitrary")),
        name="inproj",
    )(x, mod_rows, g1, w, qg, kg, cos, sin)


KEY_CHUNK = 512


def _key_chunk(n):
    return KEY_CHUNK if n % KEY_CHUNK == 0 else n


def _attn_kernel(*refs, has_ctx):
    if has_ctx:
        q_ref, k_ref, v_ref, kctx_ref, vctx_ref, o_ref = refs
        sources = ((k_ref, v_ref), (kctx_ref, vctx_ref))
    else:
        q_ref, k_ref, v_ref, o_ref = refs
        sources = ((k_ref, v_ref),)
    tq = q_ref.shape[0]
    lane = lax.broadcasted_iota(jnp.int32, (tq, LANES), 1)
    lo = lane < HEAD_DIM
    qs = []
    for j in range(2):
        q2 = q_ref[:, j * LANES:(j + 1) * LANES]
        zero = jnp.zeros_like(q2)
        qs += [jnp.where(lo, q2, zero), jnp.where(lo, zero, q2)]
    q4 = jnp.concatenate(qs, axis=0)
    rows = 4 * tq
    m = jnp.full((rows, 1), -jnp.inf, F32)
    acc = jnp.zeros((rows, LANES), F32)
    chunks = [(kr, vr, c, _key_chunk(kr.shape[0])) for kr, vr in sources
              for c in range(kr.shape[0] // _key_chunk(kr.shape[0]))]
    for kr, vr, c, kc in chunks:
        kch = kr[c * kc:(c + 1) * kc, :].astype(BF16)
        vch = vr[c * kc:(c + 1) * kc, :].astype(BF16)
        lane_k = lax.broadcasted_iota(jnp.int32, (kc, LANES), 1)
        vch = jnp.where(lane_k < HEAD_DIM, vch, jnp.ones_like(vch))
        s = _bdot_nt(q4, kch)
        m_new = jnp.maximum(m, jnp.max(s, axis=-1, keepdims=True))
        alpha = jnp.exp2(m - m_new)
        p = jnp.exp2((s - m_new).astype(BF16))
        acc = acc * alpha + jnp.dot(p, vch, preferred_element_type=F32)
        m = m_new
    o = acc * (1.0 / pltpu.roll(acc, HEAD_DIM, 1))
    for j in range(2):
        oa = o[(2 * j) * tq:(2 * j + 1) * tq]
        ob = pltpu.roll(o[(2 * j + 1) * tq:(2 * j + 2) * tq], HEAD_DIM, 1)
        o_ref[:, j * LANES:(j + 1) * LANES] = jnp.where(lo, oa, ob).astype(o_ref.dtype)


def _attn_call(q, k, v, kctx, vctx, *, tq):
    b, t, _ = q.shape
    tk = k.shape[1]
    nq = t // tq
    has_ctx = kctx is not None
    kv_spec = lambda n: pl.BlockSpec((None, n, LANES), lambda bi, g, i: (bi, 0, g))
    in_specs = [pl.BlockSpec((None, tq, 2 * LANES), lambda bi, g, i: (bi, i, g)), kv_spec(tk), kv_spec(tk)]
    args = [q, k, v]
    if has_ctx:
        in_specs += [kv_spec(kctx.shape[1]), kv_spec(kctx.shape[1])]
        args += [kctx, vctx]
    return pl.pallas_call(
        functools.partial(_attn_kernel, has_ctx=has_ctx),
        out_shape=jax.ShapeDtypeStruct((b, t, ATTN_W), BF16),
        grid=(b, N_KV_HEADS, nq),
        in_specs=in_specs,
        out_specs=pl.BlockSpec((None, tq, 2 * LANES), lambda bi, g, i: (bi, i, g)),
        compiler_params=_cparams(("arbitrary", "arbitrary", "arbitrary")),
        name="attn",
    )(*args)


LOG2E = 1.4426950408889634


def _softplus(x):
    return jnp.maximum(x, 0.0) + jnp.log(1.0 + jnp.exp(-jnp.abs(x)))


def _ssd_kernel(*refs, nc, reverse, has_h0, want_hfin):
    refs = list(refs)
    conv = not reverse
    if conv:
        xbc_ref, prev_ref, next_ref, cw_ref, cb_ref, dsk_ref = refs[:6]
        refs = refs[6:]
    else:
        xc_ref = refs.pop(0)
    dt_ref, an_ref, dtb_ref, tri_ref = refs[:4]
    refs = refs[4:]
    h0_ref = refs.pop(0) if has_h0 else None
    hprev_ref = refs.pop(0) if (want_hfin and reverse) else None
    y_ref = refs.pop(0)
    xco_ref = refs.pop(0) if conv else None
    hfin_ref = refs.pop(0) if want_hfin else None
    h_scr = refs.pop(0)

    L = CHUNK
    c = pl.program_id(1)
    cidx = (nc - 1 - c) if reverse else c

    @pl.when(c == 0)
    def _():
        if has_h0:
            h_scr[...] = h0_ref[...]
        else:
            h_scr[...] = jnp.zeros_like(h_scr)

    row = lax.broadcasted_iota(jnp.int32, (L, LANES), 0)
    lane = lax.broadcasted_iota(jnp.int32, (L, LANES), 1)
    lo = lane < SSD_HEAD_DIM
    top = row < SSD_HEAD_DIM

    if conv:
        first = cidx == 0
        last = cidx == nc - 1

        def cols(a, w):
            xm = xbc_ref[:, a:a + w].astype(F32)
            rw = lax.broadcasted_iota(jnp.int32, (L, w), 0)
            p6 = jnp.where(first, 0.0, prev_ref[6:7, a:a + w].astype(F32))
            p7 = jnp.where(first, 0.0, prev_ref[7:8, a:a + w].astype(F32))
            n0 = jnp.where(last, 0.0, next_ref[0:1, a:a + w].astype(F32))
            r1 = jnp.where(rw == 0, p7, pltpu.roll(xm, 1, 0))
            r2 = jnp.where(rw == 0, p6, jnp.where(rw == 1, p7, pltpu.roll(xm, 2, 0)))
            rn = jnp.where(rw == L - 1, n0, pltpu.roll(xm, L - 1, 0))
            y = (r2 * cw_ref[0:1, a:a + w] + r1 * cw_ref[1:2, a:a + w] + xm * cw_ref[2:3, a:a + w]
                 + rn * cw_ref[3:4, a:a + w] + cb_ref[:, a:a + w])
            y = _silu(y).astype(BF16)
            xco_ref[:, a:a + w] = y
            return y
    else:
        def cols(a, w):
            return xc_ref[:, a:a + w]

    causal = tri_ref[...] > 0.0
    dt = _softplus(dt_ref[...] + dtb_ref[...])
    la2 = dt * (an_ref[...] * LOG2E)
    acum2 = jnp.dot(tri_ref[...], la2, preferred_element_type=F32, precision=lax.Precision.HIGHEST)
    dt_t = dt.T
    acum2_t = acum2.T
    tot2_t = jnp.sum(la2.T, axis=1, keepdims=True)
    lg_dt_t = jnp.log2(dt_t)
    r_t = lg_dt_t - acum2_t
    w_t = jnp.exp2(lg_dt_t + tot2_t - acum2_t)
    e_acum = jnp.exp2(acum2)
    e_tot_t = jnp.exp2(tot2_t)

    for g in range(SSD_GROUPS):
        bgb = cols(D_INNER + g * D_STATE, D_STATE)
        cgb = cols(D_INNER + SSD_GROUPS * D_STATE + g * D_STATE, D_STATE)
        cbm = _bdot_nt(cgb, bgb)
        h_grp = h_scr[4 * g:4 * g + 4]
        yo_grp = _bdot_nt(cgb, h_grp.reshape(4 * LANES, D_STATE))
        for pr in range(4):
            hp = g * 4 + pr
            ha, hb = 2 * hp, 2 * hp + 1
            xpb = cols(hp * LANES, LANES)
            zero = jnp.zeros_like(xpb)
            xs = jnp.concatenate([jnp.where(lo, xpb, zero), jnp.where(lo, zero, xpb)], axis=0)
            ms = []
            for hh in (ha, hb):
                e = jnp.exp2(acum2[:, hh:hh + 1] + r_t[hh:hh + 1, :])
                ms.append((cbm * jnp.where(causal, e, 0.0)).astype(BF16))
            y = jnp.dot(jnp.concatenate(ms, axis=1), xs, preferred_element_type=F32)
            y = y + yo_grp[:, pr * LANES:(pr + 1) * LANES] * \
                jnp.where(lo, e_acum[:, ha:ha + 1], e_acum[:, hb:hb + 1])
            if conv:
                y = y + dsk_ref[:, hp * LANES:(hp + 1) * LANES] * xpb.astype(F32)
            y_ref[:, hp * LANES:(hp + 1) * LANES] = y.astype(y_ref.dtype)
            wsel = jnp.where(top, w_t[ha:ha + 1, :], w_t[hb:hb + 1, :])
            st = jnp.dot((xpb.astype(F32).T * wsel).astype(BF16), bgb, preferred_element_type=F32)
            cd = jnp.where(top, e_tot_t[ha:ha + 1, :], e_tot_t[hb:hb + 1, :])
            h_scr[hp] = h_grp[pr] * cd + st

    if want_hfin:
        @pl.when(c == nc - 1)
        def _():
            if reverse:
                hfin_ref[0] = hprev_ref[...]
                hfin_ref[1] = h_scr[...]
            else:
                hfin_ref[...] = h_scr[...]


def _ssd_sweep(xin, dt, wts, h0, hprev, *, reverse, want_hfin):
    b, t, _ = xin.shape
    nc = t // CHUNK
    has_h0 = h0 is not None
    rb = CHUNK // 8
    nrb = t // 8
    d = 1 if reverse else 0
    cmap = (lambda c: nc - 1 - c) if reverse else (lambda c: c)
    hshape = (SSD_HEADS // 2, 2 * SSD_HEAD_DIM, D_STATE)

    chunk_spec = pl.BlockSpec((None, CHUNK, CONV_CH), lambda bi, c: (bi, cmap(c), 0))
    if reverse:
        in_specs = [chunk_spec]
        args = [xin]
    else:
        in_specs = [
            chunk_spec,
            pl.BlockSpec((None, 8, CONV_CH), lambda bi, c: (bi, jnp.maximum(c * rb - 1, 0), 0)),
            pl.BlockSpec((None, 8, CONV_CH), lambda bi, c: (bi, jnp.minimum((c + 1) * rb, nrb - 1), 0)),
            pl.BlockSpec((D_CONV, CONV_CH), lambda bi, c: (0, 0)),
            pl.BlockSpec((1, CONV_CH), lambda bi, c: (0, 0)),
            pl.BlockSpec((1, D_INNER), lambda bi, c: (0, 0)),
        ]
        args = [xin, xin, xin, wts["conv_w"], wts["conv_b"], wts["dskip"]]
    in_specs += [
        pl.BlockSpec((None, CHUNK, LANES), lambda bi, c: (bi, cmap(c), d)),
        pl.BlockSpec((None, 1, LANES), lambda bi, c: (d, 0, 0)),
        pl.BlockSpec((None, 1, LANES), lambda bi, c: (d, 0, 0)),
        pl.BlockSpec((None, CHUNK, CHUNK), lambda bi, c: (d, 0, 0)),
    ]
    args += [dt, wts["a_neg"], wts["dt_bias"], wts["tri"]]
    if has_h0:
        in_specs.append(pl.BlockSpec((None, None) + hshape, lambda bi, c: (bi, d, 0, 0, 0)))
        args.append(h0)
    if want_hfin and reverse:
        in_specs.append(pl.BlockSpec((None,) + hshape, lambda bi, c: (bi, 0, 0, 0)))
        args.append(hprev)
    out_shape = [jax.ShapeDtypeStruct((b, t, D_INNER), BF16)]
    out_specs = [pl.BlockSpec((None, CHUNK, D_INNER), lambda bi, c: (bi, cmap(c), 0))]
    if not reverse:
        out_shape.append(jax.ShapeDtypeStruct((b, t, CONV_CH), BF16))
        out_specs.append(pl.BlockSpec((None, CHUNK, CONV_CH), lambda bi, c: (bi, c, 0)))
    if want_hfin and reverse:
        out_shape.append(jax.ShapeDtypeStruct((b, 2) + hshape, F32))
        out_specs.append(pl.BlockSpec((None, 2) + hshape, lambda bi, c: (bi, 0, 0, 0, 0)))
    elif want_hfin:
        out_shape.append(jax.ShapeDtypeStruct((b,) + hshape, F32))
        out_specs.append(pl.BlockSpec((None,) + hshape, lambda bi, c: (bi, 0, 0, 0)))
    return pl.pallas_call(
        functools.partial(_ssd_kernel, nc=nc, reverse=reverse, has_h0=has_h0, want_hfin=want_hfin),
        out_shape=tuple(out_shape),
        grid=(b, nc),
        in_specs=in_specs,
        out_specs=tuple(out_specs),
        scratch_shapes=[pltpu.VMEM(hshape, F32)],
        compiler_params=_cparams(("arbitrary", "arbitrary")),
        name="ssd_bwd" if reverse else "ssd_fwd",
    )(*args)


def _ssd_call(xbc, dt, wts, h0, *, want_hfin):
    res = _ssd_sweep(xbc, dt, wts, h0, None, reverse=False, want_hfin=want_hfin)
    y_f, xc = res[0], res[1]
    hf = res[2] if want_hfin else None
    res = _ssd_sweep(xc, dt, wts, h0, hf, reverse=True, want_hfin=want_hfin)
    return y_f, res[0], (res[1] if want_hfin else None)


def _route(logits_t, bias_col):
    e, n = logits_t.shape
    per = e // N_EXPERT_GROUPS
    scores = jax.nn.sigmoid(logits_t)
    sel = scores + bias_col
    neg = jnp.float32(-jnp.inf)
    gs = []
    for g in range(N_EXPERT_GROUPS):
        blk = sel[g * per:(g + 1) * per, :]
        m1 = jnp.max(blk, axis=0, keepdims=True)
        is_m1 = blk == m1
        cnt = jnp.sum(jnp.where(is_m1, 1.0, 0.0), axis=0, keepdims=True)
        m2 = jnp.max(jnp.where(is_m1, neg, blk), axis=0, keepdims=True)
        gs.append(m1 + jnp.where(cnt >= 2.0, m1, m2))
    keep = []
    for g in range(N_EXPERT_GROUPS):
        rank = jnp.zeros_like(gs[g])
        for j in range(N_EXPERT_GROUPS):
            if j == g:
                continue
            beats = (gs[j] > gs[g]) if j > g else (gs[j] >= gs[g])
            rank = rank + jnp.where(beats, 1.0, 0.0)
        keep.append(rank < float(TOPK_GROUPS))
    selm = jnp.concatenate(
        [jnp.where(keep[g], sel[g * per:(g + 1) * per, :], neg) for g in range(N_EXPERT_GROUPS)], axis=0)
    eidx = lax.broadcasted_iota(jnp.int32, (e, n), 0).astype(F32)
    cur = selm
    picks = []
    for _ in range(TOP_K):
        m = jnp.max(cur, axis=0, keepdims=True)
        idx = jnp.min(jnp.where(cur == m, eidx, float(e)), axis=0, keepdims=True)
        hit = eidx == idx
        picks.append((idx, hit))
        cur = jnp.where(hit, neg, cur)
    return scores, picks


def _pack_bf16_pairs(h):
    c = h.shape[1] // 2
    lo = pltpu.bitcast(h[:, :c].astype(BF16).astype(F32), jnp.uint32)
    hi = pltpu.bitcast(h[:, c:].astype(BF16).astype(F32), jnp.uint32)
    return (lo >> 16) | (hi & jnp.uint32(0xFFFF0000))


def _unpack_bf16_pairs(w):
    lo = pltpu.bitcast(w << 16, F32)
    hi = pltpu.bitcast(w & jnp.uint32(0xFFFF0000), F32)
    return lo, hi


def _rows8(rows):
    n = rows[0].shape[1]
    ridx = lax.broadcasted_iota(jnp.int32, (TOP_K, n), 0)
    out = jnp.zeros((TOP_K, n), rows[0].dtype)
    for k, r in enumerate(rows):
        out = jnp.where(ridx == k, r, out)
    return out


MERGE_SUB = 256


def _merge_kernel(x_ref, attn_ref, yf_ref, yb_ref, z_ref, gates_ref, mod_ref, wa_ref, ws_ref, wo_ref,
                  sg_ref, n2_ref, wr_ref, rb_ref, wsg_ref, wsu_ref, wsd_ref,
                  xb_ref, ha_ref, hb_ref, eid_ref, pos_ref, wk_ref, cnt_ref):
    tm = x_ref.shape[0]
    sub = MERGE_SUB

    @pl.when(pl.program_id(0) == 0)
    def _():
        cnt_ref[...] = jnp.zeros_like(cnt_ref)

    r_i = lax.broadcasted_iota(jnp.int32, (sub, sub), 0)
    c_i = lax.broadcasted_iota(jnp.int32, (sub, sub), 1)
    before = jnp.where(r_i < c_i, 1.0, 0.0).astype(BF16)
    cnt = cnt_ref[:, 0:1]

    for r0 in range(0, tm, sub):
        rs = slice(r0, r0 + sub)
        x = x_ref[rs, :]
        yy = yf_ref[rs, :].astype(F32) + yb_ref[rs, :].astype(F32)
        u = yy * _silu(z_ref[rs, :]).astype(F32)
        un = u * lax.rsqrt(jnp.mean(u * u, axis=-1, keepdims=True) + EPS) * sg_ref[...]
        ssd_o = _bdot(un, ws_ref[...])
        attn_o = jnp.dot(attn_ref[rs, :], wa_ref[...], preferred_element_type=F32)
        ga = jax.nn.sigmoid(gates_ref[rs, 0:D_MODEL]).astype(F32)
        gs = jax.nn.sigmoid(gates_ref[rs, D_MODEL:2 * D_MODEL]).astype(F32)
        mix = _bdot(ga * attn_o + gs * ssd_o, wo_ref[...])
        x1 = x + mod_ref[:, 2048:3072] * mix
        h2 = x1 * lax.rsqrt(jnp.mean(x1 * x1, axis=-1, keepdims=True) + EPS) * n2_ref[...]
        h2 = h2 * (1.0 + mod_ref[:, 4096:5120]) + mod_ref[:, 3072:4096]
        h2b = h2.astype(BF16)
        ha_ref[rs, :] = _pack_bf16_pairs(h2[:, :D_MODEL // 2])
        hb_ref[rs, :] = _pack_bf16_pairs(h2[:, D_MODEL // 2:])

        logits_t = _bdot_nt(wr_ref[...], h2b)
        scores, picks = _route(logits_t, rb_ref[...])
        chosen = jnp.zeros_like(scores)
        for _, hit in picks:
            chosen = chosen + jnp.where(hit, 1.0, 0.0)
        pos = cnt + jnp.dot(chosen.astype(BF16), before, preferred_element_type=F32)
        cnt = cnt + jnp.sum(chosen, axis=1, keepdims=True)
        poss = [jnp.sum(jnp.where(hit, pos, 0.0), axis=0, keepdims=True) for _, hit in picks]
        wks = [jnp.sum(jnp.where(hit, scores, 0.0), axis=0, keepdims=True) for _, hit in picks]
        wsum = wks[0]
        for w in wks[1:]:
            wsum = wsum + w
        eid_ref[:, rs] = _rows8([idx for idx, _ in picks]).astype(jnp.int32)
        pos_ref[:, rs] = _rows8(poss).astype(jnp.int32)
        wk8 = _rows8(wks) / wsum * ROUTED_SCALE
        wk_ref[rs, :] = jnp.concatenate([wk8, jnp.zeros((LANES - TOP_K, sub), F32)], axis=0).T

        hid = _silu(jnp.dot(h2b, wsg_ref[...], preferred_element_type=F32)) * \
            jnp.dot(h2b, wsu_ref[...], preferred_element_type=F32)
        xb_ref[rs, :] = x1 + mod_ref[:, 5120:6144] * _bdot(hid, wsd_ref[...])

    cnt_ref[...] = jnp.broadcast_to(cnt, cnt_ref.shape)


def _merge_call(x, attn, y_f, y_b, z, gates, mod_rows, wa, ws, wo, sg, n2, wr_t, rb, wsg, wsu, wsd, *, tm):
    b, t, _ = x.shape
    n = b * t
    flat = lambda a: a.reshape(n, a.shape[-1])
    tok = lambda width: pl.BlockSpec((tm, width), lambda i: (i, 0))
    const2 = lambda shape: pl.BlockSpec(shape, lambda i: (0, 0), pipeline_mode=pl.Buffered(1))
    k8 = pl.BlockSpec((TOP_K, tm), lambda i: (0, i))
    half = D_MODEL // 4
    return pl.pallas_call(
        _merge_kernel,
        out_shape=(jax.ShapeDtypeStruct((n, D_MODEL), F32),
                   jax.ShapeDtypeStruct((n, half), jnp.uint32),
                   jax.ShapeDtypeStruct((n, half), jnp.uint32),
                   jax.ShapeDtypeStruct((TOP_K, n), jnp.int32),
                   jax.ShapeDtypeStruct((TOP_K, n), jnp.int32),
                   jax.ShapeDtypeStruct((n, LANES), F32),
                   jax.ShapeDtypeStruct((N_EXPERTS, LANES), F32)),
        grid=(n // tm,),
        in_specs=[tok(D_MODEL), tok(ATTN_W), tok(D_INNER), tok(D_INNER), tok(D_INNER), tok(2 * D_MODEL),
                  pl.BlockSpec((None, 1, 6 * D_MODEL), lambda i: ((i * tm) // t, 0, 0)),
                  const2((ATTN_W, D_MODEL)), const2((D_INNER, D_MODEL)), const2((D_MODEL, D_MODEL)),
                  const2((1, D_INNER)), const2((1, D_MODEL)),
                  const2((N_EXPERTS, D_MODEL)), const2((N_EXPERTS, 1)),
                  const2((D_MODEL, D_SHARED)), const2((D_MODEL, D_SHARED)), const2((D_SHARED, D_MODEL))],
        out_specs=(tok(D_MODEL), tok(half), tok(half), k8, k8, tok(LANES),
                   pl.BlockSpec((N_EXPERTS, LANES), lambda i: (0, 0))),
        compiler_params=_cparams(("arbitrary",)),
        name="merge",
    )(flat(x), flat(attn), flat(y_f), flat(y_b), flat(z), flat(gates), mod_rows, wa, ws, wo, sg, n2, wr_t, rb,
      wsg, wsu, wsd)


ROW_TILE = 512
SC_WINDOW = 128


def _slots_kernel(start_ref, eid_ref, pos_ref, slot_ref):
    eid = eid_ref[...]
    slot = pos_ref[...]
    for e in range(N_EXPERTS):
        slot = slot + jnp.where(eid == e, start_ref[e], 0)
    slot_ref[...] = slot


def _slots_call(start, eid, pos):
    n = eid.shape[1]
    bn = 2048 if n % 2048 == 0 else n
    spec = pl.BlockSpec((TOP_K, bn), lambda i, s: (0, i))
    return pl.pallas_call(
        _slots_kernel,
        out_shape=jax.ShapeDtypeStruct((TOP_K, n), jnp.int32),
        grid_spec=pltpu.PrefetchScalarGridSpec(num_scalar_prefetch=1, grid=(n // bn,),
                                               in_specs=[spec, spec], out_specs=spec),
        compiler_params=_cparams(("arbitrary",)),
        name="slots",
    )(start, eid, pos)


def _sc_dispatch(x, slots, p):
    n, d = x.shape
    mesh = plsc.VectorSubcoreMesh(core_axis_name="core", subcore_axis_name="subcore")

    @functools.partial(pl.kernel, out_type=jax.ShapeDtypeStruct((p, d), x.dtype), mesh=mesh)
    def k(x_hbm, s_hbm, o_hbm):
        def body(x_vmem, s_vmem):
            for kk in range(TOP_K):
                pltpu.sync_copy(x_vmem, o_hbm.at[s_vmem.at[kk]])

        pltpu.emit_pipeline(
            body,
            grid=(n // SC_WINDOW,),
            in_specs=[pl.BlockSpec((SC_WINDOW, d), index_map=lambda i: (i, 0)),
                      pl.BlockSpec((TOP_K, SC_WINDOW), index_map=lambda i: (0, i))],
            out_specs=[],
            core_axis_name=("core", "subcore"),
            dimension_semantics=(pltpu.PARALLEL,),
        )(x_hbm, s_hbm)

    return k(x, slots)


def _sc_combine(y, slots):
    kk, n = slots.shape
    d = y.shape[1]
    mesh = plsc.VectorSubcoreMesh(core_axis_name="core", subcore_axis_name="subcore")

    @functools.partial(pl.kernel, out_type=jax.ShapeDtypeStruct((kk * n, d), y.dtype), mesh=mesh)
    def k(y_hbm, s_hbm, o_hbm):
        def body(s_vmem, o_vmem):
            pltpu.sync_copy(y_hbm.at[s_vmem.at[0]], o_vmem)

        pltpu.emit_pipeline(
            body,
            grid=(kk * n // SC_WINDOW,),
            in_specs=[pl.BlockSpec((1, SC_WINDOW), index_map=lambda i: (0, i))],
            out_specs=[pl.BlockSpec((SC_WINDOW, d), index_map=lambda i: (i, 0))],
            core_axis_name=("core", "subcore"),
            dimension_semantics=(pltpu.PARALLEL,),
        )(s_hbm, o_hbm)

    return k(y, slots.reshape(1, kk * n)).reshape(kk, n, d)


FFN_IN_BUFS = 4
FFN_OUT_BUFS = 3


def _ffn_kernel(st_ref, nt_ref, wg_ref, wu_ref, wd_ref, xa_hbm, xb_hbm, ya_hbm, yb_hbm,
                wg_s, wu_s, wd_s, xa_buf, xb_buf, ya_buf, yb_buf, in_sem, out_sem):
    e = pl.program_id(0)
    ne = pl.num_programs(0)
    n = nt_ref[e]
    t = ROW_TILE
    ahead = FFN_IN_BUFS - 1

    def fetch(row, slot):
        r = pl.multiple_of(row, t)
        return (pltpu.make_async_copy(xa_hbm.at[pl.ds(r, t)], xa_buf.at[slot], in_sem.at[0, slot]),
                pltpu.make_async_copy(xb_hbm.at[pl.ds(r, t)], xb_buf.at[slot], in_sem.at[1, slot]))

    def put(row, slot):
        r = pl.multiple_of(row, t)
        return (pltpu.make_async_copy(ya_buf.at[slot], ya_hbm.at[pl.ds(r, t)], out_sem.at[0, slot]),
                pltpu.make_async_copy(yb_buf.at[slot], yb_hbm.at[pl.ds(r, t)], out_sem.at[1, slot]))

    def start_head(expert):
        for k in range(ahead):
            @pl.when(k < nt_ref[expert])
            def _():
                for c in fetch(st_ref[expert] + k * t, k):
                    c.start()

    @pl.when(e == 0)
    def _():
        start_head(0)

    @pl.when(n > 0)
    def _():
        wg_s[...] = wg_ref[...].astype(BF16)
        wu_s[...] = wu_ref[...].astype(BF16)
        wd_s[...] = wd_ref[...].astype(BF16)
        base = st_ref[e]

        def body(i, carry):
            si = lax.rem(i, FFN_IN_BUFS)
            so = lax.rem(i, FFN_OUT_BUFS)
            row = base + i * t
            for c in fetch(row, si):
                c.wait()

            @pl.when(i + ahead < n)
            def _():
                for c in fetch(row + ahead * t, lax.rem(i + ahead, FFN_IN_BUFS)):
                    c.start()

            @pl.when(i >= FFN_OUT_BUFS)
            def _():
                for c in put(row - FFN_OUT_BUFS * t, so):
                    c.wait()

            parts = _unpack_bf16_pairs(xa_buf[si]) + _unpack_bf16_pairs(xb_buf[si])
            x = jnp.concatenate(parts, axis=1).astype(BF16)
            hid = _silu(jnp.dot(x, wg_s[...], preferred_element_type=F32)) * \
                jnp.dot(x, wu_s[...], preferred_element_type=F32)
            y = _bdot(hid, wd_s[...])
            ya_buf[so] = _pack_bf16_pairs(y[:, :D_MODEL // 2])
            yb_buf[so] = _pack_bf16_pairs(y[:, D_MODEL // 2:])
            for c in put(row, so):
                c.start()
            return carry

        lax.fori_loop(0, n, body, 0)

        for k in range(FFN_OUT_BUFS):
            @pl.when(n - 1 - k >= 0)
            def _():
                j = n - 1 - k
                for c in put(base + j * t, lax.rem(j, FFN_OUT_BUFS)):
                    c.wait()

    @pl.when(e + 1 < ne)
    def _():
        start_head(jnp.minimum(e + 1, ne - 1))


def _ffn_call(start, tiles_e, xa, xb, wg, wu, wd):
    p, half = xa.shape
    wspec = lambda s: pl.BlockSpec((None,) + s, lambda e, st, nt: (e, 0, 0))
    hbm = pl.BlockSpec(memory_space=pl.ANY)
    ibuf = pltpu.VMEM((FFN_IN_BUFS, ROW_TILE, half), jnp.uint32)
    obuf = pltpu.VMEM((FFN_OUT_BUFS, ROW_TILE, half), jnp.uint32)
    return pl.pallas_call(
        _ffn_kernel,
        out_shape=(jax.ShapeDtypeStruct((p, half), jnp.uint32), jax.ShapeDtypeStruct((p, half), jnp.uint32)),
        grid_spec=pltpu.PrefetchScalarGridSpec(
            num_scalar_prefetch=2, grid=(N_EXPERTS,),
            in_specs=[wspec((D_MODEL, D_EXPERT)), wspec((D_MODEL, D_EXPERT)), wspec((D_EXPERT, D_MODEL)),
                      hbm, hbm],
            out_specs=(hbm, hbm),
            scratch_shapes=[pltpu.VMEM((D_MODEL, D_EXPERT), BF16), pltpu.VMEM((D_MODEL, D_EXPERT), BF16),
                            pltpu.VMEM((D_EXPERT, D_MODEL), BF16), ibuf, ibuf, obuf, obuf,
                            pltpu.SemaphoreType.DMA((2, FFN_IN_BUFS)),
                            pltpu.SemaphoreType.DMA((2, FFN_OUT_BUFS))]),
        compiler_params=_cparams(("arbitrary",)),
        name="ffn",
    )(start, tiles_e, wg, wu, wd, xa, xb)


def _final_kernel(xb_ref, ya_ref, yb_ref, wk_ref, mod_ref, o_ref):
    q = D_MODEL // 4
    accs = [jnp.zeros((xb_ref.shape[0], q), F32) for _ in range(4)]
    for k in range(TOP_K):
        w = wk_ref[:, k:k + 1]
        parts = _unpack_bf16_pairs(ya_ref[k]) + _unpack_bf16_pairs(yb_ref[k])
        accs = [a + w * p for a, p in zip(accs, parts)]
    for i, a in enumerate(accs):
        o_ref[:, i * q:(i + 1) * q] = xb_ref[:, i * q:(i + 1) * q] + mod_ref[:, 5120 + i * q:5120 + (i + 1) * q] * a


def _final_call(xb, ya, yb, wk, mod_rows, t, *, tm):
    n = xb.shape[0]
    half = ya.shape[2]
    tok = lambda width: pl.BlockSpec((tm, width), lambda i: (i, 0))
    yspec = pl.BlockSpec((TOP_K, tm, half), lambda i: (0, i, 0))
    return pl.pallas_call(
        _final_kernel,
        out_shape=jax.ShapeDtypeStruct((n, D_MODEL), F32),
        grid=(n // tm,),
        in_specs=[tok(D_MODEL), yspec, yspec, tok(LANES),
                  pl.BlockSpec((None, 1, 6 * D_MODEL), lambda i: ((i * tm) // t, 0, 0))],
        out_specs=tok(D_MODEL),
        compiler_params=_cparams(("arbitrary",)),
        name="final",
    )(xb, ya, yb, wk, mod_rows)


def _moe_call(ha, hb, eid, pos, wk, counts, xb, mod_rows, wg, wu, wd, t):
    n = xb.shape[0]
    max_tiles = n * TOP_K // ROW_TILE + N_EXPERTS
    p = max_tiles * ROW_TILE
    cnt = counts[:, 0].astype(jnp.int32)
    tiles_e = (cnt + ROW_TILE - 1) // ROW_TILE
    ends = jnp.cumsum(tiles_e)
    start = (ends - tiles_e) * ROW_TILE
    slots = _slots_call(start.astype(jnp.int32), eid, pos)
    xa = _sc_dispatch(ha, slots, p)
    xbb = _sc_dispatch(hb, slots, p)
    ya, yb = _ffn_call(start.astype(jnp.int32), tiles_e.astype(jnp.int32), xa, xbb, wg, wu, wd)
    ga = _sc_combine(ya, slots)
    gb = _sc_combine(yb, slots)
    return _final_call(xb, ga, gb, wk, mod_rows, t, tm=512)


def _rope_tables(t):
    n_rows = t // GRID_W
    rows = jnp.repeat(jnp.arange(n_rows), GRID_W).astype(F32)
    cols = jnp.tile(jnp.arange(GRID_W), n_rows).astype(F32)
    n_freq = HEAD_DIM // 4
    freqs = ROPE_THETA ** (-jnp.arange(n_freq, dtype=F32) / n_freq)
    ang = jnp.concatenate([rows[:, None] * freqs, cols[:, None] * freqs], axis=-1)
    ang = jnp.repeat(ang, 2, axis=-1)
    ang = jnp.concatenate([ang, ang], axis=-1)
    sign = jnp.where(jnp.arange(LANES) % 2 == 0, -1.0, 1.0).astype(F32)
    return jnp.cos(ang), jnp.sin(ang) * sign


def _dup_heads(a):
    s = a.shape[:-1]
    a4 = a.reshape(s + (N_KV_HEADS, HEAD_DIM))
    return jnp.concatenate([a4, a4], axis=-1).reshape(s + (KVD_W,))


def _prep_w_in(w_in):
    idx = np.cumsum(SPLIT_SIZES)[:-1].tolist()
    q, k, v, z, xbc, dt, gates = jnp.split(w_in, idx, axis=-1)
    pad = jnp.zeros((D_MODEL, LANES - SSD_HEADS), w_in.dtype)
    cols = [q, k, v, gates, z, xbc,
            dt[:, :SSD_HEADS], pad, dt[:, SSD_HEADS:], pad]
    return jnp.concatenate(cols, axis=-1).astype(BF16)


def _pad_heads(a):
    return jnp.pad(a.astype(F32), ((0, 0), (0, LANES - SSD_HEADS)))[:, None, :]


def _trunk(x, mod_rows, wts, rope_tabs, ctx_k, ctx_v, h0, *, tm, tq, want_state):
    b, t, _ = x.shape
    rope = rope_tabs is not None
    if rope:
        cos, sin = rope_tabs
    else:
        cos = sin = jnp.zeros((t, LANES), F32)
    res = _inproj_call(x, mod_rows, wts["g1"], wts["w_in"], wts["qg"], wts["kg"], cos, sin,
                       rope=rope, emit_kv=want_state, tm=tm)
    q, k, v, gates, z, xbc, dt = res[:7]
    kv_raw = res[7:]
    attn = _attn_call(q, k, v, ctx_k, ctx_v, tq=tq)
    y_f, y_b, hfin = _ssd_call(xbc, dt, wts, h0, want_hfin=want_state)
    xb, ha, hb, eid, pos, wk, counts = _merge_call(
        x, attn, y_f, y_b, z, gates, mod_rows, wts["wa"], wts["ws"], wts["wo"], wts["sg"],
        wts["n2"], wts["wr_t"], wts["rb"], wts["wsg"], wts["wsu"], wts["wsd"], tm=MERGE_SUB)
    out = _moe_call(ha, hb, eid, pos, wk, counts, xb, mod_rows, wts["weg"], wts["weu"], wts["wed"], t)
    return out.reshape(b, t, D_MODEL), kv_raw, hfin


def kernel(x_prompt, x_sample, cache_k, cache_v, state_ssm, c, c_ctx, w_mod, b_mod, norm1_g, norm2_g, w_in,
           q_norm_g, k_norm_g, conv_w, conv_b, a_log, dt_bias, d_skip, ssd_norm_g, w_attn_proj, w_ssd_proj,
           w_out, w_router, router_bias, w_exp_gate, w_exp_up, w_exp_down, w_sh_gate, w_sh_up, w_sh_down):
    depth = w_mod.shape[0]
    assert depth == 1, "single trunk layer"
    bp, tp, _ = x_prompt.shape
    bs, ts, _ = x_sample.shape
    l = 0
    cvec = jnp.concatenate([c_ctx[None, :], c, jnp.zeros((8 - 1 - bs, D_MODEL), F32)], axis=0)
    mod = _mod_call(cvec, w_mod.reshape(D_MODEL, 6 * D_MODEL), b_mod.reshape(1, 6 * D_MODEL))
    mod_prompt = jnp.broadcast_to(mod[0:1][:, None, :], (bp, 1, 6 * D_MODEL))
    mod_sample = mod[1:1 + bs][:, None, :]

    lower = np.tril(np.ones((CHUNK, CHUNK), np.float32))
    wts = dict(
        g1=norm1_g[l][None, :], n2=norm2_g[l][None, :],
        w_in=_prep_w_in(w_in.reshape(D_MODEL, w_in.shape[-1])),
        qg=jnp.tile(q_norm_g[l], 2)[None, :], kg=jnp.tile(k_norm_g[l], 2)[None, :],
        conv_w=conv_w[l], conv_b=conv_b[l][None, :],
        a_neg=_pad_heads(-jnp.exp(a_log[l].astype(F32))), dt_bias=_pad_heads(dt_bias[l]),
        dskip=jnp.repeat(d_skip[l].astype(F32), SSD_HEAD_DIM)[None, :],
        tri=jnp.asarray(np.stack([lower, lower.T])),
        sg=ssd_norm_g[l][None, :],
        wa=w_attn_proj[l].astype(BF16), ws=w_ssd_proj[l].astype(BF16), wo=w_out[l].astype(BF16),
        wr_t=w_router[l].T.astype(BF16), rb=router_bias[l].astype(F32)[:, None],
        wsg=w_sh_gate[l].astype(BF16), wsu=w_sh_up[l].astype(BF16), wsd=w_sh_down[l].astype(BF16),
        weg=w_exp_gate.reshape(w_exp_gate.shape[1:]), weu=w_exp_up.reshape(w_exp_up.shape[1:]),
        wed=w_exp_down.reshape(w_exp_down.shape[1:]),
    )

    y_prompt, (k_p, v_p), hfin = _trunk(x_prompt, mod_prompt, wts, None, None, None, None,
                                        tm=256, tq=256, want_state=True)
    new_k = k_p.reshape(bp, 1, tp, N_KV_HEADS, HEAD_DIM)
    new_v = v_p.reshape(bp, 1, tp, N_KV_HEADS, HEAD_DIM)
    new_state = hfin.reshape(bp, 1, 2, SSD_HEADS, SSD_HEAD_DIM, D_STATE)

    past = cache_k.shape[2]
    ctx_k = _dup_heads(cache_k[:, l].reshape(bs, past, KV_W)).astype(BF16)
    ctx_v = _dup_heads(cache_v[:, l].reshape(bs, past, KV_W)).astype(BF16)
    h0 = state_ssm[:, l].reshape(bs, 2, SSD_HEADS // 2, 2 * SSD_HEAD_DIM, D_STATE)
    y_sample, _, _ = _trunk(x_sample, mod_sample, wts, _rope_tables(ts), ctx_k, ctx_v, h0,
                               tm=256, tq=256, want_state=False)
    return (y_prompt, y_sample, new_k, new_v, new_state)
```

```python
import functools

import numpy as np
import jax
import jax.numpy as jnp
from jax import lax
from jax.experimental import pallas as pl
from jax.experimental.pallas import tpu as pltpu
from jax.experimental.pallas import tpu_sc as plsc

F32 = jnp.float32
BF16 = jnp.bfloat16

D_MODEL = 1024
GRID_W = 64
EPS = 1e-6
N_HEADS = 16
N_KV_HEADS = 4
HEAD_DIM = 64
ATTN_W = N_HEADS * HEAD_DIM
KV_W = N_KV_HEADS * HEAD_DIM
ROPE_THETA = 10000.0
D_INNER = 2048
SSD_HEAD_DIM = 64
SSD_HEADS = 32
SSD_GROUPS = 4
D_STATE = 128
D_CONV = 4
CHUNK = 128
CONV_CH = D_INNER + 2 * SSD_GROUPS * D_STATE
N_EXPERTS = 64
TOP_K = 8
N_EXPERT_GROUPS = 8
TOPK_GROUPS = 4
D_EXPERT = 256
D_SHARED = 256
ROUTED_SCALE = 2.5
SPLIT_SIZES = (ATTN_W, KV_W, KV_W, D_INNER, CONV_CH, 2 * SSD_HEADS, 2 * D_MODEL)

LANES = 128
KVD_W = N_KV_HEADS * LANES
C_Q, C_K, C_V, C_G, C_Z, C_X, C_DT, C_END = 0, 1024, 1280, 1536, 3584, 5632, 8704, 8960
VMEM_LIMIT = 56 * 1024 * 1024
Q_SCALE = HEAD_DIM ** -0.5 * 1.4426950408889634


def _cparams(sem):
    return pltpu.CompilerParams(dimension_semantics=sem, vmem_limit_bytes=VMEM_LIMIT)


def _silu(x):
    return x * jax.nn.sigmoid(x)


def _bdot(a, b):
    return jnp.dot(a.astype(BF16), b.astype(BF16), preferred_element_type=F32)


def _bdot_nt(a, b):
    return lax.dot_general(a.astype(BF16), b.astype(BF16), (((1,), (1,)), ((), ())),
                           preferred_element_type=F32)


def _mod_kernel(c_ref, w_ref, b_ref, o_ref):
    o_ref[...] = _bdot(_silu(c_ref[...]), w_ref[...]) + b_ref[...]


def _mod_call(cvec, w_mod, b_mod):
    n = w_mod.shape[1]
    bn = 1024
    return pl.pallas_call(
        _mod_kernel,
        out_shape=jax.ShapeDtypeStruct((8, n), F32),
        grid=(n // bn,),
        in_specs=[pl.BlockSpec((8, D_MODEL), lambda j: (0, 0)),
                  pl.BlockSpec((D_MODEL, bn), lambda j: (0, j)),
                  pl.BlockSpec((1, bn), lambda j: (0, j))],
        out_specs=pl.BlockSpec((8, bn), lambda j: (0, j)),
        compiler_params=_cparams(("arbitrary",)),
        name="mod",
    )(cvec, w_mod, b_mod)


def _inproj_kernel(*refs, rope, emit_kv):
    if emit_kv:
        (x_ref, mod_ref, g1_ref, w_ref, qg_ref, kg_ref, cos_ref, sin_ref,
         q_ref, k_ref, v_ref, gates_ref, z_ref, xbc_ref, dt_ref, kraw_ref, vraw_ref) = refs
    else:
        (x_ref, mod_ref, g1_ref, w_ref, qg_ref, kg_ref, cos_ref, sin_ref,
         q_ref, k_ref, v_ref, gates_ref, z_ref, xbc_ref, dt_ref) = refs
    tm = x_ref.shape[0]
    x = x_ref[...]
    inv = lax.rsqrt(jnp.mean(x * x, axis=-1, keepdims=True) + EPS)
    h = (x * inv) * g1_ref[...]
    h = h * (1.0 + mod_ref[:, 1024:2048]) + mod_ref[:, 0:1024]
    hb = h.astype(BF16)

    lane = lax.broadcasted_iota(jnp.int32, (tm, LANES), 1)
    lo = lane < HEAD_DIM
    even = (lane & 1) == 0
    if rope:
        cos = cos_ref[...]
        sin = sin_ref[...]

    def rope_fn(blk):
        nxt = pltpu.roll(blk, LANES - 1, 1)
        prv = pltpu.roll(blk, 1, 1)
        return blk * cos + jnp.where(even, nxt, prv) * sin

    def head_norm(blk, g):
        sq = blk * blk
        s_all = jnp.sum(sq, axis=-1, keepdims=True)
        s_lo = jnp.sum(jnp.where(lo, sq, 0.0), axis=-1, keepdims=True)
        ms = jnp.where(lo, s_lo, s_all - s_lo) * (1.0 / HEAD_DIM)
        return blk * lax.rsqrt(ms + EPS) * g

    def dup_heads(blk):
        sw = pltpu.roll(blk, HEAD_DIM, 1)
        return jnp.where(lo, blk, sw), jnp.where(lo, sw, blk)

    qg = qg_ref[...]
    kg = kg_ref[...]
    q = jnp.dot(hb, w_ref[:, C_Q:C_K], preferred_element_type=F32)
    for j in range(ATTN_W // LANES):
        blk = head_norm(q[:, j * LANES:(j + 1) * LANES], qg)
        if rope:
            blk = rope_fn(blk)
        q_ref[:, j * LANES:(j + 1) * LANES] = (blk * Q_SCALE).astype(q_ref.dtype)

    k = jnp.dot(hb, w_ref[:, C_K:C_V], preferred_element_type=F32)
    v = jnp.dot(hb, w_ref[:, C_V:C_G], preferred_element_type=F32)
    for j in range(KV_W // LANES):
        kb = head_norm(k[:, j * LANES:(j + 1) * LANES], kg)
        vb = v[:, j * LANES:(j + 1) * LANES]
        if emit_kv:
            kraw_ref[:, j * LANES:(j + 1) * LANES] = kb
            vraw_ref[:, j * LANES:(j + 1) * LANES] = vb
        if rope:
            kb = rope_fn(kb)
        for i, (kd, vd) in enumerate(zip(dup_heads(kb), dup_heads(vb))):
            c0 = (2 * j + i) * LANES
            k_ref[:, c0:c0 + LANES] = kd.astype(k_ref.dtype)
            v_ref[:, c0:c0 + LANES] = vd.astype(v_ref.dtype)

    gates_ref[...] = jnp.dot(hb, w_ref[:, C_G:C_Z], preferred_element_type=F32).astype(gates_ref.dtype)
    z_ref[...] = jnp.dot(hb, w_ref[:, C_Z:C_X], preferred_element_type=F32).astype(z_ref.dtype)
    xbc_ref[...] = jnp.dot(hb, w_ref[:, C_X:C_DT], preferred_element_type=F32).astype(xbc_ref.dtype)
    dt_ref[...] = jnp.dot(hb, w_ref[:, C_DT:C_END], preferred_element_type=F32)


def _inproj_call(x, mod_rows, g1, w, qg, kg, cos, sin, *, rope, emit_kv, tm):
    b, t, _ = x.shape
    nt = t // tm
    tok = lambda width: pl.BlockSpec((None, tm, width), lambda bi, i: (bi, i, 0))
    const2 = lambda shape: pl.BlockSpec(shape, lambda bi, i: (0, 0))
    out_shape = [
        jax.ShapeDtypeStruct((b, t, ATTN_W), BF16),
        jax.ShapeDtypeStruct((b, t, KVD_W), BF16),
        jax.ShapeDtypeStruct((b, t, KVD_W), BF16),
        jax.ShapeDtypeStruct((b, t, 2 * D_MODEL), BF16),
        jax.ShapeDtypeStruct((b, t, D_INNER), BF16),
        jax.ShapeDtypeStruct((b, t, CONV_CH), BF16),
        jax.ShapeDtypeStruct((b, t, 2 * LANES), F32),
    ]
    out_specs = [tok(ATTN_W), tok(KVD_W), tok(KVD_W), tok(2 * D_MODEL), tok(D_INNER), tok(CONV_CH),
                 tok(2 * LANES)]
    if emit_kv:
        out_shape += [jax.ShapeDtypeStruct((b, t, KV_W), F32)] * 2
        out_specs += [tok(KV_W), tok(KV_W)]
    return pl.pallas_call(
        functools.partial(_inproj_kernel, rope=rope, emit_kv=emit_kv),
        out_shape=tuple(out_shape),
        grid=(b, nt),
        in_specs=[tok(D_MODEL),
                  pl.BlockSpec((None, 1, 6 * D_MODEL), lambda bi, i: (bi, 0, 0)),
                  const2((1, D_MODEL)),
                  pl.BlockSpec((D_MODEL, C_END), lambda bi, i: (0, 0), pipeline_mode=pl.Buffered(1)),
                  const2((1, LANES)), const2((1, LANES)),
                  pl.BlockSpec((tm, LANES), lambda bi, i: (i, 0)),
                  pl.BlockSpec((tm, LANES), lambda bi, i: (i, 0))],
        out_specs=tuple(out_specs),
        compiler_params=_cparams(("arbitrary", "arbitrary")),
        name="inproj",
    )(x, mod_rows, g1, w, qg, kg, cos, sin)


KEY_CHUNK = 512


def _key_chunk(n):
    return KEY_CHUNK if n % KEY_CHUNK == 0 else n


def _attn_kernel(*refs, has_ctx):
    if has_ctx:
        q_ref, k_ref, v_ref, kctx_ref, vctx_ref, o_ref = refs
        sources = ((k_ref, v_ref), (kctx_ref, vctx_ref))
    else:
        q_ref, k_ref, v_ref, o_ref = refs
        sources = ((k_ref, v_ref),)
    tq = q_ref.shape[0]
    lane = lax.broadcasted_iota(jnp.int32, (tq, LANES), 1)
    lo = lane < HEAD_DIM
    qs = []
    for j in range(2):
        q2 = q_ref[:, j * LANES:(j + 1) * LANES]
        zero = jnp.zeros_like(q2)
        qs += [jnp.where(lo, q2, zero), jnp.where(lo, zero, q2)]
    q4 = jnp.concatenate(qs, axis=0)
    rows = 4 * tq
    m = jnp.full((rows, 1), -jnp.inf, F32)
    acc = jnp.zeros((rows, LANES), F32)
    chunks = [(kr, vr, c, _key_chunk(kr.shape[0])) for kr, vr in sources
              for c in range(kr.shape[0] // _key_chunk(kr.shape[0]))]
    for kr, vr, c, kc in chunks:
        kch = kr[c * kc:(c + 1) * kc, :].astype(BF16)
        vch = vr[c * kc:(c + 1) * kc, :].astype(BF16)
        lane_k = lax.broadcasted_iota(jnp.int32, (kc, LANES), 1)
        vch = jnp.where(lane_k < HEAD_DIM, vch, jnp.ones_like(vch))
        s = _bdot_nt(q4, kch)
        m_new = jnp.maximum(m, jnp.max(s, axis=-1, keepdims=True))
        alpha = jnp.exp2(m - m_new)
        p = jnp.exp2((s - m_new).astype(BF16))
        acc = acc * alpha + jnp.dot(p, vch, preferred_element_type=F32)
        m = m_new
    o = acc * (1.0 / pltpu.roll(acc, HEAD_DIM, 1))
    for j in range(2):
        oa = o[(2 * j) * tq:(2 * j + 1) * tq]
        ob = pltpu.roll(o[(2 * j + 1) * tq:(2 * j + 2) * tq], HEAD_DIM, 1)
        o_ref[:, j * LANES:(j + 1) * LANES] = jnp.where(lo, oa, ob).astype(o_ref.dtype)


def _attn_call(q, k, v, kctx, vctx, *, tq):
    b, t, _ = q.shape
    tk = k.shape[1]
    nq = t // tq
    has_ctx = kctx is not None
    kv_spec = lambda n: pl.BlockSpec((None, n, LANES), lambda bi, g, i: (bi, 0, g))
    in_specs = [pl.BlockSpec((None, tq, 2 * LANES), lambda bi, g, i: (bi, i, g)), kv_spec(tk), kv_spec(tk)]
    args = [q, k, v]
    if has_ctx:
        in_specs += [kv_spec(kctx.shape[1]), kv_spec(kctx.shape[1])]
        args += [kctx, vctx]
    return pl.pallas_call(
        functools.partial(_attn_kernel, has_ctx=has_ctx),
        out_shape=jax.ShapeDtypeStruct((b, t, ATTN_W), BF16),
        grid=(b, N_KV_HEADS, nq),
        in_specs=in_specs,
        out_specs=pl.BlockSpec((None, tq, 2 * LANES), lambda bi, g, i: (bi, i, g)),
        compiler_params=_cparams(("arbitrary", "arbitrary", "arbitrary")),
        name="attn",
    )(*args)


LOG2E = 1.4426950408889634


def _softplus(x):
    return jnp.maximum(x, 0.0) + jnp.log(1.0 + jnp.exp(-jnp.abs(x)))


def _ssd_kernel(*refs, nc, reverse, has_h0, want_hfin):
    refs = list(refs)
    conv = not reverse
    if conv:
        xbc_ref, prev_ref, next_ref, cw_ref, cb_ref, dsk_ref = refs[:6]
        refs = refs[6:]
    else:
        xc_ref = refs.pop(0)
    dt_ref, an_ref, dtb_ref, tri_ref, sel_ref = refs[:5]
    refs = refs[5:]
    h0_ref = refs.pop(0) if has_h0 else None
    hprev_ref = refs.pop(0) if (want_hfin and reverse) else None
    y_ref = refs.pop(0)
    xco_ref = refs.pop(0) if conv else None
    hfin_ref = refs.pop(0) if want_hfin else None
    h_scr = refs.pop(0)

    L = CHUNK
    c = pl.program_id(1)
    cidx = (nc - 1 - c) if reverse else c

    @pl.when(c == 0)
    def _():
        if has_h0:
            h_scr[...] = h0_ref[...]
        else:
            h_scr[...] = jnp.zeros_like(h_scr)

    row = lax.broadcasted_iota(jnp.int32, (L, LANES), 0)
    lane = lax.broadcasted_iota(jnp.int32, (L, LANES), 1)
    lo = lane < SSD_HEAD_DIM
    top = row < SSD_HEAD_DIM

    if conv:
        first = cidx == 0
        last = cidx == nc - 1

        def cols(a, w):
            xm = xbc_ref[:, a:a + w].astype(F32)
            rw = lax.broadcasted_iota(jnp.int32, (L, w), 0)
            p6 = jnp.where(first, 0.0, prev_ref[6:7, a:a + w].astype(F32))
            p7 = jnp.where(first, 0.0, prev_ref[7:8, a:a + w].astype(F32))
            n0 = jnp.where(last, 0.0, next_ref[0:1, a:a + w].astype(F32))
            r1 = jnp.where(rw == 0, p7, pltpu.roll(xm, 1, 0))
            r2 = jnp.where(rw == 0, p6, jnp.where(rw == 1, p7, pltpu.roll(xm, 2, 0)))
            rn = jnp.where(rw == L - 1, n0, pltpu.roll(xm, L - 1, 0))
            y = (r2 * cw_ref[0:1, a:a + w] + r1 * cw_ref[1:2, a:a + w] + xm * cw_ref[2:3, a:a + w]
                 + rn * cw_ref[3:4, a:a + w] + cb_ref[:, a:a + w])
            y = _silu(y).astype(BF16)
            xco_ref[:, a:a + w] = y
            return y
    else:
        def cols(a, w):
            return xc_ref[:, a:a + w]

    causal = tri_ref[...] > 0.0
    dt = _softplus(dt_ref[...] + dtb_ref[...])
    la2 = dt * (an_ref[...] * LOG2E)
    acum2 = jnp.dot(tri_ref[...], la2, preferred_element_type=F32, precision=lax.Precision.HIGHEST)
    dt_t = dt.T
    acum2_t = acum2.T
    tot2_t = jnp.sum(la2.T, axis=1, keepdims=True)
    lg_dt_t = jnp.log2(dt_t)
    r_t = lg_dt_t - acum2_t
    w_t = jnp.exp2(lg_dt_t + tot2_t - acum2_t)
    e_acum_x = jnp.dot(jnp.exp2(acum2).astype(BF16), sel_ref[...], preferred_element_type=F32)
    e_tot_t = jnp.exp2(tot2_t)

    for g in range(SSD_GROUPS):
        bgb = cols(D_INNER + g * D_STATE, D_STATE)
        cgb = cols(D_INNER + SSD_GROUPS * D_STATE + g * D_STATE, D_STATE)
        cbm = _bdot_nt(cgb, bgb)
        h_grp = h_scr[4 * g:4 * g + 4]
        yo_grp = _bdot_nt(cgb, h_grp.reshape(4 * LANES, D_STATE))
        for pr in range(4):
            hp = g * 4 + pr
            ha, hb = 2 * hp, 2 * hp + 1
            xpb = cols(hp * LANES, LANES)
            zero = jnp.zeros_like(xpb)
            xs = jnp.concatenate([jnp.where(lo, xpb, zero), jnp.where(lo, zero, xpb)], axis=0)
            ms = []
            for hh in (ha, hb):
                e = jnp.exp2(acum2[:, hh:hh + 1] + r_t[hh:hh + 1, :])
                ms.append((cbm * jnp.where(causal, e, 0.0)).astype(BF16))
            y = jnp.dot(jnp.concatenate(ms, axis=1), xs, preferred_element_type=F32)
            y = y + yo_grp[:, pr * LANES:(pr + 1) * LANES] * e_acum_x[:, hp * LANES:(hp + 1) * LANES]
            if conv:
                y = y + dsk_ref[:, hp * LANES:(hp + 1) * LANES] * xpb.astype(F32)
            y_ref[:, hp * LANES:(hp + 1) * LANES] = y.astype(y_ref.dtype)
            wsel = jnp.where(top, w_t[ha:ha + 1, :], w_t[hb:hb + 1, :])
            st = jnp.dot((xpb.astype(F32).T * wsel).astype(BF16), bgb, preferred_element_type=F32)
            cd = jnp.where(top, e_tot_t[ha:ha + 1, :], e_tot_t[hb:hb + 1, :])
            h_scr[hp] = h_grp[pr] * cd + st

    if want_hfin:
        @pl.when(c == nc - 1)
        def _():
            if reverse:
                hfin_ref[0] = hprev_ref[...]
                hfin_ref[1] = h_scr[...]
            else:
                hfin_ref[...] = h_scr[...]


def _ssd_sweep(xin, dt, wts, h0, hprev, *, reverse, want_hfin):
    b, t, _ = xin.shape
    nc = t // CHUNK
    has_h0 = h0 is not None
    rb = CHUNK // 8
    nrb = t // 8
    d = 1 if reverse else 0
    cmap = (lambda c: nc - 1 - c) if reverse else (lambda c: c)
    hshape = (SSD_HEADS // 2, 2 * SSD_HEAD_DIM, D_STATE)

    chunk_spec = pl.BlockSpec((None, CHUNK, CONV_CH), lambda bi, c: (bi, cmap(c), 0))
    if reverse:
        in_specs = [chunk_spec]
        args = [xin]
    else:
        in_specs = [
            chunk_spec,
            pl.BlockSpec((None, 8, CONV_CH), lambda bi, c: (bi, jnp.maximum(c * rb - 1, 0), 0)),
            pl.BlockSpec((None, 8, CONV_CH), lambda bi, c: (bi, jnp.minimum((c + 1) * rb, nrb - 1), 0)),
            pl.BlockSpec((D_CONV, CONV_CH), lambda bi, c: (0, 0)),
            pl.BlockSpec((1, CONV_CH), lambda bi, c: (0, 0)),
            pl.BlockSpec((1, D_INNER), lambda bi, c: (0, 0)),
        ]
        args = [xin, xin, xin, wts["conv_w"], wts["conv_b"], wts["dskip"]]
    in_specs += [
        pl.BlockSpec((None, CHUNK, LANES), lambda bi, c: (bi, cmap(c), d)),
        pl.BlockSpec((None, 1, LANES), lambda bi, c: (d, 0, 0)),
        pl.BlockSpec((None, 1, LANES), lambda bi, c: (d, 0, 0)),
        pl.BlockSpec((None, CHUNK, CHUNK), lambda bi, c: (d, 0, 0)),
        pl.BlockSpec((LANES, D_INNER), lambda bi, c: (0, 0)),
    ]
    args += [dt, wts["a_neg"], wts["dt_bias"], wts["tri"], wts["head_sel"]]
    if has_h0:
        in_specs.append(pl.BlockSpec((None, None) + hshape, lambda bi, c: (bi, d, 0, 0, 0)))
        args.append(h0)
    if want_hfin and reverse:
        in_specs.append(pl.BlockSpec((None,) + hshape, lambda bi, c: (bi, 0, 0, 0)))
        args.append(hprev)
    out_shape = [jax.ShapeDtypeStruct((b, t, D_INNER), BF16)]
    out_specs = [pl.BlockSpec((None, CHUNK, D_INNER), lambda bi, c: (bi, cmap(c), 0))]
    if not reverse:
        out_shape.append(jax.ShapeDtypeStruct((b, t, CONV_CH), BF16))
        out_specs.append(pl.BlockSpec((None, CHUNK, CONV_CH), lambda bi, c: (bi, c, 0)))
    if want_hfin and reverse:
        out_shape.append(jax.ShapeDtypeStruct((b, 2) + hshape, F32))
        out_specs.append(pl.BlockSpec((None, 2) + hshape, lambda bi, c: (bi, 0, 0, 0, 0)))
    elif want_hfin:
        out_shape.append(jax.ShapeDtypeStruct((b,) + hshape, F32))
        out_specs.append(pl.BlockSpec((None,) + hshape, lambda bi, c: (bi, 0, 0, 0)))
    return pl.pallas_call(
        functools.partial(_ssd_kernel, nc=nc, reverse=reverse, has_h0=has_h0, want_hfin=want_hfin),
        out_shape=tuple(out_shape),
        grid=(b, nc),
        in_specs=in_specs,
        out_specs=tuple(out_specs),
        scratch_shapes=[pltpu.VMEM(hshape, F32)],
        compiler_params=_cparams(("arbitrary", "arbitrary")),
        name="ssd_bwd" if reverse else "ssd_fwd",
    )(*args)


def _ssd_call(xbc, dt, wts, h0, *, want_hfin):
    res = _ssd_sweep(xbc, dt, wts, h0, None, reverse=False, want_hfin=want_hfin)
    y_f, xc = res[0], res[1]
    hf = res[2] if want_hfin else None
    res = _ssd_sweep(xc, dt, wts, h0, hf, reverse=True, want_hfin=want_hfin)
    return y_f, res[0], (res[1] if want_hfin else None)


def _route(logits_t, bias_col):
    e, n = logits_t.shape
    per = e // N_EXPERT_GROUPS
    scores = jax.nn.sigmoid(logits_t)
    sel = scores + bias_col
    neg = jnp.float32(-jnp.inf)
    gs = []
    for g in range(N_EXPERT_GROUPS):
        blk = sel[g * per:(g + 1) * per, :]
        m1 = jnp.max(blk, axis=0, keepdims=True)
        is_m1 = blk == m1
        cnt = jnp.sum(jnp.where(is_m1, 1.0, 0.0), axis=0, keepdims=True)
        m2 = jnp.max(jnp.where(is_m1, neg, blk), axis=0, keepdims=True)
        gs.append(m1 + jnp.where(cnt >= 2.0, m1, m2))
    keep = []
    for g in range(N_EXPERT_GROUPS):
        rank = jnp.zeros_like(gs[g])
        for j in range(N_EXPERT_GROUPS):
            if j == g:
                continue
            beats = (gs[j] > gs[g]) if j > g else (gs[j] >= gs[g])
            rank = rank + jnp.where(beats, 1.0, 0.0)
        keep.append(rank < float(TOPK_GROUPS))
    selm = jnp.concatenate(
        [jnp.where(keep[g], sel[g * per:(g + 1) * per, :], neg) for g in range(N_EXPERT_GROUPS)], axis=0)
    eidx = lax.broadcasted_iota(jnp.int32, (e, n), 0).astype(F32)
    cur = selm
    picks = []
    for _ in range(TOP_K):
        m = jnp.max(cur, axis=0, keepdims=True)
        idx = jnp.min(jnp.where(cur == m, eidx, float(e)), axis=0, keepdims=True)
        hit = eidx == idx
        picks.append((idx, hit))
        cur = jnp.where(hit, neg, cur)
    return scores, picks


def _pack_bf16_pairs(h):
    c = h.shape[1] // 2
    lo = pltpu.bitcast(h[:, :c].astype(BF16).astype(F32), jnp.uint32)
    hi = pltpu.bitcast(h[:, c:].astype(BF16).astype(F32), jnp.uint32)
    return (lo >> 16) | (hi & jnp.uint32(0xFFFF0000))


def _unpack_bf16_pairs(w):
    lo = pltpu.bitcast(w << 16, F32)
    hi = pltpu.bitcast(w & jnp.uint32(0xFFFF0000), F32)
    return lo, hi


def _rows8(rows):
    n = rows[0].shape[1]
    ridx = lax.broadcasted_iota(jnp.int32, (TOP_K, n), 0)
    out = jnp.zeros((TOP_K, n), rows[0].dtype)
    for k, r in enumerate(rows):
        out = jnp.where(ridx == k, r, out)
    return out


MERGE_SUB = 256


def _merge_kernel(x_ref, attn_ref, yf_ref, yb_ref, z_ref, gates_ref, mod_ref, wa_ref, ws_ref, wo_ref,
                  sg_ref, n2_ref, wr_ref, rb_ref, wsg_ref, wsu_ref, wsd_ref,
                  xb_ref, ha_ref, hb_ref, eid_ref, pos_ref, wk_ref, cnt_ref):
    tm = x_ref.shape[0]
    sub = MERGE_SUB

    @pl.when(pl.program_id(0) == 0)
    def _():
        cnt_ref[...] = jnp.zeros_like(cnt_ref)

    r_i = lax.broadcasted_iota(jnp.int32, (sub, sub), 0)
    c_i = lax.broadcasted_iota(jnp.int32, (sub, sub), 1)
    before = jnp.where(r_i < c_i, 1.0, 0.0).astype(BF16)
    cnt = cnt_ref[:, 0:1]

    for r0 in range(0, tm, sub):
        rs = slice(r0, r0 + sub)
        x = x_ref[rs, :]
        yy = yf_ref[rs, :].astype(F32) + yb_ref[rs, :].astype(F32)
        u = yy * _silu(z_ref[rs, :]).astype(F32)
        un = u * lax.rsqrt(jnp.mean(u * u, axis=-1, keepdims=True) + EPS) * sg_ref[...]
        ssd_o = _bdot(un, ws_ref[...])
        attn_o = jnp.dot(attn_ref[rs, :], wa_ref[...], preferred_element_type=F32)
        ga = jax.nn.sigmoid(gates_ref[rs, 0:D_MODEL]).astype(F32)
        gs = jax.nn.sigmoid(gates_ref[rs, D_MODEL:2 * D_MODEL]).astype(F32)
        mix = _bdot(ga * attn_o + gs * ssd_o, wo_ref[...])
        x1 = x + mod_ref[:, 2048:3072] * mix
        h2 = x1 * lax.rsqrt(jnp.mean(x1 * x1, axis=-1, keepdims=True) + EPS) * n2_ref[...]
        h2 = h2 * (1.0 + mod_ref[:, 4096:5120]) + mod_ref[:, 3072:4096]
        h2b = h2.astype(BF16)
        ha_ref[rs, :] = _pack_bf16_pairs(h2[:, :D_MODEL // 2])
        hb_ref[rs, :] = _pack_bf16_pairs(h2[:, D_MODEL // 2:])

        logits_t = _bdot_nt(wr_ref[...], h2b)
        scores, picks = _route(logits_t, rb_ref[...])
        chosen = jnp.zeros_like(scores)
        for _, hit in picks:
            chosen = chosen + jnp.where(hit, 1.0, 0.0)
        pos = cnt + jnp.dot(chosen.astype(BF16), before, preferred_element_type=F32)
        cnt = cnt + jnp.sum(chosen, axis=1, keepdims=True)
        poss = [jnp.sum(jnp.where(hit, pos, 0.0), axis=0, keepdims=True) for _, hit in picks]
        wks = [jnp.sum(jnp.where(hit, scores, 0.0), axis=0, keepdims=True) for _, hit in picks]
        wsum = wks[0]
        for w in wks[1:]:
            wsum = wsum + w
        eid_ref[:, rs] = _rows8([idx for idx, _ in picks]).astype(jnp.int32)
        pos_ref[:, rs] = _rows8(poss).astype(jnp.int32)
        wk8 = _rows8(wks) / wsum * ROUTED_SCALE
        wk_ref[rs, :] = jnp.concatenate([wk8, jnp.zeros((LANES - TOP_K, sub), F32)], axis=0).T

        hid = _silu(jnp.dot(h2b, wsg_ref[...], preferred_element_type=F32)) * \
            jnp.dot(h2b, wsu_ref[...], preferred_element_type=F32)
        xb_ref[rs, :] = x1 + mod_ref[:, 5120:6144] * _bdot(hid, wsd_ref[...])

    cnt_ref[...] = jnp.broadcast_to(cnt, cnt_ref.shape)


def _merge_call(x, attn, y_f, y_b, z, gates, mod_rows, wa, ws, wo, sg, n2, wr_t, rb, wsg, wsu, wsd, *, tm):
    b, t, _ = x.shape
    n = b * t
    flat = lambda a: a.reshape(n, a.shape[-1])
    tok = lambda width: pl.BlockSpec((tm, width), lambda i: (i, 0))
    const2 = lambda shape: pl.BlockSpec(shape, lambda i: (0, 0), pipeline_mode=pl.Buffered(1))
    k8 = pl.BlockSpec((TOP_K, tm), lambda i: (0, i))
    half = D_MODEL // 4
    return pl.pallas_call(
        _merge_kernel,
        out_shape=(jax.ShapeDtypeStruct((n, D_MODEL), F32),
                   jax.ShapeDtypeStruct((n, half), jnp.uint32),
                   jax.ShapeDtypeStruct((n, half), jnp.uint32),
                   jax.ShapeDtypeStruct((TOP_K, n), jnp.int32),
                   jax.ShapeDtypeStruct((TOP_K, n), jnp.int32),
                   jax.ShapeDtypeStruct((n, LANES), F32),
                   jax.ShapeDtypeStruct((N_EXPERTS, LANES), F32)),
        grid=(n // tm,),
        in_specs=[tok(D_MODEL), tok(ATTN_W), tok(D_INNER), tok(D_INNER), tok(D_INNER), tok(2 * D_MODEL),
                  pl.BlockSpec((None, 1, 6 * D_MODEL), lambda i: ((i * tm) // t, 0, 0)),
                  const2((ATTN_W, D_MODEL)), const2((D_INNER, D_MODEL)), const2((D_MODEL, D_MODEL)),
                  const2((1, D_INNER)), const2((1, D_MODEL)),
                  const2((N_EXPERTS, D_MODEL)), const2((N_EXPERTS, 1)),
                  const2((D_MODEL, D_SHARED)), const2((D_MODEL, D_SHARED)), const2((D_SHARED, D_MODEL))],
        out_specs=(tok(D_MODEL), tok(half), tok(half), k8, k8, tok(LANES),
                   pl.BlockSpec((N_EXPERTS, LANES), lambda i: (0, 0))),
        compiler_params=_cparams(("arbitrary",)),
        name="merge",
    )(flat(x), flat(attn), flat(y_f), flat(y_b), flat(z), flat(gates), mod_rows, wa, ws, wo, sg, n2, wr_t, rb,
      wsg, wsu, wsd)


ROW_TILE = 512
SC_WINDOW = 128


def _slots_kernel(start_ref, eid_ref, pos_ref, slot_ref):
    eid = eid_ref[...]
    slot = pos_ref[...]
    for e in range(N_EXPERTS):
        slot = slot + jnp.where(eid == e, start_ref[e], 0)
    slot_ref[...] = slot


def _slots_call(start, eid, pos):
    n = eid.shape[1]
    bn = 2048 if n % 2048 == 0 else n
    spec = pl.BlockSpec((TOP_K, bn), lambda i, s: (0, i))
    return pl.pallas_call(
        _slots_kernel,
        out_shape=jax.ShapeDtypeStruct((TOP_K, n), jnp.int32),
        grid_spec=pltpu.PrefetchScalarGridSpec(num_scalar_prefetch=1, grid=(n // bn,),
                                               in_specs=[spec, spec], out_specs=spec),
        compiler_params=_cparams(("arbitrary",)),
        name="slots",
    )(start, eid, pos)


def _sc_dispatch(x, slots, p):
    n, d = x.shape
    mesh = plsc.VectorSubcoreMesh(core_axis_name="core", subcore_axis_name="subcore")

    @functools.partial(pl.kernel, out_type=jax.ShapeDtypeStruct((p, d), x.dtype), mesh=mesh)
    def k(x_hbm, s_hbm, o_hbm):
        def body(x_vmem, s_vmem):
            for kk in range(TOP_K):
                pltpu.sync_copy(x_vmem, o_hbm.at[s_vmem.at[kk]])

        pltpu.emit_pipeline(
            body,
            grid=(n // SC_WINDOW,),
            in_specs=[pl.BlockSpec((SC_WINDOW, d), index_map=lambda i: (i, 0)),
                      pl.BlockSpec((TOP_K, SC_WINDOW), index_map=lambda i: (0, i))],
            out_specs=[],
            core_axis_name=("core", "subcore"),
            dimension_semantics=(pltpu.PARALLEL,),
        )(x_hbm, s_hbm)

    return k(x, slots)


def _sc_combine(y, slots):
    kk, n = slots.shape
    d = y.shape[1]
    mesh = plsc.VectorSubcoreMesh(core_axis_name="core", subcore_axis_name="subcore")

    @functools.partial(pl.kernel, out_type=jax.ShapeDtypeStruct((kk * n, d), y.dtype), mesh=mesh)
    def k(y_hbm, s_hbm, o_hbm):
        def body(s_vmem, o_vmem):
            pltpu.sync_copy(y_hbm.at[s_vmem.at[0]], o_vmem)

        pltpu.emit_pipeline(
            body,
            grid=(kk * n // SC_WINDOW,),
            in_specs=[pl.BlockSpec((1, SC_WINDOW), index_map=lambda i: (0, i))],
            out_specs=[pl.BlockSpec((SC_WINDOW, d), index_map=lambda i: (i, 0))],
            core_axis_name=("core", "subcore"),
            dimension_semantics=(pltpu.PARALLEL,),
        )(s_hbm, o_hbm)

    return k(y, slots.reshape(1, kk * n)).reshape(kk, n, d)


FFN_IN_BUFS = 4
FFN_OUT_BUFS = 3


def _ffn_kernel(st_ref, nt_ref, wg_ref, wu_ref, wd_ref, xa_hbm, xb_hbm, ya_hbm, yb_hbm,
                wg_s, wu_s, wd_s, xa_buf, xb_buf, ya_buf, yb_buf, in_sem, out_sem):
    e = pl.program_id(0)
    ne = pl.num_programs(0)
    n = nt_ref[e]
    t = ROW_TILE
    ahead = FFN_IN_BUFS - 1

    def fetch(row, slot):
        r = pl.multiple_of(row, t)
        return (pltpu.make_async_copy(xa_hbm.at[pl.ds(r, t)], xa_buf.at[slot], in_sem.at[0, slot]),
                pltpu.make_async_copy(xb_hbm.at[pl.ds(r, t)], xb_buf.at[slot], in_sem.at[1, slot]))

    def put(row, slot):
        r = pl.multiple_of(row, t)
        return (pltpu.make_async_copy(ya_buf.at[slot], ya_hbm.at[pl.ds(r, t)], out_sem.at[0, slot]),
                pltpu.make_async_copy(yb_buf.at[slot], yb_hbm.at[pl.ds(r, t)], out_sem.at[1, slot]))

    def start_head(expert):
        for k in range(ahead):
            @pl.when(k < nt_ref[expert])
            def _():
                for c in fetch(st_ref[expert] + k * t, k):
                    c.start()

    @pl.when(e == 0)
    def _():
        start_head(0)

    @pl.when(n > 0)
    def _():
        wg_s[...] = wg_ref[...].astype(BF16)
        wu_s[...] = wu_ref[...].astype(BF16)
        wd_s[...] = wd_ref[...].astype(BF16)
        base = st_ref[e]

        def body(i, carry):
            si = lax.rem(i, FFN_IN_BUFS)
            so = lax.rem(i, FFN_OUT_BUFS)
            row = base + i * t
            for c in fetch(row, si):
                c.wait()

            @pl.when(i + ahead < n)
            def _():
                for c in fetch(row + ahead * t, lax.rem(i + ahead, FFN_IN_BUFS)):
                    c.start()

            @pl.when(i >= FFN_OUT_BUFS)
            def _():
                for c in put(row - FFN_OUT_BUFS * t, so):
                    c.wait()

            parts = _unpack_bf16_pairs(xa_buf[si]) + _unpack_bf16_pairs(xb_buf[si])
            x = jnp.concatenate(parts, axis=1).astype(BF16)
            hid = _silu(jnp.dot(x, wg_s[...], preferred_element_type=F32)) * \
                jnp.dot(x, wu_s[...], preferred_element_type=F32)
            y = _bdot(hid, wd_s[...])
            ya_buf[so] = _pack_bf16_pairs(y[:, :D_MODEL // 2])
            yb_buf[so] = _pack_bf16_pairs(y[:, D_MODEL // 2:])
            for c in put(row, so):
                c.start()
            return carry

        lax.fori_loop(0, n, body, 0)

        for k in range(FFN_OUT_BUFS):
            @pl.when(n - 1 - k >= 0)
            def _():
                j = n - 1 - k
                for c in put(base + j * t, lax.rem(j, FFN_OUT_BUFS)):
                    c.wait()

    @pl.when(e + 1 < ne)
    def _():
        start_head(jnp.minimum(e + 1, ne - 1))


def _ffn_call(start, tiles_e, xa, xb, wg, wu, wd):
    p, half = xa.shape
    wspec = lambda s: pl.BlockSpec((None,) + s, lambda e, st, nt: (e, 0, 0))
    hbm = pl.BlockSpec(memory_space=pl.ANY)
    ibuf = pltpu.VMEM((FFN_IN_BUFS, ROW_TILE, half), jnp.uint32)
    obuf = pltpu.VMEM((FFN_OUT_BUFS, ROW_TILE, half), jnp.uint32)
    return pl.pallas_call(
        _ffn_kernel,
        out_shape=(jax.ShapeDtypeStruct((p, half), jnp.uint32), jax.ShapeDtypeStruct((p, half), jnp.uint32)),
        grid_spec=pltpu.PrefetchScalarGridSpec(
            num_scalar_prefetch=2, grid=(N_EXPERTS,),
            in_specs=[wspec((D_MODEL, D_EXPERT)), wspec((D_MODEL, D_EXPERT)), wspec((D_EXPERT, D_MODEL)),
                      hbm, hbm],
            out_specs=(hbm, hbm),
            scratch_shapes=[pltpu.VMEM((D_MODEL, D_EXPERT), BF16), pltpu.VMEM((D_MODEL, D_EXPERT), BF16),
                            pltpu.VMEM((D_EXPERT, D_MODEL), BF16), ibuf, ibuf, obuf, obuf,
                            pltpu.SemaphoreType.DMA((2, FFN_IN_BUFS)),
                            pltpu.SemaphoreType.DMA((2, FFN_OUT_BUFS))]),
        compiler_params=_cparams(("arbitrary",)),
        name="ffn",
    )(start, tiles_e, wg, wu, wd, xa, xb)


def _final_kernel(xb_ref, ya_ref, yb_ref, wk_ref, mod_ref, o_ref):
    q = D_MODEL // 4
    accs = [jnp.zeros((xb_ref.shape[0], q), F32) for _ in range(4)]
    for k in range(TOP_K):
        w = wk_ref[:, k:k + 1]
        parts = _unpack_bf16_pairs(ya_ref[k]) + _unpack_bf16_pairs(yb_ref[k])
        accs = [a + w * p for a, p in zip(accs, parts)]
    for i, a in enumerate(accs):
        o_ref[:, i * q:(i + 1) * q] = xb_ref[:, i * q:(i + 1) * q] + mod_ref[:, 5120 + i * q:5120 + (i + 1) * q] * a


def _final_call(xb, ya, yb, wk, mod_rows, t, *, tm):
    n = xb.shape[0]
    half = ya.shape[2]
    tok = lambda width: pl.BlockSpec((tm, width), lambda i: (i, 0))
    yspec = pl.BlockSpec((TOP_K, tm, half), lambda i: (0, i, 0))
    return pl.pallas_call(
        _final_kernel,
        out_shape=jax.ShapeDtypeStruct((n, D_MODEL), F32),
        grid=(n // tm,),
        in_specs=[tok(D_MODEL), yspec, yspec, tok(LANES),
                  pl.BlockSpec((None, 1, 6 * D_MODEL), lambda i: ((i * tm) // t, 0, 0))],
        out_specs=tok(D_MODEL),
        compiler_params=_cparams(("arbitrary",)),
        name="final",
    )(xb, ya, yb, wk, mod_rows)


def _moe_call(ha, hb, eid, pos, wk, counts, xb, mod_rows, wg, wu, wd, t):
    n = xb.shape[0]
    max_tiles = n * TOP_K // ROW_TILE + N_EXPERTS
    p = max_tiles * ROW_TILE
    cnt = counts[:, 0].astype(jnp.int32)
    tiles_e = (cnt + ROW_TILE - 1) // ROW_TILE
    ends = jnp.cumsum(tiles_e)
    start = (ends - tiles_e) * ROW_TILE
    slots = _slots_call(start.astype(jnp.int32), eid, pos)
    xa = _sc_dispatch(ha, slots, p)
    xbb = _sc_dispatch(hb, slots, p)
    ya, yb = _ffn_call(start.astype(jnp.int32), tiles_e.astype(jnp.int32), xa, xbb, wg, wu, wd)
    ga = _sc_combine(ya, slots)
    gb = _sc_combine(yb, slots)
    return _final_call(xb, ga, gb, wk, mod_rows, t, tm=256)


def _rope_tables(t):
    n_rows = t // GRID_W
    rows = jnp.repeat(jnp.arange(n_rows), GRID_W).astype(F32)
    cols = jnp.tile(jnp.arange(GRID_W), n_rows).astype(F32)
    n_freq = HEAD_DIM // 4
    freqs = ROPE_THETA ** (-jnp.arange(n_freq, dtype=F32) / n_freq)
    ang = jnp.concatenate([rows[:, None] * freqs, cols[:, None] * freqs], axis=-1)
    ang = jnp.repeat(ang, 2, axis=-1)
    ang = jnp.concatenate([ang, ang], axis=-1)
    sign = jnp.where(jnp.arange(LANES) % 2 == 0, -1.0, 1.0).astype(F32)
    return jnp.cos(ang), jnp.sin(ang) * sign


def _dup_heads(a):
    s = a.shape[:-1]
    a4 = a.reshape(s + (N_KV_HEADS, HEAD_DIM))
    return jnp.concatenate([a4, a4], axis=-1).reshape(s + (KVD_W,))


def _prep_w_in(w_in):
    idx = np.cumsum(SPLIT_SIZES)[:-1].tolist()
    q, k, v, z, xbc, dt, gates = jnp.split(w_in, idx, axis=-1)
    pad = jnp.zeros((D_MODEL, LANES - SSD_HEADS), w_in.dtype)
    cols = [q, k, v, gates, z, xbc,
            dt[:, :SSD_HEADS], pad, dt[:, SSD_HEADS:], pad]
    return jnp.concatenate(cols, axis=-1).astype(BF16)


def _pad_heads(a):
    return jnp.pad(a.astype(F32), ((0, 0), (0, LANES - SSD_HEADS)))[:, None, :]


def _trunk(x, mod_rows, wts, rope_tabs, ctx_k, ctx_v, h0, *, tm, tq, want_state):
    b, t, _ = x.shape
    rope = rope_tabs is not None
    if rope:
        cos, sin = rope_tabs
    else:
        cos = sin = jnp.zeros((t, LANES), F32)
    res = _inproj_call(x, mod_rows, wts["g1"], wts["w_in"], wts["qg"], wts["kg"], cos, sin,
                       rope=rope, emit_kv=want_state, tm=tm)
    q, k, v, gates, z, xbc, dt = res[:7]
    kv_raw = res[7:]
    attn = _attn_call(q, k, v, ctx_k, ctx_v, tq=tq)
    y_f, y_b, hfin = _ssd_call(xbc, dt, wts, h0, want_hfin=want_state)
    xb, ha, hb, eid, pos, wk, counts = _merge_call(
        x, attn, y_f, y_b, z, gates, mod_rows, wts["wa"], wts["ws"], wts["wo"], wts["sg"],
        wts["n2"], wts["wr_t"], wts["rb"], wts["wsg"], wts["wsu"], wts["wsd"], tm=MERGE_SUB)
    out = _moe_call(ha, hb, eid, pos, wk, counts, xb, mod_rows, wts["weg"], wts["weu"], wts["wed"], t)
    return out.reshape(b, t, D_MODEL), kv_raw, hfin


def kernel(x_prompt, x_sample, cache_k, cache_v, state_ssm, c, c_ctx, w_mod, b_mod, norm1_g, norm2_g, w_in,
           q_norm_g, k_norm_g, conv_w, conv_b, a_log, dt_bias, d_skip, ssd_norm_g, w_attn_proj, w_ssd_proj,
           w_out, w_router, router_bias, w_exp_gate, w_exp_up, w_exp_down, w_sh_gate, w_sh_up, w_sh_down):
    depth = w_mod.shape[0]
    assert depth == 1, "single trunk layer"
    bp, tp, _ = x_prompt.shape
    bs, ts, _ = x_sample.shape
    l = 0
    cvec = jnp.concatenate([c_ctx[None, :], c, jnp.zeros((8 - 1 - bs, D_MODEL), F32)], axis=0)
    mod = _mod_call(cvec, w_mod.reshape(D_MODEL, 6 * D_MODEL), b_mod.reshape(1, 6 * D_MODEL))
    mod_prompt = jnp.broadcast_to(mod[0:1][:, None, :], (bp, 1, 6 * D_MODEL))
    mod_sample = mod[1:1 + bs][:, None, :]

    lower = np.tril(np.ones((CHUNK, CHUNK), np.float32))
    head_sel = (np.arange(LANES)[:, None] == np.arange(D_INNER)[None, :] // SSD_HEAD_DIM).astype(np.float32)
    wts = dict(
        g1=norm1_g[l][None, :], n2=norm2_g[l][None, :],
        w_in=_prep_w_in(w_in.reshape(D_MODEL, w_in.shape[-1])),
        qg=jnp.tile(q_norm_g[l], 2)[None, :], kg=jnp.tile(k_norm_g[l], 2)[None, :],
        conv_w=conv_w[l], conv_b=conv_b[l][None, :],
        a_neg=_pad_heads(-jnp.exp(a_log[l].astype(F32))), dt_bias=_pad_heads(dt_bias[l]),
        dskip=jnp.repeat(d_skip[l].astype(F32), SSD_HEAD_DIM)[None, :],
        tri=jnp.asarray(np.stack([lower, lower.T])),
        head_sel=jnp.asarray(head_sel, BF16),
        sg=ssd_norm_g[l][None, :],
        wa=w_attn_proj[l].astype(BF16), ws=w_ssd_proj[l].astype(BF16), wo=w_out[l].astype(BF16),
        wr_t=w_router[l].T.astype(BF16), rb=router_bias[l].astype(F32)[:, None],
        wsg=w_sh_gate[l].astype(BF16), wsu=w_sh_up[l].astype(BF16), wsd=w_sh_down[l].astype(BF16),
        weg=w_exp_gate.reshape(w_exp_gate.shape[1:]), weu=w_exp_up.reshape(w_exp_up.shape[1:]),
        wed=w_exp_down.reshape(w_exp_down.shape[1:]),
    )

    y_prompt, (k_p, v_p), hfin = _trunk(x_prompt, mod_prompt, wts, None, None, None, None,
                                        tm=256, tq=256, want_state=True)
    new_k = k_p.reshape(bp, 1, tp, N_KV_HEADS, HEAD_DIM)
    new_v = v_p.reshape(bp, 1, tp, N_KV_HEADS, HEAD_DIM)
    new_state = hfin.reshape(bp, 1, 2, SSD_HEADS, SSD_HEAD_DIM, D_STATE)

    past = cache_k.shape[2]
    ctx_k = _dup_heads(cache_k[:, l].reshape(bs, past, KV_W)).astype(BF16)
    ctx_v = _dup_heads(cache_v[:, l].reshape(bs, past, KV_W)).astype(BF16)
    h0 = state_ssm[:, l].reshape(bs, 2, SSD_HEADS // 2, 2 * SSD_HEAD_DIM, D_STATE)
    y_sample, _, _ = _trunk(x_sample, mod_sample, wts, _rope_tables(ts), ctx_k, ctx_v, h0,
                               tm=256, tq=256, want_state=False)
    return (y_prompt, y_sample, new_k, new_v, new_state)
```

```python
import functools

import numpy as np
import jax
import jax.numpy as jnp
from jax import lax
from jax.experimental import pallas as pl
from jax.experimental.pallas import tpu as pltpu
from jax.experimental.pallas import tpu_sc as plsc

F32 = jnp.float32
BF16 = jnp.bfloat16

D_MODEL = 1024
GRID_W = 64
EPS = 1e-6
N_HEADS = 16
N_KV_HEADS = 4
HEAD_DIM = 64
ATTN_W = N_HEADS * HEAD_DIM
KV_W = N_KV_HEADS * HEAD_DIM
ROPE_THETA = 10000.0
D_INNER = 2048
SSD_HEAD_DIM = 64
SSD_HEADS = 32
SSD_GROUPS = 4
D_STATE = 128
D_CONV = 4
CHUNK = 128
CONV_CH = D_INNER + 2 * SSD_GROUPS * D_STATE
N_EXPERTS = 64
TOP_K = 8
N_EXPERT_GROUPS = 8
TOPK_GROUPS = 4
D_EXPERT = 256
D_SHARED = 256
ROUTED_SCALE = 2.5
SPLIT_SIZES = (ATTN_W, KV_W, KV_W, D_INNER, CONV_CH, 2 * SSD_HEADS, 2 * D_MODEL)

LANES = 128
KVD_W = N_KV_HEADS * LANES
C_Q, C_K, C_V, C_G, C_Z, C_X, C_DT, C_END = 0, 1024, 1280, 1536, 3584, 5632, 8704, 8960
VMEM_LIMIT = 56 * 1024 * 1024
Q_SCALE = HEAD_DIM ** -0.5 * 1.4426950408889634


def _cparams(sem):
    return pltpu.CompilerParams(dimension_semantics=sem, vmem_limit_bytes=VMEM_LIMIT)


def _silu(x):
    return x * jax.nn.sigmoid(x)


def _bdot(a, b):
    return jnp.dot(a.astype(BF16), b.astype(BF16), preferred_element_type=F32)


def _bdot_nt(a, b):
    return lax.dot_general(a.astype(BF16), b.astype(BF16), (((1,), (1,)), ((), ())),
                           preferred_element_type=F32)


def _mod_kernel(c_ref, w_ref, b_ref, o_ref):
    o_ref[...] = _bdot(_silu(c_ref[...]), w_ref[...]) + b_ref[...]


def _mod_call(cvec, w_mod, b_mod):
    n = w_mod.shape[1]
    bn = 1024
    return pl.pallas_call(
        _mod_kernel,
        out_shape=jax.ShapeDtypeStruct((8, n), F32),
        grid=(n // bn,),
        in_specs=[pl.BlockSpec((8, D_MODEL), lambda j: (0, 0)),
                  pl.BlockSpec((D_MODEL, bn), lambda j: (0, j)),
                  pl.BlockSpec((1, bn), lambda j: (0, j))],
        out_specs=pl.BlockSpec((8, bn), lambda j: (0, j)),
        compiler_params=_cparams(("arbitrary",)),
        name="mod",
    )(cvec, w_mod, b_mod)


def _inproj_kernel(*refs, rope, emit_kv):
    if emit_kv:
        (x_ref, mod_ref, g1_ref, w_ref, qg_ref, kg_ref, cos_ref, sin_ref,
         q_ref, k_ref, v_ref, gates_ref, z_ref, xbc_ref, dt_ref, kraw_ref, vraw_ref) = refs
    else:
        (x_ref, mod_ref, g1_ref, w_ref, qg_ref, kg_ref, cos_ref, sin_ref,
         q_ref, k_ref, v_ref, gates_ref, z_ref, xbc_ref, dt_ref) = refs
    tm = x_ref.shape[0]
    x = x_ref[...]
    inv = lax.rsqrt(jnp.mean(x * x, axis=-1, keepdims=True) + EPS)
    h = (x * inv) * g1_ref[...]
    h = h * (1.0 + mod_ref[:, 1024:2048]) + mod_ref[:, 0:1024]
    hb = h.astype(BF16)

    lane = lax.broadcasted_iota(jnp.int32, (tm, LANES), 1)
    lo = lane < HEAD_DIM
    even = (lane & 1) == 0
    if rope:
        cos = cos_ref[...]
        sin = sin_ref[...]

    def rope_fn(blk):
        nxt = pltpu.roll(blk, LANES - 1, 1)
        prv = pltpu.roll(blk, 1, 1)
        return blk * cos + jnp.where(even, nxt, prv) * sin

    def head_norm(blk, g):
        sq = blk * blk
        s_all = jnp.sum(sq, axis=-1, keepdims=True)
        s_lo = jnp.sum(jnp.where(lo, sq, 0.0), axis=-1, keepdims=True)
        ms = jnp.where(lo, s_lo, s_all - s_lo) * (1.0 / HEAD_DIM)
        return blk * lax.rsqrt(ms + EPS) * g

    def dup_heads(blk):
        sw = pltpu.roll(blk, HEAD_DIM, 1)
        return jnp.where(lo, blk, sw), jnp.where(lo, sw, blk)

    qg = qg_ref[...]
    kg = kg_ref[...]
    q = jnp.dot(hb, w_ref[:, C_Q:C_K], preferred_element_type=F32)
    for j in range(ATTN_W // LANES):
        blk = head_norm(q[:, j * LANES:(j + 1) * LANES], qg)
        if rope:
            blk = rope_fn(blk)
        q_ref[:, j * LANES:(j + 1) * LANES] = (blk * Q_SCALE).astype(q_ref.dtype)

    k = jnp.dot(hb, w_ref[:, C_K:C_V], preferred_element_type=F32)
    v = jnp.dot(hb, w_ref[:, C_V:C_G], preferred_element_type=F32)
    for j in range(KV_W // LANES):
        kb = head_norm(k[:, j * LANES:(j + 1) * LANES], kg)
        vb = v[:, j * LANES:(j + 1) * LANES]
        if emit_kv:
            kraw_ref[:, j * LANES:(j + 1) * LANES] = kb
            vraw_ref[:, j * LANES:(j + 1) * LANES] = vb
        if rope:
            kb = rope_fn(kb)
        for i, (kd, vd) in enumerate(zip(dup_heads(kb), dup_heads(vb))):
            c0 = (2 * j + i) * LANES
            k_ref[:, c0:c0 + LANES] = kd.astype(k_ref.dtype)
            v_ref[:, c0:c0 + LANES] = vd.astype(v_ref.dtype)

    gates_ref[...] = jnp.dot(hb, w_ref[:, C_G:C_Z], preferred_element_type=F32).astype(gates_ref.dtype)
    z_ref[...] = jnp.dot(hb, w_ref[:, C_Z:C_X], preferred_element_type=F32).astype(z_ref.dtype)
    xbc_ref[...] = jnp.dot(hb, w_ref[:, C_X:C_DT], preferred_element_type=F32).astype(xbc_ref.dtype)
    dt_ref[...] = jnp.dot(hb, w_ref[:, C_DT:C_END], preferred_element_type=F32)


def _inproj_call(x, mod_rows, g1, w, qg, kg, cos, sin, *, rope, emit_kv, tm):
    b, t, _ = x.shape
    nt = t // tm
    tok = lambda width: pl.BlockSpec((None, tm, width), lambda bi, i: (bi, i, 0))
    const2 = lambda shape: pl.BlockSpec(shape, lambda bi, i: (0, 0))
    out_shape = [
        jax.ShapeDtypeStruct((b, t, ATTN_W), BF16),
        jax.ShapeDtypeStruct((b, t, KVD_W), BF16),
        jax.ShapeDtypeStruct((b, t, KVD_W), BF16),
        jax.ShapeDtypeStruct((b, t, 2 * D_MODEL), BF16),
        jax.ShapeDtypeStruct((b, t, D_INNER), BF16),
        jax.ShapeDtypeStruct((b, t, CONV_CH), BF16),
        jax.ShapeDtypeStruct((b, t, 2 * LANES), F32),
    ]
    out_specs = [tok(ATTN_W), tok(KVD_W), tok(KVD_W), tok(2 * D_MODEL), tok(D_INNER), tok(CONV_CH),
                 tok(2 * LANES)]
    if emit_kv:
        out_shape += [jax.ShapeDtypeStruct((b, t, KV_W), F32)] * 2
        out_specs += [tok(KV_W), tok(KV_W)]
    return pl.pallas_call(
        functools.partial(_inproj_kernel, rope=rope, emit_kv=emit_kv),
        out_shape=tuple(out_shape),
        grid=(b, nt),
        in_specs=[tok(D_MODEL),
                  pl.BlockSpec((None, 1, 6 * D_MODEL), lambda bi, i: (bi, 0, 0)),
                  const2((1, D_MODEL)),
                  pl.BlockSpec((D_MODEL, C_END), lambda bi, i: (0, 0), pipeline_mode=pl.Buffered(1)),
                  const2((1, LANES)), const2((1, LANES)),
                  pl.BlockSpec((tm, LANES), lambda bi, i: (i, 0)),
                  pl.BlockSpec((tm, LANES), lambda bi, i: (i, 0))],
        out_specs=tuple(out_specs),
        compiler_params=_cparams(("arbitrary", "arbitrary")),
        name="inproj",
    )(x, mod_rows, g1, w, qg, kg, cos, sin)


KEY_CHUNK = 512


def _key_chunk(n):
    return KEY_CHUNK if n % KEY_CHUNK == 0 else n


def _attn_kernel(*refs, has_ctx):
    if has_ctx:
        q_ref, k_ref, v_ref, kctx_ref, vctx_ref, o_ref = refs
        sources = ((k_ref, v_ref), (kctx_ref, vctx_ref))
    else:
        q_ref, k_ref, v_ref, o_ref = refs
        sources = ((k_ref, v_ref),)
    tq = q_ref.shape[0]
    lane = lax.broadcasted_iota(jnp.int32, (tq, LANES), 1)
    lo = lane < HEAD_DIM
    qs = []
    for j in range(2):
        q2 = q_ref[:, j * LANES:(j + 1) * LANES]
        zero = jnp.zeros_like(q2)
        qs += [jnp.where(lo, q2, zero), jnp.where(lo, zero, q2)]
    q4 = jnp.concatenate(qs, axis=0)
    rows = 4 * tq
    m = jnp.full((rows, 1), -jnp.inf, F32)
    acc = jnp.zeros((rows, LANES), F32)
    chunks = [(kr, vr, c, _key_chunk(kr.shape[0])) for kr, vr in sources
              for c in range(kr.shape[0] // _key_chunk(kr.shape[0]))]
    for kr, vr, c, kc in chunks:
        kch = kr[c * kc:(c + 1) * kc, :].astype(BF16)
        vch = vr[c * kc:(c + 1) * kc, :].astype(BF16)
        lane_k = lax.broadcasted_iota(jnp.int32, (kc, LANES), 1)
        vch = jnp.where(lane_k < HEAD_DIM, vch, jnp.ones_like(vch))
        s = _bdot_nt(q4, kch)
        m_new = jnp.maximum(m, jnp.max(s, axis=-1, keepdims=True))
        alpha = jnp.exp2(m - m_new)
        p = jnp.exp2((s - m_new).astype(BF16))
        acc = acc * alpha + jnp.dot(p, vch, preferred_element_type=F32)
        m = m_new
    o = acc * (1.0 / pltpu.roll(acc, HEAD_DIM, 1))
    for j in range(2):
        oa = o[(2 * j) * tq:(2 * j + 1) * tq]
        ob = pltpu.roll(o[(2 * j + 1) * tq:(2 * j + 2) * tq], HEAD_DIM, 1)
        o_ref[:, j * LANES:(j + 1) * LANES] = jnp.where(lo, oa, ob).astype(o_ref.dtype)


def _attn_call(q, k, v, kctx, vctx, *, tq):
    b, t, _ = q.shape
    tk = k.shape[1]
    nq = t // tq
    has_ctx = kctx is not None
    kv_spec = lambda n: pl.BlockSpec((None, n, LANES), lambda bi, g, i: (bi, 0, g))
    in_specs = [pl.BlockSpec((None, tq, 2 * LANES), lambda bi, g, i: (bi, i, g)), kv_spec(tk), kv_spec(tk)]
    args = [q, k, v]
    if has_ctx:
        in_specs += [kv_spec(kctx.shape[1]), kv_spec(kctx.shape[1])]
        args += [kctx, vctx]
    return pl.pallas_call(
        functools.partial(_attn_kernel, has_ctx=has_ctx),
        out_shape=jax.ShapeDtypeStruct((b, t, ATTN_W), BF16),
        grid=(b, N_KV_HEADS, nq),
        in_specs=in_specs,
        out_specs=pl.BlockSpec((None, tq, 2 * LANES), lambda bi, g, i: (bi, i, g)),
        compiler_params=_cparams(("arbitrary", "arbitrary", "arbitrary")),
        name="attn",
    )(*args)


LOG2E = 1.4426950408889634


def _softplus(x):
    return jnp.maximum(x, 0.0) + jnp.log(1.0 + jnp.exp(-jnp.abs(x)))


def _ssd_kernel(*refs, nc, reverse, has_h0, want_hfin):
    refs = list(refs)
    conv = not reverse
    if conv:
        xbc_ref, prev_ref, next_ref, cw_ref, cb_ref, dsk_ref = refs[:6]
        refs = refs[6:]
    else:
        xc_ref = refs.pop(0)
    dt_ref, an_ref, dtb_ref, tri_ref, sel_ref = refs[:5]
    refs = refs[5:]
    h0_ref = refs.pop(0) if has_h0 else None
    hprev_ref = refs.pop(0) if (want_hfin and reverse) else None
    y_ref = refs.pop(0)
    xco_ref = refs.pop(0) if conv else None
    hfin_ref = refs.pop(0) if want_hfin else None
    h_scr = refs.pop(0)

    L = CHUNK
    c = pl.program_id(1)
    cidx = (nc - 1 - c) if reverse else c

    @pl.when(c == 0)
    def _():
        if has_h0:
            h_scr[...] = h0_ref[...]
        else:
            h_scr[...] = jnp.zeros_like(h_scr)

    row = lax.broadcasted_iota(jnp.int32, (L, LANES), 0)
    lane = lax.broadcasted_iota(jnp.int32, (L, LANES), 1)
    lo = lane < SSD_HEAD_DIM
    top = row < SSD_HEAD_DIM

    if conv:
        first = cidx == 0
        last = cidx == nc - 1

        def cols(a, w):
            xm = xbc_ref[:, a:a + w].astype(F32)
            rw = lax.broadcasted_iota(jnp.int32, (L, w), 0)
            p6 = jnp.where(first, 0.0, prev_ref[6:7, a:a + w].astype(F32))
            p7 = jnp.where(first, 0.0, prev_ref[7:8, a:a + w].astype(F32))
            n0 = jnp.where(last, 0.0, next_ref[0:1, a:a + w].astype(F32))
            r1 = jnp.where(rw == 0, p7, pltpu.roll(xm, 1, 0))
            r2 = jnp.where(rw == 0, p6, jnp.where(rw == 1, p7, pltpu.roll(xm, 2, 0)))
            rn = jnp.where(rw == L - 1, n0, pltpu.roll(xm, L - 1, 0))
            y = (r2 * cw_ref[0:1, a:a + w] + r1 * cw_ref[1:2, a:a + w] + xm * cw_ref[2:3, a:a + w]
                 + rn * cw_ref[3:4, a:a + w] + cb_ref[:, a:a + w])
            y = _silu(y).astype(BF16)
            xco_ref[:, a:a + w] = y
            return y
    else:
        def cols(a, w):
            return xc_ref[:, a:a + w]

    causal = tri_ref[...] > 0.0
    dt = _softplus(dt_ref[...] + dtb_ref[...])
    la2 = dt * (an_ref[...] * LOG2E)
    acum2 = jnp.dot(tri_ref[...], la2, preferred_element_type=F32, precision=lax.Precision.HIGHEST)
    dt_t = dt.T
    acum2_t = acum2.T
    tot2_t = jnp.sum(la2.T, axis=1, keepdims=True)
    lg_dt_t = jnp.log2(dt_t)
    r_t = lg_dt_t - acum2_t
    w_t = jnp.exp2(lg_dt_t + tot2_t - acum2_t)
    e_acum_x = jnp.dot(jnp.exp2(acum2).astype(BF16), sel_ref[...], preferred_element_type=F32)
    e_tot_t = jnp.exp2(tot2_t)

    for g in range(SSD_GROUPS):
        bgb = cols(D_INNER + g * D_STATE, D_STATE)
        cgb = cols(D_INNER + SSD_GROUPS * D_STATE + g * D_STATE, D_STATE)
        cbm = _bdot_nt(cgb, bgb)
        h_grp = h_scr[4 * g:4 * g + 4]
        yo_grp = _bdot_nt(cgb, h_grp.reshape(4 * LANES, D_STATE))
        for pr in range(4):
            hp = g * 4 + pr
            ha, hb = 2 * hp, 2 * hp + 1
            xpb = cols(hp * LANES, LANES)
            zero = jnp.zeros_like(xpb)
            xs = jnp.concatenate([jnp.where(lo, xpb, zero), jnp.where(lo, zero, xpb)], axis=0)
            ms = []
            for hh in (ha, hb):
                e = jnp.exp2(acum2[:, hh:hh + 1] + r_t[hh:hh + 1, :])
                ms.append((cbm * jnp.where(causal, e, 0.0)).astype(BF16))
            y = jnp.dot(jnp.concatenate(ms, axis=1), xs, preferred_element_type=F32)
            y = y + yo_grp[:, pr * LANES:(pr + 1) * LANES] * e_acum_x[:, hp * LANES:(hp + 1) * LANES]
            if conv:
                y = y + dsk_ref[:, hp * LANES:(hp + 1) * LANES] * xpb.astype(F32)
            y_ref[:, hp * LANES:(hp + 1) * LANES] = y.astype(y_ref.dtype)
            wsel = jnp.where(top, w_t[ha:ha + 1, :], w_t[hb:hb + 1, :])
            st = jnp.dot((xpb.astype(F32).T * wsel).astype(BF16), bgb, preferred_element_type=F32)
            cd = jnp.where(top, e_tot_t[ha:ha + 1, :], e_tot_t[hb:hb + 1, :])
            h_scr[hp] = h_grp[pr] * cd + st

    if want_hfin:
        @pl.when(c == nc - 1)
        def _():
            if reverse:
                hfin_ref[0] = hprev_ref[...]
                hfin_ref[1] = h_scr[...]
            else:
                hfin_ref[...] = h_scr[...]


def _ssd_sweep(xin, dt, wts, h0, hprev, *, reverse, want_hfin):
    b, t, _ = xin.shape
    nc = t // CHUNK
    has_h0 = h0 is not None
    rb = CHUNK // 8
    nrb = t // 8
    d = 1 if reverse else 0
    cmap = (lambda c: nc - 1 - c) if reverse else (lambda c: c)
    hshape = (SSD_HEADS // 2, 2 * SSD_HEAD_DIM, D_STATE)

    chunk_spec = pl.BlockSpec((None, CHUNK, CONV_CH), lambda bi, c: (bi, cmap(c), 0))
    if reverse:
        in_specs = [chunk_spec]
        args = [xin]
    else:
        in_specs = [
            chunk_spec,
            pl.BlockSpec((None, 8, CONV_CH), lambda bi, c: (bi, jnp.maximum(c * rb - 1, 0), 0)),
            pl.BlockSpec((None, 8, CONV_CH), lambda bi, c: (bi, jnp.minimum((c + 1) * rb, nrb - 1), 0)),
            pl.BlockSpec((D_CONV, CONV_CH), lambda bi, c: (0, 0)),
            pl.BlockSpec((1, CONV_CH), lambda bi, c: (0, 0)),
            pl.BlockSpec((1, D_INNER), lambda bi, c: (0, 0)),
        ]
        args = [xin, xin, xin, wts["conv_w"], wts["conv_b"], wts["dskip"]]
    in_specs += [
        pl.BlockSpec((None, CHUNK, LANES), lambda bi, c: (bi, cmap(c), d)),
        pl.BlockSpec((None, 1, LANES), lambda bi, c: (d, 0, 0)),
        pl.BlockSpec((None, 1, LANES), lambda bi, c: (d, 0, 0)),
        pl.BlockSpec((None, CHUNK, CHUNK), lambda bi, c: (d, 0, 0)),
        pl.BlockSpec((LANES, D_INNER), lambda bi, c: (0, 0)),
    ]
    args += [dt, wts["a_neg"], wts["dt_bias"], wts["tri"], wts["head_sel"]]
    if has_h0:
        in_specs.append(pl.BlockSpec((None, None) + hshape, lambda bi, c: (bi, d, 0, 0, 0)))
        args.append(h0)
    if want_hfin and reverse:
        in_specs.append(pl.BlockSpec((None,) + hshape, lambda bi, c: (bi, 0, 0, 0)))
        args.append(hprev)
    out_shape = [jax.ShapeDtypeStruct((b, t, D_INNER), BF16)]
    out_specs = [pl.BlockSpec((None, CHUNK, D_INNER), lambda bi, c: (bi, cmap(c), 0))]
    if not reverse:
        out_shape.append(jax.ShapeDtypeStruct((b, t, CONV_CH), BF16))
        out_specs.append(pl.BlockSpec((None, CHUNK, CONV_CH), lambda bi, c: (bi, c, 0)))
    if want_hfin and reverse:
        out_shape.append(jax.ShapeDtypeStruct((b, 2) + hshape, F32))
        out_specs.append(pl.BlockSpec((None, 2) + hshape, lambda bi, c: (bi, 0, 0, 0, 0)))
    elif want_hfin:
        out_shape.append(jax.ShapeDtypeStruct((b,) + hshape, F32))
        out_specs.append(pl.BlockSpec((None,) + hshape, lambda bi, c: (bi, 0, 0, 0)))
    return pl.pallas_call(
        functools.partial(_ssd_kernel, nc=nc, reverse=reverse, has_h0=has_h0, want_hfin=want_hfin),
        out_shape=tuple(out_shape),
        grid=(b, nc),
        in_specs=in_specs,
        out_specs=tuple(out_specs),
        scratch_shapes=[pltpu.VMEM(hshape, F32)],
        compiler_params=_cparams(("arbitrary", "arbitrary")),
        name="ssd_bwd" if reverse else "ssd_fwd",
    )(*args)


def _ssd_call(xbc, dt, wts, h0, *, want_hfin):
    res = _ssd_sweep(xbc, dt, wts, h0, None, reverse=False, want_hfin=want_hfin)
    y_f, xc = res[0], res[1]
    hf = res[2] if want_hfin else None
    res = _ssd_sweep(xc, dt, wts, h0, hf, reverse=True, want_hfin=want_hfin)
    return y_f, res[0], (res[1] if want_hfin else None)


def _route(logits_t, bias_col):
    e, n = logits_t.shape
    per = e // N_EXPERT_GROUPS
    scores = jax.nn.sigmoid(logits_t)
    sel = scores + bias_col
    neg = jnp.float32(-jnp.inf)
    gs = []
    for g in range(N_EXPERT_GROUPS):
        blk = sel[g * per:(g + 1) * per, :]
        m1 = jnp.max(blk, axis=0, keepdims=True)
        is_m1 = blk == m1
        cnt = jnp.sum(jnp.where(is_m1, 1.0, 0.0), axis=0, keepdims=True)
        m2 = jnp.max(jnp.where(is_m1, neg, blk), axis=0, keepdims=True)
        gs.append(m1 + jnp.where(cnt >= 2.0, m1, m2))
    keep = []
    for g in range(N_EXPERT_GROUPS):
        rank = jnp.zeros_like(gs[g])
        for j in range(N_EXPERT_GROUPS):
            if j == g:
                continue
            beats = (gs[j] > gs[g]) if j > g else (gs[j] >= gs[g])
            rank = rank + jnp.where(beats, 1.0, 0.0)
        keep.append(rank < float(TOPK_GROUPS))
    selm = jnp.concatenate(
        [jnp.where(keep[g], sel[g * per:(g + 1) * per, :], neg) for g in range(N_EXPERT_GROUPS)], axis=0)
    eidx = lax.broadcasted_iota(jnp.int32, (e, n), 0).astype(F32)
    cur = selm
    picks = []
    for _ in range(TOP_K):
        m = jnp.max(cur, axis=0, keepdims=True)
        idx = jnp.min(jnp.where(cur == m, eidx, float(e)), axis=0, keepdims=True)
        hit = eidx == idx
        picks.append((idx, hit))
        cur = jnp.where(hit, neg, cur)
    return scores, picks


def _pack_bf16_pairs(h):
    c = h.shape[1] // 2
    lo = pltpu.bitcast(h[:, :c].astype(BF16).astype(F32), jnp.uint32)
    hi = pltpu.bitcast(h[:, c:].astype(BF16).astype(F32), jnp.uint32)
    return (lo >> 16) | (hi & jnp.uint32(0xFFFF0000))


def _unpack_bf16_pairs(w):
    lo = pltpu.bitcast(w << 16, F32)
    hi = pltpu.bitcast(w & jnp.uint32(0xFFFF0000), F32)
    return lo, hi


def _rows8(rows):
    n = rows[0].shape[1]
    ridx = lax.broadcasted_iota(jnp.int32, (TOP_K, n), 0)
    out = jnp.zeros((TOP_K, n), rows[0].dtype)
    for k, r in enumerate(rows):
        out = jnp.where(ridx == k, r, out)
    return out


MERGE_SUB = 512


def _merge_kernel(x_ref, attn_ref, yf_ref, yb_ref, z_ref, gates_ref, mod_ref, wa_ref, ws_ref, wo_ref,
                  sg_ref, n2_ref, wr_ref, rb_ref, wsg_ref, wsu_ref, wsd_ref,
                  xb_ref, ha_ref, hb_ref, eid_ref, pos_ref, wk_ref, cnt_ref):
    tm = x_ref.shape[0]
    sub = MERGE_SUB

    @pl.when(pl.program_id(0) == 0)
    def _():
        cnt_ref[...] = jnp.zeros_like(cnt_ref)

    r_i = lax.broadcasted_iota(jnp.int32, (sub, sub), 0)
    c_i = lax.broadcasted_iota(jnp.int32, (sub, sub), 1)
    before = jnp.where(r_i < c_i, 1.0, 0.0).astype(BF16)
    cnt = cnt_ref[:, 0:1]

    for r0 in range(0, tm, sub):
        rs = slice(r0, r0 + sub)
        x = x_ref[rs, :]
        yy = yf_ref[rs, :].astype(F32) + yb_ref[rs, :].astype(F32)
        u = yy * _silu(z_ref[rs, :]).astype(F32)
        un = u * lax.rsqrt(jnp.mean(u * u, axis=-1, keepdims=True) + EPS) * sg_ref[...]
        ssd_o = _bdot(un, ws_ref[...])
        attn_o = jnp.dot(attn_ref[rs, :], wa_ref[...], preferred_element_type=F32)
        ga = jax.nn.sigmoid(gates_ref[rs, 0:D_MODEL]).astype(F32)
        gs = jax.nn.sigmoid(gates_ref[rs, D_MODEL:2 * D_MODEL]).astype(F32)
        mix = _bdot(ga * attn_o + gs * ssd_o, wo_ref[...])
        x1 = x + mod_ref[:, 2048:3072] * mix
        h2 = x1 * lax.rsqrt(jnp.mean(x1 * x1, axis=-1, keepdims=True) + EPS) * n2_ref[...]
        h2 = h2 * (1.0 + mod_ref[:, 4096:5120]) + mod_ref[:, 3072:4096]
        h2b = h2.astype(BF16)
        ha_ref[rs, :] = _pack_bf16_pairs(h2[:, :D_MODEL // 2])
        hb_ref[rs, :] = _pack_bf16_pairs(h2[:, D_MODEL // 2:])

        logits_t = _bdot_nt(wr_ref[...], h2b)
        scores, picks = _route(logits_t, rb_ref[...])
        chosen = jnp.zeros_like(scores)
        for _, hit in picks:
            chosen = chosen + jnp.where(hit, 1.0, 0.0)
        pos = cnt + jnp.dot(chosen.astype(BF16), before, preferred_element_type=F32)
        cnt = cnt + jnp.sum(chosen, axis=1, keepdims=True)
        poss = [jnp.sum(jnp.where(hit, pos, 0.0), axis=0, keepdims=True) for _, hit in picks]
        wks = [jnp.sum(jnp.where(hit, scores, 0.0), axis=0, keepdims=True) for _, hit in picks]
        wsum = wks[0]
        for w in wks[1:]:
            wsum = wsum + w
        eid_ref[:, rs] = _rows8([idx for idx, _ in picks]).astype(jnp.int32)
        pos_ref[:, rs] = _rows8(poss).astype(jnp.int32)
        wk8 = _rows8(wks) / wsum * ROUTED_SCALE
        wk_ref[rs, :] = jnp.concatenate([wk8, jnp.zeros((LANES - TOP_K, sub), F32)], axis=0).T

        hid = _silu(jnp.dot(h2b, wsg_ref[...], preferred_element_type=F32)) * \
            jnp.dot(h2b, wsu_ref[...], preferred_element_type=F32)
        xb_ref[rs, :] = x1 + mod_ref[:, 5120:6144] * _bdot(hid, wsd_ref[...])

    cnt_ref[...] = jnp.broadcast_to(cnt, cnt_ref.shape)


def _merge_call(x, attn, y_f, y_b, z, gates, mod_rows, wa, ws, wo, sg, n2, wr_t, rb, wsg, wsu, wsd, *, tm):
    b, t, _ = x.shape
    n = b * t
    flat = lambda a: a.reshape(n, a.shape[-1])
    tok = lambda width: pl.BlockSpec((tm, width), lambda i: (i, 0))
    const2 = lambda shape: pl.BlockSpec(shape, lambda i: (0, 0), pipeline_mode=pl.Buffered(1))
    k8 = pl.BlockSpec((TOP_K, tm), lambda i: (0, i))
    half = D_MODEL // 4
    return pl.pallas_call(
        _merge_kernel,
        out_shape=(jax.ShapeDtypeStruct((n, D_MODEL), F32),
                   jax.ShapeDtypeStruct((n, half), jnp.uint32),
                   jax.ShapeDtypeStruct((n, half), jnp.uint32),
                   jax.ShapeDtypeStruct((TOP_K, n), jnp.int32),
                   jax.ShapeDtypeStruct((TOP_K, n), jnp.int32),
                   jax.ShapeDtypeStruct((n, LANES), F32),
                   jax.ShapeDtypeStruct((N_EXPERTS, LANES), F32)),
        grid=(n // tm,),
        in_specs=[tok(D_MODEL), tok(ATTN_W), tok(D_INNER), tok(D_INNER), tok(D_INNER), tok(2 * D_MODEL),
                  pl.BlockSpec((None, 1, 6 * D_MODEL), lambda i: ((i * tm) // t, 0, 0)),
                  const2((ATTN_W, D_MODEL)), const2((D_INNER, D_MODEL)), const2((D_MODEL, D_MODEL)),
                  const2((1, D_INNER)), const2((1, D_MODEL)),
                  const2((N_EXPERTS, D_MODEL)), const2((N_EXPERTS, 1)),
                  const2((D_MODEL, D_SHARED)), const2((D_MODEL, D_SHARED)), const2((D_SHARED, D_MODEL))],
        out_specs=(tok(D_MODEL), tok(half), tok(half), k8, k8, tok(LANES),
                   pl.BlockSpec((N_EXPERTS, LANES), lambda i: (0, 0))),
        compiler_params=_cparams(("arbitrary",)),
        name="merge",
    )(flat(x), flat(attn), flat(y_f), flat(y_b), flat(z), flat(gates), mod_rows, wa, ws, wo, sg, n2, wr_t, rb,
      wsg, wsu, wsd)


ROW_TILE = 512
SC_WINDOW = 128


def _slots_kernel(start_ref, eid_ref, pos_ref, slot_ref):
    eid = eid_ref[...]
    slot = pos_ref[...]
    for e in range(N_EXPERTS):
        slot = slot + jnp.where(eid == e, start_ref[e], 0)
    slot_ref[...] = slot


def _slots_call(start, eid, pos):
    n = eid.shape[1]
    bn = 2048 if n % 2048 == 0 else n
    spec = pl.BlockSpec((TOP_K, bn), lambda i, s: (0, i))
    return pl.pallas_call(
        _slots_kernel,
        out_shape=jax.ShapeDtypeStruct((TOP_K, n), jnp.int32),
        grid_spec=pltpu.PrefetchScalarGridSpec(num_scalar_prefetch=1, grid=(n // bn,),
                                               in_specs=[spec, spec], out_specs=spec),
        compiler_params=_cparams(("arbitrary",)),
        name="slots",
    )(start, eid, pos)


def _sc_dispatch(x, slots, p):
    n, d = x.shape
    mesh = plsc.VectorSubcoreMesh(core_axis_name="core", subcore_axis_name="subcore")

    @functools.partial(pl.kernel, out_type=jax.ShapeDtypeStruct((p, d), x.dtype), mesh=mesh)
    def k(x_hbm, s_hbm, o_hbm):
        def body(x_vmem, s_vmem):
            for kk in range(TOP_K):
                pltpu.sync_copy(x_vmem, o_hbm.at[s_vmem.at[kk]])

        pltpu.emit_pipeline(
            body,
            grid=(n // SC_WINDOW,),
            in_specs=[pl.BlockSpec((SC_WINDOW, d), index_map=lambda i: (i, 0)),
                      pl.BlockSpec((TOP_K, SC_WINDOW), index_map=lambda i: (0, i))],
            out_specs=[],
            core_axis_name=("core", "subcore"),
            dimension_semantics=(pltpu.PARALLEL,),
        )(x_hbm, s_hbm)

    return k(x, slots)


def _sc_combine(y, slots):
    kk, n = slots.shape
    d = y.shape[1]
    mesh = plsc.VectorSubcoreMesh(core_axis_name="core", subcore_axis_name="subcore")

    @functools.partial(pl.kernel, out_type=jax.ShapeDtypeStruct((kk * n, d), y.dtype), mesh=mesh)
    def k(y_hbm, s_hbm, o_hbm):
        def body(s_vmem, o_vmem):
            pltpu.sync_copy(y_hbm.at[s_vmem.at[0]], o_vmem)

        pltpu.emit_pipeline(
            body,
            grid=(kk * n // SC_WINDOW,),
            in_specs=[pl.BlockSpec((1, SC_WINDOW), index_map=lambda i: (0, i))],
            out_specs=[pl.BlockSpec((SC_WINDOW, d), index_map=lambda i: (i, 0))],
            core_axis_name=("core", "subcore"),
            dimension_semantics=(pltpu.PARALLEL,),
        )(s_hbm, o_hbm)

    return k(y, slots.reshape(1, kk * n)).reshape(kk, n, d)


FFN_IN_BUFS = 4
FFN_OUT_BUFS = 3


def _ffn_kernel(st_ref, nt_ref, wg_ref, wu_ref, wd_ref, xa_hbm, xb_hbm, ya_hbm, yb_hbm,
                wg_s, wu_s, wd_s, xa_buf, xb_buf, ya_buf, yb_buf, in_sem, out_sem):
    e = pl.program_id(0)
    ne = pl.num_programs(0)
    n = nt_ref[e]
    t = ROW_TILE
    ahead = FFN_IN_BUFS - 1

    def fetch(row, slot):
        r = pl.multiple_of(row, t)
        return (pltpu.make_async_copy(xa_hbm.at[pl.ds(r, t)], xa_buf.at[slot], in_sem.at[0, slot]),
                pltpu.make_async_copy(xb_hbm.at[pl.ds(r, t)], xb_buf.at[slot], in_sem.at[1, slot]))

    def put(row, slot):
        r = pl.multiple_of(row, t)
        return (pltpu.make_async_copy(ya_buf.at[slot], ya_hbm.at[pl.ds(r, t)], out_sem.at[0, slot]),
                pltpu.make_async_copy(yb_buf.at[slot], yb_hbm.at[pl.ds(r, t)], out_sem.at[1, slot]))

    def start_head(expert):
        for k in range(ahead):
            @pl.when(k < nt_ref[expert])
            def _():
                for c in fetch(st_ref[expert] + k * t, k):
                    c.start()

    @pl.when(e == 0)
    def _():
        start_head(0)

    @pl.when(n > 0)
    def _():
        wg_s[...] = wg_ref[...].astype(BF16)
        wu_s[...] = wu_ref[...].astype(BF16)
        wd_s[...] = wd_ref[...].astype(BF16)
        base = st_ref[e]

        def body(i, carry):
            si = lax.rem(i, FFN_IN_BUFS)
            so = lax.rem(i, FFN_OUT_BUFS)
            row = base + i * t
            for c in fetch(row, si):
                c.wait()

            @pl.when(i + ahead < n)
            def _():
                for c in fetch(row + ahead * t, lax.rem(i + ahead, FFN_IN_BUFS)):
                    c.start()

            @pl.when(i >= FFN_OUT_BUFS)
            def _():
                for c in put(row - FFN_OUT_BUFS * t, so):
                    c.wait()

            parts = _unpack_bf16_pairs(xa_buf[si]) + _unpack_bf16_pairs(xb_buf[si])
            x = jnp.concatenate(parts, axis=1).astype(BF16)
            hid = _silu(jnp.dot(x, wg_s[...], preferred_element_type=F32)) * \
                jnp.dot(x, wu_s[...], preferred_element_type=F32)
            y = _bdot(hid, wd_s[...])
            ya_buf[so] = _pack_bf16_pairs(y[:, :D_MODEL // 2])
            yb_buf[so] = _pack_bf16_pairs(y[:, D_MODEL // 2:])
            for c in put(row, so):
                c.start()
            return carry

        lax.fori_loop(0, n, body, 0)

        for k in range(FFN_OUT_BUFS):
            @pl.when(n - 1 - k >= 0)
            def _():
                j = n - 1 - k
                for c in put(base + j * t, lax.rem(j, FFN_OUT_BUFS)):
                    c.wait()

    @pl.when(e + 1 < ne)
    def _():
        start_head(jnp.minimum(e + 1, ne - 1))


def _ffn_call(start, tiles_e, xa, xb, wg, wu, wd):
    p, half = xa.shape
    wspec = lambda s: pl.BlockSpec((None,) + s, lambda e, st, nt: (e, 0, 0))
    hbm = pl.BlockSpec(memory_space=pl.ANY)
    ibuf = pltpu.VMEM((FFN_IN_BUFS, ROW_TILE, half), jnp.uint32)
    obuf = pltpu.VMEM((FFN_OUT_BUFS, ROW_TILE, half), jnp.uint32)
    return pl.pallas_call(
        _ffn_kernel,
        out_shape=(jax.ShapeDtypeStruct((p, half), jnp.uint32), jax.ShapeDtypeStruct((p, half), jnp.uint32)),
        grid_spec=pltpu.PrefetchScalarGridSpec(
            num_scalar_prefetch=2, grid=(N_EXPERTS,),
            in_specs=[wspec((D_MODEL, D_EXPERT)), wspec((D_MODEL, D_EXPERT)), wspec((D_EXPERT, D_MODEL)),
                      hbm, hbm],
            out_specs=(hbm, hbm),
            scratch_shapes=[pltpu.VMEM((D_MODEL, D_EXPERT), BF16), pltpu.VMEM((D_MODEL, D_EXPERT), BF16),
                            pltpu.VMEM((D_EXPERT, D_MODEL), BF16), ibuf, ibuf, obuf, obuf,
                            pltpu.SemaphoreType.DMA((2, FFN_IN_BUFS)),
                            pltpu.SemaphoreType.DMA((2, FFN_OUT_BUFS))]),
        compiler_params=_cparams(("arbitrary",)),
        name="ffn",
    )(start, tiles_e, wg, wu, wd, xa, xb)


def _final_kernel(xb_ref, ya_ref, yb_ref, wk_ref, mod_ref, o_ref):
    q = D_MODEL // 4
    accs = [jnp.zeros((xb_ref.shape[0], q), F32) for _ in range(4)]
    for k in range(TOP_K):
        w = wk_ref[:, k:k + 1]
        parts = _unpack_bf16_pairs(ya_ref[k]) + _unpack_bf16_pairs(yb_ref[k])
        accs = [a + w * p for a, p in zip(accs, parts)]
    for i, a in enumerate(accs):
        o_ref[:, i * q:(i + 1) * q] = xb_ref[:, i * q:(i + 1) * q] + mod_ref[:, 5120 + i * q:5120 + (i + 1) * q] * a


def _final_call(xb, ya, yb, wk, mod_rows, t, *, tm):
    n = xb.shape[0]
    half = ya.shape[2]
    tok = lambda width: pl.BlockSpec((tm, width), lambda i: (i, 0))
    yspec = pl.BlockSpec((TOP_K, tm, half), lambda i: (0, i, 0))
    return pl.pallas_call(
        _final_kernel,
        out_shape=jax.ShapeDtypeStruct((n, D_MODEL), F32),
        grid=(n // tm,),
        in_specs=[tok(D_MODEL), yspec, yspec, tok(LANES),
                  pl.BlockSpec((None, 1, 6 * D_MODEL), lambda i: ((i * tm) // t, 0, 0))],
        out_specs=tok(D_MODEL),
        compiler_params=_cparams(("arbitrary",)),
        name="final",
    )(xb, ya, yb, wk, mod_rows)


def _moe_call(ha, hb, eid, pos, wk, counts, xb, mod_rows, wg, wu, wd, t):
    n = xb.shape[0]
    max_tiles = n * TOP_K // ROW_TILE + N_EXPERTS
    p = max_tiles * ROW_TILE
    cnt = counts[:, 0].astype(jnp.int32)
    tiles_e = (cnt + ROW_TILE - 1) // ROW_TILE
    ends = jnp.cumsum(tiles_e)
    start = (ends - tiles_e) * ROW_TILE
    slots = _slots_call(start.astype(jnp.int32), eid, pos)
    xa = _sc_dispatch(ha, slots, p)
    xbb = _sc_dispatch(hb, slots, p)
    ya, yb = _ffn_call(start.astype(jnp.int32), tiles_e.astype(jnp.int32), xa, xbb, wg, wu, wd)
    ga = _sc_combine(ya, slots)
    gb = _sc_combine(yb, slots)
    return _final_call(xb, ga, gb, wk, mod_rows, t, tm=256)


def _rope_tables(t):
    n_rows = t // GRID_W
    rows = jnp.repeat(jnp.arange(n_rows), GRID_W).astype(F32)
    cols = jnp.tile(jnp.arange(GRID_W), n_rows).astype(F32)
    n_freq = HEAD_DIM // 4
    freqs = ROPE_THETA ** (-jnp.arange(n_freq, dtype=F32) / n_freq)
    ang = jnp.concatenate([rows[:, None] * freqs, cols[:, None] * freqs], axis=-1)
    ang = jnp.repeat(ang, 2, axis=-1)
    ang = jnp.concatenate([ang, ang], axis=-1)
    sign = jnp.where(jnp.arange(LANES) % 2 == 0, -1.0, 1.0).astype(F32)
    return jnp.cos(ang), jnp.sin(ang) * sign


def _dup_heads(a):
    s = a.shape[:-1]
    a4 = a.reshape(s + (N_KV_HEADS, HEAD_DIM))
    return jnp.concatenate([a4, a4], axis=-1).reshape(s + (KVD_W,))


def _prep_w_in(w_in):
    idx = np.cumsum(SPLIT_SIZES)[:-1].tolist()
    q, k, v, z, xbc, dt, gates = jnp.split(w_in, idx, axis=-1)
    pad = jnp.zeros((D_MODEL, LANES - SSD_HEADS), w_in.dtype)
    cols = [q, k, v, gates, z, xbc,
            dt[:, :SSD_HEADS], pad, dt[:, SSD_HEADS:], pad]
    return jnp.concatenate(cols, axis=-1).astype(BF16)


def _pad_heads(a):
    return jnp.pad(a.astype(F32), ((0, 0), (0, LANES - SSD_HEADS)))[:, None, :]


def _trunk(x, mod_rows, wts, rope_tabs, ctx_k, ctx_v, h0, *, tm, tq, want_state):
    b, t, _ = x.shape
    rope = rope_tabs is not None
    if rope:
        cos, sin = rope_tabs
    else:
        cos = sin = jnp.zeros((t, LANES), F32)
    res = _inproj_call(x, mod_rows, wts["g1"], wts["w_in"], wts["qg"], wts["kg"], cos, sin,
                       rope=rope, emit_kv=want_state, tm=tm)
    q, k, v, gates, z, xbc, dt = res[:7]
    kv_raw = res[7:]
    attn = _attn_call(q, k, v, ctx_k, ctx_v, tq=tq)
    y_f, y_b, hfin = _ssd_call(xbc, dt, wts, h0, want_hfin=want_state)
    xb, ha, hb, eid, pos, wk, counts = _merge_call(
        x, attn, y_f, y_b, z, gates, mod_rows, wts["wa"], wts["ws"], wts["wo"], wts["sg"],
        wts["n2"], wts["wr_t"], wts["rb"], wts["wsg"], wts["wsu"], wts["wsd"], tm=MERGE_SUB)
    out = _moe_call(ha, hb, eid, pos, wk, counts, xb, mod_rows, wts["weg"], wts["weu"], wts["wed"], t)
    return out.reshape(b, t, D_MODEL), kv_raw, hfin


def kernel(x_prompt, x_sample, cache_k, cache_v, state_ssm, c, c_ctx, w_mod, b_mod, norm1_g, norm2_g, w_in,
           q_norm_g, k_norm_g, conv_w, conv_b, a_log, dt_bias, d_skip, ssd_norm_g, w_attn_proj, w_ssd_proj,
           w_out, w_router, router_bias, w_exp_gate, w_exp_up, w_exp_down, w_sh_gate, w_sh_up, w_sh_down):
    depth = w_mod.shape[0]
    assert depth == 1, "single trunk layer"
    bp, tp, _ = x_prompt.shape
    bs, ts, _ = x_sample.shape
    l = 0
    cvec = jnp.concatenate([c_ctx[None, :], c, jnp.zeros((8 - 1 - bs, D_MODEL), F32)], axis=0)
    mod = _mod_call(cvec, w_mod.reshape(D_MODEL, 6 * D_MODEL), b_mod.reshape(1, 6 * D_MODEL))
    mod_prompt = jnp.broadcast_to(mod[0:1][:, None, :], (bp, 1, 6 * D_MODEL))
    mod_sample = mod[1:1 + bs][:, None, :]

    lower = np.tril(np.ones((CHUNK, CHUNK), np.float32))
    head_sel = (np.arange(LANES)[:, None] == np.arange(D_INNER)[None, :] // SSD_HEAD_DIM).astype(np.float32)
    wts = dict(
        g1=norm1_g[l][None, :], n2=norm2_g[l][None, :],
        w_in=_prep_w_in(w_in.reshape(D_MODEL, w_in.shape[-1])),
        qg=jnp.tile(q_norm_g[l], 2)[None, :], kg=jnp.tile(k_norm_g[l], 2)[None, :],
        conv_w=conv_w[l], conv_b=conv_b[l][None, :],
        a_neg=_pad_heads(-jnp.exp(a_log[l].astype(F32))), dt_bias=_pad_heads(dt_bias[l]),
        dskip=jnp.repeat(d_skip[l].astype(F32), SSD_HEAD_DIM)[None, :],
        tri=jnp.asarray(np.stack([lower, lower.T])),
        head_sel=jnp.asarray(head_sel, BF16),
        sg=ssd_norm_g[l][None, :],
        wa=w_attn_proj[l].astype(BF16), ws=w_ssd_proj[l].astype(BF16), wo=w_out[l].astype(BF16),
        wr_t=w_router[l].T.astype(BF16), rb=router_bias[l].astype(F32)[:, None],
        wsg=w_sh_gate[l].astype(BF16), wsu=w_sh_up[l].astype(BF16), wsd=w_sh_down[l].astype(BF16),
        weg=w_exp_gate.reshape(w_exp_gate.shape[1:]), weu=w_exp_up.reshape(w_exp_up.shape[1:]),
        wed=w_exp_down.reshape(w_exp_down.shape[1:]),
    )

    y_prompt, (k_p, v_p), hfin = _trunk(x_prompt, mod_prompt, wts, None, None, None, None,
                                        tm=256, tq=256, want_state=True)
    new_k = k_p.reshape(bp, 1, tp, N_KV_HEADS, HEAD_DIM)
    new_v = v_p.reshape(bp, 1, tp, N_KV_HEADS, HEAD_DIM)
    new_state = hfin.reshape(bp, 1, 2, SSD_HEADS, SSD_HEAD_DIM, D_STATE)

    past = cache_k.shape[2]
    ctx_k = _dup_heads(cache_k[:, l].reshape(bs, past, KV_W)).astype(BF16)
    ctx_v = _dup_heads(cache_v[:, l].reshape(bs, past, KV_W)).astype(BF16)
    h0 = state_ssm[:, l].reshape(bs, 2, SSD_HEADS // 2, 2 * SSD_HEAD_DIM, D_STATE)
    y_sample, _, _ = _trunk(x_sample, mod_sample, wts, _rope_tables(ts), ctx_k, ctx_v, h0,
                               tm=512, tq=256, want_state=False)
    return (y_prompt, y_sample, new_k, new_v, new_state)
```

```python
import functools

import numpy as np
import jax
import jax.numpy as jnp
from jax import lax
from jax.experimental import pallas as pl
from jax.experimental.pallas import tpu as pltpu
from jax.experimental.pallas import tpu_sc as plsc

F32 = jnp.float32
BF16 = jnp.bfloat16

D_MODEL = 1024
GRID_W = 64
EPS = 1e-6
N_HEADS = 16
N_KV_HEADS = 4
HEAD_DIM = 64
ATTN_W = N_HEADS * HEAD_DIM
KV_W = N_KV_HEADS * HEAD_DIM
ROPE_THETA = 10000.0
D_INNER = 2048
SSD_HEAD_DIM = 64
SSD_HEADS = 32
SSD_GROUPS = 4
D_STATE = 128
D_CONV = 4
CHUNK = 128
CONV_CH = D_INNER + 2 * SSD_GROUPS * D_STATE
N_EXPERTS = 64
TOP_K = 8
N_EXPERT_GROUPS = 8
TOPK_GROUPS = 4
D_EXPERT = 256
D_SHARED = 256
ROUTED_SCALE = 2.5
SPLIT_SIZES = (ATTN_W, KV_W, KV_W, D_INNER, CONV_CH, 2 * SSD_HEADS, 2 * D_MODEL)

LANES = 128
KVD_W = N_KV_HEADS * LANES
C_Q, C_K, C_V, C_G, C_Z, C_X, C_DT, C_END = (
    int(c) for c in np.cumsum((0, ATTN_W, KV_W, KV_W, 2 * D_MODEL, D_INNER, CONV_CH, 2 * LANES)))
MOD_SHIFT1, MOD_SCALE1, MOD_GATE1, MOD_SHIFT2, MOD_SCALE2, MOD_GATE2 = range(6)
VMEM_LIMIT = 56 * 1024 * 1024
Q_SCALE = HEAD_DIM ** -0.5 * 1.4426950408889634


def _cparams(sem):
    return pltpu.CompilerParams(dimension_semantics=sem, vmem_limit_bytes=VMEM_LIMIT)


def _mod(mod_ref, which, lo=0, hi=D_MODEL):
    return mod_ref[:, which * D_MODEL + lo:which * D_MODEL + hi]


def _silu(x):
    return x * jax.nn.sigmoid(x)


def _bdot(a, b):
    return jnp.dot(a.astype(BF16), b.astype(BF16), preferred_element_type=F32)


def _bdot_nt(a, b):
    return lax.dot_general(a.astype(BF16), b.astype(BF16), (((1,), (1,)), ((), ())),
                           preferred_element_type=F32)


def _mod_kernel(c_ref, w_ref, b_ref, o_ref):
    o_ref[...] = _bdot(_silu(c_ref[...]), w_ref[...]) + b_ref[...]


def _mod_call(cvec, w_mod, b_mod):
    n = w_mod.shape[1]
    bn = 1024
    return pl.pallas_call(
        _mod_kernel,
        out_shape=jax.ShapeDtypeStruct((8, n), F32),
        grid=(n // bn,),
        in_specs=[pl.BlockSpec((8, D_MODEL), lambda j: (0, 0)),
                  pl.BlockSpec((D_MODEL, bn), lambda j: (0, j)),
                  pl.BlockSpec((1, bn), lambda j: (0, j))],
        out_specs=pl.BlockSpec((8, bn), lambda j: (0, j)),
        compiler_params=_cparams(("arbitrary",)),
        name="mod",
    )(cvec, w_mod, b_mod)


def _inproj_kernel(*refs, rope, emit_kv):
    if emit_kv:
        (x_ref, mod_ref, g1_ref, w_ref, qg_ref, kg_ref, cos_ref, sin_ref,
         q_ref, k_ref, v_ref, gates_ref, z_ref, xbc_ref, dt_ref, kraw_ref, vraw_ref) = refs
    else:
        (x_ref, mod_ref, g1_ref, w_ref, qg_ref, kg_ref, cos_ref, sin_ref,
         q_ref, k_ref, v_ref, gates_ref, z_ref, xbc_ref, dt_ref) = refs
    tm = x_ref.shape[0]
    x = x_ref[...]
    inv = lax.rsqrt(jnp.mean(x * x, axis=-1, keepdims=True) + EPS)
    h = (x * inv) * g1_ref[...]
    h = h * (1.0 + _mod(mod_ref, MOD_SCALE1)) + _mod(mod_ref, MOD_SHIFT1)
    hb = h.astype(BF16)

    lane = lax.broadcasted_iota(jnp.int32, (tm, LANES), 1)
    lo = lane < HEAD_DIM
    even = (lane & 1) == 0
    if rope:
        cos = cos_ref[...]
        sin = sin_ref[...]

    def rope_fn(blk):
        nxt = pltpu.roll(blk, LANES - 1, 1)
        prv = pltpu.roll(blk, 1, 1)
        return blk * cos + jnp.where(even, nxt, prv) * sin

    def head_norm(blk, g):
        sq = blk * blk
        s_all = jnp.sum(sq, axis=-1, keepdims=True)
        s_lo = jnp.sum(jnp.where(lo, sq, 0.0), axis=-1, keepdims=True)
        ms = jnp.where(lo, s_lo, s_all - s_lo) * (1.0 / HEAD_DIM)
        return blk * lax.rsqrt(ms + EPS) * g

    def dup_heads(blk):
        sw = pltpu.roll(blk, HEAD_DIM, 1)
        return jnp.where(lo, blk, sw), jnp.where(lo, sw, blk)

    qg = qg_ref[...]
    kg = kg_ref[...]
    q = jnp.dot(hb, w_ref[:, C_Q:C_K], preferred_element_type=F32)
    for j in range(ATTN_W // LANES):
        blk = head_norm(q[:, j * LANES:(j + 1) * LANES], qg)
        if rope:
            blk = rope_fn(blk)
        q_ref[:, j * LANES:(j + 1) * LANES] = (blk * Q_SCALE).astype(q_ref.dtype)

    k = jnp.dot(hb, w_ref[:, C_K:C_V], preferred_element_type=F32)
    v = jnp.dot(hb, w_ref[:, C_V:C_G], preferred_element_type=F32)
    for j in range(KV_W // LANES):
        kb = head_norm(k[:, j * LANES:(j + 1) * LANES], kg)
        vb = v[:, j * LANES:(j + 1) * LANES]
        if emit_kv:
            kraw_ref[:, j * LANES:(j + 1) * LANES] = kb
            vraw_ref[:, j * LANES:(j + 1) * LANES] = vb
        if rope:
            kb = rope_fn(kb)
        for i, (kd, vd) in enumerate(zip(dup_heads(kb), dup_heads(vb))):
            c0 = (2 * j + i) * LANES
            k_ref[:, c0:c0 + LANES] = kd.astype(k_ref.dtype)
            v_ref[:, c0:c0 + LANES] = vd.astype(v_ref.dtype)

    gates_ref[...] = jnp.dot(hb, w_ref[:, C_G:C_Z], preferred_element_type=F32).astype(gates_ref.dtype)
    z_ref[...] = jnp.dot(hb, w_ref[:, C_Z:C_X], preferred_element_type=F32).astype(z_ref.dtype)
    xbc_ref[...] = jnp.dot(hb, w_ref[:, C_X:C_DT], preferred_element_type=F32).astype(xbc_ref.dtype)
    dt_ref[...] = jnp.dot(hb, w_ref[:, C_DT:C_END], preferred_element_type=F32)


def _inproj_call(x, mod_rows, g1, w, qg, kg, cos, sin, *, rope, emit_kv, tm):
    b, t, _ = x.shape
    nt = t // tm
    tok = lambda width: pl.BlockSpec((None, tm, width), lambda bi, i: (bi, i, 0))
    const2 = lambda shape: pl.BlockSpec(shape, lambda bi, i: (0, 0))
    out_shape = [
        jax.ShapeDtypeStruct((b, t, ATTN_W), BF16),
        jax.ShapeDtypeStruct((b, t, KVD_W), BF16),
        jax.ShapeDtypeStruct((b, t, KVD_W), BF16),
        jax.ShapeDtypeStruct((b, t, 2 * D_MODEL), BF16),
        jax.ShapeDtypeStruct((b, t, D_INNER), BF16),
        jax.ShapeDtypeStruct((b, t, CONV_CH), BF16),
        jax.ShapeDtypeStruct((b, t, 2 * LANES), F32),
    ]
    out_specs = [tok(ATTN_W), tok(KVD_W), tok(KVD_W), tok(2 * D_MODEL), tok(D_INNER), tok(CONV_CH),
                 tok(2 * LANES)]
    if emit_kv:
        out_shape += [jax.ShapeDtypeStruct((b, t, KV_W), F32)] * 2
        out_specs += [tok(KV_W), tok(KV_W)]
    return pl.pallas_call(
        functools.partial(_inproj_kernel, rope=rope, emit_kv=emit_kv),
        out_shape=tuple(out_shape),
        grid=(b, nt),
        in_specs=[tok(D_MODEL),
                  pl.BlockSpec((None, 1, 6 * D_MODEL), lambda bi, i: (bi, 0, 0)),
                  const2((1, D_MODEL)),
                  pl.BlockSpec((D_MODEL, C_END), lambda bi, i: (0, 0), pipeline_mode=pl.Buffered(1)),
                  const2((1, LANES)), const2((1, LANES)),
                  pl.BlockSpec((tm, LANES), lambda bi, i: (i, 0)),
                  pl.BlockSpec((tm, LANES), lambda bi, i: (i, 0))],
        out_specs=tuple(out_specs),
        compiler_params=_cparams(("arbitrary", "arbitrary")),
        name="inproj",
    )(x, mod_rows, g1, w, qg, kg, cos, sin)


KEY_CHUNK = 512


def _key_chunk(n):
    return KEY_CHUNK if n % KEY_CHUNK == 0 else n


def _attn_kernel(*refs, has_ctx):
    if has_ctx:
        q_ref, k_ref, v_ref, kctx_ref, vctx_ref, o_ref = refs
        sources = ((k_ref, v_ref), (kctx_ref, vctx_ref))
    else:
        q_ref, k_ref, v_ref, o_ref = refs
        sources = ((k_ref, v_ref),)
    tq = q_ref.shape[0]
    lane = lax.broadcasted_iota(jnp.int32, (tq, LANES), 1)
    lo = lane < HEAD_DIM
    qs = []
    for j in range(2):
        q2 = q_ref[:, j * LANES:(j + 1) * LANES]
        zero = jnp.zeros_like(q2)
        qs += [jnp.where(lo, q2, zero), jnp.where(lo, zero, q2)]
    q4 = jnp.concatenate(qs, axis=0)
    rows = 4 * tq
    m = jnp.full((rows, 1), -jnp.inf, F32)
    acc = jnp.zeros((rows, LANES), F32)
    chunks = [(kr, vr, c, _key_chunk(kr.shape[0])) for kr, vr in sources
              for c in range(kr.shape[0] // _key_chunk(kr.shape[0]))]
    for kr, vr, c, kc in chunks:
        kch = kr[c * kc:(c + 1) * kc, :].astype(BF16)
        vch = vr[c * kc:(c + 1) * kc, :].astype(BF16)
        lane_k = lax.broadcasted_iota(jnp.int32, (kc, LANES), 1)
        vch = jnp.where(lane_k < HEAD_DIM, vch, jnp.ones_like(vch))
        s = _bdot_nt(q4, kch)
        m_new = jnp.maximum(m, jnp.max(s, axis=-1, keepdims=True))
        alpha = jnp.exp2(m - m_new)
        p = jnp.exp2((s - m_new).astype(BF16))
        acc = acc * alpha + jnp.dot(p, vch, preferred_element_type=F32)
        m = m_new
    o = acc * (1.0 / pltpu.roll(acc, HEAD_DIM, 1))
    for j in range(2):
        oa = o[(2 * j) * tq:(2 * j + 1) * tq]
        ob = pltpu.roll(o[(2 * j + 1) * tq:(2 * j + 2) * tq], HEAD_DIM, 1)
        o_ref[:, j * LANES:(j + 1) * LANES] = jnp.where(lo, oa, ob).astype(o_ref.dtype)


def _attn_call(q, k, v, kctx, vctx, *, tq):
    b, t, _ = q.shape
    tk = k.shape[1]
    nq = t // tq
    has_ctx = kctx is not None
    kv_spec = lambda n: pl.BlockSpec((None, n, LANES), lambda bi, g, i: (bi, 0, g))
    in_specs = [pl.BlockSpec((None, tq, 2 * LANES), lambda bi, g, i: (bi, i, g)), kv_spec(tk), kv_spec(tk)]
    args = [q, k, v]
    if has_ctx:
        in_specs += [kv_spec(kctx.shape[1]), kv_spec(kctx.shape[1])]
        args += [kctx, vctx]
    return pl.pallas_call(
        functools.partial(_attn_kernel, has_ctx=has_ctx),
        out_shape=jax.ShapeDtypeStruct((b, t, ATTN_W), BF16),
        grid=(b, N_KV_HEADS, nq),
        in_specs=in_specs,
        out_specs=pl.BlockSpec((None, tq, 2 * LANES), lambda bi, g, i: (bi, i, g)),
        compiler_params=_cparams(("arbitrary", "arbitrary", "arbitrary")),
        name="attn",
    )(*args)


LOG2E = 1.4426950408889634


def _softplus(x):
    return jnp.maximum(x, 0.0) + jnp.log(1.0 + jnp.exp(-jnp.abs(x)))


def _ssd_kernel(*refs, nc, reverse, has_h0, want_hfin):
    refs = list(refs)
    conv = not reverse
    if conv:
        xbc_ref, prev_ref, next_ref, cw_ref, cb_ref, dsk_ref = refs[:6]
        refs = refs[6:]
    else:
        xc_ref = refs.pop(0)
    dt_ref, an_ref, dtb_ref, tri_ref, sel_ref = refs[:5]
    refs = refs[5:]
    h0_ref = refs.pop(0) if has_h0 else None
    hprev_ref = refs.pop(0) if (want_hfin and reverse) else None
    y_ref = refs.pop(0)
    xco_ref = refs.pop(0) if conv else None
    hfin_ref = refs.pop(0) if want_hfin else None
    h_scr = refs.pop(0)

    L = CHUNK
    c = pl.program_id(1)
    cidx = (nc - 1 - c) if reverse else c

    @pl.when(c == 0)
    def _():
        if has_h0:
            h_scr[...] = h0_ref[...]
        else:
            h_scr[...] = jnp.zeros_like(h_scr)

    row = lax.broadcasted_iota(jnp.int32, (L, LANES), 0)
    lane = lax.broadcasted_iota(jnp.int32, (L, LANES), 1)
    lo = lane < SSD_HEAD_DIM
    top = row < SSD_HEAD_DIM

    if conv:
        first = cidx == 0
        last = cidx == nc - 1

        def cols(a, w):
            xm = xbc_ref[:, a:a + w].astype(F32)
            rw = lax.broadcasted_iota(jnp.int32, (L, w), 0)
            p6 = jnp.where(first, 0.0, prev_ref[6:7, a:a + w].astype(F32))
            p7 = jnp.where(first, 0.0, prev_ref[7:8, a:a + w].astype(F32))
            n0 = jnp.where(last, 0.0, next_ref[0:1, a:a + w].astype(F32))
            r1 = jnp.where(rw == 0, p7, pltpu.roll(xm, 1, 0))
            r2 = jnp.where(rw == 0, p6, jnp.where(rw == 1, p7, pltpu.roll(xm, 2, 0)))
            rn = jnp.where(rw == L - 1, n0, pltpu.roll(xm, L - 1, 0))
            y = (r2 * cw_ref[0:1, a:a + w] + r1 * cw_ref[1:2, a:a + w] + xm * cw_ref[2:3, a:a + w]
                 + rn * cw_ref[3:4, a:a + w] + cb_ref[:, a:a + w])
            y = _silu(y).astype(BF16)
            xco_ref[:, a:a + w] = y
            return y
    else:
        def cols(a, w):
            return xc_ref[:, a:a + w]

    causal = tri_ref[...] > 0.0
    dt = _softplus(dt_ref[...] + dtb_ref[...])
    la2 = dt * (an_ref[...] * LOG2E)
    acum2 = jnp.dot(tri_ref[...], la2, preferred_element_type=F32, precision=lax.Precision.HIGHEST)
    dt_t = dt.T
    acum2_t = acum2.T
    tot2_t = jnp.sum(la2.T, axis=1, keepdims=True)
    lg_dt_t = jnp.log2(dt_t)
    r_t = lg_dt_t - acum2_t
    w_t = jnp.exp2(lg_dt_t + tot2_t - acum2_t)
    e_acum_x = jnp.dot(jnp.exp2(acum2).astype(BF16), sel_ref[...], preferred_element_type=F32)
    e_tot_t = jnp.exp2(tot2_t)

    for g in range(SSD_GROUPS):
        bgb = cols(D_INNER + g * D_STATE, D_STATE)
        cgb = cols(D_INNER + SSD_GROUPS * D_STATE + g * D_STATE, D_STATE)
        cbm = _bdot_nt(cgb, bgb)
        h_grp = h_scr[4 * g:4 * g + 4]
        yo_grp = _bdot_nt(cgb, h_grp.reshape(4 * LANES, D_STATE))
        for pr in range(4):
            hp = g * 4 + pr
            ha, hb = 2 * hp, 2 * hp + 1
            xpb = cols(hp * LANES, LANES)
            zero = jnp.zeros_like(xpb)
            xs = jnp.concatenate([jnp.where(lo, xpb, zero), jnp.where(lo, zero, xpb)], axis=0)
            ms = []
            for hh in (ha, hb):
                e = jnp.exp2(acum2[:, hh:hh + 1] + r_t[hh:hh + 1, :])
                ms.append((cbm * jnp.where(causal, e, 0.0)).astype(BF16))
            y = jnp.dot(jnp.concatenate(ms, axis=1), xs, preferred_element_type=F32)
            y = y + yo_grp[:, pr * LANES:(pr + 1) * LANES] * e_acum_x[:, hp * LANES:(hp + 1) * LANES]
            if conv:
                y = y + dsk_ref[:, hp * LANES:(hp + 1) * LANES] * xpb.astype(F32)
            y_ref[:, hp * LANES:(hp + 1) * LANES] = y.astype(y_ref.dtype)
            wsel = jnp.where(top, w_t[ha:ha + 1, :], w_t[hb:hb + 1, :])
            st = jnp.dot((xpb.astype(F32).T * wsel).astype(BF16), bgb, preferred_element_type=F32)
            cd = jnp.where(top, e_tot_t[ha:ha + 1, :], e_tot_t[hb:hb + 1, :])
            h_scr[hp] = h_grp[pr] * cd + st

    if want_hfin:
        @pl.when(c == nc - 1)
        def _():
            if reverse:
                hfin_ref[0] = hprev_ref[...]
                hfin_ref[1] = h_scr[...]
            else:
                hfin_ref[...] = h_scr[...]


def _ssd_sweep(xin, dt, wts, h0, hprev, *, reverse, want_hfin):
    b, t, _ = xin.shape
    nc = t // CHUNK
    has_h0 = h0 is not None
    rb = CHUNK // 8
    nrb = t // 8
    d = 1 if reverse else 0
    cmap = (lambda c: nc - 1 - c) if reverse else (lambda c: c)
    hshape = (SSD_HEADS // 2, 2 * SSD_HEAD_DIM, D_STATE)

    chunk_spec = pl.BlockSpec((None, CHUNK, CONV_CH), lambda bi, c: (bi, cmap(c), 0))
    if reverse:
        in_specs = [chunk_spec]
        args = [xin]
    else:
        in_specs = [
            chunk_spec,
            pl.BlockSpec((None, 8, CONV_CH), lambda bi, c: (bi, jnp.maximum(c * rb - 1, 0), 0)),
            pl.BlockSpec((None, 8, CONV_CH), lambda bi, c: (bi, jnp.minimum((c + 1) * rb, nrb - 1), 0)),
            pl.BlockSpec((D_CONV, CONV_CH), lambda bi, c: (0, 0)),
            pl.BlockSpec((1, CONV_CH), lambda bi, c: (0, 0)),
            pl.BlockSpec((1, D_INNER), lambda bi, c: (0, 0)),
        ]
        args = [xin, xin, xin, wts["conv_w"], wts["conv_b"], wts["dskip"]]
    in_specs += [
        pl.BlockSpec((None, CHUNK, LANES), lambda bi, c: (bi, cmap(c), d)),
        pl.BlockSpec((None, 1, LANES), lambda bi, c: (d, 0, 0)),
        pl.BlockSpec((None, 1, LANES), lambda bi, c: (d, 0, 0)),
        pl.BlockSpec((None, CHUNK, CHUNK), lambda bi, c: (d, 0, 0)),
        pl.BlockSpec((LANES, D_INNER), lambda bi, c: (0, 0)),
    ]
    args += [dt, wts["a_neg"], wts["dt_bias"], wts["tri"], wts["head_sel"]]
    if has_h0:
        in_specs.append(pl.BlockSpec((None, None) + hshape, lambda bi, c: (bi, d, 0, 0, 0)))
        args.append(h0)
    if want_hfin and reverse:
        in_specs.append(pl.BlockSpec((None,) + hshape, lambda bi, c: (bi, 0, 0, 0)))
        args.append(hprev)
    out_shape = [jax.ShapeDtypeStruct((b, t, D_INNER), BF16)]
    out_specs = [pl.BlockSpec((None, CHUNK, D_INNER), lambda bi, c: (bi, cmap(c), 0))]
    if not reverse:
        out_shape.append(jax.ShapeDtypeStruct((b, t, CONV_CH), BF16))
        out_specs.append(pl.BlockSpec((None, CHUNK, CONV_CH), lambda bi, c: (bi, c, 0)))
    if want_hfin and reverse:
        out_shape.append(jax.ShapeDtypeStruct((b, 2) + hshape, F32))
        out_specs.append(pl.BlockSpec((None, 2) + hshape, lambda bi, c: (bi, 0, 0, 0, 0)))
    elif want_hfin:
        out_shape.append(jax.ShapeDtypeStruct((b,) + hshape, F32))
        out_specs.append(pl.BlockSpec((None,) + hshape, lambda bi, c: (bi, 0, 0, 0)))
    return pl.pallas_call(
        functools.partial(_ssd_kernel, nc=nc, reverse=reverse, has_h0=has_h0, want_hfin=want_hfin),
        out_shape=tuple(out_shape),
        grid=(b, nc),
        in_specs=in_specs,
        out_specs=tuple(out_specs),
        scratch_shapes=[pltpu.VMEM(hshape, F32)],
        compiler_params=_cparams(("arbitrary", "arbitrary")),
        name="ssd_bwd" if reverse else "ssd_fwd",
    )(*args)


def _ssd_call(xbc, dt, wts, h0, *, want_hfin):
    res = _ssd_sweep(xbc, dt, wts, h0, None, reverse=False, want_hfin=want_hfin)
    y_f, xc = res[0], res[1]
    hf = res[2] if want_hfin else None
    res = _ssd_sweep(xc, dt, wts, h0, hf, reverse=True, want_hfin=want_hfin)
    return y_f, res[0], (res[1] if want_hfin else None)


def _route(logits_t, bias_col):
    e, n = logits_t.shape
    per = e // N_EXPERT_GROUPS
    scores = jax.nn.sigmoid(logits_t)
    sel = scores + bias_col
    neg = jnp.float32(-jnp.inf)
    gs = []
    for g in range(N_EXPERT_GROUPS):
        blk = sel[g * per:(g + 1) * per, :]
        m1 = jnp.max(blk, axis=0, keepdims=True)
        is_m1 = blk == m1
        cnt = jnp.sum(jnp.where(is_m1, 1.0, 0.0), axis=0, keepdims=True)
        m2 = jnp.max(jnp.where(is_m1, neg, blk), axis=0, keepdims=True)
        gs.append(m1 + jnp.where(cnt >= 2.0, m1, m2))
    keep = []
    for g in range(N_EXPERT_GROUPS):
        rank = jnp.zeros_like(gs[g])
        for j in range(N_EXPERT_GROUPS):
            if j == g:
                continue
            beats = (gs[j] > gs[g]) if j > g else (gs[j] >= gs[g])
            rank = rank + jnp.where(beats, 1.0, 0.0)
        keep.append(rank < float(TOPK_GROUPS))
    selm = jnp.concatenate(
        [jnp.where(keep[g], sel[g * per:(g + 1) * per, :], neg) for g in range(N_EXPERT_GROUPS)], axis=0)
    eidx = lax.broadcasted_iota(jnp.int32, (e, n), 0).astype(F32)
    cur = selm
    picks = []
    for _ in range(TOP_K):
        m = jnp.max(cur, axis=0, keepdims=True)
        idx = jnp.min(jnp.where(cur == m, eidx, float(e)), axis=0, keepdims=True)
        hit = eidx == idx
        picks.append((idx, hit))
        cur = jnp.where(hit, neg, cur)
    return scores, picks


def _pack_bf16_pairs(h):
    c = h.shape[1] // 2
    lo = pltpu.bitcast(h[:, :c].astype(BF16).astype(F32), jnp.uint32)
    hi = pltpu.bitcast(h[:, c:].astype(BF16).astype(F32), jnp.uint32)
    return (lo >> 16) | (hi & jnp.uint32(0xFFFF0000))


def _unpack_bf16_pairs(w):
    lo = pltpu.bitcast(w << 16, F32)
    hi = pltpu.bitcast(w & jnp.uint32(0xFFFF0000), F32)
    return lo, hi


def _rows8(rows):
    n = rows[0].shape[1]
    ridx = lax.broadcasted_iota(jnp.int32, (TOP_K, n), 0)
    out = jnp.zeros((TOP_K, n), rows[0].dtype)
    for k, r in enumerate(rows):
        out = jnp.where(ridx == k, r, out)
    return out


MERGE_SUB = 512


def _merge_kernel(x_ref, attn_ref, yf_ref, yb_ref, z_ref, gates_ref, mod_ref, wa_ref, ws_ref, wo_ref,
                  sg_ref, n2_ref, wr_ref, rb_ref, wsg_ref, wsu_ref, wsd_ref,
                  xb_ref, ha_ref, hb_ref, eid_ref, pos_ref, wk_ref, cnt_ref):
    tm = x_ref.shape[0]
    sub = MERGE_SUB

    @pl.when(pl.program_id(0) == 0)
    def _():
        cnt_ref[...] = jnp.zeros_like(cnt_ref)

    r_i = lax.broadcasted_iota(jnp.int32, (sub, sub), 0)
    c_i = lax.broadcasted_iota(jnp.int32, (sub, sub), 1)
    before = jnp.where(r_i < c_i, 1.0, 0.0).astype(BF16)
    cnt = cnt_ref[:, 0:1]

    for r0 in range(0, tm, sub):
        rs = slice(r0, r0 + sub)
        x = x_ref[rs, :]
        yy = yf_ref[rs, :].astype(F32) + yb_ref[rs, :].astype(F32)
        u = yy * _silu(z_ref[rs, :]).astype(F32)
        un = u * lax.rsqrt(jnp.mean(u * u, axis=-1, keepdims=True) + EPS) * sg_ref[...]
        ssd_o = _bdot(un, ws_ref[...])
        attn_o = jnp.dot(attn_ref[rs, :], wa_ref[...], preferred_element_type=F32)
        ga = jax.nn.sigmoid(gates_ref[rs, 0:D_MODEL]).astype(F32)
        gs = jax.nn.sigmoid(gates_ref[rs, D_MODEL:2 * D_MODEL]).astype(F32)
        mix = _bdot(ga * attn_o + gs * ssd_o, wo_ref[...])
        x1 = x + _mod(mod_ref, MOD_GATE1) * mix
        h2 = x1 * lax.rsqrt(jnp.mean(x1 * x1, axis=-1, keepdims=True) + EPS) * n2_ref[...]
        h2 = h2 * (1.0 + _mod(mod_ref, MOD_SCALE2)) + _mod(mod_ref, MOD_SHIFT2)
        h2b = h2.astype(BF16)
        ha_ref[rs, :] = _pack_bf16_pairs(h2[:, :D_MODEL // 2])
        hb_ref[rs, :] = _pack_bf16_pairs(h2[:, D_MODEL // 2:])

        logits_t = _bdot_nt(wr_ref[...], h2b)
        scores, picks = _route(logits_t, rb_ref[...])
        chosen = jnp.zeros_like(scores)
        for _, hit in picks:
            chosen = chosen + jnp.where(hit, 1.0, 0.0)
        pos = cnt + jnp.dot(chosen.astype(BF16), before, preferred_element_type=F32)
        cnt = cnt + jnp.sum(chosen, axis=1, keepdims=True)
        poss = [jnp.sum(jnp.where(hit, pos, 0.0), axis=0, keepdims=True) for _, hit in picks]
        wks = [jnp.sum(jnp.where(hit, scores, 0.0), axis=0, keepdims=True) for _, hit in picks]
        wsum = wks[0]
        for w in wks[1:]:
            wsum = wsum + w
        eid_ref[:, rs] = _rows8([idx for idx, _ in picks]).astype(jnp.int32)
        pos_ref[:, rs] = _rows8(poss).astype(jnp.int32)
        wk8 = _rows8(wks) / wsum * ROUTED_SCALE
        wk_ref[rs, :] = jnp.concatenate([wk8, jnp.zeros((LANES - TOP_K, sub), F32)], axis=0).T

        hid = _silu(jnp.dot(h2b, wsg_ref[...], preferred_element_type=F32)) * \
            jnp.dot(h2b, wsu_ref[...], preferred_element_type=F32)
        xb_ref[rs, :] = x1 + _mod(mod_ref, MOD_GATE2) * _bdot(hid, wsd_ref[...])

    cnt_ref[...] = jnp.broadcast_to(cnt, cnt_ref.shape)


def _merge_call(x, attn, y_f, y_b, z, gates, mod_rows, wa, ws, wo, sg, n2, wr_t, rb, wsg, wsu, wsd, *, tm):
    b, t, _ = x.shape
    n = b * t
    flat = lambda a: a.reshape(n, a.shape[-1])
    tok = lambda width: pl.BlockSpec((tm, width), lambda i: (i, 0))
    const2 = lambda shape: pl.BlockSpec(shape, lambda i: (0, 0), pipeline_mode=pl.Buffered(1))
    k8 = pl.BlockSpec((TOP_K, tm), lambda i: (0, i))
    half = D_MODEL // 4
    return pl.pallas_call(
        _merge_kernel,
        out_shape=(jax.ShapeDtypeStruct((n, D_MODEL), F32),
                   jax.ShapeDtypeStruct((n, half), jnp.uint32),
                   jax.ShapeDtypeStruct((n, half), jnp.uint32),
                   jax.ShapeDtypeStruct((TOP_K, n), jnp.int32),
                   jax.ShapeDtypeStruct((TOP_K, n), jnp.int32),
                   jax.ShapeDtypeStruct((n, LANES), F32),
                   jax.ShapeDtypeStruct((N_EXPERTS, LANES), F32)),
        grid=(n // tm,),
        in_specs=[tok(D_MODEL), tok(ATTN_W), tok(D_INNER), tok(D_INNER), tok(D_INNER), tok(2 * D_MODEL),
                  pl.BlockSpec((None, 1, 6 * D_MODEL), lambda i: ((i * tm) // t, 0, 0)),
                  const2((ATTN_W, D_MODEL)), const2((D_INNER, D_MODEL)), const2((D_MODEL, D_MODEL)),
                  const2((1, D_INNER)), const2((1, D_MODEL)),
                  const2((N_EXPERTS, D_MODEL)), const2((N_EXPERTS, 1)),
                  const2((D_MODEL, D_SHARED)), const2((D_MODEL, D_SHARED)), const2((D_SHARED, D_MODEL))],
        out_specs=(tok(D_MODEL), tok(half), tok(half), k8, k8, tok(LANES),
                   pl.BlockSpec((N_EXPERTS, LANES), lambda i: (0, 0))),
        compiler_params=_cparams(("arbitrary",)),
        name="merge",
    )(flat(x), flat(attn), flat(y_f), flat(y_b), flat(z), flat(gates), mod_rows, wa, ws, wo, sg, n2, wr_t, rb,
      wsg, wsu, wsd)


ROW_TILE = 512
SC_WINDOW = 128


def _slots_kernel(start_ref, eid_ref, pos_ref, slot_ref):
    eid = eid_ref[...]
    slot = pos_ref[...]
    for e in range(N_EXPERTS):
        slot = slot + jnp.where(eid == e, start_ref[e], 0)
    slot_ref[...] = slot


def _slots_call(start, eid, pos):
    n = eid.shape[1]
    bn = 2048 if n % 2048 == 0 else n
    spec = pl.BlockSpec((TOP_K, bn), lambda i, s: (0, i))
    return pl.pallas_call(
        _slots_kernel,
        out_shape=jax.ShapeDtypeStruct((TOP_K, n), jnp.int32),
        grid_spec=pltpu.PrefetchScalarGridSpec(num_scalar_prefetch=1, grid=(n // bn,),
                                               in_specs=[spec, spec], out_specs=spec),
        compiler_params=_cparams(("arbitrary",)),
        name="slots",
    )(start, eid, pos)


def _sc_dispatch(x, slots, p):
    n, d = x.shape
    mesh = plsc.VectorSubcoreMesh(core_axis_name="core", subcore_axis_name="subcore")

    @functools.partial(pl.kernel, out_type=jax.ShapeDtypeStruct((p, d), x.dtype), mesh=mesh)
    def k(x_hbm, s_hbm, o_hbm):
        def body(x_vmem, s_vmem):
            for kk in range(TOP_K):
                pltpu.sync_copy(x_vmem, o_hbm.at[s_vmem.at[kk]])

        pltpu.emit_pipeline(
            body,
            grid=(n // SC_WINDOW,),
            in_specs=[pl.BlockSpec((SC_WINDOW, d), index_map=lambda i: (i, 0)),
                      pl.BlockSpec((TOP_K, SC_WINDOW), index_map=lambda i: (0, i))],
            out_specs=[],
            core_axis_name=("core", "subcore"),
            dimension_semantics=(pltpu.PARALLEL,),
        )(x_hbm, s_hbm)

    return k(x, slots)


def _sc_combine(y, slots):
    kk, n = slots.shape
    d = y.shape[1]
    mesh = plsc.VectorSubcoreMesh(core_axis_name="core", subcore_axis_name="subcore")

    @functools.partial(pl.kernel, out_type=jax.ShapeDtypeStruct((kk * n, d), y.dtype), mesh=mesh)
    def k(y_hbm, s_hbm, o_hbm):
        def body(s_vmem, o_vmem):
            pltpu.sync_copy(y_hbm.at[s_vmem.at[0]], o_vmem)

        pltpu.emit_pipeline(
            body,
            grid=(kk * n // SC_WINDOW,),
            in_specs=[pl.BlockSpec((1, SC_WINDOW), index_map=lambda i: (0, i))],
            out_specs=[pl.BlockSpec((SC_WINDOW, d), index_map=lambda i: (i, 0))],
            core_axis_name=("core", "subcore"),
            dimension_semantics=(pltpu.PARALLEL,),
        )(s_hbm, o_hbm)

    return k(y, slots.reshape(1, kk * n)).reshape(kk, n, d)


FFN_IN_BUFS = 4
FFN_OUT_BUFS = 3


def _ffn_kernel(st_ref, nt_ref, wg_ref, wu_ref, wd_ref, xa_hbm, xb_hbm, ya_hbm, yb_hbm,
                wg_s, wu_s, wd_s, xa_buf, xb_buf, ya_buf, yb_buf, in_sem, out_sem):
    e = pl.program_id(0)
    ne = pl.num_programs(0)
    n = nt_ref[e]
    t = ROW_TILE
    ahead = FFN_IN_BUFS - 1

    def fetch(row, slot):
        r = pl.multiple_of(row, t)
        return (pltpu.make_async_copy(xa_hbm.at[pl.ds(r, t)], xa_buf.at[slot], in_sem.at[0, slot]),
                pltpu.make_async_copy(xb_hbm.at[pl.ds(r, t)], xb_buf.at[slot], in_sem.at[1, slot]))

    def put(row, slot):
        r = pl.multiple_of(row, t)
        return (pltpu.make_async_copy(ya_buf.at[slot], ya_hbm.at[pl.ds(r, t)], out_sem.at[0, slot]),
                pltpu.make_async_copy(yb_buf.at[slot], yb_hbm.at[pl.ds(r, t)], out_sem.at[1, slot]))

    def start_head(expert):
        for k in range(ahead):
            @pl.when(k < nt_ref[expert])
            def _():
                for c in fetch(st_ref[expert] + k * t, k):
                    c.start()

    @pl.when(e == 0)
    def _():
        start_head(0)

    @pl.when(n > 0)
    def _():
        wg_s[...] = wg_ref[...].astype(BF16)
        wu_s[...] = wu_ref[...].astype(BF16)
        wd_s[...] = wd_ref[...].astype(BF16)
        base = st_ref[e]

        def body(i, carry):
            si = lax.rem(i, FFN_IN_BUFS)
            so = lax.rem(i, FFN_OUT_BUFS)
            row = base + i * t
            for c in fetch(row, si):
                c.wait()

            @pl.when(i + ahead < n)
            def _():
                for c in fetch(row + ahead * t, lax.rem(i + ahead, FFN_IN_BUFS)):
                    c.start()

            @pl.when(i >= FFN_OUT_BUFS)
            def _():
                for c in put(row - FFN_OUT_BUFS * t, so):
                    c.wait()

            parts = _unpack_bf16_pairs(xa_buf[si]) + _unpack_bf16_pairs(xb_buf[si])
            x = jnp.concatenate(parts, axis=1).astype(BF16)
            hid = _silu(jnp.dot(x, wg_s[...], preferred_element_type=F32)) * \
                jnp.dot(x, wu_s[...], preferred_element_type=F32)
            y = _bdot(hid, wd_s[...])
            ya_buf[so] = _pack_bf16_pairs(y[:, :D_MODEL // 2])
            yb_buf[so] = _pack_bf16_pairs(y[:, D_MODEL // 2:])
            for c in put(row, so):
                c.start()
            return carry

        lax.fori_loop(0, n, body, 0)

        for k in range(FFN_OUT_BUFS):
            @pl.when(n - 1 - k >= 0)
            def _():
                j = n - 1 - k
                for c in put(base + j * t, lax.rem(j, FFN_OUT_BUFS)):
                    c.wait()

    @pl.when(e + 1 < ne)
    def _():
        start_head(jnp.minimum(e + 1, ne - 1))


def _ffn_call(start, tiles_e, xa, xb, wg, wu, wd):
    p, half = xa.shape
    wspec = lambda s: pl.BlockSpec((None,) + s, lambda e, st, nt: (e, 0, 0))
    hbm = pl.BlockSpec(memory_space=pl.ANY)
    ibuf = pltpu.VMEM((FFN_IN_BUFS, ROW_TILE, half), jnp.uint32)
    obuf = pltpu.VMEM((FFN_OUT_BUFS, ROW_TILE, half), jnp.uint32)
    return pl.pallas_call(
        _ffn_kernel,
        out_shape=(jax.ShapeDtypeStruct((p, half), jnp.uint32), jax.ShapeDtypeStruct((p, half), jnp.uint32)),
        grid_spec=pltpu.PrefetchScalarGridSpec(
            num_scalar_prefetch=2, grid=(N_EXPERTS,),
            in_specs=[wspec((D_MODEL, D_EXPERT)), wspec((D_MODEL, D_EXPERT)), wspec((D_EXPERT, D_MODEL)),
                      hbm, hbm],
            out_specs=(hbm, hbm),
            scratch_shapes=[pltpu.VMEM((D_MODEL, D_EXPERT), BF16), pltpu.VMEM((D_MODEL, D_EXPERT), BF16),
                            pltpu.VMEM((D_EXPERT, D_MODEL), BF16), ibuf, ibuf, obuf, obuf,
                            pltpu.SemaphoreType.DMA((2, FFN_IN_BUFS)),
                            pltpu.SemaphoreType.DMA((2, FFN_OUT_BUFS))]),
        compiler_params=_cparams(("arbitrary",)),
        name="ffn",
    )(start, tiles_e, wg, wu, wd, xa, xb)


def _final_kernel(xb_ref, ya_ref, yb_ref, wk_ref, mod_ref, o_ref):
    q = D_MODEL // 4
    accs = [jnp.zeros((xb_ref.shape[0], q), F32) for _ in range(4)]
    for k in range(TOP_K):
        w = wk_ref[:, k:k + 1]
        parts = _unpack_bf16_pairs(ya_ref[k]) + _unpack_bf16_pairs(yb_ref[k])
        accs = [a + w * p for a, p in zip(accs, parts)]
    for i, a in enumerate(accs):
        o_ref[:, i * q:(i + 1) * q] = xb_ref[:, i * q:(i + 1) * q] + _mod(mod_ref, MOD_GATE2, i * q, (i + 1) * q) * a


def _final_call(xb, ya, yb, wk, mod_rows, t, *, tm):
    n = xb.shape[0]
    half = ya.shape[2]
    tok = lambda width: pl.BlockSpec((tm, width), lambda i: (i, 0))
    yspec = pl.BlockSpec((TOP_K, tm, half), lambda i: (0, i, 0))
    return pl.pallas_call(
        _final_kernel,
        out_shape=jax.ShapeDtypeStruct((n, D_MODEL), F32),
        grid=(n // tm,),
        in_specs=[tok(D_MODEL), yspec, yspec, tok(LANES),
                  pl.BlockSpec((None, 1, 6 * D_MODEL), lambda i: ((i * tm) // t, 0, 0))],
        out_specs=tok(D_MODEL),
        compiler_params=_cparams(("arbitrary",)),
        name="final",
    )(xb, ya, yb, wk, mod_rows)


def _moe_call(ha, hb, eid, pos, wk, counts, xb, mod_rows, wg, wu, wd, t):
    n = xb.shape[0]
    max_tiles = n * TOP_K // ROW_TILE + N_EXPERTS
    p = max_tiles * ROW_TILE
    cnt = counts[:, 0].astype(jnp.int32)
    tiles_e = (cnt + ROW_TILE - 1) // ROW_TILE
    ends = jnp.cumsum(tiles_e)
    start = (ends - tiles_e) * ROW_TILE
    slots = _slots_call(start.astype(jnp.int32), eid, pos)
    xa = _sc_dispatch(ha, slots, p)
    xbb = _sc_dispatch(hb, slots, p)
    ya, yb = _ffn_call(start.astype(jnp.int32), tiles_e.astype(jnp.int32), xa, xbb, wg, wu, wd)
    ga = _sc_combine(ya, slots)
    gb = _sc_combine(yb, slots)
    return _final_call(xb, ga, gb, wk, mod_rows, t, tm=256)


def _rope_tables(t):
    n_rows = t // GRID_W
    rows = jnp.repeat(jnp.arange(n_rows), GRID_W).astype(F32)
    cols = jnp.tile(jnp.arange(GRID_W), n_rows).astype(F32)
    n_freq = HEAD_DIM // 4
    freqs = ROPE_THETA ** (-jnp.arange(n_freq, dtype=F32) / n_freq)
    ang = jnp.concatenate([rows[:, None] * freqs, cols[:, None] * freqs], axis=-1)
    ang = jnp.repeat(ang, 2, axis=-1)
    ang = jnp.concatenate([ang, ang], axis=-1)
    sign = jnp.where(jnp.arange(LANES) % 2 == 0, -1.0, 1.0).astype(F32)
    return jnp.cos(ang), jnp.sin(ang) * sign


def _dup_heads(a):
    s = a.shape[:-1]
    a4 = a.reshape(s + (N_KV_HEADS, HEAD_DIM))
    return jnp.concatenate([a4, a4], axis=-1).reshape(s + (KVD_W,))


def _prep_w_in(w_in):
    idx = np.cumsum(SPLIT_SIZES)[:-1].tolist()
    q, k, v, z, xbc, dt, gates = jnp.split(w_in, idx, axis=-1)
    pad = jnp.zeros((D_MODEL, LANES - SSD_HEADS), w_in.dtype)
    cols = [q, k, v, gates, z, xbc,
            dt[:, :SSD_HEADS], pad, dt[:, SSD_HEADS:], pad]
    return jnp.concatenate(cols, axis=-1).astype(BF16)


def _pad_heads(a):
    return jnp.pad(a.astype(F32), ((0, 0), (0, LANES - SSD_HEADS)))[:, None, :]


def _trunk(x, mod_rows, wts, rope_tabs, ctx_k, ctx_v, h0, *, tm, tq, want_state):
    b, t, _ = x.shape
    rope = rope_tabs is not None
    if rope:
        cos, sin = rope_tabs
    else:
        cos = sin = jnp.zeros((t, LANES), F32)
    res = _inproj_call(x, mod_rows, wts["g1"], wts["w_in"], wts["qg"], wts["kg"], cos, sin,
                       rope=rope, emit_kv=want_state, tm=tm)
    q, k, v, gates, z, xbc, dt = res[:7]
    kv_raw = res[7:]
    attn = _attn_call(q, k, v, ctx_k, ctx_v, tq=tq)
    y_f, y_b, hfin = _ssd_call(xbc, dt, wts, h0, want_hfin=want_state)
    xb, ha, hb, eid, pos, wk, counts = _merge_call(
        x, attn, y_f, y_b, z, gates, mod_rows, wts["wa"], wts["ws"], wts["wo"], wts["sg"],
        wts["n2"], wts["wr_t"], wts["rb"], wts["wsg"], wts["wsu"], wts["wsd"], tm=MERGE_SUB)
    out = _moe_call(ha, hb, eid, pos, wk, counts, xb, mod_rows, wts["weg"], wts["weu"], wts["wed"], t)
    return out.reshape(b, t, D_MODEL), kv_raw, hfin


def kernel(x_prompt, x_sample, cache_k, cache_v, state_ssm, c, c_ctx, w_mod, b_mod, norm1_g, norm2_g, w_in,
           q_norm_g, k_norm_g, conv_w, conv_b, a_log, dt_bias, d_skip, ssd_norm_g, w_attn_proj, w_ssd_proj,
           w_out, w_router, router_bias, w_exp_gate, w_exp_up, w_exp_down, w_sh_gate, w_sh_up, w_sh_down):
    depth = w_mod.shape[0]
    assert depth == 1, "single trunk layer"
    bp, tp, _ = x_prompt.shape
    bs, ts, _ = x_sample.shape
    l = 0
    cvec = jnp.concatenate([c_ctx[None, :], c, jnp.zeros((8 - 1 - bs, D_MODEL), F32)], axis=0)
    mod = _mod_call(cvec, w_mod.reshape(D_MODEL, 6 * D_MODEL), b_mod.reshape(1, 6 * D_MODEL))
    mod_prompt = jnp.broadcast_to(mod[0:1][:, None, :], (bp, 1, 6 * D_MODEL))
    mod_sample = mod[1:1 + bs][:, None, :]

    lower = np.tril(np.ones((CHUNK, CHUNK), np.float32))
    head_sel = (np.arange(LANES)[:, None] == np.arange(D_INNER)[None, :] // SSD_HEAD_DIM).astype(np.float32)
    wts = dict(
        g1=norm1_g[l][None, :], n2=norm2_g[l][None, :],
        w_in=_prep_w_in(w_in.reshape(D_MODEL, w_in.shape[-1])),
        qg=jnp.tile(q_norm_g[l], 2)[None, :], kg=jnp.tile(k_norm_g[l], 2)[None, :],
        conv_w=conv_w[l], conv_b=conv_b[l][None, :],
        a_neg=_pad_heads(-jnp.exp(a_log[l].astype(F32))), dt_bias=_pad_heads(dt_bias[l]),
        dskip=jnp.repeat(d_skip[l].astype(F32), SSD_HEAD_DIM)[None, :],
        tri=jnp.asarray(np.stack([lower, lower.T])),
        head_sel=jnp.asarray(head_sel, BF16),
        sg=ssd_norm_g[l][None, :],
        wa=w_attn_proj[l].astype(BF16), ws=w_ssd_proj[l].astype(BF16), wo=w_out[l].astype(BF16),
        wr_t=w_router[l].T.astype(BF16), rb=router_bias[l].astype(F32)[:, None],
        wsg=w_sh_gate[l].astype(BF16), wsu=w_sh_up[l].astype(BF16), wsd=w_sh_down[l].astype(BF16),
        weg=w_exp_gate.reshape(w_exp_gate.shape[1:]), weu=w_exp_up.reshape(w_exp_up.shape[1:]),
        wed=w_exp_down.reshape(w_exp_down.shape[1:]),
    )

    y_prompt, (k_p, v_p), hfin = _trunk(x_prompt, mod_prompt, wts, None, None, None, None,
                                        tm=256, tq=256, want_state=True)
    new_k = k_p.reshape(bp, 1, tp, N_KV_HEADS, HEAD_DIM)
    new_v = v_p.reshape(bp, 1, tp, N_KV_HEADS, HEAD_DIM)
    new_state = hfin.reshape(bp, 1, 2, SSD_HEADS, SSD_HEAD_DIM, D_STATE)

    past = cache_k.shape[2]
    ctx_k = _dup_heads(cache_k[:, l].reshape(bs, past, KV_W)).astype(BF16)
    ctx_v = _dup_heads(cache_v[:, l].reshape(bs, past, KV_W)).astype(BF16)
    h0 = state_ssm[:, l].reshape(bs, 2, SSD_HEADS // 2, 2 * SSD_HEAD_DIM, D_STATE)
    y_sample, _, _ = _trunk(x_sample, mod_sample, wts, _rope_tables(ts), ctx_k, ctx_v, h0,
                               tm=512, tq=256, want_state=False)
    return (y_prompt, y_sample, new_k, new_v, new_state)
```

```python
import functools

import numpy as np
import jax
import jax.numpy as jnp
from jax import lax
from jax.experimental import pallas as pl
from jax.experimental.pallas import tpu as pltpu
from jax.experimental.pallas import tpu_sc as plsc

F32 = jnp.float32
BF16 = jnp.bfloat16

D_MODEL = 1024
GRID_W = 64
EPS = 1e-6
N_HEADS = 16
N_KV_HEADS = 4
HEAD_DIM = 64
ATTN_W = N_HEADS * HEAD_DIM
KV_W = N_KV_HEADS * HEAD_DIM
ROPE_THETA = 10000.0
D_INNER = 2048
SSD_HEAD_DIM = 64
SSD_HEADS = 32
SSD_GROUPS = 4
D_STATE = 128
D_CONV = 4
CHUNK = 128
CONV_CH = D_INNER + 2 * SSD_GROUPS * D_STATE
N_EXPERTS = 64
TOP_K = 8
N_EXPERT_GROUPS = 8
TOPK_GROUPS = 4
D_EXPERT = 256
D_SHARED = 256
ROUTED_SCALE = 2.5

LANES = 128
KVD_W = N_KV_HEADS * LANES
M_Q, M_K, M_V, M_Z, M_X, M_END = (int(c) for c in np.cumsum((0, ATTN_W, KV_W, KV_W, D_INNER, CONV_CH)))
T_G, T_DT, T_END = (int(c) for c in np.cumsum((0, 2 * D_MODEL, 2 * LANES)))
MOD_SHIFT1, MOD_SCALE1, MOD_GATE1, MOD_SHIFT2, MOD_SCALE2, MOD_GATE2 = range(6)
VMEM_LIMIT = 56 * 1024 * 1024
Q_SCALE = HEAD_DIM ** -0.5 * 1.4426950408889634


def _cparams(sem):
    return pltpu.CompilerParams(dimension_semantics=sem, vmem_limit_bytes=VMEM_LIMIT)


def _mod(mod_ref, which, lo=0, hi=D_MODEL):
    return mod_ref[:, which * D_MODEL + lo:which * D_MODEL + hi]


def _silu(x):
    return x * jax.nn.sigmoid(x)


def _bdot(a, b):
    return jnp.dot(a.astype(BF16), b.astype(BF16), preferred_element_type=F32)


def _bdot_nt(a, b):
    return lax.dot_general(a.astype(BF16), b.astype(BF16), (((1,), (1,)), ((), ())),
                           preferred_element_type=F32)


def _mod_kernel(c_ref, w_ref, b_ref, o_ref):
    o_ref[...] = _bdot(_silu(c_ref[...]), w_ref[...]) + b_ref[...]


def _mod_call(cvec, w_mod, b_mod):
    n = w_mod.shape[1]
    bn = 1024
    return pl.pallas_call(
        _mod_kernel,
        out_shape=jax.ShapeDtypeStruct((8, n), F32),
        grid=(n // bn,),
        in_specs=[pl.BlockSpec((8, D_MODEL), lambda j: (0, 0)),
                  pl.BlockSpec((D_MODEL, bn), lambda j: (0, j)),
                  pl.BlockSpec((1, bn), lambda j: (0, j))],
        out_specs=pl.BlockSpec((8, bn), lambda j: (0, j)),
        compiler_params=_cparams(("arbitrary",)),
        name="mod",
    )(cvec, w_mod, b_mod)


def _inproj_kernel(*refs, rope, emit_kv):
    if emit_kv:
        (x_ref, mod_ref, g1_ref, wm_ref, wt_ref, qg_ref, kg_ref, cos_ref, sin_ref,
         q_ref, k_ref, v_ref, gates_ref, z_ref, xbc_ref, dt_ref, kraw_ref, vraw_ref) = refs
    else:
        (x_ref, mod_ref, g1_ref, wm_ref, wt_ref, qg_ref, kg_ref, cos_ref, sin_ref,
         q_ref, k_ref, v_ref, gates_ref, z_ref, xbc_ref, dt_ref) = refs
    tm = x_ref.shape[0]
    x = x_ref[...]
    inv = lax.rsqrt(jnp.mean(x * x, axis=-1, keepdims=True) + EPS)
    h = (x * inv) * g1_ref[...]
    h = h * (1.0 + _mod(mod_ref, MOD_SCALE1)) + _mod(mod_ref, MOD_SHIFT1)
    hb = h.astype(BF16)

    lane = lax.broadcasted_iota(jnp.int32, (tm, LANES), 1)
    lo = lane < HEAD_DIM
    even = (lane & 1) == 0
    if rope:
        cos = cos_ref[...]
        sin = sin_ref[...]

    def rope_fn(blk):
        nxt = pltpu.roll(blk, LANES - 1, 1)
        prv = pltpu.roll(blk, 1, 1)
        return blk * cos + jnp.where(even, nxt, prv) * sin

    def head_norm(blk, g):
        sq = blk * blk
        s_all = jnp.sum(sq, axis=-1, keepdims=True)
        s_lo = jnp.sum(jnp.where(lo, sq, 0.0), axis=-1, keepdims=True)
        ms = jnp.where(lo, s_lo, s_all - s_lo) * (1.0 / HEAD_DIM)
        return blk * lax.rsqrt(ms + EPS) * g

    def dup_heads(blk):
        sw = pltpu.roll(blk, HEAD_DIM, 1)
        return jnp.where(lo, blk, sw), jnp.where(lo, sw, blk)

    qg = qg_ref[...]
    kg = kg_ref[...]
    q = jnp.dot(hb, wm_ref[:, M_Q:M_K], preferred_element_type=F32)
    for j in range(ATTN_W // LANES):
        blk = head_norm(q[:, j * LANES:(j + 1) * LANES], qg)
        if rope:
            blk = rope_fn(blk)
        q_ref[:, j * LANES:(j + 1) * LANES] = (blk * Q_SCALE).astype(q_ref.dtype)

    k = jnp.dot(hb, wm_ref[:, M_K:M_V], preferred_element_type=F32)
    v = jnp.dot(hb, wm_ref[:, M_V:M_Z], preferred_element_type=F32)
    for j in range(KV_W // LANES):
        kb = head_norm(k[:, j * LANES:(j + 1) * LANES], kg)
        vb = v[:, j * LANES:(j + 1) * LANES]
        if emit_kv:
            kraw_ref[:, j * LANES:(j + 1) * LANES] = kb
            vraw_ref[:, j * LANES:(j + 1) * LANES] = vb
        if rope:
            kb = rope_fn(kb)
        for i, (kd, vd) in enumerate(zip(dup_heads(kb), dup_heads(vb))):
            c0 = (2 * j + i) * LANES
            k_ref[:, c0:c0 + LANES] = kd.astype(k_ref.dtype)
            v_ref[:, c0:c0 + LANES] = vd.astype(v_ref.dtype)

    gates_ref[...] = jnp.dot(hb, wt_ref[:, T_G:T_DT], preferred_element_type=F32).astype(gates_ref.dtype)
    z_ref[...] = jnp.dot(hb, wm_ref[:, M_Z:M_X], preferred_element_type=F32).astype(z_ref.dtype)
    xbc_ref[...] = jnp.dot(hb, wm_ref[:, M_X:M_END], preferred_element_type=F32).astype(xbc_ref.dtype)
    dt_ref[...] = jnp.dot(hb, wt_ref[:, T_DT:T_END], preferred_element_type=F32)


def _inproj_call(x, mod_rows, g1, w_main, w_tail, qg, kg, cos, sin, *, rope, emit_kv, tm):
    b, t, _ = x.shape
    nt = t // tm
    tok = lambda width: pl.BlockSpec((None, tm, width), lambda bi, i: (bi, i, 0))
    const2 = lambda shape: pl.BlockSpec(shape, lambda bi, i: (0, 0))
    out_shape = [
        jax.ShapeDtypeStruct((b, t, ATTN_W), BF16),
        jax.ShapeDtypeStruct((b, t, KVD_W), BF16),
        jax.ShapeDtypeStruct((b, t, KVD_W), BF16),
        jax.ShapeDtypeStruct((b, t, 2 * D_MODEL), BF16),
        jax.ShapeDtypeStruct((b, t, D_INNER), BF16),
        jax.ShapeDtypeStruct((b, t, CONV_CH), BF16),
        jax.ShapeDtypeStruct((b, t, 2 * LANES), F32),
    ]
    out_specs = [tok(ATTN_W), tok(KVD_W), tok(KVD_W), tok(2 * D_MODEL), tok(D_INNER), tok(CONV_CH),
                 tok(2 * LANES)]
    if emit_kv:
        out_shape += [jax.ShapeDtypeStruct((b, t, KV_W), F32)] * 2
        out_specs += [tok(KV_W), tok(KV_W)]
    return pl.pallas_call(
        functools.partial(_inproj_kernel, rope=rope, emit_kv=emit_kv),
        out_shape=tuple(out_shape),
        grid=(b, nt),
        in_specs=[tok(D_MODEL),
                  pl.BlockSpec((None, 1, 6 * D_MODEL), lambda bi, i: (bi, 0, 0)),
                  const2((1, D_MODEL)),
                  pl.BlockSpec((D_MODEL, M_END), lambda bi, i: (0, 0), pipeline_mode=pl.Buffered(1)),
                  pl.BlockSpec((D_MODEL, T_END), lambda bi, i: (0, 0), pipeline_mode=pl.Buffered(1)),
                  const2((1, LANES)), const2((1, LANES)),
                  pl.BlockSpec((tm, LANES), lambda bi, i: (i, 0)),
                  pl.BlockSpec((tm, LANES), lambda bi, i: (i, 0))],
        out_specs=tuple(out_specs),
        compiler_params=_cparams(("arbitrary", "arbitrary")),
        name="inproj",
    )(x, mod_rows, g1, w_main, w_tail, qg, kg, cos, sin)


KEY_CHUNK = 512


def _key_chunk(n):
    return KEY_CHUNK if n % KEY_CHUNK == 0 else n


def _attn_kernel(*refs, has_ctx):
    if has_ctx:
        q_ref, k_ref, v_ref, kctx_ref, vctx_ref, o_ref = refs
        sources = ((k_ref, v_ref), (kctx_ref, vctx_ref))
    else:
        q_ref, k_ref, v_ref, o_ref = refs
        sources = ((k_ref, v_ref),)
    tq = q_ref.shape[0]
    lane = lax.broadcasted_iota(jnp.int32, (tq, LANES), 1)
    lo = lane < HEAD_DIM
    qs = []
    for j in range(2):
        q2 = q_ref[:, j * LANES:(j + 1) * LANES]
        zero = jnp.zeros_like(q2)
        qs += [jnp.where(lo, q2, zero), jnp.where(lo, zero, q2)]
    q4 = jnp.concatenate(qs, axis=0)
    rows = 4 * tq
    m = jnp.full((rows, 1), -jnp.inf, F32)
    acc = jnp.zeros((rows, LANES), F32)
    chunks = [(kr, vr, c, _key_chunk(kr.shape[0])) for kr, vr in sources
              for c in range(kr.shape[0] // _key_chunk(kr.shape[0]))]
    for kr, vr, c, kc in chunks:
        kch = kr[c * kc:(c + 1) * kc, :].astype(BF16)
        vch = vr[c * kc:(c + 1) * kc, :].astype(BF16)
        lane_k = lax.broadcasted_iota(jnp.int32, (kc, LANES), 1)
        vch = jnp.where(lane_k < HEAD_DIM, vch, jnp.ones_like(vch))
        s = _bdot_nt(q4, kch)
        m_new = jnp.maximum(m, jnp.max(s, axis=-1, keepdims=True))
        alpha = jnp.exp2(m - m_new)
        p = jnp.exp2((s - m_new).astype(BF16))
        acc = acc * alpha + jnp.dot(p, vch, preferred_element_type=F32)
        m = m_new
    o = acc * (1.0 / pltpu.roll(acc, HEAD_DIM, 1))
    for j in range(2):
        oa = o[(2 * j) * tq:(2 * j + 1) * tq]
        ob = pltpu.roll(o[(2 * j + 1) * tq:(2 * j + 2) * tq], HEAD_DIM, 1)
        o_ref[:, j * LANES:(j + 1) * LANES] = jnp.where(lo, oa, ob).astype(o_ref.dtype)


def _attn_call(q, k, v, kctx, vctx, *, tq):
    b, t, _ = q.shape
    tk = k.shape[1]
    nq = t // tq
    has_ctx = kctx is not None
    kv_spec = lambda n: pl.BlockSpec((None, n, LANES), lambda bi, g, i: (bi, 0, g))
    in_specs = [pl.BlockSpec((None, tq, 2 * LANES), lambda bi, g, i: (bi, i, g)), kv_spec(tk), kv_spec(tk)]
    args = [q, k, v]
    if has_ctx:
        in_specs += [kv_spec(kctx.shape[1]), kv_spec(kctx.shape[1])]
        args += [kctx, vctx]
    return pl.pallas_call(
        functools.partial(_attn_kernel, has_ctx=has_ctx),
        out_shape=jax.ShapeDtypeStruct((b, t, ATTN_W), BF16),
        grid=(b, N_KV_HEADS, nq),
        in_specs=in_specs,
        out_specs=pl.BlockSpec((None, tq, 2 * LANES), lambda bi, g, i: (bi, i, g)),
        compiler_params=_cparams(("arbitrary", "arbitrary", "arbitrary")),
        name="attn",
    )(*args)


LOG2E = 1.4426950408889634


def _softplus(x):
    return jnp.maximum(x, 0.0) + jnp.log(1.0 + jnp.exp(-jnp.abs(x)))


def _ssd_kernel(*refs, nc, reverse, has_h0, want_hfin):
    refs = list(refs)
    conv = not reverse
    if conv:
        xbc_ref, prev_ref, next_ref, cw_ref, cb_ref, dsk_ref = refs[:6]
        refs = refs[6:]
    else:
        xc_ref = refs.pop(0)
    dt_ref, an_ref, dtb_ref, tri_ref, sel_ref = refs[:5]
    refs = refs[5:]
    h0_ref = refs.pop(0) if has_h0 else None
    hprev_ref = refs.pop(0) if (want_hfin and reverse) else None
    y_ref = refs.pop(0)
    xco_ref = refs.pop(0) if conv else None
    hfin_ref = refs.pop(0) if want_hfin else None
    h_scr = refs.pop(0)

    L = CHUNK
    c = pl.program_id(1)
    cidx = (nc - 1 - c) if reverse else c

    @pl.when(c == 0)
    def _():
        if has_h0:
            h_scr[...] = h0_ref[...]
        else:
            h_scr[...] = jnp.zeros_like(h_scr)

    row = lax.broadcasted_iota(jnp.int32, (L, LANES), 0)
    lane = lax.broadcasted_iota(jnp.int32, (L, LANES), 1)
    lo = lane < SSD_HEAD_DIM
    top = row < SSD_HEAD_DIM

    if conv:
        first = cidx == 0
        last = cidx == nc - 1

        def cols(a, w):
            xm = xbc_ref[:, a:a + w].astype(F32)
            rw = lax.broadcasted_iota(jnp.int32, (L, w), 0)
            p6 = jnp.where(first, 0.0, prev_ref[6:7, a:a + w].astype(F32))
            p7 = jnp.where(first, 0.0, prev_ref[7:8, a:a + w].astype(F32))
            n0 = jnp.where(last, 0.0, next_ref[0:1, a:a + w].astype(F32))
            r1 = jnp.where(rw == 0, p7, pltpu.roll(xm, 1, 0))
            r2 = jnp.where(rw == 0, p6, jnp.where(rw == 1, p7, pltpu.roll(xm, 2, 0)))
            rn = jnp.where(rw == L - 1, n0, pltpu.roll(xm, L - 1, 0))
            y = (r2 * cw_ref[0:1, a:a + w] + r1 * cw_ref[1:2, a:a + w] + xm * cw_ref[2:3, a:a + w]
                 + rn * cw_ref[3:4, a:a + w] + cb_ref[:, a:a + w])
            y = _silu(y).astype(BF16)
            xco_ref[:, a:a + w] = y
            return y
    else:
        def cols(a, w):
            return xc_ref[:, a:a + w]

    causal = tri_ref[...] > 0.0
    dt = _softplus(dt_ref[...] + dtb_ref[...])
    la2 = dt * (an_ref[...] * LOG2E)
    acum2 = jnp.dot(tri_ref[...], la2, preferred_element_type=F32, precision=lax.Precision.HIGHEST)
    dt_t = dt.T
    acum2_t = acum2.T
    tot2_t = jnp.sum(la2.T, axis=1, keepdims=True)
    lg_dt_t = jnp.log2(dt_t)
    r_t = lg_dt_t - acum2_t
    w_t = jnp.exp2(lg_dt_t + tot2_t - acum2_t)
    e_acum_x = jnp.dot(jnp.exp2(acum2).astype(BF16), sel_ref[...], preferred_element_type=F32)
    e_tot_t = jnp.exp2(tot2_t)

    for g in range(SSD_GROUPS):
        bgb = cols(D_INNER + g * D_STATE, D_STATE)
        cgb = cols(D_INNER + SSD_GROUPS * D_STATE + g * D_STATE, D_STATE)
        cbm = _bdot_nt(cgb, bgb)
        h_grp = h_scr[4 * g:4 * g + 4]
        yo_grp = _bdot_nt(cgb, h_grp.reshape(4 * LANES, D_STATE))
        xws, cds = [], []
        for pr in range(4):
            hp = g * 4 + pr
            ha, hb = 2 * hp, 2 * hp + 1
            xpb = cols(hp * LANES, LANES)
            zero = jnp.zeros_like(xpb)
            xs = jnp.concatenate([jnp.where(lo, xpb, zero), jnp.where(lo, zero, xpb)], axis=0)
            ms = []
            for hh in (ha, hb):
                e = jnp.exp2(acum2[:, hh:hh + 1] + r_t[hh:hh + 1, :])
                ms.append((cbm * jnp.where(causal, e, 0.0)).astype(BF16))
            y = jnp.dot(jnp.concatenate(ms, axis=1), xs, preferred_element_type=F32)
            y = y + yo_grp[:, pr * LANES:(pr + 1) * LANES] * e_acum_x[:, hp * LANES:(hp + 1) * LANES]
            if conv:
                y = y + dsk_ref[:, hp * LANES:(hp + 1) * LANES] * xpb.astype(F32)
            y_ref[:, hp * LANES:(hp + 1) * LANES] = y.astype(y_ref.dtype)
            wsel = jnp.where(top, w_t[ha:ha + 1, :], w_t[hb:hb + 1, :])
            xws.append((xpb.astype(F32).T * wsel).astype(BF16))
            cds.append(jnp.where(top, e_tot_t[ha:ha + 1, :], e_tot_t[hb:hb + 1, :]))
        st = jnp.dot(jnp.concatenate(xws, axis=0), bgb, preferred_element_type=F32)
        for pr in range(4):
            h_scr[g * 4 + pr] = h_grp[pr] * cds[pr] + st[pr * LANES:(pr + 1) * LANES]

    if want_hfin:
        @pl.when(c == nc - 1)
        def _():
            if reverse:
                hfin_ref[0] = hprev_ref[...]
                hfin_ref[1] = h_scr[...]
            else:
                hfin_ref[...] = h_scr[...]


def _ssd_sweep(xin, dt, wts, h0, hprev, *, reverse, want_hfin):
    b, t, _ = xin.shape
    nc = t // CHUNK
    has_h0 = h0 is not None
    rb = CHUNK // 8
    nrb = t // 8
    d = 1 if reverse else 0
    cmap = (lambda c: nc - 1 - c) if reverse else (lambda c: c)
    hshape = (SSD_HEADS // 2, 2 * SSD_HEAD_DIM, D_STATE)

    chunk_spec = pl.BlockSpec((None, CHUNK, CONV_CH), lambda bi, c: (bi, cmap(c), 0))
    if reverse:
        in_specs = [chunk_spec]
        args = [xin]
    else:
        in_specs = [
            chunk_spec,
            pl.BlockSpec((None, 8, CONV_CH), lambda bi, c: (bi, jnp.maximum(c * rb - 1, 0), 0)),
            pl.BlockSpec((None, 8, CONV_CH), lambda bi, c: (bi, jnp.minimum((c + 1) * rb, nrb - 1), 0)),
            pl.BlockSpec((D_CONV, CONV_CH), lambda bi, c: (0, 0)),
            pl.BlockSpec((1, CONV_CH), lambda bi, c: (0, 0)),
            pl.BlockSpec((1, D_INNER), lambda bi, c: (0, 0)),
        ]
        args = [xin, xin, xin, wts["conv_w"], wts["conv_b"], wts["dskip"]]
    in_specs += [
        pl.BlockSpec((None, CHUNK, LANES), lambda bi, c: (bi, cmap(c), d)),
        pl.BlockSpec((None, 1, LANES), lambda bi, c: (d, 0, 0)),
        pl.BlockSpec((None, 1, LANES), lambda bi, c: (d, 0, 0)),
        pl.BlockSpec((None, CHUNK, CHUNK), lambda bi, c: (d, 0, 0)),
        pl.BlockSpec((LANES, D_INNER), lambda bi, c: (0, 0)),
    ]
    args += [dt, wts["a_neg"], wts["dt_bias"], wts["tri"], wts["head_sel"]]
    if has_h0:
        in_specs.append(pl.BlockSpec((None, None) + hshape, lambda bi, c: (bi, d, 0, 0, 0)))
        args.append(h0)
    if want_hfin and reverse:
        in_specs.append(pl.BlockSpec((None,) + hshape, lambda bi, c: (bi, 0, 0, 0)))
        args.append(hprev)
    out_shape = [jax.ShapeDtypeStruct((b, t, D_INNER), BF16)]
    out_specs = [pl.BlockSpec((None, CHUNK, D_INNER), lambda bi, c: (bi, cmap(c), 0))]
    if not reverse:
        out_shape.append(jax.ShapeDtypeStruct((b, t, CONV_CH), BF16))
        out_specs.append(pl.BlockSpec((None, CHUNK, CONV_CH), lambda bi, c: (bi, c, 0)))
    if want_hfin and reverse:
        out_shape.append(jax.ShapeDtypeStruct((b, 2) + hshape, F32))
        out_specs.append(pl.BlockSpec((None, 2) + hshape, lambda bi, c: (bi, 0, 0, 0, 0)))
    elif want_hfin:
        out_shape.append(jax.ShapeDtypeStruct((b,) + hshape, F32))
        out_specs.append(pl.BlockSpec((None,) + hshape, lambda bi, c: (bi, 0, 0, 0)))
    return pl.pallas_call(
        functools.partial(_ssd_kernel, nc=nc, reverse=reverse, has_h0=has_h0, want_hfin=want_hfin),
        out_shape=tuple(out_shape),
        grid=(b, nc),
        in_specs=in_specs,
        out_specs=tuple(out_specs),
        scratch_shapes=[pltpu.VMEM(hshape, F32)],
        compiler_params=_cparams(("arbitrary", "arbitrary")),
        name="ssd_bwd" if reverse else "ssd_fwd",
    )(*args)


def _ssd_call(xbc, dt, wts, h0, *, want_hfin):
    res = _ssd_sweep(xbc, dt, wts, h0, None, reverse=False, want_hfin=want_hfin)
    y_f, xc = res[0], res[1]
    hf = res[2] if want_hfin else None
    res = _ssd_sweep(xc, dt, wts, h0, hf, reverse=True, want_hfin=want_hfin)
    return y_f, res[0], (res[1] if want_hfin else None)


def _route(logits_t, bias_col):
    e, n = logits_t.shape
    per = e // N_EXPERT_GROUPS
    scores = jax.nn.sigmoid(logits_t)
    sel = scores + bias_col
    neg = jnp.float32(-jnp.inf)
    gs = []
    for g in range(N_EXPERT_GROUPS):
        blk = sel[g * per:(g + 1) * per, :]
        m1 = jnp.max(blk, axis=0, keepdims=True)
        is_m1 = blk == m1
        cnt = jnp.sum(jnp.where(is_m1, 1.0, 0.0), axis=0, keepdims=True)
        m2 = jnp.max(jnp.where(is_m1, neg, blk), axis=0, keepdims=True)
        gs.append(m1 + jnp.where(cnt >= 2.0, m1, m2))
    keep = []
    for g in range(N_EXPERT_GROUPS):
        rank = jnp.zeros_like(gs[g])
        for j in range(N_EXPERT_GROUPS):
            if j == g:
                continue
            beats = (gs[j] > gs[g]) if j > g else (gs[j] >= gs[g])
            rank = rank + jnp.where(beats, 1.0, 0.0)
        keep.append(rank < float(TOPK_GROUPS))
    selm = jnp.concatenate(
        [jnp.where(keep[g], sel[g * per:(g + 1) * per, :], neg) for g in range(N_EXPERT_GROUPS)], axis=0)
    eidx = lax.broadcasted_iota(jnp.int32, (e, n), 0).astype(F32)
    cur = selm
    picks = []
    for _ in range(TOP_K):
        m = jnp.max(cur, axis=0, keepdims=True)
        idx = jnp.min(jnp.where(cur == m, eidx, float(e)), axis=0, keepdims=True)
        hit = eidx == idx
        picks.append((idx, hit))
        cur = jnp.where(hit, neg, cur)
    return scores, picks


def _pack_bf16_pairs(h):
    c = h.shape[1] // 2
    lo = pltpu.bitcast(h[:, :c].astype(BF16).astype(F32), jnp.uint32)
    hi = pltpu.bitcast(h[:, c:].astype(BF16).astype(F32), jnp.uint32)
    return (lo >> 16) | (hi & jnp.uint32(0xFFFF0000))


def _unpack_bf16_pairs(w):
    lo = pltpu.bitcast(w << 16, F32)
    hi = pltpu.bitcast(w & jnp.uint32(0xFFFF0000), F32)
    return lo, hi


def _rows8(rows):
    n = rows[0].shape[1]
    ridx = lax.broadcasted_iota(jnp.int32, (TOP_K, n), 0)
    out = jnp.zeros((TOP_K, n), rows[0].dtype)
    for k, r in enumerate(rows):
        out = jnp.where(ridx == k, r, out)
    return out


MERGE_SUB = 512


def _merge_kernel(x_ref, attn_ref, yf_ref, yb_ref, z_ref, gates_ref, mod_ref, wa_ref, ws_ref, wo_ref,
                  sg_ref, n2_ref, wr_ref, rb_ref, wsg_ref, wsu_ref, wsd_ref,
                  xb_ref, ha_ref, hb_ref, eid_ref, pos_ref, wk_ref, cnt_ref):
    tm = x_ref.shape[0]
    sub = MERGE_SUB

    @pl.when(pl.program_id(0) == 0)
    def _():
        cnt_ref[...] = jnp.zeros_like(cnt_ref)

    r_i = lax.broadcasted_iota(jnp.int32, (sub, sub), 0)
    c_i = lax.broadcasted_iota(jnp.int32, (sub, sub), 1)
    before = jnp.where(r_i < c_i, 1.0, 0.0).astype(BF16)
    cnt = cnt_ref[:, 0:1]

    for r0 in range(0, tm, sub):
        rs = slice(r0, r0 + sub)
        x = x_ref[rs, :]
        yy = yf_ref[rs, :].astype(F32) + yb_ref[rs, :].astype(F32)
        u = yy * _silu(z_ref[rs, :]).astype(F32)
        un = u * lax.rsqrt(jnp.mean(u * u, axis=-1, keepdims=True) + EPS) * sg_ref[...]
        ssd_o = _bdot(un, ws_ref[...])
        attn_o = jnp.dot(attn_ref[rs, :], wa_ref[...], preferred_element_type=F32)
        ga = jax.nn.sigmoid(gates_ref[rs, 0:D_MODEL]).astype(F32)
        gs = jax.nn.sigmoid(gates_ref[rs, D_MODEL:2 * D_MODEL]).astype(F32)
        mix = _bdot(ga * attn_o + gs * ssd_o, wo_ref[...])
        x1 = x + _mod(mod_ref, MOD_GATE1) * mix
        h2 = x1 * lax.rsqrt(jnp.mean(x1 * x1, axis=-1, keepdims=True) + EPS) * n2_ref[...]
        h2 = h2 * (1.0 + _mod(mod_ref, MOD_SCALE2)) + _mod(mod_ref, MOD_SHIFT2)
        h2b = h2.astype(BF16)
        ha_ref[rs, :] = _pack_bf16_pairs(h2[:, :D_MODEL // 2])
        hb_ref[rs, :] = _pack_bf16_pairs(h2[:, D_MODEL // 2:])

        logits_t = _bdot_nt(wr_ref[...], h2b)
        scores, picks = _route(logits_t, rb_ref[...])
        chosen = jnp.zeros_like(scores)
        for _, hit in picks:
            chosen = chosen + jnp.where(hit, 1.0, 0.0)
        pos = cnt + jnp.dot(chosen.astype(BF16), before, preferred_element_type=F32)
        cnt = cnt + jnp.sum(chosen, axis=1, keepdims=True)
        poss = [jnp.sum(jnp.where(hit, pos, 0.0), axis=0, keepdims=True) for _, hit in picks]
        wks = [jnp.sum(jnp.where(hit, scores, 0.0), axis=0, keepdims=True) for _, hit in picks]
        wsum = wks[0]
        for w in wks[1:]:
            wsum = wsum + w
        eid_ref[:, rs] = _rows8([idx for idx, _ in picks]).astype(jnp.int32)
        pos_ref[:, rs] = _rows8(poss).astype(jnp.int32)
        wk8 = _rows8(wks) / wsum * ROUTED_SCALE
        wk_ref[rs, :] = jnp.concatenate([wk8, jnp.zeros((LANES - TOP_K, sub), F32)], axis=0).T

        hid = _silu(jnp.dot(h2b, wsg_ref[...], preferred_element_type=F32)) * \
            jnp.dot(h2b, wsu_ref[...], preferred_element_type=F32)
        xb_ref[rs, :] = x1 + _mod(mod_ref, MOD_GATE2) * _bdot(hid, wsd_ref[...])

    cnt_ref[...] = jnp.broadcast_to(cnt, cnt_ref.shape)


def _merge_call(x, attn, y_f, y_b, z, gates, mod_rows, wa, ws, wo, sg, n2, wr_t, rb, wsg, wsu, wsd, *, tm):
    b, t, _ = x.shape
    n = b * t
    flat = lambda a: a.reshape(n, a.shape[-1])
    tok = lambda width: pl.BlockSpec((tm, width), lambda i: (i, 0))
    const2 = lambda shape: pl.BlockSpec(shape, lambda i: (0, 0), pipeline_mode=pl.Buffered(1))
    k8 = pl.BlockSpec((TOP_K, tm), lambda i: (0, i))
    half = D_MODEL // 4
    return pl.pallas_call(
        _merge_kernel,
        out_shape=(jax.ShapeDtypeStruct((n, D_MODEL), F32),
                   jax.ShapeDtypeStruct((n, half), jnp.uint32),
                   jax.ShapeDtypeStruct((n, half), jnp.uint32),
                   jax.ShapeDtypeStruct((TOP_K, n), jnp.int32),
                   jax.ShapeDtypeStruct((TOP_K, n), jnp.int32),
                   jax.ShapeDtypeStruct((n, LANES), F32),
                   jax.ShapeDtypeStruct((N_EXPERTS, LANES), F32)),
        grid=(n // tm,),
        in_specs=[tok(D_MODEL), tok(ATTN_W), tok(D_INNER), tok(D_INNER), tok(D_INNER), tok(2 * D_MODEL),
                  pl.BlockSpec((None, 1, 6 * D_MODEL), lambda i: ((i * tm) // t, 0, 0)),
                  const2((ATTN_W, D_MODEL)), const2((D_INNER, D_MODEL)), const2((D_MODEL, D_MODEL)),
                  const2((1, D_INNER)), const2((1, D_MODEL)),
                  const2((N_EXPERTS, D_MODEL)), const2((N_EXPERTS, 1)),
                  const2((D_MODEL, D_SHARED)), const2((D_MODEL, D_SHARED)), const2((D_SHARED, D_MODEL))],
        out_specs=(tok(D_MODEL), tok(half), tok(half), k8, k8, tok(LANES),
                   pl.BlockSpec((N_EXPERTS, LANES), lambda i: (0, 0))),
        compiler_params=_cparams(("arbitrary",)),
        name="merge",
    )(flat(x), flat(attn), flat(y_f), flat(y_b), flat(z), flat(gates), mod_rows, wa, ws, wo, sg, n2, wr_t, rb,
      wsg, wsu, wsd)


ROW_TILE = 512
SC_WINDOW = 128


def _slots_kernel(start_ref, eid_ref, pos_ref, slot_ref):
    eid = eid_ref[...]
    slot = pos_ref[...]
    for e in range(N_EXPERTS):
        slot = slot + jnp.where(eid == e, start_ref[e], 0)
    slot_ref[...] = slot


def _slots_call(start, eid, pos):
    n = eid.shape[1]
    bn = 2048 if n % 2048 == 0 else n
    spec = pl.BlockSpec((TOP_K, bn), lambda i, s: (0, i))
    return pl.pallas_call(
        _slots_kernel,
        out_shape=jax.ShapeDtypeStruct((TOP_K, n), jnp.int32),
        grid_spec=pltpu.PrefetchScalarGridSpec(num_scalar_prefetch=1, grid=(n // bn,),
                                               in_specs=[spec, spec], out_specs=spec),
        compiler_params=_cparams(("arbitrary",)),
        name="slots",
    )(start, eid, pos)


def _sc_dispatch(x, slots, p):
    n, d = x.shape
    mesh = plsc.VectorSubcoreMesh(core_axis_name="core", subcore_axis_name="subcore")

    @functools.partial(pl.kernel, out_type=jax.ShapeDtypeStruct((p, d), x.dtype), mesh=mesh)
    def k(x_hbm, s_hbm, o_hbm):
        def body(x_vmem, s_vmem):
            for kk in range(TOP_K):
                pltpu.sync_copy(x_vmem, o_hbm.at[s_vmem.at[kk]])

        pltpu.emit_pipeline(
            body,
            grid=(n // SC_WINDOW,),
            in_specs=[pl.BlockSpec((SC_WINDOW, d), index_map=lambda i: (i, 0)),
                      pl.BlockSpec((TOP_K, SC_WINDOW), index_map=lambda i: (0, i))],
            out_specs=[],
            core_axis_name=("core", "subcore"),
            dimension_semantics=(pltpu.PARALLEL,),
        )(x_hbm, s_hbm)

    return k(x, slots)


def _sc_combine(y, slots):
    kk, n = slots.shape
    d = y.shape[1]
    mesh = plsc.VectorSubcoreMesh(core_axis_name="core", subcore_axis_name="subcore")

    @functools.partial(pl.kernel, out_type=jax.ShapeDtypeStruct((kk * n, d), y.dtype), mesh=mesh)
    def k(y_hbm, s_hbm, o_hbm):
        def body(s_vmem, o_vmem):
            pltpu.sync_copy(y_hbm.at[s_vmem.at[0]], o_vmem)

        pltpu.emit_pipeline(
            body,
            grid=(kk * n // SC_WINDOW,),
            in_specs=[pl.BlockSpec((1, SC_WINDOW), index_map=lambda i: (0, i))],
            out_specs=[pl.BlockSpec((SC_WINDOW, d), index_map=lambda i: (i, 0))],
            core_axis_name=("core", "subcore"),
            dimension_semantics=(pltpu.PARALLEL,),
        )(s_hbm, o_hbm)

    return k(y, slots.reshape(1, kk * n)).reshape(kk, n, d)


FFN_IN_BUFS = 4
FFN_OUT_BUFS = 3


def _ffn_kernel(st_ref, nt_ref, wg_ref, wu_ref, wd_ref, xa_hbm, xb_hbm, ya_hbm, yb_hbm,
                wg_s, wu_s, wd_s, xa_buf, xb_buf, ya_buf, yb_buf, in_sem, out_sem):
    e = pl.program_id(0)
    ne = pl.num_programs(0)
    n = nt_ref[e]
    t = ROW_TILE
    ahead = FFN_IN_BUFS - 1

    def fetch(row, slot):
        r = pl.multiple_of(row, t)
        return (pltpu.make_async_copy(xa_hbm.at[pl.ds(r, t)], xa_buf.at[slot], in_sem.at[0, slot]),
                pltpu.make_async_copy(xb_hbm.at[pl.ds(r, t)], xb_buf.at[slot], in_sem.at[1, slot]))

    def put(row, slot):
        r = pl.multiple_of(row, t)
        return (pltpu.make_async_copy(ya_buf.at[slot], ya_hbm.at[pl.ds(r, t)], out_sem.at[0, slot]),
                pltpu.make_async_copy(yb_buf.at[slot], yb_hbm.at[pl.ds(r, t)], out_sem.at[1, slot]))

    def start_head(expert):
        for k in range(ahead):
            @pl.when(k < nt_ref[expert])
            def _():
                for c in fetch(st_ref[expert] + k * t, k):
                    c.start()

    @pl.when(e == 0)
    def _():
        start_head(0)

    @pl.when(n > 0)
    def _():
        wg_s[...] = wg_ref[...].astype(BF16)
        wu_s[...] = wu_ref[...].astype(BF16)
        wd_s[...] = wd_ref[...].astype(BF16)
        base = st_ref[e]

        def body(i, carry):
            si = lax.rem(i, FFN_IN_BUFS)
            so = lax.rem(i, FFN_OUT_BUFS)
            row = base + i * t
            for c in fetch(row, si):
                c.wait()

            @pl.when(i + ahead < n)
            def _():
                for c in fetch(row + ahead * t, lax.rem(i + ahead, FFN_IN_BUFS)):
                    c.start()

            @pl.when(i >= FFN_OUT_BUFS)
            def _():
                for c in put(row - FFN_OUT_BUFS * t, so):
                    c.wait()

            parts = _unpack_bf16_pairs(xa_buf[si]) + _unpack_bf16_pairs(xb_buf[si])
            x = jnp.concatenate(parts, axis=1).astype(BF16)
            hid = _silu(jnp.dot(x, wg_s[...], preferred_element_type=F32)) * \
                jnp.dot(x, wu_s[...], preferred_element_type=F32)
            y = _bdot(hid, wd_s[...])
            ya_buf[so] = _pack_bf16_pairs(y[:, :D_MODEL // 2])
            yb_buf[so] = _pack_bf16_pairs(y[:, D_MODEL // 2:])
            for c in put(row, so):
                c.start()
            return carry

        lax.fori_loop(0, n, body, 0)

        for k in range(FFN_OUT_BUFS):
            @pl.when(n - 1 - k >= 0)
            def _():
                j = n - 1 - k
                for c in put(base + j * t, lax.rem(j, FFN_OUT_BUFS)):
                    c.wait()

    @pl.when(e + 1 < ne)
    def _():
        start_head(jnp.minimum(e + 1, ne - 1))


def _ffn_call(start, tiles_e, xa, xb, wg, wu, wd):
    p, half = xa.shape
    wspec = lambda s: pl.BlockSpec((None,) + s, lambda e, st, nt: (e, 0, 0))
    hbm = pl.BlockSpec(memory_space=pl.ANY)
    ibuf = pltpu.VMEM((FFN_IN_BUFS, ROW_TILE, half), jnp.uint32)
    obuf = pltpu.VMEM((FFN_OUT_BUFS, ROW_TILE, half), jnp.uint32)
    return pl.pallas_call(
        _ffn_kernel,
        out_shape=(jax.ShapeDtypeStruct((p, half), jnp.uint32), jax.ShapeDtypeStruct((p, half), jnp.uint32)),
        grid_spec=pltpu.PrefetchScalarGridSpec(
            num_scalar_prefetch=2, grid=(N_EXPERTS,),
            in_specs=[wspec((D_MODEL, D_EXPERT)), wspec((D_MODEL, D_EXPERT)), wspec((D_EXPERT, D_MODEL)),
                      hbm, hbm],
            out_specs=(hbm, hbm),
            scratch_shapes=[pltpu.VMEM((D_MODEL, D_EXPERT), BF16), pltpu.VMEM((D_MODEL, D_EXPERT), BF16),
                            pltpu.VMEM((D_EXPERT, D_MODEL), BF16), ibuf, ibuf, obuf, obuf,
                            pltpu.SemaphoreType.DMA((2, FFN_IN_BUFS)),
                            pltpu.SemaphoreType.DMA((2, FFN_OUT_BUFS))]),
        compiler_params=_cparams(("arbitrary",)),
        name="ffn",
    )(start, tiles_e, wg, wu, wd, xa, xb)


def _final_kernel(xb_ref, ya_ref, yb_ref, wk_ref, mod_ref, o_ref):
    q = D_MODEL // 4
    accs = [jnp.zeros((xb_ref.shape[0], q), F32) for _ in range(4)]
    for k in range(TOP_K):
        w = wk_ref[:, k:k + 1]
        parts = _unpack_bf16_pairs(ya_ref[k]) + _unpack_bf16_pairs(yb_ref[k])
        accs = [a + w * p for a, p in zip(accs, parts)]
    for i, a in enumerate(accs):
        o_ref[:, i * q:(i + 1) * q] = xb_ref[:, i * q:(i + 1) * q] + _mod(mod_ref, MOD_GATE2, i * q, (i + 1) * q) * a


def _final_call(xb, ya, yb, wk, mod_rows, t, *, tm):
    n = xb.shape[0]
    half = ya.shape[2]
    tok = lambda width: pl.BlockSpec((tm, width), lambda i: (i, 0))
    yspec = pl.BlockSpec((TOP_K, tm, half), lambda i: (0, i, 0))
    return pl.pallas_call(
        _final_kernel,
        out_shape=jax.ShapeDtypeStruct((n, D_MODEL), F32),
        grid=(n // tm,),
        in_specs=[tok(D_MODEL), yspec, yspec, tok(LANES),
                  pl.BlockSpec((None, 1, 6 * D_MODEL), lambda i: ((i * tm) // t, 0, 0))],
        out_specs=tok(D_MODEL),
        compiler_params=_cparams(("arbitrary",)),
        name="final",
    )(xb, ya, yb, wk, mod_rows)


def _moe_call(ha, hb, eid, pos, wk, counts, xb, mod_rows, wg, wu, wd, t):
    n = xb.shape[0]
    max_tiles = n * TOP_K // ROW_TILE + N_EXPERTS
    p = max_tiles * ROW_TILE
    cnt = counts[:, 0].astype(jnp.int32)
    tiles_e = (cnt + ROW_TILE - 1) // ROW_TILE
    ends = jnp.cumsum(tiles_e)
    start = (ends - tiles_e) * ROW_TILE
    slots = _slots_call(start.astype(jnp.int32), eid, pos)
    xa = _sc_dispatch(ha, slots, p)
    xbb = _sc_dispatch(hb, slots, p)
    ya, yb = _ffn_call(start.astype(jnp.int32), tiles_e.astype(jnp.int32), xa, xbb, wg, wu, wd)
    ga = _sc_combine(ya, slots)
    gb = _sc_combine(yb, slots)
    return _final_call(xb, ga, gb, wk, mod_rows, t, tm=256)


def _rope_tables(t):
    n_rows = t // GRID_W
    rows = jnp.repeat(jnp.arange(n_rows), GRID_W).astype(F32)
    cols = jnp.tile(jnp.arange(GRID_W), n_rows).astype(F32)
    n_freq = HEAD_DIM // 4
    freqs = ROPE_THETA ** (-jnp.arange(n_freq, dtype=F32) / n_freq)
    ang = jnp.concatenate([rows[:, None] * freqs, cols[:, None] * freqs], axis=-1)
    ang = jnp.repeat(ang, 2, axis=-1)
    ang = jnp.concatenate([ang, ang], axis=-1)
    sign = jnp.where(jnp.arange(LANES) % 2 == 0, -1.0, 1.0).astype(F32)
    return jnp.cos(ang), jnp.sin(ang) * sign


def _dup_heads(a):
    s = a.shape[:-1]
    a4 = a.reshape(s + (N_KV_HEADS, HEAD_DIM))
    return jnp.concatenate([a4, a4], axis=-1).reshape(s + (KVD_W,))


def _prep_w_in(w_in):
    pad = jnp.zeros((D_MODEL, LANES - SSD_HEADS), w_in.dtype)
    dt0 = M_END
    g0 = dt0 + 2 * SSD_HEADS
    tail = [w_in[:, g0:g0 + 2 * D_MODEL], w_in[:, dt0:dt0 + SSD_HEADS], pad,
            w_in[:, dt0 + SSD_HEADS:g0], pad]
    return w_in[:, :M_END].astype(BF16), jnp.concatenate(tail, axis=-1).astype(BF16)


def _pad_heads(a):
    return jnp.pad(a.astype(F32), ((0, 0), (0, LANES - SSD_HEADS)))[:, None, :]


def _trunk(x, mod_rows, wts, rope_tabs, ctx_k, ctx_v, h0, *, tm, tq, want_state):
    b, t, _ = x.shape
    rope = rope_tabs is not None
    if rope:
        cos, sin = rope_tabs
    else:
        cos = sin = jnp.zeros((t, LANES), F32)
    res = _inproj_call(x, mod_rows, wts["g1"], *wts["w_in"], wts["qg"], wts["kg"], cos, sin,
                       rope=rope, emit_kv=want_state, tm=tm)
    q, k, v, gates, z, xbc, dt = res[:7]
    kv_raw = res[7:]
    attn = _attn_call(q, k, v, ctx_k, ctx_v, tq=tq)
    y_f, y_b, hfin = _ssd_call(xbc, dt, wts, h0, want_hfin=want_state)
    xb, ha, hb, eid, pos, wk, counts = _merge_call(
        x, attn, y_f, y_b, z, gates, mod_rows, wts["wa"], wts["ws"], wts["wo"], wts["sg"],
        wts["n2"], wts["wr_t"], wts["rb"], wts["wsg"], wts["wsu"], wts["wsd"], tm=MERGE_SUB)
    out = _moe_call(ha, hb, eid, pos, wk, counts, xb, mod_rows, wts["weg"], wts["weu"], wts["wed"], t)
    return out.reshape(b, t, D_MODEL), kv_raw, hfin


def kernel(x_prompt, x_sample, cache_k, cache_v, state_ssm, c, c_ctx, w_mod, b_mod, norm1_g, norm2_g, w_in,
           q_norm_g, k_norm_g, conv_w, conv_b, a_log, dt_bias, d_skip, ssd_norm_g, w_attn_proj, w_ssd_proj,
           w_out, w_router, router_bias, w_exp_gate, w_exp_up, w_exp_down, w_sh_gate, w_sh_up, w_sh_down):
    depth = w_mod.shape[0]
    assert depth == 1, "single trunk layer"
    bp, tp, _ = x_prompt.shape
    bs, ts, _ = x_sample.shape
    l = 0
    cvec = jnp.concatenate([c_ctx[None, :], c, jnp.zeros((8 - 1 - bs, D_MODEL), F32)], axis=0)
    mod = _mod_call(cvec, w_mod.reshape(D_MODEL, 6 * D_MODEL), b_mod.reshape(1, 6 * D_MODEL))
    mod_prompt = jnp.broadcast_to(mod[0:1][:, None, :], (bp, 1, 6 * D_MODEL))
    mod_sample = mod[1:1 + bs][:, None, :]

    lower = np.tril(np.ones((CHUNK, CHUNK), np.float32))
    head_sel = (np.arange(LANES)[:, None] == np.arange(D_INNER)[None, :] // SSD_HEAD_DIM).astype(np.float32)
    wts = dict(
        g1=norm1_g[l][None, :], n2=norm2_g[l][None, :],
        w_in=_prep_w_in(w_in.reshape(D_MODEL, w_in.shape[-1])),
        qg=jnp.tile(q_norm_g[l], 2)[None, :], kg=jnp.tile(k_norm_g[l], 2)[None, :],
        conv_w=conv_w[l], conv_b=conv_b[l][None, :],
        a_neg=_pad_heads(-jnp.exp(a_log[l].astype(F32))), dt_bias=_pad_heads(dt_bias[l]),
        dskip=jnp.repeat(d_skip[l].astype(F32), SSD_HEAD_DIM)[None, :],
        tri=jnp.asarray(np.stack([lower, lower.T])),
        head_sel=jnp.asarray(head_sel, BF16),
        sg=ssd_norm_g[l][None, :],
        wa=w_attn_proj[l].astype(BF16), ws=w_ssd_proj[l].astype(BF16), wo=w_out[l].astype(BF16),
        wr_t=w_router[l].T.astype(BF16), rb=router_bias[l].astype(F32)[:, None],
        wsg=w_sh_gate[l].astype(BF16), wsu=w_sh_up[l].astype(BF16), wsd=w_sh_down[l].astype(BF16),
        weg=w_exp_gate.reshape(w_exp_gate.shape[1:]), weu=w_exp_up.reshape(w_exp_up.shape[1:]),
        wed=w_exp_down.reshape(w_exp_down.shape[1:]),
    )

    y_prompt, (k_p, v_p), hfin = _trunk(x_prompt, mod_prompt, wts, None, None, None, None,
                                        tm=256, tq=256, want_state=True)
    new_k = k_p.reshape(bp, 1, tp, N_KV_HEADS, HEAD_DIM)
    new_v = v_p.reshape(bp, 1, tp, N_KV_HEADS, HEAD_DIM)
    new_state = hfin.reshape(bp, 1, 2, SSD_HEADS, SSD_HEAD_DIM, D_STATE)

    past = cache_k.shape[2]
    ctx_k = _dup_heads(cache_k[:, l].reshape(bs, past, KV_W)).astype(BF16)
    ctx_v = _dup_heads(cache_v[:, l].reshape(bs, past, KV_W)).astype(BF16)
    h0 = state_ssm[:, l].reshape(bs, 2, SSD_HEADS // 2, 2 * SSD_HEAD_DIM, D_STATE)
    y_sample, _, _ = _trunk(x_sample, mod_sample, wts, _rope_tables(ts), ctx_k, ctx_v, h0,
                               tm=512, tq=256, want_state=False)
    return (y_prompt, y_sample, new_k, new_v, new_state)
```

```python
import functools

import numpy as np
import jax
import jax.numpy as jnp
from jax import lax
from jax.experimental import pallas as pl
from jax.experimental.pallas import tpu as pltpu
from jax.experimental.pallas import tpu_sc as plsc

F32 = jnp.float32
BF16 = jnp.bfloat16

D_MODEL = 1024
GRID_W = 64
EPS = 1e-6
N_HEADS = 16
N_KV_HEADS = 4
HEAD_DIM = 64
ATTN_W = N_HEADS * HEAD_DIM
KV_W = N_KV_HEADS * HEAD_DIM
ROPE_THETA = 10000.0
D_INNER = 2048
SSD_HEAD_DIM = 64
SSD_HEADS = 32
SSD_GROUPS = 4
D_STATE = 128
D_CONV = 4
CHUNK = 128
CONV_CH = D_INNER + 2 * SSD_GROUPS * D_STATE
N_EXPERTS = 64
TOP_K = 8
N_EXPERT_GROUPS = 8
TOPK_GROUPS = 4
D_EXPERT = 256
D_SHARED = 256
ROUTED_SCALE = 2.5

LANES = 128
KVD_W = N_KV_HEADS * LANES
M_Q, M_K, M_V, M_Z, M_X, M_END = (int(c) for c in np.cumsum((0, ATTN_W, KV_W, KV_W, D_INNER, CONV_CH)))
T_G, T_DT, T_END = (int(c) for c in np.cumsum((0, 2 * D_MODEL, 2 * LANES)))
MOD_SHIFT1, MOD_SCALE1, MOD_GATE1, MOD_SHIFT2, MOD_SCALE2, MOD_GATE2 = range(6)
VMEM_LIMIT = 56 * 1024 * 1024
Q_SCALE = HEAD_DIM ** -0.5 * 1.4426950408889634


def _cparams(sem):
    return pltpu.CompilerParams(dimension_semantics=sem, vmem_limit_bytes=VMEM_LIMIT)


def _mod(mod_ref, which, lo=0, hi=D_MODEL):
    return mod_ref[:, which * D_MODEL + lo:which * D_MODEL + hi]


def _silu(x):
    return x * jax.nn.sigmoid(x)


def _bdot(a, b):
    return jnp.dot(a.astype(BF16), b.astype(BF16), preferred_element_type=F32)


def _bdot_nt(a, b):
    return lax.dot_general(a.astype(BF16), b.astype(BF16), (((1,), (1,)), ((), ())),
                           preferred_element_type=F32)


def _mod_kernel(c_ref, w_ref, b_ref, o_ref):
    o_ref[...] = _bdot(_silu(c_ref[...]), w_ref[...]) + b_ref[...]


def _mod_call(cvec, w_mod, b_mod):
    n = w_mod.shape[1]
    bn = 1024
    return pl.pallas_call(
        _mod_kernel,
        out_shape=jax.ShapeDtypeStruct((8, n), F32),
        grid=(n // bn,),
        in_specs=[pl.BlockSpec((8, D_MODEL), lambda j: (0, 0)),
                  pl.BlockSpec((D_MODEL, bn), lambda j: (0, j)),
                  pl.BlockSpec((1, bn), lambda j: (0, j))],
        out_specs=pl.BlockSpec((8, bn), lambda j: (0, j)),
        compiler_params=_cparams(("arbitrary",)),
        name="mod",
    )(cvec, w_mod, b_mod)


def _inproj_kernel(*refs, rope, emit_kv):
    if emit_kv:
        (x_ref, mod_ref, g1_ref, wm_ref, wt_ref, qg_ref, kg_ref, cos_ref, sin_ref,
         q_ref, k_ref, v_ref, gates_ref, z_ref, xbc_ref, dt_ref, kraw_ref, vraw_ref) = refs
    else:
        (x_ref, mod_ref, g1_ref, wm_ref, wt_ref, qg_ref, kg_ref, cos_ref, sin_ref,
         q_ref, k_ref, v_ref, gates_ref, z_ref, xbc_ref, dt_ref) = refs
    tm = x_ref.shape[0]
    x = x_ref[...]
    inv = lax.rsqrt(jnp.mean(x * x, axis=-1, keepdims=True) + EPS)
    h = (x * inv) * g1_ref[...]
    h = h * (1.0 + _mod(mod_ref, MOD_SCALE1)) + _mod(mod_ref, MOD_SHIFT1)
    hb = h.astype(BF16)

    lane = lax.broadcasted_iota(jnp.int32, (tm, LANES), 1)
    lo = lane < HEAD_DIM
    even = (lane & 1) == 0
    if rope:
        cos = cos_ref[...]
        sin = sin_ref[...]

    def rope_fn(blk):
        nxt = pltpu.roll(blk, LANES - 1, 1)
        prv = pltpu.roll(blk, 1, 1)
        return blk * cos + jnp.where(even, nxt, prv) * sin

    def head_norm(blk, g):
        sq = blk * blk
        s_all = jnp.sum(sq, axis=-1, keepdims=True)
        s_lo = jnp.sum(jnp.where(lo, sq, 0.0), axis=-1, keepdims=True)
        ms = jnp.where(lo, s_lo, s_all - s_lo) * (1.0 / HEAD_DIM)
        return blk * lax.rsqrt(ms + EPS) * g

    def dup_heads(blk):
        sw = pltpu.roll(blk, HEAD_DIM, 1)
        return jnp.where(lo, blk, sw), jnp.where(lo, sw, blk)

    qg = qg_ref[...]
    kg = kg_ref[...]
    q = jnp.dot(hb, wm_ref[:, M_Q:M_K], preferred_element_type=F32)
    for j in range(ATTN_W // LANES):
        blk = head_norm(q[:, j * LANES:(j + 1) * LANES], qg)
        if rope:
            blk = rope_fn(blk)
        q_ref[:, j * LANES:(j + 1) * LANES] = (blk * Q_SCALE).astype(q_ref.dtype)

    k = jnp.dot(hb, wm_ref[:, M_K:M_V], preferred_element_type=F32)
    v = jnp.dot(hb, wm_ref[:, M_V:M_Z], preferred_element_type=F32)
    for j in range(KV_W // LANES):
        kb = head_norm(k[:, j * LANES:(j + 1) * LANES], kg)
        vb = v[:, j * LANES:(j + 1) * LANES]
        if emit_kv:
            kraw_ref[:, j * LANES:(j + 1) * LANES] = kb
            vraw_ref[:, j * LANES:(j + 1) * LANES] = vb
        if rope:
            kb = rope_fn(kb)
        for i, (kd, vd) in enumerate(zip(dup_heads(kb), dup_heads(vb))):
            c0 = (2 * j + i) * LANES
            k_ref[:, c0:c0 + LANES] = kd.astype(k_ref.dtype)
            v_ref[:, c0:c0 + LANES] = vd.astype(v_ref.dtype)

    gates_ref[...] = jnp.dot(hb, wt_ref[:, T_G:T_DT], preferred_element_type=F32).astype(gates_ref.dtype)
    z_ref[...] = jnp.dot(hb, wm_ref[:, M_Z:M_X], preferred_element_type=F32).astype(z_ref.dtype)
    xbc_ref[...] = jnp.dot(hb, wm_ref[:, M_X:M_END], preferred_element_type=F32).astype(xbc_ref.dtype)
    dt_ref[...] = jnp.dot(hb, wt_ref[:, T_DT:T_END], preferred_element_type=F32)


def _inproj_call(x, mod_rows, g1, w_main, w_tail, qg, kg, cos, sin, *, rope, emit_kv, tm):
    b, t, _ = x.shape
    nt = t // tm
    tok = lambda width: pl.BlockSpec((None, tm, width), lambda bi, i: (bi, i, 0))
    const2 = lambda shape: pl.BlockSpec(shape, lambda bi, i: (0, 0))
    out_shape = [
        jax.ShapeDtypeStruct((b, t, ATTN_W), BF16),
        jax.ShapeDtypeStruct((b, t, KVD_W), BF16),
        jax.ShapeDtypeStruct((b, t, KVD_W), BF16),
        jax.ShapeDtypeStruct((b, t, 2 * D_MODEL), BF16),
        jax.ShapeDtypeStruct((b, t, D_INNER), BF16),
        jax.ShapeDtypeStruct((b, t, CONV_CH), BF16),
        jax.ShapeDtypeStruct((b, t, 2 * LANES), F32),
    ]
    out_specs = [tok(ATTN_W), tok(KVD_W), tok(KVD_W), tok(2 * D_MODEL), tok(D_INNER), tok(CONV_CH),
                 tok(2 * LANES)]
    if emit_kv:
        out_shape += [jax.ShapeDtypeStruct((b, t, KV_W), F32)] * 2
        out_specs += [tok(KV_W), tok(KV_W)]
    return pl.pallas_call(
        functools.partial(_inproj_kernel, rope=rope, emit_kv=emit_kv),
        out_shape=tuple(out_shape),
        grid=(b, nt),
        in_specs=[tok(D_MODEL),
                  pl.BlockSpec((None, 1, 6 * D_MODEL), lambda bi, i: (bi, 0, 0)),
                  const2((1, D_MODEL)),
                  pl.BlockSpec((D_MODEL, M_END), lambda bi, i: (0, 0), pipeline_mode=pl.Buffered(1)),
                  pl.BlockSpec((D_MODEL, T_END), lambda bi, i: (0, 0), pipeline_mode=pl.Buffered(1)),
                  const2((1, LANES)), const2((1, LANES)),
                  pl.BlockSpec((tm, LANES), lambda bi, i: (i, 0)),
                  pl.BlockSpec((tm, LANES), lambda bi, i: (i, 0))],
        out_specs=tuple(out_specs),
        compiler_params=_cparams(("arbitrary", "arbitrary")),
        name="inproj",
    )(x, mod_rows, g1, w_main, w_tail, qg, kg, cos, sin)


KEY_CHUNK = 512


def _key_chunk(n):
    return KEY_CHUNK if n % KEY_CHUNK == 0 else n


def _attn_kernel(*refs, has_ctx):
    if has_ctx:
        q_ref, k_ref, v_ref, kctx_ref, vctx_ref, o_ref = refs
        sources = ((k_ref, v_ref), (kctx_ref, vctx_ref))
    else:
        q_ref, k_ref, v_ref, o_ref = refs
        sources = ((k_ref, v_ref),)
    tq = q_ref.shape[0]
    lane = lax.broadcasted_iota(jnp.int32, (tq, LANES), 1)
    lo = lane < HEAD_DIM
    qs = []
    for j in range(2):
        q2 = q_ref[:, j * LANES:(j + 1) * LANES]
        zero = jnp.zeros_like(q2)
        qs += [jnp.where(lo, q2, zero), jnp.where(lo, zero, q2)]
    q4 = jnp.concatenate(qs, axis=0)
    rows = 4 * tq
    m = jnp.full((rows, 1), -jnp.inf, F32)
    acc = jnp.zeros((rows, LANES), F32)
    chunks = [(kr, vr, c, _key_chunk(kr.shape[0])) for kr, vr in sources
              for c in range(kr.shape[0] // _key_chunk(kr.shape[0]))]
    for kr, vr, c, kc in chunks:
        kch = kr[c * kc:(c + 1) * kc, :].astype(BF16)
        vch = vr[c * kc:(c + 1) * kc, :].astype(BF16)
        lane_k = lax.broadcasted_iota(jnp.int32, (kc, LANES), 1)
        vch = jnp.where(lane_k < HEAD_DIM, vch, jnp.ones_like(vch))
        s = _bdot_nt(q4, kch)
        m_new = jnp.maximum(m, jnp.max(s, axis=-1, keepdims=True))
        alpha = jnp.exp2(m - m_new)
        p = jnp.exp2((s - m_new).astype(BF16))
        acc = acc * alpha + jnp.dot(p, vch, preferred_element_type=F32)
        m = m_new
    o = acc * (1.0 / pltpu.roll(acc, HEAD_DIM, 1))
    for j in range(2):
        oa = o[(2 * j) * tq:(2 * j + 1) * tq]
        ob = pltpu.roll(o[(2 * j + 1) * tq:(2 * j + 2) * tq], HEAD_DIM, 1)
        o_ref[:, j * LANES:(j + 1) * LANES] = jnp.where(lo, oa, ob).astype(o_ref.dtype)


def _attn_call(q, k, v, kctx, vctx, *, tq):
    b, t, _ = q.shape
    tk = k.shape[1]
    nq = t // tq
    has_ctx = kctx is not None
    kv_spec = lambda n: pl.BlockSpec((None, n, LANES), lambda bi, g, i: (bi, 0, g))
    in_specs = [pl.BlockSpec((None, tq, 2 * LANES), lambda bi, g, i: (bi, i, g)), kv_spec(tk), kv_spec(tk)]
    args = [q, k, v]
    if has_ctx:
        in_specs += [kv_spec(kctx.shape[1]), kv_spec(kctx.shape[1])]
        args += [kctx, vctx]
    return pl.pallas_call(
        functools.partial(_attn_kernel, has_ctx=has_ctx),
        out_shape=jax.ShapeDtypeStruct((b, t, ATTN_W), BF16),
        grid=(b, N_KV_HEADS, nq),
        in_specs=in_specs,
        out_specs=pl.BlockSpec((None, tq, 2 * LANES), lambda bi, g, i: (bi, i, g)),
        compiler_params=_cparams(("arbitrary", "arbitrary", "arbitrary")),
        name="attn",
    )(*args)


LOG2E = 1.4426950408889634


def _softplus(x):
    return jnp.maximum(x, 0.0) + jnp.log(1.0 + jnp.exp(-jnp.abs(x)))


def _ssd_kernel(*refs, nc, reverse, has_h0, want_hfin):
    refs = list(refs)
    conv = not reverse
    if conv:
        xbc_ref, prev_ref, next_ref, cw_ref, cb_ref, dsk_ref = refs[:6]
        refs = refs[6:]
    else:
        xc_ref = refs.pop(0)
    dt_ref, an_ref, dtb_ref, tri_ref, sel_ref = refs[:5]
    refs = refs[5:]
    h0_ref = refs.pop(0) if has_h0 else None
    hprev_ref = refs.pop(0) if (want_hfin and reverse) else None
    y_ref = refs.pop(0)
    xco_ref = refs.pop(0) if conv else None
    hfin_ref = refs.pop(0) if want_hfin else None
    h_scr = refs.pop(0)

    L = CHUNK
    c = pl.program_id(1)
    cidx = (nc - 1 - c) if reverse else c

    @pl.when(c == 0)
    def _():
        if has_h0:
            h_scr[...] = h0_ref[...]
        else:
            h_scr[...] = jnp.zeros_like(h_scr)

    row = lax.broadcasted_iota(jnp.int32, (L, LANES), 0)
    lane = lax.broadcasted_iota(jnp.int32, (L, LANES), 1)
    lo = lane < SSD_HEAD_DIM
    top = row < SSD_HEAD_DIM

    if conv:
        first = cidx == 0
        last = cidx == nc - 1

        def cols(a, w):
            xm = xbc_ref[:, a:a + w].astype(F32)
            rw = lax.broadcasted_iota(jnp.int32, (L, w), 0)
            p6 = jnp.where(first, 0.0, prev_ref[6:7, a:a + w].astype(F32))
            p7 = jnp.where(first, 0.0, prev_ref[7:8, a:a + w].astype(F32))
            n0 = jnp.where(last, 0.0, next_ref[0:1, a:a + w].astype(F32))
            r1 = jnp.where(rw == 0, p7, pltpu.roll(xm, 1, 0))
            r2 = jnp.where(rw == 0, p6, jnp.where(rw == 1, p7, pltpu.roll(xm, 2, 0)))
            rn = jnp.where(rw == L - 1, n0, pltpu.roll(xm, L - 1, 0))
            y = (r2 * cw_ref[0:1, a:a + w] + r1 * cw_ref[1:2, a:a + w] + xm * cw_ref[2:3, a:a + w]
                 + rn * cw_ref[3:4, a:a + w] + cb_ref[:, a:a + w])
            y = _silu(y).astype(BF16)
            xco_ref[:, a:a + w] = y
            return y
    else:
        def cols(a, w):
            return xc_ref[:, a:a + w]

    causal = tri_ref[...] > 0.0
    dt = _softplus(dt_ref[...] + dtb_ref[...])
    la2 = dt * (an_ref[...] * LOG2E)
    acum2 = jnp.dot(tri_ref[...], la2, preferred_element_type=F32, precision=lax.Precision.HIGHEST)
    dt_t = dt.T
    acum2_t = acum2.T
    tot2_t = jnp.sum(la2.T, axis=1, keepdims=True)
    lg_dt_t = jnp.log2(dt_t)
    r_t = lg_dt_t - acum2_t
    w_t = jnp.exp2(lg_dt_t + tot2_t - acum2_t)
    e_acum_x = jnp.dot(jnp.exp2(acum2).astype(BF16), sel_ref[...], preferred_element_type=F32)
    e_tot_t = jnp.exp2(tot2_t)

    for g in range(SSD_GROUPS):
        bgb = cols(D_INNER + g * D_STATE, D_STATE)
        cgb = cols(D_INNER + SSD_GROUPS * D_STATE + g * D_STATE, D_STATE)
        cbm = _bdot_nt(cgb, bgb)
        h_grp = h_scr[4 * g:4 * g + 4]
        yo_grp = _bdot_nt(cgb, h_grp.reshape(4 * LANES, D_STATE))
        xws, cds = [], []
        for pr in range(4):
            hp = g * 4 + pr
            ha, hb = 2 * hp, 2 * hp + 1
            xpb = cols(hp * LANES, LANES)
            zero = jnp.zeros_like(xpb)
            xs = jnp.concatenate([jnp.where(lo, xpb, zero), jnp.where(lo, zero, xpb)], axis=0)
            ms = []
            for hh in (ha, hb):
                e = jnp.exp2(acum2[:, hh:hh + 1] + r_t[hh:hh + 1, :])
                ms.append((cbm * jnp.where(causal, e, 0.0)).astype(BF16))
            y = jnp.dot(jnp.concatenate(ms, axis=1), xs, preferred_element_type=F32)
            y = y + yo_grp[:, pr * LANES:(pr + 1) * LANES] * e_acum_x[:, hp * LANES:(hp + 1) * LANES]
            if conv:
                y = y + dsk_ref[:, hp * LANES:(hp + 1) * LANES] * xpb.astype(F32)
            y_ref[:, hp * LANES:(hp + 1) * LANES] = y.astype(y_ref.dtype)
            wsel = jnp.where(top, w_t[ha:ha + 1, :], w_t[hb:hb + 1, :])
            xws.append((xpb.astype(F32).T * wsel).astype(BF16))
            cds.append(jnp.where(top, e_tot_t[ha:ha + 1, :], e_tot_t[hb:hb + 1, :]))
        st = jnp.dot(jnp.concatenate(xws, axis=0), bgb, preferred_element_type=F32)
        for pr in range(4):
            h_scr[g * 4 + pr] = h_grp[pr] * cds[pr] + st[pr * LANES:(pr + 1) * LANES]

    if want_hfin:
        @pl.when(c == nc - 1)
        def _():
            if reverse:
                hfin_ref[0] = hprev_ref[...]
                hfin_ref[1] = h_scr[...]
            else:
                hfin_ref[...] = h_scr[...]


def _ssd_sweep(xin, dt, wts, h0, hprev, *, reverse, want_hfin):
    b, t, _ = xin.shape
    nc = t // CHUNK
    has_h0 = h0 is not None
    rb = CHUNK // 8
    nrb = t // 8
    d = 1 if reverse else 0
    cmap = (lambda c: nc - 1 - c) if reverse else (lambda c: c)
    hshape = (SSD_HEADS // 2, 2 * SSD_HEAD_DIM, D_STATE)

    chunk_spec = pl.BlockSpec((None, CHUNK, CONV_CH), lambda bi, c: (bi, cmap(c), 0))
    if reverse:
        in_specs = [chunk_spec]
        args = [xin]
    else:
        in_specs = [
            chunk_spec,
            pl.BlockSpec((None, 8, CONV_CH), lambda bi, c: (bi, jnp.maximum(c * rb - 1, 0), 0)),
            pl.BlockSpec((None, 8, CONV_CH), lambda bi, c: (bi, jnp.minimum((c + 1) * rb, nrb - 1), 0)),
            pl.BlockSpec((D_CONV, CONV_CH), lambda bi, c: (0, 0)),
            pl.BlockSpec((1, CONV_CH), lambda bi, c: (0, 0)),
            pl.BlockSpec((1, D_INNER), lambda bi, c: (0, 0)),
        ]
        args = [xin, xin, xin, wts["conv_w"], wts["conv_b"], wts["dskip"]]
    in_specs += [
        pl.BlockSpec((None, CHUNK, LANES), lambda bi, c: (bi, cmap(c), d)),
        pl.BlockSpec((None, 1, LANES), lambda bi, c: (d, 0, 0)),
        pl.BlockSpec((None, 1, LANES), lambda bi, c: (d, 0, 0)),
        pl.BlockSpec((None, CHUNK, CHUNK), lambda bi, c: (d, 0, 0)),
        pl.BlockSpec((LANES, D_INNER), lambda bi, c: (0, 0)),
    ]
    args += [dt, wts["a_neg"], wts["dt_bias"], wts["tri"], wts["head_sel"]]
    if has_h0:
        in_specs.append(pl.BlockSpec((None, None) + hshape, lambda bi, c: (bi, d, 0, 0, 0)))
        args.append(h0)
    if want_hfin and reverse:
        in_specs.append(pl.BlockSpec((None,) + hshape, lambda bi, c: (bi, 0, 0, 0)))
        args.append(hprev)
    out_shape = [jax.ShapeDtypeStruct((b, t, D_INNER), BF16)]
    out_specs = [pl.BlockSpec((None, CHUNK, D_INNER), lambda bi, c: (bi, cmap(c), 0))]
    if not reverse:
        out_shape.append(jax.ShapeDtypeStruct((b, t, CONV_CH), BF16))
        out_specs.append(pl.BlockSpec((None, CHUNK, CONV_CH), lambda bi, c: (bi, c, 0)))
    if want_hfin and reverse:
        out_shape.append(jax.ShapeDtypeStruct((b, 2) + hshape, F32))
        out_specs.append(pl.BlockSpec((None, 2) + hshape, lambda bi, c: (bi, 0, 0, 0, 0)))
    elif want_hfin:
        out_shape.append(jax.ShapeDtypeStruct((b,) + hshape, F32))
        out_specs.append(pl.BlockSpec((None,) + hshape, lambda bi, c: (bi, 0, 0, 0)))
    return pl.pallas_call(
        functools.partial(_ssd_kernel, nc=nc, reverse=reverse, has_h0=has_h0, want_hfin=want_hfin),
        out_shape=tuple(out_shape),
        grid=(b, nc),
        in_specs=in_specs,
        out_specs=tuple(out_specs),
        scratch_shapes=[pltpu.VMEM(hshape, F32)],
        compiler_params=_cparams(("arbitrary", "arbitrary")),
        name="ssd_bwd" if reverse else "ssd_fwd",
    )(*args)


def _ssd_call(xbc, dt, wts, h0, *, want_hfin):
    res = _ssd_sweep(xbc, dt, wts, h0, None, reverse=False, want_hfin=want_hfin)
    y_f, xc = res[0], res[1]
    hf = res[2] if want_hfin else None
    res = _ssd_sweep(xc, dt, wts, h0, hf, reverse=True, want_hfin=want_hfin)
    return y_f, res[0], (res[1] if want_hfin else None)


def _route(logits_t, bias_col):
    e, n = logits_t.shape
    per = e // N_EXPERT_GROUPS
    scores = jax.nn.sigmoid(logits_t)
    sel = scores + bias_col
    neg = jnp.float32(-jnp.inf)
    gs = []
    for g in range(N_EXPERT_GROUPS):
        blk = sel[g * per:(g + 1) * per, :]
        m1 = jnp.max(blk, axis=0, keepdims=True)
        is_m1 = blk == m1
        cnt = jnp.sum(jnp.where(is_m1, 1.0, 0.0), axis=0, keepdims=True)
        m2 = jnp.max(jnp.where(is_m1, neg, blk), axis=0, keepdims=True)
        gs.append(m1 + jnp.where(cnt >= 2.0, m1, m2))
    keep = []
    for g in range(N_EXPERT_GROUPS):
        rank = jnp.zeros_like(gs[g])
        for j in range(N_EXPERT_GROUPS):
            if j == g:
                continue
            beats = (gs[j] > gs[g]) if j > g else (gs[j] >= gs[g])
            rank = rank + jnp.where(beats, 1.0, 0.0)
        keep.append(rank < float(TOPK_GROUPS))
    selm = jnp.concatenate(
        [jnp.where(keep[g], sel[g * per:(g + 1) * per, :], neg) for g in range(N_EXPERT_GROUPS)], axis=0)
    eidx = lax.broadcasted_iota(jnp.int32, (e, n), 0).astype(F32)
    cur = selm
    picks = []
    for _ in range(TOP_K):
        m = jnp.max(cur, axis=0, keepdims=True)
        idx = jnp.min(jnp.where(cur == m, eidx, float(e)), axis=0, keepdims=True)
        hit = eidx == idx
        picks.append((idx, hit))
        cur = jnp.where(hit, neg, cur)
    return scores, picks


def _pack_bf16_pairs(h):
    c = h.shape[1] // 2
    lo = pltpu.bitcast(h[:, :c].astype(BF16).astype(F32), jnp.uint32)
    hi = pltpu.bitcast(h[:, c:].astype(BF16).astype(F32), jnp.uint32)
    return (lo >> 16) | (hi & jnp.uint32(0xFFFF0000))


def _unpack_bf16_pairs(w):
    lo = pltpu.bitcast(w << 16, F32)
    hi = pltpu.bitcast(w & jnp.uint32(0xFFFF0000), F32)
    return lo, hi


def _rows8(rows):
    n = rows[0].shape[1]
    ridx = lax.broadcasted_iota(jnp.int32, (TOP_K, n), 0)
    out = jnp.zeros((TOP_K, n), rows[0].dtype)
    for k, r in enumerate(rows):
        out = jnp.where(ridx == k, r, out)
    return out


MERGE_SUB = 512


def _merge_kernel(x_ref, attn_ref, yf_ref, yb_ref, z_ref, gates_ref, mod_ref, wa_ref, ws_ref, wo_ref,
                  sg_ref, n2_ref, wr_ref, rb_ref, wsg_ref, wsu_ref, wsd_ref,
                  xb_ref, ha_ref, hb_ref, eid_ref, pos_ref, wk_ref, cnt_ref):
    tm = x_ref.shape[0]
    sub = MERGE_SUB

    @pl.when(pl.program_id(0) == 0)
    def _():
        cnt_ref[...] = jnp.zeros_like(cnt_ref)

    r_i = lax.broadcasted_iota(jnp.int32, (sub, sub), 0)
    c_i = lax.broadcasted_iota(jnp.int32, (sub, sub), 1)
    before = jnp.where(r_i < c_i, 1.0, 0.0).astype(BF16)
    cnt = cnt_ref[:, 0:1]

    for r0 in range(0, tm, sub):
        rs = slice(r0, r0 + sub)
        x = x_ref[rs, :]
        yy = yf_ref[rs, :].astype(F32) + yb_ref[rs, :].astype(F32)
        u = yy * _silu(z_ref[rs, :]).astype(F32)
        un = u * lax.rsqrt(jnp.mean(u * u, axis=-1, keepdims=True) + EPS) * sg_ref[...]
        ssd_o = _bdot(un, ws_ref[...])
        attn_o = jnp.dot(attn_ref[rs, :], wa_ref[...], preferred_element_type=F32)
        ga = jax.nn.sigmoid(gates_ref[rs, 0:D_MODEL]).astype(F32)
        gs = jax.nn.sigmoid(gates_ref[rs, D_MODEL:2 * D_MODEL]).astype(F32)
        mix = _bdot(ga * attn_o + gs * ssd_o, wo_ref[...])
        x1 = x + _mod(mod_ref, MOD_GATE1) * mix
        h2 = x1 * lax.rsqrt(jnp.mean(x1 * x1, axis=-1, keepdims=True) + EPS) * n2_ref[...]
        h2 = h2 * (1.0 + _mod(mod_ref, MOD_SCALE2)) + _mod(mod_ref, MOD_SHIFT2)
        h2b = h2.astype(BF16)
        ha_ref[rs, :] = _pack_bf16_pairs(h2[:, :D_MODEL // 2])
        hb_ref[rs, :] = _pack_bf16_pairs(h2[:, D_MODEL // 2:])

        logits_t = _bdot_nt(wr_ref[...], h2b)
        scores, picks = _route(logits_t, rb_ref[...])
        chosen = jnp.zeros_like(scores)
        for _, hit in picks:
            chosen = chosen + jnp.where(hit, 1.0, 0.0)
        pos = cnt + jnp.dot(chosen.astype(BF16), before, preferred_element_type=F32)
        cnt = cnt + jnp.sum(chosen, axis=1, keepdims=True)
        poss = [jnp.sum(jnp.where(hit, pos, 0.0), axis=0, keepdims=True) for _, hit in picks]
        wks = [jnp.sum(jnp.where(hit, scores, 0.0), axis=0, keepdims=True) for _, hit in picks]
        wsum = wks[0]
        for w in wks[1:]:
            wsum = wsum + w
        eid_ref[:, rs] = _rows8([idx for idx, _ in picks]).astype(jnp.int32)
        pos_ref[:, rs] = _rows8(poss).astype(jnp.int32)
        wk8 = _rows8(wks) / wsum * ROUTED_SCALE
        wk_ref[rs, :] = jnp.concatenate([wk8, jnp.zeros((LANES - TOP_K, sub), F32)], axis=0).T

        hid = _silu(jnp.dot(h2b, wsg_ref[...], preferred_element_type=F32)) * \
            jnp.dot(h2b, wsu_ref[...], preferred_element_type=F32)
        xb_ref[rs, :] = x1 + _mod(mod_ref, MOD_GATE2) * _bdot(hid, wsd_ref[...])

    cnt_ref[...] = jnp.broadcast_to(cnt, cnt_ref.shape)


def _merge_call(x, attn, y_f, y_b, z, gates, mod_rows, wa, ws, wo, sg, n2, wr_t, rb, wsg, wsu, wsd, *, tm):
    b, t, _ = x.shape
    n = b * t
    flat = lambda a: a.reshape(n, a.shape[-1])
    tok = lambda width: pl.BlockSpec((tm, width), lambda i: (i, 0))
    const2 = lambda shape: pl.BlockSpec(shape, lambda i: (0, 0), pipeline_mode=pl.Buffered(1))
    k8 = pl.BlockSpec((TOP_K, tm), lambda i: (0, i))
    half = D_MODEL // 4
    return pl.pallas_call(
        _merge_kernel,
        out_shape=(jax.ShapeDtypeStruct((n, D_MODEL), F32),
                   jax.ShapeDtypeStruct((n, half), jnp.uint32),
                   jax.ShapeDtypeStruct((n, half), jnp.uint32),
                   jax.ShapeDtypeStruct((TOP_K, n), jnp.int32),
                   jax.ShapeDtypeStruct((TOP_K, n), jnp.int32),
                   jax.ShapeDtypeStruct((n, LANES), F32),
                   jax.ShapeDtypeStruct((N_EXPERTS, LANES), F32)),
        grid=(n // tm,),
        in_specs=[tok(D_MODEL), tok(ATTN_W), tok(D_INNER), tok(D_INNER), tok(D_INNER), tok(2 * D_MODEL),
                  pl.BlockSpec((None, 1, 6 * D_MODEL), lambda i: ((i * tm) // t, 0, 0)),
                  const2((ATTN_W, D_MODEL)), const2((D_INNER, D_MODEL)), const2((D_MODEL, D_MODEL)),
                  const2((1, D_INNER)), const2((1, D_MODEL)),
                  const2((N_EXPERTS, D_MODEL)), const2((N_EXPERTS, 1)),
                  const2((D_MODEL, D_SHARED)), const2((D_MODEL, D_SHARED)), const2((D_SHARED, D_MODEL))],
        out_specs=(tok(D_MODEL), tok(half), tok(half), k8, k8, tok(LANES),
                   pl.BlockSpec((N_EXPERTS, LANES), lambda i: (0, 0))),
        compiler_params=_cparams(("arbitrary",)),
        name="merge",
    )(flat(x), flat(attn), flat(y_f), flat(y_b), flat(z), flat(gates), mod_rows, wa, ws, wo, sg, n2, wr_t, rb,
      wsg, wsu, wsd)


ROW_TILE = 512
SC_WINDOW = 128


def _slots_kernel(start_ref, eid_ref, pos_ref, slot_ref):
    eid = eid_ref[...]
    slot = pos_ref[...]
    for e in range(N_EXPERTS):
        slot = slot + jnp.where(eid == e, start_ref[e], 0)
    slot_ref[...] = slot


def _slots_call(start, eid, pos):
    n = eid.shape[1]
    bn = 2048 if n % 2048 == 0 else n
    spec = pl.BlockSpec((TOP_K, bn), lambda i, s: (0, i))
    return pl.pallas_call(
        _slots_kernel,
        out_shape=jax.ShapeDtypeStruct((TOP_K, n), jnp.int32),
        grid_spec=pltpu.PrefetchScalarGridSpec(num_scalar_prefetch=1, grid=(n // bn,),
                                               in_specs=[spec, spec], out_specs=spec),
        compiler_params=_cparams(("arbitrary",)),
        name="slots",
    )(start, eid, pos)


def _sc_dispatch(x, slots, p):
    n, d = x.shape
    mesh = plsc.VectorSubcoreMesh(core_axis_name="core", subcore_axis_name="subcore")

    @functools.partial(pl.kernel, out_type=jax.ShapeDtypeStruct((p, d), x.dtype), mesh=mesh)
    def k(x_hbm, s_hbm, o_hbm):
        def body(x_vmem, s_vmem):
            for kk in range(TOP_K):
                pltpu.sync_copy(x_vmem, o_hbm.at[s_vmem.at[kk]])

        pltpu.emit_pipeline(
            body,
            grid=(n // SC_WINDOW,),
            in_specs=[pl.BlockSpec((SC_WINDOW, d), index_map=lambda i: (i, 0)),
                      pl.BlockSpec((TOP_K, SC_WINDOW), index_map=lambda i: (0, i))],
            out_specs=[],
            core_axis_name=("core", "subcore"),
            dimension_semantics=(pltpu.PARALLEL,),
        )(x_hbm, s_hbm)

    return k(x, slots)


def _sc_combine(y, slots):
    kk, n = slots.shape
    d = y.shape[1]
    mesh = plsc.VectorSubcoreMesh(core_axis_name="core", subcore_axis_name="subcore")

    @functools.partial(pl.kernel, out_type=jax.ShapeDtypeStruct((kk * n, d), y.dtype), mesh=mesh)
    def k(y_hbm, s_hbm, o_hbm):
        def body(s_vmem, o_vmem):
            pltpu.sync_copy(y_hbm.at[s_vmem.at[0]], o_vmem)

        pltpu.emit_pipeline(
            body,
            grid=(kk * n // SC_WINDOW,),
            in_specs=[pl.BlockSpec((1, SC_WINDOW), index_map=lambda i: (0, i))],
            out_specs=[pl.BlockSpec((SC_WINDOW, d), index_map=lambda i: (i, 0))],
            core_axis_name=("core", "subcore"),
            dimension_semantics=(pltpu.PARALLEL,),
        )(s_hbm, o_hbm)

    return k(y, slots.reshape(1, kk * n)).reshape(kk, n, d)


FFN_IN_BUFS = 4
FFN_OUT_BUFS = 3


def _ffn_kernel(st_ref, nt_ref, wg_ref, wu_ref, wd_ref, xa_hbm, xb_hbm, ya_hbm, yb_hbm,
                wg_s, wu_s, wd_s, xa_buf, xb_buf, ya_buf, yb_buf, in_sem, out_sem, done_ref):
    e = pl.program_id(0)
    ne = pl.num_programs(0)
    nxt = jnp.minimum(e + 1, ne - 1)
    n = nt_ref[e]
    n_next = jnp.where(e + 1 < ne, nt_ref[nxt], 0)
    t = ROW_TILE
    ahead = FFN_IN_BUFS - 1

    def fetch(row, slot):
        r = pl.multiple_of(row, t)
        return (pltpu.make_async_copy(xa_hbm.at[pl.ds(r, t)], xa_buf.at[slot], in_sem.at[0, slot]),
                pltpu.make_async_copy(xb_hbm.at[pl.ds(r, t)], xb_buf.at[slot], in_sem.at[1, slot]))

    def put(row, slot):
        r = pl.multiple_of(row, t)
        return (pltpu.make_async_copy(ya_buf.at[slot], ya_hbm.at[pl.ds(r, t)], out_sem.at[0, slot]),
                pltpu.make_async_copy(yb_buf.at[slot], yb_hbm.at[pl.ds(r, t)], out_sem.at[1, slot]))

    def wait_put(slot):
        for c in put(0, slot):
            c.wait()

    def start_fetch(row, g):
        for c in fetch(row, lax.rem(g, FFN_IN_BUFS)):
            c.start()

    @pl.when(e == 0)
    def _():
        done_ref[0] = 0
        for k in range(ahead):
            @pl.when(k < n)
            def _():
                start_fetch(st_ref[0] + k * t, k)

    done = done_ref[0]

    def prefetch(i):
        j = i + ahead - n

        @pl.when(j < 0)
        def _():
            start_fetch(st_ref[e] + (i + ahead) * t, done + i + ahead)

        @pl.when(jnp.logical_and(j >= 0, j < n_next))
        def _():
            start_fetch(st_ref[nxt] + j * t, done + i + ahead)

    @pl.when(n > 0)
    def _():
        wg_s[...] = wg_ref[...].astype(BF16)
        wu_s[...] = wu_ref[...].astype(BF16)
        wd_s[...] = wd_ref[...].astype(BF16)
        base = st_ref[e]

        def body(i, carry):
            g = done + i
            si = lax.rem(g, FFN_IN_BUFS)
            so = lax.rem(g, FFN_OUT_BUFS)
            row = base + i * t
            for c in fetch(row, si):
                c.wait()
            prefetch(i)

            @pl.when(g >= FFN_OUT_BUFS)
            def _():
                wait_put(so)

            parts = _unpack_bf16_pairs(xa_buf[si]) + _unpack_bf16_pairs(xb_buf[si])
            x = jnp.concatenate(parts, axis=1).astype(BF16)
            hid = _silu(jnp.dot(x, wg_s[...], preferred_element_type=F32)) * \
                jnp.dot(x, wu_s[...], preferred_element_type=F32)
            y = _bdot(hid, wd_s[...])
            ya_buf[so] = _pack_bf16_pairs(y[:, :D_MODEL // 2])
            yb_buf[so] = _pack_bf16_pairs(y[:, D_MODEL // 2:])
            for c in put(row, so):
                c.start()
            return carry

        lax.fori_loop(0, n, body, 0)
        done_ref[0] = done + n

    for k in range(ahead):
        @pl.when(jnp.logical_and(k < ahead - n, k < n_next))
        def _():
            start_fetch(st_ref[nxt] + k * t, done + n + k)

    @pl.when(e == ne - 1)
    def _():
        total = done_ref[0]
        for k in range(FFN_OUT_BUFS):
            @pl.when(total - 1 - k >= 0)
            def _():
                wait_put(lax.rem(total - 1 - k, FFN_OUT_BUFS))


def _ffn_call(start, tiles_e, xa, xb, wg, wu, wd):
    p, half = xa.shape
    wspec = lambda s: pl.BlockSpec((None,) + s, lambda e, st, nt: (e, 0, 0))
    hbm = pl.BlockSpec(memory_space=pl.ANY)
    ibuf = pltpu.VMEM((FFN_IN_BUFS, ROW_TILE, half), jnp.uint32)
    obuf = pltpu.VMEM((FFN_OUT_BUFS, ROW_TILE, half), jnp.uint32)
    return pl.pallas_call(
        _ffn_kernel,
        out_shape=(jax.ShapeDtypeStruct((p, half), jnp.uint32), jax.ShapeDtypeStruct((p, half), jnp.uint32)),
        grid_spec=pltpu.PrefetchScalarGridSpec(
            num_scalar_prefetch=2, grid=(N_EXPERTS,),
            in_specs=[wspec((D_MODEL, D_EXPERT)), wspec((D_MODEL, D_EXPERT)), wspec((D_EXPERT, D_MODEL)),
                      hbm, hbm],
            out_specs=(hbm, hbm),
            scratch_shapes=[pltpu.VMEM((D_MODEL, D_EXPERT), BF16), pltpu.VMEM((D_MODEL, D_EXPERT), BF16),
                            pltpu.VMEM((D_EXPERT, D_MODEL), BF16), ibuf, ibuf, obuf, obuf,
                            pltpu.SemaphoreType.DMA((2, FFN_IN_BUFS)),
                            pltpu.SemaphoreType.DMA((2, FFN_OUT_BUFS)),
                            pltpu.SMEM((1,), jnp.int32)]),
        compiler_params=_cparams(("arbitrary",)),
        name="ffn",
    )(start, tiles_e, wg, wu, wd, xa, xb)


def _final_kernel(xb_ref, ya_ref, yb_ref, wk_ref, mod_ref, o_ref):
    q = D_MODEL // 4
    accs = [jnp.zeros((xb_ref.shape[0], q), F32) for _ in range(4)]
    for k in range(TOP_K):
        w = wk_ref[:, k:k + 1]
        parts = _unpack_bf16_pairs(ya_ref[k]) + _unpack_bf16_pairs(yb_ref[k])
        accs = [a + w * p for a, p in zip(accs, parts)]
    for i, a in enumerate(accs):
        o_ref[:, i * q:(i + 1) * q] = xb_ref[:, i * q:(i + 1) * q] + _mod(mod_ref, MOD_GATE2, i * q, (i + 1) * q) * a


def _final_call(xb, ya, yb, wk, mod_rows, t, *, tm):
    n = xb.shape[0]
    half = ya.shape[2]
    tok = lambda width: pl.BlockSpec((tm, width), lambda i: (i, 0))
    yspec = pl.BlockSpec((TOP_K, tm, half), lambda i: (0, i, 0))
    return pl.pallas_call(
        _final_kernel,
        out_shape=jax.ShapeDtypeStruct((n, D_MODEL), F32),
        grid=(n // tm,),
        in_specs=[tok(D_MODEL), yspec, yspec, tok(LANES),
                  pl.BlockSpec((None, 1, 6 * D_MODEL), lambda i: ((i * tm) // t, 0, 0))],
        out_specs=tok(D_MODEL),
        compiler_params=_cparams(("arbitrary",)),
        name="final",
    )(xb, ya, yb, wk, mod_rows)


def _moe_call(ha, hb, eid, pos, wk, counts, xb, mod_rows, wg, wu, wd, t):
    n = xb.shape[0]
    max_tiles = n * TOP_K // ROW_TILE + N_EXPERTS
    p = max_tiles * ROW_TILE
    cnt = counts[:, 0].astype(jnp.int32)
    tiles_e = (cnt + ROW_TILE - 1) // ROW_TILE
    ends = jnp.cumsum(tiles_e)
    start = (ends - tiles_e) * ROW_TILE
    slots = _slots_call(start.astype(jnp.int32), eid, pos)
    xa = _sc_dispatch(ha, slots, p)
    xbb = _sc_dispatch(hb, slots, p)
    ya, yb = _ffn_call(start.astype(jnp.int32), tiles_e.astype(jnp.int32), xa, xbb, wg, wu, wd)
    ga = _sc_combine(ya, slots)
    gb = _sc_combine(yb, slots)
    return _final_call(xb, ga, gb, wk, mod_rows, t, tm=256)


def _rope_tables(t):
    n_rows = t // GRID_W
    rows = jnp.repeat(jnp.arange(n_rows), GRID_W).astype(F32)
    cols = jnp.tile(jnp.arange(GRID_W), n_rows).astype(F32)
    n_freq = HEAD_DIM // 4
    freqs = ROPE_THETA ** (-jnp.arange(n_freq, dtype=F32) / n_freq)
    ang = jnp.concatenate([rows[:, None] * freqs, cols[:, None] * freqs], axis=-1)
    ang = jnp.repeat(ang, 2, axis=-1)
    ang = jnp.concatenate([ang, ang], axis=-1)
    sign = jnp.where(jnp.arange(LANES) % 2 == 0, -1.0, 1.0).astype(F32)
    return jnp.cos(ang), jnp.sin(ang) * sign


def _dup_heads(a):
    s = a.shape[:-1]
    a4 = a.reshape(s + (N_KV_HEADS, HEAD_DIM))
    return jnp.concatenate([a4, a4], axis=-1).reshape(s + (KVD_W,))


def _prep_w_in(w_in):
    pad = jnp.zeros((D_MODEL, LANES - SSD_HEADS), w_in.dtype)
    dt0 = M_END
    g0 = dt0 + 2 * SSD_HEADS
    tail = [w_in[:, g0:g0 + 2 * D_MODEL], w_in[:, dt0:dt0 + SSD_HEADS], pad,
            w_in[:, dt0 + SSD_HEADS:g0], pad]
    return w_in[:, :M_END].astype(BF16), jnp.concatenate(tail, axis=-1).astype(BF16)


def _pad_heads(a):
    return jnp.pad(a.astype(F32), ((0, 0), (0, LANES - SSD_HEADS)))[:, None, :]


def _trunk(x, mod_rows, wts, rope_tabs, ctx_k, ctx_v, h0, *, tm, tq, want_state):
    b, t, _ = x.shape
    rope = rope_tabs is not None
    if rope:
        cos, sin = rope_tabs
    else:
        cos = sin = jnp.zeros((t, LANES), F32)
    res = _inproj_call(x, mod_rows, wts["g1"], *wts["w_in"], wts["qg"], wts["kg"], cos, sin,
                       rope=rope, emit_kv=want_state, tm=tm)
    q, k, v, gates, z, xbc, dt = res[:7]
    kv_raw = res[7:]
    attn = _attn_call(q, k, v, ctx_k, ctx_v, tq=tq)
    y_f, y_b, hfin = _ssd_call(xbc, dt, wts, h0, want_hfin=want_state)
    xb, ha, hb, eid, pos, wk, counts = _merge_call(
        x, attn, y_f, y_b, z, gates, mod_rows, wts["wa"], wts["ws"], wts["wo"], wts["sg"],
        wts["n2"], wts["wr_t"], wts["rb"], wts["wsg"], wts["wsu"], wts["wsd"], tm=MERGE_SUB)
    out = _moe_call(ha, hb, eid, pos, wk, counts, xb, mod_rows, wts["weg"], wts["weu"], wts["wed"], t)
    return out.reshape(b, t, D_MODEL), kv_raw, hfin


def kernel(x_prompt, x_sample, cache_k, cache_v, state_ssm, c, c_ctx, w_mod, b_mod, norm1_g, norm2_g, w_in,
           q_norm_g, k_norm_g, conv_w, conv_b, a_log, dt_bias, d_skip, ssd_norm_g, w_attn_proj, w_ssd_proj,
           w_out, w_router, router_bias, w_exp_gate, w_exp_up, w_exp_down, w_sh_gate, w_sh_up, w_sh_down):
    depth = w_mod.shape[0]
    assert depth == 1, "single trunk layer"
    bp, tp, _ = x_prompt.shape
    bs, ts, _ = x_sample.shape
    l = 0
    cvec = jnp.concatenate([c_ctx[None, :], c, jnp.zeros((8 - 1 - bs, D_MODEL), F32)], axis=0)
    mod = _mod_call(cvec, w_mod.reshape(D_MODEL, 6 * D_MODEL), b_mod.reshape(1, 6 * D_MODEL))
    mod_prompt = jnp.broadcast_to(mod[0:1][:, None, :], (bp, 1, 6 * D_MODEL))
    mod_sample = mod[1:1 + bs][:, None, :]

    lower = np.tril(np.ones((CHUNK, CHUNK), np.float32))
    head_sel = (np.arange(LANES)[:, None] == np.arange(D_INNER)[None, :] // SSD_HEAD_DIM).astype(np.float32)
    wts = dict(
        g1=norm1_g[l][None, :], n2=norm2_g[l][None, :],
        w_in=_prep_w_in(w_in.reshape(D_MODEL, w_in.shape[-1])),
        qg=jnp.tile(q_norm_g[l], 2)[None, :], kg=jnp.tile(k_norm_g[l], 2)[None, :],
        conv_w=conv_w[l], conv_b=conv_b[l][None, :],
        a_neg=_pad_heads(-jnp.exp(a_log[l].astype(F32))), dt_bias=_pad_heads(dt_bias[l]),
        dskip=jnp.repeat(d_skip[l].astype(F32), SSD_HEAD_DIM)[None, :],
        tri=jnp.asarray(np.stack([lower, lower.T])),
        head_sel=jnp.asarray(head_sel, BF16),
        sg=ssd_norm_g[l][None, :],
        wa=w_attn_proj[l].astype(BF16), ws=w_ssd_proj[l].astype(BF16), wo=w_out[l].astype(BF16),
        wr_t=w_router[l].T.astype(BF16), rb=router_bias[l].astype(F32)[:, None],
        wsg=w_sh_gate[l].astype(BF16), wsu=w_sh_up[l].astype(BF16), wsd=w_sh_down[l].astype(BF16),
        weg=w_exp_gate.reshape(w_exp_gate.shape[1:]), weu=w_exp_up.reshape(w_exp_up.shape[1:]),
        wed=w_exp_down.reshape(w_exp_down.shape[1:]),
    )

    y_prompt, (k_p, v_p), hfin = _trunk(x_prompt, mod_prompt, wts, None, None, None, None,
                                        tm=256, tq=256, want_state=True)
    new_k = k_p.reshape(bp, 1, tp, N_KV_HEADS, HEAD_DIM)
    new_v = v_p.reshape(bp, 1, tp, N_KV_HEADS, HEAD_DIM)
    new_state = hfin.reshape(bp, 1, 2, SSD_HEADS, SSD_HEAD_DIM, D_STATE)

    past = cache_k.shape[2]
    ctx_k = _dup_heads(cache_k[:, l].reshape(bs, past, KV_W)).astype(BF16)
    ctx_v = _dup_heads(cache_v[:, l].reshape(bs, past, KV_W)).astype(BF16)
    h0 = state_ssm[:, l].reshape(bs, 2, SSD_HEADS // 2, 2 * SSD_HEAD_DIM, D_STATE)
    y_sample, _, _ = _trunk(x_sample, mod_sample, wts, _rope_tables(ts), ctx_k, ctx_v, h0,
                               tm=512, tq=256, want_state=False)
    return (y_prompt, y_sample, new_k, new_v, new_state)
```

```python
import functools

import numpy as np
import jax
import jax.numpy as jnp
from jax import lax
from jax.experimental import pallas as pl
from jax.experimental.pallas import tpu as pltpu
from jax.experimental.pallas import tpu_sc as plsc

F32 = jnp.float32
BF16 = jnp.bfloat16

D_MODEL = 1024
GRID_W = 64
EPS = 1e-6
N_HEADS = 16
N_KV_HEADS = 4
HEAD_DIM = 64
ATTN_W = N_HEADS * HEAD_DIM
KV_W = N_KV_HEADS * HEAD_DIM
ROPE_THETA = 10000.0
D_INNER = 2048
SSD_HEAD_DIM = 64
SSD_HEADS = 32
SSD_GROUPS = 4
D_STATE = 128
D_CONV = 4
CHUNK = 128
CONV_CH = D_INNER + 2 * SSD_GROUPS * D_STATE
N_EXPERTS = 64
TOP_K = 8
N_EXPERT_GROUPS = 8
TOPK_GROUPS = 4
D_EXPERT = 256
D_SHARED = 256
ROUTED_SCALE = 2.5

LANES = 128
KVD_W = N_KV_HEADS * LANES
M_Q, M_K, M_V, M_Z, M_X, M_END = (int(c) for c in np.cumsum((0, ATTN_W, KV_W, KV_W, D_INNER, CONV_CH)))
T_G, T_DT, T_END = (int(c) for c in np.cumsum((0, 2 * D_MODEL, 2 * LANES)))
MOD_SHIFT1, MOD_SCALE1, MOD_GATE1, MOD_SHIFT2, MOD_SCALE2, MOD_GATE2 = range(6)
VMEM_LIMIT = 56 * 1024 * 1024
Q_SCALE = HEAD_DIM ** -0.5 * 1.4426950408889634


def _cparams(sem):
    return pltpu.CompilerParams(dimension_semantics=sem, vmem_limit_bytes=VMEM_LIMIT)


def _mod(mod_ref, which, lo=0, hi=D_MODEL):
    return mod_ref[:, which * D_MODEL + lo:which * D_MODEL + hi]


def _silu(x):
    return x * jax.nn.sigmoid(x)


def _bdot(a, b):
    return jnp.dot(a.astype(BF16), b.astype(BF16), preferred_element_type=F32)


def _bdot_nt(a, b):
    return lax.dot_general(a.astype(BF16), b.astype(BF16), (((1,), (1,)), ((), ())),
                           preferred_element_type=F32)


def _mod_kernel(c_ref, w_ref, b_ref, o_ref):
    o_ref[...] = _bdot(_silu(c_ref[...]), w_ref[...]) + b_ref[...]


def _mod_call(cvec, w_mod, b_mod):
    n = w_mod.shape[1]
    bn = 1024
    return pl.pallas_call(
        _mod_kernel,
        out_shape=jax.ShapeDtypeStruct((8, n), F32),
        grid=(n // bn,),
        in_specs=[pl.BlockSpec((8, D_MODEL), lambda j: (0, 0)),
                  pl.BlockSpec((D_MODEL, bn), lambda j: (0, j)),
                  pl.BlockSpec((1, bn), lambda j: (0, j))],
        out_specs=pl.BlockSpec((8, bn), lambda j: (0, j)),
        compiler_params=_cparams(("arbitrary",)),
        name="mod",
    )(cvec, w_mod, b_mod)


def _inproj_kernel(*refs, rope, emit_kv):
    if emit_kv:
        (x_ref, mod_ref, g1_ref, wm_ref, wt_ref, qg_ref, kg_ref, cos_ref, sin_ref,
         q_ref, k_ref, v_ref, gates_ref, z_ref, xbc_ref, dt_ref, kraw_ref, vraw_ref) = refs
    else:
        (x_ref, mod_ref, g1_ref, wm_ref, wt_ref, qg_ref, kg_ref, cos_ref, sin_ref,
         q_ref, k_ref, v_ref, gates_ref, z_ref, xbc_ref, dt_ref) = refs
    tm = x_ref.shape[0]
    x = x_ref[...]
    inv = lax.rsqrt(jnp.mean(x * x, axis=-1, keepdims=True) + EPS)
    h = (x * inv) * g1_ref[...]
    h = h * (1.0 + _mod(mod_ref, MOD_SCALE1)) + _mod(mod_ref, MOD_SHIFT1)
    hb = h.astype(BF16)

    lane = lax.broadcasted_iota(jnp.int32, (tm, LANES), 1)
    lo = lane < HEAD_DIM
    even = (lane & 1) == 0
    if rope:
        cos = cos_ref[...]
        sin = sin_ref[...]

    def rope_fn(blk):
        nxt = pltpu.roll(blk, LANES - 1, 1)
        prv = pltpu.roll(blk, 1, 1)
        return blk * cos + jnp.where(even, nxt, prv) * sin

    def head_norm(blk, g):
        sq = blk * blk
        s_all = jnp.sum(sq, axis=-1, keepdims=True)
        s_lo = jnp.sum(jnp.where(lo, sq, 0.0), axis=-1, keepdims=True)
        ms = jnp.where(lo, s_lo, s_all - s_lo) * (1.0 / HEAD_DIM)
        return blk * lax.rsqrt(ms + EPS) * g

    def dup_heads(blk):
        sw = pltpu.roll(blk, HEAD_DIM, 1)
        return jnp.where(lo, blk, sw), jnp.where(lo, sw, blk)

    qg = qg_ref[...]
    kg = kg_ref[...]
    q = jnp.dot(hb, wm_ref[:, M_Q:M_K], preferred_element_type=F32)
    for j in range(ATTN_W // LANES):
        blk = head_norm(q[:, j * LANES:(j + 1) * LANES], qg)
        if rope:
            blk = rope_fn(blk)
        q_ref[:, j * LANES:(j + 1) * LANES] = (blk * Q_SCALE).astype(q_ref.dtype)

    k = jnp.dot(hb, wm_ref[:, M_K:M_V], preferred_element_type=F32)
    v = jnp.dot(hb, wm_ref[:, M_V:M_Z], preferred_element_type=F32)
    for j in range(KV_W // LANES):
        kb = head_norm(k[:, j * LANES:(j + 1) * LANES], kg)
        vb = v[:, j * LANES:(j + 1) * LANES]
        if emit_kv:
            kraw_ref[:, j * LANES:(j + 1) * LANES] = kb
            vraw_ref[:, j * LANES:(j + 1) * LANES] = vb
        if rope:
            kb = rope_fn(kb)
        for i, (kd, vd) in enumerate(zip(dup_heads(kb), dup_heads(vb))):
            c0 = (2 * j + i) * LANES
            k_ref[:, c0:c0 + LANES] = kd.astype(k_ref.dtype)
            v_ref[:, c0:c0 + LANES] = vd.astype(v_ref.dtype)

    gates_ref[...] = jnp.dot(hb, wt_ref[:, T_G:T_DT], preferred_element_type=F32).astype(gates_ref.dtype)
    z_ref[...] = jnp.dot(hb, wm_ref[:, M_Z:M_X], preferred_element_type=F32).astype(z_ref.dtype)
    xbc_ref[...] = jnp.dot(hb, wm_ref[:, M_X:M_END], preferred_element_type=F32).astype(xbc_ref.dtype)
    dt_ref[...] = jnp.dot(hb, wt_ref[:, T_DT:T_END], preferred_element_type=F32)


def _inproj_call(x, mod_rows, g1, w_main, w_tail, qg, kg, cos, sin, *, rope, emit_kv, tm):
    b, t, _ = x.shape
    assert t % tm == 0, (t, tm)
    nt = t // tm
    tok = lambda width: pl.BlockSpec((None, tm, width), lambda bi, i: (bi, i, 0))
    const2 = lambda shape: pl.BlockSpec(shape, lambda bi, i: (0, 0))
    out_shape = [
        jax.ShapeDtypeStruct((b, t, ATTN_W), BF16),
        jax.ShapeDtypeStruct((b, t, KVD_W), BF16),
        jax.ShapeDtypeStruct((b, t, KVD_W), BF16),
        jax.ShapeDtypeStruct((b, t, 2 * D_MODEL), BF16),
        jax.ShapeDtypeStruct((b, t, D_INNER), BF16),
        jax.ShapeDtypeStruct((b, t, CONV_CH), BF16),
        jax.ShapeDtypeStruct((b, t, 2 * LANES), F32),
    ]
    out_specs = [tok(ATTN_W), tok(KVD_W), tok(KVD_W), tok(2 * D_MODEL), tok(D_INNER), tok(CONV_CH),
                 tok(2 * LANES)]
    if emit_kv:
        out_shape += [jax.ShapeDtypeStruct((b, t, KV_W), F32)] * 2
        out_specs += [tok(KV_W), tok(KV_W)]
    return pl.pallas_call(
        functools.partial(_inproj_kernel, rope=rope, emit_kv=emit_kv),
        out_shape=tuple(out_shape),
        grid=(b, nt),
        in_specs=[tok(D_MODEL),
                  pl.BlockSpec((None, 1, 6 * D_MODEL), lambda bi, i: (jnp.minimum(bi, mod_rows.shape[0] - 1), 0, 0)),
                  const2((1, D_MODEL)),
                  pl.BlockSpec((D_MODEL, M_END), lambda bi, i: (0, 0), pipeline_mode=pl.Buffered(1)),
                  pl.BlockSpec((D_MODEL, T_END), lambda bi, i: (0, 0), pipeline_mode=pl.Buffered(1)),
                  const2((1, LANES)), const2((1, LANES)),
                  pl.BlockSpec((tm, LANES), lambda bi, i: (i, 0)),
                  pl.BlockSpec((tm, LANES), lambda bi, i: (i, 0))],
        out_specs=tuple(out_specs),
        compiler_params=_cparams(("arbitrary", "arbitrary")),
        name="inproj",
    )(x, mod_rows, g1, w_main, w_tail, qg, kg, cos, sin)


KEY_CHUNK = 512


def _key_chunk(n):
    return KEY_CHUNK if n % KEY_CHUNK == 0 else n


def _attn_kernel(*refs, has_ctx):
    if has_ctx:
        q_ref, k_ref, v_ref, kctx_ref, vctx_ref, o_ref = refs
        sources = ((k_ref, v_ref), (kctx_ref, vctx_ref))
    else:
        q_ref, k_ref, v_ref, o_ref = refs
        sources = ((k_ref, v_ref),)
    tq = q_ref.shape[0]
    lane = lax.broadcasted_iota(jnp.int32, (tq, LANES), 1)
    lo = lane < HEAD_DIM
    qs = []
    for j in range(2):
        q2 = q_ref[:, j * LANES:(j + 1) * LANES]
        zero = jnp.zeros_like(q2)
        qs += [jnp.where(lo, q2, zero), jnp.where(lo, zero, q2)]
    q4 = jnp.concatenate(qs, axis=0)
    rows = 4 * tq
    m = jnp.full((rows, 1), -jnp.inf, F32)
    acc = jnp.zeros((rows, LANES), F32)
    chunks = [(kr, vr, c, _key_chunk(kr.shape[0])) for kr, vr in sources
              for c in range(kr.shape[0] // _key_chunk(kr.shape[0]))]
    for kr, vr, c, kc in chunks:
        kch = kr[c * kc:(c + 1) * kc, :].astype(BF16)
        vch = vr[c * kc:(c + 1) * kc, :].astype(BF16)
        lane_k = lax.broadcasted_iota(jnp.int32, (kc, LANES), 1)
        vch = jnp.where(lane_k < HEAD_DIM, vch, jnp.ones_like(vch))
        s = _bdot_nt(q4, kch)
        m_new = jnp.maximum(m, jnp.max(s, axis=-1, keepdims=True))
        alpha = jnp.exp2(m - m_new)
        p = jnp.exp2((s - m_new).astype(BF16))
        acc = acc * alpha + jnp.dot(p, vch, preferred_element_type=F32)
        m = m_new
    o = acc * (1.0 / pltpu.roll(acc, HEAD_DIM, 1))
    for j in range(2):
        oa = o[(2 * j) * tq:(2 * j + 1) * tq]
        ob = pltpu.roll(o[(2 * j + 1) * tq:(2 * j + 2) * tq], HEAD_DIM, 1)
        o_ref[:, j * LANES:(j + 1) * LANES] = jnp.where(lo, oa, ob).astype(o_ref.dtype)


def _attn_call(q, k, v, kctx, vctx, *, tq):
    b, t, _ = q.shape
    assert t % tq == 0, (t, tq)
    tk = k.shape[1]
    nq = t // tq
    has_ctx = kctx is not None
    kv_spec = lambda n: pl.BlockSpec((None, n, LANES), lambda bi, g, i: (bi, 0, g))
    in_specs = [pl.BlockSpec((None, tq, 2 * LANES), lambda bi, g, i: (bi, i, g)), kv_spec(tk), kv_spec(tk)]
    args = [q, k, v]
    if has_ctx:
        in_specs += [kv_spec(kctx.shape[1]), kv_spec(kctx.shape[1])]
        args += [kctx, vctx]
    return pl.pallas_call(
        functools.partial(_attn_kernel, has_ctx=has_ctx),
        out_shape=jax.ShapeDtypeStruct((b, t, ATTN_W), BF16),
        grid=(b, N_KV_HEADS, nq),
        in_specs=in_specs,
        out_specs=pl.BlockSpec((None, tq, 2 * LANES), lambda bi, g, i: (bi, i, g)),
        compiler_params=_cparams(("arbitrary", "arbitrary", "arbitrary")),
        name="attn",
    )(*args)


LOG2E = 1.4426950408889634


def _softplus(x):
    return jnp.maximum(x, 0.0) + jnp.log(1.0 + jnp.exp(-jnp.abs(x)))


def _ssd_kernel(*refs, nc, reverse, has_h0, want_hfin):
    refs = list(refs)
    conv = not reverse
    if conv:
        xbc_ref, prev_ref, next_ref, cw_ref, cb_ref, dsk_ref = refs[:6]
        refs = refs[6:]
    else:
        xc_ref = refs.pop(0)
    dt_ref, an_ref, dtb_ref, tri_ref, sel_ref = refs[:5]
    refs = refs[5:]
    h0_ref = refs.pop(0) if has_h0 else None
    hprev_ref = refs.pop(0) if (want_hfin and reverse) else None
    y_ref = refs.pop(0)
    xco_ref = refs.pop(0) if conv else None
    hfin_ref = refs.pop(0) if want_hfin else None
    h_scr = refs.pop(0)

    L = CHUNK
    c = pl.program_id(1)
    cidx = (nc - 1 - c) if reverse else c

    @pl.when(c == 0)
    def _():
        if has_h0:
            h_scr[...] = h0_ref[...]
        else:
            h_scr[...] = jnp.zeros_like(h_scr)

    row = lax.broadcasted_iota(jnp.int32, (L, LANES), 0)
    lane = lax.broadcasted_iota(jnp.int32, (L, LANES), 1)
    lo = lane < SSD_HEAD_DIM
    top = row < SSD_HEAD_DIM

    if conv:
        first = cidx == 0
        last = cidx == nc - 1

        def cols(a, w):
            xm = xbc_ref[:, a:a + w].astype(F32)
            rw = lax.broadcasted_iota(jnp.int32, (L, w), 0)
            p6 = jnp.where(first, 0.0, prev_ref[6:7, a:a + w].astype(F32))
            p7 = jnp.where(first, 0.0, prev_ref[7:8, a:a + w].astype(F32))
            n0 = jnp.where(last, 0.0, next_ref[0:1, a:a + w].astype(F32))
            r1 = jnp.where(rw == 0, p7, pltpu.roll(xm, 1, 0))
            r2 = jnp.where(rw == 0, p6, jnp.where(rw == 1, p7, pltpu.roll(xm, 2, 0)))
            rn = jnp.where(rw == L - 1, n0, pltpu.roll(xm, L - 1, 0))
            y = (r2 * cw_ref[0:1, a:a + w] + r1 * cw_ref[1:2, a:a + w] + xm * cw_ref[2:3, a:a + w]
                 + rn * cw_ref[3:4, a:a + w] + cb_ref[:, a:a + w])
            y = _silu(y).astype(BF16)
            xco_ref[:, a:a + w] = y
            return y
    else:
        def cols(a, w):
            return xc_ref[:, a:a + w]

    causal = tri_ref[...] > 0.0
    dt = _softplus(dt_ref[...] + dtb_ref[...])
    la2 = dt * (an_ref[...] * LOG2E)
    acum2 = jnp.dot(tri_ref[...], la2, preferred_element_type=F32, precision=lax.Precision.HIGHEST)
    dt_t = dt.T
    acum2_t = acum2.T
    tot2_t = jnp.sum(la2.T, axis=1, keepdims=True)
    lg_dt_t = jnp.log2(dt_t)
    r_t = lg_dt_t - acum2_t
    w_t = jnp.exp2(lg_dt_t + tot2_t - acum2_t)
    e_acum_x = jnp.dot(jnp.exp2(acum2).astype(BF16), sel_ref[...], preferred_element_type=F32)
    e_tot_t = jnp.exp2(tot2_t)

    for g in range(SSD_GROUPS):
        bgb = cols(D_INNER + g * D_STATE, D_STATE)
        cgb = cols(D_INNER + SSD_GROUPS * D_STATE + g * D_STATE, D_STATE)
        cbm = _bdot_nt(cgb, bgb)
        h_grp = h_scr[4 * g:4 * g + 4]
        yo_grp = _bdot_nt(cgb, h_grp.reshape(4 * LANES, D_STATE))
        xws, cds = [], []
        for pr in range(4):
            hp = g * 4 + pr
            ha, hb = 2 * hp, 2 * hp + 1
            xpb = cols(hp * LANES, LANES)
            zero = jnp.zeros_like(xpb)
            xs = jnp.concatenate([jnp.where(lo, xpb, zero), jnp.where(lo, zero, xpb)], axis=0)
            ms = []
            for hh in (ha, hb):
                e = jnp.exp2(acum2[:, hh:hh + 1] + r_t[hh:hh + 1, :])
                ms.append((cbm * jnp.where(causal, e, 0.0)).astype(BF16))
            y = jnp.dot(jnp.concatenate(ms, axis=1), xs, preferred_element_type=F32)
            y = y + yo_grp[:, pr * LANES:(pr + 1) * LANES] * e_acum_x[:, hp * LANES:(hp + 1) * LANES]
            if conv:
                y = y + dsk_ref[:, hp * LANES:(hp + 1) * LANES] * xpb.astype(F32)
            y_ref[:, hp * LANES:(hp + 1) * LANES] = y.astype(y_ref.dtype)
            wsel = jnp.where(top, w_t[ha:ha + 1, :], w_t[hb:hb + 1, :])
            xws.append((xpb.astype(F32).T * wsel).astype(BF16))
            cds.append(jnp.where(top, e_tot_t[ha:ha + 1, :], e_tot_t[hb:hb + 1, :]))
        st = jnp.dot(jnp.concatenate(xws, axis=0), bgb, preferred_element_type=F32)
        for pr in range(4):
            h_scr[g * 4 + pr] = h_grp[pr] * cds[pr] + st[pr * LANES:(pr + 1) * LANES]

    if want_hfin:
        @pl.when(c == nc - 1)
        def _():
            if reverse:
                hfin_ref[0] = hprev_ref[...]
                hfin_ref[1] = h_scr[...]
            else:
                hfin_ref[...] = h_scr[...]


def _ssd_sweep(xin, dt, wts, h0, hprev, *, reverse, want_hfin):
    b, t, _ = xin.shape
    assert t % CHUNK == 0, t
    nc = t // CHUNK
    has_h0 = h0 is not None
    rb = CHUNK // 8
    nrb = t // 8
    d = 1 if reverse else 0
    cmap = (lambda c: nc - 1 - c) if reverse else (lambda c: c)
    hshape = (SSD_HEADS // 2, 2 * SSD_HEAD_DIM, D_STATE)

    chunk_spec = pl.BlockSpec((None, CHUNK, CONV_CH), lambda bi, c: (bi, cmap(c), 0))
    if reverse:
        in_specs = [chunk_spec]
        args = [xin]
    else:
        in_specs = [
            chunk_spec,
            pl.BlockSpec((None, 8, CONV_CH), lambda bi, c: (bi, jnp.maximum(c * rb - 1, 0), 0)),
            pl.BlockSpec((None, 8, CONV_CH), lambda bi, c: (bi, jnp.minimum((c + 1) * rb, nrb - 1), 0)),
            pl.BlockSpec((D_CONV, CONV_CH), lambda bi, c: (0, 0)),
            pl.BlockSpec((1, CONV_CH), lambda bi, c: (0, 0)),
            pl.BlockSpec((1, D_INNER), lambda bi, c: (0, 0)),
        ]
        args = [xin, xin, xin, wts["conv_w"], wts["conv_b"], wts["dskip"]]
    in_specs += [
        pl.BlockSpec((None, CHUNK, LANES), lambda bi, c: (bi, cmap(c), d)),
        pl.BlockSpec((None, 1, LANES), lambda bi, c: (d, 0, 0)),
        pl.BlockSpec((None, 1, LANES), lambda bi, c: (d, 0, 0)),
        pl.BlockSpec((None, CHUNK, CHUNK), lambda bi, c: (d, 0, 0)),
        pl.BlockSpec((LANES, D_INNER), lambda bi, c: (0, 0)),
    ]
    args += [dt, wts["a_neg"], wts["dt_bias"], wts["tri"], wts["head_sel"]]
    if has_h0:
        in_specs.append(pl.BlockSpec((None, None) + hshape, lambda bi, c: (bi, d, 0, 0, 0)))
        args.append(h0)
    if want_hfin and reverse:
        in_specs.append(pl.BlockSpec((None,) + hshape, lambda bi, c: (bi, 0, 0, 0)))
        args.append(hprev)
    out_shape = [jax.ShapeDtypeStruct((b, t, D_INNER), BF16)]
    out_specs = [pl.BlockSpec((None, CHUNK, D_INNER), lambda bi, c: (bi, cmap(c), 0))]
    if not reverse:
        out_shape.append(jax.ShapeDtypeStruct((b, t, CONV_CH), BF16))
        out_specs.append(pl.BlockSpec((None, CHUNK, CONV_CH), lambda bi, c: (bi, c, 0)))
    if want_hfin and reverse:
        out_shape.append(jax.ShapeDtypeStruct((b, 2) + hshape, F32))
        out_specs.append(pl.BlockSpec((None, 2) + hshape, lambda bi, c: (bi, 0, 0, 0, 0)))
    elif want_hfin:
        out_shape.append(jax.ShapeDtypeStruct((b,) + hshape, F32))
        out_specs.append(pl.BlockSpec((None,) + hshape, lambda bi, c: (bi, 0, 0, 0)))
    return pl.pallas_call(
        functools.partial(_ssd_kernel, nc=nc, reverse=reverse, has_h0=has_h0, want_hfin=want_hfin),
        out_shape=tuple(out_shape),
        grid=(b, nc),
        in_specs=in_specs,
        out_specs=tuple(out_specs),
        scratch_shapes=[pltpu.VMEM(hshape, F32)],
        compiler_params=_cparams(("arbitrary", "arbitrary")),
        name="ssd_bwd" if reverse else "ssd_fwd",
    )(*args)


def _ssd_call(xbc, dt, wts, h0, *, want_hfin):
    res = _ssd_sweep(xbc, dt, wts, h0, None, reverse=False, want_hfin=want_hfin)
    y_f, xc = res[0], res[1]
    hf = res[2] if want_hfin else None
    res = _ssd_sweep(xc, dt, wts, h0, hf, reverse=True, want_hfin=want_hfin)
    return y_f, res[0], (res[1] if want_hfin else None)


def _route(logits_t, bias_col):
    e, n = logits_t.shape
    per = e // N_EXPERT_GROUPS
    scores = jax.nn.sigmoid(logits_t)
    sel = scores + bias_col
    neg = jnp.float32(-jnp.inf)
    gs = []
    for g in range(N_EXPERT_GROUPS):
        blk = sel[g * per:(g + 1) * per, :]
        m1 = jnp.max(blk, axis=0, keepdims=True)
        is_m1 = blk == m1
        cnt = jnp.sum(jnp.where(is_m1, 1.0, 0.0), axis=0, keepdims=True)
        m2 = jnp.max(jnp.where(is_m1, neg, blk), axis=0, keepdims=True)
        gs.append(m1 + jnp.where(cnt >= 2.0, m1, m2))
    keep = []
    for g in range(N_EXPERT_GROUPS):
        rank = jnp.zeros_like(gs[g])
        for j in range(N_EXPERT_GROUPS):
            if j == g:
                continue
            beats = (gs[j] > gs[g]) if j > g else (gs[j] >= gs[g])
            rank = rank + jnp.where(beats, 1.0, 0.0)
        keep.append(rank < float(TOPK_GROUPS))
    selm = jnp.concatenate(
        [jnp.where(keep[g], sel[g * per:(g + 1) * per, :], neg) for g in range(N_EXPERT_GROUPS)], axis=0)
    eidx = lax.broadcasted_iota(jnp.int32, (e, n), 0).astype(F32)
    cur = selm
    picks = []
    for _ in range(TOP_K):
        m = jnp.max(cur, axis=0, keepdims=True)
        idx = jnp.min(jnp.where(cur == m, eidx, float(e)), axis=0, keepdims=True)
        hit = eidx == idx
        picks.append((idx, hit))
        cur = jnp.where(hit, neg, cur)
    return scores, picks


def _pack_bf16_pairs(h):
    c = h.shape[1] // 2
    lo = pltpu.bitcast(h[:, :c].astype(BF16).astype(F32), jnp.uint32)
    hi = pltpu.bitcast(h[:, c:].astype(BF16).astype(F32), jnp.uint32)
    return (lo >> 16) | (hi & jnp.uint32(0xFFFF0000))


def _unpack_bf16_pairs(w):
    lo = pltpu.bitcast(w << 16, F32)
    hi = pltpu.bitcast(w & jnp.uint32(0xFFFF0000), F32)
    return lo, hi


def _rows8(rows):
    n = rows[0].shape[1]
    ridx = lax.broadcasted_iota(jnp.int32, (TOP_K, n), 0)
    out = jnp.zeros((TOP_K, n), rows[0].dtype)
    for k, r in enumerate(rows):
        out = jnp.where(ridx == k, r, out)
    return out


MERGE_SUB = 512


def _merge_kernel(x_ref, attn_ref, yf_ref, yb_ref, z_ref, gates_ref, mod_ref, wa_ref, ws_ref, wo_ref,
                  sg_ref, n2_ref, wr_ref, rb_ref, wsg_ref, wsu_ref, wsd_ref,
                  xb_ref, ha_ref, hb_ref, eid_ref, pos_ref, wk_ref, cnt_ref):
    tm = x_ref.shape[0]
    sub = MERGE_SUB

    @pl.when(pl.program_id(0) == 0)
    def _():
        cnt_ref[...] = jnp.zeros_like(cnt_ref)

    r_i = lax.broadcasted_iota(jnp.int32, (sub, sub), 0)
    c_i = lax.broadcasted_iota(jnp.int32, (sub, sub), 1)
    before = jnp.where(r_i < c_i, 1.0, 0.0).astype(BF16)
    cnt = cnt_ref[:, 0:1]

    for r0 in range(0, tm, sub):
        rs = slice(r0, r0 + sub)
        x = x_ref[rs, :]
        yy = yf_ref[rs, :].astype(F32) + yb_ref[rs, :].astype(F32)
        u = yy * _silu(z_ref[rs, :]).astype(F32)
        un = u * lax.rsqrt(jnp.mean(u * u, axis=-1, keepdims=True) + EPS) * sg_ref[...]
        ssd_o = _bdot(un, ws_ref[...])
        attn_o = jnp.dot(attn_ref[rs, :], wa_ref[...], preferred_element_type=F32)
        ga = jax.nn.sigmoid(gates_ref[rs, 0:D_MODEL]).astype(F32)
        gs = jax.nn.sigmoid(gates_ref[rs, D_MODEL:2 * D_MODEL]).astype(F32)
        mix = _bdot(ga * attn_o + gs * ssd_o, wo_ref[...])
        x1 = x + _mod(mod_ref, MOD_GATE1) * mix
        h2 = x1 * lax.rsqrt(jnp.mean(x1 * x1, axis=-1, keepdims=True) + EPS) * n2_ref[...]
        h2 = h2 * (1.0 + _mod(mod_ref, MOD_SCALE2)) + _mod(mod_ref, MOD_SHIFT2)
        h2b = h2.astype(BF16)
        ha_ref[rs, :] = _pack_bf16_pairs(h2[:, :D_MODEL // 2])
        hb_ref[rs, :] = _pack_bf16_pairs(h2[:, D_MODEL // 2:])

        logits_t = _bdot_nt(wr_ref[...], h2b)
        scores, picks = _route(logits_t, rb_ref[...])
        chosen = jnp.zeros_like(scores)
        for _, hit in picks:
            chosen = chosen + jnp.where(hit, 1.0, 0.0)
        pos = cnt + jnp.dot(chosen.astype(BF16), before, preferred_element_type=F32)
        cnt = cnt + jnp.sum(chosen, axis=1, keepdims=True)
        poss = [jnp.sum(jnp.where(hit, pos, 0.0), axis=0, keepdims=True) for _, hit in picks]
        wks = [jnp.sum(jnp.where(hit, scores, 0.0), axis=0, keepdims=True) for _, hit in picks]
        wsum = wks[0]
        for w in wks[1:]:
            wsum = wsum + w
        eid_ref[:, rs] = _rows8([idx for idx, _ in picks]).astype(jnp.int32)
        pos_ref[:, rs] = _rows8(poss).astype(jnp.int32)
        wk8 = _rows8(wks) / wsum * ROUTED_SCALE
        wk_ref[rs, :] = jnp.concatenate([wk8, jnp.zeros((LANES - TOP_K, sub), F32)], axis=0).T

        hid = _silu(jnp.dot(h2b, wsg_ref[...], preferred_element_type=F32)) * \
            jnp.dot(h2b, wsu_ref[...], preferred_element_type=F32)
        xb_ref[rs, :] = x1 + _mod(mod_ref, MOD_GATE2) * _bdot(hid, wsd_ref[...])

    cnt_ref[...] = jnp.broadcast_to(cnt, cnt_ref.shape)


def _merge_call(x, attn, y_f, y_b, z, gates, mod_rows, wa, ws, wo, sg, n2, wr_t, rb, wsg, wsu, wsd, *, tm):
    b, t, _ = x.shape
    n = b * t
    assert n % tm == 0 and (t % tm == 0 or mod_rows.shape[0] == 1), (n, t, tm)
    flat = lambda a: a.reshape(n, a.shape[-1])
    tok = lambda width: pl.BlockSpec((tm, width), lambda i: (i, 0))
    const2 = lambda shape: pl.BlockSpec(shape, lambda i: (0, 0), pipeline_mode=pl.Buffered(1))
    k8 = pl.BlockSpec((TOP_K, tm), lambda i: (0, i))
    half = D_MODEL // 4
    return pl.pallas_call(
        _merge_kernel,
        out_shape=(jax.ShapeDtypeStruct((n, D_MODEL), F32),
                   jax.ShapeDtypeStruct((n, half), jnp.uint32),
                   jax.ShapeDtypeStruct((n, half), jnp.uint32),
                   jax.ShapeDtypeStruct((TOP_K, n), jnp.int32),
                   jax.ShapeDtypeStruct((TOP_K, n), jnp.int32),
                   jax.ShapeDtypeStruct((n, LANES), F32),
                   jax.ShapeDtypeStruct((N_EXPERTS, LANES), F32)),
        grid=(n // tm,),
        in_specs=[tok(D_MODEL), tok(ATTN_W), tok(D_INNER), tok(D_INNER), tok(D_INNER), tok(2 * D_MODEL),
                  pl.BlockSpec((None, 1, 6 * D_MODEL), lambda i: (jnp.minimum((i * tm) // t, mod_rows.shape[0] - 1), 0, 0)),
                  const2((ATTN_W, D_MODEL)), const2((D_INNER, D_MODEL)), const2((D_MODEL, D_MODEL)),
                  const2((1, D_INNER)), const2((1, D_MODEL)),
                  const2((N_EXPERTS, D_MODEL)), const2((N_EXPERTS, 1)),
                  const2((D_MODEL, D_SHARED)), const2((D_MODEL, D_SHARED)), const2((D_SHARED, D_MODEL))],
        out_specs=(tok(D_MODEL), tok(half), tok(half), k8, k8, tok(LANES),
                   pl.BlockSpec((N_EXPERTS, LANES), lambda i: (0, 0))),
        compiler_params=_cparams(("arbitrary",)),
        name="merge",
    )(flat(x), flat(attn), flat(y_f), flat(y_b), flat(z), flat(gates), mod_rows, wa, ws, wo, sg, n2, wr_t, rb,
      wsg, wsu, wsd)


ROW_TILE = 512
SC_WINDOW = 128


def _slots_kernel(start_ref, eid_ref, pos_ref, slot_ref):
    eid = eid_ref[...]
    slot = pos_ref[...]
    for e in range(N_EXPERTS):
        slot = slot + jnp.where(eid == e, start_ref[e], 0)
    slot_ref[...] = slot


def _slots_call(start, eid, pos):
    n = eid.shape[1]
    bn = 2048 if n % 2048 == 0 else n
    spec = pl.BlockSpec((TOP_K, bn), lambda i, s: (0, i))
    return pl.pallas_call(
        _slots_kernel,
        out_shape=jax.ShapeDtypeStruct((TOP_K, n), jnp.int32),
        grid_spec=pltpu.PrefetchScalarGridSpec(num_scalar_prefetch=1, grid=(n // bn,),
                                               in_specs=[spec, spec], out_specs=spec),
        compiler_params=_cparams(("arbitrary",)),
        name="slots",
    )(start, eid, pos)


def _sc_dispatch(x, slots, p):
    n, d = x.shape
    assert n % SC_WINDOW == 0, n
    mesh = plsc.VectorSubcoreMesh(core_axis_name="core", subcore_axis_name="subcore")

    @functools.partial(pl.kernel, out_type=jax.ShapeDtypeStruct((p, d), x.dtype), mesh=mesh)
    def k(x_hbm, s_hbm, o_hbm):
        def body(x_vmem, s_vmem):
            for kk in range(TOP_K):
                pltpu.sync_copy(x_vmem, o_hbm.at[s_vmem.at[kk]])

        pltpu.emit_pipeline(
            body,
            grid=(n // SC_WINDOW,),
            in_specs=[pl.BlockSpec((SC_WINDOW, d), index_map=lambda i: (i, 0)),
                      pl.BlockSpec((TOP_K, SC_WINDOW), index_map=lambda i: (0, i))],
            out_specs=[],
            core_axis_name=("core", "subcore"),
            dimension_semantics=(pltpu.PARALLEL,),
        )(x_hbm, s_hbm)

    return k(x, slots)


def _sc_combine(y, slots):
    kk, n = slots.shape
    assert n % SC_WINDOW == 0, n
    d = y.shape[1]
    mesh = plsc.VectorSubcoreMesh(core_axis_name="core", subcore_axis_name="subcore")

    @functools.partial(pl.kernel, out_type=jax.ShapeDtypeStruct((kk * n, d), y.dtype), mesh=mesh)
    def k(y_hbm, s_hbm, o_hbm):
        def body(s_vmem, o_vmem):
            pltpu.sync_copy(y_hbm.at[s_vmem.at[0]], o_vmem)

        pltpu.emit_pipeline(
            body,
            grid=(kk * n // SC_WINDOW,),
            in_specs=[pl.BlockSpec((1, SC_WINDOW), index_map=lambda i: (0, i))],
            out_specs=[pl.BlockSpec((SC_WINDOW, d), index_map=lambda i: (i, 0))],
            core_axis_name=("core", "subcore"),
            dimension_semantics=(pltpu.PARALLEL,),
        )(s_hbm, o_hbm)

    return k(y, slots.reshape(1, kk * n)).reshape(kk, n, d)


FFN_IN_BUFS = 4
FFN_OUT_BUFS = 3


def _ffn_kernel(st_ref, nt_ref, wg_ref, wu_ref, wd_ref, xa_hbm, xb_hbm, ya_hbm, yb_hbm,
                wg_s, wu_s, wd_s, xa_buf, xb_buf, ya_buf, yb_buf, in_sem, out_sem, done_ref):
    e = pl.program_id(0)
    ne = pl.num_programs(0)
    nxt = jnp.minimum(e + 1, ne - 1)
    n = nt_ref[e]
    n_next = jnp.where(e + 1 < ne, nt_ref[nxt], 0)
    t = ROW_TILE
    ahead = FFN_IN_BUFS - 1

    def fetch(row, slot):
        r = pl.multiple_of(row, t)
        return (pltpu.make_async_copy(xa_hbm.at[pl.ds(r, t)], xa_buf.at[slot], in_sem.at[0, slot]),
                pltpu.make_async_copy(xb_hbm.at[pl.ds(r, t)], xb_buf.at[slot], in_sem.at[1, slot]))

    def put(row, slot):
        r = pl.multiple_of(row, t)
        return (pltpu.make_async_copy(ya_buf.at[slot], ya_hbm.at[pl.ds(r, t)], out_sem.at[0, slot]),
                pltpu.make_async_copy(yb_buf.at[slot], yb_hbm.at[pl.ds(r, t)], out_sem.at[1, slot]))

    def wait_put(slot):
        for c in put(0, slot):
            c.wait()

    def start_fetch(row, g):
        for c in fetch(row, lax.rem(g, FFN_IN_BUFS)):
            c.start()

    @pl.when(e == 0)
    def _():
        done_ref[0] = 0
        for k in range(ahead):
            @pl.when(k < n)
            def _():
                start_fetch(st_ref[0] + k * t, k)

    done = done_ref[0]

    def prefetch(i):
        j = i + ahead - n

        @pl.when(j < 0)
        def _():
            start_fetch(st_ref[e] + (i + ahead) * t, done + i + ahead)

        @pl.when(jnp.logical_and(j >= 0, j < n_next))
        def _():
            start_fetch(st_ref[nxt] + j * t, done + i + ahead)

    @pl.when(n > 0)
    def _():
        wg_s[...] = wg_ref[...].astype(BF16)
        wu_s[...] = wu_ref[...].astype(BF16)
        wd_s[...] = wd_ref[...].astype(BF16)
        base = st_ref[e]

        def body(i, carry):
            g = done + i
            si = lax.rem(g, FFN_IN_BUFS)
            so = lax.rem(g, FFN_OUT_BUFS)
            row = base + i * t
            for c in fetch(row, si):
                c.wait()
            prefetch(i)

            @pl.when(g >= FFN_OUT_BUFS)
            def _():
                wait_put(so)

            parts = _unpack_bf16_pairs(xa_buf[si]) + _unpack_bf16_pairs(xb_buf[si])
            x = jnp.concatenate(parts, axis=1).astype(BF16)
            hid = _silu(jnp.dot(x, wg_s[...], preferred_element_type=F32)) * \
                jnp.dot(x, wu_s[...], preferred_element_type=F32)
            y = _bdot(hid, wd_s[...])
            ya_buf[so] = _pack_bf16_pairs(y[:, :D_MODEL // 2])
            yb_buf[so] = _pack_bf16_pairs(y[:, D_MODEL // 2:])
            for c in put(row, so):
                c.start()
            return carry

        lax.fori_loop(0, n, body, 0)
        done_ref[0] = done + n

    for k in range(ahead):
        @pl.when(jnp.logical_and(k < ahead - n, k < n_next))
        def _():
            start_fetch(st_ref[nxt] + k * t, done + n + k)

    @pl.when(e == ne - 1)
    def _():
        total = done_ref[0]
        for k in range(FFN_OUT_BUFS):
            @pl.when(total - 1 - k >= 0)
            def _():
                wait_put(lax.rem(total - 1 - k, FFN_OUT_BUFS))


def _ffn_call(start, tiles_e, xa, xb, wg, wu, wd):
    p, half = xa.shape
    wspec = lambda s: pl.BlockSpec((None,) + s, lambda e, st, nt: (e, 0, 0))
    hbm = pl.BlockSpec(memory_space=pl.ANY)
    ibuf = pltpu.VMEM((FFN_IN_BUFS, ROW_TILE, half), jnp.uint32)
    obuf = pltpu.VMEM((FFN_OUT_BUFS, ROW_TILE, half), jnp.uint32)
    return pl.pallas_call(
        _ffn_kernel,
        out_shape=(jax.ShapeDtypeStruct((p, half), jnp.uint32), jax.ShapeDtypeStruct((p, half), jnp.uint32)),
        grid_spec=pltpu.PrefetchScalarGridSpec(
            num_scalar_prefetch=2, grid=(N_EXPERTS,),
            in_specs=[wspec((D_MODEL, D_EXPERT)), wspec((D_MODEL, D_EXPERT)), wspec((D_EXPERT, D_MODEL)),
                      hbm, hbm],
            out_specs=(hbm, hbm),
            scratch_shapes=[pltpu.VMEM((D_MODEL, D_EXPERT), BF16), pltpu.VMEM((D_MODEL, D_EXPERT), BF16),
                            pltpu.VMEM((D_EXPERT, D_MODEL), BF16), ibuf, ibuf, obuf, obuf,
                            pltpu.SemaphoreType.DMA((2, FFN_IN_BUFS)),
                            pltpu.SemaphoreType.DMA((2, FFN_OUT_BUFS)),
                            pltpu.SMEM((1,), jnp.int32)]),
        compiler_params=_cparams(("arbitrary",)),
        name="ffn",
    )(start, tiles_e, wg, wu, wd, xa, xb)


def _final_kernel(xb_ref, ya_ref, yb_ref, wk_ref, mod_ref, o_ref):
    q = D_MODEL // 4
    accs = [jnp.zeros((xb_ref.shape[0], q), F32) for _ in range(4)]
    for k in range(TOP_K):
        w = wk_ref[:, k:k + 1]
        parts = _unpack_bf16_pairs(ya_ref[k]) + _unpack_bf16_pairs(yb_ref[k])
        accs = [a + w * p for a, p in zip(accs, parts)]
    for i, a in enumerate(accs):
        o_ref[:, i * q:(i + 1) * q] = xb_ref[:, i * q:(i + 1) * q] + _mod(mod_ref, MOD_GATE2, i * q, (i + 1) * q) * a


def _final_call(xb, ya, yb, wk, mod_rows, t, *, tm):
    n = xb.shape[0]
    assert n % tm == 0 and (t % tm == 0 or mod_rows.shape[0] == 1), (n, t, tm)
    half = ya.shape[2]
    tok = lambda width: pl.BlockSpec((tm, width), lambda i: (i, 0))
    yspec = pl.BlockSpec((TOP_K, tm, half), lambda i: (0, i, 0))
    return pl.pallas_call(
        _final_kernel,
        out_shape=jax.ShapeDtypeStruct((n, D_MODEL), F32),
        grid=(n // tm,),
        in_specs=[tok(D_MODEL), yspec, yspec, tok(LANES),
                  pl.BlockSpec((None, 1, 6 * D_MODEL), lambda i: (jnp.minimum((i * tm) // t, mod_rows.shape[0] - 1), 0, 0))],
        out_specs=tok(D_MODEL),
        compiler_params=_cparams(("arbitrary",)),
        name="final",
    )(xb, ya, yb, wk, mod_rows)


def _moe_call(ha, hb, eid, pos, wk, counts, xb, mod_rows, wg, wu, wd, t):
    n = xb.shape[0]
    max_tiles = n * TOP_K // ROW_TILE + N_EXPERTS
    p = max_tiles * ROW_TILE
    cnt = counts[:, 0].astype(jnp.int32)
    tiles_e = (cnt + ROW_TILE - 1) // ROW_TILE
    ends = jnp.cumsum(tiles_e)
    start = (ends - tiles_e) * ROW_TILE
    slots = _slots_call(start.astype(jnp.int32), eid, pos)
    xa = _sc_dispatch(ha, slots, p)
    xbb = _sc_dispatch(hb, slots, p)
    ya, yb = _ffn_call(start.astype(jnp.int32), tiles_e.astype(jnp.int32), xa, xbb, wg, wu, wd)
    ga = _sc_combine(ya, slots)
    gb = _sc_combine(yb, slots)
    return _final_call(xb, ga, gb, wk, mod_rows, t, tm=256)


def _rope_tables(t):
    n_rows = t // GRID_W
    rows = jnp.repeat(jnp.arange(n_rows), GRID_W).astype(F32)
    cols = jnp.tile(jnp.arange(GRID_W), n_rows).astype(F32)
    n_freq = HEAD_DIM // 4
    freqs = ROPE_THETA ** (-jnp.arange(n_freq, dtype=F32) / n_freq)
    ang = jnp.concatenate([rows[:, None] * freqs, cols[:, None] * freqs], axis=-1)
    ang = jnp.repeat(ang, 2, axis=-1)
    ang = jnp.concatenate([ang, ang], axis=-1)
    sign = jnp.where(jnp.arange(LANES) % 2 == 0, -1.0, 1.0).astype(F32)
    return jnp.cos(ang), jnp.sin(ang) * sign


def _dup_heads(a):
    s = a.shape[:-1]
    a4 = a.reshape(s + (N_KV_HEADS, HEAD_DIM))
    return jnp.concatenate([a4, a4], axis=-1).reshape(s + (KVD_W,))


def _prep_w_in(w_in):
    pad = jnp.zeros((D_MODEL, LANES - SSD_HEADS), w_in.dtype)
    dt0 = M_END
    g0 = dt0 + 2 * SSD_HEADS
    tail = [w_in[:, g0:g0 + 2 * D_MODEL], w_in[:, dt0:dt0 + SSD_HEADS], pad,
            w_in[:, dt0 + SSD_HEADS:g0], pad]
    return w_in[:, :M_END].astype(BF16), jnp.concatenate(tail, axis=-1).astype(BF16)


def _pad_heads(a):
    return jnp.pad(a.astype(F32), ((0, 0), (0, LANES - SSD_HEADS)))[:, None, :]


def _trunk(x, mod_rows, wts, rope_tabs, ctx_k, ctx_v, h0, *, tm, tq, want_state):
    b, t, _ = x.shape
    rope = rope_tabs is not None
    if rope:
        cos, sin = rope_tabs
    else:
        cos = sin = jnp.zeros((t, LANES), F32)
    res = _inproj_call(x, mod_rows, wts["g1"], *wts["w_in"], wts["qg"], wts["kg"], cos, sin,
                       rope=rope, emit_kv=want_state, tm=tm)
    q, k, v, gates, z, xbc, dt = res[:7]
    kv_raw = res[7:]
    attn = _attn_call(q, k, v, ctx_k, ctx_v, tq=tq)
    y_f, y_b, hfin = _ssd_call(xbc, dt, wts, h0, want_hfin=want_state)
    xb, ha, hb, eid, pos, wk, counts = _merge_call(
        x, attn, y_f, y_b, z, gates, mod_rows, wts["wa"], wts["ws"], wts["wo"], wts["sg"],
        wts["n2"], wts["wr_t"], wts["rb"], wts["wsg"], wts["wsu"], wts["wsd"], tm=MERGE_SUB)
    out = _moe_call(ha, hb, eid, pos, wk, counts, xb, mod_rows, wts["weg"], wts["weu"], wts["wed"], t)
    return out.reshape(b, t, D_MODEL), kv_raw, hfin


def kernel(x_prompt, x_sample, cache_k, cache_v, state_ssm, c, c_ctx, w_mod, b_mod, norm1_g, norm2_g, w_in,
           q_norm_g, k_norm_g, conv_w, conv_b, a_log, dt_bias, d_skip, ssd_norm_g, w_attn_proj, w_ssd_proj,
           w_out, w_router, router_bias, w_exp_gate, w_exp_up, w_exp_down, w_sh_gate, w_sh_up, w_sh_down):
    depth = w_mod.shape[0]
    assert depth == 1, "single trunk layer"
    bp, tp, _ = x_prompt.shape
    bs, ts, _ = x_sample.shape
    assert bs + 1 <= 8, "modulation rows are computed 8 at a time"
    l = 0
    cvec = jnp.concatenate([c_ctx[None, :], c, jnp.zeros((8 - 1 - bs, D_MODEL), F32)], axis=0)
    mod = _mod_call(cvec, w_mod.reshape(D_MODEL, 6 * D_MODEL), b_mod.reshape(1, 6 * D_MODEL))
    mod_prompt = mod[0:1][:, None, :]
    mod_sample = mod[1:1 + bs][:, None, :]

    lower = np.tril(np.ones((CHUNK, CHUNK), np.float32))
    head_sel = (np.arange(LANES)[:, None] == np.arange(D_INNER)[None, :] // SSD_HEAD_DIM).astype(np.float32)
    wts = dict(
        g1=norm1_g[l][None, :], n2=norm2_g[l][None, :],
        w_in=_prep_w_in(w_in.reshape(D_MODEL, w_in.shape[-1])),
        qg=jnp.tile(q_norm_g[l], 2)[None, :], kg=jnp.tile(k_norm_g[l], 2)[None, :],
        conv_w=conv_w[l], conv_b=conv_b[l][None, :],
        a_neg=_pad_heads(-jnp.exp(a_log[l].astype(F32))), dt_bias=_pad_heads(dt_bias[l]),
        dskip=jnp.repeat(d_skip[l].astype(F32), SSD_HEAD_DIM)[None, :],
        tri=jnp.asarray(np.stack([lower, lower.T])),
        head_sel=jnp.asarray(head_sel, BF16),
        sg=ssd_norm_g[l][None, :],
        wa=w_attn_proj[l].astype(BF16), ws=w_ssd_proj[l].astype(BF16), wo=w_out[l].astype(BF16),
        wr_t=w_router[l].T.astype(BF16), rb=router_bias[l].astype(F32)[:, None],
        wsg=w_sh_gate[l].astype(BF16), wsu=w_sh_up[l].astype(BF16), wsd=w_sh_down[l].astype(BF16),
        weg=w_exp_gate.reshape(w_exp_gate.shape[1:]), weu=w_exp_up.reshape(w_exp_up.shape[1:]),
        wed=w_exp_down.reshape(w_exp_down.shape[1:]),
    )

    y_prompt, (k_p, v_p), hfin = _trunk(x_prompt, mod_prompt, wts, None, None, None, None,
                                        tm=256, tq=256, want_state=True)
    new_k = k_p.reshape(bp, 1, tp, N_KV_HEADS, HEAD_DIM)
    new_v = v_p.reshape(bp, 1, tp, N_KV_HEADS, HEAD_DIM)
    new_state = hfin.reshape(bp, 1, 2, SSD_HEADS, SSD_HEAD_DIM, D_STATE)

    past = cache_k.shape[2]
    ctx_k = _dup_heads(cache_k[:, l].reshape(bs, past, KV_W)).astype(BF16)
    ctx_v = _dup_heads(cache_v[:, l].reshape(bs, past, KV_W)).astype(BF16)
    h0 = state_ssm[:, l].reshape(bs, 2, SSD_HEADS // 2, 2 * SSD_HEAD_DIM, D_STATE)
    y_sample, _, _ = _trunk(x_sample, mod_sample, wts, _rope_tables(ts), ctx_k, ctx_v, h0,
                               tm=512, tq=256, want_state=False)
    return (y_prompt, y_sample, new_k, new_v, new_state)
```

```python
import functools

import numpy as np
import jax
import jax.numpy as jnp
from jax import lax
from jax.experimental import pallas as pl
from jax.experimental.pallas import tpu as pltpu
from jax.experimental.pallas import tpu_sc as plsc

F32 = jnp.float32
BF16 = jnp.bfloat16

D_MODEL = 1024
GRID_W = 64
EPS = 1e-6
N_HEADS = 16
N_KV_HEADS = 4
HEAD_DIM = 64
ATTN_W = N_HEADS * HEAD_DIM
KV_W = N_KV_HEADS * HEAD_DIM
ROPE_THETA = 10000.0
D_INNER = 2048
SSD_HEAD_DIM = 64
SSD_HEADS = 32
SSD_GROUPS = 4
D_STATE = 128
D_CONV = 4
CHUNK = 128
CONV_CH = D_INNER + 2 * SSD_GROUPS * D_STATE
N_EXPERTS = 64
TOP_K = 8
N_EXPERT_GROUPS = 8
TOPK_GROUPS = 4
D_EXPERT = 256
D_SHARED = 256
ROUTED_SCALE = 2.5

LANES = 128
KVD_W = N_KV_HEADS * LANES
M_Q, M_K, M_V, M_Z, M_X, M_END = (int(c) for c in np.cumsum((0, ATTN_W, KV_W, KV_W, D_INNER, CONV_CH)))
T_G, T_DT, T_END = (int(c) for c in np.cumsum((0, 2 * D_MODEL, 2 * LANES)))
MOD_SHIFT1, MOD_SCALE1, MOD_GATE1, MOD_SHIFT2, MOD_SCALE2, MOD_GATE2 = range(6)
VMEM_LIMIT = 56 * 1024 * 1024
Q_SCALE = HEAD_DIM ** -0.5 * 1.4426950408889634


def _cparams(sem):
    return pltpu.CompilerParams(dimension_semantics=sem, vmem_limit_bytes=VMEM_LIMIT)


def _mod(mod_ref, which, lo=0, hi=D_MODEL):
    return mod_ref[:, which * D_MODEL + lo:which * D_MODEL + hi]


def _silu(x):
    return x * jax.nn.sigmoid(x)


def _bdot(a, b):
    return jnp.dot(a.astype(BF16), b.astype(BF16), preferred_element_type=F32)


def _bdot_nt(a, b):
    return lax.dot_general(a.astype(BF16), b.astype(BF16), (((1,), (1,)), ((), ())),
                           preferred_element_type=F32)


def _mod_kernel(c_ref, w_ref, b_ref, o_ref):
    o_ref[...] = _bdot(_silu(c_ref[...]), w_ref[...]) + b_ref[...]


def _mod_call(cvec, w_mod, b_mod):
    n = w_mod.shape[1]
    bn = 1024
    return pl.pallas_call(
        _mod_kernel,
        out_shape=jax.ShapeDtypeStruct((8, n), F32),
        grid=(n // bn,),
        in_specs=[pl.BlockSpec((8, D_MODEL), lambda j: (0, 0)),
                  pl.BlockSpec((D_MODEL, bn), lambda j: (0, j)),
                  pl.BlockSpec((1, bn), lambda j: (0, j))],
        out_specs=pl.BlockSpec((8, bn), lambda j: (0, j)),
        compiler_params=_cparams(("arbitrary",)),
        name="mod",
    )(cvec, w_mod, b_mod)


def _inproj_kernel(*refs, rope, emit_kv):
    if emit_kv:
        (x_ref, mod_ref, g1_ref, wm_ref, wt_ref, qg_ref, kg_ref, cos_ref, sin_ref,
         q_ref, k_ref, v_ref, gates_ref, z_ref, xbc_ref, dt_ref, kraw_ref, vraw_ref) = refs
    else:
        (x_ref, mod_ref, g1_ref, wm_ref, wt_ref, qg_ref, kg_ref, cos_ref, sin_ref,
         q_ref, k_ref, v_ref, gates_ref, z_ref, xbc_ref, dt_ref) = refs
    tm = x_ref.shape[0]
    x = x_ref[...]
    inv = lax.rsqrt(jnp.mean(x * x, axis=-1, keepdims=True) + EPS)
    h = (x * inv) * g1_ref[...]
    h = h * (1.0 + _mod(mod_ref, MOD_SCALE1)) + _mod(mod_ref, MOD_SHIFT1)
    hb = h.astype(BF16)

    lane = lax.broadcasted_iota(jnp.int32, (tm, LANES), 1)
    lo = lane < HEAD_DIM
    even = (lane & 1) == 0
    if rope:
        cos = cos_ref[...]
        sin = sin_ref[...]

    def rope_fn(blk):
        nxt = pltpu.roll(blk, LANES - 1, 1)
        prv = pltpu.roll(blk, 1, 1)
        return blk * cos + jnp.where(even, nxt, prv) * sin

    def head_norm(blk, g):
        sq = blk * blk
        s_all = jnp.sum(sq, axis=-1, keepdims=True)
        s_lo = jnp.sum(jnp.where(lo, sq, 0.0), axis=-1, keepdims=True)
        ms = jnp.where(lo, s_lo, s_all - s_lo) * (1.0 / HEAD_DIM)
        return blk * lax.rsqrt(ms + EPS) * g

    def dup_heads(blk):
        sw = pltpu.roll(blk, HEAD_DIM, 1)
        return jnp.where(lo, blk, sw), jnp.where(lo, sw, blk)

    qg = qg_ref[...]
    kg = kg_ref[...]
    q = jnp.dot(hb, wm_ref[:, M_Q:M_K], preferred_element_type=F32)
    for j in range(ATTN_W // LANES):
        blk = head_norm(q[:, j * LANES:(j + 1) * LANES], qg)
        if rope:
            blk = rope_fn(blk)
        q_ref[:, j * LANES:(j + 1) * LANES] = (blk * Q_SCALE).astype(q_ref.dtype)

    k = jnp.dot(hb, wm_ref[:, M_K:M_V], preferred_element_type=F32)
    v = jnp.dot(hb, wm_ref[:, M_V:M_Z], preferred_element_type=F32)
    for j in range(KV_W // LANES):
        kb = head_norm(k[:, j * LANES:(j + 1) * LANES], kg)
        vb = v[:, j * LANES:(j + 1) * LANES]
        if emit_kv:
            kraw_ref[:, j * LANES:(j + 1) * LANES] = kb
            vraw_ref[:, j * LANES:(j + 1) * LANES] = vb
        if rope:
            kb = rope_fn(kb)
        for i, (kd, vd) in enumerate(zip(dup_heads(kb), dup_heads(vb))):
            c0 = (2 * j + i) * LANES
            k_ref[:, c0:c0 + LANES] = kd.astype(k_ref.dtype)
            v_ref[:, c0:c0 + LANES] = vd.astype(v_ref.dtype)

    gates_ref[...] = jnp.dot(hb, wt_ref[:, T_G:T_DT], preferred_element_type=F32).astype(gates_ref.dtype)
    z_ref[...] = jnp.dot(hb, wm_ref[:, M_Z:M_X], preferred_element_type=F32).astype(z_ref.dtype)
    xbc_ref[...] = jnp.dot(hb, wm_ref[:, M_X:M_END], preferred_element_type=F32).astype(xbc_ref.dtype)
    dt_ref[...] = jnp.dot(hb, wt_ref[:, T_DT:T_END], preferred_element_type=F32)


def _inproj_call(x, mod_rows, g1, w_main, w_tail, qg, kg, cos, sin, *, rope, emit_kv, tm):
    b, t, _ = x.shape
    assert t % tm == 0, (t, tm)
    nt = t // tm
    tok = lambda width: pl.BlockSpec((None, tm, width), lambda bi, i: (bi, i, 0))
    const2 = lambda shape: pl.BlockSpec(shape, lambda bi, i: (0, 0))
    out_shape = [
        jax.ShapeDtypeStruct((b, t, ATTN_W), BF16),
        jax.ShapeDtypeStruct((b, t, KVD_W), BF16),
        jax.ShapeDtypeStruct((b, t, KVD_W), BF16),
        jax.ShapeDtypeStruct((b, t, 2 * D_MODEL), BF16),
        jax.ShapeDtypeStruct((b, t, D_INNER), BF16),
        jax.ShapeDtypeStruct((b, t, CONV_CH), BF16),
        jax.ShapeDtypeStruct((b, t, 2 * LANES), F32),
    ]
    out_specs = [tok(ATTN_W), tok(KVD_W), tok(KVD_W), tok(2 * D_MODEL), tok(D_INNER), tok(CONV_CH),
                 tok(2 * LANES)]
    if emit_kv:
        out_shape += [jax.ShapeDtypeStruct((b, t, KV_W), F32)] * 2
        out_specs += [tok(KV_W), tok(KV_W)]
    return pl.pallas_call(
        functools.partial(_inproj_kernel, rope=rope, emit_kv=emit_kv),
        out_shape=tuple(out_shape),
        grid=(b, nt),
        in_specs=[tok(D_MODEL),
                  pl.BlockSpec((None, 1, 6 * D_MODEL), lambda bi, i: (jnp.minimum(bi, mod_rows.shape[0] - 1), 0, 0)),
                  const2((1, D_MODEL)),
                  pl.BlockSpec((D_MODEL, M_END), lambda bi, i: (0, 0), pipeline_mode=pl.Buffered(1)),
                  pl.BlockSpec((D_MODEL, T_END), lambda bi, i: (0, 0), pipeline_mode=pl.Buffered(1)),
                  const2((1, LANES)), const2((1, LANES)),
                  pl.BlockSpec((tm, LANES), lambda bi, i: (i, 0)),
                  pl.BlockSpec((tm, LANES), lambda bi, i: (i, 0))],
        out_specs=tuple(out_specs),
        compiler_params=_cparams(("arbitrary", "arbitrary")),
        name="inproj",
    )(x, mod_rows, g1, w_main, w_tail, qg, kg, cos, sin)


KEY_CHUNK = 512


def _key_chunk(n):
    return KEY_CHUNK if n % KEY_CHUNK == 0 else n


def _attn_kernel(*refs, has_ctx):
    if has_ctx:
        q_ref, k_ref, v_ref, kctx_ref, vctx_ref, o_ref = refs
        sources = ((k_ref, v_ref), (kctx_ref, vctx_ref))
    else:
        q_ref, k_ref, v_ref, o_ref = refs
        sources = ((k_ref, v_ref),)
    tq = q_ref.shape[0]
    lane = lax.broadcasted_iota(jnp.int32, (tq, LANES), 1)
    lo = lane < HEAD_DIM
    qs = []
    for j in range(2):
        q2 = q_ref[:, j * LANES:(j + 1) * LANES]
        zero = jnp.zeros_like(q2)
        qs += [jnp.where(lo, q2, zero), jnp.where(lo, zero, q2)]
    q4 = jnp.concatenate(qs, axis=0)
    rows = 4 * tq
    m = jnp.full((rows, 1), -jnp.inf, F32)
    acc = jnp.zeros((rows, LANES), F32)
    chunks = [(kr, vr, c, _key_chunk(kr.shape[0])) for kr, vr in sources
              for c in range(kr.shape[0] // _key_chunk(kr.shape[0]))]
    for kr, vr, c, kc in chunks:
        kch = kr[c * kc:(c + 1) * kc, :].astype(BF16)
        vch = vr[c * kc:(c + 1) * kc, :].astype(BF16)
        lane_k = lax.broadcasted_iota(jnp.int32, (kc, LANES), 1)
        vch = jnp.where(lane_k < HEAD_DIM, vch, jnp.ones_like(vch))
        s = _bdot_nt(q4, kch)
        m_new = jnp.maximum(m, jnp.max(s, axis=-1, keepdims=True))
        alpha = jnp.exp2(m - m_new)
        p = jnp.exp2((s - m_new).astype(BF16))
        acc = acc * alpha + jnp.dot(p, vch, preferred_element_type=F32)
        m = m_new
    o = acc * (1.0 / pltpu.roll(acc, HEAD_DIM, 1))
    for j in range(2):
        oa = o[(2 * j) * tq:(2 * j + 1) * tq]
        ob = pltpu.roll(o[(2 * j + 1) * tq:(2 * j + 2) * tq], HEAD_DIM, 1)
        o_ref[:, j * LANES:(j + 1) * LANES] = jnp.where(lo, oa, ob).astype(o_ref.dtype)


def _attn_call(q, k, v, kctx, vctx, *, tq):
    b, t, _ = q.shape
    assert t % tq == 0, (t, tq)
    tk = k.shape[1]
    nq = t // tq
    has_ctx = kctx is not None
    kv_spec = lambda n: pl.BlockSpec((None, n, LANES), lambda bi, g, i: (bi, 0, g))
    in_specs = [pl.BlockSpec((None, tq, 2 * LANES), lambda bi, g, i: (bi, i, g)), kv_spec(tk), kv_spec(tk)]
    args = [q, k, v]
    if has_ctx:
        in_specs += [kv_spec(kctx.shape[1]), kv_spec(kctx.shape[1])]
        args += [kctx, vctx]
    return pl.pallas_call(
        functools.partial(_attn_kernel, has_ctx=has_ctx),
        out_shape=jax.ShapeDtypeStruct((b, t, ATTN_W), BF16),
        grid=(b, N_KV_HEADS, nq),
        in_specs=in_specs,
        out_specs=pl.BlockSpec((None, tq, 2 * LANES), lambda bi, g, i: (bi, i, g)),
        compiler_params=_cparams(("arbitrary", "arbitrary", "arbitrary")),
        name="attn",
    )(*args)


LOG2E = 1.4426950408889634


def _softplus(x):
    return jnp.maximum(x, 0.0) + jnp.log(1.0 + jnp.exp(-jnp.abs(x)))


def _ssd_kernel(*refs, nc, reverse, has_h0, want_hfin):
    refs = list(refs)
    conv = not reverse
    if conv:
        xbc_ref, prev_ref, next_ref, cw_ref, cb_ref, dsk_ref = refs[:6]
        refs = refs[6:]
    else:
        xc_ref = refs.pop(0)
    dt_ref, an_ref, dtb_ref, tri_ref, sel_ref = refs[:5]
    refs = refs[5:]
    h0_ref = refs.pop(0) if has_h0 else None
    hprev_ref = refs.pop(0) if (want_hfin and reverse) else None
    y_ref = refs.pop(0)
    xco_ref = refs.pop(0) if conv else None
    hfin_ref = refs.pop(0) if want_hfin else None
    h_scr = refs.pop(0)

    L = CHUNK
    c = pl.program_id(1)
    cidx = (nc - 1 - c) if reverse else c

    @pl.when(c == 0)
    def _():
        if has_h0:
            h_scr[...] = h0_ref[...]
        else:
            h_scr[...] = jnp.zeros_like(h_scr)

    row = lax.broadcasted_iota(jnp.int32, (L, LANES), 0)
    lane = lax.broadcasted_iota(jnp.int32, (L, LANES), 1)
    lo = lane < SSD_HEAD_DIM
    top = row < SSD_HEAD_DIM

    if conv:
        first = cidx == 0
        last = cidx == nc - 1

        def cols(a, w):
            xm = xbc_ref[:, a:a + w].astype(F32)
            rw = lax.broadcasted_iota(jnp.int32, (L, w), 0)
            p6 = jnp.where(first, 0.0, prev_ref[6:7, a:a + w].astype(F32))
            p7 = jnp.where(first, 0.0, prev_ref[7:8, a:a + w].astype(F32))
            n0 = jnp.where(last, 0.0, next_ref[0:1, a:a + w].astype(F32))
            r1 = jnp.where(rw == 0, p7, pltpu.roll(xm, 1, 0))
            r2 = jnp.where(rw == 0, p6, jnp.where(rw == 1, p7, pltpu.roll(xm, 2, 0)))
            rn = jnp.where(rw == L - 1, n0, pltpu.roll(xm, L - 1, 0))
            y = (r2 * cw_ref[0:1, a:a + w] + r1 * cw_ref[1:2, a:a + w] + xm * cw_ref[2:3, a:a + w]
                 + rn * cw_ref[3:4, a:a + w] + cb_ref[:, a:a + w])
            y = _silu(y).astype(BF16)
            xco_ref[:, a:a + w] = y
            return y
    else:
        def cols(a, w):
            return xc_ref[:, a:a + w]

    causal = tri_ref[...] > 0.0
    dt = _softplus(dt_ref[...] + dtb_ref[...])
    la2 = dt * (an_ref[...] * LOG2E)
    acum2 = jnp.dot(tri_ref[...], la2, preferred_element_type=F32, precision=lax.Precision.HIGHEST)
    dt_t = dt.T
    acum2_t = acum2.T
    tot2_t = jnp.sum(la2.T, axis=1, keepdims=True)
    lg_dt_t = jnp.log2(dt_t)
    r_t = lg_dt_t - acum2_t
    w_t = jnp.exp2(lg_dt_t + tot2_t - acum2_t)
    e_acum_x = jnp.dot(jnp.exp2(acum2).astype(BF16), sel_ref[...], preferred_element_type=F32)
    e_tot_t = jnp.exp2(tot2_t)

    for g in range(SSD_GROUPS):
        bgb = cols(D_INNER + g * D_STATE, D_STATE)
        cgb = cols(D_INNER + SSD_GROUPS * D_STATE + g * D_STATE, D_STATE)
        cbm = _bdot_nt(cgb, bgb)
        h_grp = h_scr[4 * g:4 * g + 4]
        yo_grp = _bdot_nt(cgb, h_grp.reshape(4 * LANES, D_STATE))
        xws, cds = [], []
        for pr in range(4):
            hp = g * 4 + pr
            ha, hb = 2 * hp, 2 * hp + 1
            xpb = cols(hp * LANES, LANES)
            zero = jnp.zeros_like(xpb)
            xs = jnp.concatenate([jnp.where(lo, xpb, zero), jnp.where(lo, zero, xpb)], axis=0)
            ms = []
            for hh in (ha, hb):
                e = jnp.exp2(acum2[:, hh:hh + 1] + r_t[hh:hh + 1, :])
                ms.append((cbm * jnp.where(causal, e, 0.0)).astype(BF16))
            y = jnp.dot(jnp.concatenate(ms, axis=1), xs, preferred_element_type=F32)
            y = y + yo_grp[:, pr * LANES:(pr + 1) * LANES] * e_acum_x[:, hp * LANES:(hp + 1) * LANES]
            if conv:
                y = y + dsk_ref[:, hp * LANES:(hp + 1) * LANES] * xpb.astype(F32)
            y_ref[:, hp * LANES:(hp + 1) * LANES] = y.astype(y_ref.dtype)
            wsel = jnp.where(top, w_t[ha:ha + 1, :], w_t[hb:hb + 1, :])
            xws.append((xpb.astype(F32).T * wsel).astype(BF16))
            cds.append(jnp.where(top, e_tot_t[ha:ha + 1, :], e_tot_t[hb:hb + 1, :]))
        st = jnp.dot(jnp.concatenate(xws, axis=0), bgb, preferred_element_type=F32)
        for pr in range(4):
            h_scr[g * 4 + pr] = h_grp[pr] * cds[pr] + st[pr * LANES:(pr + 1) * LANES]

    if want_hfin:
        @pl.when(c == nc - 1)
        def _():
            if reverse:
                hfin_ref[0] = hprev_ref[...]
                hfin_ref[1] = h_scr[...]
            else:
                hfin_ref[...] = h_scr[...]


def _ssd_sweep(xin, dt, wts, h0, hprev, *, reverse, want_hfin):
    b, t, _ = xin.shape
    assert t % CHUNK == 0, t
    nc = t // CHUNK
    has_h0 = h0 is not None
    rb = CHUNK // 8
    nrb = t // 8
    d = 1 if reverse else 0
    cmap = (lambda c: nc - 1 - c) if reverse else (lambda c: c)
    hshape = (SSD_HEADS // 2, 2 * SSD_HEAD_DIM, D_STATE)

    chunk_spec = pl.BlockSpec((None, CHUNK, CONV_CH), lambda bi, c: (bi, cmap(c), 0))
    if reverse:
        in_specs = [chunk_spec]
        args = [xin]
    else:
        in_specs = [
            chunk_spec,
            pl.BlockSpec((None, 8, CONV_CH), lambda bi, c: (bi, jnp.maximum(c * rb - 1, 0), 0)),
            pl.BlockSpec((None, 8, CONV_CH), lambda bi, c: (bi, jnp.minimum((c + 1) * rb, nrb - 1), 0)),
            pl.BlockSpec((D_CONV, CONV_CH), lambda bi, c: (0, 0)),
            pl.BlockSpec((1, CONV_CH), lambda bi, c: (0, 0)),
            pl.BlockSpec((1, D_INNER), lambda bi, c: (0, 0)),
        ]
        args = [xin, xin, xin, wts["conv_w"], wts["conv_b"], wts["dskip"]]
    in_specs += [
        pl.BlockSpec((None, CHUNK, LANES), lambda bi, c: (bi, cmap(c), d)),
        pl.BlockSpec((None, 1, LANES), lambda bi, c: (d, 0, 0)),
        pl.BlockSpec((None, 1, LANES), lambda bi, c: (d, 0, 0)),
        pl.BlockSpec((None, CHUNK, CHUNK), lambda bi, c: (d, 0, 0)),
        pl.BlockSpec((LANES, D_INNER), lambda bi, c: (0, 0)),
    ]
    args += [dt, wts["a_neg"], wts["dt_bias"], wts["tri"], wts["head_sel"]]
    if has_h0:
        in_specs.append(pl.BlockSpec((None, None) + hshape, lambda bi, c: (bi, d, 0, 0, 0)))
        args.append(h0)
    if want_hfin and reverse:
        in_specs.append(pl.BlockSpec((None,) + hshape, lambda bi, c: (bi, 0, 0, 0)))
        args.append(hprev)
    out_shape = [jax.ShapeDtypeStruct((b, t, D_INNER), BF16)]
    out_specs = [pl.BlockSpec((None, CHUNK, D_INNER), lambda bi, c: (bi, cmap(c), 0))]
    if not reverse:
        out_shape.append(jax.ShapeDtypeStruct((b, t, CONV_CH), BF16))
        out_specs.append(pl.BlockSpec((None, CHUNK, CONV_CH), lambda bi, c: (bi, c, 0)))
    if want_hfin and reverse:
        out_shape.append(jax.ShapeDtypeStruct((b, 2) + hshape, F32))
        out_specs.append(pl.BlockSpec((None, 2) + hshape, lambda bi, c: (bi, 0, 0, 0, 0)))
    elif want_hfin:
        out_shape.append(jax.ShapeDtypeStruct((b,) + hshape, F32))
        out_specs.append(pl.BlockSpec((None,) + hshape, lambda bi, c: (bi, 0, 0, 0)))
    return pl.pallas_call(
        functools.partial(_ssd_kernel, nc=nc, reverse=reverse, has_h0=has_h0, want_hfin=want_hfin),
        out_shape=tuple(out_shape),
        grid=(b, nc),
        in_specs=in_specs,
        out_specs=tuple(out_specs),
        scratch_shapes=[pltpu.VMEM(hshape, F32)],
        compiler_params=_cparams(("arbitrary", "arbitrary")),
        name="ssd_bwd" if reverse else "ssd_fwd",
    )(*args)


def _ssd_call(xbc, dt, wts, h0, *, want_hfin):
    res = _ssd_sweep(xbc, dt, wts, h0, None, reverse=False, want_hfin=want_hfin)
    y_f, xc = res[0], res[1]
    hf = res[2] if want_hfin else None
    res = _ssd_sweep(xc, dt, wts, h0, hf, reverse=True, want_hfin=want_hfin)
    return y_f, res[0], (res[1] if want_hfin else None)


def _route(logits_t, bias_col):
    e, n = logits_t.shape
    per = e // N_EXPERT_GROUPS
    scores = jax.nn.sigmoid(logits_t)
    sel = scores + bias_col
    neg = jnp.float32(-jnp.inf)
    gs = []
    for g in range(N_EXPERT_GROUPS):
        blk = sel[g * per:(g + 1) * per, :]
        m1 = jnp.max(blk, axis=0, keepdims=True)
        is_m1 = blk == m1
        cnt = jnp.sum(jnp.where(is_m1, 1.0, 0.0), axis=0, keepdims=True)
        m2 = jnp.max(jnp.where(is_m1, neg, blk), axis=0, keepdims=True)
        gs.append(m1 + jnp.where(cnt >= 2.0, m1, m2))
    keep = []
    for g in range(N_EXPERT_GROUPS):
        rank = jnp.zeros_like(gs[g])
        for j in range(N_EXPERT_GROUPS):
            if j == g:
                continue
            beats = (gs[j] > gs[g]) if j > g else (gs[j] >= gs[g])
            rank = rank + jnp.where(beats, 1.0, 0.0)
        keep.append(rank < float(TOPK_GROUPS))
    selm = jnp.concatenate(
        [jnp.where(keep[g], sel[g * per:(g + 1) * per, :], neg) for g in range(N_EXPERT_GROUPS)], axis=0)
    eidx = lax.broadcasted_iota(jnp.int32, (e, n), 0).astype(F32)
    cur = selm
    picks = []
    for _ in range(TOP_K):
        m = jnp.max(cur, axis=0, keepdims=True)
        idx = jnp.min(jnp.where(cur == m, eidx, float(e)), axis=0, keepdims=True)
        hit = eidx == idx
        picks.append((idx, hit))
        cur = jnp.where(hit, neg, cur)
    return scores, picks


def _pack_bf16_pairs(h):
    c = h.shape[1] // 2
    lo = pltpu.bitcast(h[:, :c].astype(BF16).astype(F32), jnp.uint32)
    hi = pltpu.bitcast(h[:, c:].astype(BF16).astype(F32), jnp.uint32)
    return (lo >> 16) | (hi & jnp.uint32(0xFFFF0000))


def _unpack_bf16_pairs(w):
    lo = pltpu.bitcast(w << 16, F32)
    hi = pltpu.bitcast(w & jnp.uint32(0xFFFF0000), F32)
    return lo, hi


def _rows8(rows):
    n = rows[0].shape[1]
    ridx = lax.broadcasted_iota(jnp.int32, (TOP_K, n), 0)
    out = jnp.zeros((TOP_K, n), rows[0].dtype)
    for k, r in enumerate(rows):
        out = jnp.where(ridx == k, r, out)
    return out


MERGE_SUB = 512


def _merge_kernel(x_ref, attn_ref, yf_ref, yb_ref, z_ref, gates_ref, mod_ref, wa_ref, ws_ref, wo_ref,
                  sg_ref, n2_ref, wr_ref, rb_ref, wsg_ref, wsu_ref, wsd_ref,
                  xb_ref, ha_ref, hb_ref, eid_ref, pos_ref, wk_ref, cnt_ref):
    tm = x_ref.shape[0]
    sub = MERGE_SUB

    @pl.when(pl.program_id(0) == 0)
    def _():
        cnt_ref[...] = jnp.zeros_like(cnt_ref)

    r_i = lax.broadcasted_iota(jnp.int32, (sub, sub), 0)
    c_i = lax.broadcasted_iota(jnp.int32, (sub, sub), 1)
    before = jnp.where(r_i < c_i, 1.0, 0.0).astype(BF16)
    cnt = cnt_ref[:, 0:1]

    for r0 in range(0, tm, sub):
        rs = slice(r0, r0 + sub)
        x = x_ref[rs, :]
        yy = yf_ref[rs, :].astype(F32) + yb_ref[rs, :].astype(F32)
        u = yy * _silu(z_ref[rs, :]).astype(F32)
        un = u * lax.rsqrt(jnp.mean(u * u, axis=-1, keepdims=True) + EPS) * sg_ref[...]
        ssd_o = _bdot(un, ws_ref[...])
        attn_o = jnp.dot(attn_ref[rs, :], wa_ref[...], preferred_element_type=F32)
        ga = jax.nn.sigmoid(gates_ref[rs, 0:D_MODEL]).astype(F32)
        gs = jax.nn.sigmoid(gates_ref[rs, D_MODEL:2 * D_MODEL]).astype(F32)
        mix = _bdot(ga * attn_o + gs * ssd_o, wo_ref[...])
        x1 = x + _mod(mod_ref, MOD_GATE1) * mix
        h2 = x1 * lax.rsqrt(jnp.mean(x1 * x1, axis=-1, keepdims=True) + EPS) * n2_ref[...]
        h2 = h2 * (1.0 + _mod(mod_ref, MOD_SCALE2)) + _mod(mod_ref, MOD_SHIFT2)
        h2b = h2.astype(BF16)
        ha_ref[rs, :] = _pack_bf16_pairs(h2[:, :D_MODEL // 2])
        hb_ref[rs, :] = _pack_bf16_pairs(h2[:, D_MODEL // 2:])

        logits_t = _bdot_nt(wr_ref[...], h2b)
        scores, picks = _route(logits_t, rb_ref[...])
        chosen = jnp.zeros_like(scores)
        for _, hit in picks:
            chosen = chosen + jnp.where(hit, 1.0, 0.0)
        pos = cnt + jnp.dot(chosen.astype(BF16), before, preferred_element_type=F32)
        cnt = cnt + jnp.sum(chosen, axis=1, keepdims=True)
        poss = [jnp.sum(jnp.where(hit, pos, 0.0), axis=0, keepdims=True) for _, hit in picks]
        wks = [jnp.sum(jnp.where(hit, scores, 0.0), axis=0, keepdims=True) for _, hit in picks]
        wsum = wks[0]
        for w in wks[1:]:
            wsum = wsum + w
        eid_ref[:, rs] = _rows8([idx for idx, _ in picks]).astype(jnp.int32)
        pos_ref[:, rs] = _rows8(poss).astype(jnp.int32)
        wk8 = _rows8(wks) / wsum * ROUTED_SCALE
        wk_ref[rs, :] = jnp.concatenate([wk8, jnp.zeros((LANES - TOP_K, sub), F32)], axis=0).T

        hid = _silu(jnp.dot(h2b, wsg_ref[...], preferred_element_type=F32)) * \
            jnp.dot(h2b, wsu_ref[...], preferred_element_type=F32)
        xb_ref[rs, :] = x1 + _mod(mod_ref, MOD_GATE2) * _bdot(hid, wsd_ref[...])

    cnt_ref[...] = jnp.broadcast_to(cnt, cnt_ref.shape)


def _merge_call(x, attn, y_f, y_b, z, gates, mod_rows, wa, ws, wo, sg, n2, wr_t, rb, wsg, wsu, wsd, *, tm):
    b, t, _ = x.shape
    n = b * t
    assert n % tm == 0 and (t % tm == 0 or mod_rows.shape[0] == 1), (n, t, tm)
    flat = lambda a: a.reshape(n, a.shape[-1])
    tok = lambda width: pl.BlockSpec((tm, width), lambda i: (i, 0))
    const2 = lambda shape: pl.BlockSpec(shape, lambda i: (0, 0), pipeline_mode=pl.Buffered(1))
    k8 = pl.BlockSpec((TOP_K, tm), lambda i: (0, i))
    half = D_MODEL // 4
    return pl.pallas_call(
        _merge_kernel,
        out_shape=(jax.ShapeDtypeStruct((n, D_MODEL), F32),
                   jax.ShapeDtypeStruct((n, half), jnp.uint32),
                   jax.ShapeDtypeStruct((n, half), jnp.uint32),
                   jax.ShapeDtypeStruct((TOP_K, n), jnp.int32),
                   jax.ShapeDtypeStruct((TOP_K, n), jnp.int32),
                   jax.ShapeDtypeStruct((n, LANES), F32),
                   jax.ShapeDtypeStruct((N_EXPERTS, LANES), F32)),
        grid=(n // tm,),
        in_specs=[tok(D_MODEL), tok(ATTN_W), tok(D_INNER), tok(D_INNER), tok(D_INNER), tok(2 * D_MODEL),
                  pl.BlockSpec((None, 1, 6 * D_MODEL), lambda i: (jnp.minimum((i * tm) // t, mod_rows.shape[0] - 1), 0, 0)),
                  const2((ATTN_W, D_MODEL)), const2((D_INNER, D_MODEL)), const2((D_MODEL, D_MODEL)),
                  const2((1, D_INNER)), const2((1, D_MODEL)),
                  const2((N_EXPERTS, D_MODEL)), const2((N_EXPERTS, 1)),
                  const2((D_MODEL, D_SHARED)), const2((D_MODEL, D_SHARED)), const2((D_SHARED, D_MODEL))],
        out_specs=(tok(D_MODEL), tok(half), tok(half), k8, k8, tok(LANES),
                   pl.BlockSpec((N_EXPERTS, LANES), lambda i: (0, 0))),
        compiler_params=_cparams(("arbitrary",)),
        name="merge",
    )(flat(x), flat(attn), flat(y_f), flat(y_b), flat(z), flat(gates), mod_rows, wa, ws, wo, sg, n2, wr_t, rb,
      wsg, wsu, wsd)


ROW_TILE = 512
SC_WINDOW = 128


def _slots_kernel(start_ref, eid_ref, pos_ref, slot_ref):
    eid = eid_ref[...]
    slot = pos_ref[...]
    for e in range(N_EXPERTS):
        slot = slot + jnp.where(eid == e, start_ref[e], 0)
    slot_ref[...] = slot


def _slots_call(start, eid, pos):
    n = eid.shape[1]
    bn = 2048 if n % 2048 == 0 else n
    spec = pl.BlockSpec((TOP_K, bn), lambda i, s: (0, i))
    return pl.pallas_call(
        _slots_kernel,
        out_shape=jax.ShapeDtypeStruct((TOP_K, n), jnp.int32),
        grid_spec=pltpu.PrefetchScalarGridSpec(num_scalar_prefetch=1, grid=(n // bn,),
                                               in_specs=[spec, spec], out_specs=spec),
        compiler_params=_cparams(("arbitrary",)),
        name="slots",
    )(start, eid, pos)


def _sc_dispatch(x, slots, p):
    n, d = x.shape
    assert n % SC_WINDOW == 0, n
    mesh = plsc.VectorSubcoreMesh(core_axis_name="core", subcore_axis_name="subcore")

    @functools.partial(pl.kernel, out_type=jax.ShapeDtypeStruct((p, d), x.dtype), mesh=mesh)
    def k(x_hbm, s_hbm, o_hbm):
        def body(x_vmem, s_vmem):
            for kk in range(TOP_K):
                pltpu.sync_copy(x_vmem, o_hbm.at[s_vmem.at[kk]])

        pltpu.emit_pipeline(
            body,
            grid=(n // SC_WINDOW,),
            in_specs=[pl.BlockSpec((SC_WINDOW, d), index_map=lambda i: (i, 0)),
                      pl.BlockSpec((TOP_K, SC_WINDOW), index_map=lambda i: (0, i))],
            out_specs=[],
            core_axis_name=("core", "subcore"),
            dimension_semantics=(pltpu.PARALLEL,),
        )(x_hbm, s_hbm)

    return k(x, slots)


def _sc_combine(y, slots):
    kk, n = slots.shape
    assert n % SC_WINDOW == 0, n
    d = y.shape[1]
    mesh = plsc.VectorSubcoreMesh(core_axis_name="core", subcore_axis_name="subcore")

    @functools.partial(pl.kernel, out_type=jax.ShapeDtypeStruct((kk * n, d), y.dtype), mesh=mesh)
    def k(y_hbm, s_hbm, o_hbm):
        def body(s_vmem, o_vmem):
            pltpu.sync_copy(y_hbm.at[s_vmem.at[0]], o_vmem)

        pltpu.emit_pipeline(
            body,
            grid=(kk * n // SC_WINDOW,),
            in_specs=[pl.BlockSpec((1, SC_WINDOW), index_map=lambda i: (0, i))],
            out_specs=[pl.BlockSpec((SC_WINDOW, d), index_map=lambda i: (i, 0))],
            core_axis_name=("core", "subcore"),
            dimension_semantics=(pltpu.PARALLEL,),
        )(s_hbm, o_hbm)

    return k(y, slots.reshape(1, kk * n)).reshape(kk, n, d)


FFN_IN_BUFS = 4
FFN_OUT_BUFS = 3


def _ffn_kernel(st_ref, nt_ref, wg_ref, wu_ref, wd_ref, xa_hbm, xb_hbm, ya_hbm, yb_hbm,
                wg_s, wu_s, wd_s, xa_buf, xb_buf, ya_buf, yb_buf, in_sem, out_sem, done_ref):
    e = pl.program_id(0)
    ne = pl.num_programs(0)
    nxt = jnp.minimum(e + 1, ne - 1)
    n = nt_ref[e]
    n_next = jnp.where(e + 1 < ne, nt_ref[nxt], 0)
    t = ROW_TILE
    ahead = FFN_IN_BUFS - 1

    def fetch(row, slot):
        r = pl.multiple_of(row, t)
        return (pltpu.make_async_copy(xa_hbm.at[pl.ds(r, t)], xa_buf.at[slot], in_sem.at[0, slot]),
                pltpu.make_async_copy(xb_hbm.at[pl.ds(r, t)], xb_buf.at[slot], in_sem.at[1, slot]))

    def put(row, slot):
        r = pl.multiple_of(row, t)
        return (pltpu.make_async_copy(ya_buf.at[slot], ya_hbm.at[pl.ds(r, t)], out_sem.at[0, slot]),
                pltpu.make_async_copy(yb_buf.at[slot], yb_hbm.at[pl.ds(r, t)], out_sem.at[1, slot]))

    def wait_put(slot):
        for c in put(0, slot):
            c.wait()

    def start_fetch(row, g):
        for c in fetch(row, lax.rem(g, FFN_IN_BUFS)):
            c.start()

    @pl.when(e == 0)
    def _():
        done_ref[0] = 0
        for k in range(ahead):
            @pl.when(k < n)
            def _():
                start_fetch(st_ref[0] + k * t, k)

    done = done_ref[0]

    def prefetch(i):
        j = i + ahead - n

        @pl.when(j < 0)
        def _():
            start_fetch(st_ref[e] + (i + ahead) * t, done + i + ahead)

        @pl.when(jnp.logical_and(j >= 0, j < n_next))
        def _():
            start_fetch(st_ref[nxt] + j * t, done + i + ahead)

    @pl.when(n > 0)
    def _():
        wg_s[...] = wg_ref[...].astype(BF16)
        wu_s[...] = wu_ref[...].astype(BF16)
        wd_s[...] = wd_ref[...].astype(BF16)
        base = st_ref[e]

        def body(i, carry):
            g = done + i
            si = lax.rem(g, FFN_IN_BUFS)
            so = lax.rem(g, FFN_OUT_BUFS)
            row = base + i * t
            for c in fetch(row, si):
                c.wait()
            prefetch(i)

            @pl.when(g >= FFN_OUT_BUFS)
            def _():
                wait_put(so)

            parts = _unpack_bf16_pairs(xa_buf[si]) + _unpack_bf16_pairs(xb_buf[si])
            x = jnp.concatenate(parts, axis=1).astype(BF16)
            hid = _silu(jnp.dot(x, wg_s[...], preferred_element_type=F32)) * \
                jnp.dot(x, wu_s[...], preferred_element_type=F32)
            y = _bdot(hid, wd_s[...])
            ya_buf[so] = _pack_bf16_pairs(y[:, :D_MODEL // 2])
            yb_buf[so] = _pack_bf16_pairs(y[:, D_MODEL // 2:])
            for c in put(row, so):
                c.start()
            return carry

        lax.fori_loop(0, n, body, 0)
        done_ref[0] = done + n

    for k in range(ahead):
        @pl.when(jnp.logical_and(k < ahead - n, k < n_next))
        def _():
            start_fetch(st_ref[nxt] + k * t, done + n + k)

    @pl.when(e == ne - 1)
    def _():
        total = done_ref[0]
        for k in range(FFN_OUT_BUFS):
            @pl.when(total - 1 - k >= 0)
            def _():
                wait_put(lax.rem(total - 1 - k, FFN_OUT_BUFS))


def _ffn_call(start, tiles_e, xa, xb, wg, wu, wd):
    p, half = xa.shape
    wspec = lambda s: pl.BlockSpec((None,) + s, lambda e, st, nt: (e, 0, 0))
    hbm = pl.BlockSpec(memory_space=pl.ANY)
    ibuf = pltpu.VMEM((FFN_IN_BUFS, ROW_TILE, half), jnp.uint32)
    obuf = pltpu.VMEM((FFN_OUT_BUFS, ROW_TILE, half), jnp.uint32)
    return pl.pallas_call(
        _ffn_kernel,
        out_shape=(jax.ShapeDtypeStruct((p, half), jnp.uint32), jax.ShapeDtypeStruct((p, half), jnp.uint32)),
        grid_spec=pltpu.PrefetchScalarGridSpec(
            num_scalar_prefetch=2, grid=(N_EXPERTS,),
            in_specs=[wspec((D_MODEL, D_EXPERT)), wspec((D_MODEL, D_EXPERT)), wspec((D_EXPERT, D_MODEL)),
                      hbm, hbm],
            out_specs=(hbm, hbm),
            scratch_shapes=[pltpu.VMEM((D_MODEL, D_EXPERT), BF16), pltpu.VMEM((D_MODEL, D_EXPERT), BF16),
                            pltpu.VMEM((D_EXPERT, D_MODEL), BF16), ibuf, ibuf, obuf, obuf,
                            pltpu.SemaphoreType.DMA((2, FFN_IN_BUFS)),
                            pltpu.SemaphoreType.DMA((2, FFN_OUT_BUFS)),
                            pltpu.SMEM((1,), jnp.int32)]),
        compiler_params=_cparams(("arbitrary",)),
        name="ffn",
    )(start, tiles_e, wg, wu, wd, xa, xb)


def _final_kernel(xb_ref, ya_ref, yb_ref, wk_ref, mod_ref, o_ref):
    q = D_MODEL // 4
    accs = [jnp.zeros((xb_ref.shape[0], q), F32) for _ in range(4)]
    for k in range(TOP_K):
        w = wk_ref[:, k:k + 1]
        parts = _unpack_bf16_pairs(ya_ref[k]) + _unpack_bf16_pairs(yb_ref[k])
        accs = [a + w * p for a, p in zip(accs, parts)]
    for i, a in enumerate(accs):
        o_ref[:, i * q:(i + 1) * q] = xb_ref[:, i * q:(i + 1) * q] + _mod(mod_ref, MOD_GATE2, i * q, (i + 1) * q) * a


def _final_call(xb, ya, yb, wk, mod_rows, t, *, tm):
    n = xb.shape[0]
    assert n % tm == 0 and (t % tm == 0 or mod_rows.shape[0] == 1), (n, t, tm)
    half = ya.shape[2]
    tok = lambda width: pl.BlockSpec((tm, width), lambda i: (i, 0))
    yspec = pl.BlockSpec((TOP_K, tm, half), lambda i: (0, i, 0))
    return pl.pallas_call(
        _final_kernel,
        out_shape=jax.ShapeDtypeStruct((n, D_MODEL), F32),
        grid=(n // tm,),
        in_specs=[tok(D_MODEL), yspec, yspec, tok(LANES),
                  pl.BlockSpec((None, 1, 6 * D_MODEL), lambda i: (jnp.minimum((i * tm) // t, mod_rows.shape[0] - 1), 0, 0))],
        out_specs=tok(D_MODEL),
        compiler_params=_cparams(("arbitrary",)),
        name="final",
    )(xb, ya, yb, wk, mod_rows)


def _moe_call(ha, hb, eid, pos, wk, counts, xb, mod_rows, wg, wu, wd, t):
    n = xb.shape[0]
    max_tiles = n * TOP_K // ROW_TILE + N_EXPERTS
    p = max_tiles * ROW_TILE
    cnt = counts[:, 0].astype(jnp.int32)
    tiles_e = (cnt + ROW_TILE - 1) // ROW_TILE
    ends = jnp.cumsum(tiles_e)
    start = (ends - tiles_e) * ROW_TILE
    slots = _slots_call(start.astype(jnp.int32), eid, pos)
    xa = _sc_dispatch(ha, slots, p)
    xbb = _sc_dispatch(hb, slots, p)
    ya, yb = _ffn_call(start.astype(jnp.int32), tiles_e.astype(jnp.int32), xa, xbb, wg, wu, wd)
    ga = _sc_combine(ya, slots)
    gb = _sc_combine(yb, slots)
    return _final_call(xb, ga, gb, wk, mod_rows, t, tm=256)


def _rope_tables(t):
    n_rows = t // GRID_W
    rows = np.repeat(np.arange(n_rows), GRID_W).astype(np.float32)
    cols = np.tile(np.arange(GRID_W), n_rows).astype(np.float32)
    n_freq = HEAD_DIM // 4
    freqs = (np.float32(ROPE_THETA) ** (-np.arange(n_freq, dtype=np.float32) / np.float32(n_freq))).astype(np.float32)
    ang = np.concatenate([rows[:, None] * freqs, cols[:, None] * freqs], axis=-1)
    ang = np.repeat(ang, 2, axis=-1)
    ang = np.concatenate([ang, ang], axis=-1).astype(np.float32)
    sign = np.where(np.arange(LANES) % 2 == 0, -1.0, 1.0).astype(np.float32)
    return jnp.asarray(np.cos(ang)), jnp.asarray(np.sin(ang) * sign)


def _dup_heads(a):
    s = a.shape[:-1]
    a4 = a.reshape(s + (N_KV_HEADS, HEAD_DIM))
    return jnp.concatenate([a4, a4], axis=-1).reshape(s + (KVD_W,))


def _prep_w_in(w_in):
    pad = jnp.zeros((D_MODEL, LANES - SSD_HEADS), w_in.dtype)
    dt0 = M_END
    g0 = dt0 + 2 * SSD_HEADS
    tail = [w_in[:, g0:g0 + 2 * D_MODEL], w_in[:, dt0:dt0 + SSD_HEADS], pad,
            w_in[:, dt0 + SSD_HEADS:g0], pad]
    return w_in[:, :M_END].astype(BF16), jnp.concatenate(tail, axis=-1).astype(BF16)


def _pad_heads(a):
    return jnp.pad(a.astype(F32), ((0, 0), (0, LANES - SSD_HEADS)))[:, None, :]


def _trunk(x, mod_rows, wts, rope_tabs, ctx_k, ctx_v, h0, *, tm, tq, want_state):
    b, t, _ = x.shape
    rope = rope_tabs is not None
    if rope:
        cos, sin = rope_tabs
    else:
        cos = sin = jnp.zeros((t, LANES), F32)
    res = _inproj_call(x, mod_rows, wts["g1"], *wts["w_in"], wts["qg"], wts["kg"], cos, sin,
                       rope=rope, emit_kv=want_state, tm=tm)
    q, k, v, gates, z, xbc, dt = res[:7]
    kv_raw = res[7:]
    attn = _attn_call(q, k, v, ctx_k, ctx_v, tq=tq)
    y_f, y_b, hfin = _ssd_call(xbc, dt, wts, h0, want_hfin=want_state)
    xb, ha, hb, eid, pos, wk, counts = _merge_call(
        x, attn, y_f, y_b, z, gates, mod_rows, wts["wa"], wts["ws"], wts["wo"], wts["sg"],
        wts["n2"], wts["wr_t"], wts["rb"], wts["wsg"], wts["wsu"], wts["wsd"], tm=MERGE_SUB)
    out = _moe_call(ha, hb, eid, pos, wk, counts, xb, mod_rows, wts["weg"], wts["weu"], wts["wed"], t)
    return out.reshape(b, t, D_MODEL), kv_raw, hfin


def kernel(x_prompt, x_sample, cache_k, cache_v, state_ssm, c, c_ctx, w_mod, b_mod, norm1_g, norm2_g, w_in,
           q_norm_g, k_norm_g, conv_w, conv_b, a_log, dt_bias, d_skip, ssd_norm_g, w_attn_proj, w_ssd_proj,
           w_out, w_router, router_bias, w_exp_gate, w_exp_up, w_exp_down, w_sh_gate, w_sh_up, w_sh_down):
    depth = w_mod.shape[0]
    assert depth == 1, "single trunk layer"
    bp, tp, _ = x_prompt.shape
    bs, ts, _ = x_sample.shape
    assert bs + 1 <= 8, "modulation rows are computed 8 at a time"
    l = 0
    cvec = jnp.concatenate([c_ctx[None, :], c, jnp.zeros((8 - 1 - bs, D_MODEL), F32)], axis=0)
    mod = _mod_call(cvec, w_mod.reshape(D_MODEL, 6 * D_MODEL), b_mod.reshape(1, 6 * D_MODEL))
    mod_prompt = mod[0:1][:, None, :]
    mod_sample = mod[1:1 + bs][:, None, :]

    lower = np.tril(np.ones((CHUNK, CHUNK), np.float32))
    head_sel = (np.arange(LANES)[:, None] == np.arange(D_INNER)[None, :] // SSD_HEAD_DIM).astype(np.float32)
    wts = dict(
        g1=norm1_g[l][None, :], n2=norm2_g[l][None, :],
        w_in=_prep_w_in(w_in.reshape(D_MODEL, w_in.shape[-1])),
        qg=jnp.tile(q_norm_g[l], 2)[None, :], kg=jnp.tile(k_norm_g[l], 2)[None, :],
        conv_w=conv_w[l], conv_b=conv_b[l][None, :],
        a_neg=_pad_heads(-jnp.exp(a_log[l].astype(F32))), dt_bias=_pad_heads(dt_bias[l]),
        dskip=jnp.repeat(d_skip[l].astype(F32), SSD_HEAD_DIM)[None, :],
        tri=jnp.asarray(np.stack([lower, lower.T])),
        head_sel=jnp.asarray(head_sel, BF16),
        sg=ssd_norm_g[l][None, :],
        wa=w_attn_proj[l].astype(BF16), ws=w_ssd_proj[l].astype(BF16), wo=w_out[l].astype(BF16),
        wr_t=w_router[l].T.astype(BF16), rb=router_bias[l].astype(F32)[:, None],
        wsg=w_sh_gate[l].astype(BF16), wsu=w_sh_up[l].astype(BF16), wsd=w_sh_down[l].astype(BF16),
        weg=w_exp_gate.reshape(w_exp_gate.shape[1:]), weu=w_exp_up.reshape(w_exp_up.shape[1:]),
        wed=w_exp_down.reshape(w_exp_down.shape[1:]),
    )

    y_prompt, (k_p, v_p), hfin = _trunk(x_prompt, mod_prompt, wts, None, None, None, None,
                                        tm=256, tq=256, want_state=True)
    new_k = k_p.reshape(bp, 1, tp, N_KV_HEADS, HEAD_DIM)
    new_v = v_p.reshape(bp, 1, tp, N_KV_HEADS, HEAD_DIM)
    new_state = hfin.reshape(bp, 1, 2, SSD_HEADS, SSD_HEAD_DIM, D_STATE)

    past = cache_k.shape[2]
    ctx_k = _dup_heads(cache_k[:, l].reshape(bs, past, KV_W)).astype(BF16)
    ctx_v = _dup_heads(cache_v[:, l].reshape(bs, past, KV_W)).astype(BF16)
    h0 = state_ssm[:, l].reshape(bs, 2, SSD_HEADS // 2, 2 * SSD_HEAD_DIM, D_STATE)
    y_sample, _, _ = _trunk(x_sample, mod_sample, wts, _rope_tables(ts), ctx_k, ctx_v, h0,
                               tm=512, tq=512, want_state=False)
    return (y_prompt, y_sample, new_k, new_v, new_state)
```

```python
import functools

import numpy as np
import jax
import jax.numpy as jnp
from jax import lax
from jax.experimental import pallas as pl
from jax.experimental.pallas import tpu as pltpu
from jax.experimental.pallas import tpu_sc as plsc

F32 = jnp.float32
BF16 = jnp.bfloat16

D_MODEL = 1024
GRID_W = 64
EPS = 1e-6
N_HEADS = 16
N_KV_HEADS = 4
HEAD_DIM = 64
ATTN_W = N_HEADS * HEAD_DIM
KV_W = N_KV_HEADS * HEAD_DIM
ROPE_THETA = 10000.0
D_INNER = 2048
SSD_HEAD_DIM = 64
SSD_HEADS = 32
SSD_GROUPS = 4
D_STATE = 128
D_CONV = 4
CHUNK = 128
CONV_CH = D_INNER + 2 * SSD_GROUPS * D_STATE
N_EXPERTS = 64
TOP_K = 8
N_EXPERT_GROUPS = 8
TOPK_GROUPS = 4
D_EXPERT = 256
D_SHARED = 256
ROUTED_SCALE = 2.5

LANES = 128
KVD_W = N_KV_HEADS * LANES
M_Q, M_K, M_V, M_Z, M_X, M_END = (int(c) for c in np.cumsum((0, ATTN_W, KV_W, KV_W, D_INNER, CONV_CH)))
T_G, T_DT, T_END = (int(c) for c in np.cumsum((0, 2 * D_MODEL, 2 * LANES)))
MOD_SHIFT1, MOD_SCALE1, MOD_GATE1, MOD_SHIFT2, MOD_SCALE2, MOD_GATE2 = range(6)
VMEM_LIMIT = 56 * 1024 * 1024
Q_SCALE = HEAD_DIM ** -0.5 * 1.4426950408889634


def _cparams(sem):
    return pltpu.CompilerParams(dimension_semantics=sem, vmem_limit_bytes=VMEM_LIMIT)


def _mod(mod_ref, which, lo=0, hi=D_MODEL):
    return mod_ref[:, which * D_MODEL + lo:which * D_MODEL + hi]


def _silu(x):
    return x * jax.nn.sigmoid(x)


def _bdot(a, b):
    return jnp.dot(a.astype(BF16), b.astype(BF16), preferred_element_type=F32)


def _bdot_nt(a, b):
    return lax.dot_general(a.astype(BF16), b.astype(BF16), (((1,), (1,)), ((), ())),
                           preferred_element_type=F32)


def _mod_kernel(c_ref, w_ref, b_ref, o_ref):
    o_ref[...] = _bdot(_silu(c_ref[...]), w_ref[...]) + b_ref[...]


def _mod_call(cvec, w_mod, b_mod):
    n = w_mod.shape[1]
    bn = 1024
    return pl.pallas_call(
        _mod_kernel,
        out_shape=jax.ShapeDtypeStruct((8, n), F32),
        grid=(n // bn,),
        in_specs=[pl.BlockSpec((8, D_MODEL), lambda j: (0, 0)),
                  pl.BlockSpec((D_MODEL, bn), lambda j: (0, j)),
                  pl.BlockSpec((1, bn), lambda j: (0, j))],
        out_specs=pl.BlockSpec((8, bn), lambda j: (0, j)),
        compiler_params=_cparams(("arbitrary",)),
        name="mod",
    )(cvec, w_mod, b_mod)


def _inproj_kernel(*refs, rope, emit_kv):
    if emit_kv:
        (x_ref, mod_ref, g1_ref, wm_ref, wt_ref, qg_ref, kg_ref, cos_ref, sin_ref,
         q_ref, k_ref, v_ref, gates_ref, z_ref, xbc_ref, dt_ref, kraw_ref, vraw_ref) = refs
    else:
        (x_ref, mod_ref, g1_ref, wm_ref, wt_ref, qg_ref, kg_ref, cos_ref, sin_ref,
         q_ref, k_ref, v_ref, gates_ref, z_ref, xbc_ref, dt_ref) = refs
    tm = x_ref.shape[0]
    x = x_ref[...]
    inv = lax.rsqrt(jnp.mean(x * x, axis=-1, keepdims=True) + EPS)
    h = (x * inv) * g1_ref[...]
    h = h * (1.0 + _mod(mod_ref, MOD_SCALE1)) + _mod(mod_ref, MOD_SHIFT1)
    hb = h.astype(BF16)

    lane = lax.broadcasted_iota(jnp.int32, (tm, LANES), 1)
    lo = lane < HEAD_DIM
    even = (lane & 1) == 0
    if rope:
        cos = cos_ref[...]
        sin = sin_ref[...]

    def rope_fn(blk):
        nxt = pltpu.roll(blk, LANES - 1, 1)
        prv = pltpu.roll(blk, 1, 1)
        return blk * cos + jnp.where(even, nxt, prv) * sin

    def head_norm(blk, g):
        sq = blk * blk
        s_all = jnp.sum(sq, axis=-1, keepdims=True)
        s_lo = jnp.sum(jnp.where(lo, sq, 0.0), axis=-1, keepdims=True)
        ms = jnp.where(lo, s_lo, s_all - s_lo) * (1.0 / HEAD_DIM)
        return blk * lax.rsqrt(ms + EPS) * g

    def dup_heads(blk):
        sw = pltpu.roll(blk, HEAD_DIM, 1)
        return jnp.where(lo, blk, sw), jnp.where(lo, sw, blk)

    qg = qg_ref[...]
    kg = kg_ref[...]
    q = jnp.dot(hb, wm_ref[:, M_Q:M_K], preferred_element_type=F32)
    for j in range(ATTN_W // LANES):
        blk = head_norm(q[:, j * LANES:(j + 1) * LANES], qg)
        if rope:
            blk = rope_fn(blk)
        q_ref[:, j * LANES:(j + 1) * LANES] = (blk * Q_SCALE).astype(q_ref.dtype)

    k = jnp.dot(hb, wm_ref[:, M_K:M_V], preferred_element_type=F32)
    v = jnp.dot(hb, wm_ref[:, M_V:M_Z], preferred_element_type=F32)
    for j in range(KV_W // LANES):
        kb = head_norm(k[:, j * LANES:(j + 1) * LANES], kg)
        vb = v[:, j * LANES:(j + 1) * LANES]
        if emit_kv:
            kraw_ref[:, j * LANES:(j + 1) * LANES] = kb
            vraw_ref[:, j * LANES:(j + 1) * LANES] = vb
        if rope:
            kb = rope_fn(kb)
        for i, (kd, vd) in enumerate(zip(dup_heads(kb), dup_heads(vb))):
            c0 = (2 * j + i) * LANES
            k_ref[:, c0:c0 + LANES] = kd.astype(k_ref.dtype)
            v_ref[:, c0:c0 + LANES] = vd.astype(v_ref.dtype)

    gates_ref[...] = jnp.dot(hb, wt_ref[:, T_G:T_DT], preferred_element_type=F32).astype(gates_ref.dtype)
    z_ref[...] = jnp.dot(hb, wm_ref[:, M_Z:M_X], preferred_element_type=F32).astype(z_ref.dtype)
    xbc_ref[...] = jnp.dot(hb, wm_ref[:, M_X:M_END], preferred_element_type=F32).astype(xbc_ref.dtype)
    dt_ref[...] = jnp.dot(hb, wt_ref[:, T_DT:T_END], preferred_element_type=F32)


def _inproj_call(x, mod_rows, g1, w_main, w_tail, qg, kg, cos, sin, *, rope, emit_kv, tm):
    b, t, _ = x.shape
    assert t % tm == 0, (t, tm)
    nt = t // tm
    tok = lambda width: pl.BlockSpec((None, tm, width), lambda bi, i: (bi, i, 0))
    const2 = lambda shape: pl.BlockSpec(shape, lambda bi, i: (0, 0))
    out_shape = [
        jax.ShapeDtypeStruct((b, t, ATTN_W), BF16),
        jax.ShapeDtypeStruct((b, t, KVD_W), BF16),
        jax.ShapeDtypeStruct((b, t, KVD_W), BF16),
        jax.ShapeDtypeStruct((b, t, 2 * D_MODEL), BF16),
        jax.ShapeDtypeStruct((b, t, D_INNER), BF16),
        jax.ShapeDtypeStruct((b, t, CONV_CH), BF16),
        jax.ShapeDtypeStruct((b, t, 2 * LANES), F32),
    ]
    out_specs = [tok(ATTN_W), tok(KVD_W), tok(KVD_W), tok(2 * D_MODEL), tok(D_INNER), tok(CONV_CH),
                 tok(2 * LANES)]
    if emit_kv:
        out_shape += [jax.ShapeDtypeStruct((b, t, KV_W), F32)] * 2
        out_specs += [tok(KV_W), tok(KV_W)]
    return pl.pallas_call(
        functools.partial(_inproj_kernel, rope=rope, emit_kv=emit_kv),
        out_shape=tuple(out_shape),
        grid=(b, nt),
        in_specs=[tok(D_MODEL),
                  pl.BlockSpec((None, 1, 6 * D_MODEL), lambda bi, i: (jnp.minimum(bi, mod_rows.shape[0] - 1), 0, 0)),
                  const2((1, D_MODEL)),
                  pl.BlockSpec((D_MODEL, M_END), lambda bi, i: (0, 0), pipeline_mode=pl.Buffered(1)),
                  pl.BlockSpec((D_MODEL, T_END), lambda bi, i: (0, 0), pipeline_mode=pl.Buffered(1)),
                  const2((1, LANES)), const2((1, LANES)),
                  pl.BlockSpec((tm, LANES), lambda bi, i: (i, 0)),
                  pl.BlockSpec((tm, LANES), lambda bi, i: (i, 0))],
        out_specs=tuple(out_specs),
        compiler_params=_cparams(("arbitrary", "arbitrary")),
        name="inproj",
    )(x, mod_rows, g1, w_main, w_tail, qg, kg, cos, sin)


KEY_CHUNK = 512


def _key_chunk(n):
    return KEY_CHUNK if n % KEY_CHUNK == 0 else n


def _attn_kernel(*refs, has_ctx):
    if has_ctx:
        q_ref, k_ref, v_ref, kctx_ref, vctx_ref, o_ref = refs
        sources = ((k_ref, v_ref), (kctx_ref, vctx_ref))
    else:
        q_ref, k_ref, v_ref, o_ref = refs
        sources = ((k_ref, v_ref),)
    tq = q_ref.shape[0]
    lane = lax.broadcasted_iota(jnp.int32, (tq, LANES), 1)
    lo = lane < HEAD_DIM
    qs = []
    for j in range(2):
        q2 = q_ref[:, j * LANES:(j + 1) * LANES]
        zero = jnp.zeros_like(q2)
        qs += [jnp.where(lo, q2, zero), jnp.where(lo, zero, q2)]
    q4 = jnp.concatenate(qs, axis=0)
    rows = 4 * tq
    m = jnp.full((rows, 1), -jnp.inf, F32)
    acc = jnp.zeros((rows, LANES), F32)
    chunks = [(kr, vr, c, _key_chunk(kr.shape[0])) for kr, vr in sources
              for c in range(kr.shape[0] // _key_chunk(kr.shape[0]))]
    for kr, vr, c, kc in chunks:
        kch = kr[c * kc:(c + 1) * kc, :].astype(BF16)
        vch = vr[c * kc:(c + 1) * kc, :].astype(BF16)
        lane_k = lax.broadcasted_iota(jnp.int32, (kc, LANES), 1)
        vch = jnp.where(lane_k < HEAD_DIM, vch, jnp.ones_like(vch))
        s = _bdot_nt(q4, kch)
        m_new = jnp.maximum(m, jnp.max(s, axis=-1, keepdims=True))
        alpha = jnp.exp2(m - m_new)
        p = jnp.exp2((s - m_new).astype(BF16))
        acc = acc * alpha + jnp.dot(p, vch, preferred_element_type=F32)
        m = m_new
    o = acc * (1.0 / pltpu.roll(acc, HEAD_DIM, 1))
    for j in range(2):
        oa = o[(2 * j) * tq:(2 * j + 1) * tq]
        ob = pltpu.roll(o[(2 * j + 1) * tq:(2 * j + 2) * tq], HEAD_DIM, 1)
        o_ref[:, j * LANES:(j + 1) * LANES] = jnp.where(lo, oa, ob).astype(o_ref.dtype)


def _attn_call(q, k, v, kctx, vctx, *, tq):
    b, t, _ = q.shape
    assert t % tq == 0, (t, tq)
    tk = k.shape[1]
    nq = t // tq
    has_ctx = kctx is not None
    kv_spec = lambda n: pl.BlockSpec((None, n, LANES), lambda bi, g, i: (bi, 0, g))
    in_specs = [pl.BlockSpec((None, tq, 2 * LANES), lambda bi, g, i: (bi, i, g)), kv_spec(tk), kv_spec(tk)]
    args = [q, k, v]
    if has_ctx:
        in_specs += [kv_spec(kctx.shape[1]), kv_spec(kctx.shape[1])]
        args += [kctx, vctx]
    return pl.pallas_call(
        functools.partial(_attn_kernel, has_ctx=has_ctx),
        out_shape=jax.ShapeDtypeStruct((b, t, ATTN_W), BF16),
        grid=(b, N_KV_HEADS, nq),
        in_specs=in_specs,
        out_specs=pl.BlockSpec((None, tq, 2 * LANES), lambda bi, g, i: (bi, i, g)),
        compiler_params=_cparams(("arbitrary", "arbitrary", "arbitrary")),
        name="attn",
    )(*args)


LOG2E = 1.4426950408889634


def _softplus(x):
    return jnp.maximum(x, 0.0) + jnp.log(1.0 + jnp.exp(-jnp.abs(x)))


def _ssd_kernel(*refs, nc, reverse, has_h0, want_hfin):
    refs = list(refs)
    conv = not reverse
    if conv:
        xbc_ref, prev_ref, next_ref, cw_ref, cb_ref, dsk_ref = refs[:6]
        refs = refs[6:]
    else:
        xc_ref = refs.pop(0)
    dt_ref, an_ref, dtb_ref, tri_ref, sel_ref = refs[:5]
    refs = refs[5:]
    h0_ref = refs.pop(0) if has_h0 else None
    hprev_ref = refs.pop(0) if (want_hfin and reverse) else None
    y_ref = refs.pop(0)
    xco_ref = refs.pop(0) if conv else None
    hfin_ref = refs.pop(0) if want_hfin else None
    h_scr = refs.pop(0)

    L = CHUNK
    c = pl.program_id(1)
    cidx = (nc - 1 - c) if reverse else c

    @pl.when(c == 0)
    def _():
        if has_h0:
            h_scr[...] = h0_ref[...]
        else:
            h_scr[...] = jnp.zeros_like(h_scr)

    row = lax.broadcasted_iota(jnp.int32, (L, LANES), 0)
    lane = lax.broadcasted_iota(jnp.int32, (L, LANES), 1)
    lo = lane < SSD_HEAD_DIM
    top = row < SSD_HEAD_DIM

    if conv:
        first = cidx == 0
        last = cidx == nc - 1

        def cols(a, w):
            xm = xbc_ref[:, a:a + w].astype(F32)
            rw = lax.broadcasted_iota(jnp.int32, (L, w), 0)
            p6 = jnp.where(first, 0.0, prev_ref[6:7, a:a + w].astype(F32))
            p7 = jnp.where(first, 0.0, prev_ref[7:8, a:a + w].astype(F32))
            n0 = jnp.where(last, 0.0, next_ref[0:1, a:a + w].astype(F32))
            r1 = jnp.where(rw == 0, p7, pltpu.roll(xm, 1, 0))
            r2 = jnp.where(rw == 0, p6, jnp.where(rw == 1, p7, pltpu.roll(xm, 2, 0)))
            rn = jnp.where(rw == L - 1, n0, pltpu.roll(xm, L - 1, 0))
            y = (r2 * cw_ref[0:1, a:a + w] + r1 * cw_ref[1:2, a:a + w] + xm * cw_ref[2:3, a:a + w]
                 + rn * cw_ref[3:4, a:a + w] + cb_ref[:, a:a + w])
            y = _silu(y).astype(BF16)
            xco_ref[:, a:a + w] = y
            return y
    else:
        def cols(a, w):
            return xc_ref[:, a:a + w]

    causal = tri_ref[...] > 0.0
    dt = _softplus(dt_ref[...] + dtb_ref[...])
    la2 = dt * (an_ref[...] * LOG2E)
    acum2 = jnp.dot(tri_ref[...], la2, preferred_element_type=F32, precision=lax.Precision.HIGHEST)
    dt_t = dt.T
    acum2_t = acum2.T
    tot2_t = jnp.sum(la2.T, axis=1, keepdims=True)
    lg_dt_t = jnp.log2(dt_t)
    r_t = lg_dt_t - acum2_t
    w_t = jnp.exp2(lg_dt_t + tot2_t - acum2_t)
    e_acum_x = jnp.dot(jnp.exp2(acum2).astype(BF16), sel_ref[...], preferred_element_type=F32)
    e_tot_t = jnp.exp2(tot2_t)

    for g in range(SSD_GROUPS):
        bgb = cols(D_INNER + g * D_STATE, D_STATE)
        cgb = cols(D_INNER + SSD_GROUPS * D_STATE + g * D_STATE, D_STATE)
        cbm = _bdot_nt(cgb, bgb)
        h_grp = h_scr[4 * g:4 * g + 4]
        yo_grp = _bdot_nt(cgb, h_grp.reshape(4 * LANES, D_STATE))
        xws, cds = [], []
        for pr in range(4):
            hp = g * 4 + pr
            ha, hb = 2 * hp, 2 * hp + 1
            xpb = cols(hp * LANES, LANES)
            zero = jnp.zeros_like(xpb)
            xs = jnp.concatenate([jnp.where(lo, xpb, zero), jnp.where(lo, zero, xpb)], axis=0)
            ms = []
            for hh in (ha, hb):
                e = jnp.exp2(acum2[:, hh:hh + 1] + r_t[hh:hh + 1, :])
                ms.append((cbm * jnp.where(causal, e, 0.0)).astype(BF16))
            y = jnp.dot(jnp.concatenate(ms, axis=1), xs, preferred_element_type=F32)
            y = y + yo_grp[:, pr * LANES:(pr + 1) * LANES] * e_acum_x[:, hp * LANES:(hp + 1) * LANES]
            if conv:
                y = y + dsk_ref[:, hp * LANES:(hp + 1) * LANES] * xpb.astype(F32)
            y_ref[:, hp * LANES:(hp + 1) * LANES] = y.astype(y_ref.dtype)
            wsel = jnp.where(top, w_t[ha:ha + 1, :], w_t[hb:hb + 1, :])
            xws.append((xpb.astype(F32).T * wsel).astype(BF16))
            cds.append(jnp.where(top, e_tot_t[ha:ha + 1, :], e_tot_t[hb:hb + 1, :]))
        st = jnp.dot(jnp.concatenate(xws, axis=0), bgb, preferred_element_type=F32)
        for pr in range(4):
            h_scr[g * 4 + pr] = h_grp[pr] * cds[pr] + st[pr * LANES:(pr + 1) * LANES]

    if want_hfin:
        @pl.when(c == nc - 1)
        def _():
            if reverse:
                hfin_ref[0] = hprev_ref[...]
                hfin_ref[1] = h_scr[...]
            else:
                hfin_ref[...] = h_scr[...]


def _ssd_sweep(xin, dt, wts, h0, hprev, *, reverse, want_hfin):
    b, t, _ = xin.shape
    assert t % CHUNK == 0, t
    nc = t // CHUNK
    has_h0 = h0 is not None
    rb = CHUNK // 8
    nrb = t // 8
    d = 1 if reverse else 0
    cmap = (lambda c: nc - 1 - c) if reverse else (lambda c: c)
    hshape = (SSD_HEADS // 2, 2 * SSD_HEAD_DIM, D_STATE)

    chunk_spec = pl.BlockSpec((None, CHUNK, CONV_CH), lambda bi, c: (bi, cmap(c), 0))
    if reverse:
        in_specs = [chunk_spec]
        args = [xin]
    else:
        in_specs = [
            chunk_spec,
            pl.BlockSpec((None, 8, CONV_CH), lambda bi, c: (bi, jnp.maximum(c * rb - 1, 0), 0)),
            pl.BlockSpec((None, 8, CONV_CH), lambda bi, c: (bi, jnp.minimum((c + 1) * rb, nrb - 1), 0)),
            pl.BlockSpec((D_CONV, CONV_CH), lambda bi, c: (0, 0)),
            pl.BlockSpec((1, CONV_CH), lambda bi, c: (0, 0)),
            pl.BlockSpec((1, D_INNER), lambda bi, c: (0, 0)),
        ]
        args = [xin, xin, xin, wts["conv_w"], wts["conv_b"], wts["dskip"]]
    in_specs += [
        pl.BlockSpec((None, CHUNK, LANES), lambda bi, c: (bi, cmap(c), d)),
        pl.BlockSpec((None, 1, LANES), lambda bi, c: (d, 0, 0)),
        pl.BlockSpec((None, 1, LANES), lambda bi, c: (d, 0, 0)),
        pl.BlockSpec((None, CHUNK, CHUNK), lambda bi, c: (d, 0, 0)),
        pl.BlockSpec((LANES, D_INNER), lambda bi, c: (0, 0)),
    ]
    args += [dt, wts["a_neg"], wts["dt_bias"], wts["tri"], wts["head_sel"]]
    if has_h0:
        in_specs.append(pl.BlockSpec((None, None) + hshape, lambda bi, c: (bi, d, 0, 0, 0)))
        args.append(h0)
    if want_hfin and reverse:
        in_specs.append(pl.BlockSpec((None,) + hshape, lambda bi, c: (bi, 0, 0, 0)))
        args.append(hprev)
    out_shape = [jax.ShapeDtypeStruct((b, t, D_INNER), BF16)]
    out_specs = [pl.BlockSpec((None, CHUNK, D_INNER), lambda bi, c: (bi, cmap(c), 0))]
    if not reverse:
        out_shape.append(jax.ShapeDtypeStruct((b, t, CONV_CH), BF16))
        out_specs.append(pl.BlockSpec((None, CHUNK, CONV_CH), lambda bi, c: (bi, c, 0)))
    if want_hfin and reverse:
        out_shape.append(jax.ShapeDtypeStruct((b, 2) + hshape, F32))
        out_specs.append(pl.BlockSpec((None, 2) + hshape, lambda bi, c: (bi, 0, 0, 0, 0)))
    elif want_hfin:
        out_shape.append(jax.ShapeDtypeStruct((b,) + hshape, F32))
        out_specs.append(pl.BlockSpec((None,) + hshape, lambda bi, c: (bi, 0, 0, 0)))
    return pl.pallas_call(
        functools.partial(_ssd_kernel, nc=nc, reverse=reverse, has_h0=has_h0, want_hfin=want_hfin),
        out_shape=tuple(out_shape),
        grid=(b, nc),
        in_specs=in_specs,
        out_specs=tuple(out_specs),
        scratch_shapes=[pltpu.VMEM(hshape, F32)],
        compiler_params=_cparams(("arbitrary", "arbitrary")),
        name="ssd_bwd" if reverse else "ssd_fwd",
    )(*args)


def _ssd_call(xbc, dt, wts, h0, *, want_hfin):
    res = _ssd_sweep(xbc, dt, wts, h0, None, reverse=False, want_hfin=want_hfin)
    y_f, xc = res[0], res[1]
    hf = res[2] if want_hfin else None
    res = _ssd_sweep(xc, dt, wts, h0, hf, reverse=True, want_hfin=want_hfin)
    return y_f, res[0], (res[1] if want_hfin else None)


def _route(logits_t, bias_col):
    e, n = logits_t.shape
    per = e // N_EXPERT_GROUPS
    scores = jax.nn.sigmoid(logits_t)
    sel = scores + bias_col
    neg = jnp.float32(-jnp.inf)
    gs = []
    for g in range(N_EXPERT_GROUPS):
        blk = sel[g * per:(g + 1) * per, :]
        m1 = jnp.max(blk, axis=0, keepdims=True)
        is_m1 = blk == m1
        cnt = jnp.sum(jnp.where(is_m1, 1.0, 0.0), axis=0, keepdims=True)
        m2 = jnp.max(jnp.where(is_m1, neg, blk), axis=0, keepdims=True)
        gs.append(m1 + jnp.where(cnt >= 2.0, m1, m2))
    keep = []
    for g in range(N_EXPERT_GROUPS):
        rank = jnp.zeros_like(gs[g])
        for j in range(N_EXPERT_GROUPS):
            if j == g:
                continue
            beats = (gs[j] > gs[g]) if j > g else (gs[j] >= gs[g])
            rank = rank + jnp.where(beats, 1.0, 0.0)
        keep.append(rank < float(TOPK_GROUPS))
    selm = jnp.concatenate(
        [jnp.where(keep[g], sel[g * per:(g + 1) * per, :], neg) for g in range(N_EXPERT_GROUPS)], axis=0)
    eidx = lax.broadcasted_iota(jnp.int32, (e, n), 0).astype(F32)
    cur = selm
    picks = []
    for _ in range(TOP_K):
        m = jnp.max(cur, axis=0, keepdims=True)
        idx = jnp.min(jnp.where(cur == m, eidx, float(e)), axis=0, keepdims=True)
        hit = eidx == idx
        picks.append((idx, hit))
        cur = jnp.where(hit, neg, cur)
    return scores, picks


def _pack_bf16_pairs(h):
    c = h.shape[1] // 2
    lo = pltpu.bitcast(h[:, :c].astype(BF16).astype(F32), jnp.uint32)
    hi = pltpu.bitcast(h[:, c:].astype(BF16).astype(F32), jnp.uint32)
    return (lo >> 16) | (hi & jnp.uint32(0xFFFF0000))


def _unpack_bf16_pairs(w):
    lo = pltpu.bitcast(w << 16, F32)
    hi = pltpu.bitcast(w & jnp.uint32(0xFFFF0000), F32)
    return lo, hi


def _rows8(rows):
    n = rows[0].shape[1]
    ridx = lax.broadcasted_iota(jnp.int32, (TOP_K, n), 0)
    out = jnp.zeros((TOP_K, n), rows[0].dtype)
    for k, r in enumerate(rows):
        out = jnp.where(ridx == k, r, out)
    return out


MERGE_SUB = 512


def _merge_kernel(x_ref, attn_ref, yf_ref, yb_ref, z_ref, gates_ref, mod_ref, wa_ref, ws_ref, wo_ref,
                  sg_ref, n2_ref, wr_ref, rb_ref, wsg_ref, wsu_ref, wsd_ref,
                  xb_ref, ha_ref, hb_ref, eid_ref, pos_ref, wk_ref, cnt_ref):
    tm = x_ref.shape[0]
    sub = MERGE_SUB

    @pl.when(pl.program_id(0) == 0)
    def _():
        cnt_ref[...] = jnp.zeros_like(cnt_ref)

    r_i = lax.broadcasted_iota(jnp.int32, (sub, sub), 0)
    c_i = lax.broadcasted_iota(jnp.int32, (sub, sub), 1)
    before = jnp.where(r_i < c_i, 1.0, 0.0).astype(BF16)
    cnt = cnt_ref[:, 0:1]

    for r0 in range(0, tm, sub):
        rs = slice(r0, r0 + sub)
        x = x_ref[rs, :]
        yy = yf_ref[rs, :].astype(F32) + yb_ref[rs, :].astype(F32)
        u = yy * _silu(z_ref[rs, :]).astype(F32)
        un = u * lax.rsqrt(jnp.mean(u * u, axis=-1, keepdims=True) + EPS) * sg_ref[...]
        ssd_o = _bdot(un, ws_ref[...])
        attn_o = jnp.dot(attn_ref[rs, :], wa_ref[...], preferred_element_type=F32)
        ga = jax.nn.sigmoid(gates_ref[rs, 0:D_MODEL]).astype(F32)
        gs = jax.nn.sigmoid(gates_ref[rs, D_MODEL:2 * D_MODEL]).astype(F32)
        mix = _bdot(ga * attn_o + gs * ssd_o, wo_ref[...])
        x1 = x + _mod(mod_ref, MOD_GATE1) * mix
        h2 = x1 * lax.rsqrt(jnp.mean(x1 * x1, axis=-1, keepdims=True) + EPS) * n2_ref[...]
        h2 = h2 * (1.0 + _mod(mod_ref, MOD_SCALE2)) + _mod(mod_ref, MOD_SHIFT2)
        h2b = h2.astype(BF16)
        ha_ref[rs, :] = _pack_bf16_pairs(h2[:, :D_MODEL // 2])
        hb_ref[rs, :] = _pack_bf16_pairs(h2[:, D_MODEL // 2:])

        logits_t = _bdot_nt(wr_ref[...], h2b)
        scores, picks = _route(logits_t, rb_ref[...])
        chosen = jnp.zeros_like(scores)
        for _, hit in picks:
            chosen = chosen + jnp.where(hit, 1.0, 0.0)
        pos = cnt + jnp.dot(chosen.astype(BF16), before, preferred_element_type=F32)
        cnt = cnt + jnp.sum(chosen, axis=1, keepdims=True)
        poss = [jnp.sum(jnp.where(hit, pos, 0.0), axis=0, keepdims=True) for _, hit in picks]
        wks = [jnp.sum(jnp.where(hit, scores, 0.0), axis=0, keepdims=True) for _, hit in picks]
        wsum = wks[0]
        for w in wks[1:]:
            wsum = wsum + w
        eid_ref[:, rs] = _rows8([idx for idx, _ in picks]).astype(jnp.int32)
        pos_ref[:, rs] = _rows8(poss).astype(jnp.int32)
        wk8 = _rows8(wks) / wsum * ROUTED_SCALE
        wk_ref[rs, :] = jnp.concatenate([wk8, jnp.zeros((LANES - TOP_K, sub), F32)], axis=0).T

        hid = _silu(jnp.dot(h2b, wsg_ref[...], preferred_element_type=F32)) * \
            jnp.dot(h2b, wsu_ref[...], preferred_element_type=F32)
        xb_ref[rs, :] = x1 + _mod(mod_ref, MOD_GATE2) * _bdot(hid, wsd_ref[...])

    cnt_ref[...] = jnp.broadcast_to(cnt, cnt_ref.shape)


def _merge_call(x, attn, y_f, y_b, z, gates, mod_rows, wa, ws, wo, sg, n2, wr_t, rb, wsg, wsu, wsd, *, tm):
    b, t, _ = x.shape
    n = b * t
    assert n % tm == 0 and (t % tm == 0 or mod_rows.shape[0] == 1), (n, t, tm)
    flat = lambda a: a.reshape(n, a.shape[-1])
    tok = lambda width: pl.BlockSpec((tm, width), lambda i: (i, 0))
    const2 = lambda shape: pl.BlockSpec(shape, lambda i: (0, 0), pipeline_mode=pl.Buffered(1))
    k8 = pl.BlockSpec((TOP_K, tm), lambda i: (0, i))
    half = D_MODEL // 4
    return pl.pallas_call(
        _merge_kernel,
        out_shape=(jax.ShapeDtypeStruct((n, D_MODEL), F32),
                   jax.ShapeDtypeStruct((n, half), jnp.uint32),
                   jax.ShapeDtypeStruct((n, half), jnp.uint32),
                   jax.ShapeDtypeStruct((TOP_K, n), jnp.int32),
                   jax.ShapeDtypeStruct((TOP_K, n), jnp.int32),
                   jax.ShapeDtypeStruct((n, LANES), F32),
                   jax.ShapeDtypeStruct((N_EXPERTS, LANES), F32)),
        grid=(n // tm,),
        in_specs=[tok(D_MODEL), tok(ATTN_W), tok(D_INNER), tok(D_INNER), tok(D_INNER), tok(2 * D_MODEL),
                  pl.BlockSpec((None, 1, 6 * D_MODEL), lambda i: (jnp.minimum((i * tm) // t, mod_rows.shape[0] - 1), 0, 0)),
                  const2((ATTN_W, D_MODEL)), const2((D_INNER, D_MODEL)), const2((D_MODEL, D_MODEL)),
                  const2((1, D_INNER)), const2((1, D_MODEL)),
                  const2((N_EXPERTS, D_MODEL)), const2((N_EXPERTS, 1)),
                  const2((D_MODEL, D_SHARED)), const2((D_MODEL, D_SHARED)), const2((D_SHARED, D_MODEL))],
        out_specs=(tok(D_MODEL), tok(half), tok(half), k8, k8, tok(LANES),
                   pl.BlockSpec((N_EXPERTS, LANES), lambda i: (0, 0))),
        compiler_params=_cparams(("arbitrary",)),
        name="merge",
    )(flat(x), flat(attn), flat(y_f), flat(y_b), flat(z), flat(gates), mod_rows, wa, ws, wo, sg, n2, wr_t, rb,
      wsg, wsu, wsd)


ROW_TILE = 512
SC_WINDOW = 128


def _slots_kernel(start_ref, eid_ref, pos_ref, slot_ref):
    eid = eid_ref[...]
    slot = pos_ref[...]
    for e in range(N_EXPERTS):
        slot = slot + jnp.where(eid == e, start_ref[e], 0)
    slot_ref[...] = slot


def _slots_call(start, eid, pos):
    n = eid.shape[1]
    bn = 2048 if n % 2048 == 0 else n
    spec = pl.BlockSpec((TOP_K, bn), lambda i, s: (0, i))
    return pl.pallas_call(
        _slots_kernel,
        out_shape=jax.ShapeDtypeStruct((TOP_K, n), jnp.int32),
        grid_spec=pltpu.PrefetchScalarGridSpec(num_scalar_prefetch=1, grid=(n // bn,),
                                               in_specs=[spec, spec], out_specs=spec),
        compiler_params=_cparams(("arbitrary",)),
        name="slots",
    )(start, eid, pos)


def _sc_dispatch(x, slots, p):
    n, d = x.shape
    assert n % SC_WINDOW == 0, n
    mesh = plsc.VectorSubcoreMesh(core_axis_name="core", subcore_axis_name="subcore")

    @functools.partial(pl.kernel, out_type=jax.ShapeDtypeStruct((p, d), x.dtype), mesh=mesh)
    def k(x_hbm, s_hbm, o_hbm):
        def body(x_vmem, s_vmem):
            for kk in range(TOP_K):
                pltpu.sync_copy(x_vmem, o_hbm.at[s_vmem.at[kk]])

        pltpu.emit_pipeline(
            body,
            grid=(n // SC_WINDOW,),
            in_specs=[pl.BlockSpec((SC_WINDOW, d), index_map=lambda i: (i, 0)),
                      pl.BlockSpec((TOP_K, SC_WINDOW), index_map=lambda i: (0, i))],
            out_specs=[],
            core_axis_name=("core", "subcore"),
            dimension_semantics=(pltpu.PARALLEL,),
        )(x_hbm, s_hbm)

    return k(x, slots)


def _sc_combine(y, slots):
    kk, n = slots.shape
    assert n % SC_WINDOW == 0, n
    d = y.shape[1]
    mesh = plsc.VectorSubcoreMesh(core_axis_name="core", subcore_axis_name="subcore")

    @functools.partial(pl.kernel, out_type=jax.ShapeDtypeStruct((kk * n, d), y.dtype), mesh=mesh)
    def k(y_hbm, s_hbm, o_hbm):
        def body(s_vmem, o_vmem):
            pltpu.sync_copy(y_hbm.at[s_vmem.at[0]], o_vmem)

        pltpu.emit_pipeline(
            body,
            grid=(kk * n // SC_WINDOW,),
            in_specs=[pl.BlockSpec((1, SC_WINDOW), index_map=lambda i: (0, i))],
            out_specs=[pl.BlockSpec((SC_WINDOW, d), index_map=lambda i: (i, 0))],
            core_axis_name=("core", "subcore"),
            dimension_semantics=(pltpu.PARALLEL,),
        )(s_hbm, o_hbm)

    return k(y, slots.reshape(1, kk * n)).reshape(kk, n, d)


FFN_IN_BUFS = 4
FFN_OUT_BUFS = 3


def _ffn_kernel(st_ref, nt_ref, wg_ref, wu_ref, wd_ref, xa_hbm, xb_hbm, ya_hbm, yb_hbm,
                wg_s, wu_s, wd_s, xa_buf, xb_buf, ya_buf, yb_buf, in_sem, out_sem, done_ref):
    e = pl.program_id(0)
    ne = pl.num_programs(0)
    nxt = jnp.minimum(e + 1, ne - 1)
    n = nt_ref[e]
    n_next = jnp.where(e + 1 < ne, nt_ref[nxt], 0)
    t = ROW_TILE
    ahead = FFN_IN_BUFS - 1

    def fetch(row, slot):
        r = pl.multiple_of(row, t)
        return (pltpu.make_async_copy(xa_hbm.at[pl.ds(r, t)], xa_buf.at[slot], in_sem.at[0, slot]),
                pltpu.make_async_copy(xb_hbm.at[pl.ds(r, t)], xb_buf.at[slot], in_sem.at[1, slot]))

    def put(row, slot):
        r = pl.multiple_of(row, t)
        return (pltpu.make_async_copy(ya_buf.at[slot], ya_hbm.at[pl.ds(r, t)], out_sem.at[0, slot]),
                pltpu.make_async_copy(yb_buf.at[slot], yb_hbm.at[pl.ds(r, t)], out_sem.at[1, slot]))

    def wait_put(slot):
        for c in put(0, slot):
            c.wait()

    def start_fetch(row, g):
        for c in fetch(row, lax.rem(g, FFN_IN_BUFS)):
            c.start()

    @pl.when(e == 0)
    def _():
        done_ref[0] = 0
        for k in range(ahead):
            @pl.when(k < n)
            def _():
                start_fetch(st_ref[0] + k * t, k)

    done = done_ref[0]

    def prefetch(i):
        j = i + ahead - n

        @pl.when(j < 0)
        def _():
            start_fetch(st_ref[e] + (i + ahead) * t, done + i + ahead)

        @pl.when(jnp.logical_and(j >= 0, j < n_next))
        def _():
            start_fetch(st_ref[nxt] + j * t, done + i + ahead)

    @pl.when(n > 0)
    def _():
        wg_s[...] = wg_ref[...].astype(BF16)
        wu_s[...] = wu_ref[...].astype(BF16)
        wd_s[...] = wd_ref[...].astype(BF16)
        base = st_ref[e]

        def body(i, carry):
            g = done + i
            si = lax.rem(g, FFN_IN_BUFS)
            so = lax.rem(g, FFN_OUT_BUFS)
            row = base + i * t
            for c in fetch(row, si):
                c.wait()
            prefetch(i)

            @pl.when(g >= FFN_OUT_BUFS)
            def _():
                wait_put(so)

            parts = _unpack_bf16_pairs(xa_buf[si]) + _unpack_bf16_pairs(xb_buf[si])
            x = jnp.concatenate(parts, axis=1).astype(BF16)
            hid = _silu(jnp.dot(x, wg_s[...], preferred_element_type=F32)) * \
                jnp.dot(x, wu_s[...], preferred_element_type=F32)
            y = _bdot(hid, wd_s[...])
            ya_buf[so] = _pack_bf16_pairs(y[:, :D_MODEL // 2])
            yb_buf[so] = _pack_bf16_pairs(y[:, D_MODEL // 2:])
            for c in put(row, so):
                c.start()
            return carry

        lax.fori_loop(0, n, body, 0)
        done_ref[0] = done + n

    for k in range(ahead):
        @pl.when(jnp.logical_and(k < ahead - n, k < n_next))
        def _():
            start_fetch(st_ref[nxt] + k * t, done + n + k)

    @pl.when(e == ne - 1)
    def _():
        total = done_ref[0]
        for k in range(FFN_OUT_BUFS):
            @pl.when(total - 1 - k >= 0)
            def _():
                wait_put(lax.rem(total - 1 - k, FFN_OUT_BUFS))


def _ffn_call(start, tiles_e, xa, xb, wg, wu, wd):
    p, half = xa.shape
    wspec = lambda s: pl.BlockSpec((None,) + s, lambda e, st, nt: (e, 0, 0))
    hbm = pl.BlockSpec(memory_space=pl.ANY)
    ibuf = pltpu.VMEM((FFN_IN_BUFS, ROW_TILE, half), jnp.uint32)
    obuf = pltpu.VMEM((FFN_OUT_BUFS, ROW_TILE, half), jnp.uint32)
    return pl.pallas_call(
        _ffn_kernel,
        out_shape=(jax.ShapeDtypeStruct((p, half), jnp.uint32), jax.ShapeDtypeStruct((p, half), jnp.uint32)),
        grid_spec=pltpu.PrefetchScalarGridSpec(
            num_scalar_prefetch=2, grid=(N_EXPERTS,),
            in_specs=[wspec((D_MODEL, D_EXPERT)), wspec((D_MODEL, D_EXPERT)), wspec((D_EXPERT, D_MODEL)),
                      hbm, hbm],
            out_specs=(hbm, hbm),
            scratch_shapes=[pltpu.VMEM((D_MODEL, D_EXPERT), BF16), pltpu.VMEM((D_MODEL, D_EXPERT), BF16),
                            pltpu.VMEM((D_EXPERT, D_MODEL), BF16), ibuf, ibuf, obuf, obuf,
                            pltpu.SemaphoreType.DMA((2, FFN_IN_BUFS)),
                            pltpu.SemaphoreType.DMA((2, FFN_OUT_BUFS)),
                            pltpu.SMEM((1,), jnp.int32)]),
        compiler_params=_cparams(("arbitrary",)),
        name="ffn",
    )(start, tiles_e, wg, wu, wd, xa, xb)


def _final_kernel(xb_ref, ya_ref, yb_ref, wk_ref, mod_ref, o_ref):
    q = D_MODEL // 4
    accs = [jnp.zeros((xb_ref.shape[0], q), F32) for _ in range(4)]
    for k in range(TOP_K):
        w = wk_ref[:, k:k + 1]
        parts = _unpack_bf16_pairs(ya_ref[k]) + _unpack_bf16_pairs(yb_ref[k])
        accs = [a + w * p for a, p in zip(accs, parts)]
    for i, a in enumerate(accs):
        o_ref[:, i * q:(i + 1) * q] = xb_ref[:, i * q:(i + 1) * q] + _mod(mod_ref, MOD_GATE2, i * q, (i + 1) * q) * a


def _final_call(xb, ya, yb, wk, mod_rows, t, *, tm):
    n = xb.shape[0]
    assert n % tm == 0 and (t % tm == 0 or mod_rows.shape[0] == 1), (n, t, tm)
    half = ya.shape[2]
    tok = lambda width: pl.BlockSpec((tm, width), lambda i: (i, 0))
    yspec = pl.BlockSpec((TOP_K, tm, half), lambda i: (0, i, 0))
    return pl.pallas_call(
        _final_kernel,
        out_shape=jax.ShapeDtypeStruct((n, D_MODEL), F32),
        grid=(n // tm,),
        in_specs=[tok(D_MODEL), yspec, yspec, tok(LANES),
                  pl.BlockSpec((None, 1, 6 * D_MODEL), lambda i: (jnp.minimum((i * tm) // t, mod_rows.shape[0] - 1), 0, 0))],
        out_specs=tok(D_MODEL),
        compiler_params=_cparams(("arbitrary",)),
        name="final",
    )(xb, ya, yb, wk, mod_rows)


def _moe_call(ha, hb, eid, pos, wk, counts, xb, mod_rows, wg, wu, wd, t):
    n = xb.shape[0]
    max_tiles = n * TOP_K // ROW_TILE + N_EXPERTS
    p = max_tiles * ROW_TILE
    cnt = counts[:, 0].astype(jnp.int32)
    tiles_e = (cnt + ROW_TILE - 1) // ROW_TILE
    ends = jnp.cumsum(tiles_e)
    start = (ends - tiles_e) * ROW_TILE
    slots = _slots_call(start.astype(jnp.int32), eid, pos)
    xa = _sc_dispatch(ha, slots, p)
    xbb = _sc_dispatch(hb, slots, p)
    ya, yb = _ffn_call(start.astype(jnp.int32), tiles_e.astype(jnp.int32), xa, xbb, wg, wu, wd)
    ga = _sc_combine(ya, slots)
    gb = _sc_combine(yb, slots)
    return _final_call(xb, ga, gb, wk, mod_rows, t, tm=256)


def _rope_tables(t):
    n_rows = t // GRID_W
    rows = np.repeat(np.arange(n_rows), GRID_W).astype(np.float32)
    cols = np.tile(np.arange(GRID_W), n_rows).astype(np.float32)
    n_freq = HEAD_DIM // 4
    freqs = (np.float32(ROPE_THETA) ** (-np.arange(n_freq, dtype=np.float32) / np.float32(n_freq))).astype(np.float32)
    ang = np.concatenate([rows[:, None] * freqs, cols[:, None] * freqs], axis=-1)
    ang = np.repeat(ang, 2, axis=-1)
    ang = np.concatenate([ang, ang], axis=-1).astype(np.float32)
    sign = np.where(np.arange(LANES) % 2 == 0, -1.0, 1.0).astype(np.float32)
    return jnp.asarray(np.cos(ang)), jnp.asarray(np.sin(ang) * sign)


def _dup_heads(a):
    s = a.shape[:-1]
    a4 = a.reshape(s + (N_KV_HEADS, HEAD_DIM))
    return jnp.concatenate([a4, a4], axis=-1).reshape(s + (KVD_W,))


def _prep_w_in(w_in):
    pad = jnp.zeros((D_MODEL, LANES - SSD_HEADS), w_in.dtype)
    dt0 = M_END
    g0 = dt0 + 2 * SSD_HEADS
    tail = [w_in[:, g0:g0 + 2 * D_MODEL], w_in[:, dt0:dt0 + SSD_HEADS], pad,
            w_in[:, dt0 + SSD_HEADS:g0], pad]
    return w_in[:, :M_END].astype(BF16), jnp.concatenate(tail, axis=-1).astype(BF16)


def _pad_heads(a):
    return jnp.pad(a.astype(F32), ((0, 0), (0, LANES - SSD_HEADS)))[:, None, :]


def _trunk(x, mod_rows, wts, rope_tabs, ctx_k, ctx_v, h0, *, tm, tq, want_state):
    b, t, _ = x.shape
    rope = rope_tabs is not None
    if rope:
        cos, sin = rope_tabs
    else:
        cos = sin = jnp.zeros((t, LANES), F32)
    res = _inproj_call(x, mod_rows, wts["g1"], *wts["w_in"], wts["qg"], wts["kg"], cos, sin,
                       rope=rope, emit_kv=want_state, tm=tm)
    q, k, v, gates, z, xbc, dt = res[:7]
    kv_raw = res[7:]
    attn = _attn_call(q, k, v, ctx_k, ctx_v, tq=tq)
    y_f, y_b, hfin = _ssd_call(xbc, dt, wts, h0, want_hfin=want_state)
    xb, ha, hb, eid, pos, wk, counts = _merge_call(
        x, attn, y_f, y_b, z, gates, mod_rows, wts["wa"], wts["ws"], wts["wo"], wts["sg"],
        wts["n2"], wts["wr_t"], wts["rb"], wts["wsg"], wts["wsu"], wts["wsd"], tm=MERGE_SUB)
    out = _moe_call(ha, hb, eid, pos, wk, counts, xb, mod_rows, wts["weg"], wts["weu"], wts["wed"], t)
    return out.reshape(b, t, D_MODEL), kv_raw, hfin


def kernel(x_prompt, x_sample, cache_k, cache_v, state_ssm, c, c_ctx, w_mod, b_mod, norm1_g, norm2_g, w_in,
           q_norm_g, k_norm_g, conv_w, conv_b, a_log, dt_bias, d_skip, ssd_norm_g, w_attn_proj, w_ssd_proj,
           w_out, w_router, router_bias, w_exp_gate, w_exp_up, w_exp_down, w_sh_gate, w_sh_up, w_sh_down):
    depth = w_mod.shape[0]
    assert depth == 1, "single trunk layer"
    bp, tp, _ = x_prompt.shape
    bs, ts, _ = x_sample.shape
    assert bs + 1 <= 8, "modulation rows are computed 8 at a time"
    l = 0
    cvec = jnp.concatenate([c_ctx[None, :], c, jnp.zeros((8 - 1 - bs, D_MODEL), F32)], axis=0)
    mod = _mod_call(cvec, w_mod.reshape(D_MODEL, 6 * D_MODEL), b_mod.reshape(1, 6 * D_MODEL))
    mod_prompt = mod[0:1][:, None, :]
    mod_sample = mod[1:1 + bs][:, None, :]

    lower = np.tril(np.ones((CHUNK, CHUNK), np.float32))
    head_sel = (np.arange(LANES)[:, None] == np.arange(D_INNER)[None, :] // SSD_HEAD_DIM).astype(np.float32)
    wts = dict(
        g1=norm1_g[l][None, :], n2=norm2_g[l][None, :],
        w_in=_prep_w_in(w_in.reshape(D_MODEL, w_in.shape[-1])),
        qg=jnp.tile(q_norm_g[l], 2)[None, :], kg=jnp.tile(k_norm_g[l], 2)[None, :],
        conv_w=conv_w[l], conv_b=conv_b[l][None, :],
        a_neg=_pad_heads(-jnp.exp(a_log[l].astype(F32))), dt_bias=_pad_heads(dt_bias[l]),
        dskip=jnp.repeat(d_skip[l].astype(F32), SSD_HEAD_DIM)[None, :],
        tri=jnp.asarray(np.stack([lower, lower.T])),
        head_sel=jnp.asarray(head_sel, BF16),
        sg=ssd_norm_g[l][None, :],
        wa=w_attn_proj[l].astype(BF16), ws=w_ssd_proj[l].astype(BF16), wo=w_out[l].astype(BF16),
        wr_t=w_router[l].T.astype(BF16), rb=router_bias[l].astype(F32)[:, None],
        wsg=w_sh_gate[l].astype(BF16), wsu=w_sh_up[l].astype(BF16), wsd=w_sh_down[l].astype(BF16),
        weg=w_exp_gate.reshape(w_exp_gate.shape[1:]), weu=w_exp_up.reshape(w_exp_up.shape[1:]),
        wed=w_exp_down.reshape(w_exp_down.shape[1:]),
    )

    y_prompt, (k_p, v_p), hfin = _trunk(x_prompt, mod_prompt, wts, None, None, None, None,
                                        tm=256, tq=256, want_state=True)
    new_k = k_p.reshape(bp, 1, tp, N_KV_HEADS, HEAD_DIM)
    new_v = v_p.reshape(bp, 1, tp, N_KV_HEADS, HEAD_DIM)
    new_state = hfin.reshape(bp, 1, 2, SSD_HEADS, SSD_HEAD_DIM, D_STATE)

    past = cache_k.shape[2]
    ctx_k = _dup_heads(cache_k[:, l].reshape(bs, past, KV_W)).astype(BF16)
    ctx_v = _dup_heads(cache_v[:, l].reshape(bs, past, KV_W)).astype(BF16)
    h0 = state_ssm[:, l].reshape(bs, 2, SSD_HEADS // 2, 2 * SSD_HEAD_DIM, D_STATE)
    y_sample, _, _ = _trunk(x_sample, mod_sample, wts, _rope_tables(ts), ctx_k, ctx_v, h0,
                               tm=512, tq=1024, want_state=False)
    return (y_prompt, y_sample, new_k, new_v, new_state)
```

```python
import functools

import numpy as np
import jax
import jax.numpy as jnp
from jax import lax
from jax.experimental import pallas as pl
from jax.experimental.pallas import tpu as pltpu
from jax.experimental.pallas import tpu_sc as plsc

F32 = jnp.float32
BF16 = jnp.bfloat16

D_MODEL = 1024
GRID_W = 64
EPS = 1e-6
N_HEADS = 16
N_KV_HEADS = 4
HEAD_DIM = 64
ATTN_W = N_HEADS * HEAD_DIM
KV_W = N_KV_HEADS * HEAD_DIM
ROPE_THETA = 10000.0
D_INNER = 2048
SSD_HEAD_DIM = 64
SSD_HEADS = 32
SSD_GROUPS = 4
D_STATE = 128
D_CONV = 4
CHUNK = 128
CONV_CH = D_INNER + 2 * SSD_GROUPS * D_STATE
N_EXPERTS = 64
TOP_K = 8
N_EXPERT_GROUPS = 8
TOPK_GROUPS = 4
D_EXPERT = 256
D_SHARED = 256
ROUTED_SCALE = 2.5

LANES = 128
KVD_W = N_KV_HEADS * LANES
M_Q, M_K, M_V, M_Z, M_X, M_END = (int(c) for c in np.cumsum((0, ATTN_W, KV_W, KV_W, D_INNER, CONV_CH)))
T_G, T_DT, T_END = (int(c) for c in np.cumsum((0, 2 * D_MODEL, 2 * LANES)))
MOD_SHIFT1, MOD_SCALE1, MOD_GATE1, MOD_SHIFT2, MOD_SCALE2, MOD_GATE2 = range(6)
VMEM_LIMIT = 56 * 1024 * 1024
Q_SCALE = HEAD_DIM ** -0.5 * 1.4426950408889634


def _cparams(sem):
    return pltpu.CompilerParams(dimension_semantics=sem, vmem_limit_bytes=VMEM_LIMIT)


def _mod(mod_ref, which, lo=0, hi=D_MODEL):
    return mod_ref[:, which * D_MODEL + lo:which * D_MODEL + hi]


def _silu(x):
    return x * jax.nn.sigmoid(x)


def _bdot(a, b):
    return jnp.dot(a.astype(BF16), b.astype(BF16), preferred_element_type=F32)


def _bdot_nt(a, b):
    return lax.dot_general(a.astype(BF16), b.astype(BF16), (((1,), (1,)), ((), ())),
                           preferred_element_type=F32)


def _mod_kernel(c_ref, w_ref, b_ref, o_ref):
    o_ref[...] = _bdot(_silu(c_ref[...]), w_ref[...]) + b_ref[...]


def _mod_call(cvec, w_mod, b_mod):
    n = w_mod.shape[1]
    bn = 1024
    return pl.pallas_call(
        _mod_kernel,
        out_shape=jax.ShapeDtypeStruct((8, n), F32),
        grid=(n // bn,),
        in_specs=[pl.BlockSpec((8, D_MODEL), lambda j: (0, 0)),
                  pl.BlockSpec((D_MODEL, bn), lambda j: (0, j)),
                  pl.BlockSpec((1, bn), lambda j: (0, j))],
        out_specs=pl.BlockSpec((8, bn), lambda j: (0, j)),
        compiler_params=_cparams(("arbitrary",)),
        name="mod",
    )(cvec, w_mod, b_mod)


def _inproj_kernel(*refs, rope, emit_kv):
    if emit_kv:
        (x_ref, mod_ref, g1_ref, wm_ref, wt_ref, qg_ref, kg_ref, cos_ref, sin_ref,
         q_ref, k_ref, v_ref, gates_ref, z_ref, xbc_ref, dt_ref, kraw_ref, vraw_ref) = refs
    else:
        (x_ref, mod_ref, g1_ref, wm_ref, wt_ref, qg_ref, kg_ref, cos_ref, sin_ref,
         q_ref, k_ref, v_ref, gates_ref, z_ref, xbc_ref, dt_ref) = refs
    tm = x_ref.shape[0]
    x = x_ref[...]
    inv = lax.rsqrt(jnp.mean(x * x, axis=-1, keepdims=True) + EPS)
    h = (x * inv) * g1_ref[...]
    h = h * (1.0 + _mod(mod_ref, MOD_SCALE1)) + _mod(mod_ref, MOD_SHIFT1)
    hb = h.astype(BF16)

    lane = lax.broadcasted_iota(jnp.int32, (tm, LANES), 1)
    lo = lane < HEAD_DIM
    even = (lane & 1) == 0
    if rope:
        cos = cos_ref[...]
        sin = sin_ref[...]

    def rope_fn(blk):
        nxt = pltpu.roll(blk, LANES - 1, 1)
        prv = pltpu.roll(blk, 1, 1)
        return blk * cos + jnp.where(even, nxt, prv) * sin

    def head_norm(blk, g):
        sq = blk * blk
        s_all = jnp.sum(sq, axis=-1, keepdims=True)
        s_lo = jnp.sum(jnp.where(lo, sq, 0.0), axis=-1, keepdims=True)
        ms = jnp.where(lo, s_lo, s_all - s_lo) * (1.0 / HEAD_DIM)
        return blk * lax.rsqrt(ms + EPS) * g

    def dup_heads(blk):
        sw = pltpu.roll(blk, HEAD_DIM, 1)
        return jnp.where(lo, blk, sw), jnp.where(lo, sw, blk)

    qg = qg_ref[...]
    kg = kg_ref[...]
    q = jnp.dot(hb, wm_ref[:, M_Q:M_K], preferred_element_type=F32)
    for j in range(ATTN_W // LANES):
        blk = head_norm(q[:, j * LANES:(j + 1) * LANES], qg)
        if rope:
            blk = rope_fn(blk)
        q_ref[:, j * LANES:(j + 1) * LANES] = (blk * Q_SCALE).astype(q_ref.dtype)

    k = jnp.dot(hb, wm_ref[:, M_K:M_V], preferred_element_type=F32)
    v = jnp.dot(hb, wm_ref[:, M_V:M_Z], preferred_element_type=F32)
    for j in range(KV_W // LANES):
        kb = head_norm(k[:, j * LANES:(j + 1) * LANES], kg)
        vb = v[:, j * LANES:(j + 1) * LANES]
        if emit_kv:
            kraw_ref[:, j * LANES:(j + 1) * LANES] = kb
            vraw_ref[:, j * LANES:(j + 1) * LANES] = vb
        if rope:
            kb = rope_fn(kb)
        for i, (kd, vd) in enumerate(zip(dup_heads(kb), dup_heads(vb))):
            c0 = (2 * j + i) * LANES
            k_ref[:, c0:c0 + LANES] = kd.astype(k_ref.dtype)
            v_ref[:, c0:c0 + LANES] = vd.astype(v_ref.dtype)

    gates_ref[...] = jnp.dot(hb, wt_ref[:, T_G:T_DT], preferred_element_type=F32).astype(gates_ref.dtype)
    z_ref[...] = jnp.dot(hb, wm_ref[:, M_Z:M_X], preferred_element_type=F32).astype(z_ref.dtype)
    xbc_ref[...] = jnp.dot(hb, wm_ref[:, M_X:M_END], preferred_element_type=F32).astype(xbc_ref.dtype)
    dt_ref[...] = jnp.dot(hb, wt_ref[:, T_DT:T_END], preferred_element_type=F32)


def _inproj_call(x, mod_rows, g1, w_main, w_tail, qg, kg, cos, sin, *, rope, emit_kv, tm):
    b, t, _ = x.shape
    assert t % tm == 0, (t, tm)
    nt = t // tm
    tok = lambda width: pl.BlockSpec((None, tm, width), lambda bi, i: (bi, i, 0))
    const2 = lambda shape: pl.BlockSpec(shape, lambda bi, i: (0, 0))
    out_shape = [
        jax.ShapeDtypeStruct((b, t, ATTN_W), BF16),
        jax.ShapeDtypeStruct((b, t, KVD_W), BF16),
        jax.ShapeDtypeStruct((b, t, KVD_W), BF16),
        jax.ShapeDtypeStruct((b, t, 2 * D_MODEL), BF16),
        jax.ShapeDtypeStruct((b, t, D_INNER), BF16),
        jax.ShapeDtypeStruct((b, t, CONV_CH), BF16),
        jax.ShapeDtypeStruct((b, t, 2 * LANES), F32),
    ]
    out_specs = [tok(ATTN_W), tok(KVD_W), tok(KVD_W), tok(2 * D_MODEL), tok(D_INNER), tok(CONV_CH),
                 tok(2 * LANES)]
    if emit_kv:
        out_shape += [jax.ShapeDtypeStruct((b, t, KV_W), F32)] * 2
        out_specs += [tok(KV_W), tok(KV_W)]
    return pl.pallas_call(
        functools.partial(_inproj_kernel, rope=rope, emit_kv=emit_kv),
        out_shape=tuple(out_shape),
        grid=(b, nt),
        in_specs=[tok(D_MODEL),
                  pl.BlockSpec((None, 1, 6 * D_MODEL), lambda bi, i: (jnp.minimum(bi, mod_rows.shape[0] - 1), 0, 0)),
                  const2((1, D_MODEL)),
                  pl.BlockSpec((D_MODEL, M_END), lambda bi, i: (0, 0), pipeline_mode=pl.Buffered(1)),
                  pl.BlockSpec((D_MODEL, T_END), lambda bi, i: (0, 0), pipeline_mode=pl.Buffered(1)),
                  const2((1, LANES)), const2((1, LANES)),
                  pl.BlockSpec((tm, LANES), lambda bi, i: (i, 0)),
                  pl.BlockSpec((tm, LANES), lambda bi, i: (i, 0))],
        out_specs=tuple(out_specs),
        compiler_params=_cparams(("arbitrary", "arbitrary")),
        name="inproj",
    )(x, mod_rows, g1, w_main, w_tail, qg, kg, cos, sin)


KEY_CHUNK = 512


def _key_chunk(n):
    return KEY_CHUNK if n % KEY_CHUNK == 0 else n


def _attn_kernel(*refs, has_ctx):
    if has_ctx:
        q_ref, k_ref, v_ref, kctx_ref, vctx_ref, o_ref = refs
        sources = ((k_ref, v_ref), (kctx_ref, vctx_ref))
    else:
        q_ref, k_ref, v_ref, o_ref = refs
        sources = ((k_ref, v_ref),)
    tq = q_ref.shape[0]
    lane = lax.broadcasted_iota(jnp.int32, (tq, LANES), 1)
    lo = lane < HEAD_DIM
    qs = []
    for j in range(2):
        q2 = q_ref[:, j * LANES:(j + 1) * LANES]
        zero = jnp.zeros_like(q2)
        qs += [jnp.where(lo, q2, zero), jnp.where(lo, zero, q2)]
    q4 = jnp.concatenate(qs, axis=0)
    rows = 4 * tq
    m = jnp.full((rows, 1), -jnp.inf, F32)
    acc = jnp.zeros((rows, LANES), F32)
    chunks = [(kr, vr, c, _key_chunk(kr.shape[0])) for kr, vr in sources
              for c in range(kr.shape[0] // _key_chunk(kr.shape[0]))]
    for kr, vr, c, kc in chunks:
        kch = kr[c * kc:(c + 1) * kc, :].astype(BF16)
        vch = vr[c * kc:(c + 1) * kc, :].astype(BF16)
        lane_k = lax.broadcasted_iota(jnp.int32, (kc, LANES), 1)
        vch = jnp.where(lane_k < HEAD_DIM, vch, jnp.ones_like(vch))
        s = _bdot_nt(q4, kch)
        m_new = jnp.maximum(m, jnp.max(s, axis=-1, keepdims=True))
        alpha = jnp.exp2(m - m_new)
        p = jnp.exp2((s - m_new).astype(BF16))
        acc = acc * alpha + jnp.dot(p, vch, preferred_element_type=F32)
        m = m_new
    o = acc * (1.0 / pltpu.roll(acc, HEAD_DIM, 1))
    for j in range(2):
        oa = o[(2 * j) * tq:(2 * j + 1) * tq]
        ob = pltpu.roll(o[(2 * j + 1) * tq:(2 * j + 2) * tq], HEAD_DIM, 1)
        o_ref[:, j * LANES:(j + 1) * LANES] = jnp.where(lo, oa, ob).astype(o_ref.dtype)


def _attn_call(q, k, v, kctx, vctx, *, tq):
    b, t, _ = q.shape
    assert t % tq == 0, (t, tq)
    tk = k.shape[1]
    nq = t // tq
    has_ctx = kctx is not None
    kv_spec = lambda n: pl.BlockSpec((None, n, LANES), lambda bi, g, i: (bi, 0, g))
    in_specs = [pl.BlockSpec((None, tq, 2 * LANES), lambda bi, g, i: (bi, i, g)), kv_spec(tk), kv_spec(tk)]
    args = [q, k, v]
    if has_ctx:
        in_specs += [kv_spec(kctx.shape[1]), kv_spec(kctx.shape[1])]
        args += [kctx, vctx]
    return pl.pallas_call(
        functools.partial(_attn_kernel, has_ctx=has_ctx),
        out_shape=jax.ShapeDtypeStruct((b, t, ATTN_W), BF16),
        grid=(b, N_KV_HEADS, nq),
        in_specs=in_specs,
        out_specs=pl.BlockSpec((None, tq, 2 * LANES), lambda bi, g, i: (bi, i, g)),
        compiler_params=_cparams(("arbitrary", "arbitrary", "arbitrary")),
        name="attn",
    )(*args)


LOG2E = 1.4426950408889634
SSD_BWD_CHUNKS = 4


def _softplus(x):
    return jnp.maximum(x, 0.0) + jnp.log(1.0 + jnp.exp(-jnp.abs(x)))


def _ssd_kernel(*refs, nc, reverse, has_h0, want_hfin, cps):
    refs = list(refs)
    conv = not reverse
    if conv:
        xbc_ref, prev_ref, next_ref, cw_ref, cb_ref, dsk_ref = refs[:6]
        refs = refs[6:]
    else:
        xc_ref = refs.pop(0)
    dt_ref, an_ref, dtb_ref, tri_ref, sel_ref = refs[:5]
    refs = refs[5:]
    h0_ref = refs.pop(0) if has_h0 else None
    hprev_ref = refs.pop(0) if (want_hfin and reverse) else None
    y_ref = refs.pop(0)
    xco_ref = refs.pop(0) if conv else None
    hfin_ref = refs.pop(0) if want_hfin else None
    h_scr = refs.pop(0)

    L = CHUNK
    c = pl.program_id(1)
    cidx = (nc - 1 - c) if reverse else c

    @pl.when(c == 0)
    def _():
        if has_h0:
            h_scr[...] = h0_ref[...]
        else:
            h_scr[...] = jnp.zeros_like(h_scr)

    row = lax.broadcasted_iota(jnp.int32, (L, LANES), 0)
    lane = lax.broadcasted_iota(jnp.int32, (L, LANES), 1)
    lo = lane < SSD_HEAD_DIM
    top = row < SSD_HEAD_DIM

    if conv:
        first = cidx == 0
        last = cidx == nc - 1

        def cols(a, w, r0=0):
            xm = xbc_ref[:, a:a + w].astype(F32)
            rw = lax.broadcasted_iota(jnp.int32, (L, w), 0)
            p6 = jnp.where(first, 0.0, prev_ref[6:7, a:a + w].astype(F32))
            p7 = jnp.where(first, 0.0, prev_ref[7:8, a:a + w].astype(F32))
            n0 = jnp.where(last, 0.0, next_ref[0:1, a:a + w].astype(F32))
            r1 = jnp.where(rw == 0, p7, pltpu.roll(xm, 1, 0))
            r2 = jnp.where(rw == 0, p6, jnp.where(rw == 1, p7, pltpu.roll(xm, 2, 0)))
            rn = jnp.where(rw == L - 1, n0, pltpu.roll(xm, L - 1, 0))
            y = (r2 * cw_ref[0:1, a:a + w] + r1 * cw_ref[1:2, a:a + w] + xm * cw_ref[2:3, a:a + w]
                 + rn * cw_ref[3:4, a:a + w] + cb_ref[:, a:a + w])
            y = _silu(y).astype(BF16)
            xco_ref[:, a:a + w] = y
            return y
    else:
        def cols(a, w, r0=0):
            return xc_ref[pl.ds(r0, L), a:a + w]

    causal = tri_ref[...] > 0.0

    def chunk(r0):
        dt = _softplus(dt_ref[pl.ds(r0, L), :] + dtb_ref[...])
        la2 = dt * (an_ref[...] * LOG2E)
        acum2 = jnp.dot(tri_ref[...], la2, preferred_element_type=F32, precision=lax.Precision.HIGHEST)
        dt_t = dt.T
        acum2_t = acum2.T
        tot2_t = jnp.sum(la2.T, axis=1, keepdims=True)
        lg_dt_t = jnp.log2(dt_t)
        r_t = lg_dt_t - acum2_t
        w_t = jnp.exp2(lg_dt_t + tot2_t - acum2_t)
        e_acum_x = jnp.dot(jnp.exp2(acum2).astype(BF16), sel_ref[...], preferred_element_type=F32)
        e_tot_t = jnp.exp2(tot2_t)

        for g in range(SSD_GROUPS):
            bgb = cols(D_INNER + g * D_STATE, D_STATE, r0)
            cgb = cols(D_INNER + SSD_GROUPS * D_STATE + g * D_STATE, D_STATE, r0)
            cbm = _bdot_nt(cgb, bgb)
            h_grp = h_scr[4 * g:4 * g + 4]
            yo_grp = _bdot_nt(cgb, h_grp.reshape(4 * LANES, D_STATE))
            xws, cds = [], []
            for pr in range(4):
                hp = g * 4 + pr
                ha, hb = 2 * hp, 2 * hp + 1
                xpb = cols(hp * LANES, LANES, r0)
                zero = jnp.zeros_like(xpb)
                xs = jnp.concatenate([jnp.where(lo, xpb, zero), jnp.where(lo, zero, xpb)], axis=0)
                ms = []
                for hh in (ha, hb):
                    e = jnp.exp2(acum2[:, hh:hh + 1] + r_t[hh:hh + 1, :])
                    ms.append((cbm * jnp.where(causal, e, 0.0)).astype(BF16))
                y = jnp.dot(jnp.concatenate(ms, axis=1), xs, preferred_element_type=F32)
                y = y + yo_grp[:, pr * LANES:(pr + 1) * LANES] * e_acum_x[:, hp * LANES:(hp + 1) * LANES]
                if conv:
                    y = y + dsk_ref[:, hp * LANES:(hp + 1) * LANES] * xpb.astype(F32)
                y_ref[pl.ds(r0, L), hp * LANES:(hp + 1) * LANES] = y.astype(y_ref.dtype)
                wsel = jnp.where(top, w_t[ha:ha + 1, :], w_t[hb:hb + 1, :])
                xws.append((xpb.astype(F32).T * wsel).astype(BF16))
                cds.append(jnp.where(top, e_tot_t[ha:ha + 1, :], e_tot_t[hb:hb + 1, :]))
            st = jnp.dot(jnp.concatenate(xws, axis=0), bgb, preferred_element_type=F32)
            for pr in range(4):
                h_scr[g * 4 + pr] = h_grp[pr] * cds[pr] + st[pr * LANES:(pr + 1) * LANES]

    if cps == 1:
        chunk(0)
    else:
        def step(j, carry):
            chunk(pl.multiple_of((cps - 1 - j) * L, L))
            return carry

        lax.fori_loop(0, cps, step, 0)

    if want_hfin:
        @pl.when(c == nc - 1)
        def _():
            if reverse:
                hfin_ref[0] = hprev_ref[...]
                hfin_ref[1] = h_scr[...]
            else:
                hfin_ref[...] = h_scr[...]


def _ssd_sweep(xin, dt, wts, h0, hprev, *, reverse, want_hfin):
    b, t, _ = xin.shape
    cps = SSD_BWD_CHUNKS if (reverse and t % (SSD_BWD_CHUNKS * CHUNK) == 0) else 1
    rows = cps * CHUNK
    assert t % rows == 0, t
    nc = t // rows
    has_h0 = h0 is not None
    rb = CHUNK // 8
    nrb = t // 8
    d = 1 if reverse else 0
    cmap = (lambda c: nc - 1 - c) if reverse else (lambda c: c)
    hshape = (SSD_HEADS // 2, 2 * SSD_HEAD_DIM, D_STATE)

    chunk_spec = pl.BlockSpec((None, rows, CONV_CH), lambda bi, c: (bi, cmap(c), 0))
    if reverse:
        in_specs = [chunk_spec]
        args = [xin]
    else:
        in_specs = [
            chunk_spec,
            pl.BlockSpec((None, 8, CONV_CH), lambda bi, c: (bi, jnp.maximum(c * rb - 1, 0), 0)),
            pl.BlockSpec((None, 8, CONV_CH), lambda bi, c: (bi, jnp.minimum((c + 1) * rb, nrb - 1), 0)),
            pl.BlockSpec((D_CONV, CONV_CH), lambda bi, c: (0, 0)),
            pl.BlockSpec((1, CONV_CH), lambda bi, c: (0, 0)),
            pl.BlockSpec((1, D_INNER), lambda bi, c: (0, 0)),
        ]
        args = [xin, xin, xin, wts["conv_w"], wts["conv_b"], wts["dskip"]]
    in_specs += [
        pl.BlockSpec((None, rows, LANES), lambda bi, c: (bi, cmap(c), d)),
        pl.BlockSpec((None, 1, LANES), lambda bi, c: (d, 0, 0)),
        pl.BlockSpec((None, 1, LANES), lambda bi, c: (d, 0, 0)),
        pl.BlockSpec((None, CHUNK, CHUNK), lambda bi, c: (d, 0, 0)),
        pl.BlockSpec((LANES, D_INNER), lambda bi, c: (0, 0)),
    ]
    args += [dt, wts["a_neg"], wts["dt_bias"], wts["tri"], wts["head_sel"]]
    if has_h0:
        in_specs.append(pl.BlockSpec((None, None) + hshape, lambda bi, c: (bi, d, 0, 0, 0)))
        args.append(h0)
    if want_hfin and reverse:
        in_specs.append(pl.BlockSpec((None,) + hshape, lambda bi, c: (bi, 0, 0, 0)))
        args.append(hprev)
    out_shape = [jax.ShapeDtypeStruct((b, t, D_INNER), BF16)]
    out_specs = [pl.BlockSpec((None, rows, D_INNER), lambda bi, c: (bi, cmap(c), 0))]
    if not reverse:
        out_shape.append(jax.ShapeDtypeStruct((b, t, CONV_CH), BF16))
        out_specs.append(pl.BlockSpec((None, CHUNK, CONV_CH), lambda bi, c: (bi, c, 0)))
    if want_hfin and reverse:
        out_shape.append(jax.ShapeDtypeStruct((b, 2) + hshape, F32))
        out_specs.append(pl.BlockSpec((None, 2) + hshape, lambda bi, c: (bi, 0, 0, 0, 0)))
    elif want_hfin:
        out_shape.append(jax.ShapeDtypeStruct((b,) + hshape, F32))
        out_specs.append(pl.BlockSpec((None,) + hshape, lambda bi, c: (bi, 0, 0, 0)))
    return pl.pallas_call(
        functools.partial(_ssd_kernel, nc=nc, reverse=reverse, has_h0=has_h0, want_hfin=want_hfin, cps=cps),
        out_shape=tuple(out_shape),
        grid=(b, nc),
        in_specs=in_specs,
        out_specs=tuple(out_specs),
        scratch_shapes=[pltpu.VMEM(hshape, F32)],
        compiler_params=_cparams(("arbitrary", "arbitrary")),
        name="ssd_bwd" if reverse else "ssd_fwd",
    )(*args)


def _ssd_call(xbc, dt, wts, h0, *, want_hfin):
    res = _ssd_sweep(xbc, dt, wts, h0, None, reverse=False, want_hfin=want_hfin)
    y_f, xc = res[0], res[1]
    hf = res[2] if want_hfin else None
    res = _ssd_sweep(xc, dt, wts, h0, hf, reverse=True, want_hfin=want_hfin)
    return y_f, res[0], (res[1] if want_hfin else None)


def _route(logits_t, bias_col):
    e, n = logits_t.shape
    per = e // N_EXPERT_GROUPS
    scores = jax.nn.sigmoid(logits_t)
    sel = scores + bias_col
    neg = jnp.float32(-jnp.inf)
    gs = []
    for g in range(N_EXPERT_GROUPS):
        blk = sel[g * per:(g + 1) * per, :]
        m1 = jnp.max(blk, axis=0, keepdims=True)
        is_m1 = blk == m1
        cnt = jnp.sum(jnp.where(is_m1, 1.0, 0.0), axis=0, keepdims=True)
        m2 = jnp.max(jnp.where(is_m1, neg, blk), axis=0, keepdims=True)
        gs.append(m1 + jnp.where(cnt >= 2.0, m1, m2))
    keep = []
    for g in range(N_EXPERT_GROUPS):
        rank = jnp.zeros_like(gs[g])
        for j in range(N_EXPERT_GROUPS):
            if j == g:
                continue
            beats = (gs[j] > gs[g]) if j > g else (gs[j] >= gs[g])
            rank = rank + jnp.where(beats, 1.0, 0.0)
        keep.append(rank < float(TOPK_GROUPS))
    selm = jnp.concatenate(
        [jnp.where(keep[g], sel[g * per:(g + 1) * per, :], neg) for g in range(N_EXPERT_GROUPS)], axis=0)
    eidx = lax.broadcasted_iota(jnp.int32, (e, n), 0).astype(F32)
    cur = selm
    picks = []
    for _ in range(TOP_K):
        m = jnp.max(cur, axis=0, keepdims=True)
        idx = jnp.min(jnp.where(cur == m, eidx, float(e)), axis=0, keepdims=True)
        hit = eidx == idx
        picks.append((idx, hit))
        cur = jnp.where(hit, neg, cur)
    return scores, picks


def _pack_bf16_pairs(h):
    c = h.shape[1] // 2
    lo = pltpu.bitcast(h[:, :c].astype(BF16).astype(F32), jnp.uint32)
    hi = pltpu.bitcast(h[:, c:].astype(BF16).astype(F32), jnp.uint32)
    return (lo >> 16) | (hi & jnp.uint32(0xFFFF0000))


def _unpack_bf16_pairs(w):
    lo = pltpu.bitcast(w << 16, F32)
    hi = pltpu.bitcast(w & jnp.uint32(0xFFFF0000), F32)
    return lo, hi


def _rows8(rows):
    n = rows[0].shape[1]
    ridx = lax.broadcasted_iota(jnp.int32, (TOP_K, n), 0)
    out = jnp.zeros((TOP_K, n), rows[0].dtype)
    for k, r in enumerate(rows):
        out = jnp.where(ridx == k, r, out)
    return out


MERGE_SUB = 512


def _merge_kernel(x_ref, attn_ref, yf_ref, yb_ref, z_ref, gates_ref, mod_ref, wa_ref, ws_ref, wo_ref,
                  sg_ref, n2_ref, wr_ref, rb_ref, wsg_ref, wsu_ref, wsd_ref,
                  xb_ref, ha_ref, hb_ref, eid_ref, pos_ref, wk_ref, cnt_ref):
    tm = x_ref.shape[0]
    sub = MERGE_SUB

    @pl.when(pl.program_id(0) == 0)
    def _():
        cnt_ref[...] = jnp.zeros_like(cnt_ref)

    r_i = lax.broadcasted_iota(jnp.int32, (sub, sub), 0)
    c_i = lax.broadcasted_iota(jnp.int32, (sub, sub), 1)
    before = jnp.where(r_i < c_i, 1.0, 0.0).astype(BF16)
    cnt = cnt_ref[:, 0:1]

    for r0 in range(0, tm, sub):
        rs = slice(r0, r0 + sub)
        x = x_ref[rs, :]
        yy = yf_ref[rs, :].astype(F32) + yb_ref[rs, :].astype(F32)
        u = yy * _silu(z_ref[rs, :]).astype(F32)
        un = u * lax.rsqrt(jnp.mean(u * u, axis=-1, keepdims=True) + EPS) * sg_ref[...]
        ssd_o = _bdot(un, ws_ref[...])
        attn_o = jnp.dot(attn_ref[rs, :], wa_ref[...], preferred_element_type=F32)
        ga = jax.nn.sigmoid(gates_ref[rs, 0:D_MODEL]).astype(F32)
        gs = jax.nn.sigmoid(gates_ref[rs, D_MODEL:2 * D_MODEL]).astype(F32)
        mix = _bdot(ga * attn_o + gs * ssd_o, wo_ref[...])
        x1 = x + _mod(mod_ref, MOD_GATE1) * mix
        h2 = x1 * lax.rsqrt(jnp.mean(x1 * x1, axis=-1, keepdims=True) + EPS) * n2_ref[...]
        h2 = h2 * (1.0 + _mod(mod_ref, MOD_SCALE2)) + _mod(mod_ref, MOD_SHIFT2)
        h2b = h2.astype(BF16)
        ha_ref[rs, :] = _pack_bf16_pairs(h2[:, :D_MODEL // 2])
        hb_ref[rs, :] = _pack_bf16_pairs(h2[:, D_MODEL // 2:])

        logits_t = _bdot_nt(wr_ref[...], h2b)
        scores, picks = _route(logits_t, rb_ref[...])
        chosen = jnp.zeros_like(scores)
        for _, hit in picks:
            chosen = chosen + jnp.where(hit, 1.0, 0.0)
        pos = cnt + jnp.dot(chosen.astype(BF16), before, preferred_element_type=F32)
        cnt = cnt + jnp.sum(chosen, axis=1, keepdims=True)
        poss = [jnp.sum(jnp.where(hit, pos, 0.0), axis=0, keepdims=True) for _, hit in picks]
        wks = [jnp.sum(jnp.where(hit, scores, 0.0), axis=0, keepdims=True) for _, hit in picks]
        wsum = wks[0]
        for w in wks[1:]:
            wsum = wsum + w
        eid_ref[:, rs] = _rows8([idx for idx, _ in picks]).astype(jnp.int32)
        pos_ref[:, rs] = _rows8(poss).astype(jnp.int32)
        wk8 = _rows8(wks) / wsum * ROUTED_SCALE
        wk_ref[rs, :] = jnp.concatenate([wk8, jnp.zeros((LANES - TOP_K, sub), F32)], axis=0).T

        hid = _silu(jnp.dot(h2b, wsg_ref[...], preferred_element_type=F32)) * \
            jnp.dot(h2b, wsu_ref[...], preferred_element_type=F32)
        xb_ref[rs, :] = x1 + _mod(mod_ref, MOD_GATE2) * _bdot(hid, wsd_ref[...])

    cnt_ref[...] = jnp.broadcast_to(cnt, cnt_ref.shape)


def _merge_call(x, attn, y_f, y_b, z, gates, mod_rows, wa, ws, wo, sg, n2, wr_t, rb, wsg, wsu, wsd, *, tm):
    b, t, _ = x.shape
    n = b * t
    assert n % tm == 0 and (t % tm == 0 or mod_rows.shape[0] == 1), (n, t, tm)
    flat = lambda a: a.reshape(n, a.shape[-1])
    tok = lambda width: pl.BlockSpec((tm, width), lambda i: (i, 0))
    const2 = lambda shape: pl.BlockSpec(shape, lambda i: (0, 0), pipeline_mode=pl.Buffered(1))
    k8 = pl.BlockSpec((TOP_K, tm), lambda i: (0, i))
    half = D_MODEL // 4
    return pl.pallas_call(
        _merge_kernel,
        out_shape=(jax.ShapeDtypeStruct((n, D_MODEL), F32),
                   jax.ShapeDtypeStruct((n, half), jnp.uint32),
                   jax.ShapeDtypeStruct((n, half), jnp.uint32),
                   jax.ShapeDtypeStruct((TOP_K, n), jnp.int32),
                   jax.ShapeDtypeStruct((TOP_K, n), jnp.int32),
                   jax.ShapeDtypeStruct((n, LANES), F32),
                   jax.ShapeDtypeStruct((N_EXPERTS, LANES), F32)),
        grid=(n // tm,),
        in_specs=[tok(D_MODEL), tok(ATTN_W), tok(D_INNER), tok(D_INNER), tok(D_INNER), tok(2 * D_MODEL),
                  pl.BlockSpec((None, 1, 6 * D_MODEL), lambda i: (jnp.minimum((i * tm) // t, mod_rows.shape[0] - 1), 0, 0)),
                  const2((ATTN_W, D_MODEL)), const2((D_INNER, D_MODEL)), const2((D_MODEL, D_MODEL)),
                  const2((1, D_INNER)), const2((1, D_MODEL)),
                  const2((N_EXPERTS, D_MODEL)), const2((N_EXPERTS, 1)),
                  const2((D_MODEL, D_SHARED)), const2((D_MODEL, D_SHARED)), const2((D_SHARED, D_MODEL))],
        out_specs=(tok(D_MODEL), tok(half), tok(half), k8, k8, tok(LANES),
                   pl.BlockSpec((N_EXPERTS, LANES), lambda i: (0, 0))),
        compiler_params=_cparams(("arbitrary",)),
        name="merge",
    )(flat(x), flat(attn), flat(y_f), flat(y_b), flat(z), flat(gates), mod_rows, wa, ws, wo, sg, n2, wr_t, rb,
      wsg, wsu, wsd)


ROW_TILE = 512
SC_WINDOW = 128


def _slots_kernel(start_ref, eid_ref, pos_ref, slot_ref):
    eid = eid_ref[...]
    slot = pos_ref[...]
    for e in range(N_EXPERTS):
        slot = slot + jnp.where(eid == e, start_ref[e], 0)
    slot_ref[...] = slot


def _slots_call(start, eid, pos):
    n = eid.shape[1]
    bn = 2048 if n % 2048 == 0 else n
    spec = pl.BlockSpec((TOP_K, bn), lambda i, s: (0, i))
    return pl.pallas_call(
        _slots_kernel,
        out_shape=jax.ShapeDtypeStruct((TOP_K, n), jnp.int32),
        grid_spec=pltpu.PrefetchScalarGridSpec(num_scalar_prefetch=1, grid=(n // bn,),
                                               in_specs=[spec, spec], out_specs=spec),
        compiler_params=_cparams(("arbitrary",)),
        name="slots",
    )(start, eid, pos)


def _sc_dispatch(x, slots, p):
    n, d = x.shape
    assert n % SC_WINDOW == 0, n
    mesh = plsc.VectorSubcoreMesh(core_axis_name="core", subcore_axis_name="subcore")

    @functools.partial(pl.kernel, out_type=jax.ShapeDtypeStruct((p, d), x.dtype), mesh=mesh)
    def k(x_hbm, s_hbm, o_hbm):
        def body(x_vmem, s_vmem):
            for kk in range(TOP_K):
                pltpu.sync_copy(x_vmem, o_hbm.at[s_vmem.at[kk]])

        pltpu.emit_pipeline(
            body,
            grid=(n // SC_WINDOW,),
            in_specs=[pl.BlockSpec((SC_WINDOW, d), index_map=lambda i: (i, 0)),
                      pl.BlockSpec((TOP_K, SC_WINDOW), index_map=lambda i: (0, i))],
            out_specs=[],
            core_axis_name=("core", "subcore"),
            dimension_semantics=(pltpu.PARALLEL,),
        )(x_hbm, s_hbm)

    return k(x, slots)


def _sc_combine(y, slots):
    kk, n = slots.shape
    assert n % SC_WINDOW == 0, n
    d = y.shape[1]
    mesh = plsc.VectorSubcoreMesh(core_axis_name="core", subcore_axis_name="subcore")

    @functools.partial(pl.kernel, out_type=jax.ShapeDtypeStruct((kk * n, d), y.dtype), mesh=mesh)
    def k(y_hbm, s_hbm, o_hbm):
        def body(s_vmem, o_vmem):
            pltpu.sync_copy(y_hbm.at[s_vmem.at[0]], o_vmem)

        pltpu.emit_pipeline(
            body,
            grid=(kk * n // SC_WINDOW,),
            in_specs=[pl.BlockSpec((1, SC_WINDOW), index_map=lambda i: (0, i))],
            out_specs=[pl.BlockSpec((SC_WINDOW, d), index_map=lambda i: (i, 0))],
            core_axis_name=("core", "subcore"),
            dimension_semantics=(pltpu.PARALLEL,),
        )(s_hbm, o_hbm)

    return k(y, slots.reshape(1, kk * n)).reshape(kk, n, d)


FFN_IN_BUFS = 4
FFN_OUT_BUFS = 3


def _ffn_kernel(st_ref, nt_ref, wg_ref, wu_ref, wd_ref, xa_hbm, xb_hbm, ya_hbm, yb_hbm,
                wg_s, wu_s, wd_s, xa_buf, xb_buf, ya_buf, yb_buf, in_sem, out_sem, done_ref):
    e = pl.program_id(0)
    ne = pl.num_programs(0)
    nxt = jnp.minimum(e + 1, ne - 1)
    n = nt_ref[e]
    n_next = jnp.where(e + 1 < ne, nt_ref[nxt], 0)
    t = ROW_TILE
    ahead = FFN_IN_BUFS - 1

    def fetch(row, slot):
        r = pl.multiple_of(row, t)
        return (pltpu.make_async_copy(xa_hbm.at[pl.ds(r, t)], xa_buf.at[slot], in_sem.at[0, slot]),
                pltpu.make_async_copy(xb_hbm.at[pl.ds(r, t)], xb_buf.at[slot], in_sem.at[1, slot]))

    def put(row, slot):
        r = pl.multiple_of(row, t)
        return (pltpu.make_async_copy(ya_buf.at[slot], ya_hbm.at[pl.ds(r, t)], out_sem.at[0, slot]),
                pltpu.make_async_copy(yb_buf.at[slot], yb_hbm.at[pl.ds(r, t)], out_sem.at[1, slot]))

    def wait_put(slot):
        for c in put(0, slot):
            c.wait()

    def start_fetch(row, g):
        for c in fetch(row, lax.rem(g, FFN_IN_BUFS)):
            c.start()

    @pl.when(e == 0)
    def _():
        done_ref[0] = 0
        for k in range(ahead):
            @pl.when(k < n)
            def _():
                start_fetch(st_ref[0] + k * t, k)

    done = done_ref[0]

    def prefetch(i):
        j = i + ahead - n

        @pl.when(j < 0)
        def _():
            start_fetch(st_ref[e] + (i + ahead) * t, done + i + ahead)

        @pl.when(jnp.logical_and(j >= 0, j < n_next))
        def _():
            start_fetch(st_ref[nxt] + j * t, done + i + ahead)

    @pl.when(n > 0)
    def _():
        wg_s[...] = wg_ref[...].astype(BF16)
        wu_s[...] = wu_ref[...].astype(BF16)
        wd_s[...] = wd_ref[...].astype(BF16)
        base = st_ref[e]

        def body(i, carry):
            g = done + i
            si = lax.rem(g, FFN_IN_BUFS)
            so = lax.rem(g, FFN_OUT_BUFS)
            row = base + i * t
            for c in fetch(row, si):
                c.wait()
            prefetch(i)

            @pl.when(g >= FFN_OUT_BUFS)
            def _():
                wait_put(so)

            parts = _unpack_bf16_pairs(xa_buf[si]) + _unpack_bf16_pairs(xb_buf[si])
            x = jnp.concatenate(parts, axis=1).astype(BF16)
            hid = _silu(jnp.dot(x, wg_s[...], preferred_element_type=F32)) * \
                jnp.dot(x, wu_s[...], preferred_element_type=F32)
            y = _bdot(hid, wd_s[...])
            ya_buf[so] = _pack_bf16_pairs(y[:, :D_MODEL // 2])
            yb_buf[so] = _pack_bf16_pairs(y[:, D_MODEL // 2:])
            for c in put(row, so):
                c.start()
            return carry

        lax.fori_loop(0, n, body, 0)
        done_ref[0] = done + n

    for k in range(ahead):
        @pl.when(jnp.logical_and(k < ahead - n, k < n_next))
        def _():
            start_fetch(st_ref[nxt] + k * t, done + n + k)

    @pl.when(e == ne - 1)
    def _():
        total = done_ref[0]
        for k in range(FFN_OUT_BUFS):
            @pl.when(total - 1 - k >= 0)
            def _():
                wait_put(lax.rem(total - 1 - k, FFN_OUT_BUFS))


def _ffn_call(start, tiles_e, xa, xb, wg, wu, wd):
    p, half = xa.shape
    wspec = lambda s: pl.BlockSpec((None,) + s, lambda e, st, nt: (e, 0, 0))
    hbm = pl.BlockSpec(memory_space=pl.ANY)
    ibuf = pltpu.VMEM((FFN_IN_BUFS, ROW_TILE, half), jnp.uint32)
    obuf = pltpu.VMEM((FFN_OUT_BUFS, ROW_TILE, half), jnp.uint32)
    return pl.pallas_call(
        _ffn_kernel,
        out_shape=(jax.ShapeDtypeStruct((p, half), jnp.uint32), jax.ShapeDtypeStruct((p, half), jnp.uint32)),
        grid_spec=pltpu.PrefetchScalarGridSpec(
            num_scalar_prefetch=2, grid=(N_EXPERTS,),
            in_specs=[wspec((D_MODEL, D_EXPERT)), wspec((D_MODEL, D_EXPERT)), wspec((D_EXPERT, D_MODEL)),
                      hbm, hbm],
            out_specs=(hbm, hbm),
            scratch_shapes=[pltpu.VMEM((D_MODEL, D_EXPERT), BF16), pltpu.VMEM((D_MODEL, D_EXPERT), BF16),
                            pltpu.VMEM((D_EXPERT, D_MODEL), BF16), ibuf, ibuf, obuf, obuf,
                            pltpu.SemaphoreType.DMA((2, FFN_IN_BUFS)),
                            pltpu.SemaphoreType.DMA((2, FFN_OUT_BUFS)),
                            pltpu.SMEM((1,), jnp.int32)]),
        compiler_params=_cparams(("arbitrary",)),
        name="ffn",
    )(start, tiles_e, wg, wu, wd, xa, xb)


def _final_kernel(xb_ref, ya_ref, yb_ref, wk_ref, mod_ref, o_ref):
    q = D_MODEL // 4
    accs = [jnp.zeros((xb_ref.shape[0], q), F32) for _ in range(4)]
    for k in range(TOP_K):
        w = wk_ref[:, k:k + 1]
        parts = _unpack_bf16_pairs(ya_ref[k]) + _unpack_bf16_pairs(yb_ref[k])
        accs = [a + w * p for a, p in zip(accs, parts)]
    for i, a in enumerate(accs):
        o_ref[:, i * q:(i + 1) * q] = xb_ref[:, i * q:(i + 1) * q] + _mod(mod_ref, MOD_GATE2, i * q, (i + 1) * q) * a


def _final_call(xb, ya, yb, wk, mod_rows, t, *, tm):
    n = xb.shape[0]
    assert n % tm == 0 and (t % tm == 0 or mod_rows.shape[0] == 1), (n, t, tm)
    half = ya.shape[2]
    tok = lambda width: pl.BlockSpec((tm, width), lambda i: (i, 0))
    yspec = pl.BlockSpec((TOP_K, tm, half), lambda i: (0, i, 0))
    return pl.pallas_call(
        _final_kernel,
        out_shape=jax.ShapeDtypeStruct((n, D_MODEL), F32),
        grid=(n // tm,),
        in_specs=[tok(D_MODEL), yspec, yspec, tok(LANES),
                  pl.BlockSpec((None, 1, 6 * D_MODEL), lambda i: (jnp.minimum((i * tm) // t, mod_rows.shape[0] - 1), 0, 0))],
        out_specs=tok(D_MODEL),
        compiler_params=_cparams(("arbitrary",)),
        name="final",
    )(xb, ya, yb, wk, mod_rows)


def _moe_call(ha, hb, eid, pos, wk, counts, xb, mod_rows, wg, wu, wd, t):
    n = xb.shape[0]
    max_tiles = n * TOP_K // ROW_TILE + N_EXPERTS
    p = max_tiles * ROW_TILE
    cnt = counts[:, 0].astype(jnp.int32)
    tiles_e = (cnt + ROW_TILE - 1) // ROW_TILE
    ends = jnp.cumsum(tiles_e)
    start = (ends - tiles_e) * ROW_TILE
    slots = _slots_call(start.astype(jnp.int32), eid, pos)
    xa = _sc_dispatch(ha, slots, p)
    xbb = _sc_dispatch(hb, slots, p)
    ya, yb = _ffn_call(start.astype(jnp.int32), tiles_e.astype(jnp.int32), xa, xbb, wg, wu, wd)
    ga = _sc_combine(ya, slots)
    gb = _sc_combine(yb, slots)
    return _final_call(xb, ga, gb, wk, mod_rows, t, tm=256)


def _rope_tables(t):
    n_rows = t // GRID_W
    rows = np.repeat(np.arange(n_rows), GRID_W).astype(np.float32)
    cols = np.tile(np.arange(GRID_W), n_rows).astype(np.float32)
    n_freq = HEAD_DIM // 4
    freqs = (np.float32(ROPE_THETA) ** (-np.arange(n_freq, dtype=np.float32) / np.float32(n_freq))).astype(np.float32)
    ang = np.concatenate([rows[:, None] * freqs, cols[:, None] * freqs], axis=-1)
    ang = np.repeat(ang, 2, axis=-1)
    ang = np.concatenate([ang, ang], axis=-1).astype(np.float32)
    sign = np.where(np.arange(LANES) % 2 == 0, -1.0, 1.0).astype(np.float32)
    return jnp.asarray(np.cos(ang)), jnp.asarray(np.sin(ang) * sign)


def _dup_heads(a):
    s = a.shape[:-1]
    a4 = a.reshape(s + (N_KV_HEADS, HEAD_DIM))
    return jnp.concatenate([a4, a4], axis=-1).reshape(s + (KVD_W,))


def _prep_w_in(w_in):
    pad = jnp.zeros((D_MODEL, LANES - SSD_HEADS), w_in.dtype)
    dt0 = M_END
    g0 = dt0 + 2 * SSD_HEADS
    tail = [w_in[:, g0:g0 + 2 * D_MODEL], w_in[:, dt0:dt0 + SSD_HEADS], pad,
            w_in[:, dt0 + SSD_HEADS:g0], pad]
    return w_in[:, :M_END].astype(BF16), jnp.concatenate(tail, axis=-1).astype(BF16)


def _pad_heads(a):
    return jnp.pad(a.astype(F32), ((0, 0), (0, LANES - SSD_HEADS)))[:, None, :]


def _trunk(x, mod_rows, wts, rope_tabs, ctx_k, ctx_v, h0, *, tm, tq, want_state):
    b, t, _ = x.shape
    rope = rope_tabs is not None
    if rope:
        cos, sin = rope_tabs
    else:
        cos = sin = jnp.zeros((t, LANES), F32)
    res = _inproj_call(x, mod_rows, wts["g1"], *wts["w_in"], wts["qg"], wts["kg"], cos, sin,
                       rope=rope, emit_kv=want_state, tm=tm)
    q, k, v, gates, z, xbc, dt = res[:7]
    kv_raw = res[7:]
    attn = _attn_call(q, k, v, ctx_k, ctx_v, tq=tq)
    y_f, y_b, hfin = _ssd_call(xbc, dt, wts, h0, want_hfin=want_state)
    xb, ha, hb, eid, pos, wk, counts = _merge_call(
        x, attn, y_f, y_b, z, gates, mod_rows, wts["wa"], wts["ws"], wts["wo"], wts["sg"],
        wts["n2"], wts["wr_t"], wts["rb"], wts["wsg"], wts["wsu"], wts["wsd"], tm=MERGE_SUB)
    out = _moe_call(ha, hb, eid, pos, wk, counts, xb, mod_rows, wts["weg"], wts["weu"], wts["wed"], t)
    return out.reshape(b, t, D_MODEL), kv_raw, hfin


def kernel(x_prompt, x_sample, cache_k, cache_v, state_ssm, c, c_ctx, w_mod, b_mod, norm1_g, norm2_g, w_in,
           q_norm_g, k_norm_g, conv_w, conv_b, a_log, dt_bias, d_skip, ssd_norm_g, w_attn_proj, w_ssd_proj,
           w_out, w_router, router_bias, w_exp_gate, w_exp_up, w_exp_down, w_sh_gate, w_sh_up, w_sh_down):
    depth = w_mod.shape[0]
    assert depth == 1, "single trunk layer"
    bp, tp, _ = x_prompt.shape
    bs, ts, _ = x_sample.shape
    assert bs + 1 <= 8, "modulation rows are computed 8 at a time"
    l = 0
    cvec = jnp.concatenate([c_ctx[None, :], c, jnp.zeros((8 - 1 - bs, D_MODEL), F32)], axis=0)
    mod = _mod_call(cvec, w_mod.reshape(D_MODEL, 6 * D_MODEL), b_mod.reshape(1, 6 * D_MODEL))
    mod_prompt = mod[0:1][:, None, :]
    mod_sample = mod[1:1 + bs][:, None, :]

    lower = np.tril(np.ones((CHUNK, CHUNK), np.float32))
    head_sel = (np.arange(LANES)[:, None] == np.arange(D_INNER)[None, :] // SSD_HEAD_DIM).astype(np.float32)
    wts = dict(
        g1=norm1_g[l][None, :], n2=norm2_g[l][None, :],
        w_in=_prep_w_in(w_in.reshape(D_MODEL, w_in.shape[-1])),
        qg=jnp.tile(q_norm_g[l], 2)[None, :], kg=jnp.tile(k_norm_g[l], 2)[None, :],
        conv_w=conv_w[l], conv_b=conv_b[l][None, :],
        a_neg=_pad_heads(-jnp.exp(a_log[l].astype(F32))), dt_bias=_pad_heads(dt_bias[l]),
        dskip=jnp.repeat(d_skip[l].astype(F32), SSD_HEAD_DIM)[None, :],
        tri=jnp.asarray(np.stack([lower, lower.T])),
        head_sel=jnp.asarray(head_sel, BF16),
        sg=ssd_norm_g[l][None, :],
        wa=w_attn_proj[l].astype(BF16), ws=w_ssd_proj[l].astype(BF16), wo=w_out[l].astype(BF16),
        wr_t=w_router[l].T.astype(BF16), rb=router_bias[l].astype(F32)[:, None],
        wsg=w_sh_gate[l].astype(BF16), wsu=w_sh_up[l].astype(BF16), wsd=w_sh_down[l].astype(BF16),
        weg=w_exp_gate.reshape(w_exp_gate.shape[1:]), weu=w_exp_up.reshape(w_exp_up.shape[1:]),
        wed=w_exp_down.reshape(w_exp_down.shape[1:]),
    )

    y_prompt, (k_p, v_p), hfin = _trunk(x_prompt, mod_prompt, wts, None, None, None, None,
                                        tm=256, tq=256, want_state=True)
    new_k = k_p.reshape(bp, 1, tp, N_KV_HEADS, HEAD_DIM)
    new_v = v_p.reshape(bp, 1, tp, N_KV_HEADS, HEAD_DIM)
    new_state = hfin.reshape(bp, 1, 2, SSD_HEADS, SSD_HEAD_DIM, D_STATE)

    past = cache_k.shape[2]
    ctx_k = _dup_heads(cache_k[:, l].reshape(bs, past, KV_W)).astype(BF16)
    ctx_v = _dup_heads(cache_v[:, l].reshape(bs, past, KV_W)).astype(BF16)
    h0 = state_ssm[:, l].reshape(bs, 2, SSD_HEADS // 2, 2 * SSD_HEAD_DIM, D_STATE)
    y_sample, _, _ = _trunk(x_sample, mod_sample, wts, _rope_tables(ts), ctx_k, ctx_v, h0,
                               tm=512, tq=512, want_state=False)
    return (y_prompt, y_sample, new_k, new_v, new_state)
```

```python
import functools

import numpy as np
import jax
import jax.numpy as jnp
from jax import lax
from jax.experimental import pallas as pl
from jax.experimental.pallas import tpu as pltpu
from jax.experimental.pallas import tpu_sc as plsc

F32 = jnp.float32
BF16 = jnp.bfloat16

D_MODEL = 1024
GRID_W = 64
EPS = 1e-6
N_HEADS = 16
N_KV_HEADS = 4
HEAD_DIM = 64
ATTN_W = N_HEADS * HEAD_DIM
KV_W = N_KV_HEADS * HEAD_DIM
ROPE_THETA = 10000.0
D_INNER = 2048
SSD_HEAD_DIM = 64
SSD_HEADS = 32
SSD_GROUPS = 4
D_STATE = 128
D_CONV = 4
CHUNK = 128
CONV_CH = D_INNER + 2 * SSD_GROUPS * D_STATE
N_EXPERTS = 64
TOP_K = 8
N_EXPERT_GROUPS = 8
TOPK_GROUPS = 4
D_EXPERT = 256
D_SHARED = 256
ROUTED_SCALE = 2.5

LANES = 128
KVD_W = N_KV_HEADS * LANES
M_Q, M_K, M_V, M_Z, M_X, M_END = (int(c) for c in np.cumsum((0, ATTN_W, KV_W, KV_W, D_INNER, CONV_CH)))
T_G, T_DT, T_END = (int(c) for c in np.cumsum((0, 2 * D_MODEL, 2 * LANES)))
MOD_SHIFT1, MOD_SCALE1, MOD_GATE1, MOD_SHIFT2, MOD_SCALE2, MOD_GATE2 = range(6)
VMEM_LIMIT = 56 * 1024 * 1024
Q_SCALE = HEAD_DIM ** -0.5 * 1.4426950408889634


def _cparams(sem):
    return pltpu.CompilerParams(dimension_semantics=sem, vmem_limit_bytes=VMEM_LIMIT)


def _mod(mod_ref, which, lo=0, hi=D_MODEL):
    return mod_ref[:, which * D_MODEL + lo:which * D_MODEL + hi]


def _silu(x):
    return x * jax.nn.sigmoid(x)


def _bdot(a, b):
    return jnp.dot(a.astype(BF16), b.astype(BF16), preferred_element_type=F32)


def _bdot_nt(a, b):
    return lax.dot_general(a.astype(BF16), b.astype(BF16), (((1,), (1,)), ((), ())),
                           preferred_element_type=F32)


def _mod_kernel(c_ref, w_ref, b_ref, o_ref):
    o_ref[...] = _bdot(_silu(c_ref[...]), w_ref[...]) + b_ref[...]


def _mod_call(cvec, w_mod, b_mod):
    n = w_mod.shape[1]
    bn = 1024
    return pl.pallas_call(
        _mod_kernel,
        out_shape=jax.ShapeDtypeStruct((8, n), F32),
        grid=(n // bn,),
        in_specs=[pl.BlockSpec((8, D_MODEL), lambda j: (0, 0)),
                  pl.BlockSpec((D_MODEL, bn), lambda j: (0, j)),
                  pl.BlockSpec((1, bn), lambda j: (0, j))],
        out_specs=pl.BlockSpec((8, bn), lambda j: (0, j)),
        compiler_params=_cparams(("arbitrary",)),
        name="mod",
    )(cvec, w_mod, b_mod)


def _inproj_kernel(*refs, rope, emit_kv):
    if emit_kv:
        (x_ref, mod_ref, g1_ref, wm_ref, wt_ref, qg_ref, kg_ref, cos_ref, sin_ref,
         q_ref, k_ref, v_ref, gates_ref, z_ref, xbc_ref, dt_ref, kraw_ref, vraw_ref) = refs
    else:
        (x_ref, mod_ref, g1_ref, wm_ref, wt_ref, qg_ref, kg_ref, cos_ref, sin_ref,
         q_ref, k_ref, v_ref, gates_ref, z_ref, xbc_ref, dt_ref) = refs
    tm = x_ref.shape[0]
    x = x_ref[...]
    inv = lax.rsqrt(jnp.mean(x * x, axis=-1, keepdims=True) + EPS)
    h = (x * inv) * g1_ref[...]
    h = h * (1.0 + _mod(mod_ref, MOD_SCALE1)) + _mod(mod_ref, MOD_SHIFT1)
    hb = h.astype(BF16)

    lane = lax.broadcasted_iota(jnp.int32, (tm, LANES), 1)
    lo = lane < HEAD_DIM
    even = (lane & 1) == 0
    if rope:
        cos = cos_ref[...]
        sin = sin_ref[...]

    def rope_fn(blk):
        nxt = pltpu.roll(blk, LANES - 1, 1)
        prv = pltpu.roll(blk, 1, 1)
        return blk * cos + jnp.where(even, nxt, prv) * sin

    def head_norm(blk, g):
        sq = blk * blk
        s_all = jnp.sum(sq, axis=-1, keepdims=True)
        s_lo = jnp.sum(jnp.where(lo, sq, 0.0), axis=-1, keepdims=True)
        ms = jnp.where(lo, s_lo, s_all - s_lo) * (1.0 / HEAD_DIM)
        return blk * lax.rsqrt(ms + EPS) * g

    def dup_heads(blk):
        sw = pltpu.roll(blk, HEAD_DIM, 1)
        return jnp.where(lo, blk, sw), jnp.where(lo, sw, blk)

    qg = qg_ref[...]
    kg = kg_ref[...]
    q = jnp.dot(hb, wm_ref[:, M_Q:M_K], preferred_element_type=F32)
    for j in range(ATTN_W // LANES):
        blk = head_norm(q[:, j * LANES:(j + 1) * LANES], qg)
        if rope:
            blk = rope_fn(blk)
        q_ref[:, j * LANES:(j + 1) * LANES] = (blk * Q_SCALE).astype(q_ref.dtype)

    k = jnp.dot(hb, wm_ref[:, M_K:M_V], preferred_element_type=F32)
    v = jnp.dot(hb, wm_ref[:, M_V:M_Z], preferred_element_type=F32)
    for j in range(KV_W // LANES):
        kb = head_norm(k[:, j * LANES:(j + 1) * LANES], kg)
        vb = v[:, j * LANES:(j + 1) * LANES]
        if emit_kv:
            kraw_ref[:, j * LANES:(j + 1) * LANES] = kb
            vraw_ref[:, j * LANES:(j + 1) * LANES] = vb
        if rope:
            kb = rope_fn(kb)
        for i, (kd, vd) in enumerate(zip(dup_heads(kb), dup_heads(vb))):
            c0 = (2 * j + i) * LANES
            k_ref[:, c0:c0 + LANES] = kd.astype(k_ref.dtype)
            v_ref[:, c0:c0 + LANES] = vd.astype(v_ref.dtype)

    gates_ref[...] = jnp.dot(hb, wt_ref[:, T_G:T_DT], preferred_element_type=F32).astype(gates_ref.dtype)
    z_ref[...] = jnp.dot(hb, wm_ref[:, M_Z:M_X], preferred_element_type=F32).astype(z_ref.dtype)
    xbc_ref[...] = jnp.dot(hb, wm_ref[:, M_X:M_END], preferred_element_type=F32).astype(xbc_ref.dtype)
    dt_ref[...] = jnp.dot(hb, wt_ref[:, T_DT:T_END], preferred_element_type=F32)


def _inproj_call(x, mod_rows, g1, w_main, w_tail, qg, kg, cos, sin, *, rope, emit_kv, tm):
    b, t, _ = x.shape
    assert t % tm == 0, (t, tm)
    nt = t // tm
    tok = lambda width: pl.BlockSpec((None, tm, width), lambda bi, i: (bi, i, 0))
    const2 = lambda shape: pl.BlockSpec(shape, lambda bi, i: (0, 0))
    out_shape = [
        jax.ShapeDtypeStruct((b, t, ATTN_W), BF16),
        jax.ShapeDtypeStruct((b, t, KVD_W), BF16),
        jax.ShapeDtypeStruct((b, t, KVD_W), BF16),
        jax.ShapeDtypeStruct((b, t, 2 * D_MODEL), BF16),
        jax.ShapeDtypeStruct((b, t, D_INNER), BF16),
        jax.ShapeDtypeStruct((b, t, CONV_CH), BF16),
        jax.ShapeDtypeStruct((b, t, 2 * LANES), F32),
    ]
    out_specs = [tok(ATTN_W), tok(KVD_W), tok(KVD_W), tok(2 * D_MODEL), tok(D_INNER), tok(CONV_CH),
                 tok(2 * LANES)]
    if emit_kv:
        out_shape += [jax.ShapeDtypeStruct((b, t, KV_W), F32)] * 2
        out_specs += [tok(KV_W), tok(KV_W)]
    return pl.pallas_call(
        functools.partial(_inproj_kernel, rope=rope, emit_kv=emit_kv),
        out_shape=tuple(out_shape),
        grid=(b, nt),
        in_specs=[tok(D_MODEL),
                  pl.BlockSpec((None, 1, 6 * D_MODEL), lambda bi, i: (jnp.minimum(bi, mod_rows.shape[0] - 1), 0, 0)),
                  const2((1, D_MODEL)),
                  pl.BlockSpec((D_MODEL, M_END), lambda bi, i: (0, 0), pipeline_mode=pl.Buffered(1)),
                  pl.BlockSpec((D_MODEL, T_END), lambda bi, i: (0, 0), pipeline_mode=pl.Buffered(1)),
                  const2((1, LANES)), const2((1, LANES)),
                  pl.BlockSpec((tm, LANES), lambda bi, i: (i, 0)),
                  pl.BlockSpec((tm, LANES), lambda bi, i: (i, 0))],
        out_specs=tuple(out_specs),
        compiler_params=_cparams(("arbitrary", "arbitrary")),
        name="inproj",
    )(x, mod_rows, g1, w_main, w_tail, qg, kg, cos, sin)


KEY_CHUNK = 512


def _key_chunk(n):
    return KEY_CHUNK if n % KEY_CHUNK == 0 else n


def _attn_kernel(*refs, has_ctx):
    if has_ctx:
        q_ref, k_ref, v_ref, kctx_ref, vctx_ref, o_ref = refs
        sources = ((k_ref, v_ref), (kctx_ref, vctx_ref))
    else:
        q_ref, k_ref, v_ref, o_ref = refs
        sources = ((k_ref, v_ref),)
    tq = q_ref.shape[0]
    lane = lax.broadcasted_iota(jnp.int32, (tq, LANES), 1)
    lo = lane < HEAD_DIM
    qs = []
    for j in range(2):
        q2 = q_ref[:, j * LANES:(j + 1) * LANES]
        zero = jnp.zeros_like(q2)
        qs += [jnp.where(lo, q2, zero), jnp.where(lo, zero, q2)]
    q4 = jnp.concatenate(qs, axis=0)
    rows = 4 * tq
    m = jnp.full((rows, 1), -jnp.inf, F32)
    acc = jnp.zeros((rows, LANES), F32)
    chunks = [(kr, vr, c, _key_chunk(kr.shape[0])) for kr, vr in sources
              for c in range(kr.shape[0] // _key_chunk(kr.shape[0]))]
    for kr, vr, c, kc in chunks:
        kch = kr[c * kc:(c + 1) * kc, :].astype(BF16)
        vch = vr[c * kc:(c + 1) * kc, :].astype(BF16)
        lane_k = lax.broadcasted_iota(jnp.int32, (kc, LANES), 1)
        vch = jnp.where(lane_k < HEAD_DIM, vch, jnp.ones_like(vch))
        s = _bdot_nt(q4, kch)
        m_new = jnp.maximum(m, jnp.max(s, axis=-1, keepdims=True))
        alpha = jnp.exp2(m - m_new)
        p = jnp.exp2((s - m_new).astype(BF16))
        acc = acc * alpha + jnp.dot(p, vch, preferred_element_type=F32)
        m = m_new
    o = acc * (1.0 / pltpu.roll(acc, HEAD_DIM, 1))
    for j in range(2):
        oa = o[(2 * j) * tq:(2 * j + 1) * tq]
        ob = pltpu.roll(o[(2 * j + 1) * tq:(2 * j + 2) * tq], HEAD_DIM, 1)
        o_ref[:, j * LANES:(j + 1) * LANES] = jnp.where(lo, oa, ob).astype(o_ref.dtype)


def _attn_call(q, k, v, kctx, vctx, *, tq):
    b, t, _ = q.shape
    assert t % tq == 0, (t, tq)
    tk = k.shape[1]
    nq = t // tq
    has_ctx = kctx is not None
    kv_spec = lambda n: pl.BlockSpec((None, n, LANES), lambda bi, g, i: (bi, 0, g))
    in_specs = [pl.BlockSpec((None, tq, 2 * LANES), lambda bi, g, i: (bi, i, g)), kv_spec(tk), kv_spec(tk)]
    args = [q, k, v]
    if has_ctx:
        in_specs += [kv_spec(kctx.shape[1]), kv_spec(kctx.shape[1])]
        args += [kctx, vctx]
    return pl.pallas_call(
        functools.partial(_attn_kernel, has_ctx=has_ctx),
        out_shape=jax.ShapeDtypeStruct((b, t, ATTN_W), BF16),
        grid=(b, N_KV_HEADS, nq),
        in_specs=in_specs,
        out_specs=pl.BlockSpec((None, tq, 2 * LANES), lambda bi, g, i: (bi, i, g)),
        compiler_params=_cparams(("arbitrary", "arbitrary", "arbitrary")),
        name="attn",
    )(*args)


LOG2E = 1.4426950408889634


def _softplus(x):
    return jnp.maximum(x, 0.0) + jnp.log(1.0 + jnp.exp(-jnp.abs(x)))


def _ssd_kernel(*refs, nc, reverse, has_h0, want_hfin):
    refs = list(refs)
    conv = not reverse
    if conv:
        xbc_ref, prev_ref, next_ref, cw_ref, cb_ref, dsk_ref = refs[:6]
        refs = refs[6:]
    else:
        xc_ref = refs.pop(0)
    dt_ref, an_ref, dtb_ref, tri_ref, sel_ref = refs[:5]
    refs = refs[5:]
    h0_ref = refs.pop(0) if has_h0 else None
    hprev_ref = refs.pop(0) if (want_hfin and reverse) else None
    y_ref = refs.pop(0)
    xco_ref = refs.pop(0) if conv else None
    hfin_ref = refs.pop(0) if want_hfin else None
    h_scr = refs.pop(0)

    L = CHUNK
    c = pl.program_id(1)
    cidx = (nc - 1 - c) if reverse else c

    @pl.when(c == 0)
    def _():
        if has_h0:
            h_scr[...] = h0_ref[...]
        else:
            h_scr[...] = jnp.zeros_like(h_scr)

    row = lax.broadcasted_iota(jnp.int32, (L, LANES), 0)
    lane = lax.broadcasted_iota(jnp.int32, (L, LANES), 1)
    lo = lane < SSD_HEAD_DIM
    top = row < SSD_HEAD_DIM

    if conv:
        first = cidx == 0
        last = cidx == nc - 1

        def cols(a, w):
            xm = xbc_ref[:, a:a + w].astype(F32)
            rw = lax.broadcasted_iota(jnp.int32, (L, w), 0)
            p6 = jnp.where(first, 0.0, prev_ref[6:7, a:a + w].astype(F32))
            p7 = jnp.where(first, 0.0, prev_ref[7:8, a:a + w].astype(F32))
            n0 = jnp.where(last, 0.0, next_ref[0:1, a:a + w].astype(F32))
            r1 = jnp.where(rw == 0, p7, pltpu.roll(xm, 1, 0))
            r2 = jnp.where(rw == 0, p6, jnp.where(rw == 1, p7, pltpu.roll(xm, 2, 0)))
            rn = jnp.where(rw == L - 1, n0, pltpu.roll(xm, L - 1, 0))
            y = (r2 * cw_ref[0:1, a:a + w] + r1 * cw_ref[1:2, a:a + w] + xm * cw_ref[2:3, a:a + w]
                 + rn * cw_ref[3:4, a:a + w] + cb_ref[:, a:a + w])
            y = _silu(y).astype(BF16)
            xco_ref[:, a:a + w] = y
            return y
    else:
        def cols(a, w):
            return xc_ref[:, a:a + w]

    causal = tri_ref[...] > 0.0
    dt = _softplus(dt_ref[...] + dtb_ref[...])
    la2 = dt * (an_ref[...] * LOG2E)
    acum2 = jnp.dot(tri_ref[...], la2, preferred_element_type=F32, precision=lax.Precision.HIGHEST)
    dt_t = dt.T
    acum2_t = acum2.T
    tot2_t = jnp.sum(la2.T, axis=1, keepdims=True)
    lg_dt_t = jnp.log2(dt_t)
    r_t = lg_dt_t - acum2_t
    w_t = jnp.exp2(lg_dt_t + tot2_t - acum2_t)
    e_acum_x = jnp.dot(jnp.exp2(acum2).astype(BF16), sel_ref[...], preferred_element_type=F32)
    e_tot_t = jnp.exp2(tot2_t)

    for g in range(SSD_GROUPS):
        bgb = cols(D_INNER + g * D_STATE, D_STATE)
        cgb = cols(D_INNER + SSD_GROUPS * D_STATE + g * D_STATE, D_STATE)
        cbm = _bdot_nt(cgb, bgb)
        h_grp = h_scr[4 * g:4 * g + 4]
        yo_grp = _bdot_nt(cgb, h_grp.reshape(4 * LANES, D_STATE))
        xws, cds = [], []
        for pr in range(4):
            hp = g * 4 + pr
            ha, hb = 2 * hp, 2 * hp + 1
            xpb = cols(hp * LANES, LANES)
            zero = jnp.zeros_like(xpb)
            xs = jnp.concatenate([jnp.where(lo, xpb, zero), jnp.where(lo, zero, xpb)], axis=0)
            ms = []
            for hh in (ha, hb):
                e = jnp.exp2(acum2[:, hh:hh + 1] + r_t[hh:hh + 1, :])
                ms.append((cbm * jnp.where(causal, e, 0.0)).astype(BF16))
            y = jnp.dot(jnp.concatenate(ms, axis=1), xs, preferred_element_type=F32)
            y = y + yo_grp[:, pr * LANES:(pr + 1) * LANES] * e_acum_x[:, hp * LANES:(hp + 1) * LANES]
            if conv:
                y = y + dsk_ref[:, hp * LANES:(hp + 1) * LANES] * xpb.astype(F32)
            y_ref[:, hp * LANES:(hp + 1) * LANES] = y.astype(y_ref.dtype)
            wsel = jnp.where(top, w_t[ha:ha + 1, :], w_t[hb:hb + 1, :])
            xws.append((xpb.astype(F32).T * wsel).astype(BF16))
            cds.append(jnp.where(top, e_tot_t[ha:ha + 1, :], e_tot_t[hb:hb + 1, :]))
        st = jnp.dot(jnp.concatenate(xws, axis=0), bgb, preferred_element_type=F32)
        for pr in range(4):
            h_scr[g * 4 + pr] = h_grp[pr] * cds[pr] + st[pr * LANES:(pr + 1) * LANES]

    if want_hfin:
        @pl.when(c == nc - 1)
        def _():
            if reverse:
                hfin_ref[0] = hprev_ref[...]
                hfin_ref[1] = h_scr[...]
            else:
                hfin_ref[...] = h_scr[...]


def _ssd_sweep(xin, dt, wts, h0, hprev, *, reverse, want_hfin):
    b, t, _ = xin.shape
    assert t % CHUNK == 0, t
    nc = t // CHUNK
    has_h0 = h0 is not None
    rb = CHUNK // 8
    nrb = t // 8
    d = 1 if reverse else 0
    cmap = (lambda c: nc - 1 - c) if reverse else (lambda c: c)
    hshape = (SSD_HEADS // 2, 2 * SSD_HEAD_DIM, D_STATE)

    chunk_spec = pl.BlockSpec((None, CHUNK, CONV_CH), lambda bi, c: (bi, cmap(c), 0))
    if reverse:
        in_specs = [chunk_spec]
        args = [xin]
    else:
        in_specs = [
            chunk_spec,
            pl.BlockSpec((None, 8, CONV_CH), lambda bi, c: (bi, jnp.maximum(c * rb - 1, 0), 0)),
            pl.BlockSpec((None, 8, CONV_CH), lambda bi, c: (bi, jnp.minimum((c + 1) * rb, nrb - 1), 0)),
            pl.BlockSpec((D_CONV, CONV_CH), lambda bi, c: (0, 0)),
            pl.BlockSpec((1, CONV_CH), lambda bi, c: (0, 0)),
            pl.BlockSpec((1, D_INNER), lambda bi, c: (0, 0)),
        ]
        args = [xin, xin, xin, wts["conv_w"], wts["conv_b"], wts["dskip"]]
    in_specs += [
        pl.BlockSpec((None, CHUNK, LANES), lambda bi, c: (bi, cmap(c), d)),
        pl.BlockSpec((None, 1, LANES), lambda bi, c: (d, 0, 0)),
        pl.BlockSpec((None, 1, LANES), lambda bi, c: (d, 0, 0)),
        pl.BlockSpec((None, CHUNK, CHUNK), lambda bi, c: (d, 0, 0)),
        pl.BlockSpec((LANES, D_INNER), lambda bi, c: (0, 0)),
    ]
    args += [dt, wts["a_neg"], wts["dt_bias"], wts["tri"], wts["head_sel"]]
    if has_h0:
        in_specs.append(pl.BlockSpec((None, None) + hshape, lambda bi, c: (bi, d, 0, 0, 0)))
        args.append(h0)
    if want_hfin and reverse:
        in_specs.append(pl.BlockSpec((None,) + hshape, lambda bi, c: (bi, 0, 0, 0)))
        args.append(hprev)
    out_shape = [jax.ShapeDtypeStruct((b, t, D_INNER), BF16)]
    out_specs = [pl.BlockSpec((None, CHUNK, D_INNER), lambda bi, c: (bi, cmap(c), 0))]
    if not reverse:
        out_shape.append(jax.ShapeDtypeStruct((b, t, CONV_CH), BF16))
        out_specs.append(pl.BlockSpec((None, CHUNK, CONV_CH), lambda bi, c: (bi, c, 0)))
    if want_hfin and reverse:
        out_shape.append(jax.ShapeDtypeStruct((b, 2) + hshape, F32))
        out_specs.append(pl.BlockSpec((None, 2) + hshape, lambda bi, c: (bi, 0, 0, 0, 0)))
    elif want_hfin:
        out_shape.append(jax.ShapeDtypeStruct((b,) + hshape, F32))
        out_specs.append(pl.BlockSpec((None,) + hshape, lambda bi, c: (bi, 0, 0, 0)))
    return pl.pallas_call(
        functools.partial(_ssd_kernel, nc=nc, reverse=reverse, has_h0=has_h0, want_hfin=want_hfin),
        out_shape=tuple(out_shape),
        grid=(b, nc),
        in_specs=in_specs,
        out_specs=tuple(out_specs),
        scratch_shapes=[pltpu.VMEM(hshape, F32)],
        compiler_params=_cparams(("arbitrary", "arbitrary")),
        name="ssd_bwd" if reverse else "ssd_fwd",
    )(*args)


def _ssd_call(xbc, dt, wts, h0, *, want_hfin):
    res = _ssd_sweep(xbc, dt, wts, h0, None, reverse=False, want_hfin=want_hfin)
    y_f, xc = res[0], res[1]
    hf = res[2] if want_hfin else None
    res = _ssd_sweep(xc, dt, wts, h0, hf, reverse=True, want_hfin=want_hfin)
    return y_f, res[0], (res[1] if want_hfin else None)


def _route(logits_t, bias_col):
    e, n = logits_t.shape
    per = e // N_EXPERT_GROUPS
    scores = jax.nn.sigmoid(logits_t)
    sel = scores + bias_col
    neg = jnp.float32(-jnp.inf)
    gs = []
    for g in range(N_EXPERT_GROUPS):
        blk = sel[g * per:(g + 1) * per, :]
        m1 = jnp.max(blk, axis=0, keepdims=True)
        is_m1 = blk == m1
        cnt = jnp.sum(jnp.where(is_m1, 1.0, 0.0), axis=0, keepdims=True)
        m2 = jnp.max(jnp.where(is_m1, neg, blk), axis=0, keepdims=True)
        gs.append(m1 + jnp.where(cnt >= 2.0, m1, m2))
    keep = []
    for g in range(N_EXPERT_GROUPS):
        rank = jnp.zeros_like(gs[g])
        for j in range(N_EXPERT_GROUPS):
            if j == g:
                continue
            beats = (gs[j] > gs[g]) if j > g else (gs[j] >= gs[g])
            rank = rank + jnp.where(beats, 1.0, 0.0)
        keep.append(rank < float(TOPK_GROUPS))
    selm = jnp.concatenate(
        [jnp.where(keep[g], sel[g * per:(g + 1) * per, :], neg) for g in range(N_EXPERT_GROUPS)], axis=0)
    eidx = lax.broadcasted_iota(jnp.int32, (e, n), 0).astype(F32)
    cur = selm
    picks = []
    for _ in range(TOP_K):
        m = jnp.max(cur, axis=0, keepdims=True)
        idx = jnp.min(jnp.where(cur == m, eidx, float(e)), axis=0, keepdims=True)
        hit = eidx == idx
        picks.append((idx, hit))
        cur = jnp.where(hit, neg, cur)
    return scores, picks


def _pack_bf16_pairs(h):
    c = h.shape[1] // 2
    lo = pltpu.bitcast(h[:, :c].astype(BF16).astype(F32), jnp.uint32)
    hi = pltpu.bitcast(h[:, c:].astype(BF16).astype(F32), jnp.uint32)
    return (lo >> 16) | (hi & jnp.uint32(0xFFFF0000))


def _unpack_bf16_pairs(w):
    lo = pltpu.bitcast(w << 16, F32)
    hi = pltpu.bitcast(w & jnp.uint32(0xFFFF0000), F32)
    return lo, hi


def _rows8(rows):
    n = rows[0].shape[1]
    ridx = lax.broadcasted_iota(jnp.int32, (TOP_K, n), 0)
    out = jnp.zeros((TOP_K, n), rows[0].dtype)
    for k, r in enumerate(rows):
        out = jnp.where(ridx == k, r, out)
    return out


MERGE_SUB = 512


def _merge_kernel(x_ref, attn_ref, yf_ref, yb_ref, z_ref, gates_ref, mod_ref, wa_ref, ws_ref, wo_ref,
                  sg_ref, n2_ref, wr_ref, rb_ref, wsg_ref, wsu_ref, wsd_ref,
                  xb_ref, ha_ref, hb_ref, eid_ref, pos_ref, wk_ref, cnt_ref):
    tm = x_ref.shape[0]
    sub = MERGE_SUB

    @pl.when(pl.program_id(0) == 0)
    def _():
        cnt_ref[...] = jnp.zeros_like(cnt_ref)

    r_i = lax.broadcasted_iota(jnp.int32, (sub, sub), 0)
    c_i = lax.broadcasted_iota(jnp.int32, (sub, sub), 1)
    before = jnp.where(r_i < c_i, 1.0, 0.0).astype(BF16)
    cnt = cnt_ref[:, 0:1]

    for r0 in range(0, tm, sub):
        rs = slice(r0, r0 + sub)
        x = x_ref[rs, :]
        yy = yf_ref[rs, :].astype(F32) + yb_ref[rs, :].astype(F32)
        u = yy * _silu(z_ref[rs, :]).astype(F32)
        un = u * lax.rsqrt(jnp.mean(u * u, axis=-1, keepdims=True) + EPS) * sg_ref[...]
        ssd_o = _bdot(un, ws_ref[...])
        attn_o = jnp.dot(attn_ref[rs, :], wa_ref[...], preferred_element_type=F32)
        ga = jax.nn.sigmoid(gates_ref[rs, 0:D_MODEL]).astype(F32)
        gs = jax.nn.sigmoid(gates_ref[rs, D_MODEL:2 * D_MODEL]).astype(F32)
        mix = _bdot(ga * attn_o + gs * ssd_o, wo_ref[...])
        x1 = x + _mod(mod_ref, MOD_GATE1) * mix
        h2 = x1 * lax.rsqrt(jnp.mean(x1 * x1, axis=-1, keepdims=True) + EPS) * n2_ref[...]
        h2 = h2 * (1.0 + _mod(mod_ref, MOD_SCALE2)) + _mod(mod_ref, MOD_SHIFT2)
        h2b = h2.astype(BF16)
        ha_ref[rs, :] = _pack_bf16_pairs(h2[:, :D_MODEL // 2])
        hb_ref[rs, :] = _pack_bf16_pairs(h2[:, D_MODEL // 2:])

        logits_t = _bdot_nt(wr_ref[...], h2b)
        scores, picks = _route(logits_t, rb_ref[...])
        chosen = jnp.zeros_like(scores)
        for _, hit in picks:
            chosen = chosen + jnp.where(hit, 1.0, 0.0)
        pos = cnt + jnp.dot(chosen.astype(BF16), before, preferred_element_type=F32)
        cnt = cnt + jnp.sum(chosen, axis=1, keepdims=True)
        poss = [jnp.sum(jnp.where(hit, pos, 0.0), axis=0, keepdims=True) for _, hit in picks]
        wks = [jnp.sum(jnp.where(hit, scores, 0.0), axis=0, keepdims=True) for _, hit in picks]
        wsum = wks[0]
        for w in wks[1:]:
            wsum = wsum + w
        eid_ref[:, rs] = _rows8([idx for idx, _ in picks]).astype(jnp.int32)
        pos_ref[:, rs] = _rows8(poss).astype(jnp.int32)
        wk8 = _rows8(wks) / wsum * ROUTED_SCALE
        wk_ref[rs, :] = jnp.concatenate([wk8, jnp.zeros((LANES - TOP_K, sub), F32)], axis=0).T

        hid = _silu(jnp.dot(h2b, wsg_ref[...], preferred_element_type=F32)) * \
            jnp.dot(h2b, wsu_ref[...], preferred_element_type=F32)
        xb_ref[rs, :] = x1 + _mod(mod_ref, MOD_GATE2) * _bdot(hid, wsd_ref[...])

    cnt_ref[...] = jnp.broadcast_to(cnt, cnt_ref.shape)


def _merge_call(x, attn, y_f, y_b, z, gates, mod_rows, wa, ws, wo, sg, n2, wr_t, rb, wsg, wsu, wsd, *, tm):
    b, t, _ = x.shape
    n = b * t
    assert n % tm == 0 and (t % tm == 0 or mod_rows.shape[0] == 1), (n, t, tm)
    flat = lambda a: a.reshape(n, a.shape[-1])
    tok = lambda width: pl.BlockSpec((tm, width), lambda i: (i, 0))
    const2 = lambda shape: pl.BlockSpec(shape, lambda i: (0, 0), pipeline_mode=pl.Buffered(1))
    k8 = pl.BlockSpec((TOP_K, tm), lambda i: (0, i))
    half = D_MODEL // 4
    return pl.pallas_call(
        _merge_kernel,
        out_shape=(jax.ShapeDtypeStruct((n, D_MODEL), F32),
                   jax.ShapeDtypeStruct((n, half), jnp.uint32),
                   jax.ShapeDtypeStruct((n, half), jnp.uint32),
                   jax.ShapeDtypeStruct((TOP_K, n), jnp.int32),
                   jax.ShapeDtypeStruct((TOP_K, n), jnp.int32),
                   jax.ShapeDtypeStruct((n, LANES), F32),
                   jax.ShapeDtypeStruct((N_EXPERTS, LANES), F32)),
        grid=(n // tm,),
        in_specs=[tok(D_MODEL), tok(ATTN_W), tok(D_INNER), tok(D_INNER), tok(D_INNER), tok(2 * D_MODEL),
                  pl.BlockSpec((None, 1, 6 * D_MODEL), lambda i: (jnp.minimum((i * tm) // t, mod_rows.shape[0] - 1), 0, 0)),
                  const2((ATTN_W, D_MODEL)), const2((D_INNER, D_MODEL)), const2((D_MODEL, D_MODEL)),
                  const2((1, D_INNER)), const2((1, D_MODEL)),
                  const2((N_EXPERTS, D_MODEL)), const2((N_EXPERTS, 1)),
                  const2((D_MODEL, D_SHARED)), const2((D_MODEL, D_SHARED)), const2((D_SHARED, D_MODEL))],
        out_specs=(tok(D_MODEL), tok(half), tok(half), k8, k8, tok(LANES),
                   pl.BlockSpec((N_EXPERTS, LANES), lambda i: (0, 0))),
        compiler_params=_cparams(("arbitrary",)),
        name="merge",
    )(flat(x), flat(attn), flat(y_f), flat(y_b), flat(z), flat(gates), mod_rows, wa, ws, wo, sg, n2, wr_t, rb,
      wsg, wsu, wsd)


ROW_TILE = 512
SC_WINDOW = 128


def _slots_kernel(start_ref, eid_ref, pos_ref, slot_ref):
    eid = eid_ref[...]
    slot = pos_ref[...]
    for e in range(N_EXPERTS):
        slot = slot + jnp.where(eid == e, start_ref[e], 0)
    slot_ref[...] = slot


def _slots_call(start, eid, pos):
    n = eid.shape[1]
    bn = 2048 if n % 2048 == 0 else n
    spec = pl.BlockSpec((TOP_K, bn), lambda i, s: (0, i))
    return pl.pallas_call(
        _slots_kernel,
        out_shape=jax.ShapeDtypeStruct((TOP_K, n), jnp.int32),
        grid_spec=pltpu.PrefetchScalarGridSpec(num_scalar_prefetch=1, grid=(n // bn,),
                                               in_specs=[spec, spec], out_specs=spec),
        compiler_params=_cparams(("arbitrary",)),
        name="slots",
    )(start, eid, pos)


def _sc_dispatch(x, slots, p):
    n, d = x.shape
    assert n % SC_WINDOW == 0, n
    mesh = plsc.VectorSubcoreMesh(core_axis_name="core", subcore_axis_name="subcore")

    @functools.partial(pl.kernel, out_type=jax.ShapeDtypeStruct((p, d), x.dtype), mesh=mesh)
    def k(x_hbm, s_hbm, o_hbm):
        def body(x_vmem, s_vmem):
            for kk in range(TOP_K):
                pltpu.sync_copy(x_vmem, o_hbm.at[s_vmem.at[kk]])

        pltpu.emit_pipeline(
            body,
            grid=(n // SC_WINDOW,),
            in_specs=[pl.BlockSpec((SC_WINDOW, d), index_map=lambda i: (i, 0)),
                      pl.BlockSpec((TOP_K, SC_WINDOW), index_map=lambda i: (0, i))],
            out_specs=[],
            core_axis_name=("core", "subcore"),
            dimension_semantics=(pltpu.PARALLEL,),
        )(x_hbm, s_hbm)

    return k(x, slots)


def _sc_combine(y, slots):
    kk, n = slots.shape
    assert n % SC_WINDOW == 0, n
    d = y.shape[1]
    mesh = plsc.VectorSubcoreMesh(core_axis_name="core", subcore_axis_name="subcore")

    @functools.partial(pl.kernel, out_type=jax.ShapeDtypeStruct((kk * n, d), y.dtype), mesh=mesh)
    def k(y_hbm, s_hbm, o_hbm):
        def body(s_vmem, o_vmem):
            pltpu.sync_copy(y_hbm.at[s_vmem.at[0]], o_vmem)

        pltpu.emit_pipeline(
            body,
            grid=(kk * n // SC_WINDOW,),
            in_specs=[pl.BlockSpec((1, SC_WINDOW), index_map=lambda i: (0, i))],
            out_specs=[pl.BlockSpec((SC_WINDOW, d), index_map=lambda i: (i, 0))],
            core_axis_name=("core", "subcore"),
            dimension_semantics=(pltpu.PARALLEL,),
        )(s_hbm, o_hbm)

    return k(y, slots.reshape(1, kk * n)).reshape(kk, n, d)


FFN_IN_BUFS = 4
FFN_OUT_BUFS = 3
FFN_ROW_DMA_PRIORITY = 1


def _ffn_kernel(st_ref, nt_ref, wg_ref, wu_ref, wd_ref, xa_hbm, xb_hbm, ya_hbm, yb_hbm,
                wg_s, wu_s, wd_s, xa_buf, xb_buf, ya_buf, yb_buf, in_sem, out_sem, done_ref):
    e = pl.program_id(0)
    ne = pl.num_programs(0)
    nxt = jnp.minimum(e + 1, ne - 1)
    n = nt_ref[e]
    n_next = jnp.where(e + 1 < ne, nt_ref[nxt], 0)
    t = ROW_TILE
    ahead = FFN_IN_BUFS - 1

    def fetch(row, slot):
        r = pl.multiple_of(row, t)
        return (pltpu.make_async_copy(xa_hbm.at[pl.ds(r, t)], xa_buf.at[slot], in_sem.at[0, slot]),
                pltpu.make_async_copy(xb_hbm.at[pl.ds(r, t)], xb_buf.at[slot], in_sem.at[1, slot]))

    def put(row, slot):
        r = pl.multiple_of(row, t)
        return (pltpu.make_async_copy(ya_buf.at[slot], ya_hbm.at[pl.ds(r, t)], out_sem.at[0, slot]),
                pltpu.make_async_copy(yb_buf.at[slot], yb_hbm.at[pl.ds(r, t)], out_sem.at[1, slot]))

    def wait_put(slot):
        for c in put(0, slot):
            c.wait()

    def start_fetch(row, g):
        for c in fetch(row, lax.rem(g, FFN_IN_BUFS)):
            c.start(priority=FFN_ROW_DMA_PRIORITY)

    @pl.when(e == 0)
    def _():
        done_ref[0] = 0
        for k in range(ahead):
            @pl.when(k < n)
            def _():
                start_fetch(st_ref[0] + k * t, k)

    done = done_ref[0]

    def prefetch(i):
        j = i + ahead - n

        @pl.when(j < 0)
        def _():
            start_fetch(st_ref[e] + (i + ahead) * t, done + i + ahead)

        @pl.when(jnp.logical_and(j >= 0, j < n_next))
        def _():
            start_fetch(st_ref[nxt] + j * t, done + i + ahead)

    @pl.when(n > 0)
    def _():
        wg_s[...] = wg_ref[...].astype(BF16)
        wu_s[...] = wu_ref[...].astype(BF16)
        wd_s[...] = wd_ref[...].astype(BF16)
        base = st_ref[e]

        def body(i, carry):
            g = done + i
            si = lax.rem(g, FFN_IN_BUFS)
            so = lax.rem(g, FFN_OUT_BUFS)
            row = base + i * t
            for c in fetch(row, si):
                c.wait()
            prefetch(i)

            @pl.when(g >= FFN_OUT_BUFS)
            def _():
                wait_put(so)

            parts = _unpack_bf16_pairs(xa_buf[si]) + _unpack_bf16_pairs(xb_buf[si])
            x = jnp.concatenate(parts, axis=1).astype(BF16)
            hid = _silu(jnp.dot(x, wg_s[...], preferred_element_type=F32)) * \
                jnp.dot(x, wu_s[...], preferred_element_type=F32)
            y = _bdot(hid, wd_s[...])
            ya_buf[so] = _pack_bf16_pairs(y[:, :D_MODEL // 2])
            yb_buf[so] = _pack_bf16_pairs(y[:, D_MODEL // 2:])
            for c in put(row, so):
                c.start(priority=FFN_ROW_DMA_PRIORITY)
            return carry

        lax.fori_loop(0, n, body, 0)
        done_ref[0] = done + n

    for k in range(ahead):
        @pl.when(jnp.logical_and(k < ahead - n, k < n_next))
        def _():
            start_fetch(st_ref[nxt] + k * t, done + n + k)

    @pl.when(e == ne - 1)
    def _():
        total = done_ref[0]
        for k in range(FFN_OUT_BUFS):
            @pl.when(total - 1 - k >= 0)
            def _():
                wait_put(lax.rem(total - 1 - k, FFN_OUT_BUFS))


def _ffn_call(start, tiles_e, xa, xb, wg, wu, wd):
    p, half = xa.shape
    wspec = lambda s: pl.BlockSpec((None,) + s, lambda e, st, nt: (e, 0, 0))
    hbm = pl.BlockSpec(memory_space=pl.ANY)
    ibuf = pltpu.VMEM((FFN_IN_BUFS, ROW_TILE, half), jnp.uint32)
    obuf = pltpu.VMEM((FFN_OUT_BUFS, ROW_TILE, half), jnp.uint32)
    return pl.pallas_call(
        _ffn_kernel,
        out_shape=(jax.ShapeDtypeStruct((p, half), jnp.uint32), jax.ShapeDtypeStruct((p, half), jnp.uint32)),
        grid_spec=pltpu.PrefetchScalarGridSpec(
            num_scalar_prefetch=2, grid=(N_EXPERTS,),
            in_specs=[wspec((D_MODEL, D_EXPERT)), wspec((D_MODEL, D_EXPERT)), wspec((D_EXPERT, D_MODEL)),
                      hbm, hbm],
            out_specs=(hbm, hbm),
            scratch_shapes=[pltpu.VMEM((D_MODEL, D_EXPERT), BF16), pltpu.VMEM((D_MODEL, D_EXPERT), BF16),
                            pltpu.VMEM((D_EXPERT, D_MODEL), BF16), ibuf, ibuf, obuf, obuf,
                            pltpu.SemaphoreType.DMA((2, FFN_IN_BUFS)),
                            pltpu.SemaphoreType.DMA((2, FFN_OUT_BUFS)),
                            pltpu.SMEM((1,), jnp.int32)]),
        compiler_params=_cparams(("arbitrary",)),
        name="ffn",
    )(start, tiles_e, wg, wu, wd, xa, xb)


def _final_kernel(xb_ref, ya_ref, yb_ref, wk_ref, mod_ref, o_ref):
    q = D_MODEL // 4
    accs = [jnp.zeros((xb_ref.shape[0], q), F32) for _ in range(4)]
    for k in range(TOP_K):
        w = wk_ref[:, k:k + 1]
        parts = _unpack_bf16_pairs(ya_ref[k]) + _unpack_bf16_pairs(yb_ref[k])
        accs = [a + w * p for a, p in zip(accs, parts)]
    for i, a in enumerate(accs):
        o_ref[:, i * q:(i + 1) * q] = xb_ref[:, i * q:(i + 1) * q] + _mod(mod_ref, MOD_GATE2, i * q, (i + 1) * q) * a


def _final_call(xb, ya, yb, wk, mod_rows, t, *, tm):
    n = xb.shape[0]
    assert n % tm == 0 and (t % tm == 0 or mod_rows.shape[0] == 1), (n, t, tm)
    half = ya.shape[2]
    tok = lambda width: pl.BlockSpec((tm, width), lambda i: (i, 0))
    yspec = pl.BlockSpec((TOP_K, tm, half), lambda i: (0, i, 0))
    return pl.pallas_call(
        _final_kernel,
        out_shape=jax.ShapeDtypeStruct((n, D_MODEL), F32),
        grid=(n // tm,),
        in_specs=[tok(D_MODEL), yspec, yspec, tok(LANES),
                  pl.BlockSpec((None, 1, 6 * D_MODEL), lambda i: (jnp.minimum((i * tm) // t, mod_rows.shape[0] - 1), 0, 0))],
        out_specs=tok(D_MODEL),
        compiler_params=_cparams(("arbitrary",)),
        name="final",
    )(xb, ya, yb, wk, mod_rows)


def _moe_call(ha, hb, eid, pos, wk, counts, xb, mod_rows, wg, wu, wd, t):
    n = xb.shape[0]
    max_tiles = n * TOP_K // ROW_TILE + N_EXPERTS
    p = max_tiles * ROW_TILE
    cnt = counts[:, 0].astype(jnp.int32)
    tiles_e = (cnt + ROW_TILE - 1) // ROW_TILE
    ends = jnp.cumsum(tiles_e)
    start = (ends - tiles_e) * ROW_TILE
    slots = _slots_call(start.astype(jnp.int32), eid, pos)
    xa = _sc_dispatch(ha, slots, p)
    xbb = _sc_dispatch(hb, slots, p)
    ya, yb = _ffn_call(start.astype(jnp.int32), tiles_e.astype(jnp.int32), xa, xbb, wg, wu, wd)
    ga = _sc_combine(ya, slots)
    gb = _sc_combine(yb, slots)
    return _final_call(xb, ga, gb, wk, mod_rows, t, tm=256)


def _rope_tables(t):
    n_rows = t // GRID_W
    rows = np.repeat(np.arange(n_rows), GRID_W).astype(np.float32)
    cols = np.tile(np.arange(GRID_W), n_rows).astype(np.float32)
    n_freq = HEAD_DIM // 4
    freqs = (np.float32(ROPE_THETA) ** (-np.arange(n_freq, dtype=np.float32) / np.float32(n_freq))).astype(np.float32)
    ang = np.concatenate([rows[:, None] * freqs, cols[:, None] * freqs], axis=-1)
    ang = np.repeat(ang, 2, axis=-1)
    ang = np.concatenate([ang, ang], axis=-1).astype(np.float32)
    sign = np.where(np.arange(LANES) % 2 == 0, -1.0, 1.0).astype(np.float32)
    return jnp.asarray(np.cos(ang)), jnp.asarray(np.sin(ang) * sign)


def _dup_heads(a):
    s = a.shape[:-1]
    a4 = a.reshape(s + (N_KV_HEADS, HEAD_DIM))
    return jnp.concatenate([a4, a4], axis=-1).reshape(s + (KVD_W,))


def _prep_w_in(w_in):
    pad = jnp.zeros((D_MODEL, LANES - SSD_HEADS), w_in.dtype)
    dt0 = M_END
    g0 = dt0 + 2 * SSD_HEADS
    tail = [w_in[:, g0:g0 + 2 * D_MODEL], w_in[:, dt0:dt0 + SSD_HEADS], pad,
            w_in[:, dt0 + SSD_HEADS:g0], pad]
    return w_in[:, :M_END].astype(BF16), jnp.concatenate(tail, axis=-1).astype(BF16)


def _pad_heads(a):
    return jnp.pad(a.astype(F32), ((0, 0), (0, LANES - SSD_HEADS)))[:, None, :]


def _trunk(x, mod_rows, wts, rope_tabs, ctx_k, ctx_v, h0, *, tm, tq, want_state):
    b, t, _ = x.shape
    rope = rope_tabs is not None
    if rope:
        cos, sin = rope_tabs
    else:
        cos = sin = jnp.zeros((t, LANES), F32)
    res = _inproj_call(x, mod_rows, wts["g1"], *wts["w_in"], wts["qg"], wts["kg"], cos, sin,
                       rope=rope, emit_kv=want_state, tm=tm)
    q, k, v, gates, z, xbc, dt = res[:7]
    kv_raw = res[7:]
    attn = _attn_call(q, k, v, ctx_k, ctx_v, tq=tq)
    y_f, y_b, hfin = _ssd_call(xbc, dt, wts, h0, want_hfin=want_state)
    xb, ha, hb, eid, pos, wk, counts = _merge_call(
        x, attn, y_f, y_b, z, gates, mod_rows, wts["wa"], wts["ws"], wts["wo"], wts["sg"],
        wts["n2"], wts["wr_t"], wts["rb"], wts["wsg"], wts["wsu"], wts["wsd"], tm=MERGE_SUB)
    out = _moe_call(ha, hb, eid, pos, wk, counts, xb, mod_rows, wts["weg"], wts["weu"], wts["wed"], t)
    return out.reshape(b, t, D_MODEL), kv_raw, hfin


def kernel(x_prompt, x_sample, cache_k, cache_v, state_ssm, c, c_ctx, w_mod, b_mod, norm1_g, norm2_g, w_in,
           q_norm_g, k_norm_g, conv_w, conv_b, a_log, dt_bias, d_skip, ssd_norm_g, w_attn_proj, w_ssd_proj,
           w_out, w_router, router_bias, w_exp_gate, w_exp_up, w_exp_down, w_sh_gate, w_sh_up, w_sh_down):
    depth = w_mod.shape[0]
    assert depth == 1, "single trunk layer"
    bp, tp, _ = x_prompt.shape
    bs, ts, _ = x_sample.shape
    assert bs + 1 <= 8, "modulation rows are computed 8 at a time"
    l = 0
    cvec = jnp.concatenate([c_ctx[None, :], c, jnp.zeros((8 - 1 - bs, D_MODEL), F32)], axis=0)
    mod = _mod_call(cvec, w_mod.reshape(D_MODEL, 6 * D_MODEL), b_mod.reshape(1, 6 * D_MODEL))
    mod_prompt = mod[0:1][:, None, :]
    mod_sample = mod[1:1 + bs][:, None, :]

    lower = np.tril(np.ones((CHUNK, CHUNK), np.float32))
    head_sel = (np.arange(LANES)[:, None] == np.arange(D_INNER)[None, :] // SSD_HEAD_DIM).astype(np.float32)
    wts = dict(
        g1=norm1_g[l][None, :], n2=norm2_g[l][None, :],
        w_in=_prep_w_in(w_in.reshape(D_MODEL, w_in.shape[-1])),
        qg=jnp.tile(q_norm_g[l], 2)[None, :], kg=jnp.tile(k_norm_g[l], 2)[None, :],
        conv_w=conv_w[l], conv_b=conv_b[l][None, :],
        a_neg=_pad_heads(-jnp.exp(a_log[l].astype(F32))), dt_bias=_pad_heads(dt_bias[l]),
        dskip=jnp.repeat(d_skip[l].astype(F32), SSD_HEAD_DIM)[None, :],
        tri=jnp.asarray(np.stack([lower, lower.T])),
        head_sel=jnp.asarray(head_sel, BF16),
        sg=ssd_norm_g[l][None, :],
        wa=w_attn_proj[l].astype(BF16), ws=w_ssd_proj[l].astype(BF16), wo=w_out[l].astype(BF16),
        wr_t=w_router[l].T.astype(BF16), rb=router_bias[l].astype(F32)[:, None],
        wsg=w_sh_gate[l].astype(BF16), wsu=w_sh_up[l].astype(BF16), wsd=w_sh_down[l].astype(BF16),
        weg=w_exp_gate.reshape(w_exp_gate.shape[1:]), weu=w_exp_up.reshape(w_exp_up.shape[1:]),
        wed=w_exp_down.reshape(w_exp_down.shape[1:]),
    )

    y_prompt, (k_p, v_p), hfin = _trunk(x_prompt, mod_prompt, wts, None, None, None, None,
                                        tm=256, tq=256, want_state=True)
    new_k = k_p.reshape(bp, 1, tp, N_KV_HEADS, HEAD_DIM)
    new_v = v_p.reshape(bp, 1, tp, N_KV_HEADS, HEAD_DIM)
    new_state = hfin.reshape(bp, 1, 2, SSD_HEADS, SSD_HEAD_DIM, D_STATE)

    past = cache_k.shape[2]
    ctx_k = _dup_heads(cache_k[:, l].reshape(bs, past, KV_W)).astype(BF16)
    ctx_v = _dup_heads(cache_v[:, l].reshape(bs, past, KV_W)).astype(BF16)
    h0 = state_ssm[:, l].reshape(bs, 2, SSD_HEADS // 2, 2 * SSD_HEAD_DIM, D_STATE)
    y_sample, _, _ = _trunk(x_sample, mod_sample, wts, _rope_tables(ts), ctx_k, ctx_v, h0,
                               tm=512, tq=512, want_state=False)
    return (y_prompt, y_sample, new_k, new_v, new_state)
```

```python
import functools

import numpy as np
import jax
import jax.numpy as jnp
from jax import lax
from jax.experimental import pallas as pl
from jax.experimental.pallas import tpu as pltpu
from jax.experimental.pallas import tpu_sc as plsc

F32 = jnp.float32
BF16 = jnp.bfloat16

D_MODEL = 1024
GRID_W = 64
EPS = 1e-6
N_HEADS = 16
N_KV_HEADS = 4
HEAD_DIM = 64
ATTN_W = N_HEADS * HEAD_DIM
KV_W = N_KV_HEADS * HEAD_DIM
ROPE_THETA = 10000.0
D_INNER = 2048
SSD_HEAD_DIM = 64
SSD_HEADS = 32
SSD_GROUPS = 4
D_STATE = 128
D_CONV = 4
CHUNK = 128
CONV_CH = D_INNER + 2 * SSD_GROUPS * D_STATE
N_EXPERTS = 64
TOP_K = 8
N_EXPERT_GROUPS = 8
TOPK_GROUPS = 4
D_EXPERT = 256
D_SHARED = 256
ROUTED_SCALE = 2.5

LANES = 128
KVD_W = N_KV_HEADS * LANES
M_Q, M_K, M_V, M_Z, M_X, M_END = (int(c) for c in np.cumsum((0, ATTN_W, KV_W, KV_W, D_INNER, CONV_CH)))
T_G, T_DT, T_END = (int(c) for c in np.cumsum((0, 2 * D_MODEL, 2 * LANES)))
MOD_SHIFT1, MOD_SCALE1, MOD_GATE1, MOD_SHIFT2, MOD_SCALE2, MOD_GATE2 = range(6)
VMEM_LIMIT = 56 * 1024 * 1024
Q_SCALE = HEAD_DIM ** -0.5 * 1.4426950408889634


def _cparams(sem):
    return pltpu.CompilerParams(dimension_semantics=sem, vmem_limit_bytes=VMEM_LIMIT)


def _mod(mod_ref, which, lo=0, hi=D_MODEL):
    return mod_ref[:, which * D_MODEL + lo:which * D_MODEL + hi]


def _silu(x):
    return x * jax.nn.sigmoid(x)


def _bdot(a, b):
    return jnp.dot(a.astype(BF16), b.astype(BF16), preferred_element_type=F32)


def _bdot_nt(a, b):
    return lax.dot_general(a.astype(BF16), b.astype(BF16), (((1,), (1,)), ((), ())),
                           preferred_element_type=F32)


def _mod_kernel(c_ref, w_ref, b_ref, o_ref):
    o_ref[...] = _bdot(_silu(c_ref[...]), w_ref[...]) + b_ref[...]


def _mod_call(cvec, w_mod, b_mod):
    n = w_mod.shape[1]
    bn = 1024
    return pl.pallas_call(
        _mod_kernel,
        out_shape=jax.ShapeDtypeStruct((8, n), F32),
        grid=(n // bn,),
        in_specs=[pl.BlockSpec((8, D_MODEL), lambda j: (0, 0)),
                  pl.BlockSpec((D_MODEL, bn), lambda j: (0, j)),
                  pl.BlockSpec((1, bn), lambda j: (0, j))],
        out_specs=pl.BlockSpec((8, bn), lambda j: (0, j)),
        compiler_params=_cparams(("arbitrary",)),
        name="mod",
    )(cvec, w_mod, b_mod)


def _inproj_kernel(*refs, rope, emit_kv):
    if emit_kv:
        (x_ref, mod_ref, g1_ref, wm_ref, wt_ref, qg_ref, kg_ref, cos_ref, sin_ref,
         q_ref, k_ref, v_ref, gates_ref, z_ref, xbc_ref, dt_ref, kraw_ref, vraw_ref) = refs
    else:
        (x_ref, mod_ref, g1_ref, wm_ref, wt_ref, qg_ref, kg_ref, cos_ref, sin_ref,
         q_ref, k_ref, v_ref, gates_ref, z_ref, xbc_ref, dt_ref) = refs
    tm = x_ref.shape[0]
    x = x_ref[...]
    inv = lax.rsqrt(jnp.mean(x * x, axis=-1, keepdims=True) + EPS)
    h = (x * inv) * g1_ref[...]
    h = h * (1.0 + _mod(mod_ref, MOD_SCALE1)) + _mod(mod_ref, MOD_SHIFT1)
    hb = h.astype(BF16)

    lane = lax.broadcasted_iota(jnp.int32, (tm, LANES), 1)
    lo = lane < HEAD_DIM
    even = (lane & 1) == 0
    if rope:
        cos = cos_ref[...]
        sin = sin_ref[...]

    def rope_fn(blk):
        nxt = pltpu.roll(blk, LANES - 1, 1)
        prv = pltpu.roll(blk, 1, 1)
        return blk * cos + jnp.where(even, nxt, prv) * sin

    def head_norm(blk, g):
        sq = blk * blk
        s_all = jnp.sum(sq, axis=-1, keepdims=True)
        s_lo = jnp.sum(jnp.where(lo, sq, 0.0), axis=-1, keepdims=True)
        ms = jnp.where(lo, s_lo, s_all - s_lo) * (1.0 / HEAD_DIM)
        return blk * lax.rsqrt(ms + EPS) * g

    def dup_heads(blk):
        sw = pltpu.roll(blk, HEAD_DIM, 1)
        return jnp.where(lo, blk, sw), jnp.where(lo, sw, blk)

    qg = qg_ref[...]
    kg = kg_ref[...]
    q = jnp.dot(hb, wm_ref[:, M_Q:M_K], preferred_element_type=F32)
    for j in range(ATTN_W // LANES):
        blk = head_norm(q[:, j * LANES:(j + 1) * LANES], qg)
        if rope:
            blk = rope_fn(blk)
        q_ref[:, j * LANES:(j + 1) * LANES] = (blk * Q_SCALE).astype(q_ref.dtype)

    k = jnp.dot(hb, wm_ref[:, M_K:M_V], preferred_element_type=F32)
    v = jnp.dot(hb, wm_ref[:, M_V:M_Z], preferred_element_type=F32)
    for j in range(KV_W // LANES):
        kb = head_norm(k[:, j * LANES:(j + 1) * LANES], kg)
        vb = v[:, j * LANES:(j + 1) * LANES]
        if emit_kv:
            kraw_ref[:, j * LANES:(j + 1) * LANES] = kb
            vraw_ref[:, j * LANES:(j + 1) * LANES] = vb
        if rope:
            kb = rope_fn(kb)
        for i, (kd, vd) in enumerate(zip(dup_heads(kb), dup_heads(vb))):
            c0 = (2 * j + i) * LANES
            k_ref[:, c0:c0 + LANES] = kd.astype(k_ref.dtype)
            v_ref[:, c0:c0 + LANES] = vd.astype(v_ref.dtype)

    gates_ref[...] = jnp.dot(hb, wt_ref[:, T_G:T_DT], preferred_element_type=F32).astype(gates_ref.dtype)
    z_ref[...] = jnp.dot(hb, wm_ref[:, M_Z:M_X], preferred_element_type=F32).astype(z_ref.dtype)
    xbc_ref[...] = jnp.dot(hb, wm_ref[:, M_X:M_END], preferred_element_type=F32).astype(xbc_ref.dtype)
    dt_ref[...] = jnp.dot(hb, wt_ref[:, T_DT:T_END], preferred_element_type=F32)


def _inproj_call(x, mod_rows, g1, w_main, w_tail, qg, kg, cos, sin, *, rope, emit_kv, tm):
    b, t, _ = x.shape
    assert t % tm == 0, (t, tm)
    nt = t // tm
    tok = lambda width: pl.BlockSpec((None, tm, width), lambda bi, i: (bi, i, 0))
    const2 = lambda shape: pl.BlockSpec(shape, lambda bi, i: (0, 0))
    out_shape = [
        jax.ShapeDtypeStruct((b, t, ATTN_W), BF16),
        jax.ShapeDtypeStruct((b, t, KVD_W), BF16),
        jax.ShapeDtypeStruct((b, t, KVD_W), BF16),
        jax.ShapeDtypeStruct((b, t, 2 * D_MODEL), BF16),
        jax.ShapeDtypeStruct((b, t, D_INNER), BF16),
        jax.ShapeDtypeStruct((b, t, CONV_CH), BF16),
        jax.ShapeDtypeStruct((b, t, 2 * LANES), F32),
    ]
    out_specs = [tok(ATTN_W), tok(KVD_W), tok(KVD_W), tok(2 * D_MODEL), tok(D_INNER), tok(CONV_CH),
                 tok(2 * LANES)]
    if emit_kv:
        out_shape += [jax.ShapeDtypeStruct((b, t, KV_W), F32)] * 2
        out_specs += [tok(KV_W), tok(KV_W)]
    return pl.pallas_call(
        functools.partial(_inproj_kernel, rope=rope, emit_kv=emit_kv),
        out_shape=tuple(out_shape),
        grid=(b, nt),
        in_specs=[tok(D_MODEL),
                  pl.BlockSpec((None, 1, 6 * D_MODEL), lambda bi, i: (jnp.minimum(bi, mod_rows.shape[0] - 1), 0, 0)),
                  const2((1, D_MODEL)),
                  pl.BlockSpec((D_MODEL, M_END), lambda bi, i: (0, 0), pipeline_mode=pl.Buffered(1)),
                  pl.BlockSpec((D_MODEL, T_END), lambda bi, i: (0, 0), pipeline_mode=pl.Buffered(1)),
                  const2((1, LANES)), const2((1, LANES)),
                  pl.BlockSpec((tm, LANES), lambda bi, i: (i, 0)),
                  pl.BlockSpec((tm, LANES), lambda bi, i: (i, 0))],
        out_specs=tuple(out_specs),
        compiler_params=_cparams(("arbitrary", "arbitrary")),
        name="inproj",
    )(x, mod_rows, g1, w_main, w_tail, qg, kg, cos, sin)


KEY_CHUNK = 512


def _key_chunk(n):
    return KEY_CHUNK if n % KEY_CHUNK == 0 else n


def _attn_kernel(*refs, has_ctx):
    if has_ctx:
        q_ref, k_ref, v_ref, kctx_ref, vctx_ref, o_ref = refs
        sources = ((k_ref, v_ref), (kctx_ref, vctx_ref))
    else:
        q_ref, k_ref, v_ref, o_ref = refs
        sources = ((k_ref, v_ref),)
    tq = q_ref.shape[0]
    lane = lax.broadcasted_iota(jnp.int32, (tq, LANES), 1)
    lo = lane < HEAD_DIM
    qs = []
    for j in range(2):
        q2 = q_ref[:, j * LANES:(j + 1) * LANES]
        zero = jnp.zeros_like(q2)
        qs += [jnp.where(lo, q2, zero), jnp.where(lo, zero, q2)]
    q4 = jnp.concatenate(qs, axis=0)
    rows = 4 * tq
    m = jnp.full((rows, 1), -jnp.inf, F32)
    acc = jnp.zeros((rows, LANES), F32)
    chunks = [(kr, vr, c, _key_chunk(kr.shape[0])) for kr, vr in sources
              for c in range(kr.shape[0] // _key_chunk(kr.shape[0]))]
    for kr, vr, c, kc in chunks:
        kch = kr[c * kc:(c + 1) * kc, :].astype(BF16)
        vch = vr[c * kc:(c + 1) * kc, :].astype(BF16)
        lane_k = lax.broadcasted_iota(jnp.int32, (kc, LANES), 1)
        vch = jnp.where(lane_k < HEAD_DIM, vch, jnp.ones_like(vch))
        s = _bdot_nt(q4, kch)
        m_new = jnp.maximum(m, jnp.max(s, axis=-1, keepdims=True))
        alpha = jnp.exp2(m - m_new)
        p = jnp.exp2((s - m_new).astype(BF16))
        acc = acc * alpha + jnp.dot(p, vch, preferred_element_type=F32)
        m = m_new
    o = acc * (1.0 / pltpu.roll(acc, HEAD_DIM, 1))
    for j in range(2):
        oa = o[(2 * j) * tq:(2 * j + 1) * tq]
        ob = pltpu.roll(o[(2 * j + 1) * tq:(2 * j + 2) * tq], HEAD_DIM, 1)
        o_ref[:, j * LANES:(j + 1) * LANES] = jnp.where(lo, oa, ob).astype(o_ref.dtype)


def _attn_call(q, k, v, kctx, vctx, *, tq):
    b, t, _ = q.shape
    assert t % tq == 0, (t, tq)
    tk = k.shape[1]
    nq = t // tq
    has_ctx = kctx is not None
    kv_spec = lambda n: pl.BlockSpec((None, n, LANES), lambda bi, g, i: (bi, 0, g))
    in_specs = [pl.BlockSpec((None, tq, 2 * LANES), lambda bi, g, i: (bi, i, g)), kv_spec(tk), kv_spec(tk)]
    args = [q, k, v]
    if has_ctx:
        in_specs += [kv_spec(kctx.shape[1]), kv_spec(kctx.shape[1])]
        args += [kctx, vctx]
    return pl.pallas_call(
        functools.partial(_attn_kernel, has_ctx=has_ctx),
        out_shape=jax.ShapeDtypeStruct((b, t, ATTN_W), BF16),
        grid=(b, N_KV_HEADS, nq),
        in_specs=in_specs,
        out_specs=pl.BlockSpec((None, tq, 2 * LANES), lambda bi, g, i: (bi, i, g)),
        compiler_params=_cparams(("arbitrary", "arbitrary", "arbitrary")),
        name="attn",
    )(*args)


LOG2E = 1.4426950408889634


def _softplus(x):
    return jnp.maximum(x, 0.0) + jnp.log(1.0 + jnp.exp(-jnp.abs(x)))


def _ssd_kernel(*refs, nc, reverse, has_h0, want_hfin):
    refs = list(refs)
    conv = not reverse
    if conv:
        xbc_ref, prev_ref, next_ref, cw_ref, cb_ref, dsk_ref = refs[:6]
        refs = refs[6:]
    else:
        xc_ref = refs.pop(0)
    dt_ref, an_ref, dtb_ref, tri_ref, sel_ref = refs[:5]
    refs = refs[5:]
    h0_ref = refs.pop(0) if has_h0 else None
    hprev_ref = refs.pop(0) if (want_hfin and reverse) else None
    y_ref = refs.pop(0)
    xco_ref = refs.pop(0) if conv else None
    hfin_ref = refs.pop(0) if want_hfin else None
    h_scr = refs.pop(0)

    L = CHUNK
    c = pl.program_id(1)
    cidx = (nc - 1 - c) if reverse else c

    @pl.when(c == 0)
    def _():
        if has_h0:
            h_scr[...] = h0_ref[...]
        else:
            h_scr[...] = jnp.zeros_like(h_scr)

    row = lax.broadcasted_iota(jnp.int32, (L, LANES), 0)
    lane = lax.broadcasted_iota(jnp.int32, (L, LANES), 1)
    lo = lane < SSD_HEAD_DIM
    top = row < SSD_HEAD_DIM

    if conv:
        first = cidx == 0
        last = cidx == nc - 1

        def cols(a, w):
            xm = xbc_ref[:, a:a + w].astype(F32)
            rw = lax.broadcasted_iota(jnp.int32, (L, w), 0)
            p6 = jnp.where(first, 0.0, prev_ref[6:7, a:a + w].astype(F32))
            p7 = jnp.where(first, 0.0, prev_ref[7:8, a:a + w].astype(F32))
            n0 = jnp.where(last, 0.0, next_ref[0:1, a:a + w].astype(F32))
            r1 = jnp.where(rw == 0, p7, pltpu.roll(xm, 1, 0))
            r2 = jnp.where(rw == 0, p6, jnp.where(rw == 1, p7, pltpu.roll(xm, 2, 0)))
            rn = jnp.where(rw == L - 1, n0, pltpu.roll(xm, L - 1, 0))
            y = (r2 * cw_ref[0:1, a:a + w] + r1 * cw_ref[1:2, a:a + w] + xm * cw_ref[2:3, a:a + w]
                 + rn * cw_ref[3:4, a:a + w] + cb_ref[:, a:a + w])
            y = _silu(y).astype(BF16)
            xco_ref[:, a:a + w] = y
            return y
    else:
        def cols(a, w):
            return xc_ref[:, a:a + w]

    causal = tri_ref[...] > 0.0
    dt = _softplus(dt_ref[...] + dtb_ref[...])
    la2 = dt * (an_ref[...] * LOG2E)
    acum2 = jnp.dot(tri_ref[...], la2, preferred_element_type=F32, precision=lax.Precision.HIGHEST)
    dt_t = dt.T
    acum2_t = acum2.T
    tot2_t = jnp.sum(la2.T, axis=1, keepdims=True)
    lg_dt_t = jnp.log2(dt_t)
    r_t = lg_dt_t - acum2_t
    w_t = jnp.exp2(lg_dt_t + tot2_t - acum2_t)
    e_acum_x = jnp.dot(jnp.exp2(acum2).astype(BF16), sel_ref[...], preferred_element_type=F32)
    e_tot_t = jnp.exp2(tot2_t)

    for g in range(SSD_GROUPS):
        bgb = cols(D_INNER + g * D_STATE, D_STATE)
        cgb = cols(D_INNER + SSD_GROUPS * D_STATE + g * D_STATE, D_STATE)
        cbm = _bdot_nt(cgb, bgb)
        h_grp = h_scr[4 * g:4 * g + 4]
        yo_grp = _bdot_nt(cgb, h_grp.reshape(4 * LANES, D_STATE))
        xws, cds = [], []
        for pr in range(4):
            hp = g * 4 + pr
            ha, hb = 2 * hp, 2 * hp + 1
            xpb = cols(hp * LANES, LANES)
            zero = jnp.zeros_like(xpb)
            xs = jnp.concatenate([jnp.where(lo, xpb, zero), jnp.where(lo, zero, xpb)], axis=0)
            ms = []
            for hh in (ha, hb):
                e = jnp.exp2(acum2[:, hh:hh + 1] + r_t[hh:hh + 1, :])
                ms.append((cbm * jnp.where(causal, e, 0.0)).astype(BF16))
            y = jnp.dot(jnp.concatenate(ms, axis=1), xs, preferred_element_type=F32)
            y = y + yo_grp[:, pr * LANES:(pr + 1) * LANES] * e_acum_x[:, hp * LANES:(hp + 1) * LANES]
            if conv:
                y = y + dsk_ref[:, hp * LANES:(hp + 1) * LANES] * xpb.astype(F32)
            y_ref[:, hp * LANES:(hp + 1) * LANES] = y.astype(y_ref.dtype)
            wsel = jnp.where(top, w_t[ha:ha + 1, :], w_t[hb:hb + 1, :])
            xws.append((xpb.astype(F32).T * wsel).astype(BF16))
            cds.append(jnp.where(top, e_tot_t[ha:ha + 1, :], e_tot_t[hb:hb + 1, :]))
        st = jnp.dot(jnp.concatenate(xws, axis=0), bgb, preferred_element_type=F32)
        for pr in range(4):
            h_scr[g * 4 + pr] = h_grp[pr] * cds[pr] + st[pr * LANES:(pr + 1) * LANES]

    if want_hfin:
        @pl.when(c == nc - 1)
        def _():
            if reverse:
                hfin_ref[0] = hprev_ref[...]
                hfin_ref[1] = h_scr[...]
            else:
                hfin_ref[...] = h_scr[...]


def _ssd_sweep(xin, dt, wts, h0, hprev, *, reverse, want_hfin):
    b, t, _ = xin.shape
    assert t % CHUNK == 0, t
    nc = t // CHUNK
    has_h0 = h0 is not None
    rb = CHUNK // 8
    nrb = t // 8
    d = 1 if reverse else 0
    cmap = (lambda c: nc - 1 - c) if reverse else (lambda c: c)
    hshape = (SSD_HEADS // 2, 2 * SSD_HEAD_DIM, D_STATE)

    chunk_spec = pl.BlockSpec((None, CHUNK, CONV_CH), lambda bi, c: (bi, cmap(c), 0))
    if reverse:
        in_specs = [chunk_spec]
        args = [xin]
    else:
        in_specs = [
            chunk_spec,
            pl.BlockSpec((None, 8, CONV_CH), lambda bi, c: (bi, jnp.maximum(c * rb - 1, 0), 0)),
            pl.BlockSpec((None, 8, CONV_CH), lambda bi, c: (bi, jnp.minimum((c + 1) * rb, nrb - 1), 0)),
            pl.BlockSpec((D_CONV, CONV_CH), lambda bi, c: (0, 0)),
            pl.BlockSpec((1, CONV_CH), lambda bi, c: (0, 0)),
            pl.BlockSpec((1, D_INNER), lambda bi, c: (0, 0)),
        ]
        args = [xin, xin, xin, wts["conv_w"], wts["conv_b"], wts["dskip"]]
    in_specs += [
        pl.BlockSpec((None, CHUNK, LANES), lambda bi, c: (bi, cmap(c), d)),
        pl.BlockSpec((None, 1, LANES), lambda bi, c: (d, 0, 0)),
        pl.BlockSpec((None, 1, LANES), lambda bi, c: (d, 0, 0)),
        pl.BlockSpec((None, CHUNK, CHUNK), lambda bi, c: (d, 0, 0)),
        pl.BlockSpec((LANES, D_INNER), lambda bi, c: (0, 0)),
    ]
    args += [dt, wts["a_neg"], wts["dt_bias"], wts["tri"], wts["head_sel"]]
    if has_h0:
        in_specs.append(pl.BlockSpec((None, None) + hshape, lambda bi, c: (bi, d, 0, 0, 0)))
        args.append(h0)
    if want_hfin and reverse:
        in_specs.append(pl.BlockSpec((None,) + hshape, lambda bi, c: (bi, 0, 0, 0)))
        args.append(hprev)
    out_shape = [jax.ShapeDtypeStruct((b, t, D_INNER), BF16)]
    out_specs = [pl.BlockSpec((None, CHUNK, D_INNER), lambda bi, c: (bi, cmap(c), 0))]
    if not reverse:
        out_shape.append(jax.ShapeDtypeStruct((b, t, CONV_CH), BF16))
        out_specs.append(pl.BlockSpec((None, CHUNK, CONV_CH), lambda bi, c: (bi, c, 0)))
    if want_hfin and reverse:
        out_shape.append(jax.ShapeDtypeStruct((b, 2) + hshape, F32))
        out_specs.append(pl.BlockSpec((None, 2) + hshape, lambda bi, c: (bi, 0, 0, 0, 0)))
    elif want_hfin:
        out_shape.append(jax.ShapeDtypeStruct((b,) + hshape, F32))
        out_specs.append(pl.BlockSpec((None,) + hshape, lambda bi, c: (bi, 0, 0, 0)))
    return pl.pallas_call(
        functools.partial(_ssd_kernel, nc=nc, reverse=reverse, has_h0=has_h0, want_hfin=want_hfin),
        out_shape=tuple(out_shape),
        grid=(b, nc),
        in_specs=in_specs,
        out_specs=tuple(out_specs),
        scratch_shapes=[pltpu.VMEM(hshape, F32)],
        compiler_params=_cparams(("arbitrary", "arbitrary")),
        name="ssd_bwd" if reverse else "ssd_fwd",
    )(*args)


def _ssd_call(xbc, dt, wts, h0, *, want_hfin):
    res = _ssd_sweep(xbc, dt, wts, h0, None, reverse=False, want_hfin=want_hfin)
    y_f, xc = res[0], res[1]
    hf = res[2] if want_hfin else None
    res = _ssd_sweep(xc, dt, wts, h0, hf, reverse=True, want_hfin=want_hfin)
    return y_f, res[0], (res[1] if want_hfin else None)


def _route(logits_t, bias_col):
    e, n = logits_t.shape
    per = e // N_EXPERT_GROUPS
    scores = jax.nn.sigmoid(logits_t)
    sel = scores + bias_col
    neg = jnp.float32(-jnp.inf)
    gs = []
    for g in range(N_EXPERT_GROUPS):
        blk = sel[g * per:(g + 1) * per, :]
        m1 = jnp.max(blk, axis=0, keepdims=True)
        is_m1 = blk == m1
        cnt = jnp.sum(jnp.where(is_m1, 1.0, 0.0), axis=0, keepdims=True)
        m2 = jnp.max(jnp.where(is_m1, neg, blk), axis=0, keepdims=True)
        gs.append(m1 + jnp.where(cnt >= 2.0, m1, m2))
    keep = []
    for g in range(N_EXPERT_GROUPS):
        rank = jnp.zeros_like(gs[g])
        for j in range(N_EXPERT_GROUPS):
            if j == g:
                continue
            beats = (gs[j] > gs[g]) if j > g else (gs[j] >= gs[g])
            rank = rank + jnp.where(beats, 1.0, 0.0)
        keep.append(rank < float(TOPK_GROUPS))
    selm = jnp.concatenate(
        [jnp.where(keep[g], sel[g * per:(g + 1) * per, :], neg) for g in range(N_EXPERT_GROUPS)], axis=0)
    eidx = lax.broadcasted_iota(jnp.int32, (e, n), 0).astype(F32)
    cur = selm
    picks = []
    for _ in range(TOP_K):
        m = jnp.max(cur, axis=0, keepdims=True)
        idx = jnp.min(jnp.where(cur == m, eidx, float(e)), axis=0, keepdims=True)
        hit = eidx == idx
        picks.append((idx, hit))
        cur = jnp.where(hit, neg, cur)
    return scores, picks


def _pack_bf16_pairs(h):
    c = h.shape[1] // 2
    lo = pltpu.bitcast(h[:, :c].astype(BF16).astype(F32), jnp.uint32)
    hi = pltpu.bitcast(h[:, c:].astype(BF16).astype(F32), jnp.uint32)
    return (lo >> 16) | (hi & jnp.uint32(0xFFFF0000))


def _unpack_bf16_pairs(w):
    lo = pltpu.bitcast(w << 16, F32)
    hi = pltpu.bitcast(w & jnp.uint32(0xFFFF0000), F32)
    return lo, hi


def _rows8(rows):
    n = rows[0].shape[1]
    ridx = lax.broadcasted_iota(jnp.int32, (TOP_K, n), 0)
    out = jnp.zeros((TOP_K, n), rows[0].dtype)
    for k, r in enumerate(rows):
        out = jnp.where(ridx == k, r, out)
    return out


MERGE_SUB = 512


def _merge_kernel(x_ref, attn_ref, yf_ref, yb_ref, z_ref, gates_ref, mod_ref, wa_ref, ws_ref, wo_ref,
                  sg_ref, n2_ref, wr_ref, rb_ref, wsg_ref, wsu_ref, wsd_ref,
                  xb_ref, ha_ref, hb_ref, eid_ref, pos_ref, wk_ref, cnt_ref):
    tm = x_ref.shape[0]
    sub = MERGE_SUB

    @pl.when(pl.program_id(0) == 0)
    def _():
        cnt_ref[...] = jnp.zeros_like(cnt_ref)

    r_i = lax.broadcasted_iota(jnp.int32, (sub, sub), 0)
    c_i = lax.broadcasted_iota(jnp.int32, (sub, sub), 1)
    before = jnp.where(r_i < c_i, 1.0, 0.0).astype(BF16)
    cnt = cnt_ref[:, 0:1]

    for r0 in range(0, tm, sub):
        rs = slice(r0, r0 + sub)
        x = x_ref[rs, :]
        yy = yf_ref[rs, :].astype(F32) + yb_ref[rs, :].astype(F32)
        u = yy * _silu(z_ref[rs, :]).astype(F32)
        un = u * lax.rsqrt(jnp.mean(u * u, axis=-1, keepdims=True) + EPS) * sg_ref[...]
        ssd_o = _bdot(un, ws_ref[...])
        attn_o = jnp.dot(attn_ref[rs, :], wa_ref[...], preferred_element_type=F32)
        ga = jax.nn.sigmoid(gates_ref[rs, 0:D_MODEL]).astype(F32)
        gs = jax.nn.sigmoid(gates_ref[rs, D_MODEL:2 * D_MODEL]).astype(F32)
        mix = _bdot(ga * attn_o + gs * ssd_o, wo_ref[...])
        x1 = x + _mod(mod_ref, MOD_GATE1) * mix
        h2 = x1 * lax.rsqrt(jnp.mean(x1 * x1, axis=-1, keepdims=True) + EPS) * n2_ref[...]
        h2 = h2 * (1.0 + _mod(mod_ref, MOD_SCALE2)) + _mod(mod_ref, MOD_SHIFT2)
        h2b = h2.astype(BF16)
        ha_ref[rs, :] = _pack_bf16_pairs(h2[:, :D_MODEL // 2])
        hb_ref[rs, :] = _pack_bf16_pairs(h2[:, D_MODEL // 2:])

        logits_t = _bdot_nt(wr_ref[...], h2b)
        scores, picks = _route(logits_t, rb_ref[...])
        chosen = jnp.zeros_like(scores)
        for _, hit in picks:
            chosen = chosen + jnp.where(hit, 1.0, 0.0)
        pos = cnt + jnp.dot(chosen.astype(BF16), before, preferred_element_type=F32)
        cnt = cnt + jnp.sum(chosen, axis=1, keepdims=True)
        poss = [jnp.sum(jnp.where(hit, pos, 0.0), axis=0, keepdims=True) for _, hit in picks]
        wks = [jnp.sum(jnp.where(hit, scores, 0.0), axis=0, keepdims=True) for _, hit in picks]
        wsum = wks[0]
        for w in wks[1:]:
            wsum = wsum + w
        eid_ref[:, rs] = _rows8([idx for idx, _ in picks]).astype(jnp.int32)
        pos_ref[:, rs] = _rows8(poss).astype(jnp.int32)
        wk8 = _rows8(wks) / wsum * ROUTED_SCALE
        wk_ref[rs, :] = jnp.concatenate([wk8, jnp.zeros((LANES - TOP_K, sub), F32)], axis=0).T

        hid = _silu(jnp.dot(h2b, wsg_ref[...], preferred_element_type=F32)) * \
            jnp.dot(h2b, wsu_ref[...], preferred_element_type=F32)
        xb_ref[rs, :] = x1 + _mod(mod_ref, MOD_GATE2) * _bdot(hid, wsd_ref[...])

    cnt_ref[...] = jnp.broadcast_to(cnt, cnt_ref.shape)


def _merge_call(x, attn, y_f, y_b, z, gates, mod_rows, wa, ws, wo, sg, n2, wr_t, rb, wsg, wsu, wsd, *, tm):
    b, t, _ = x.shape
    n = b * t
    assert n % tm == 0 and (t % tm == 0 or mod_rows.shape[0] == 1), (n, t, tm)
    flat = lambda a: a.reshape(n, a.shape[-1])
    tok = lambda width: pl.BlockSpec((tm, width), lambda i: (i, 0))
    const2 = lambda shape: pl.BlockSpec(shape, lambda i: (0, 0), pipeline_mode=pl.Buffered(1))
    k8 = pl.BlockSpec((TOP_K, tm), lambda i: (0, i))
    half = D_MODEL // 4
    return pl.pallas_call(
        _merge_kernel,
        out_shape=(jax.ShapeDtypeStruct((n, D_MODEL), F32),
                   jax.ShapeDtypeStruct((n, half), jnp.uint32),
                   jax.ShapeDtypeStruct((n, half), jnp.uint32),
                   jax.ShapeDtypeStruct((TOP_K, n), jnp.int32),
                   jax.ShapeDtypeStruct((TOP_K, n), jnp.int32),
                   jax.ShapeDtypeStruct((n, LANES), F32),
                   jax.ShapeDtypeStruct((N_EXPERTS, LANES), F32)),
        grid=(n // tm,),
        in_specs=[tok(D_MODEL), tok(ATTN_W), tok(D_INNER), tok(D_INNER), tok(D_INNER), tok(2 * D_MODEL),
                  pl.BlockSpec((None, 1, 6 * D_MODEL), lambda i: (jnp.minimum((i * tm) // t, mod_rows.shape[0] - 1), 0, 0)),
                  const2((ATTN_W, D_MODEL)), const2((D_INNER, D_MODEL)), const2((D_MODEL, D_MODEL)),
                  const2((1, D_INNER)), const2((1, D_MODEL)),
                  const2((N_EXPERTS, D_MODEL)), const2((N_EXPERTS, 1)),
                  const2((D_MODEL, D_SHARED)), const2((D_MODEL, D_SHARED)), const2((D_SHARED, D_MODEL))],
        out_specs=(tok(D_MODEL), tok(half), tok(half), k8, k8, tok(LANES),
                   pl.BlockSpec((N_EXPERTS, LANES), lambda i: (0, 0))),
        compiler_params=_cparams(("arbitrary",)),
        name="merge",
    )(flat(x), flat(attn), flat(y_f), flat(y_b), flat(z), flat(gates), mod_rows, wa, ws, wo, sg, n2, wr_t, rb,
      wsg, wsu, wsd)


ROW_TILE = 512
SC_WINDOW = 128


def _slots_kernel(start_ref, eid_ref, pos_ref, slot_ref):
    eid = eid_ref[...]
    slot = pos_ref[...]
    for e in range(N_EXPERTS):
        slot = slot + jnp.where(eid == e, start_ref[e], 0)
    slot_ref[...] = slot


def _slots_call(start, eid, pos):
    n = eid.shape[1]
    bn = 2048 if n % 2048 == 0 else n
    spec = pl.BlockSpec((TOP_K, bn), lambda i, s: (0, i))
    return pl.pallas_call(
        _slots_kernel,
        out_shape=jax.ShapeDtypeStruct((TOP_K, n), jnp.int32),
        grid_spec=pltpu.PrefetchScalarGridSpec(num_scalar_prefetch=1, grid=(n // bn,),
                                               in_specs=[spec, spec], out_specs=spec),
        compiler_params=_cparams(("arbitrary",)),
        name="slots",
    )(start, eid, pos)


def _sc_dispatch(x, slots, p):
    n, d = x.shape
    assert n % SC_WINDOW == 0, n
    mesh = plsc.VectorSubcoreMesh(core_axis_name="core", subcore_axis_name="subcore")

    @functools.partial(pl.kernel, out_type=jax.ShapeDtypeStruct((p, d), x.dtype), mesh=mesh)
    def k(x_hbm, s_hbm, o_hbm):
        def body(x_vmem, s_vmem):
            for kk in range(TOP_K):
                pltpu.sync_copy(x_vmem, o_hbm.at[s_vmem.at[kk]])

        pltpu.emit_pipeline(
            body,
            grid=(n // SC_WINDOW,),
            in_specs=[pl.BlockSpec((SC_WINDOW, d), index_map=lambda i: (i, 0)),
                      pl.BlockSpec((TOP_K, SC_WINDOW), index_map=lambda i: (0, i))],
            out_specs=[],
            core_axis_name=("core", "subcore"),
            dimension_semantics=(pltpu.PARALLEL,),
        )(x_hbm, s_hbm)

    return k(x, slots)


def _sc_combine(y, slots):
    kk, n = slots.shape
    assert n % SC_WINDOW == 0, n
    d = y.shape[1]
    mesh = plsc.VectorSubcoreMesh(core_axis_name="core", subcore_axis_name="subcore")

    @functools.partial(pl.kernel, out_type=jax.ShapeDtypeStruct((kk * n, d), y.dtype), mesh=mesh)
    def k(y_hbm, s_hbm, o_hbm):
        def body(s_vmem, o_vmem):
            pltpu.sync_copy(y_hbm.at[s_vmem.at[0]], o_vmem)

        pltpu.emit_pipeline(
            body,
            grid=(kk * n // SC_WINDOW,),
            in_specs=[pl.BlockSpec((1, SC_WINDOW), index_map=lambda i: (0, i))],
            out_specs=[pl.BlockSpec((SC_WINDOW, d), index_map=lambda i: (i, 0))],
            core_axis_name=("core", "subcore"),
            dimension_semantics=(pltpu.PARALLEL,),
        )(s_hbm, o_hbm)

    return k(y, slots.reshape(1, kk * n)).reshape(kk, n, d)


FFN_IN_BUFS = 4
FFN_OUT_BUFS = 3
FFN_ROW_DMA_PRIORITY = 1


def _ffn_kernel(st_ref, nt_ref, wg_ref, wu_ref, wd_ref, xa_hbm, xb_hbm, ya_hbm, yb_hbm,
                wg_s, wu_s, wd_s, xa_buf, xb_buf, ya_buf, yb_buf, in_sem, out_sem, done_ref):
    e = pl.program_id(0)
    ne = pl.num_programs(0)
    nxt = jnp.minimum(e + 1, ne - 1)
    n = nt_ref[e]
    n_next = jnp.where(e + 1 < ne, nt_ref[nxt], 0)
    t = ROW_TILE
    ahead = FFN_IN_BUFS - 1

    def fetch(row, slot):
        r = pl.multiple_of(row, t)
        return (pltpu.make_async_copy(xa_hbm.at[pl.ds(r, t)], xa_buf.at[slot], in_sem.at[0, slot]),
                pltpu.make_async_copy(xb_hbm.at[pl.ds(r, t)], xb_buf.at[slot], in_sem.at[1, slot]))

    def put(row, slot):
        r = pl.multiple_of(row, t)
        return (pltpu.make_async_copy(ya_buf.at[slot], ya_hbm.at[pl.ds(r, t)], out_sem.at[0, slot]),
                pltpu.make_async_copy(yb_buf.at[slot], yb_hbm.at[pl.ds(r, t)], out_sem.at[1, slot]))

    def wait_put(slot):
        for c in put(0, slot):
            c.wait()

    def start_fetch(row, g):
        for c in fetch(row, lax.rem(g, FFN_IN_BUFS)):
            c.start(priority=FFN_ROW_DMA_PRIORITY)

    @pl.when(e == 0)
    def _():
        done_ref[0] = 0
        for k in range(ahead):
            @pl.when(k < n)
            def _():
                start_fetch(st_ref[0] + k * t, k)

    done = done_ref[0]

    def prefetch(i):
        j = i + ahead - n

        @pl.when(j < 0)
        def _():
            start_fetch(st_ref[e] + (i + ahead) * t, done + i + ahead)

        @pl.when(jnp.logical_and(j >= 0, j < n_next))
        def _():
            start_fetch(st_ref[nxt] + j * t, done + i + ahead)

    @pl.when(n > 0)
    def _():
        wg_s[...] = wg_ref[...].astype(BF16)
        wu_s[...] = wu_ref[...].astype(BF16)
        wd_s[...] = wd_ref[...].astype(BF16)
        base = st_ref[e]

        def body(i, carry):
            g = done + i
            si = lax.rem(g, FFN_IN_BUFS)
            so = lax.rem(g, FFN_OUT_BUFS)
            row = base + i * t
            for c in fetch(row, si):
                c.wait()
            prefetch(i)

            @pl.when(g >= FFN_OUT_BUFS)
            def _():
                wait_put(so)

            parts = _unpack_bf16_pairs(xa_buf[si]) + _unpack_bf16_pairs(xb_buf[si])
            x = jnp.concatenate(parts, axis=1).astype(BF16)
            hid = _silu(jnp.dot(x, wg_s[...], preferred_element_type=F32)) * \
                jnp.dot(x, wu_s[...], preferred_element_type=F32)
            y = _bdot(hid, wd_s[...])
            ya_buf[so] = _pack_bf16_pairs(y[:, :D_MODEL // 2])
            yb_buf[so] = _pack_bf16_pairs(y[:, D_MODEL // 2:])
            for c in put(row, so):
                c.start()
            return carry

        lax.fori_loop(0, n, body, 0)
        done_ref[0] = done + n

    for k in range(ahead):
        @pl.when(jnp.logical_and(k < ahead - n, k < n_next))
        def _():
            start_fetch(st_ref[nxt] + k * t, done + n + k)

    @pl.when(e == ne - 1)
    def _():
        total = done_ref[0]
        for k in range(FFN_OUT_BUFS):
            @pl.when(total - 1 - k >= 0)
            def _():
                wait_put(lax.rem(total - 1 - k, FFN_OUT_BUFS))


def _ffn_call(start, tiles_e, xa, xb, wg, wu, wd):
    p, half = xa.shape
    wspec = lambda s: pl.BlockSpec((None,) + s, lambda e, st, nt: (e, 0, 0))
    hbm = pl.BlockSpec(memory_space=pl.ANY)
    ibuf = pltpu.VMEM((FFN_IN_BUFS, ROW_TILE, half), jnp.uint32)
    obuf = pltpu.VMEM((FFN_OUT_BUFS, ROW_TILE, half), jnp.uint32)
    return pl.pallas_call(
        _ffn_kernel,
        out_shape=(jax.ShapeDtypeStruct((p, half), jnp.uint32), jax.ShapeDtypeStruct((p, half), jnp.uint32)),
        grid_spec=pltpu.PrefetchScalarGridSpec(
            num_scalar_prefetch=2, grid=(N_EXPERTS,),
            in_specs=[wspec((D_MODEL, D_EXPERT)), wspec((D_MODEL, D_EXPERT)), wspec((D_EXPERT, D_MODEL)),
                      hbm, hbm],
            out_specs=(hbm, hbm),
            scratch_shapes=[pltpu.VMEM((D_MODEL, D_EXPERT), BF16), pltpu.VMEM((D_MODEL, D_EXPERT), BF16),
                            pltpu.VMEM((D_EXPERT, D_MODEL), BF16), ibuf, ibuf, obuf, obuf,
                            pltpu.SemaphoreType.DMA((2, FFN_IN_BUFS)),
                            pltpu.SemaphoreType.DMA((2, FFN_OUT_BUFS)),
                            pltpu.SMEM((1,), jnp.int32)]),
        compiler_params=_cparams(("arbitrary",)),
        name="ffn",
    )(start, tiles_e, wg, wu, wd, xa, xb)


def _final_kernel(xb_ref, ya_ref, yb_ref, wk_ref, mod_ref, o_ref):
    q = D_MODEL // 4
    accs = [jnp.zeros((xb_ref.shape[0], q), F32) for _ in range(4)]
    for k in range(TOP_K):
        w = wk_ref[:, k:k + 1]
        parts = _unpack_bf16_pairs(ya_ref[k]) + _unpack_bf16_pairs(yb_ref[k])
        accs = [a + w * p for a, p in zip(accs, parts)]
    for i, a in enumerate(accs):
        o_ref[:, i * q:(i + 1) * q] = xb_ref[:, i * q:(i + 1) * q] + _mod(mod_ref, MOD_GATE2, i * q, (i + 1) * q) * a


def _final_call(xb, ya, yb, wk, mod_rows, t, *, tm):
    n = xb.shape[0]
    assert n % tm == 0 and (t % tm == 0 or mod_rows.shape[0] == 1), (n, t, tm)
    half = ya.shape[2]
    tok = lambda width: pl.BlockSpec((tm, width), lambda i: (i, 0))
    yspec = pl.BlockSpec((TOP_K, tm, half), lambda i: (0, i, 0))
    return pl.pallas_call(
        _final_kernel,
        out_shape=jax.ShapeDtypeStruct((n, D_MODEL), F32),
        grid=(n // tm,),
        in_specs=[tok(D_MODEL), yspec, yspec, tok(LANES),
                  pl.BlockSpec((None, 1, 6 * D_MODEL), lambda i: (jnp.minimum((i * tm) // t, mod_rows.shape[0] - 1), 0, 0))],
        out_specs=tok(D_MODEL),
        compiler_params=_cparams(("arbitrary",)),
        name="final",
    )(xb, ya, yb, wk, mod_rows)


def _moe_call(ha, hb, eid, pos, wk, counts, xb, mod_rows, wg, wu, wd, t):
    n = xb.shape[0]
    max_tiles = n * TOP_K // ROW_TILE + N_EXPERTS
    p = max_tiles * ROW_TILE
    cnt = counts[:, 0].astype(jnp.int32)
    tiles_e = (cnt + ROW_TILE - 1) // ROW_TILE
    ends = jnp.cumsum(tiles_e)
    start = (ends - tiles_e) * ROW_TILE
    slots = _slots_call(start.astype(jnp.int32), eid, pos)
    xa = _sc_dispatch(ha, slots, p)
    xbb = _sc_dispatch(hb, slots, p)
    ya, yb = _ffn_call(start.astype(jnp.int32), tiles_e.astype(jnp.int32), xa, xbb, wg, wu, wd)
    ga = _sc_combine(ya, slots)
    gb = _sc_combine(yb, slots)
    return _final_call(xb, ga, gb, wk, mod_rows, t, tm=256)


def _rope_tables(t):
    n_rows = t // GRID_W
    rows = np.repeat(np.arange(n_rows), GRID_W).astype(np.float32)
    cols = np.tile(np.arange(GRID_W), n_rows).astype(np.float32)
    n_freq = HEAD_DIM // 4
    freqs = (np.float32(ROPE_THETA) ** (-np.arange(n_freq, dtype=np.float32) / np.float32(n_freq))).astype(np.float32)
    ang = np.concatenate([rows[:, None] * freqs, cols[:, None] * freqs], axis=-1)
    ang = np.repeat(ang, 2, axis=-1)
    ang = np.concatenate([ang, ang], axis=-1).astype(np.float32)
    sign = np.where(np.arange(LANES) % 2 == 0, -1.0, 1.0).astype(np.float32)
    return jnp.asarray(np.cos(ang)), jnp.asarray(np.sin(ang) * sign)


def _dup_heads(a):
    s = a.shape[:-1]
    a4 = a.reshape(s + (N_KV_HEADS, HEAD_DIM))
    return jnp.concatenate([a4, a4], axis=-1).reshape(s + (KVD_W,))


def _prep_w_in(w_in):
    pad = jnp.zeros((D_MODEL, LANES - SSD_HEADS), w_in.dtype)
    dt0 = M_END
    g0 = dt0 + 2 * SSD_HEADS
    tail = [w_in[:, g0:g0 + 2 * D_MODEL], w_in[:, dt0:dt0 + SSD_HEADS], pad,
            w_in[:, dt0 + SSD_HEADS:g0], pad]
    return w_in[:, :M_END].astype(BF16), jnp.concatenate(tail, axis=-1).astype(BF16)


def _pad_heads(a):
    return jnp.pad(a.astype(F32), ((0, 0), (0, LANES - SSD_HEADS)))[:, None, :]


def _trunk(x, mod_rows, wts, rope_tabs, ctx_k, ctx_v, h0, *, tm, tq, want_state):
    b, t, _ = x.shape
    rope = rope_tabs is not None
    if rope:
        cos, sin = rope_tabs
    else:
        cos = sin = jnp.zeros((t, LANES), F32)
    res = _inproj_call(x, mod_rows, wts["g1"], *wts["w_in"], wts["qg"], wts["kg"], cos, sin,
                       rope=rope, emit_kv=want_state, tm=tm)
    q, k, v, gates, z, xbc, dt = res[:7]
    kv_raw = res[7:]
    attn = _attn_call(q, k, v, ctx_k, ctx_v, tq=tq)
    y_f, y_b, hfin = _ssd_call(xbc, dt, wts, h0, want_hfin=want_state)
    xb, ha, hb, eid, pos, wk, counts = _merge_call(
        x, attn, y_f, y_b, z, gates, mod_rows, wts["wa"], wts["ws"], wts["wo"], wts["sg"],
        wts["n2"], wts["wr_t"], wts["rb"], wts["wsg"], wts["wsu"], wts["wsd"], tm=MERGE_SUB)
    out = _moe_call(ha, hb, eid, pos, wk, counts, xb, mod_rows, wts["weg"], wts["weu"], wts["wed"], t)
    return out.reshape(b, t, D_MODEL), kv_raw, hfin


def kernel(x_prompt, x_sample, cache_k, cache_v, state_ssm, c, c_ctx, w_mod, b_mod, norm1_g, norm2_g, w_in,
           q_norm_g, k_norm_g, conv_w, conv_b, a_log, dt_bias, d_skip, ssd_norm_g, w_attn_proj, w_ssd_proj,
           w_out, w_router, router_bias, w_exp_gate, w_exp_up, w_exp_down, w_sh_gate, w_sh_up, w_sh_down):
    depth = w_mod.shape[0]
    assert depth == 1, "single trunk layer"
    bp, tp, _ = x_prompt.shape
    bs, ts, _ = x_sample.shape
    assert bs + 1 <= 8, "modulation rows are computed 8 at a time"
    l = 0
    cvec = jnp.concatenate([c_ctx[None, :], c, jnp.zeros((8 - 1 - bs, D_MODEL), F32)], axis=0)
    mod = _mod_call(cvec, w_mod.reshape(D_MODEL, 6 * D_MODEL), b_mod.reshape(1, 6 * D_MODEL))
    mod_prompt = mod[0:1][:, None, :]
    mod_sample = mod[1:1 + bs][:, None, :]

    lower = np.tril(np.ones((CHUNK, CHUNK), np.float32))
    head_sel = (np.arange(LANES)[:, None] == np.arange(D_INNER)[None, :] // SSD_HEAD_DIM).astype(np.float32)
    wts = dict(
        g1=norm1_g[l][None, :], n2=norm2_g[l][None, :],
        w_in=_prep_w_in(w_in.reshape(D_MODEL, w_in.shape[-1])),
        qg=jnp.tile(q_norm_g[l], 2)[None, :], kg=jnp.tile(k_norm_g[l], 2)[None, :],
        conv_w=conv_w[l], conv_b=conv_b[l][None, :],
        a_neg=_pad_heads(-jnp.exp(a_log[l].astype(F32))), dt_bias=_pad_heads(dt_bias[l]),
        dskip=jnp.repeat(d_skip[l].astype(F32), SSD_HEAD_DIM)[None, :],
        tri=jnp.asarray(np.stack([lower, lower.T])),
        head_sel=jnp.asarray(head_sel, BF16),
        sg=ssd_norm_g[l][None, :],
        wa=w_attn_proj[l].astype(BF16), ws=w_ssd_proj[l].astype(BF16), wo=w_out[l].astype(BF16),
        wr_t=w_router[l].T.astype(BF16), rb=router_bias[l].astype(F32)[:, None],
        wsg=w_sh_gate[l].astype(BF16), wsu=w_sh_up[l].astype(BF16), wsd=w_sh_down[l].astype(BF16),
        weg=w_exp_gate.reshape(w_exp_gate.shape[1:]), weu=w_exp_up.reshape(w_exp_up.shape[1:]),
        wed=w_exp_down.reshape(w_exp_down.shape[1:]),
    )

    y_prompt, (k_p, v_p), hfin = _trunk(x_prompt, mod_prompt, wts, None, None, None, None,
                                        tm=256, tq=256, want_state=True)
    new_k = k_p.reshape(bp, 1, tp, N_KV_HEADS, HEAD_DIM)
    new_v = v_p.reshape(bp, 1, tp, N_KV_HEADS, HEAD_DIM)
    new_state = hfin.reshape(bp, 1, 2, SSD_HEADS, SSD_HEAD_DIM, D_STATE)

    past = cache_k.shape[2]
    ctx_k = _dup_heads(cache_k[:, l].reshape(bs, past, KV_W)).astype(BF16)
    ctx_v = _dup_heads(cache_v[:, l].reshape(bs, past, KV_W)).astype(BF16)
    h0 = state_ssm[:, l].reshape(bs, 2, SSD_HEADS // 2, 2 * SSD_HEAD_DIM, D_STATE)
    y_sample, _, _ = _trunk(x_sample, mod_sample, wts, _rope_tables(ts), ctx_k, ctx_v, h0,
                               tm=512, tq=512, want_state=False)
    return (y_prompt, y_sample, new_k, new_v, new_state)
```
